```python
import jax, jax.numpy as jnp
from jax import lax
import numpy as np

D_MODEL = 1024
BATCH = 8
SEQ = 8192
DEPTH = 1

CONV_DIM = 512
CONV_KERNEL = 31
HGRN_DIM = 1024
HGRN_HEADS = 8
HGRN_HEAD_DIM = HGRN_DIM // HGRN_HEADS
HGRN_CHUNK = 64
N_BRANCHES = 2
D_FF = 2816
FFN_KERNEL = 3
LN_EPS = 1e-5
RMS_EPS = 1e-6
ALPHA = (2.0 * DEPTH) ** 0.25
BETA = (8.0 * DEPTH) ** -0.25

IN_SPLITS = [CONV_DIM, CONV_DIM, HGRN_DIM, HGRN_DIM, HGRN_DIM, HGRN_DIM, N_BRANCHES * D_MODEL]
IN_COLS = sum(IN_SPLITS)
IN_OFFSETS = list(np.cumsum(IN_SPLITS)[:-1])

kernel_name = "hybrid_conformer_conv_hgrn2_gated_merge_convffn"


def layer_norm(x, g, b):
    xf = x.astype(jnp.float32)
    mu = jnp.mean(xf, axis=-1, keepdims=True)
    var = jnp.mean(jnp.square(xf - mu), axis=-1, keepdims=True)
    y = (xf - mu) * lax.rsqrt(var + LN_EPS) * g.astype(jnp.float32) + b.astype(jnp.float32)
    return y.astype(x.dtype)


def causal_dwconv(x, w, b):
    k_w = w.shape[0]
    c = x.shape[-1]
    y = lax.conv_general_dilated(
        x, w[:, None, :].astype(x.dtype), window_strides=(1,), padding=[(k_w - 1, 0)],
        dimension_numbers=("NWC", "WIO", "NWC"), feature_group_count=c)
    return y + b.astype(x.dtype)


def hgrn2_chunked(q, k, v, logf):
    bsz, seq, nh, dk = q.shape
    dv = v.shape[-1]
    nc = seq // HGRN_CHUNK

    def to_chunks(t):
        return t.reshape(bsz, nc, HGRN_CHUNK, nh, t.shape[-1]).transpose(1, 0, 3, 2, 4)

    qc, kc, vc, lfc = to_chunks(q), to_chunks(k), to_chunks(v), to_chunks(logf)
    bc = jnp.cumsum(lfc, axis=3)
    mask = jnp.tril(jnp.ones((HGRN_CHUNK, HGRN_CHUNK), dtype=bool))[:, :, None]

    def step(state, inp):
        q_c, k_c, v_c, b_c = inp
        diff = b_c[:, :, :, None, :] - b_c[:, :, None, :, :]
        decay = jnp.exp(jnp.where(mask, diff, -jnp.inf))
        scores = jnp.einsum("bhtk,bhsk,bhtsk->bhts", q_c, k_c, decay)
        o_intra = jnp.einsum("bhts,bhsv->bhtv", scores, v_c)
        o_inter = jnp.einsum("bhtk,bhkv->bhtv", q_c * jnp.exp(b_c), state)
        b_last = b_c[:, :, -1, :]
        k_tail = k_c * jnp.exp(b_last[:, :, None, :] - b_c)
        new_state = jnp.exp(b_last)[..., None] * state + jnp.einsum("bhsk,bhsv->bhkv", k_tail, v_c)
        return new_state, o_intra + o_inter

    s0 = jnp.zeros((bsz, nh, dk, dv), jnp.float32)
    _, oc = lax.scan(step, s0, (qc, kc, vc, bc))
    return oc.transpose(1, 0, 3, 2, 4).reshape(bsz, seq, nh, dv)


def _fwd_setup_inputs(seed: int = 0) -> dict:
    key = jax.random.key(seed)
    ks = jax.random.split(key, 20)

    def nrm(k, shape, scale):
        return jax.random.normal(k, shape, jnp.float32) * scale

    col_scale = jnp.concatenate([
        jnp.ones((2 * CONV_DIM + 2 * HGRN_DIM,), jnp.float32),
        jnp.full((HGRN_DIM,), BETA, jnp.float32),
        jnp.ones((HGRN_DIM + N_BRANCHES * D_MODEL,), jnp.float32)])
    ffn_scale = jnp.concatenate([jnp.full((D_FF,), BETA, jnp.float32), jnp.ones((D_FF,), jnp.float32)])
    return {
        "x": nrm(ks[0], (BATCH, SEQ, D_MODEL), 1.0),
        "w_in": nrm(ks[1], (DEPTH, D_MODEL, IN_COLS), D_MODEL ** -0.5) * col_scale,
        "w_conv_dw": nrm(ks[2], (DEPTH, CONV_KERNEL, CONV_DIM), CONV_KERNEL ** -0.5),
        "b_conv_dw": nrm(ks[3], (DEPTH, CONV_DIM), 0.02),
        "conv_ln_g": 1.0 + nrm(ks[4], (DEPTH, CONV_DIM), 0.02),
        "conv_ln_b": nrm(ks[5], (DEPTH, CONV_DIM), 0.02),
        "w_conv_out": nrm(ks[6], (DEPTH, CONV_DIM, D_MODEL), BETA * CONV_DIM ** -0.5),
        "hgrn_lb_logits": nrm(ks[7], (DEPTH + 1, HGRN_DIM), 0.5),
        "hgrn_norm_g": 1.0 + nrm(ks[8], (DEPTH, HGRN_DIM), 0.02),
        "w_hgrn_out": nrm(ks[9], (DEPTH, HGRN_DIM, D_MODEL), BETA * HGRN_DIM ** -0.5),
        "w_out": nrm(ks[10], (DEPTH, D_MODEL, D_MODEL), BETA * D_MODEL ** -0.5),
        "ln1_g": 1.0 + nrm(ks[11], (DEPTH, D_MODEL), 0.02),
        "ln1_b": nrm(ks[12], (DEPTH, D_MODEL), 0.02),
        "w_ffn_in": nrm(ks[13], (DEPTH, D_MODEL, 2 * D_FF), D_MODEL ** -0.5) * ffn_scale,
        "w_ffn_dw": nrm(ks[14], (DEPTH, FFN_KERNEL, D_FF), FFN_KERNEL ** -0.5),
        "b_ffn_dw": nrm(ks[15], (DEPTH, D_FF), 0.02),
        "w_ffn_out": nrm(ks[16], (DEPTH, D_FF, D_MODEL), BETA * D_FF ** -0.5),
        "ln2_g": 1.0 + nrm(ks[17], (DEPTH, D_MODEL), 0.02),
        "ln2_b": nrm(ks[18], (DEPTH, D_MODEL), 0.02),
    }


def _fwd_reference(x, w_in, w_conv_dw, b_conv_dw, conv_ln_g, conv_ln_b, w_conv_out,
              hgrn_lb_logits, hgrn_norm_g, w_hgrn_out, w_out, ln1_g, ln1_b,
              w_ffn_in, w_ffn_dw, b_ffn_dw, w_ffn_out, ln2_g, ln2_b):
    bsz, seq, _ = x.shape
    lb_all = jnp.cumsum(jax.nn.softmax(hgrn_lb_logits.astype(jnp.float32), axis=0), axis=0)
    for l in range(DEPTH):
        h = x
        proj = h @ w_in[l]
        c_val, c_gate, q_z, f_z, i_v, g_z, m_z = jnp.split(proj, IN_OFFSETS, axis=-1)

        c = c_val * jax.nn.sigmoid(c_gate)
        c = causal_dwconv(c, w_conv_dw[l], b_conv_dw[l])
        c = jax.nn.silu(layer_norm(c, conv_ln_g[l], conv_ln_b[l]))
        y_conv = c @ w_conv_out[l]

        lb = lb_all[l]
        zf = f_z.astype(jnp.float32)
        logf = jnp.log(lb + (1.0 - lb) * jax.nn.sigmoid(zf))
        k_in = (1.0 - lb) * jax.nn.sigmoid(-zf)
        qf = jax.nn.silu(q_z.astype(jnp.float32))
        heads = lambda t: t.reshape(bsz, seq, HGRN_HEADS, HGRN_HEAD_DIM)
        o = hgrn2_chunked(heads(qf), heads(k_in), heads(i_v.astype(jnp.float32)), heads(logf))
        o = o * lax.rsqrt(jnp.mean(jnp.square(o), axis=-1, keepdims=True) + RMS_EPS)
        o = o.reshape(bsz, seq, HGRN_DIM) * hgrn_norm_g[l].astype(jnp.float32)
        o = o.astype(x.dtype) * jax.nn.silu(g_z)
        y_hgrn = o @ w_hgrn_out[l]

        gates = jax.nn.sigmoid(m_z).reshape(bsz, seq, N_BRANCHES, D_MODEL)
        mixed = gates[:, :, 0, :] * y_conv + gates[:, :, 1, :] * y_hgrn
        mix = mixed @ w_out[l]
        x = layer_norm(ALPHA * x + mix, ln1_g[l], ln1_b[l])

        z = x @ w_ffn_in[l]
        u, gv = jnp.split(z, [D_FF], axis=-1)
        u = causal_dwconv(u, w_ffn_dw[l], b_ffn_dw[l])
        y_ffn = (jax.nn.gelu(u) * gv) @ w_ffn_out[l]
        x = layer_norm(ALPHA * x + y_ffn, ln2_g[l], ln2_b[l])
    return x


import jax as _jax
import jax.numpy as _jnp

TWIN_FORMAT = 'train_step'
FWD_PARAMS = ['x', 'w_in', 'w_conv_dw', 'b_conv_dw', 'conv_ln_g', 'conv_ln_b', 'w_conv_out', 'hgrn_lb_logits', 'hgrn_norm_g', 'w_hgrn_out', 'w_out', 'ln1_g', 'ln1_b', 'w_ffn_in', 'w_ffn_dw', 'b_ffn_dw', 'w_ffn_out', 'ln2_g', 'ln2_b']
TWIN_WEIGHTS = ['w_in', 'w_conv_dw', 'b_conv_dw', 'conv_ln_g', 'conv_ln_b', 'w_conv_out', 'hgrn_lb_logits', 'hgrn_norm_g', 'w_hgrn_out', 'w_out', 'ln1_g', 'ln1_b', 'w_ffn_in', 'w_ffn_dw', 'b_ffn_dw', 'w_ffn_out', 'ln2_g', 'ln2_b']
TWIN_DIFF_INPUT = 'x'
TWIN_INPUTS = ['x', 'w_in', 'w_conv_dw', 'b_conv_dw', 'conv_ln_g', 'conv_ln_b', 'w_conv_out', 'hgrn_lb_logits', 'hgrn_norm_g', 'w_hgrn_out', 'w_out', 'ln1_g', 'ln1_b', 'w_ffn_in', 'w_ffn_dw', 'b_ffn_dw', 'w_ffn_out', 'ln2_g', 'ln2_b', 'loss_target', 'm_w_in', 'm_w_conv_dw', 'm_b_conv_dw', 'm_conv_ln_g', 'm_conv_ln_b', 'm_w_conv_out', 'm_hgrn_lb_logits', 'm_hgrn_norm_g', 'm_w_hgrn_out', 'm_w_out', 'm_ln1_g', 'm_ln1_b', 'm_w_ffn_in', 'm_w_ffn_dw', 'm_b_ffn_dw', 'm_w_ffn_out', 'm_ln2_g', 'm_ln2_b', 'v_w_in', 'v_w_conv_dw', 'v_b_conv_dw', 'v_conv_ln_g', 'v_conv_ln_b', 'v_w_conv_out', 'v_hgrn_lb_logits', 'v_hgrn_norm_g', 'v_w_hgrn_out', 'v_w_out', 'v_ln1_g', 'v_ln1_b', 'v_w_ffn_in', 'v_w_ffn_dw', 'v_b_ffn_dw', 'v_w_ffn_out', 'v_ln2_g', 'v_ln2_b']
TWIN_OUTPUTS = ['loss', 'grad_x', 'grad_w_in', 'grad_w_conv_dw', 'grad_b_conv_dw', 'grad_conv_ln_g', 'grad_conv_ln_b', 'grad_w_conv_out', 'grad_hgrn_lb_logits', 'grad_hgrn_norm_g', 'grad_w_hgrn_out', 'grad_w_out', 'grad_ln1_g', 'grad_ln1_b', 'grad_w_ffn_in', 'grad_w_ffn_dw', 'grad_b_ffn_dw', 'grad_w_ffn_out', 'grad_ln2_g', 'grad_ln2_b', 'delta_w_in', 'delta_w_conv_dw', 'delta_b_conv_dw', 'delta_conv_ln_g', 'delta_conv_ln_b', 'delta_w_conv_out', 'delta_hgrn_lb_logits', 'delta_hgrn_norm_g', 'delta_w_hgrn_out', 'delta_w_out', 'delta_ln1_g', 'delta_ln1_b', 'delta_w_ffn_in', 'delta_w_ffn_dw', 'delta_b_ffn_dw', 'delta_w_ffn_out', 'delta_ln2_g', 'delta_ln2_b', 'new_m_w_in', 'new_m_w_conv_dw', 'new_m_b_conv_dw', 'new_m_conv_ln_g', 'new_m_conv_ln_b', 'new_m_w_conv_out', 'new_m_hgrn_lb_logits', 'new_m_hgrn_norm_g', 'new_m_w_hgrn_out', 'new_m_w_out', 'new_m_ln1_g', 'new_m_ln1_b', 'new_m_w_ffn_in', 'new_m_w_ffn_dw', 'new_m_b_ffn_dw', 'new_m_w_ffn_out', 'new_m_ln2_g', 'new_m_ln2_b', 'new_v_w_in', 'new_v_w_conv_dw', 'new_v_b_conv_dw', 'new_v_conv_ln_g', 'new_v_conv_ln_b', 'new_v_w_conv_out', 'new_v_hgrn_lb_logits', 'new_v_hgrn_norm_g', 'new_v_w_hgrn_out', 'new_v_w_out', 'new_v_ln1_g', 'new_v_ln1_b', 'new_v_w_ffn_in', 'new_v_w_ffn_dw', 'new_v_b_ffn_dw', 'new_v_w_ffn_out', 'new_v_ln2_g', 'new_v_ln2_b']
TWIN_LEAF_KINDS = {'loss': 'loss', 'grad_x': 'grad_x', 'grad_w_in': 'grad_w', 'grad_w_conv_dw': 'grad_w', 'grad_b_conv_dw': 'grad_w', 'grad_conv_ln_g': 'grad_w', 'grad_conv_ln_b': 'grad_w', 'grad_w_conv_out': 'grad_w', 'grad_hgrn_lb_logits': 'grad_w', 'grad_hgrn_norm_g': 'grad_w', 'grad_w_hgrn_out': 'grad_w', 'grad_w_out': 'grad_w', 'grad_ln1_g': 'grad_w', 'grad_ln1_b': 'grad_w', 'grad_w_ffn_in': 'grad_w', 'grad_w_ffn_dw': 'grad_w', 'grad_b_ffn_dw': 'grad_w', 'grad_w_ffn_out': 'grad_w', 'grad_ln2_g': 'grad_w', 'grad_ln2_b': 'grad_w', 'delta_w_in': 'delta_w', 'delta_w_conv_dw': 'delta_w', 'delta_b_conv_dw': 'delta_w', 'delta_conv_ln_g': 'delta_w', 'delta_conv_ln_b': 'delta_w', 'delta_w_conv_out': 'delta_w', 'delta_hgrn_lb_logits': 'delta_w', 'delta_hgrn_norm_g': 'delta_w', 'delta_w_hgrn_out': 'delta_w', 'delta_w_out': 'delta_w', 'delta_ln1_g': 'delta_w', 'delta_ln1_b': 'delta_w', 'delta_w_ffn_in': 'delta_w', 'delta_w_ffn_dw': 'delta_w', 'delta_b_ffn_dw': 'delta_w', 'delta_w_ffn_out': 'delta_w', 'delta_ln2_g': 'delta_w', 'delta_ln2_b': 'delta_w', 'new_m_w_in': 'new_m', 'new_m_w_conv_dw': 'new_m', 'new_m_b_conv_dw': 'new_m', 'new_m_conv_ln_g': 'new_m', 'new_m_conv_ln_b': 'new_m', 'new_m_w_conv_out': 'new_m', 'new_m_hgrn_lb_logits': 'new_m', 'new_m_hgrn_norm_g': 'new_m', 'new_m_w_hgrn_out': 'new_m', 'new_m_w_out': 'new_m', 'new_m_ln1_g': 'new_m', 'new_m_ln1_b': 'new_m', 'new_m_w_ffn_in': 'new_m', 'new_m_w_ffn_dw': 'new_m', 'new_m_b_ffn_dw': 'new_m', 'new_m_w_ffn_out': 'new_m', 'new_m_ln2_g': 'new_m', 'new_m_ln2_b': 'new_m', 'new_v_w_in': 'new_v', 'new_v_w_conv_dw': 'new_v', 'new_v_b_conv_dw': 'new_v', 'new_v_conv_ln_g': 'new_v', 'new_v_conv_ln_b': 'new_v', 'new_v_w_conv_out': 'new_v', 'new_v_hgrn_lb_logits': 'new_v', 'new_v_hgrn_norm_g': 'new_v', 'new_v_w_hgrn_out': 'new_v', 'new_v_w_out': 'new_v', 'new_v_ln1_g': 'new_v', 'new_v_ln1_b': 'new_v', 'new_v_w_ffn_in': 'new_v', 'new_v_w_ffn_dw': 'new_v', 'new_v_b_ffn_dw': 'new_v', 'new_v_w_ffn_out': 'new_v', 'new_v_ln2_g': 'new_v', 'new_v_ln2_b': 'new_v'}


def _forward(args):
    return _fwd_reference(*[args[k] for k in FWD_PARAMS])


def _output_shape():
    def fwd():
        inp = _fwd_setup_inputs(0)
        return _fwd_reference(*[inp[k] for k in FWD_PARAMS])
    out = _jax.eval_shape(fwd)
    return out.shape, out.dtype

N_MICROBATCH = 1
ADAM_LR = 0.001
ADAM_B1 = 0.9
ADAM_B2 = 0.999
ADAM_EPS = 1e-08
ADAM_WD = 0.01
ADAM_STEP = 10
PER_EXAMPLE_BATCH_AXIS = {'x': 0, 'loss_target': 0}
SHARED_INPUTS = []
_WEIGHT_DTYPES = {'w_in': _jnp.float32, 'w_conv_dw': _jnp.float32, 'b_conv_dw': _jnp.float32, 'conv_ln_g': _jnp.float32, 'conv_ln_b': _jnp.float32, 'w_conv_out': _jnp.float32, 'hgrn_lb_logits': _jnp.float32, 'hgrn_norm_g': _jnp.float32, 'w_hgrn_out': _jnp.float32, 'w_out': _jnp.float32, 'ln1_g': _jnp.float32, 'ln1_b': _jnp.float32, 'w_ffn_in': _jnp.float32, 'w_ffn_dw': _jnp.float32, 'b_ffn_dw': _jnp.float32, 'w_ffn_out': _jnp.float32, 'ln2_g': _jnp.float32, 'ln2_b': _jnp.float32}
MOMENT_SCALE = {'w_in': 2.150782e-02, 'w_conv_dw': 3.580677e-02, 'b_conv_dw': 1.568936e-01, 'conv_ln_g': 6.164446e-02, 'conv_ln_b': 9.697243e-02, 'w_conv_out': 5.592715e-02, 'hgrn_lb_logits': 2.076505e-03, 'hgrn_norm_g': 2.404048e-02, 'w_hgrn_out': 4.075446e-02, 'w_out': 6.892765e-02, 'ln1_g': 2.212172e+00, 'ln1_b': 1.021691e+00, 'w_ffn_in': 3.861381e-02, 'w_ffn_dw': 2.893758e-02, 'b_ffn_dw': 4.709235e-02, 'w_ffn_out': 7.592029e-02, 'ln2_g': 6.406370e+01, 'ln2_b': 2.328231e+00}


def _to_microbatches(a, axis):
    t = _jnp.moveaxis(a, axis, 0)
    t = t.reshape((N_MICROBATCH, t.shape[0] // N_MICROBATCH) + t.shape[1:])
    return _jnp.moveaxis(t, 1, axis + 1)


def setup_inputs(seed: int = 0) -> dict:
    inp = _fwd_setup_inputs(seed)
    key = _jax.random.fold_in(_jax.random.key(seed), 7919)
    shape, _ = _output_shape()
    out = dict(inp)
    out["loss_target"] = _jax.random.normal(_jax.random.fold_in(key, 0), shape, _jnp.float32)
    for i, name in enumerate(TWIN_WEIGHTS):
        w = inp[name].astype(_jnp.float32)
        if MOMENT_SCALE is None:
            s = _jnp.sqrt(_jnp.mean(_jnp.square(w)) + 1e-30)
        else:
            s = MOMENT_SCALE[name]
        km, kv = _jax.random.split(_jax.random.fold_in(key, i + 1))
        out[name] = w
        out["m_" + name] = s * _jax.random.normal(km, w.shape, _jnp.float32)
        out["v_" + name] = (s * s) * _jax.random.uniform(kv, w.shape, _jnp.float32, 0.5, 1.5)
    if N_MICROBATCH > 1:
        for name, axis in PER_EXAMPLE_BATCH_AXIS.items():
            out[name] = _to_microbatches(out[name], axis)
    return {'x': out['x'], 'w_in': out['w_in'], 'w_conv_dw': out['w_conv_dw'], 'b_conv_dw': out['b_conv_dw'], 'conv_ln_g': out['conv_ln_g'], 'conv_ln_b': out['conv_ln_b'], 'w_conv_out': out['w_conv_out'], 'hgrn_lb_logits': out['hgrn_lb_logits'], 'hgrn_norm_g': out['hgrn_norm_g'], 'w_hgrn_out': out['w_hgrn_out'], 'w_out': out['w_out'], 'ln1_g': out['ln1_g'], 'ln1_b': out['ln1_b'], 'w_ffn_in': out['w_ffn_in'], 'w_ffn_dw': out['w_ffn_dw'], 'b_ffn_dw': out['b_ffn_dw'], 'w_ffn_out': out['w_ffn_out'], 'ln2_g': out['ln2_g'], 'ln2_b': out['ln2_b'], 'loss_target': out['loss_target'], 'm_w_in': out['m_w_in'], 'm_w_conv_dw': out['m_w_conv_dw'], 'm_b_conv_dw': out['m_b_conv_dw'], 'm_conv_ln_g': out['m_conv_ln_g'], 'm_conv_ln_b': out['m_conv_ln_b'], 'm_w_conv_out': out['m_w_conv_out'], 'm_hgrn_lb_logits': out['m_hgrn_lb_logits'], 'm_hgrn_norm_g': out['m_hgrn_norm_g'], 'm_w_hgrn_out': out['m_w_hgrn_out'], 'm_w_out': out['m_w_out'], 'm_ln1_g': out['m_ln1_g'], 'm_ln1_b': out['m_ln1_b'], 'm_w_ffn_in': out['m_w_ffn_in'], 'm_w_ffn_dw': out['m_w_ffn_dw'], 'm_b_ffn_dw': out['m_b_ffn_dw'], 'm_w_ffn_out': out['m_w_ffn_out'], 'm_ln2_g': out['m_ln2_g'], 'm_ln2_b': out['m_ln2_b'], 'v_w_in': out['v_w_in'], 'v_w_conv_dw': out['v_w_conv_dw'], 'v_b_conv_dw': out['v_b_conv_dw'], 'v_conv_ln_g': out['v_conv_ln_g'], 'v_conv_ln_b': out['v_conv_ln_b'], 'v_w_conv_out': out['v_w_conv_out'], 'v_hgrn_lb_logits': out['v_hgrn_lb_logits'], 'v_hgrn_norm_g': out['v_hgrn_norm_g'], 'v_w_hgrn_out': out['v_w_hgrn_out'], 'v_w_out': out['v_w_out'], 'v_ln1_g': out['v_ln1_g'], 'v_ln1_b': out['v_ln1_b'], 'v_w_ffn_in': out['v_w_ffn_in'], 'v_w_ffn_dw': out['v_w_ffn_dw'], 'v_b_ffn_dw': out['v_b_ffn_dw'], 'v_w_ffn_out': out['v_w_ffn_out'], 'v_ln2_g': out['v_ln2_g'], 'v_ln2_b': out['v_ln2_b']}


def _loss(weights, diff, rest, loss_target):
    with _jax.named_scope("forward"):
        args = {**rest, TWIN_DIFF_INPUT: diff, **{k: w.astype(_WEIGHT_DTYPES[k]) for k, w in weights.items()}}
        y = _forward(args)
    with _jax.named_scope("loss_head"):
        err = _jnp.square(y.astype(_jnp.float32) - loss_target)
        return 0.5 * _jnp.sum(_jnp.mean(err, axis=-1)) if err.ndim else 0.5 * err


def _adamw(w, g, m, v):
    m = ADAM_B1 * m + (1.0 - ADAM_B1) * g
    v = ADAM_B2 * v + (1.0 - ADAM_B2) * _jnp.square(g)
    m_hat = m / (1.0 - ADAM_B1 ** ADAM_STEP)
    v_hat = v / (1.0 - ADAM_B2 ** ADAM_STEP)
    delta = -ADAM_LR * (m_hat / (_jnp.sqrt(v_hat) + ADAM_EPS) + ADAM_WD * w)
    return delta, m, v


def reference(x, w_in, w_conv_dw, b_conv_dw, conv_ln_g, conv_ln_b, w_conv_out, hgrn_lb_logits, hgrn_norm_g, w_hgrn_out, w_out, ln1_g, ln1_b, w_ffn_in, w_ffn_dw, b_ffn_dw, w_ffn_out, ln2_g, ln2_b, loss_target, m_w_in, m_w_conv_dw, m_b_conv_dw, m_conv_ln_g, m_conv_ln_b, m_w_conv_out, m_hgrn_lb_logits, m_hgrn_norm_g, m_w_hgrn_out, m_w_out, m_ln1_g, m_ln1_b, m_w_ffn_in, m_w_ffn_dw, m_b_ffn_dw, m_w_ffn_out, m_ln2_g, m_ln2_b, v_w_in, v_w_conv_dw, v_b_conv_dw, v_conv_ln_g, v_conv_ln_b, v_w_conv_out, v_hgrn_lb_logits, v_hgrn_norm_g, v_w_hgrn_out, v_w_out, v_ln1_g, v_ln1_b, v_w_ffn_in, v_w_ffn_dw, v_b_ffn_dw, v_w_ffn_out, v_ln2_g, v_ln2_b):
    given = dict(x=x, w_in=w_in, w_conv_dw=w_conv_dw, b_conv_dw=b_conv_dw, conv_ln_g=conv_ln_g, conv_ln_b=conv_ln_b, w_conv_out=w_conv_out, hgrn_lb_logits=hgrn_lb_logits, hgrn_norm_g=hgrn_norm_g, w_hgrn_out=w_hgrn_out, w_out=w_out, ln1_g=ln1_g, ln1_b=ln1_b, w_ffn_in=w_ffn_in, w_ffn_dw=w_ffn_dw, b_ffn_dw=b_ffn_dw, w_ffn_out=w_ffn_out, ln2_g=ln2_g, ln2_b=ln2_b, loss_target=loss_target, m_w_in=m_w_in, m_w_conv_dw=m_w_conv_dw, m_b_conv_dw=m_b_conv_dw, m_conv_ln_g=m_conv_ln_g, m_conv_ln_b=m_conv_ln_b, m_w_conv_out=m_w_conv_out, m_hgrn_lb_logits=m_hgrn_lb_logits, m_hgrn_norm_g=m_hgrn_norm_g, m_w_hgrn_out=m_w_hgrn_out, m_w_out=m_w_out, m_ln1_g=m_ln1_g, m_ln1_b=m_ln1_b, m_w_ffn_in=m_w_ffn_in, m_w_ffn_dw=m_w_ffn_dw, m_b_ffn_dw=m_b_ffn_dw, m_w_ffn_out=m_w_ffn_out, m_ln2_g=m_ln2_g, m_ln2_b=m_ln2_b, v_w_in=v_w_in, v_w_conv_dw=v_w_conv_dw, v_b_conv_dw=v_b_conv_dw, v_conv_ln_g=v_conv_ln_g, v_conv_ln_b=v_conv_ln_b, v_w_conv_out=v_w_conv_out, v_hgrn_lb_logits=v_hgrn_lb_logits, v_hgrn_norm_g=v_hgrn_norm_g, v_w_hgrn_out=v_w_hgrn_out, v_w_out=v_w_out, v_ln1_g=v_ln1_g, v_ln1_b=v_ln1_b, v_w_ffn_in=v_w_ffn_in, v_w_ffn_dw=v_w_ffn_dw, v_b_ffn_dw=v_b_ffn_dw, v_w_ffn_out=v_w_ffn_out, v_ln2_g=v_ln2_g, v_ln2_b=v_ln2_b)
    weights = {n: given[n] for n in TWIN_WEIGHTS}
    shared = {n: given[n] for n in SHARED_INPUTS}
    per_example = {n: given[n] for n in ['x']}
    grad_fn = _jax.value_and_grad(_loss, argnums=(0, 1))

    def one_microbatch(ex, loss_target):
        ex = dict(ex)
        diff = ex.pop(TWIN_DIFF_INPUT)
        return grad_fn(weights, diff, {**shared, **ex}, loss_target)

    if N_MICROBATCH == 1:
        loss, (grad_w, grad_x) = one_microbatch(per_example, given["loss_target"])
    else:
        def body(carry, xs):
            loss_sum, grad_sum = carry
            l_k, (gw_k, gx_k) = one_microbatch(xs[0], xs[1])
            with _jax.named_scope("update"):
                return (loss_sum + l_k, _jax.tree.map(_jnp.add, grad_sum, gw_k)), gx_k

        init = (_jnp.zeros((), _jnp.float32), _jax.tree.map(_jnp.zeros_like, weights))
        (loss, grad_w), grad_x = _jax.lax.scan(body, init, (per_example, given["loss_target"]))
    with _jax.named_scope("update"):
        delta_w, new_m, new_v = {}, {}, {}
        for n in TWIN_WEIGHTS:
            delta_w[n], new_m[n], new_v[n] = _adamw(weights[n], grad_w[n], given["m_" + n], given["v_" + n])
    return (loss, grad_x, *[grad_w[n] for n in TWIN_WEIGHTS], *[delta_w[n] for n in TWIN_WEIGHTS],
            *[new_m[n] for n in TWIN_WEIGHTS], *[new_v[n] for n in TWIN_WEIGHTS])
```

```python
import functools
import math

import jax
import jax.numpy as jnp
from jax import lax
from jax.experimental import pallas as pl
from jax.experimental.pallas import tpu as pltpu

F32 = jnp.float32
BF16 = jnp.bfloat16

N_DEV = 8
D_MODEL = 1024
CONV_DIM = 512
CONV_K = 31
HGRN_DIM = 1024
HEADS = 8
HEAD_DIM = 128
D_FF = 2816
FFN_K = 3
FF_SHARD = 2 * D_FF // N_DEV
IN_COLS = 7168
IN_SHARD = IN_COLS // N_DEV
LN_EPS = 1e-5
RMS_EPS = 1e-6
ALPHA = 2.0 ** 0.25

ADAM_LR = 0.001
ADAM_B1 = 0.9
ADAM_B2 = 0.999
ADAM_EPS = 1e-08
ADAM_WD = 0.01
ADAM_STEP = 10

CHUNK = 64
CHUNKS_PER_BLOCK = 4
CONV_HALO = 32
FFN_HALO = 8
ROW_BLOCK = 64
VMEM_LIMIT = 48 * 1024 * 1024

MESH = pl.DeviceIdType.MESH
ANY = pl.BlockSpec(memory_space=pl.ANY)

NN = (((1,), (0,)), ((), ()))
NT = (((1,), (1,)), ((), ()))
TN = (((0,), (0,)), ((), ()))


def _params(sem):
    return pltpu.CompilerParams(dimension_semantics=sem, vmem_limit_bytes=VMEM_LIMIT)


def _dot(a, b, dims):
    return lax.dot_general(a.astype(BF16), b.astype(BF16), dims, preferred_element_type=F32)


def _dot_f32(a, b, dims):
    return lax.dot_general(a, b, dims, preferred_element_type=F32, precision=lax.Precision.HIGHEST)


def _sigmoid(x):
    return jax.nn.sigmoid(x)


def _ln(r):
    mu = jnp.mean(r, axis=-1, keepdims=True)
    xc = r - mu
    var = jnp.mean(xc * xc, axis=-1, keepdims=True)
    rstd = lax.rsqrt(var + LN_EPS)
    return xc * rstd, rstd


def _ln_bwd(dy, xhat, rstd, g):
    dxh = dy * g
    m1 = jnp.mean(dxh, axis=-1, keepdims=True)
    m2 = jnp.mean(dxh * xhat, axis=-1, keepdims=True)
    return rstd * (dxh - m1 - xhat * m2)


def _colsum(x):
    return jnp.sum(x, axis=0, keepdims=True)


def _mm(name, a, b, out_shape, out_dtype, grid, a_spec, b_spec, o_spec, dims, acc_shape):
    nk = grid[2]

    def body(a_ref, b_ref, o_ref, acc_ref):
        k = pl.program_id(2)

        @pl.when(k == 0)
        def _():
            acc_ref[...] = jnp.zeros_like(acc_ref)

        acc_ref[...] += _dot(a_ref[...], b_ref[...], dims)

        @pl.when(k == nk - 1)
        def _():
            o_ref[...] = acc_ref[...].astype(o_ref.dtype)

    return pl.pallas_call(
        body, name=name, grid=grid, in_specs=[a_spec, b_spec], out_specs=o_spec,
        out_shape=jax.ShapeDtypeStruct(out_shape, out_dtype),
        scratch_shapes=[pltpu.VMEM(acc_shape, F32)],
        compiler_params=_params(("parallel", "parallel", "arbitrary")),
    )(a, b)


def _pick(t, pref):
    return pref if t % pref == 0 else t


def _glu(p):
    return p[:, :CONV_DIM] * _sigmoid(p[:, CONV_DIM:])


def _conv_fwd(proj, w_dw, b_dw, g, b):
    t = proj.shape[0]
    tm = _pick(t, 256)
    nh = tm // CONV_HALO

    def body(p_ref, ph_ref, w_ref, bd_ref, g_ref, b_ref, act_ref, pre_ref, xs_ref):
        i = pl.program_id(0)
        halo = _glu(ph_ref[...])
        xs_ref[0:CONV_HALO, :] = jnp.where(i == 0, 0.0, halo)
        xs_ref[CONV_HALO:CONV_HALO + tm, :] = _glu(p_ref[...])
        for r in range(tm // ROW_BLOCK):
            acc = jnp.broadcast_to(bd_ref[...], (ROW_BLOCK, CONV_DIM))
            for k in range(CONV_K):
                off = r * ROW_BLOCK + CONV_HALO - (CONV_K - 1) + k
                acc = acc + w_ref[k:k + 1, :] * xs_ref[off:off + ROW_BLOCK, :]
            rows = slice(r * ROW_BLOCK, (r + 1) * ROW_BLOCK)
            pre_ref[rows, :] = acc
            xhat, _ = _ln(acc)
            yln = xhat * g_ref[...] + b_ref[...]
            act_ref[rows, :] = (yln * _sigmoid(yln)).astype(BF16)

    full = lambda s: pl.BlockSpec(s, lambda i: (0, 0))
    return pl.pallas_call(
        body, name="conv_fwd", grid=(t // tm,),
        in_specs=[pl.BlockSpec((tm, 2 * CONV_DIM), lambda i: (i, 0)),
                  pl.BlockSpec((CONV_HALO, 2 * CONV_DIM), lambda i: (jnp.maximum(i * nh - 1, 0), 0)),
                  full((CONV_K, CONV_DIM)), full((1, CONV_DIM)), full((1, CONV_DIM)), full((1, CONV_DIM))],
        out_specs=[pl.BlockSpec((tm, CONV_DIM), lambda i: (i, 0)), pl.BlockSpec((tm, CONV_DIM), lambda i: (i, 0))],
        out_shape=[jax.ShapeDtypeStruct((t, CONV_DIM), BF16), jax.ShapeDtypeStruct((t, CONV_DIM), F32)],
        scratch_shapes=[pltpu.VMEM((CONV_HALO + tm, CONV_DIM), F32)],
        compiler_params=_params(("arbitrary",)),
    )(proj, proj, w_dw, b_dw, g, b)


def _conv_bwd_norm(d_c, pre, g, b):
    t = d_c.shape[0]
    tm = _pick(t, 256)
    nt = t // tm

    def body(dc_ref, pre_ref, g_ref, b_ref, dpre_ref, sums_ref):
        i = pl.program_id(0)

        @pl.when(i == 0)
        def _():
            sums_ref[...] = jnp.zeros_like(sums_ref)

        xhat, rstd = _ln(pre_ref[...])
        yln = xhat * g_ref[...] + b_ref[...]
        sg = _sigmoid(yln)
        dyln = dc_ref[...] * (sg * (1.0 + yln * (1.0 - sg)))
        dpre = _ln_bwd(dyln, xhat, rstd, g_ref[...])
        dpre_ref[...] = dpre
        sums_ref[0:1, :] += _colsum(dyln * xhat)
        sums_ref[1:2, :] += _colsum(dyln)
        sums_ref[2:3, :] += _colsum(dpre)

    full = lambda s: pl.BlockSpec(s, lambda i: (0, 0))
    tile = pl.BlockSpec((tm, CONV_DIM), lambda i: (i, 0))
    return pl.pallas_call(
        body, name="conv_bwd_norm", grid=(nt,),
        in_specs=[tile, tile, full((1, CONV_DIM)), full((1, CONV_DIM))],
        out_specs=[tile, full((8, CONV_DIM))],
        out_shape=[jax.ShapeDtypeStruct((t, CONV_DIM), F32), jax.ShapeDtypeStruct((8, CONV_DIM), F32)],
        compiler_params=_params(("arbitrary",)),
    )(d_c, pre, g, b)


def _conv_bwd_dw(d_pre, proj, w_dw):
    t = d_pre.shape[0]
    tm = _pick(t, 256)
    nt = t // tm
    nh = tm // CONV_HALO
    last_h = t // CONV_HALO - 1

    def body(dp_ref, dph_ref, p_ref, ph_ref, w_ref, dproj_ref, dw_ref, xs_ref, ds_ref):
        i = pl.program_id(0)

        @pl.when(i == 0)
        def _():
            dw_ref[...] = jnp.zeros_like(dw_ref)

        halo = _glu(ph_ref[...])
        xs_ref[0:CONV_HALO, :] = jnp.where(i == 0, 0.0, halo)
        xs_ref[CONV_HALO:CONV_HALO + tm, :] = _glu(p_ref[...])
        ds_ref[0:tm, :] = dp_ref[...]
        ds_ref[tm:tm + CONV_HALO, :] = jnp.where(i == nt - 1, 0.0, dph_ref[...])
        for r in range(tm // ROW_BLOCK):
            base = r * ROW_BLOCK
            dpre = ds_ref[base:base + ROW_BLOCK, :]
            acc = jnp.zeros((ROW_BLOCK, CONV_DIM), F32)
            for k in range(CONV_K):
                acc = acc + w_ref[k:k + 1, :] * ds_ref[base + CONV_K - 1 - k:base + CONV_K - 1 - k + ROW_BLOCK, :]
                off = base + CONV_HALO - (CONV_K - 1) + k
                dw_ref[k:k + 1, :] += _colsum(dpre * xs_ref[off:off + ROW_BLOCK, :])
            rows = slice(base, base + ROW_BLOCK)
            cval = p_ref[rows, 0:CONV_DIM]
            sg = _sigmoid(p_ref[rows, CONV_DIM:2 * CONV_DIM])
            dproj_ref[rows, 0:CONV_DIM] = (acc * sg).astype(BF16)
            dproj_ref[rows, CONV_DIM:2 * CONV_DIM] = (acc * cval * sg * (1.0 - sg)).astype(BF16)

    full = lambda s: pl.BlockSpec(s, lambda i: (0, 0))
    return pl.pallas_call(
        body, name="conv_bwd_dw", grid=(nt,),
        in_specs=[pl.BlockSpec((tm, CONV_DIM), lambda i: (i, 0)),
                  pl.BlockSpec((CONV_HALO, CONV_DIM), lambda i: (jnp.minimum((i + 1) * nh, last_h), 0)),
                  pl.BlockSpec((tm, 2 * CONV_DIM), lambda i: (i, 0)),
                  pl.BlockSpec((CONV_HALO, 2 * CONV_DIM), lambda i: (jnp.maximum(i * nh - 1, 0), 0)),
                  full((CONV_K, CONV_DIM))],
        out_specs=[pl.BlockSpec((tm, 2 * CONV_DIM), lambda i: (i, 0)), full((CONV_HALO, CONV_DIM))],
        out_shape=[jax.ShapeDtypeStruct((t, 2 * CONV_DIM), BF16), jax.ShapeDtypeStruct((CONV_HALO, CONV_DIM), F32)],
        scratch_shapes=[pltpu.VMEM((CONV_HALO + tm, CONV_DIM), F32), pltpu.VMEM((tm + CONV_HALO, CONV_DIM), F32)],
        compiler_params=_params(("arbitrary",)),
    )(d_pre, d_pre, proj, proj, w_dw)


def _lower_bound(logit_ref):
    l0 = logit_ref[0:1, :]
    l1 = logit_ref[1:2, :]
    m = jnp.maximum(l0, l1)
    e0 = jnp.exp(l0 - m)
    e1 = jnp.exp(l1 - m)
    return e0 / (e0 + e1)


def _tri(lower):
    r = lax.broadcasted_iota(jnp.int32, (CHUNK, CHUNK), 0)
    c = lax.broadcasted_iota(jnp.int32, (CHUNK, CHUNK), 1)
    return (c <= r) if lower else (c >= r)


def _hgrn_gates(fz, lb):
    s = _sigmoid(fz)
    sn = _sigmoid(-fz)
    f = lb + (1.0 - lb) * s
    return s, sn, f


def _hgrn_fwd(proj, logits, norm_g):
    t = proj.shape[0]
    tm = CHUNK * CHUNKS_PER_BLOCK if t % (CHUNK * CHUNKS_PER_BLOCK) == 0 else CHUNK
    cpb = tm // CHUNK
    nt = t // tm
    half = CHUNK // 2

    def body(qz_ref, fz_ref, iv_ref, gz_ref, lg_ref, ng_ref, o_ref, og_ref, st_ref, state_ref):
        j = pl.program_id(1)

        @pl.when(j == 0)
        def _():
            state_ref[...] = jnp.zeros_like(state_ref)

        lb = _lower_bound(lg_ref)
        causal = _tri(True)
        tri = causal.astype(F32)
        for c in range(cpb):
            rows = slice(c * CHUNK, (c + 1) * CHUNK)
            qz = qz_ref[rows, :]
            q = qz * _sigmoid(qz)
            _, sn, f = _hgrn_gates(fz_ref[rows, :], lb)
            kk = (1.0 - lb) * sn
            v = iv_ref[rows, :]
            bcum = _dot_f32(tri, jnp.log(f), NN)
            bref = bcum[half - 1:half, :]
            blast = bcum[CHUNK - 1:CHUNK, :]
            qe = q * jnp.exp(bcum - bref)
            ke = kk * jnp.exp(bref - bcum)
            a = jnp.where(causal, _dot(qe, ke, NT), 0.0)
            state = state_ref[...]
            st_ref[c] = state.astype(BF16)
            o = _dot(a, v, NN) + _dot(q * jnp.exp(bcum), state, NT)
            state_ref[...] = state * jnp.exp(blast) + _dot(v, kk * jnp.exp(blast - bcum), TN)
            o_ref[rows, :] = o
            r = lax.rsqrt(jnp.mean(o * o, axis=-1, keepdims=True) + RMS_EPS)
            gz = gz_ref[rows, :]
            og_ref[rows, :] = ((o * r * ng_ref[...]) * (gz * _sigmoid(gz))).astype(BF16)

    col = lambda base: pl.BlockSpec((tm, HEAD_DIM), lambda h, j: (j, base + h))
    return pl.pallas_call(
        body, name="hgrn_fwd", grid=(HEADS, nt),
        in_specs=[col(8), col(16), col(24), col(32),
                  pl.BlockSpec((2, HEAD_DIM), lambda h, j: (0, h)), pl.BlockSpec((1, HEAD_DIM), lambda h, j: (0, h))],
        out_specs=[col(0), col(0), pl.BlockSpec((None, cpb, HEAD_DIM, HEAD_DIM), lambda h, j: (h, j, 0, 0))],
        out_shape=[jax.ShapeDtypeStruct((t, HGRN_DIM), F32), jax.ShapeDtypeStruct((t, HGRN_DIM), BF16),
                   jax.ShapeDtypeStruct((HEADS, t // CHUNK, HEAD_DIM, HEAD_DIM), BF16)],
        scratch_shapes=[pltpu.VMEM((HEAD_DIM, HEAD_DIM), F32)],
        compiler_params=_params(("parallel", "arbitrary")),
    )(proj, proj, proj, proj, logits, norm_g)


def _hgrn_bwd(d_og, o, proj, states, logits, norm_g):
    t = proj.shape[0]
    tm = CHUNK * CHUNKS_PER_BLOCK if t % (CHUNK * CHUNKS_PER_BLOCK) == 0 else CHUNK
    cpb = tm // CHUNK
    nt = t // tm
    half = CHUNK // 2

    def body(dog_ref, o_ref, qz_ref, fz_ref, iv_ref, gz_ref, st_ref, lg_ref, ng_ref,
             dqz_ref, dfz_ref, div_ref, dgz_ref, sums_ref, dstate_ref):
        j = pl.program_id(1)

        @pl.when(j == 0)
        def _():
            dstate_ref[...] = jnp.zeros_like(dstate_ref)
            sums_ref[...] = jnp.zeros_like(sums_ref)

        lb = _lower_bound(lg_ref)
        ng = ng_ref[...]
        causal = _tri(True)
        tri = causal.astype(F32)
        triu = _tri(False).astype(F32)
        tril_strict = 1.0 - triu
        for c in reversed(range(cpb)):
            rows = slice(c * CHUNK, (c + 1) * CHUNK)
            qz = qz_ref[rows, :]
            sq = _sigmoid(qz)
            q = qz * sq
            s, sn, f = _hgrn_gates(fz_ref[rows, :], lb)
            kk = (1.0 - lb) * sn
            v = iv_ref[rows, :]
            bcum = _dot_f32(tri, jnp.log(f), NN)
            bref = bcum[half - 1:half, :]
            blast = bcum[CHUNK - 1:CHUNK, :]
            eb = jnp.exp(bcum)
            ebr = jnp.exp(bcum - bref)
            ekr = jnp.exp(bref - bcum)
            ebl = jnp.exp(blast - bcum)
            qe = q * ebr
            ke = kk * ekr
            a = jnp.where(causal, _dot(qe, ke, NT), 0.0)

            ov = o_ref[rows, :]
            r = lax.rsqrt(jnp.mean(ov * ov, axis=-1, keepdims=True) + RMS_EPS)
            on = ov * r
            gz = gz_ref[rows, :]
            sg = _sigmoid(gz)
            dog = dog_ref[rows, :]
            dgz_ref[rows, :] = (dog * (on * ng) * (sg * (1.0 + gz * (1.0 - sg)))).astype(BF16)
            d_ong = dog * (gz * sg)
            sums_ref[0:1, :] += _colsum(d_ong * on)
            d_on = d_ong * ng
            do = r * (d_on - on * jnp.mean(d_on * on, axis=-1, keepdims=True))

            state = st_ref[c]
            dstate = dstate_ref[...]
            dv = _dot(a, do, TN) + _dot(kk * ebl, dstate, NT)
            da = jnp.where(causal, _dot(do, v, NT), 0.0)
            qe_mxu = qe.astype(BF16)
            ke_mxu = ke.astype(BF16)
            dqe = _dot(da, ke_mxu, NN)
            dke = _dot(da, qe_mxu, TN)
            dq_inter = _dot(do, state, NN) * eb
            dk_inter = _dot(v, dstate, NN) * ebl
            dq = dqe * ebr + dq_inter
            dk = dke * ekr + dk_inter
            eblast = jnp.exp(blast)
            dstate_ref[...] = dstate * eblast + _dot(do, q * eb, TN)

            through = eblast * _colsum(dstate * state.astype(F32))
            inside = qe_mxu.astype(F32) * dqe - ke_mxu.astype(F32) * dke
            dlf = (_dot_f32(triu, inside + q * dq_inter, NN) + _dot_f32(tril_strict, kk * dk_inter, NN) + through)
            common = sn * (dlf / f - dk)
            dfz_ref[rows, :] = ((1.0 - lb) * s * common).astype(BF16)
            sums_ref[1:2, :] += _colsum(common)
            dqz_ref[rows, :] = (dq * (sq * (1.0 + qz * (1.0 - sq)))).astype(BF16)
            div_ref[rows, :] = dv.astype(BF16)

        @pl.when(j == nt - 1)
        def _():
            sums_ref[1:2, :] = sums_ref[1:2, :] * lb * (1.0 - lb)

    rev = lambda base: pl.BlockSpec((tm, HEAD_DIM), lambda h, j: (nt - 1 - j, base + h))
    vec = lambda n: pl.BlockSpec((n, HEAD_DIM), lambda h, j: (0, h))
    bf = jax.ShapeDtypeStruct((t, HGRN_DIM), BF16)
    return pl.pallas_call(
        body, name="hgrn_bwd", grid=(HEADS, nt),
        in_specs=[rev(0), rev(0), rev(8), rev(16), rev(24), rev(32),
                  pl.BlockSpec((None, cpb, HEAD_DIM, HEAD_DIM), lambda h, j: (h, nt - 1 - j, 0, 0)),
                  vec(2), vec(1)],
        out_specs=[rev(0), rev(0), rev(0), rev(0), vec(8)],
        out_shape=[bf, bf, bf, bf, jax.ShapeDtypeStruct((8, HGRN_DIM), F32)],
        scratch_shapes=[pltpu.VMEM((HEAD_DIM, HEAD_DIM), F32)],
        compiler_params=_params(("parallel", "arbitrary")),
    )(d_og, o, proj, proj, proj, proj, states, logits, norm_g)


def _merge_fwd(proj, y_conv, y_hgrn):
    t = proj.shape[0]
    tm = _pick(t, 512)

    def body(m0_ref, m1_ref, yc_ref, yh_ref, o_ref):
        o_ref[...] = (_sigmoid(m0_ref[...]) * yc_ref[...] + _sigmoid(m1_ref[...]) * yh_ref[...]).astype(BF16)

    tile = pl.BlockSpec((tm, D_MODEL), lambda i: (i, 0))
    return pl.pallas_call(
        body, name="merge_fwd", grid=(t // tm,),
        in_specs=[pl.BlockSpec((tm, D_MODEL), lambda i: (i, 5)), pl.BlockSpec((tm, D_MODEL), lambda i: (i, 6)), tile, tile],
        out_specs=tile, out_shape=jax.ShapeDtypeStruct((t, D_MODEL), BF16),
        compiler_params=_params(("parallel",)),
    )(proj, proj, y_conv, y_hgrn)


def _merge_bwd(d_mixed, proj, y_conv, y_hgrn):
    t = proj.shape[0]
    tm = _pick(t, 512)

    def body(dm_ref, m_ref, yc_ref, yh_ref, dy_ref, dmz_ref):
        br = pl.program_id(1)
        y = jnp.where(br == 0, yc_ref[...], yh_ref[...])
        sg = _sigmoid(m_ref[...])
        dm = dm_ref[...]
        dy_ref[...] = (sg * dm).astype(BF16)
        dmz_ref[...] = (dm * y * sg * (1.0 - sg)).astype(BF16)

    tile = pl.BlockSpec((tm, D_MODEL), lambda i, br: (i, 0))
    return pl.pallas_call(
        body, name="merge_bwd", grid=(t // tm, 2),
        in_specs=[tile, pl.BlockSpec((tm, D_MODEL), lambda i, br: (i, 5 + br)), tile, tile],
        out_specs=[pl.BlockSpec((None, tm, D_MODEL), lambda i, br: (br, i, 0)), pl.BlockSpec((tm, D_MODEL), lambda i, br: (i, br))],
        out_shape=[jax.ShapeDtypeStruct((2, t, D_MODEL), BF16), jax.ShapeDtypeStruct((t, 2 * D_MODEL), BF16)],
        compiler_params=_params(("parallel", "arbitrary")),
    )(d_mixed, proj, y_conv, y_hgrn)


def _ln1_fwd(x, mix, g, b):
    t = x.shape[0]
    tm = _pick(t, 512)

    def body(x_ref, mix_ref, g_ref, b_ref, r_ref, xb_ref):
        r = ALPHA * x_ref[...] + mix_ref[...]
        r_ref[...] = r
        xhat, _ = _ln(r)
        xb_ref[...] = (xhat * g_ref[...] + b_ref[...]).astype(BF16)

    tile = pl.BlockSpec((tm, D_MODEL), lambda i: (i, 0))
    vec = pl.BlockSpec((1, D_MODEL), lambda i: (0, 0))
    return pl.pallas_call(
        body, name="ln1_fwd", grid=(t // tm,), in_specs=[tile, tile, vec, vec], out_specs=[tile, tile],
        out_shape=[jax.ShapeDtypeStruct((t, D_MODEL), F32), jax.ShapeDtypeStruct((t, D_MODEL), BF16)],
        compiler_params=_params(("parallel",)),
    )(x, mix, g, b)


def _ln2_loss_bwd(r1, y_ffn, target, g1, b1, g2, b2):
    t = r1.shape[0]
    tm = _pick(t, 256)
    nt = t // tm

    def body(r1_ref, y_ref, tg_ref, g1_ref, b1_ref, g2_ref, b2_ref, dr_ref, sums_ref, sq_ref):
        i = pl.program_id(0)

        @pl.when(i == 0)
        def _():
            sums_ref[...] = jnp.zeros_like(sums_ref)
            sq_ref[...] = jnp.zeros_like(sq_ref)

        xh1, _ = _ln(r1_ref[...])
        x1 = xh1 * g1_ref[...] + b1_ref[...]
        xh2, rstd2 = _ln(ALPHA * x1 + y_ref[...])
        diff = xh2 * g2_ref[...] + b2_ref[...] - tg_ref[...]
        dy = diff * (1.0 / D_MODEL)
        dr_ref[...] = _ln_bwd(dy, xh2, rstd2, g2_ref[...])
        sums_ref[0:1, :] += _colsum(dy * xh2)
        sums_ref[1:2, :] += _colsum(dy)
        sq_ref[...] += _colsum(diff * diff)

        @pl.when(i == nt - 1)
        def _():
            total = jnp.sum(sq_ref[...], axis=-1, keepdims=True) * (0.5 / D_MODEL)
            sums_ref[2:3, :] = jnp.broadcast_to(total, (1, D_MODEL))

    tile = pl.BlockSpec((tm, D_MODEL), lambda i: (i, 0))
    vec = pl.BlockSpec((1, D_MODEL), lambda i: (0, 0))
    return pl.pallas_call(
        body, name="ln2_loss_bwd", grid=(nt,), in_specs=[tile, tile, tile, vec, vec, vec, vec],
        out_specs=[tile, pl.BlockSpec((8, D_MODEL), lambda i: (0, 0))],
        out_shape=[jax.ShapeDtypeStruct((t, D_MODEL), F32), jax.ShapeDtypeStruct((8, D_MODEL), F32)],
        scratch_shapes=[pltpu.VMEM((1, D_MODEL), F32)],
        compiler_params=_params(("arbitrary",)),
    )(r1, y_ffn, target, g1, b1, g2, b2)


def _ln1_bwd(d_r2, d_x1_ffn, r1, g1):
    t = r1.shape[0]
    tm = _pick(t, 256)

    def body(dr2_ref, dx_ref, r1_ref, g_ref, dr1_ref, sums_ref):
        i = pl.program_id(0)

        @pl.when(i == 0)
        def _():
            sums_ref[...] = jnp.zeros_like(sums_ref)

        xhat, rstd = _ln(r1_ref[...])
        dx1 = ALPHA * dr2_ref[...] + dx_ref[...]
        dr1_ref[...] = _ln_bwd(dx1, xhat, rstd, g_ref[...])
        sums_ref[0:1, :] += _colsum(dx1 * xhat)
        sums_ref[1:2, :] += _colsum(dx1)

    tile = pl.BlockSpec((tm, D_MODEL), lambda i: (i, 0))
    vec = pl.BlockSpec((1, D_MODEL), lambda i: (0, 0))
    return pl.pallas_call(
        body, name="ln1_bwd", grid=(t // tm,), in_specs=[tile, tile, tile, vec],
        out_specs=[tile, pl.BlockSpec((8, D_MODEL), lambda i: (0, 0))],
        out_shape=[jax.ShapeDtypeStruct((t, D_MODEL), F32), jax.ShapeDtypeStruct((8, D_MODEL), F32)],
        compiler_params=_params(("arbitrary",)),
    )(d_r2, d_x1_ffn, r1, g1)


def _grad_x(d_r1, d_x_proj):
    t = d_r1.shape[0]
    tm = _pick(t, 512)

    def body(a_ref, b_ref, o_ref):
        o_ref[...] = ALPHA * a_ref[...] + b_ref[...]

    tile = pl.BlockSpec((tm, D_MODEL), lambda i: (i, 0))
    return pl.pallas_call(
        body, name="grad_x", grid=(t // tm,), in_specs=[tile, tile], out_specs=tile,
        out_shape=jax.ShapeDtypeStruct((t, D_MODEL), F32), compiler_params=_params(("parallel",)),
    )(d_r1, d_x_proj)


_GELU_C = math.sqrt(2.0 / math.pi)


def _gelu_parts(u):
    inner = _GELU_C * (u + 0.044715 * u * u * u)
    th = jnp.tanh(inner)
    return th, 0.5 * u * (1.0 + th)


def _ffn_act_fwd(z, w_dw, b_dw):
    t = z.shape[2]
    tm = _pick(t, 256)
    nh = tm // FFN_HALO

    def body(z_ref, zh_ref, w_ref, b_ref, act_ref, us_ref):
        i = pl.program_id(1)
        us_ref[0:FFN_HALO, :] = jnp.where(i == 0, 0.0, zh_ref[...])
        us_ref[FFN_HALO:FFN_HALO + tm, :] = z_ref[0]
        for r in range(tm // ROW_BLOCK):
            base = r * ROW_BLOCK
            uc = jnp.broadcast_to(b_ref[...], (ROW_BLOCK, FF_SHARD))
            for k in range(FFN_K):
                off = base + FFN_HALO - (FFN_K - 1) + k
                uc = uc + w_ref[k:k + 1, :] * us_ref[off:off + ROW_BLOCK, :]
            _, gelu = _gelu_parts(uc)
            act_ref[base:base + ROW_BLOCK, :] = (gelu * z_ref[1, base:base + ROW_BLOCK, :]).astype(BF16)

    return pl.pallas_call(
        body, name="ffn_act_fwd", grid=(4, t // tm),
        in_specs=[pl.BlockSpec((2, None, tm, FF_SHARD), lambda j, i: (0, j, i, 0)),
                  pl.BlockSpec((None, None, FFN_HALO, FF_SHARD), lambda j, i: (0, j, jnp.maximum(i * nh - 1, 0), 0)),
                  pl.BlockSpec((None, FFN_K, FF_SHARD), lambda j, i: (j, 0, 0)),
                  pl.BlockSpec((None, 1, FF_SHARD), lambda j, i: (j, 0, 0))],
        out_specs=pl.BlockSpec((None, tm, FF_SHARD), lambda j, i: (j, i, 0)),
        out_shape=jax.ShapeDtypeStruct((4, t, FF_SHARD), BF16),
        scratch_shapes=[pltpu.VMEM((FFN_HALO + tm, FF_SHARD), F32)],
        compiler_params=_params(("parallel", "arbitrary")),
    )(z, z, w_dw, b_dw)


def _ffn_act_bwd(d_act, z, w_dw, b_dw):
    t = z.shape[2]
    tm = _pick(t, 256)
    nt = t // tm
    nh = tm // FFN_HALO
    last_h = t // FFN_HALO - 1
    pad = FFN_HALO - (FFN_K - 1)

    def gelu_grad(uc):
        th, gelu = _gelu_parts(uc)
        dgelu = 0.5 * (1.0 + th) + 0.5 * uc * (1.0 - th * th) * _GELU_C * (1.0 + 3.0 * 0.044715 * uc * uc)
        return gelu, dgelu

    def body(da_ref, dah_ref, z_ref, zp_ref, zn_ref, w_ref, b_ref, dz_ref, sums_ref, us_ref, ds_ref):
        i = pl.program_id(1)

        @pl.when(i == 0)
        def _():
            sums_ref[...] = jnp.zeros_like(sums_ref)

        us_ref[0:FFN_HALO, :] = jnp.where(i == 0, 0.0, zp_ref[...])
        us_ref[FFN_HALO:FFN_HALO + tm, :] = z_ref[0]
        us_ref[FFN_HALO + tm:2 * FFN_HALO + tm, :] = zn_ref[0]

        def conv(base, n):
            uc = jnp.broadcast_to(b_ref[...], (n, FF_SHARD))
            for k in range(FFN_K):
                uc = uc + w_ref[k:k + 1, :] * us_ref[base + pad + k:base + pad + k + n, :]
            return uc

        for r in range(tm // ROW_BLOCK):
            base = r * ROW_BLOCK
            rows = slice(base, base + ROW_BLOCK)
            gelu, dgelu = gelu_grad(conv(base, ROW_BLOCK))
            da = da_ref[rows, :]
            dz_ref[1, rows, :] = (da * gelu).astype(BF16)
            duc = da * z_ref[1, rows, :] * dgelu
            ds_ref[rows, :] = duc
            for k in range(FFN_K):
                sums_ref[k:k + 1, :] += _colsum(duc * us_ref[base + pad + k:base + pad + k + ROW_BLOCK, :])
            sums_ref[FFN_K:FFN_K + 1, :] += _colsum(duc)
        _, dgelu_h = gelu_grad(conv(tm, FFN_HALO))
        ds_ref[tm:tm + FFN_HALO, :] = jnp.where(i == nt - 1, 0.0, dah_ref[...] * zn_ref[1] * dgelu_h)
        for r in range(tm // ROW_BLOCK):
            base = r * ROW_BLOCK
            du = jnp.zeros((ROW_BLOCK, FF_SHARD), F32)
            for k in range(FFN_K):
                off = base + FFN_K - 1 - k
                du = du + w_ref[k:k + 1, :] * ds_ref[off:off + ROW_BLOCK, :]
            dz_ref[0, base:base + ROW_BLOCK, :] = du.astype(BF16)

    nxt = lambda j, i: jnp.minimum((i + 1) * nh, last_h)
    return pl.pallas_call(
        body, name="ffn_act_bwd", grid=(4, nt),
        in_specs=[pl.BlockSpec((None, tm, FF_SHARD), lambda j, i: (j, i, 0)),
                  pl.BlockSpec((None, FFN_HALO, FF_SHARD), lambda j, i: (j, nxt(j, i), 0)),
                  pl.BlockSpec((2, None, tm, FF_SHARD), lambda j, i: (0, j, i, 0)),
                  pl.BlockSpec((None, None, FFN_HALO, FF_SHARD), lambda j, i: (0, j, jnp.maximum(i * nh - 1, 0), 0)),
                  pl.BlockSpec((2, None, FFN_HALO, FF_SHARD), lambda j, i: (0, j, nxt(j, i), 0)),
                  pl.BlockSpec((None, FFN_K, FF_SHARD), lambda j, i: (j, 0, 0)),
                  pl.BlockSpec((None, 1, FF_SHARD), lambda j, i: (j, 0, 0))],
        out_specs=[pl.BlockSpec((2, None, tm, FF_SHARD), lambda j, i: (0, j, i, 0)),
                   pl.BlockSpec((None, 8, FF_SHARD), lambda j, i: (j, 0, 0))],
        out_shape=[jax.ShapeDtypeStruct((2, 4, t, FF_SHARD), BF16), jax.ShapeDtypeStruct((4, 8, FF_SHARD), F32)],
        scratch_shapes=[pltpu.VMEM((2 * FFN_HALO + tm, FF_SHARD), F32), pltpu.VMEM((tm + FFN_HALO, FF_SHARD), F32)],
        compiler_params=_params(("parallel", "arbitrary")),
    )(d_act, d_act, z, z, z, w_dw, b_dw)


def _local_step(x, target, w_in, w_conv_out, w_hgrn_out, w_out, w_ffn_in, w_ffn_out, small):
    t = x.shape[0]
    tm = _pick(t, 512)
    tk = _pick(t, 512)
    nm = t // tm
    nk = t // tk
    d = D_MODEL

    proj = _mm("proj", x, w_in, (t, IN_COLS), F32, (nm, N_DEV, 1),
               pl.BlockSpec((tm, d), lambda i, j, k: (i, 0)),
               pl.BlockSpec((None, d, IN_SHARD), lambda i, j, k: (j, 0, 0)),
               pl.BlockSpec((tm, IN_SHARD), lambda i, j, k: (i, j)), NN, (tm, IN_SHARD))
    c_act, conv_pre = _conv_fwd(proj, small["w_conv_dw"], small["b_conv_dw"], small["conv_ln_g"], small["conv_ln_b"])
    y_conv = _mm("y_conv", c_act, w_conv_out, (t, d), F32, (nm, N_DEV, 1),
                 pl.BlockSpec((tm, CONV_DIM), lambda i, j, k: (i, 0)),
                 pl.BlockSpec((None, CONV_DIM, 128), lambda i, j, k: (j, 0, 0)),
                 pl.BlockSpec((tm, 128), lambda i, j, k: (i, j)), NN, (tm, 128))
    o, og, states = _hgrn_fwd(proj, small["hgrn_lb_logits"], small["hgrn_norm_g"])
    sq_w = pl.BlockSpec((d, d), lambda i, j, k: (0, 0))
    row_tile = pl.BlockSpec((tm, d), lambda i, j, k: (i, 0))
    y_hgrn = _mm("y_hgrn", og, w_hgrn_out, (t, d), F32, (nm, 1, 1), row_tile, sq_w, row_tile, NN, (tm, d))
    mixed = _merge_fwd(proj, y_conv, y_hgrn)
    mix = _mm("mix", mixed, w_out, (t, d), F32, (nm, 1, 1), row_tile, sq_w, row_tile, NN, (tm, d))
    r1, x1b = _ln1_fwd(x, mix, small["ln1_g"], small["ln1_b"])
    z = _mm("ffn_in", x1b, w_ffn_in, (N_DEV, t, FF_SHARD), F32, (nm, N_DEV, 1), row_tile,
            pl.BlockSpec((None, d, FF_SHARD), lambda i, j, k: (j, 0, 0)),
            pl.BlockSpec((None, tm, FF_SHARD), lambda i, j, k: (j, i, 0)), NN, (tm, FF_SHARD))
    z = z.reshape(2, 4, t, FF_SHARD)
    act = _ffn_act_fwd(z, small["w_ffn_dw"], small["b_ffn_dw"])
    y_ffn = _mm("ffn_out", act, w_ffn_out, (t, d), F32, (nm, 1, 4),
                pl.BlockSpec((None, tm, FF_SHARD), lambda i, j, k: (k, i, 0)),
                pl.BlockSpec((None, FF_SHARD, d), lambda i, j, k: (k, 0, 0)), row_tile, NN, (tm, d))

    d_r2, sums_ln2 = _ln2_loss_bwd(r1, y_ffn, target, small["ln1_g"], small["ln1_b"], small["ln2_g"], small["ln2_b"])
    d_act = _mm("d_act", d_r2, w_ffn_out, (4, t, FF_SHARD), F32, (nm, 4, 1), row_tile,
                pl.BlockSpec((None, FF_SHARD, d), lambda i, j, k: (j, 0, 0)),
                pl.BlockSpec((None, tm, FF_SHARD), lambda i, j, k: (j, i, 0)), NT, (tm, FF_SHARD))
    g_w_ffn_out = _mm("g_w_ffn_out", act, d_r2, (4, FF_SHARD, d), F32, (4, 1, nk),
                      pl.BlockSpec((None, tk, FF_SHARD), lambda i, j, k: (i, k, 0)),
                      pl.BlockSpec((tk, d), lambda i, j, k: (k, 0)),
                      pl.BlockSpec((None, FF_SHARD, d), lambda i, j, k: (i, 0, 0)), TN, (FF_SHARD, d))
    d_z, sums_ffn = _ffn_act_bwd(d_act, z, small["w_ffn_dw"], small["b_ffn_dw"])
    d_z8 = d_z.reshape(N_DEV, t, FF_SHARD)
    d_x1_ffn = _mm("d_x1_ffn", d_z8, w_ffn_in, (t, d), F32, (nm, 1, N_DEV),
                   pl.BlockSpec((None, tm, FF_SHARD), lambda i, j, k: (k, i, 0)),
                   pl.BlockSpec((None, d, FF_SHARD), lambda i, j, k: (k, 0, 0)), row_tile, NT, (tm, d))
    g_w_ffn_in = _mm("g_w_ffn_in", x1b, d_z8, (N_DEV, d, FF_SHARD), F32, (N_DEV, 1, nk),
                     pl.BlockSpec((tk, d), lambda i, j, k: (k, 0)),
                     pl.BlockSpec((None, tk, FF_SHARD), lambda i, j, k: (i, k, 0)),
                     pl.BlockSpec((None, d, FF_SHARD), lambda i, j, k: (i, 0, 0)), TN, (d, FF_SHARD))
    d_r1, sums_ln1 = _ln1_bwd(d_r2, d_x1_ffn, r1, small["ln1_g"])
    d_mixed = _mm("d_mixed", d_r1, w_out, (t, d), F32, (nm, 1, 1), row_tile, sq_w, row_tile, NT, (tm, d))
    k_tile = pl.BlockSpec((tk, d), lambda i, j, k: (k, 0))
    g_w_out = _mm("g_w_out", mixed, d_r1, (d, d), F32, (1, 1, nk), k_tile, k_tile, sq_w, TN, (d, d))
    d_y, d_mz = _merge_bwd(d_mixed, proj, y_conv, y_hgrn)
    d_c = _mm("d_c", d_y, w_conv_out, (t, CONV_DIM), F32, (nm, 1, N_DEV),
              pl.BlockSpec((None, tm, 128), lambda i, j, k: (0, i, k)),
              pl.BlockSpec((None, CONV_DIM, 128), lambda i, j, k: (k, 0, 0)),
              pl.BlockSpec((tm, CONV_DIM), lambda i, j, k: (i, 0)), NT, (tm, CONV_DIM))
    g_w_conv_out = _mm("g_w_conv_out", c_act, d_y, (N_DEV, CONV_DIM, 128), F32, (N_DEV, 1, nk),
                       pl.BlockSpec((tk, CONV_DIM), lambda i, j, k: (k, 0)),
                       pl.BlockSpec((None, tk, 128), lambda i, j, k: (0, k, i)),
                       pl.BlockSpec((None, CONV_DIM, 128), lambda i, j, k: (i, 0, 0)), TN, (CONV_DIM, 128))
    d_og = _mm("d_og", d_y, w_hgrn_out, (t, d), F32, (nm, 1, 1),
               pl.BlockSpec((None, tm, d), lambda i, j, k: (1, i, 0)), sq_w, row_tile, NT, (tm, d))
    g_w_hgrn_out = _mm("g_w_hgrn_out", og, d_y, (d, d), F32, (1, 1, nk), k_tile,
                       pl.BlockSpec((None, tk, d), lambda i, j, k: (1, k, 0)), sq_w, TN, (d, d))
    d_pre, sums_conv = _conv_bwd_norm(d_c, conv_pre, small["conv_ln_g"], small["conv_ln_b"])
    d_cproj, g_w_conv_dw = _conv_bwd_dw(d_pre, proj, small["w_conv_dw"])
    d_qz, d_fz, d_iv, d_gz, sums_hgrn = _hgrn_bwd(d_og, o, proj, states, small["hgrn_lb_logits"], small["hgrn_norm_g"])
    d_proj = jnp.concatenate([d_cproj, d_qz, d_fz, d_iv, d_gz, d_mz], axis=1)
    d_x_proj = _mm("d_x_proj", d_proj, w_in, (t, d), F32, (nm, 1, N_DEV),
                   pl.BlockSpec((tm, IN_SHARD), lambda i, j, k: (i, k)),
                   pl.BlockSpec((None, d, IN_SHARD), lambda i, j, k: (k, 0, 0)), row_tile, NT, (tm, d))
    g_w_in = _mm("g_w_in", x, d_proj, (N_DEV, d, IN_SHARD), F32, (N_DEV, 1, nk), k_tile,
                 pl.BlockSpec((tk, IN_SHARD), lambda i, j, k: (k, i)),
                 pl.BlockSpec((None, d, IN_SHARD), lambda i, j, k: (i, 0, 0)), TN, (d, IN_SHARD))
    grad_x = _grad_x(d_r1, d_x_proj)

    d_l0 = sums_hgrn[1:2]
    small_grads = {
        "loss": sums_ln2[2:3, 0:128],
        "b_conv_dw": sums_conv[2:3], "conv_ln_g": sums_conv[0:1], "conv_ln_b": sums_conv[1:2],
        "hgrn_lb_logits": jnp.concatenate([d_l0, -d_l0], axis=1),
        "hgrn_norm_g": sums_hgrn[0:1],
        "ln1_g": sums_ln1[0:1], "ln1_b": sums_ln1[1:2],
        "b_ffn_dw": sums_ffn[:, FFN_K, :].reshape(1, D_FF),
        "ln2_g": sums_ln2[0:1], "ln2_b": sums_ln2[1:2],
        "w_conv_dw": g_w_conv_dw[0:CONV_K].reshape(1, CONV_K * CONV_DIM),
        "w_ffn_dw": jnp.transpose(sums_ffn[:, 0:FFN_K, :], (1, 0, 2)).reshape(1, FFN_K * D_FF),
    }
    large_grads = (g_w_in, g_w_conv_out, g_w_hgrn_out, g_w_out, g_w_ffn_in, g_w_ffn_out)
    return grad_x, large_grads, small_grads


def _coords():
    return lax.axis_index("x"), lax.axis_index("y"), lax.axis_index("c")


def _all_gather(shards):
    n = len(shards)

    def body(*refs):
        ins, outs = refs[:n], refs[n:2 * n]
        send_sems, recv_sems, local_sems = refs[2 * n:]
        x, y, c = _coords()
        me = 4 * x + 2 * y + c
        sibling = (x, y, 1 - c)
        chips = [(1 - x, y), (x, 1 - y), (1 - x, 1 - y)]

        def copy(a, k, block, to, src=None):
            return pltpu.make_async_remote_copy(
                src_ref=outs[a].at[block] if src is None else src, dst_ref=outs[a].at[block],
                send_sem=send_sems.at[a, k], recv_sem=recv_sems.at[a, k], device_id=to, device_id_type=MESH)

        local = [pltpu.make_async_copy(ins[a], outs[a].at[me], local_sems.at[a]) for a in range(n)]
        for cp in local:
            cp.start()
        first = []
        for a in range(n):
            first.append(copy(a, 0, me, sibling, src=ins[a]))
            for j, chip in enumerate(chips):
                first.append(copy(a, 1 + j, me, (*chip, c), src=ins[a]))
        for cp in first:
            cp.start()
        passed = []
        for j, (px, py) in enumerate(chips):
            for a in range(n):
                copy(a, 1 + j, 4 * px + 2 * py + c, sibling).wait_recv()
                cp = copy(a, 4 + j, 4 * px + 2 * py + c, sibling)
                cp.start()
                passed.append(cp)
        for a in range(n):
            copy(a, 0, 4 * x + 2 * y + 1 - c, sibling).wait_recv()
            for j, (px, py) in enumerate(chips):
                copy(a, 4 + j, 4 * px + 2 * py + 1 - c, sibling).wait_recv()
        for cp in first + passed:
            cp.wait_send()
        for cp in local:
            cp.wait()

    return pl.pallas_call(
        body, name="all_gather_weights",
        in_specs=[ANY] * n, out_specs=[ANY] * n,
        out_shape=[jax.ShapeDtypeStruct((N_DEV,) + s.shape, s.dtype) for s in shards],
        scratch_shapes=[pltpu.SemaphoreType.DMA((n, 7)), pltpu.SemaphoreType.DMA((n, 7)), pltpu.SemaphoreType.DMA((n,))],
        compiler_params=pltpu.CompilerParams(has_side_effects=True),
    )(*shards)


def _pair_exchange(grads):
    n = len(grads)

    def body(*refs):
        ins, outs = refs[:n], refs[n:2 * n]
        send_sems, recv_sems = refs[2 * n:]
        x, y, c = _coords()
        copies = [pltpu.make_async_remote_copy(
            src_ref=ins[a].at[:, 1 - c], dst_ref=outs[a], send_sem=send_sems.at[a], recv_sem=recv_sems.at[a],
            device_id=(x, y, 1 - c), device_id_type=MESH) for a in range(n)]
        for cp in copies:
            cp.start()
        for cp in copies:
            cp.wait()

    return pl.pallas_call(
        body, name="grad_pair_exchange",
        in_specs=[ANY] * n, out_specs=[ANY] * n,
        out_shape=[jax.ShapeDtypeStruct((4,) + g.shape[2:], g.dtype) for g in grads],
        scratch_shapes=[pltpu.SemaphoreType.DMA((n,)), pltpu.SemaphoreType.DMA((n,))],
        compiler_params=pltpu.CompilerParams(has_side_effects=True),
    )(*grads)


def _chip_exchange(parts):
    n = len(parts)

    def body(*refs):
        ins, outs = refs[:n], refs[n:2 * n]
        send_sems, recv_sems = refs[2 * n:]
        x, y, c = _coords()
        chips = [(1 - x, y), (x, 1 - y), (1 - x, 1 - y)]
        copies = []
        for a in range(n):
            for j, (px, py) in enumerate(chips):
                copies.append(pltpu.make_async_remote_copy(
                    src_ref=ins[a].at[2 * px + py], dst_ref=outs[a].at[j],
                    send_sem=send_sems.at[a, j], recv_sem=recv_sems.at[a, j],
                    device_id=(px, py, c), device_id_type=MESH))
        for cp in copies:
            cp.start()
        for cp in copies:
            cp.wait()

    return pl.pallas_call(
        body, name="grad_chip_exchange",
        in_specs=[ANY] * n, out_specs=[ANY] * n,
        out_shape=[jax.ShapeDtypeStruct((3,) + p.shape[1:], p.dtype) for p in parts],
        scratch_shapes=[pltpu.SemaphoreType.DMA((n, 3)), pltpu.SemaphoreType.DMA((n, 3))],
        compiler_params=pltpu.CompilerParams(has_side_effects=True),
    )(*parts)


def _row_tile(rows):
    return 256 if rows % 256 == 0 else rows


def _pair_sum(name, grad, recv, core):
    _, _, rows, cols = grad.shape
    tr = _row_tile(rows)

    def body(core_ref, g_ref, r_ref, o_ref):
        o_ref[...] = g_ref[...] + r_ref[...]

    return pl.pallas_call(
        body, name=name,
        grid_spec=pltpu.PrefetchScalarGridSpec(
            num_scalar_prefetch=1, grid=(4, rows // tr),
            in_specs=[pl.BlockSpec((None, None, tr, cols), lambda q, r, core_ref: (q, core_ref[0], r, 0)),
                      pl.BlockSpec((None, tr, cols), lambda q, r, core_ref: (q, r, 0))],
            out_specs=pl.BlockSpec((None, tr, cols), lambda q, r, core_ref: (q, r, 0))),
        out_shape=jax.ShapeDtypeStruct((4, rows, cols), F32),
        compiler_params=_params(("parallel", "parallel")),
    )(core, grad, recv)


def _adam_math(w, g, m, v):
    m_new = ADAM_B1 * m + (1.0 - ADAM_B1) * g
    v_new = ADAM_B2 * v + (1.0 - ADAM_B2) * (g * g)
    m_hat = m_new / (1.0 - ADAM_B1 ** ADAM_STEP)
    v_hat = v_new / (1.0 - ADAM_B2 ** ADAM_STEP)
    delta = -ADAM_LR * (m_hat / (jnp.sqrt(v_hat) + ADAM_EPS) + ADAM_WD * w)
    return delta, m_new, v_new


def _adam_large(name, part, recv, chip, w, m, v):
    rows, cols = w.shape
    tr = _row_tile(rows)

    def body(chip_ref, p_ref, r_ref, w_ref, m_ref, v_ref, g_out, d_out, m_out, v_out):
        g = ((p_ref[...] + r_ref[0]) + r_ref[1]) + r_ref[2]
        delta, m_new, v_new = _adam_math(w_ref[...], g, m_ref[...], v_ref[...])
        g_out[...] = g
        d_out[...] = delta
        m_out[...] = m_new
        v_out[...] = v_new

    tile = pl.BlockSpec((tr, cols), lambda r, chip_ref: (r, 0))
    sds = jax.ShapeDtypeStruct((rows, cols), F32)
    return pl.pallas_call(
        body, name=name,
        grid_spec=pltpu.PrefetchScalarGridSpec(
            num_scalar_prefetch=1, grid=(rows // tr,),
            in_specs=[pl.BlockSpec((None, tr, cols), lambda r, chip_ref: (chip_ref[0], r, 0)),
                      pl.BlockSpec((3, tr, cols), lambda r, chip_ref: (0, r, 0)), tile, tile, tile],
            out_specs=[tile, tile, tile, tile]),
        out_shape=[sds, sds, sds, sds],
        compiler_params=_params(("parallel",)),
    )(chip, part, recv, w, m, v)


def _small_allreduce(vec):
    rows = vec.shape[0]

    def body(v_ref, o_ref, gat_ref, send_sems, recv_sems):
        x, y, c = _coords()
        me = 4 * x + 2 * y + c
        gat_ref[me] = v_ref[...]
        copies = []
        for k in range(1, N_DEV):
            px, py, pc = x ^ (k >> 2), y ^ ((k >> 1) & 1), c ^ (k & 1)
            copies.append(pltpu.make_async_remote_copy(
                src_ref=v_ref, dst_ref=gat_ref.at[me], send_sem=send_sems.at[k - 1], recv_sem=recv_sems.at[k - 1],
                device_id=(px, py, pc), device_id_type=MESH))
        for cp in copies:
            cp.start()
        for k in range(1, N_DEV):
            px, py, pc = x ^ (k >> 2), y ^ ((k >> 1) & 1), c ^ (k & 1)
            pltpu.make_async_remote_copy(
                src_ref=v_ref, dst_ref=gat_ref.at[4 * px + 2 * py + pc], send_sem=send_sems.at[k - 1],
                recv_sem=recv_sems.at[k - 1], device_id=(px, py, pc), device_id_type=MESH).wait_recv()
        for cp in copies:
            cp.wait_send()
        acc = gat_ref[0]
        for dev in range(1, N_DEV):
            acc = acc + gat_ref[dev]
        o_ref[...] = acc

    whole = pl.BlockSpec(memory_space=pltpu.VMEM)
    return pl.pallas_call(
        body, name="small_allreduce", in_specs=[whole], out_specs=whole,
        out_shape=jax.ShapeDtypeStruct((rows, 128), F32),
        scratch_shapes=[pltpu.VMEM((N_DEV, rows, 128), F32), pltpu.SemaphoreType.DMA((N_DEV - 1,)),
                        pltpu.SemaphoreType.DMA((N_DEV - 1,))],
        compiler_params=pltpu.CompilerParams(has_side_effects=True, vmem_limit_bytes=VMEM_LIMIT),
    )(vec)


def _adam_small(w, g, m, v):
    def body(w_ref, g_ref, m_ref, v_ref, d_out, m_out, v_out):
        delta, m_new, v_new = _adam_math(w_ref[...], g_ref[...], m_ref[...], v_ref[...])
        d_out[...] = delta
        m_out[...] = m_new
        v_out[...] = v_new

    whole = pl.BlockSpec(memory_space=pltpu.VMEM)
    sds = jax.ShapeDtypeStruct(w.shape, F32)
    return pl.pallas_call(body, name="adam_small", in_specs=[whole] * 4, out_specs=[whole] * 3,
                          out_shape=[sds, sds, sds])(w, g, m, v)


_SMALL_ORDER = ["loss", "b_conv_dw", "conv_ln_g", "conv_ln_b", "hgrn_lb_logits", "hgrn_norm_g", "ln1_g", "ln1_b",
                "b_ffn_dw", "ln2_g", "ln2_b", "w_conv_dw", "w_ffn_dw"]
_WEIGHTS = ["w_in", "w_conv_dw", "b_conv_dw", "conv_ln_g", "conv_ln_b", "w_conv_out", "hgrn_lb_logits", "hgrn_norm_g",
            "w_hgrn_out", "w_out", "ln1_g", "ln1_b", "w_ffn_in", "w_ffn_dw", "b_ffn_dw", "w_ffn_out", "ln2_g", "ln2_b"]
_LARGE = ["w_in", "w_conv_out", "w_hgrn_out", "w_out", "w_ffn_in", "w_ffn_out"]
_CONV_DW_SHARD = CONV_DIM // N_DEV
_FFN_DW_SHARD = D_FF // N_DEV


def kernel(x, w_in, w_conv_dw, b_conv_dw, conv_ln_g, conv_ln_b, w_conv_out, hgrn_lb_logits, hgrn_norm_g, w_hgrn_out, w_out, ln1_g, ln1_b, w_ffn_in, w_ffn_dw, b_ffn_dw, w_ffn_out, ln2_g, ln2_b, loss_target, m_w_in, m_w_conv_dw, m_b_conv_dw, m_conv_ln_g, m_conv_ln_b, m_w_conv_out, m_hgrn_lb_logits, m_hgrn_norm_g, m_w_hgrn_out, m_w_out, m_ln1_g, m_ln1_b, m_w_ffn_in, m_w_ffn_dw, m_b_ffn_dw, m_w_ffn_out, m_ln2_g, m_ln2_b, v_w_in, v_w_conv_dw, v_b_conv_dw, v_conv_ln_g, v_conv_ln_b, v_w_conv_out, v_hgrn_lb_logits, v_hgrn_norm_g, v_w_hgrn_out, v_w_out, v_ln1_g, v_ln1_b, v_w_ffn_in, v_w_ffn_dw, v_b_ffn_dw, v_w_ffn_out, v_ln2_g, v_ln2_b):
    w = dict(w_in=w_in, w_conv_dw=w_conv_dw, b_conv_dw=b_conv_dw, conv_ln_g=conv_ln_g, conv_ln_b=conv_ln_b,
             w_conv_out=w_conv_out, hgrn_lb_logits=hgrn_lb_logits, hgrn_norm_g=hgrn_norm_g, w_hgrn_out=w_hgrn_out,
             w_out=w_out, ln1_g=ln1_g, ln1_b=ln1_b, w_ffn_in=w_ffn_in, w_ffn_dw=w_ffn_dw, b_ffn_dw=b_ffn_dw,
             w_ffn_out=w_ffn_out, ln2_g=ln2_g, ln2_b=ln2_b)
    m = dict(w_in=m_w_in, w_conv_dw=m_w_conv_dw, b_conv_dw=m_b_conv_dw, conv_ln_g=m_conv_ln_g, conv_ln_b=m_conv_ln_b,
             w_conv_out=m_w_conv_out, hgrn_lb_logits=m_hgrn_lb_logits, hgrn_norm_g=m_hgrn_norm_g,
             w_hgrn_out=m_w_hgrn_out, w_out=m_w_out, ln1_g=m_ln1_g, ln1_b=m_ln1_b, w_ffn_in=m_w_ffn_in,
             w_ffn_dw=m_w_ffn_dw, b_ffn_dw=m_b_ffn_dw, w_ffn_out=m_w_ffn_out, ln2_g=m_ln2_g, ln2_b=m_ln2_b)
    v = dict(w_in=v_w_in, w_conv_dw=v_w_conv_dw, b_conv_dw=v_b_conv_dw, conv_ln_g=v_conv_ln_g, conv_ln_b=v_conv_ln_b,
             w_conv_out=v_w_conv_out, hgrn_lb_logits=v_hgrn_lb_logits, hgrn_norm_g=v_hgrn_norm_g,
             w_hgrn_out=v_w_hgrn_out, w_out=v_w_out, ln1_g=v_ln1_g, ln1_b=v_ln1_b, w_ffn_in=v_w_ffn_in,
             w_ffn_dw=v_w_ffn_dw, b_ffn_dw=v_b_ffn_dw, w_ffn_out=v_w_ffn_out, ln2_g=v_ln2_g, ln2_b=v_ln2_b)
    xi, yi, ci = lax.axis_index("x"), lax.axis_index("y"), lax.axis_index("c")
    me = 4 * xi + 2 * yi + ci
    core = jnp.reshape(ci, (1,)).astype(jnp.int32)
    chip = jnp.reshape(2 * xi + yi, (1,)).astype(jnp.int32)

    shards = [w[name][0].astype(BF16) for name in _LARGE]
    shards.append(jnp.pad(w_conv_dw[0], ((0, 1), (0, 128 - _CONV_DW_SHARD))))
    shards.append(jnp.pad(w_ffn_dw[0], ((0, 8 - FFN_K), (0, 384 - _FFN_DW_SHARD))))
    g_in, g_conv_out, g_hgrn_out, g_out, g_ffn_in, g_ffn_out, g_conv_dw, g_ffn_dw = _all_gather(shards)
    conv_dw_full = jnp.transpose(g_conv_dw[:, :CONV_K, :_CONV_DW_SHARD], (1, 0, 2)).reshape(CONV_K, CONV_DIM)
    ffn_dw_full = jnp.transpose(g_ffn_dw[:, :FFN_K, :_FFN_DW_SHARD], (1, 0, 2)).reshape(FFN_K, 4, FF_SHARD)
    small = dict(w_conv_dw=conv_dw_full, b_conv_dw=b_conv_dw, conv_ln_g=conv_ln_g, conv_ln_b=conv_ln_b,
                 hgrn_lb_logits=hgrn_lb_logits, hgrn_norm_g=hgrn_norm_g, ln1_g=ln1_g, ln1_b=ln1_b, ln2_g=ln2_g,
                 ln2_b=ln2_b, w_ffn_dw=jnp.transpose(ffn_dw_full, (1, 0, 2)), b_ffn_dw=b_ffn_dw.reshape(4, 1, FF_SHARD))

    grad_x, large_grads, small_grads = _local_step(
        x[0], loss_target[0], g_in, g_conv_out, g_hgrn_out.reshape(D_MODEL, D_MODEL), g_out.reshape(D_MODEL, D_MODEL),
        g_ffn_in, g_ffn_out.reshape(4, FF_SHARD, D_MODEL), small)

    shard_shapes = [w[name].shape[1:] for name in _LARGE]
    views = [g.reshape((4, 2) + s) for g, s in zip(large_grads, shard_shapes)]
    from_sibling = _pair_exchange(views)
    parts = [_pair_sum("pair_sum_" + name, g, r, core) for name, g, r in zip(_LARGE, views, from_sibling)]
    from_chips = _chip_exchange(parts)
    out = {}
    for name, p, r in zip(_LARGE, parts, from_chips):
        out[name] = _adam_large("adam_" + name, p, r, chip, w[name][0], m[name][0], v[name][0])

    vec = jnp.concatenate([small_grads[name] for name in _SMALL_ORDER], axis=1)
    total = _small_allreduce(vec.reshape(-1, 128)).reshape(1, -1)
    sizes = [small_grads[name].shape[1] for name in _SMALL_ORDER]
    offs = [0]
    for s in sizes:
        offs.append(offs[-1] + s)
    summed = {name: total[:, offs[i]:offs[i + 1]] for i, name in enumerate(_SMALL_ORDER)}
    loss = summed["loss"][0, 0]
    conv_dw_g = lax.dynamic_slice_in_dim(summed["w_conv_dw"].reshape(CONV_K, CONV_DIM), me * _CONV_DW_SHARD, _CONV_DW_SHARD, axis=1)
    ffn_dw_g = lax.dynamic_slice_in_dim(summed["w_ffn_dw"].reshape(FFN_K, D_FF), me * _FFN_DW_SHARD, _FFN_DW_SHARD, axis=1)
    small_g = dict(summed, w_conv_dw=conv_dw_g.reshape(1, -1), w_ffn_dw=ffn_dw_g.reshape(1, -1))
    names = [n for n in _SMALL_ORDER if n != "loss"]
    flat = lambda d, n: d[n].reshape(1, -1)
    n_small = sum(small_g[n].shape[1] for n in names)
    pad = (-n_small) % 1024
    pack = lambda pieces: jnp.pad(jnp.concatenate(pieces, axis=1), ((0, 0), (0, pad))).reshape(-1, 128)
    d_s, m_s, v_s = _adam_small(pack([flat(w, n) for n in names]), pack([small_g[n] for n in names]),
                                pack([flat(m, n) for n in names]), pack([flat(v, n) for n in names]))
    pos = 0
    for n in names:
        size = small_g[n].shape[1]
        cut = lambda a: a.reshape(1, -1)[:, pos:pos + size].reshape(w[n].shape)
        out[n] = (small_g[n].reshape(w[n].shape), cut(d_s), cut(m_s), cut(v_s))
        pos += size

    for name in _LARGE:
        out[name] = tuple(a.reshape(w[name].shape) for a in out[name])
    grads = [out[n][0] for n in _WEIGHTS]
    deltas = [out[n][1] for n in _WEIGHTS]
    new_m = [out[n][2] for n in _WEIGHTS]
    new_v = [out[n][3] for n in _WEIGHTS]
    return (loss, grad_x[None], *grads, *deltas, *new_m, *new_v)
```

```python
import functools
import math

import jax
import jax.numpy as jnp
from jax import lax
from jax.experimental import pallas as pl
from jax.experimental.pallas import tpu as pltpu

F32 = jnp.float32
BF16 = jnp.bfloat16

N_DEV = 8
D_MODEL = 1024
CONV_DIM = 512
CONV_K = 31
HGRN_DIM = 1024
HEADS = 8
HEAD_DIM = 128
D_FF = 2816
FFN_K = 3
FF_SHARD = 2 * D_FF // N_DEV
IN_COLS = 7168
IN_SHARD = IN_COLS // N_DEV
LN_EPS = 1e-5
RMS_EPS = 1e-6
ALPHA = 2.0 ** 0.25

ADAM_LR = 0.001
ADAM_B1 = 0.9
ADAM_B2 = 0.999
ADAM_EPS = 1e-08
ADAM_WD = 0.01
ADAM_STEP = 10

CHUNK = 64
CHUNKS_PER_BLOCK = 4
CONV_HALO = 32
FFN_HALO = 8
ROW_BLOCK = 64
VMEM_LIMIT = 48 * 1024 * 1024

MESH = pl.DeviceIdType.MESH
ANY = pl.BlockSpec(memory_space=pl.ANY)

NN = (((1,), (0,)), ((), ()))
NT = (((1,), (1,)), ((), ()))
TN = (((0,), (0,)), ((), ()))


def _params(sem):
    return pltpu.CompilerParams(dimension_semantics=sem, vmem_limit_bytes=VMEM_LIMIT)


def _dot(a, b, dims):
    return lax.dot_general(a.astype(BF16), b.astype(BF16), dims, preferred_element_type=F32)


def _dot_f32(a, b, dims):
    return lax.dot_general(a, b, dims, preferred_element_type=F32, precision=lax.Precision.HIGHEST)


def _sigmoid(x):
    return jax.nn.sigmoid(x)


def _ln(r):
    mu = jnp.mean(r, axis=-1, keepdims=True)
    xc = r - mu
    var = jnp.mean(xc * xc, axis=-1, keepdims=True)
    rstd = lax.rsqrt(var + LN_EPS)
    return xc * rstd, rstd


def _ln_bwd(dy, xhat, rstd, g):
    dxh = dy * g
    m1 = jnp.mean(dxh, axis=-1, keepdims=True)
    m2 = jnp.mean(dxh * xhat, axis=-1, keepdims=True)
    return rstd * (dxh - m1 - xhat * m2)


def _colsum(x):
    return jnp.sum(x, axis=0, keepdims=True)


class _Hosted:
    def __init__(self, inputs, out_shapes, sem_shapes, start, finish):
        self.inputs, self.out_shapes, self.sem_shapes = list(inputs), list(out_shapes), list(sem_shapes)
        self.start, self.finish = start, finish


def _call(body, *, name, grid, in_specs, out_specs, out_shape, scratch_shapes, semantics, operands, hosted=None):
    if hosted is None:
        return pl.pallas_call(
            body, name=name, grid=grid, in_specs=list(in_specs), out_specs=list(out_specs), out_shape=list(out_shape),
            scratch_shapes=list(scratch_shapes), compiler_params=_params(semantics))(*operands)
    n_in, n_out, n_scr = len(in_specs), len(out_specs), len(scratch_shapes)
    h_in, h_out = len(hosted.inputs), len(hosted.out_shapes)

    def full_body(*refs):
        ins, refs = refs[:n_in], refs[n_in:]
        h_ins, refs = refs[:h_in], refs[h_in:]
        outs, refs = refs[:n_out], refs[n_out:]
        h_outs, refs = refs[:h_out], refs[h_out:]
        scr, sems = refs[:n_scr], refs[n_scr:]
        first = functools.reduce(jnp.logical_and, [pl.program_id(d) == 0 for d in range(len(grid))])
        last = functools.reduce(jnp.logical_and, [pl.program_id(d) == grid[d] - 1 for d in range(len(grid))])

        @pl.when(first)
        def _():
            hosted.start(h_ins, h_outs, sems)

        body(*ins, *outs, *scr)

        @pl.when(last)
        def _():
            hosted.finish(h_ins, h_outs, sems)

    return pl.pallas_call(
        full_body, name=name, grid=grid, in_specs=list(in_specs) + [ANY] * h_in,
        out_specs=list(out_specs) + [ANY] * h_out, out_shape=list(out_shape) + hosted.out_shapes,
        scratch_shapes=list(scratch_shapes) + hosted.sem_shapes,
        compiler_params=pltpu.CompilerParams(dimension_semantics=("arbitrary",) * len(grid),
                                             vmem_limit_bytes=VMEM_LIMIT, has_side_effects=True),
    )(*operands, *hosted.inputs)


def _run_hosted(name, hosted):
    h_in, h_out = len(hosted.inputs), len(hosted.out_shapes)

    def body(*refs):
        ins, outs, sems = refs[:h_in], refs[h_in:h_in + h_out], refs[h_in + h_out:]
        hosted.start(ins, outs, sems)
        hosted.finish(ins, outs, sems)

    return pl.pallas_call(
        body, name=name, in_specs=[ANY] * h_in, out_specs=[ANY] * h_out, out_shape=hosted.out_shapes,
        scratch_shapes=hosted.sem_shapes, compiler_params=pltpu.CompilerParams(has_side_effects=True),
    )(*hosted.inputs)


def _mm(name, a, b, out_shape, out_dtype, grid, a_spec, b_spec, o_spec, dims, acc_shape, hosted=None):
    nk = grid[2]
    if nk == 1:
        def body(a_ref, b_ref, o_ref):
            o_ref[...] = _dot(a_ref[...], b_ref[...], dims).astype(o_ref.dtype)
        scratch = []
    else:
        def body(a_ref, b_ref, o_ref, acc_ref):
            k = pl.program_id(2)

            @pl.when(k == 0)
            def _():
                acc_ref[...] = jnp.zeros_like(acc_ref)

            acc_ref[...] += _dot(a_ref[...], b_ref[...], dims)

            @pl.when(k == nk - 1)
            def _():
                o_ref[...] = acc_ref[...].astype(o_ref.dtype)
        scratch = [pltpu.VMEM(acc_shape, F32)]

    outs = _call(body, name=name, grid=grid, in_specs=[a_spec, b_spec], out_specs=[o_spec],
                 out_shape=[jax.ShapeDtypeStruct(out_shape, out_dtype)], scratch_shapes=scratch,
                 semantics=("parallel", "parallel", "arbitrary"), operands=(a, b), hosted=hosted)
    return outs[0] if hosted is None else (outs[0], list(outs[1:]))


def _pick(t, pref):
    return pref if t % pref == 0 else t


def _glu(p):
    return p[:, :CONV_DIM] * _sigmoid(p[:, CONV_DIM:])


def _conv_fwd(proj, w_dw, b_dw, g, b):
    t = proj.shape[0]
    tm = _pick(t, 256)
    nh = tm // CONV_HALO

    def body(p_ref, ph_ref, w_ref, bd_ref, g_ref, b_ref, act_ref, pre_ref, xs_ref):
        i = pl.program_id(0)
        halo = _glu(ph_ref[...])
        xs_ref[0:CONV_HALO, :] = jnp.where(i == 0, 0.0, halo)
        xs_ref[CONV_HALO:CONV_HALO + tm, :] = _glu(p_ref[...])
        for r in range(tm // ROW_BLOCK):
            acc = jnp.broadcast_to(bd_ref[...], (ROW_BLOCK, CONV_DIM))
            for k in range(CONV_K):
                off = r * ROW_BLOCK + CONV_HALO - (CONV_K - 1) + k
                acc = acc + w_ref[k:k + 1, :] * xs_ref[off:off + ROW_BLOCK, :]
            rows = slice(r * ROW_BLOCK, (r + 1) * ROW_BLOCK)
            pre_ref[rows, :] = acc
            xhat, _ = _ln(acc)
            yln = xhat * g_ref[...] + b_ref[...]
            act_ref[rows, :] = (yln * _sigmoid(yln)).astype(BF16)

    full = lambda s: pl.BlockSpec(s, lambda i: (0, 0))
    return pl.pallas_call(
        body, name="conv_fwd", grid=(t // tm,),
        in_specs=[pl.BlockSpec((tm, 2 * CONV_DIM), lambda i: (i, 0)),
                  pl.BlockSpec((CONV_HALO, 2 * CONV_DIM), lambda i: (jnp.maximum(i * nh - 1, 0), 0)),
                  full((CONV_K, CONV_DIM)), full((1, CONV_DIM)), full((1, CONV_DIM)), full((1, CONV_DIM))],
        out_specs=[pl.BlockSpec((tm, CONV_DIM), lambda i: (i, 0)), pl.BlockSpec((tm, CONV_DIM), lambda i: (i, 0))],
        out_shape=[jax.ShapeDtypeStruct((t, CONV_DIM), BF16), jax.ShapeDtypeStruct((t, CONV_DIM), F32)],
        scratch_shapes=[pltpu.VMEM((CONV_HALO + tm, CONV_DIM), F32)],
        compiler_params=_params(("arbitrary",)),
    )(proj, proj, w_dw, b_dw, g, b)


def _conv_bwd_norm(d_c, pre, g, b):
    t = d_c.shape[0]
    tm = _pick(t, 256)
    nt = t // tm

    def body(dc_ref, pre_ref, g_ref, b_ref, dpre_ref, sums_ref):
        i = pl.program_id(0)

        @pl.when(i == 0)
        def _():
            sums_ref[...] = jnp.zeros_like(sums_ref)

        xhat, rstd = _ln(pre_ref[...])
        yln = xhat * g_ref[...] + b_ref[...]
        sg = _sigmoid(yln)
        dyln = dc_ref[...] * (sg * (1.0 + yln * (1.0 - sg)))
        dpre = _ln_bwd(dyln, xhat, rstd, g_ref[...])
        dpre_ref[...] = dpre
        sums_ref[0:1, :] += _colsum(dyln * xhat)
        sums_ref[1:2, :] += _colsum(dyln)
        sums_ref[2:3, :] += _colsum(dpre)

    full = lambda s: pl.BlockSpec(s, lambda i: (0, 0))
    tile = pl.BlockSpec((tm, CONV_DIM), lambda i: (i, 0))
    return pl.pallas_call(
        body, name="conv_bwd_norm", grid=(nt,),
        in_specs=[tile, tile, full((1, CONV_DIM)), full((1, CONV_DIM))],
        out_specs=[tile, full((8, CONV_DIM))],
        out_shape=[jax.ShapeDtypeStruct((t, CONV_DIM), F32), jax.ShapeDtypeStruct((8, CONV_DIM), F32)],
        compiler_params=_params(("arbitrary",)),
    )(d_c, pre, g, b)


def _conv_bwd_dw(d_pre, proj, w_dw):
    t = d_pre.shape[0]
    tm = _pick(t, 256)
    nt = t // tm
    nh = tm // CONV_HALO
    last_h = t // CONV_HALO - 1

    def body(dp_ref, dph_ref, p_ref, ph_ref, w_ref, dproj_ref, dw_ref, xs_ref, ds_ref):
        i = pl.program_id(0)

        @pl.when(i == 0)
        def _():
            dw_ref[...] = jnp.zeros_like(dw_ref)

        halo = _glu(ph_ref[...])
        xs_ref[0:CONV_HALO, :] = jnp.where(i == 0, 0.0, halo)
        xs_ref[CONV_HALO:CONV_HALO + tm, :] = _glu(p_ref[...])
        ds_ref[0:tm, :] = dp_ref[...]
        ds_ref[tm:tm + CONV_HALO, :] = jnp.where(i == nt - 1, 0.0, dph_ref[...])
        for r in range(tm // ROW_BLOCK):
            base = r * ROW_BLOCK
            dpre = ds_ref[base:base + ROW_BLOCK, :]
            acc = jnp.zeros((ROW_BLOCK, CONV_DIM), F32)
            for k in range(CONV_K):
                acc = acc + w_ref[k:k + 1, :] * ds_ref[base + CONV_K - 1 - k:base + CONV_K - 1 - k + ROW_BLOCK, :]
                off = base + CONV_HALO - (CONV_K - 1) + k
                dw_ref[k:k + 1, :] += _colsum(dpre * xs_ref[off:off + ROW_BLOCK, :])
            rows = slice(base, base + ROW_BLOCK)
            cval = p_ref[rows, 0:CONV_DIM]
            sg = _sigmoid(p_ref[rows, CONV_DIM:2 * CONV_DIM])
            dproj_ref[rows, 0:CONV_DIM] = (acc * sg).astype(BF16)
            dproj_ref[rows, CONV_DIM:2 * CONV_DIM] = (acc * cval * sg * (1.0 - sg)).astype(BF16)

    full = lambda s: pl.BlockSpec(s, lambda i: (0, 0))
    return pl.pallas_call(
        body, name="conv_bwd_dw", grid=(nt,),
        in_specs=[pl.BlockSpec((tm, CONV_DIM), lambda i: (i, 0)),
                  pl.BlockSpec((CONV_HALO, CONV_DIM), lambda i: (jnp.minimum((i + 1) * nh, last_h), 0)),
                  pl.BlockSpec((tm, 2 * CONV_DIM), lambda i: (i, 0)),
                  pl.BlockSpec((CONV_HALO, 2 * CONV_DIM), lambda i: (jnp.maximum(i * nh - 1, 0), 0)),
                  full((CONV_K, CONV_DIM))],
        out_specs=[pl.BlockSpec((tm, 2 * CONV_DIM), lambda i: (i, 0)), full((CONV_HALO, CONV_DIM))],
        out_shape=[jax.ShapeDtypeStruct((t, 2 * CONV_DIM), BF16), jax.ShapeDtypeStruct((CONV_HALO, CONV_DIM), F32)],
        scratch_shapes=[pltpu.VMEM((CONV_HALO + tm, CONV_DIM), F32), pltpu.VMEM((tm + CONV_HALO, CONV_DIM), F32)],
        compiler_params=_params(("arbitrary",)),
    )(d_pre, d_pre, proj, proj, w_dw)


def _lower_bound(logit_ref):
    l0 = logit_ref[0:1, :]
    l1 = logit_ref[1:2, :]
    m = jnp.maximum(l0, l1)
    e0 = jnp.exp(l0 - m)
    e1 = jnp.exp(l1 - m)
    return e0 / (e0 + e1)


def _tri(lower):
    r = lax.broadcasted_iota(jnp.int32, (CHUNK, CHUNK), 0)
    c = lax.broadcasted_iota(jnp.int32, (CHUNK, CHUNK), 1)
    return (c <= r) if lower else (c >= r)


def _hgrn_gates(fz, lb):
    s = _sigmoid(fz)
    sn = _sigmoid(-fz)
    f = lb + (1.0 - lb) * s
    return s, sn, f


def _hgrn_fwd(proj, logits, norm_g):
    t = proj.shape[0]
    tm = CHUNK * CHUNKS_PER_BLOCK if t % (CHUNK * CHUNKS_PER_BLOCK) == 0 else CHUNK
    cpb = tm // CHUNK
    nt = t // tm
    half = CHUNK // 2

    def body(qz_ref, fz_ref, iv_ref, gz_ref, lg_ref, ng_ref, o_ref, og_ref, st_ref, state_ref):
        j = pl.program_id(1)

        @pl.when(j == 0)
        def _():
            state_ref[...] = jnp.zeros_like(state_ref)

        lb = _lower_bound(lg_ref)
        causal = _tri(True)
        tri = causal.astype(F32)
        for c in range(cpb):
            rows = slice(c * CHUNK, (c + 1) * CHUNK)
            qz = qz_ref[rows, :]
            q = qz * _sigmoid(qz)
            _, sn, f = _hgrn_gates(fz_ref[rows, :], lb)
            kk = (1.0 - lb) * sn
            v = iv_ref[rows, :]
            bcum = _dot_f32(tri, jnp.log(f), NN)
            bref = bcum[half - 1:half, :]
            blast = bcum[CHUNK - 1:CHUNK, :]
            qe = q * jnp.exp(bcum - bref)
            ke = kk * jnp.exp(bref - bcum)
            a = jnp.where(causal, _dot(qe, ke, NT), 0.0)
            state = state_ref[...]
            st_ref[c] = state.astype(BF16)
            o = _dot(a, v, NN) + _dot(q * jnp.exp(bcum), state, NT)
            state_ref[...] = state * jnp.exp(blast) + _dot(v, kk * jnp.exp(blast - bcum), TN)
            o_ref[rows, :] = o
            r = lax.rsqrt(jnp.mean(o * o, axis=-1, keepdims=True) + RMS_EPS)
            gz = gz_ref[rows, :]
            og_ref[rows, :] = ((o * r * ng_ref[...]) * (gz * _sigmoid(gz))).astype(BF16)

    col = lambda base: pl.BlockSpec((tm, HEAD_DIM), lambda h, j: (j, base + h))
    return pl.pallas_call(
        body, name="hgrn_fwd", grid=(HEADS, nt),
        in_specs=[col(8), col(16), col(24), col(32),
                  pl.BlockSpec((2, HEAD_DIM), lambda h, j: (0, h)), pl.BlockSpec((1, HEAD_DIM), lambda h, j: (0, h))],
        out_specs=[col(0), col(0), pl.BlockSpec((None, cpb, HEAD_DIM, HEAD_DIM), lambda h, j: (h, j, 0, 0))],
        out_shape=[jax.ShapeDtypeStruct((t, HGRN_DIM), F32), jax.ShapeDtypeStruct((t, HGRN_DIM), BF16),
                   jax.ShapeDtypeStruct((HEADS, t // CHUNK, HEAD_DIM, HEAD_DIM), BF16)],
        scratch_shapes=[pltpu.VMEM((HEAD_DIM, HEAD_DIM), F32)],
        compiler_params=_params(("parallel", "arbitrary")),
    )(proj, proj, proj, proj, logits, norm_g)


def _hgrn_bwd(d_og, o, proj, states, logits, norm_g, hosted=None):
    t = proj.shape[0]
    tm = CHUNK * CHUNKS_PER_BLOCK if t % (CHUNK * CHUNKS_PER_BLOCK) == 0 else CHUNK
    cpb = tm // CHUNK
    nt = t // tm
    half = CHUNK // 2

    def body(dog_ref, o_ref, qz_ref, fz_ref, iv_ref, gz_ref, st_ref, lg_ref, ng_ref,
             dqz_ref, dfz_ref, div_ref, dgz_ref, sums_ref, dstate_ref):
        j = pl.program_id(1)

        @pl.when(j == 0)
        def _():
            dstate_ref[...] = jnp.zeros_like(dstate_ref)
            sums_ref[...] = jnp.zeros_like(sums_ref)

        lb = _lower_bound(lg_ref)
        ng = ng_ref[...]
        causal = _tri(True)
        tri = causal.astype(F32)
        triu = _tri(False).astype(F32)
        tril_strict = 1.0 - triu
        for c in reversed(range(cpb)):
            rows = slice(c * CHUNK, (c + 1) * CHUNK)
            qz = qz_ref[rows, :]
            sq = _sigmoid(qz)
            q = qz * sq
            s, sn, f = _hgrn_gates(fz_ref[rows, :], lb)
            kk = (1.0 - lb) * sn
            v = iv_ref[rows, :]
            bcum = _dot_f32(tri, jnp.log(f), NN)
            bref = bcum[half - 1:half, :]
            blast = bcum[CHUNK - 1:CHUNK, :]
            eb = jnp.exp(bcum)
            ebr = jnp.exp(bcum - bref)
            ekr = jnp.exp(bref - bcum)
            ebl = jnp.exp(blast - bcum)
            qe = q * ebr
            ke = kk * ekr
            a = jnp.where(causal, _dot(qe, ke, NT), 0.0)

            ov = o_ref[rows, :]
            r = lax.rsqrt(jnp.mean(ov * ov, axis=-1, keepdims=True) + RMS_EPS)
            on = ov * r
            gz = gz_ref[rows, :]
            sg = _sigmoid(gz)
            dog = dog_ref[rows, :]
            dgz_ref[rows, :] = (dog * (on * ng) * (sg * (1.0 + gz * (1.0 - sg)))).astype(BF16)
            d_ong = dog * (gz * sg)
            sums_ref[0:1, :] += _colsum(d_ong * on)
            d_on = d_ong * ng
            do = r * (d_on - on * jnp.mean(d_on * on, axis=-1, keepdims=True))

            state = st_ref[c]
            dstate = dstate_ref[...]
            dv = _dot(a, do, TN) + _dot(kk * ebl, dstate, NT)
            da = jnp.where(causal, _dot(do, v, NT), 0.0)
            qe_mxu = qe.astype(BF16)
            ke_mxu = ke.astype(BF16)
            dqe = _dot(da, ke_mxu, NN)
            dke = _dot(da, qe_mxu, TN)
            dq_inter = _dot(do, state, NN) * eb
            dk_inter = _dot(v, dstate, NN) * ebl
            dq = dqe * ebr + dq_inter
            dk = dke * ekr + dk_inter
            eblast = jnp.exp(blast)
            dstate_ref[...] = dstate * eblast + _dot(do, q * eb, TN)

            through = eblast * _colsum(dstate * state.astype(F32))
            inside = qe_mxu.astype(F32) * dqe - ke_mxu.astype(F32) * dke
            dlf = (_dot_f32(triu, inside + q * dq_inter, NN) + _dot_f32(tril_strict, kk * dk_inter, NN) + through)
            common = sn * (dlf / f - dk)
            dfz_ref[rows, :] = ((1.0 - lb) * s * common).astype(BF16)
            sums_ref[1:2, :] += _colsum(common)
            dqz_ref[rows, :] = (dq * (sq * (1.0 + qz * (1.0 - sq)))).astype(BF16)
            div_ref[rows, :] = dv.astype(BF16)

        @pl.when(j == nt - 1)
        def _():
            sums_ref[1:2, :] = sums_ref[1:2, :] * lb * (1.0 - lb)

    rev = lambda base: pl.BlockSpec((tm, HEAD_DIM), lambda h, j: (nt - 1 - j, base + h))
    vec = lambda n: pl.BlockSpec((n, HEAD_DIM), lambda h, j: (0, h))
    bf = jax.ShapeDtypeStruct((t, HGRN_DIM), BF16)
    return _call(
        body, name="hgrn_bwd", grid=(HEADS, nt),
        in_specs=[rev(0), rev(0), rev(8), rev(16), rev(24), rev(32),
                  pl.BlockSpec((None, cpb, HEAD_DIM, HEAD_DIM), lambda h, j: (h, nt - 1 - j, 0, 0)),
                  vec(2), vec(1)],
        out_specs=[rev(0), rev(0), rev(0), rev(0), vec(8)],
        out_shape=[bf, bf, bf, bf, jax.ShapeDtypeStruct((8, HGRN_DIM), F32)],
        scratch_shapes=[pltpu.VMEM((HEAD_DIM, HEAD_DIM), F32)],
        semantics=("parallel", "arbitrary"),
        operands=(d_og, o, proj, proj, proj, proj, states, logits, norm_g), hosted=hosted)


def _merge_fwd(proj, y_conv, y_hgrn):
    t = proj.shape[0]
    tm = _pick(t, 512)

    def body(m0_ref, m1_ref, yc_ref, yh_ref, o_ref):
        o_ref[...] = (_sigmoid(m0_ref[...]) * yc_ref[...] + _sigmoid(m1_ref[...]) * yh_ref[...]).astype(BF16)

    tile = pl.BlockSpec((tm, D_MODEL), lambda i: (i, 0))
    return pl.pallas_call(
        body, name="merge_fwd", grid=(t // tm,),
        in_specs=[pl.BlockSpec((tm, D_MODEL), lambda i: (i, 5)), pl.BlockSpec((tm, D_MODEL), lambda i: (i, 6)), tile, tile],
        out_specs=tile, out_shape=jax.ShapeDtypeStruct((t, D_MODEL), BF16),
        compiler_params=_params(("parallel",)),
    )(proj, proj, y_conv, y_hgrn)


def _merge_bwd(d_mixed, proj, y_conv, y_hgrn):
    t = proj.shape[0]
    tm = _pick(t, 512)

    def body(dm_ref, m_ref, yc_ref, yh_ref, dy_ref, dmz_ref):
        br = pl.program_id(1)
        y = jnp.where(br == 0, yc_ref[...], yh_ref[...])
        sg = _sigmoid(m_ref[...])
        dm = dm_ref[...]
        dy_ref[...] = (sg * dm).astype(BF16)
        dmz_ref[...] = (dm * y * sg * (1.0 - sg)).astype(BF16)

    tile = pl.BlockSpec((tm, D_MODEL), lambda i, br: (i, 0))
    return pl.pallas_call(
        body, name="merge_bwd", grid=(t // tm, 2),
        in_specs=[tile, pl.BlockSpec((tm, D_MODEL), lambda i, br: (i, 5 + br)), tile, tile],
        out_specs=[pl.BlockSpec((None, tm, D_MODEL), lambda i, br: (br, i, 0)), pl.BlockSpec((tm, D_MODEL), lambda i, br: (i, br))],
        out_shape=[jax.ShapeDtypeStruct((2, t, D_MODEL), BF16), jax.ShapeDtypeStruct((t, 2 * D_MODEL), BF16)],
        compiler_params=_params(("parallel", "arbitrary")),
    )(d_mixed, proj, y_conv, y_hgrn)


def _ln1_fwd(x, mix, g, b):
    t = x.shape[0]
    tm = _pick(t, 512)

    def body(x_ref, mix_ref, g_ref, b_ref, r_ref, xb_ref):
        r = ALPHA * x_ref[...] + mix_ref[...]
        r_ref[...] = r
        xhat, _ = _ln(r)
        xb_ref[...] = (xhat * g_ref[...] + b_ref[...]).astype(BF16)

    tile = pl.BlockSpec((tm, D_MODEL), lambda i: (i, 0))
    vec = pl.BlockSpec((1, D_MODEL), lambda i: (0, 0))
    return pl.pallas_call(
        body, name="ln1_fwd", grid=(t // tm,), in_specs=[tile, tile, vec, vec], out_specs=[tile, tile],
        out_shape=[jax.ShapeDtypeStruct((t, D_MODEL), F32), jax.ShapeDtypeStruct((t, D_MODEL), BF16)],
        compiler_params=_params(("parallel",)),
    )(x, mix, g, b)


def _ln2_loss_bwd(r1, y_ffn, target, g1, b1, g2, b2):
    t = r1.shape[0]
    tm = _pick(t, 256)
    nt = t // tm

    def body(r1_ref, y_ref, tg_ref, g1_ref, b1_ref, g2_ref, b2_ref, dr_ref, drb_ref, sums_ref, sq_ref):
        i = pl.program_id(0)

        @pl.when(i == 0)
        def _():
            sums_ref[...] = jnp.zeros_like(sums_ref)
            sq_ref[...] = jnp.zeros_like(sq_ref)

        xh1, _ = _ln(r1_ref[...])
        x1 = xh1 * g1_ref[...] + b1_ref[...]
        xh2, rstd2 = _ln(ALPHA * x1 + y_ref[...])
        diff = xh2 * g2_ref[...] + b2_ref[...] - tg_ref[...]
        dy = diff * (1.0 / D_MODEL)
        dr = _ln_bwd(dy, xh2, rstd2, g2_ref[...])
        dr_ref[...] = dr
        drb_ref[...] = dr.astype(BF16)
        sums_ref[0:1, :] += _colsum(dy * xh2)
        sums_ref[1:2, :] += _colsum(dy)
        sq_ref[...] += _colsum(diff * diff)

        @pl.when(i == nt - 1)
        def _():
            total = jnp.sum(sq_ref[...], axis=-1, keepdims=True) * (0.5 / D_MODEL)
            sums_ref[2:3, :] = jnp.broadcast_to(total, (1, D_MODEL))

    tile = pl.BlockSpec((tm, D_MODEL), lambda i: (i, 0))
    vec = pl.BlockSpec((1, D_MODEL), lambda i: (0, 0))
    return pl.pallas_call(
        body, name="ln2_loss_bwd", grid=(nt,), in_specs=[tile, tile, tile, vec, vec, vec, vec],
        out_specs=[tile, tile, pl.BlockSpec((8, D_MODEL), lambda i: (0, 0))],
        out_shape=[jax.ShapeDtypeStruct((t, D_MODEL), F32), jax.ShapeDtypeStruct((t, D_MODEL), BF16),
                   jax.ShapeDtypeStruct((8, D_MODEL), F32)],
        scratch_shapes=[pltpu.VMEM((1, D_MODEL), F32)],
        compiler_params=_params(("arbitrary",)),
    )(r1, y_ffn, target, g1, b1, g2, b2)


def _ln1_bwd(d_r2, d_x1_ffn, r1, g1):
    t = r1.shape[0]
    tm = _pick(t, 256)

    def body(dr2_ref, dx_ref, r1_ref, g_ref, dr1_ref, dr1b_ref, sums_ref):
        i = pl.program_id(0)

        @pl.when(i == 0)
        def _():
            sums_ref[...] = jnp.zeros_like(sums_ref)

        xhat, rstd = _ln(r1_ref[...])
        dx1 = ALPHA * dr2_ref[...] + dx_ref[...]
        dr1 = _ln_bwd(dx1, xhat, rstd, g_ref[...])
        dr1_ref[...] = dr1
        dr1b_ref[...] = dr1.astype(BF16)
        sums_ref[0:1, :] += _colsum(dx1 * xhat)
        sums_ref[1:2, :] += _colsum(dx1)

    tile = pl.BlockSpec((tm, D_MODEL), lambda i: (i, 0))
    vec = pl.BlockSpec((1, D_MODEL), lambda i: (0, 0))
    return pl.pallas_call(
        body, name="ln1_bwd", grid=(t // tm,), in_specs=[tile, tile, tile, vec],
        out_specs=[tile, tile, pl.BlockSpec((8, D_MODEL), lambda i: (0, 0))],
        out_shape=[jax.ShapeDtypeStruct((t, D_MODEL), F32), jax.ShapeDtypeStruct((t, D_MODEL), BF16),
                   jax.ShapeDtypeStruct((8, D_MODEL), F32)],
        compiler_params=_params(("arbitrary",)),
    )(d_r2, d_x1_ffn, r1, g1)


def _cast_bf16(x):
    t = x.shape[0]
    tm = _pick(t, 512)

    def body(x_ref, o_ref):
        o_ref[...] = x_ref[...].astype(BF16)

    tile = pl.BlockSpec((tm, D_MODEL), lambda i: (i, 0))
    return pl.pallas_call(
        body, name="cast_x", grid=(t // tm,), in_specs=[tile], out_specs=tile,
        out_shape=jax.ShapeDtypeStruct((t, D_MODEL), BF16), compiler_params=_params(("parallel",)),
    )(x)


def _relayout(name, a, in_block, in_map, out_block, out_map, out_shape):
    def body(a_ref, o_ref):
        o_ref[...] = a_ref[...].astype(o_ref.dtype)

    return pl.pallas_call(
        body, name=name, grid=(N_DEV,), in_specs=[pl.BlockSpec(in_block, in_map)],
        out_specs=pl.BlockSpec(out_block, out_map), out_shape=out_shape, compiler_params=_params(("parallel",)),
    )(a)


def _grad_x(d_r1, d_x_proj):
    t = d_r1.shape[0]
    tm = _pick(t, 512)

    def body(a_ref, b_ref, o_ref):
        o_ref[...] = ALPHA * a_ref[...] + b_ref[...]

    tile = pl.BlockSpec((tm, D_MODEL), lambda i: (i, 0))
    return pl.pallas_call(
        body, name="grad_x", grid=(t // tm,), in_specs=[tile, tile], out_specs=tile,
        out_shape=jax.ShapeDtypeStruct((t, D_MODEL), F32), compiler_params=_params(("parallel",)),
    )(d_r1, d_x_proj)


_GELU_C = math.sqrt(2.0 / math.pi)


def _gelu_parts(u):
    inner = _GELU_C * (u + 0.044715 * u * u * u)
    th = jnp.tanh(inner)
    return th, 0.5 * u * (1.0 + th)


def _ffn_act_fwd(z, w_dw, b_dw):
    t = z.shape[2]
    tm = _pick(t, 256)
    nh = tm // FFN_HALO

    def body(z_ref, zh_ref, w_ref, b_ref, act_ref, us_ref):
        i = pl.program_id(1)
        us_ref[0:FFN_HALO, :] = jnp.where(i == 0, 0.0, zh_ref[...])
        us_ref[FFN_HALO:FFN_HALO + tm, :] = z_ref[0]
        for r in range(tm // ROW_BLOCK):
            base = r * ROW_BLOCK
            uc = jnp.broadcast_to(b_ref[...], (ROW_BLOCK, FF_SHARD))
            for k in range(FFN_K):
                off = base + FFN_HALO - (FFN_K - 1) + k
                uc = uc + w_ref[k:k + 1, :] * us_ref[off:off + ROW_BLOCK, :]
            _, gelu = _gelu_parts(uc)
            act_ref[base:base + ROW_BLOCK, :] = (gelu * z_ref[1, base:base + ROW_BLOCK, :]).astype(BF16)

    return pl.pallas_call(
        body, name="ffn_act_fwd", grid=(4, t // tm),
        in_specs=[pl.BlockSpec((2, None, tm, FF_SHARD), lambda j, i: (0, j, i, 0)),
                  pl.BlockSpec((None, None, FFN_HALO, FF_SHARD), lambda j, i: (0, j, jnp.maximum(i * nh - 1, 0), 0)),
                  pl.BlockSpec((None, FFN_K, FF_SHARD), lambda j, i: (j, 0, 0)),
                  pl.BlockSpec((None, 1, FF_SHARD), lambda j, i: (j, 0, 0))],
        out_specs=pl.BlockSpec((None, tm, FF_SHARD), lambda j, i: (j, i, 0)),
        out_shape=jax.ShapeDtypeStruct((4, t, FF_SHARD), BF16),
        scratch_shapes=[pltpu.VMEM((FFN_HALO + tm, FF_SHARD), F32)],
        compiler_params=_params(("parallel", "arbitrary")),
    )(z, z, w_dw, b_dw)


def _ffn_act_bwd(d_act, z, w_dw, b_dw):
    t = z.shape[2]
    tm = _pick(t, 256)
    nt = t // tm
    nh = tm // FFN_HALO
    last_h = t // FFN_HALO - 1
    pad = FFN_HALO - (FFN_K - 1)

    def gelu_grad(uc):
        th, gelu = _gelu_parts(uc)
        dgelu = 0.5 * (1.0 + th) + 0.5 * uc * (1.0 - th * th) * _GELU_C * (1.0 + 3.0 * 0.044715 * uc * uc)
        return gelu, dgelu

    def body(da_ref, dah_ref, z_ref, zp_ref, zn_ref, w_ref, b_ref, dz_ref, sums_ref, us_ref, ds_ref):
        i = pl.program_id(1)

        @pl.when(i == 0)
        def _():
            sums_ref[...] = jnp.zeros_like(sums_ref)

        us_ref[0:FFN_HALO, :] = jnp.where(i == 0, 0.0, zp_ref[...])
        us_ref[FFN_HALO:FFN_HALO + tm, :] = z_ref[0]
        us_ref[FFN_HALO + tm:2 * FFN_HALO + tm, :] = zn_ref[0]

        def conv(base, n):
            uc = jnp.broadcast_to(b_ref[...], (n, FF_SHARD))
            for k in range(FFN_K):
                uc = uc + w_ref[k:k + 1, :] * us_ref[base + pad + k:base + pad + k + n, :]
            return uc

        for r in range(tm // ROW_BLOCK):
            base = r * ROW_BLOCK
            rows = slice(base, base + ROW_BLOCK)
            gelu, dgelu = gelu_grad(conv(base, ROW_BLOCK))
            da = da_ref[rows, :]
            dz_ref[1, rows, :] = (da * gelu).astype(BF16)
            duc = da * z_ref[1, rows, :] * dgelu
            ds_ref[rows, :] = duc
            for k in range(FFN_K):
                sums_ref[k:k + 1, :] += _colsum(duc * us_ref[base + pad + k:base + pad + k + ROW_BLOCK, :])
            sums_ref[FFN_K:FFN_K + 1, :] += _colsum(duc)
        _, dgelu_h = gelu_grad(conv(tm, FFN_HALO))
        ds_ref[tm:tm + FFN_HALO, :] = jnp.where(i == nt - 1, 0.0, dah_ref[...] * zn_ref[1] * dgelu_h)
        for r in range(tm // ROW_BLOCK):
            base = r * ROW_BLOCK
            du = jnp.zeros((ROW_BLOCK, FF_SHARD), F32)
            for k in range(FFN_K):
                off = base + FFN_K - 1 - k
                du = du + w_ref[k:k + 1, :] * ds_ref[off:off + ROW_BLOCK, :]
            dz_ref[0, base:base + ROW_BLOCK, :] = du.astype(BF16)

    nxt = lambda j, i: jnp.minimum((i + 1) * nh, last_h)
    return pl.pallas_call(
        body, name="ffn_act_bwd", grid=(4, nt),
        in_specs=[pl.BlockSpec((None, tm, FF_SHARD), lambda j, i: (j, i, 0)),
                  pl.BlockSpec((None, FFN_HALO, FF_SHARD), lambda j, i: (j, nxt(j, i), 0)),
                  pl.BlockSpec((2, None, tm, FF_SHARD), lambda j, i: (0, j, i, 0)),
                  pl.BlockSpec((None, None, FFN_HALO, FF_SHARD), lambda j, i: (0, j, jnp.maximum(i * nh - 1, 0), 0)),
                  pl.BlockSpec((2, None, FFN_HALO, FF_SHARD), lambda j, i: (0, j, nxt(j, i), 0)),
                  pl.BlockSpec((None, FFN_K, FF_SHARD), lambda j, i: (j, 0, 0)),
                  pl.BlockSpec((None, 1, FF_SHARD), lambda j, i: (j, 0, 0))],
        out_specs=[pl.BlockSpec((2, None, tm, FF_SHARD), lambda j, i: (0, j, i, 0)),
                   pl.BlockSpec((None, 8, FF_SHARD), lambda j, i: (j, 0, 0))],
        out_shape=[jax.ShapeDtypeStruct((2, 4, t, FF_SHARD), BF16), jax.ShapeDtypeStruct((4, 8, FF_SHARD), F32)],
        scratch_shapes=[pltpu.VMEM((2 * FFN_HALO + tm, FF_SHARD), F32), pltpu.VMEM((tm + FFN_HALO, FF_SHARD), F32)],
        compiler_params=_params(("parallel", "arbitrary")),
    )(d_act, d_act, z, z, z, w_dw, b_dw)


def _local_step(x, target, w_in, rest, small, scatter=None):
    t = x.shape[0]
    tm = _pick(t, 1024)
    tk = _pick(t, 2048)
    nm = t // tm
    nk = t // tk
    d = D_MODEL

    xb = _cast_bf16(x)
    proj = _mm("proj", xb, w_in, (t, IN_COLS), F32, (nm, N_DEV, 1),
               pl.BlockSpec((tm, d), lambda i, j, k: (i, 0)),
               pl.BlockSpec((None, d, IN_SHARD), lambda i, j, k: (j, 0, 0)),
               pl.BlockSpec((tm, IN_SHARD), lambda i, j, k: (i, j)), NN, (tm, IN_SHARD),
               hosted=rest if isinstance(rest, _Hosted) else None)
    if isinstance(rest, _Hosted):
        proj, rest = proj
    w_conv_out8, w_hgrn_out8, w_out8, w_ffn_in, w_ffn_out8, conv_dw8, ffn_dw8 = rest
    w_conv_out = _relayout("w_conv_out_natural", w_conv_out8, (None, CONV_DIM, 128), lambda j: (j, 0, 0),
                           (CONV_DIM, 128), lambda j: (0, j), jax.ShapeDtypeStruct((CONV_DIM, d), BF16))
    w_hgrn_out = w_hgrn_out8.reshape(d, d)
    w_out = w_out8.reshape(d, d)
    w_ffn_out = w_ffn_out8.reshape(4, FF_SHARD, d)
    conv_dw = jnp.transpose(conv_dw8[:, :CONV_K, :CONV_DIM // N_DEV], (1, 0, 2)).reshape(CONV_K, CONV_DIM)
    ffn_dw = jnp.transpose(ffn_dw8[:, :FFN_K, :D_FF // N_DEV], (1, 0, 2)).reshape(FFN_K, 4, FF_SHARD)
    small = dict(small, w_conv_dw=conv_dw, w_ffn_dw=jnp.transpose(ffn_dw, (1, 0, 2)),
                 b_ffn_dw=small["b_ffn_dw"].reshape(4, 1, FF_SHARD))

    c_act, conv_pre = _conv_fwd(proj, small["w_conv_dw"], small["b_conv_dw"], small["conv_ln_g"], small["conv_ln_b"])
    y_conv = _mm("y_conv", c_act, w_conv_out, (t, d), F32, (nm, 1, 1),
                 pl.BlockSpec((tm, CONV_DIM), lambda i, j, k: (i, 0)),
                 pl.BlockSpec((CONV_DIM, d), lambda i, j, k: (0, 0)),
                 pl.BlockSpec((tm, d), lambda i, j, k: (i, 0)), NN, (tm, d))
    o, og, states = _hgrn_fwd(proj, small["hgrn_lb_logits"], small["hgrn_norm_g"])
    sq_w = pl.BlockSpec((d, d), lambda i, j, k: (0, 0))
    row_tile = pl.BlockSpec((tm, d), lambda i, j, k: (i, 0))
    y_hgrn = _mm("y_hgrn", og, w_hgrn_out, (t, d), F32, (nm, 1, 1), row_tile, sq_w, row_tile, NN, (tm, d))
    mixed = _merge_fwd(proj, y_conv, y_hgrn)
    mix = _mm("mix", mixed, w_out, (t, d), F32, (nm, 1, 1), row_tile, sq_w, row_tile, NN, (tm, d))
    r1, x1b = _ln1_fwd(x, mix, small["ln1_g"], small["ln1_b"])
    z = _mm("ffn_in", x1b, w_ffn_in, (N_DEV, t, FF_SHARD), F32, (nm, N_DEV, 1), row_tile,
            pl.BlockSpec((None, d, FF_SHARD), lambda i, j, k: (j, 0, 0)),
            pl.BlockSpec((None, tm, FF_SHARD), lambda i, j, k: (j, i, 0)), NN, (tm, FF_SHARD))
    z = z.reshape(2, 4, t, FF_SHARD)
    act = _ffn_act_fwd(z, small["w_ffn_dw"], small["b_ffn_dw"])
    y_ffn = _mm("ffn_out", act, w_ffn_out, (t, d), F32, (nm, 1, 4),
                pl.BlockSpec((None, tm, FF_SHARD), lambda i, j, k: (k, i, 0)),
                pl.BlockSpec((None, FF_SHARD, d), lambda i, j, k: (k, 0, 0)), row_tile, NN, (tm, d))

    d_r2, d_r2b, sums_ln2 = _ln2_loss_bwd(r1, y_ffn, target, small["ln1_g"], small["ln1_b"], small["ln2_g"], small["ln2_b"])
    d_act = _mm("d_act", d_r2b, w_ffn_out, (4, t, FF_SHARD), F32, (nm, 4, 1), row_tile,
                pl.BlockSpec((None, FF_SHARD, d), lambda i, j, k: (j, 0, 0)),
                pl.BlockSpec((None, tm, FF_SHARD), lambda i, j, k: (j, i, 0)), NT, (tm, FF_SHARD))
    g_w_ffn_out = _mm("g_w_ffn_out", act, d_r2b, (4, FF_SHARD, d), BF16, (4, 1, nk),
                      pl.BlockSpec((None, tk, FF_SHARD), lambda i, j, k: (i, k, 0)),
                      pl.BlockSpec((tk, d), lambda i, j, k: (k, 0)),
                      pl.BlockSpec((None, FF_SHARD, d), lambda i, j, k: (i, 0, 0)), TN, (FF_SHARD, d))
    d_z, sums_ffn = _ffn_act_bwd(d_act, z, small["w_ffn_dw"], small["b_ffn_dw"])
    d_z8 = d_z.reshape(N_DEV, t, FF_SHARD)
    d_x1_ffn = _mm("d_x1_ffn", d_z8, w_ffn_in, (t, d), F32, (nm, 1, N_DEV),
                   pl.BlockSpec((None, tm, FF_SHARD), lambda i, j, k: (k, i, 0)),
                   pl.BlockSpec((None, d, FF_SHARD), lambda i, j, k: (k, 0, 0)), row_tile, NT, (tm, d))
    g_w_ffn_in = _mm("g_w_ffn_in", x1b, d_z8, (N_DEV, d, FF_SHARD), BF16, (N_DEV, 1, nk),
                     pl.BlockSpec((tk, d), lambda i, j, k: (k, 0)),
                     pl.BlockSpec((None, tk, FF_SHARD), lambda i, j, k: (i, k, 0)),
                     pl.BlockSpec((None, d, FF_SHARD), lambda i, j, k: (i, 0, 0)), TN, (d, FF_SHARD))
    d_r1, d_r1b, sums_ln1 = _ln1_bwd(d_r2, d_x1_ffn, r1, small["ln1_g"])
    d_mixed = _mm("d_mixed", d_r1b, w_out, (t, d), F32, (nm, 1, 1), row_tile, sq_w, row_tile, NT, (tm, d))
    k_tile = pl.BlockSpec((tk, d), lambda i, j, k: (k, 0))
    g_w_out = _mm("g_w_out", mixed, d_r1b, (d, d), BF16, (1, 1, nk), k_tile, k_tile, sq_w, TN, (d, d))
    d_y, d_mz = _merge_bwd(d_mixed, proj, y_conv, y_hgrn)
    d_c = _mm("d_c", d_y, w_conv_out, (t, CONV_DIM), F32, (nm, 1, 1),
              pl.BlockSpec((None, tm, d), lambda i, j, k: (0, i, 0)),
              pl.BlockSpec((CONV_DIM, d), lambda i, j, k: (0, 0)),
              pl.BlockSpec((tm, CONV_DIM), lambda i, j, k: (i, 0)), NT, (tm, CONV_DIM))
    g_w_conv_out = _mm("g_w_conv_out", c_act, d_y, (CONV_DIM, d), BF16, (1, 1, nk),
                       pl.BlockSpec((tk, CONV_DIM), lambda i, j, k: (k, 0)),
                       pl.BlockSpec((None, tk, d), lambda i, j, k: (0, k, 0)),
                       pl.BlockSpec((CONV_DIM, d), lambda i, j, k: (0, 0)), TN, (CONV_DIM, d))
    g_w_conv_out = _relayout("g_w_conv_out_shards", g_w_conv_out, (CONV_DIM, 128), lambda j: (0, j),
                             (None, CONV_DIM, 128), lambda j: (j, 0, 0),
                             jax.ShapeDtypeStruct((N_DEV, CONV_DIM, 128), BF16))
    d_og = _mm("d_og", d_y, w_hgrn_out, (t, d), F32, (nm, 1, 1),
               pl.BlockSpec((None, tm, d), lambda i, j, k: (1, i, 0)), sq_w, row_tile, NT, (tm, d))
    g_w_hgrn_out = _mm("g_w_hgrn_out", og, d_y, (d, d), BF16, (1, 1, nk), k_tile,
                       pl.BlockSpec((None, tk, d), lambda i, j, k: (1, k, 0)), sq_w, TN, (d, d))
    d_pre, sums_conv = _conv_bwd_norm(d_c, conv_pre, small["conv_ln_g"], small["conv_ln_b"])
    d_cproj, g_w_conv_dw = _conv_bwd_dw(d_pre, proj, small["w_conv_dw"])

    early = [g_w_conv_out, g_w_hgrn_out.reshape(N_DEV, d // N_DEV, d), g_w_out.reshape(N_DEV, d // N_DEV, d),
             g_w_ffn_in, g_w_ffn_out.reshape(N_DEV, D_FF // N_DEV, d)]
    hgrn_out = _hgrn_bwd(d_og, o, proj, states, small["hgrn_lb_logits"], small["hgrn_norm_g"],
                         hosted=None if scatter is None else scatter(early))
    d_qz, d_fz, d_iv, d_gz, sums_hgrn = hgrn_out[:5]
    early_recv = list(hgrn_out[5:])
    d_proj = jnp.concatenate([d_cproj, d_qz, d_fz, d_iv, d_gz, d_mz], axis=1)
    g_w_in = _mm("g_w_in", xb, d_proj, (N_DEV, d, IN_SHARD), BF16, (N_DEV, 1, nk), k_tile,
                 pl.BlockSpec((tk, IN_SHARD), lambda i, j, k: (k, i)),
                 pl.BlockSpec((None, d, IN_SHARD), lambda i, j, k: (i, 0, 0)), TN, (d, IN_SHARD))
    d_x_proj = _mm("d_x_proj", d_proj, w_in, (t, d), F32, (nm, 1, N_DEV),
                   pl.BlockSpec((tm, IN_SHARD), lambda i, j, k: (i, k)),
                   pl.BlockSpec((None, d, IN_SHARD), lambda i, j, k: (k, 0, 0)), row_tile, NT, (tm, d),
                   hosted=None if scatter is None else scatter([g_w_in]))
    late_recv = []
    if scatter is not None:
        d_x_proj, late_recv = d_x_proj
    grad_x = _grad_x(d_r1, d_x_proj)

    d_l0 = sums_hgrn[1:2]
    small_grads = {
        "loss": sums_ln2[2:3, 0:128],
        "b_conv_dw": sums_conv[2:3], "conv_ln_g": sums_conv[0:1], "conv_ln_b": sums_conv[1:2],
        "hgrn_lb_logits": jnp.concatenate([d_l0, -d_l0], axis=1),
        "hgrn_norm_g": sums_hgrn[0:1],
        "ln1_g": sums_ln1[0:1], "ln1_b": sums_ln1[1:2],
        "b_ffn_dw": sums_ffn[:, FFN_K, :].reshape(1, D_FF),
        "ln2_g": sums_ln2[0:1], "ln2_b": sums_ln2[1:2],
        "w_conv_dw": g_w_conv_dw[0:CONV_K].reshape(1, CONV_K * CONV_DIM),
        "w_ffn_dw": jnp.transpose(sums_ffn[:, 0:FFN_K, :], (1, 0, 2)).reshape(1, FFN_K * D_FF),
    }
    large_grads = [g_w_in] + early
    if scatter is not None:
        large_grads = list(zip(large_grads, late_recv + early_recv))
    return grad_x, large_grads, small_grads


def _coords():
    return lax.axis_index("x"), lax.axis_index("y"), lax.axis_index("c")


def _gather(shards):
    n = len(shards)

    def parts(ins, outs, sems):
        send_sems, recv_sems, local_sems = sems
        x, y, c = _coords()
        me = 4 * x + 2 * y + c
        sibling = (x, y, 1 - c)
        chips = [(1 - x, y), (x, 1 - y), (1 - x, 1 - y)]

        def copy(a, k, block, to, src=None):
            return pltpu.make_async_remote_copy(
                src_ref=outs[a].at[block] if src is None else src, dst_ref=outs[a].at[block],
                send_sem=send_sems.at[a, k], recv_sem=recv_sems.at[a, k], device_id=to, device_id_type=MESH)

        local = [pltpu.make_async_copy(ins[a], outs[a].at[me], local_sems.at[a]) for a in range(n)]
        first = []
        for a in range(n):
            first.append(copy(a, 0, me, sibling, src=ins[a]))
            for j, chip in enumerate(chips):
                first.append(copy(a, 1 + j, me, (*chip, c), src=ins[a]))
        return x, y, c, sibling, chips, copy, local, first

    def start(ins, outs, sems):
        *_, local, first = parts(ins, outs, sems)
        for cp in local + first:
            cp.start()

    def finish(ins, outs, sems):
        x, y, c, sibling, chips, copy, local, first = parts(ins, outs, sems)
        passed = []
        for j, (px, py) in enumerate(chips):
            for a in range(n):
                copy(a, 1 + j, 4 * px + 2 * py + c, sibling).wait_recv()
                cp = copy(a, 4 + j, 4 * px + 2 * py + c, sibling)
                cp.start()
                passed.append(cp)
        for a in range(n):
            copy(a, 0, 4 * x + 2 * y + 1 - c, sibling).wait_recv()
            for j, (px, py) in enumerate(chips):
                copy(a, 4 + j, 4 * px + 2 * py + 1 - c, sibling).wait_recv()
        for cp in first + passed:
            cp.wait_send()
        for cp in local:
            cp.wait()

    return _Hosted(shards, [jax.ShapeDtypeStruct((N_DEV,) + s.shape, s.dtype) for s in shards],
                   [pltpu.SemaphoreType.DMA((n, 7)), pltpu.SemaphoreType.DMA((n, 7)), pltpu.SemaphoreType.DMA((n,))],
                   start, finish)


def _scatter(grads):
    n = len(grads)

    def copies(ins, outs, sems):
        send_sems, recv_sems = sems
        x, y, c = _coords()
        out = []
        for a in range(n):
            for k in range(1, N_DEV):
                px, py, pc = x ^ (k >> 2), y ^ ((k >> 1) & 1), c ^ (k & 1)
                out.append(pltpu.make_async_remote_copy(
                    src_ref=ins[a].at[4 * px + 2 * py + pc], dst_ref=outs[a].at[k - 1],
                    send_sem=send_sems.at[a, k - 1], recv_sem=recv_sems.at[a, k - 1],
                    device_id=(px, py, pc), device_id_type=MESH))
        return out

    def start(ins, outs, sems):
        for cp in copies(ins, outs, sems):
            cp.start()

    def finish(ins, outs, sems):
        for cp in copies(ins, outs, sems):
            cp.wait()

    return _Hosted(grads, [jax.ShapeDtypeStruct((N_DEV - 1,) + g.shape[1:], g.dtype) for g in grads],
                   [pltpu.SemaphoreType.DMA((n, N_DEV - 1)), pltpu.SemaphoreType.DMA((n, N_DEV - 1))], start, finish)


def _row_tile(rows):
    return 256 if rows % 256 == 0 else rows


def _adam_math(w, g, m, v):
    m_new = ADAM_B1 * m + (1.0 - ADAM_B1) * g
    v_new = ADAM_B2 * v + (1.0 - ADAM_B2) * (g * g)
    m_hat = m_new / (1.0 - ADAM_B1 ** ADAM_STEP)
    v_hat = v_new / (1.0 - ADAM_B2 ** ADAM_STEP)
    delta = -ADAM_LR * (m_hat / (jnp.sqrt(v_hat) + ADAM_EPS) + ADAM_WD * w)
    return delta, m_new, v_new


def _adam_large(name, own, recv, me, w, m, v):
    rows, cols = w.shape
    tr = _row_tile(rows)

    def body(me_ref, p_ref, r_ref, w_ref, m_ref, v_ref, g_out, d_out, m_out, v_out):
        g = p_ref[...].astype(F32)
        for k in range(N_DEV - 1):
            g = g + r_ref[k].astype(F32)
        delta, m_new, v_new = _adam_math(w_ref[...], g, m_ref[...], v_ref[...])
        g_out[...] = g
        d_out[...] = delta
        m_out[...] = m_new
        v_out[...] = v_new

    tile = pl.BlockSpec((tr, cols), lambda r, me_ref: (r, 0))
    sds = jax.ShapeDtypeStruct((rows, cols), F32)
    return pl.pallas_call(
        body, name=name,
        grid_spec=pltpu.PrefetchScalarGridSpec(
            num_scalar_prefetch=1, grid=(rows // tr,),
            in_specs=[pl.BlockSpec((None, tr, cols), lambda r, me_ref: (me_ref[0], r, 0)),
                      pl.BlockSpec((N_DEV - 1, tr, cols), lambda r, me_ref: (0, r, 0)), tile, tile, tile],
            out_specs=[tile, tile, tile, tile]),
        out_shape=[sds, sds, sds, sds],
        compiler_params=_params(("parallel",)),
    )(me, own, recv, w, m, v)


def _small_allreduce(vec):
    rows = vec.shape[0]

    def body(v_ref, o_ref, gat_ref, send_sems, recv_sems):
        x, y, c = _coords()
        me = 4 * x + 2 * y + c
        gat_ref[me] = v_ref[...]
        copies = []
        for k in range(1, N_DEV):
            px, py, pc = x ^ (k >> 2), y ^ ((k >> 1) & 1), c ^ (k & 1)
            copies.append(pltpu.make_async_remote_copy(
                src_ref=v_ref, dst_ref=gat_ref.at[me], send_sem=send_sems.at[k - 1], recv_sem=recv_sems.at[k - 1],
                device_id=(px, py, pc), device_id_type=MESH))
        for cp in copies:
            cp.start()
        for k in range(1, N_DEV):
            px, py, pc = x ^ (k >> 2), y ^ ((k >> 1) & 1), c ^ (k & 1)
            pltpu.make_async_remote_copy(
                src_ref=v_ref, dst_ref=gat_ref.at[4 * px + 2 * py + pc], send_sem=send_sems.at[k - 1],
                recv_sem=recv_sems.at[k - 1], device_id=(px, py, pc), device_id_type=MESH).wait_recv()
        for cp in copies:
            cp.wait_send()
        acc = gat_ref[0]
        for dev in range(1, N_DEV):
            acc = acc + gat_ref[dev]
        o_ref[...] = acc

    whole = pl.BlockSpec(memory_space=pltpu.VMEM)
    return pl.pallas_call(
        body, name="small_allreduce", in_specs=[whole], out_specs=whole,
        out_shape=jax.ShapeDtypeStruct((rows, 128), F32),
        scratch_shapes=[pltpu.VMEM((N_DEV, rows, 128), F32), pltpu.SemaphoreType.DMA((N_DEV - 1,)),
                        pltpu.SemaphoreType.DMA((N_DEV - 1,))],
        compiler_params=pltpu.CompilerParams(has_side_effects=True, vmem_limit_bytes=VMEM_LIMIT),
    )(vec)


def _adam_small(w, g, m, v):
    def body(w_ref, g_ref, m_ref, v_ref, d_out, m_out, v_out):
        delta, m_new, v_new = _adam_math(w_ref[...], g_ref[...], m_ref[...], v_ref[...])
        d_out[...] = delta
        m_out[...] = m_new
        v_out[...] = v_new

    whole = pl.BlockSpec(memory_space=pltpu.VMEM)
    sds = jax.ShapeDtypeStruct(w.shape, F32)
    return pl.pallas_call(body, name="adam_small", in_specs=[whole] * 4, out_specs=[whole] * 3,
                          out_shape=[sds, sds, sds])(w, g, m, v)


_SMALL_ORDER = ["loss", "b_conv_dw", "conv_ln_g", "conv_ln_b", "hgrn_lb_logits", "hgrn_norm_g", "ln1_g", "ln1_b",
                "b_ffn_dw", "ln2_g", "ln2_b", "w_conv_dw", "w_ffn_dw"]
_WEIGHTS = ["w_in", "w_conv_dw", "b_conv_dw", "conv_ln_g", "conv_ln_b", "w_conv_out", "hgrn_lb_logits", "hgrn_norm_g",
            "w_hgrn_out", "w_out", "ln1_g", "ln1_b", "w_ffn_in", "w_ffn_dw", "b_ffn_dw", "w_ffn_out", "ln2_g", "ln2_b"]
_LARGE = ["w_in", "w_conv_out", "w_hgrn_out", "w_out", "w_ffn_in", "w_ffn_out"]
_CONV_DW_SHARD = CONV_DIM // N_DEV
_FFN_DW_SHARD = D_FF // N_DEV


def kernel(x, w_in, w_conv_dw, b_conv_dw, conv_ln_g, conv_ln_b, w_conv_out, hgrn_lb_logits, hgrn_norm_g, w_hgrn_out, w_out, ln1_g, ln1_b, w_ffn_in, w_ffn_dw, b_ffn_dw, w_ffn_out, ln2_g, ln2_b, loss_target, m_w_in, m_w_conv_dw, m_b_conv_dw, m_conv_ln_g, m_conv_ln_b, m_w_conv_out, m_hgrn_lb_logits, m_hgrn_norm_g, m_w_hgrn_out, m_w_out, m_ln1_g, m_ln1_b, m_w_ffn_in, m_w_ffn_dw, m_b_ffn_dw, m_w_ffn_out, m_ln2_g, m_ln2_b, v_w_in, v_w_conv_dw, v_b_conv_dw, v_conv_ln_g, v_conv_ln_b, v_w_conv_out, v_hgrn_lb_logits, v_hgrn_norm_g, v_w_hgrn_out, v_w_out, v_ln1_g, v_ln1_b, v_w_ffn_in, v_w_ffn_dw, v_b_ffn_dw, v_w_ffn_out, v_ln2_g, v_ln2_b):
    w = dict(w_in=w_in, w_conv_dw=w_conv_dw, b_conv_dw=b_conv_dw, conv_ln_g=conv_ln_g, conv_ln_b=conv_ln_b,
             w_conv_out=w_conv_out, hgrn_lb_logits=hgrn_lb_logits, hgrn_norm_g=hgrn_norm_g, w_hgrn_out=w_hgrn_out,
             w_out=w_out, ln1_g=ln1_g, ln1_b=ln1_b, w_ffn_in=w_ffn_in, w_ffn_dw=w_ffn_dw, b_ffn_dw=b_ffn_dw,
             w_ffn_out=w_ffn_out, ln2_g=ln2_g, ln2_b=ln2_b)
    m = dict(w_in=m_w_in, w_conv_dw=m_w_conv_dw, b_conv_dw=m_b_conv_dw, conv_ln_g=m_conv_ln_g, conv_ln_b=m_conv_ln_b,
             w_conv_out=m_w_conv_out, hgrn_lb_logits=m_hgrn_lb_logits, hgrn_norm_g=m_hgrn_norm_g,
             w_hgrn_out=m_w_hgrn_out, w_out=m_w_out, ln1_g=m_ln1_g, ln1_b=m_ln1_b, w_ffn_in=m_w_ffn_in,
             w_ffn_dw=m_w_ffn_dw, b_ffn_dw=m_b_ffn_dw, w_ffn_out=m_w_ffn_out, ln2_g=m_ln2_g, ln2_b=m_ln2_b)
    v = dict(w_in=v_w_in, w_conv_dw=v_w_conv_dw, b_conv_dw=v_b_conv_dw, conv_ln_g=v_conv_ln_g, conv_ln_b=v_conv_ln_b,
             w_conv_out=v_w_conv_out, hgrn_lb_logits=v_hgrn_lb_logits, hgrn_norm_g=v_hgrn_norm_g,
             w_hgrn_out=v_w_hgrn_out, w_out=v_w_out, ln1_g=v_ln1_g, ln1_b=v_ln1_b, w_ffn_in=v_w_ffn_in,
             w_ffn_dw=v_w_ffn_dw, b_ffn_dw=v_b_ffn_dw, w_ffn_out=v_w_ffn_out, ln2_g=v_ln2_g, ln2_b=v_ln2_b)
    xi, yi, ci = lax.axis_index("x"), lax.axis_index("y"), lax.axis_index("c")
    me = 4 * xi + 2 * yi + ci
    me_op = jnp.reshape(me, (1,)).astype(jnp.int32)

    shards = [w[name][0].astype(BF16) for name in _LARGE]
    shards.append(jnp.pad(w_conv_dw[0], ((0, 1), (0, 128 - _CONV_DW_SHARD))))
    shards.append(jnp.pad(w_ffn_dw[0], ((0, 8 - FFN_K), (0, 384 - _FFN_DW_SHARD))))
    (w_in_all,) = _run_hosted("all_gather_w_in", _gather(shards[:1]))
    small = dict(b_conv_dw=b_conv_dw, conv_ln_g=conv_ln_g, conv_ln_b=conv_ln_b, hgrn_lb_logits=hgrn_lb_logits,
                 hgrn_norm_g=hgrn_norm_g, ln1_g=ln1_g, ln1_b=ln1_b, ln2_g=ln2_g, ln2_b=ln2_b, b_ffn_dw=b_ffn_dw)

    grad_x, large_grads, small_grads = _local_step(x[0], loss_target[0], w_in_all, _gather(shards[1:]), small, _scatter)

    out = {}
    for name, (own, recv) in zip(_LARGE, large_grads):
        out[name] = _adam_large("adam_" + name, own, recv, me_op, w[name][0], m[name][0], v[name][0])

    vec = jnp.concatenate([small_grads[name] for name in _SMALL_ORDER], axis=1)
    total = _small_allreduce(vec.reshape(-1, 128)).reshape(1, -1)
    sizes = [small_grads[name].shape[1] for name in _SMALL_ORDER]
    offs = [0]
    for s in sizes:
        offs.append(offs[-1] + s)
    summed = {name: total[:, offs[i]:offs[i + 1]] for i, name in enumerate(_SMALL_ORDER)}
    loss = summed["loss"][0, 0]
    conv_dw_g = lax.dynamic_slice_in_dim(summed["w_conv_dw"].reshape(CONV_K, CONV_DIM), me * _CONV_DW_SHARD, _CONV_DW_SHARD, axis=1)
    ffn_dw_g = lax.dynamic_slice_in_dim(summed["w_ffn_dw"].reshape(FFN_K, D_FF), me * _FFN_DW_SHARD, _FFN_DW_SHARD, axis=1)
    small_g = dict(summed, w_conv_dw=conv_dw_g.reshape(1, -1), w_ffn_dw=ffn_dw_g.reshape(1, -1))
    names = [n for n in _SMALL_ORDER if n != "loss"]
    flat = lambda d, n: d[n].reshape(1, -1)
    n_small = sum(small_g[n].shape[1] for n in names)
    pad = (-n_small) % 1024
    pack = lambda pieces: jnp.pad(jnp.concatenate(pieces, axis=1), ((0, 0), (0, pad))).reshape(-1, 128)
    d_s, m_s, v_s = _adam_small(pack([flat(w, n) for n in names]), pack([small_g[n] for n in names]),
                                pack([flat(m, n) for n in names]), pack([flat(v, n) for n in names]))
    pos = 0
    for n in names:
        size = small_g[n].shape[1]
        cut = lambda a: a.reshape(1, -1)[:, pos:pos + size].reshape(w[n].shape)
        out[n] = (small_g[n].reshape(w[n].shape), cut(d_s), cut(m_s), cut(v_s))
        pos += size

    for name in _LARGE:
        out[name] = tuple(a.reshape(w[name].shape) for a in out[name])
    grads = [out[n][0] for n in _WEIGHTS]
    deltas = [out[n][1] for n in _WEIGHTS]
    new_m = [out[n][2] for n in _WEIGHTS]
    new_v = [out[n][3] for n in _WEIGHTS]
    return (loss, grad_x[None], *grads, *deltas, *new_m, *new_v)
```

```python
import functools
import math

import jax
import jax.numpy as jnp
from jax import lax
from jax.experimental import pallas as pl
from jax.experimental.pallas import tpu as pltpu

F32 = jnp.float32
BF16 = jnp.bfloat16

N_DEV = 8
D_MODEL = 1024
CONV_DIM = 512
CONV_K = 31
HGRN_DIM = 1024
HEADS = 8
HEAD_DIM = 128
D_FF = 2816
FFN_K = 3
FF_SHARD = 2 * D_FF // N_DEV
IN_COLS = 7168
IN_SHARD = IN_COLS // N_DEV
LN_EPS = 1e-5
RMS_EPS = 1e-6
ALPHA = 2.0 ** 0.25

ADAM_LR = 0.001
ADAM_B1 = 0.9
ADAM_B2 = 0.999
ADAM_EPS = 1e-08
ADAM_WD = 0.01
ADAM_STEP = 10

CHUNK = 64
CHUNKS_PER_BLOCK = 8
CONV_HALO = 32
FFN_HALO = 8
ROW_BLOCK = 64
VMEM_LIMIT = 48 * 1024 * 1024
MXU_DEPTH = 256

MESH = pl.DeviceIdType.MESH
ANY = pl.BlockSpec(memory_space=pl.ANY)

NN = (((1,), (0,)), ((), ()))
NT = (((1,), (1,)), ((), ()))
TN = (((0,), (0,)), ((), ()))


def _params(sem):
    return pltpu.CompilerParams(dimension_semantics=sem, vmem_limit_bytes=VMEM_LIMIT)


def _dot(a, b, dims):
    return lax.dot_general(a.astype(BF16), b.astype(BF16), dims, preferred_element_type=F32)


def _dot_f32(a, b, dims):
    return lax.dot_general(a, b, dims, preferred_element_type=F32, precision=lax.Precision.HIGHEST)


def _sigmoid(x):
    return jax.nn.sigmoid(x)


def _ln(r):
    mu = jnp.mean(r, axis=-1, keepdims=True)
    xc = r - mu
    var = jnp.mean(xc * xc, axis=-1, keepdims=True)
    rstd = lax.rsqrt(var + LN_EPS)
    return xc * rstd, rstd


def _ln_bwd(dy, xhat, rstd, g):
    dxh = dy * g
    m1 = jnp.mean(dxh, axis=-1, keepdims=True)
    m2 = jnp.mean(dxh * xhat, axis=-1, keepdims=True)
    return rstd * (dxh - m1 - xhat * m2)


def _colsum(x):
    return jnp.sum(x, axis=0, keepdims=True)


class _Hosted:
    def __init__(self, inputs, out_shapes, sem_shapes, start, finish):
        self.inputs, self.out_shapes, self.sem_shapes = list(inputs), list(out_shapes), list(sem_shapes)
        self.start, self.finish = start, finish


def _call(body, *, name, grid, in_specs, out_specs, out_shape, scratch_shapes, semantics, operands, hosted=None):
    if hosted is None:
        return pl.pallas_call(
            body, name=name, grid=grid, in_specs=list(in_specs), out_specs=list(out_specs), out_shape=list(out_shape),
            scratch_shapes=list(scratch_shapes), compiler_params=_params(semantics))(*operands)
    n_in, n_out, n_scr = len(in_specs), len(out_specs), len(scratch_shapes)
    h_in, h_out = len(hosted.inputs), len(hosted.out_shapes)

    def full_body(*refs):
        ins, refs = refs[:n_in], refs[n_in:]
        h_ins, refs = refs[:h_in], refs[h_in:]
        outs, refs = refs[:n_out], refs[n_out:]
        h_outs, refs = refs[:h_out], refs[h_out:]
        scr, sems = refs[:n_scr], refs[n_scr:]
        first = functools.reduce(jnp.logical_and, [pl.program_id(d) == 0 for d in range(len(grid))])
        last = functools.reduce(jnp.logical_and, [pl.program_id(d) == grid[d] - 1 for d in range(len(grid))])

        @pl.when(first)
        def _():
            hosted.start(h_ins, h_outs, sems)

        body(*ins, *outs, *scr)

        @pl.when(last)
        def _():
            hosted.finish(h_ins, h_outs, sems)

    return pl.pallas_call(
        full_body, name=name, grid=grid, in_specs=list(in_specs) + [ANY] * h_in,
        out_specs=list(out_specs) + [ANY] * h_out, out_shape=list(out_shape) + hosted.out_shapes,
        scratch_shapes=list(scratch_shapes) + hosted.sem_shapes,
        compiler_params=pltpu.CompilerParams(dimension_semantics=("arbitrary",) * len(grid),
                                             vmem_limit_bytes=VMEM_LIMIT, has_side_effects=True),
    )(*operands, *hosted.inputs)


def _run_hosted(name, hosted):
    h_in, h_out = len(hosted.inputs), len(hosted.out_shapes)

    def body(*refs):
        ins, outs, sems = refs[:h_in], refs[h_in:h_in + h_out], refs[h_in + h_out:]
        hosted.start(ins, outs, sems)
        hosted.finish(ins, outs, sems)

    return pl.pallas_call(
        body, name=name, in_specs=[ANY] * h_in, out_specs=[ANY] * h_out, out_shape=hosted.out_shapes,
        scratch_shapes=hosted.sem_shapes, compiler_params=pltpu.CompilerParams(has_side_effects=True),
    )(*hosted.inputs)


def _mm(name, a, b, out_shape, out_dtype, grid, a_spec, b_spec, o_spec, dims, acc_shape, hosted=None):
    nk = grid[2]
    if nk == 1:
        def body(a_ref, b_ref, o_ref):
            o_ref[...] = _dot(a_ref[...], b_ref[...], dims).astype(o_ref.dtype)
        scratch = []
    else:
        def body(a_ref, b_ref, o_ref, acc_ref):
            k = pl.program_id(2)

            @pl.when(k == 0)
            def _():
                acc_ref[...] = jnp.zeros_like(acc_ref)

            acc_ref[...] += _dot(a_ref[...], b_ref[...], dims)

            @pl.when(k == nk - 1)
            def _():
                o_ref[...] = acc_ref[...].astype(o_ref.dtype)
        scratch = [pltpu.VMEM(acc_shape, F32)]

    outs = _call(body, name=name, grid=grid, in_specs=[a_spec, b_spec], out_specs=[o_spec],
                 out_shape=[jax.ShapeDtypeStruct(out_shape, out_dtype)], scratch_shapes=scratch,
                 semantics=("parallel", "parallel", "arbitrary"), operands=(a, b), hosted=hosted)
    return outs[0] if hosted is None else (outs[0], list(outs[1:]))


def _pick(t, pref):
    return pref if t % pref == 0 else t


def _glu(p):
    return p[:, :CONV_DIM] * _sigmoid(p[:, CONV_DIM:])


def _conv_fwd(proj, w_dw, b_dw, g, b):
    t = proj.shape[0]
    tm = _pick(t, 256)
    nh = tm // CONV_HALO

    def body(p_ref, ph_ref, w_ref, bd_ref, g_ref, b_ref, act_ref, pre_ref, xs_ref):
        i = pl.program_id(0)
        halo = _glu(ph_ref[...])
        xs_ref[0:CONV_HALO, :] = jnp.where(i == 0, 0.0, halo)
        xs_ref[CONV_HALO:CONV_HALO + tm, :] = _glu(p_ref[...])
        for r in range(tm // ROW_BLOCK):
            acc = jnp.broadcast_to(bd_ref[...], (ROW_BLOCK, CONV_DIM))
            for k in range(CONV_K):
                off = r * ROW_BLOCK + CONV_HALO - (CONV_K - 1) + k
                acc = acc + w_ref[k:k + 1, :] * xs_ref[off:off + ROW_BLOCK, :]
            rows = slice(r * ROW_BLOCK, (r + 1) * ROW_BLOCK)
            pre_ref[rows, :] = acc
            xhat, _ = _ln(acc)
            yln = xhat * g_ref[...] + b_ref[...]
            act_ref[rows, :] = (yln * _sigmoid(yln)).astype(BF16)

    full = lambda s: pl.BlockSpec(s, lambda i: (0, 0))
    return pl.pallas_call(
        body, name="conv_fwd", grid=(t // tm,),
        in_specs=[pl.BlockSpec((tm, 2 * CONV_DIM), lambda i: (i, 0)),
                  pl.BlockSpec((CONV_HALO, 2 * CONV_DIM), lambda i: (jnp.maximum(i * nh - 1, 0), 0)),
                  full((CONV_K, CONV_DIM)), full((1, CONV_DIM)), full((1, CONV_DIM)), full((1, CONV_DIM))],
        out_specs=[pl.BlockSpec((tm, CONV_DIM), lambda i: (i, 0)), pl.BlockSpec((tm, CONV_DIM), lambda i: (i, 0))],
        out_shape=[jax.ShapeDtypeStruct((t, CONV_DIM), BF16), jax.ShapeDtypeStruct((t, CONV_DIM), F32)],
        scratch_shapes=[pltpu.VMEM((CONV_HALO + tm, CONV_DIM), F32)],
        compiler_params=_params(("arbitrary",)),
    )(proj, proj, w_dw, b_dw, g, b)


def _conv_bwd_norm(d_c, pre, g, b):
    t = d_c.shape[0]
    tm = _pick(t, 256)
    nt = t // tm

    def body(dc_ref, pre_ref, g_ref, b_ref, dpre_ref, sums_ref):
        i = pl.program_id(0)

        @pl.when(i == 0)
        def _():
            sums_ref[...] = jnp.zeros_like(sums_ref)

        xhat, rstd = _ln(pre_ref[...])
        yln = xhat * g_ref[...] + b_ref[...]
        sg = _sigmoid(yln)
        dyln = dc_ref[...] * (sg * (1.0 + yln * (1.0 - sg)))
        dpre = _ln_bwd(dyln, xhat, rstd, g_ref[...])
        dpre_ref[...] = dpre
        sums_ref[0:1, :] += _colsum(dyln * xhat)
        sums_ref[1:2, :] += _colsum(dyln)
        sums_ref[2:3, :] += _colsum(dpre)

    full = lambda s: pl.BlockSpec(s, lambda i: (0, 0))
    tile = pl.BlockSpec((tm, CONV_DIM), lambda i: (i, 0))
    return pl.pallas_call(
        body, name="conv_bwd_norm", grid=(nt,),
        in_specs=[tile, tile, full((1, CONV_DIM)), full((1, CONV_DIM))],
        out_specs=[tile, full((8, CONV_DIM))],
        out_shape=[jax.ShapeDtypeStruct((t, CONV_DIM), F32), jax.ShapeDtypeStruct((8, CONV_DIM), F32)],
        compiler_params=_params(("arbitrary",)),
    )(d_c, pre, g, b)


def _conv_bwd_dw(d_pre, proj, w_dw):
    t = d_pre.shape[0]
    tm = _pick(t, 256)
    nt = t // tm
    nh = tm // CONV_HALO
    last_h = t // CONV_HALO - 1

    def body(dp_ref, dph_ref, p_ref, ph_ref, w_ref, dproj_ref, dw_ref, xs_ref, ds_ref):
        i = pl.program_id(0)

        @pl.when(i == 0)
        def _():
            dw_ref[...] = jnp.zeros_like(dw_ref)

        halo = _glu(ph_ref[...])
        xs_ref[0:CONV_HALO, :] = jnp.where(i == 0, 0.0, halo)
        xs_ref[CONV_HALO:CONV_HALO + tm, :] = _glu(p_ref[...])
        ds_ref[0:tm, :] = dp_ref[...]
        ds_ref[tm:tm + CONV_HALO, :] = jnp.where(i == nt - 1, 0.0, dph_ref[...])
        for r in range(tm // ROW_BLOCK):
            base = r * ROW_BLOCK
            dpre = ds_ref[base:base + ROW_BLOCK, :]
            acc = jnp.zeros((ROW_BLOCK, CONV_DIM), F32)
            for k in range(CONV_K):
                acc = acc + w_ref[k:k + 1, :] * ds_ref[base + CONV_K - 1 - k:base + CONV_K - 1 - k + ROW_BLOCK, :]
                off = base + CONV_HALO - (CONV_K - 1) + k
                dw_ref[k:k + 1, :] += _colsum(dpre * xs_ref[off:off + ROW_BLOCK, :])
            rows = slice(base, base + ROW_BLOCK)
            cval = p_ref[rows, 0:CONV_DIM]
            sg = _sigmoid(p_ref[rows, CONV_DIM:2 * CONV_DIM])
            dproj_ref[rows, 0:CONV_DIM] = (acc * sg).astype(BF16)
            dproj_ref[rows, CONV_DIM:2 * CONV_DIM] = (acc * cval * sg * (1.0 - sg)).astype(BF16)

    full = lambda s: pl.BlockSpec(s, lambda i: (0, 0))
    return pl.pallas_call(
        body, name="conv_bwd_dw", grid=(nt,),
        in_specs=[pl.BlockSpec((tm, CONV_DIM), lambda i: (i, 0)),
                  pl.BlockSpec((CONV_HALO, CONV_DIM), lambda i: (jnp.minimum((i + 1) * nh, last_h), 0)),
                  pl.BlockSpec((tm, 2 * CONV_DIM), lambda i: (i, 0)),
                  pl.BlockSpec((CONV_HALO, 2 * CONV_DIM), lambda i: (jnp.maximum(i * nh - 1, 0), 0)),
                  full((CONV_K, CONV_DIM))],
        out_specs=[pl.BlockSpec((tm, 2 * CONV_DIM), lambda i: (i, 0)), full((CONV_HALO, CONV_DIM))],
        out_shape=[jax.ShapeDtypeStruct((t, 2 * CONV_DIM), BF16), jax.ShapeDtypeStruct((CONV_HALO, CONV_DIM), F32)],
        scratch_shapes=[pltpu.VMEM((CONV_HALO + tm, CONV_DIM), F32), pltpu.VMEM((tm + CONV_HALO, CONV_DIM), F32)],
        compiler_params=_params(("arbitrary",)),
    )(d_pre, d_pre, proj, proj, w_dw)


def _lower_bound(logit_ref):
    l0 = logit_ref[0:1, :]
    l1 = logit_ref[1:2, :]
    m = jnp.maximum(l0, l1)
    e0 = jnp.exp(l0 - m)
    e1 = jnp.exp(l1 - m)
    return e0 / (e0 + e1)


def _tri(lower):
    r = lax.broadcasted_iota(jnp.int32, (CHUNK, CHUNK), 0)
    c = lax.broadcasted_iota(jnp.int32, (CHUNK, CHUNK), 1)
    return (c <= r) if lower else (c >= r)


def _hgrn_gates(fz, lb):
    s = _sigmoid(fz)
    sn = _sigmoid(-fz)
    f = lb + (1.0 - lb) * s
    return s, sn, f


def _block_tri(rows, lower=True):
    r = lax.broadcasted_iota(jnp.int32, (rows, rows), 0)
    c = lax.broadcasted_iota(jnp.int32, (rows, rows), 1)
    tri = (c <= r) if lower else (c >= r)
    return (tri & (r // CHUNK == c // CHUNK)).astype(BF16)


def _tri_rows(tm):
    return min(tm, MXU_DEPTH)


def _tri_matmul(tri_ref, x):
    hi = x.astype(BF16)
    lo = (x - hi.astype(F32)).astype(BF16)
    tri = tri_ref[...]
    return (lax.dot_general(tri, hi, NN, preferred_element_type=F32)
            + lax.dot_general(tri, lo, NN, preferred_element_type=F32))


def _groups(tm):
    g = _tri_rows(tm)
    return [slice(i * g, (i + 1) * g) for i in range(tm // g)]


def _hgrn_fwd(proj, logits, norm_g):
    t = proj.shape[0]
    tm = CHUNK * CHUNKS_PER_BLOCK if t % (CHUNK * CHUNKS_PER_BLOCK) == 0 else CHUNK
    cpb = tm // CHUNK
    nt = t // tm
    half = CHUNK // 2

    def body(qz_ref, fz_ref, iv_ref, gz_ref, lg_ref, ng_ref, tri_ref, o_ref, og_ref, st_ref,
             state_ref, qe_ref, ke_ref, qb_ref, kl_ref, v_ref, upd_ref, decay_ref, a_ref, q_ref, kk_ref, b_ref):
        j = pl.program_id(1)

        @pl.when(j == 0)
        def _():
            state_ref[...] = jnp.zeros_like(state_ref)

        lb = _lower_bound(lg_ref)
        chunks = [slice(c * CHUNK, (c + 1) * CHUNK) for c in range(cpb)]
        for rows in chunks:
            qz = qz_ref[rows, :]
            q_ref[rows, :] = qz * _sigmoid(qz)
            _, sn, f = _hgrn_gates(fz_ref[rows, :], lb)
            kk_ref[rows, :] = (1.0 - lb) * sn
            b_ref[rows, :] = jnp.log(f)
            v_ref[rows, :] = iv_ref[rows, :].astype(BF16)
        for rows in _groups(tm):
            b_ref[rows, :] = _tri_matmul(tri_ref, b_ref[rows, :])
        for c, rows in enumerate(chunks):
            b = b_ref[rows, :]
            bref = b[half - 1:half, :]
            blast = b[CHUNK - 1:CHUNK, :]
            q = q_ref[rows, :]
            kk = kk_ref[rows, :]
            qb_ref[rows, :] = (q * jnp.exp(b)).astype(BF16)
            qe_ref[rows, :] = (q * jnp.exp(b - bref)).astype(BF16)
            ke_ref[rows, :] = (kk * jnp.exp(bref - b)).astype(BF16)
            kl_ref[rows, :] = (kk * jnp.exp(blast - b)).astype(BF16)
            decay_ref[c:c + 1, :] = jnp.exp(blast)
        causal = _tri(True)
        for c, rows in enumerate(chunks):
            upd_ref[c] = _dot(v_ref[rows, :], kl_ref[rows, :], TN)
            a_ref[c] = jnp.where(causal, _dot(qe_ref[rows, :], ke_ref[rows, :], NT), 0.0).astype(BF16)
        state = state_ref[...]
        for c in range(cpb):
            st_ref[c] = state.astype(BF16)
            state = state * decay_ref[c:c + 1, :] + upd_ref[c]
        state_ref[...] = state
        for c, rows in enumerate(chunks):
            o_ref[rows, :] = _dot(a_ref[c], v_ref[rows, :], NN) + _dot(qb_ref[rows, :], st_ref[c], NT)
        for rows in chunks:
            o = o_ref[rows, :]
            r = lax.rsqrt(jnp.mean(o * o, axis=-1, keepdims=True) + RMS_EPS)
            gz = gz_ref[rows, :]
            og_ref[rows, :] = ((o * r * ng_ref[...]) * (gz * _sigmoid(gz))).astype(BF16)

    col = lambda base: pl.BlockSpec((tm, HEAD_DIM), lambda h, j: (j, base + h))
    tile_bf = pltpu.VMEM((tm, HEAD_DIM), BF16)
    tile_f32 = pltpu.VMEM((tm, HEAD_DIM), F32)
    return pl.pallas_call(
        body, name="hgrn_fwd", grid=(HEADS, nt),
        in_specs=[col(8), col(16), col(24), col(32),
                  pl.BlockSpec((2, HEAD_DIM), lambda h, j: (0, h)), pl.BlockSpec((1, HEAD_DIM), lambda h, j: (0, h)),
                  pl.BlockSpec((_tri_rows(tm), _tri_rows(tm)), lambda h, j: (0, 0))],
        out_specs=[col(0), col(0), pl.BlockSpec((None, cpb, HEAD_DIM, HEAD_DIM), lambda h, j: (h, j, 0, 0))],
        out_shape=[jax.ShapeDtypeStruct((t, HGRN_DIM), F32), jax.ShapeDtypeStruct((t, HGRN_DIM), BF16),
                   jax.ShapeDtypeStruct((HEADS, t // CHUNK, HEAD_DIM, HEAD_DIM), BF16)],
        scratch_shapes=[pltpu.VMEM((HEAD_DIM, HEAD_DIM), F32), tile_bf, tile_bf, tile_bf, tile_bf, tile_bf,
                        pltpu.VMEM((cpb, HEAD_DIM, HEAD_DIM), F32), pltpu.VMEM((max(cpb, 8), HEAD_DIM), F32),
                        pltpu.VMEM((cpb, CHUNK, CHUNK), BF16), tile_f32, tile_f32, tile_f32],
        compiler_params=_params(("parallel", "arbitrary")),
    )(proj, proj, proj, proj, logits, norm_g, _block_tri(_tri_rows(tm)))


def _hgrn_bwd(d_og, o, proj, states, logits, norm_g, hosted=None):
    t = proj.shape[0]
    tm = CHUNK * CHUNKS_PER_BLOCK if t % (CHUNK * CHUNKS_PER_BLOCK) == 0 else CHUNK
    cpb = tm // CHUNK
    nt = t // tm
    half = CHUNK // 2

    def body(dog_ref, o_ref, qz_ref, fz_ref, iv_ref, gz_ref, st_ref, lg_ref, ng_ref, tril_ref, triu_ref,
             dqz_ref, dfz_ref, div_ref, dgz_ref, sums_ref,
             dstate_ref, qe_ref, ke_ref, qb_ref, kl_ref, v_ref, do_ref, upd_ref, dst_ref, a_ref, da_ref,
             decay_ref, through_ref, q_ref, kk_ref, b_ref, dsilu_ref, gs_ref, gf_ref, sn_ref,
             eb_ref, ebr_ref, ekr_ref, ebl_ref, rev_ref, pre_ref, dk_ref):
        j = pl.program_id(1)

        @pl.when(j == 0)
        def _():
            dstate_ref[...] = jnp.zeros_like(dstate_ref)
            sums_ref[...] = jnp.zeros_like(sums_ref)

        lb = _lower_bound(lg_ref)
        ng = ng_ref[...]
        chunks = [slice(c * CHUNK, (c + 1) * CHUNK) for c in range(cpb)]
        for rows in chunks:
            qz = qz_ref[rows, :]
            sq = _sigmoid(qz)
            q_ref[rows, :] = qz * sq
            dsilu_ref[rows, :] = sq * (1.0 + qz * (1.0 - sq))
            s, sn, f = _hgrn_gates(fz_ref[rows, :], lb)
            kk_ref[rows, :] = (1.0 - lb) * sn
            b_ref[rows, :] = jnp.log(f)
            sn_ref[rows, :] = sn
            gf_ref[rows, :] = sn / f
            gs_ref[rows, :] = (1.0 - lb) * s
            v_ref[rows, :] = iv_ref[rows, :].astype(BF16)
            ov = o_ref[rows, :]
            r = lax.rsqrt(jnp.mean(ov * ov, axis=-1, keepdims=True) + RMS_EPS)
            on = ov * r
            gz = gz_ref[rows, :]
            sg = _sigmoid(gz)
            dog = dog_ref[rows, :]
            dgz_ref[rows, :] = (dog * (on * ng) * (sg * (1.0 + gz * (1.0 - sg)))).astype(BF16)
            d_ong = dog * (gz * sg)
            sums_ref[0:1, :] += _colsum(d_ong * on)
            d_on = d_ong * ng
            do_ref[rows, :] = (r * (d_on - on * jnp.mean(d_on * on, axis=-1, keepdims=True))).astype(BF16)
        for rows in _groups(tm):
            b_ref[rows, :] = _tri_matmul(tril_ref, b_ref[rows, :])
        for c, rows in enumerate(chunks):
            b = b_ref[rows, :]
            bref = b[half - 1:half, :]
            blast = b[CHUNK - 1:CHUNK, :]
            q = q_ref[rows, :]
            kk = kk_ref[rows, :]
            eb = jnp.exp(b)
            ebr = jnp.exp(b - bref)
            ekr = jnp.exp(bref - b)
            ebl = jnp.exp(blast - b)
            eb_ref[rows, :] = eb
            ebr_ref[rows, :] = ebr
            ekr_ref[rows, :] = ekr
            ebl_ref[rows, :] = ebl
            qb_ref[rows, :] = (q * eb).astype(BF16)
            qe_ref[rows, :] = (q * ebr).astype(BF16)
            ke_ref[rows, :] = (kk * ekr).astype(BF16)
            kl_ref[rows, :] = (kk * ebl).astype(BF16)
            decay_ref[c:c + 1, :] = jnp.exp(blast)
        causal = _tri(True)
        for c, rows in enumerate(chunks):
            upd_ref[c] = _dot(do_ref[rows, :], qb_ref[rows, :], TN)
            a_ref[c] = jnp.where(causal, _dot(qe_ref[rows, :], ke_ref[rows, :], NT), 0.0).astype(BF16)
            da_ref[c] = jnp.where(causal, _dot(do_ref[rows, :], v_ref[rows, :], NT), 0.0).astype(BF16)
        dstate = dstate_ref[...]
        for c in reversed(range(cpb)):
            dst_ref[c] = dstate.astype(BF16)
            decay = decay_ref[c:c + 1, :]
            through_ref[c:c + 1, :] = decay * _colsum(dstate * st_ref[c].astype(F32))
            dstate = dstate * decay + upd_ref[c]
        dstate_ref[...] = dstate
        for c, rows in enumerate(chunks):
            qe, ke, v, do = qe_ref[rows, :], ke_ref[rows, :], v_ref[rows, :], do_ref[rows, :]
            div_ref[rows, :] = (_dot(a_ref[c], do, TN) + _dot(kl_ref[rows, :], dst_ref[c], NT)).astype(BF16)
            dqe = _dot(da_ref[c], ke, NN)
            dke = _dot(da_ref[c], qe, TN)
            dq_inter = _dot(do, st_ref[c], NN) * eb_ref[rows, :]
            dk_inter = _dot(v, dst_ref[c], NN) * ebl_ref[rows, :]
            dk_ref[rows, :] = dke * ekr_ref[rows, :] + dk_inter
            dqz_ref[rows, :] = ((dqe * ebr_ref[rows, :] + dq_inter) * dsilu_ref[rows, :]).astype(BF16)
            rev_ref[rows, :] = (qe.astype(F32) * dqe - ke.astype(F32) * dke) + q_ref[rows, :] * dq_inter
            pre_ref[rows, :] = kk_ref[rows, :] * dk_inter
        for rows in _groups(tm):
            pre = pre_ref[rows, :]
            rev_ref[rows, :] = _tri_matmul(triu_ref, rev_ref[rows, :]) + (_tri_matmul(tril_ref, pre) - pre)
        for c, rows in enumerate(chunks):
            dlf = rev_ref[rows, :] + through_ref[c:c + 1, :]
            common = gf_ref[rows, :] * dlf - sn_ref[rows, :] * dk_ref[rows, :]
            dfz_ref[rows, :] = (gs_ref[rows, :] * common).astype(BF16)
            sums_ref[1:2, :] += _colsum(common)

        @pl.when(j == nt - 1)
        def _():
            sums_ref[1:2, :] = sums_ref[1:2, :] * lb * (1.0 - lb)

    rev = lambda base: pl.BlockSpec((tm, HEAD_DIM), lambda h, j: (nt - 1 - j, base + h))
    vec = lambda n: pl.BlockSpec((n, HEAD_DIM), lambda h, j: (0, h))
    const = pl.BlockSpec((_tri_rows(tm), _tri_rows(tm)), lambda h, j: (0, 0))
    bf = jax.ShapeDtypeStruct((t, HGRN_DIM), BF16)
    tile_bf = pltpu.VMEM((tm, HEAD_DIM), BF16)
    tile_f32 = pltpu.VMEM((tm, HEAD_DIM), F32)
    square = lambda dtype: pltpu.VMEM((cpb, HEAD_DIM, HEAD_DIM), dtype)
    rows8 = pltpu.VMEM((max(cpb, 8), HEAD_DIM), F32)
    return _call(
        body, name="hgrn_bwd", grid=(HEADS, nt),
        in_specs=[rev(0), rev(0), rev(8), rev(16), rev(24), rev(32),
                  pl.BlockSpec((None, cpb, HEAD_DIM, HEAD_DIM), lambda h, j: (h, nt - 1 - j, 0, 0)),
                  vec(2), vec(1), const, const],
        out_specs=[rev(0), rev(0), rev(0), rev(0), vec(8)],
        out_shape=[bf, bf, bf, bf, jax.ShapeDtypeStruct((8, HGRN_DIM), F32)],
        scratch_shapes=[pltpu.VMEM((HEAD_DIM, HEAD_DIM), F32)] + [tile_bf] * 6 + [square(F32), square(BF16)]
        + [pltpu.VMEM((cpb, CHUNK, CHUNK), BF16)] * 2 + [rows8, rows8] + [tile_f32] * 14,
        semantics=("parallel", "arbitrary"),
        operands=(d_og, o, proj, proj, proj, proj, states, logits, norm_g, _block_tri(_tri_rows(tm)), _block_tri(_tri_rows(tm), lower=False)),
        hosted=hosted)


def _merge_fwd(proj, y_conv, y_hgrn):
    t = proj.shape[0]
    tm = _pick(t, 512)

    def body(m0_ref, m1_ref, yc_ref, yh_ref, o_ref):
        o_ref[...] = (_sigmoid(m0_ref[...]) * yc_ref[...] + _sigmoid(m1_ref[...]) * yh_ref[...]).astype(BF16)

    tile = pl.BlockSpec((tm, D_MODEL), lambda i: (i, 0))
    return pl.pallas_call(
        body, name="merge_fwd", grid=(t // tm,),
        in_specs=[pl.BlockSpec((tm, D_MODEL), lambda i: (i, 5)), pl.BlockSpec((tm, D_MODEL), lambda i: (i, 6)), tile, tile],
        out_specs=tile, out_shape=jax.ShapeDtypeStruct((t, D_MODEL), BF16),
        compiler_params=_params(("parallel",)),
    )(proj, proj, y_conv, y_hgrn)


def _merge_bwd(d_mixed, proj, y_conv, y_hgrn):
    t = proj.shape[0]
    tm = _pick(t, 512)

    def body(dm_ref, m_ref, yc_ref, yh_ref, dy_ref, dmz_ref):
        br = pl.program_id(1)
        y = jnp.where(br == 0, yc_ref[...], yh_ref[...])
        sg = _sigmoid(m_ref[...])
        dm = dm_ref[...]
        dy_ref[...] = (sg * dm).astype(BF16)
        dmz_ref[...] = (dm * y * sg * (1.0 - sg)).astype(BF16)

    tile = pl.BlockSpec((tm, D_MODEL), lambda i, br: (i, 0))
    return pl.pallas_call(
        body, name="merge_bwd", grid=(t // tm, 2),
        in_specs=[tile, pl.BlockSpec((tm, D_MODEL), lambda i, br: (i, 5 + br)), tile, tile],
        out_specs=[pl.BlockSpec((None, tm, D_MODEL), lambda i, br: (br, i, 0)), pl.BlockSpec((tm, D_MODEL), lambda i, br: (i, br))],
        out_shape=[jax.ShapeDtypeStruct((2, t, D_MODEL), BF16), jax.ShapeDtypeStruct((t, 2 * D_MODEL), BF16)],
        compiler_params=_params(("parallel", "arbitrary")),
    )(d_mixed, proj, y_conv, y_hgrn)


def _ln1_fwd(x, mix, g, b):
    t = x.shape[0]
    tm = _pick(t, 512)

    def body(x_ref, mix_ref, g_ref, b_ref, r_ref, xb_ref):
        r = ALPHA * x_ref[...] + mix_ref[...]
        r_ref[...] = r
        xhat, _ = _ln(r)
        xb_ref[...] = (xhat * g_ref[...] + b_ref[...]).astype(BF16)

    tile = pl.BlockSpec((tm, D_MODEL), lambda i: (i, 0))
    vec = pl.BlockSpec((1, D_MODEL), lambda i: (0, 0))
    return pl.pallas_call(
        body, name="ln1_fwd", grid=(t // tm,), in_specs=[tile, tile, vec, vec], out_specs=[tile, tile],
        out_shape=[jax.ShapeDtypeStruct((t, D_MODEL), F32), jax.ShapeDtypeStruct((t, D_MODEL), BF16)],
        compiler_params=_params(("parallel",)),
    )(x, mix, g, b)


def _ln2_loss_bwd(r1, y_ffn, target, g1, b1, g2, b2):
    t = r1.shape[0]
    tm = _pick(t, 256)
    nt = t // tm

    def body(r1_ref, y_ref, tg_ref, g1_ref, b1_ref, g2_ref, b2_ref, dr_ref, drb_ref, sums_ref, sq_ref):
        i = pl.program_id(0)

        @pl.when(i == 0)
        def _():
            sums_ref[...] = jnp.zeros_like(sums_ref)
            sq_ref[...] = jnp.zeros_like(sq_ref)

        xh1, _ = _ln(r1_ref[...])
        x1 = xh1 * g1_ref[...] + b1_ref[...]
        xh2, rstd2 = _ln(ALPHA * x1 + y_ref[...])
        diff = xh2 * g2_ref[...] + b2_ref[...] - tg_ref[...]
        dy = diff * (1.0 / D_MODEL)
        dr = _ln_bwd(dy, xh2, rstd2, g2_ref[...])
        dr_ref[...] = dr
        drb_ref[...] = dr.astype(BF16)
        sums_ref[0:1, :] += _colsum(dy * xh2)
        sums_ref[1:2, :] += _colsum(dy)
        sq_ref[...] += _colsum(diff * diff)

        @pl.when(i == nt - 1)
        def _():
            total = jnp.sum(sq_ref[...], axis=-1, keepdims=True) * (0.5 / D_MODEL)
            sums_ref[2:3, :] = jnp.broadcast_to(total, (1, D_MODEL))

    tile = pl.BlockSpec((tm, D_MODEL), lambda i: (i, 0))
    vec = pl.BlockSpec((1, D_MODEL), lambda i: (0, 0))
    return pl.pallas_call(
        body, name="ln2_loss_bwd", grid=(nt,), in_specs=[tile, tile, tile, vec, vec, vec, vec],
        out_specs=[tile, tile, pl.BlockSpec((8, D_MODEL), lambda i: (0, 0))],
        out_shape=[jax.ShapeDtypeStruct((t, D_MODEL), F32), jax.ShapeDtypeStruct((t, D_MODEL), BF16),
                   jax.ShapeDtypeStruct((8, D_MODEL), F32)],
        scratch_shapes=[pltpu.VMEM((1, D_MODEL), F32)],
        compiler_params=_params(("arbitrary",)),
    )(r1, y_ffn, target, g1, b1, g2, b2)


def _ln1_bwd(d_r2, d_x1_ffn, r1, g1):
    t = r1.shape[0]
    tm = _pick(t, 256)

    def body(dr2_ref, dx_ref, r1_ref, g_ref, dr1_ref, dr1b_ref, sums_ref):
        i = pl.program_id(0)

        @pl.when(i == 0)
        def _():
            sums_ref[...] = jnp.zeros_like(sums_ref)

        xhat, rstd = _ln(r1_ref[...])
        dx1 = ALPHA * dr2_ref[...] + dx_ref[...]
        dr1 = _ln_bwd(dx1, xhat, rstd, g_ref[...])
        dr1_ref[...] = dr1
        dr1b_ref[...] = dr1.astype(BF16)
        sums_ref[0:1, :] += _colsum(dx1 * xhat)
        sums_ref[1:2, :] += _colsum(dx1)

    tile = pl.BlockSpec((tm, D_MODEL), lambda i: (i, 0))
    vec = pl.BlockSpec((1, D_MODEL), lambda i: (0, 0))
    return pl.pallas_call(
        body, name="ln1_bwd", grid=(t // tm,), in_specs=[tile, tile, tile, vec],
        out_specs=[tile, tile, pl.BlockSpec((8, D_MODEL), lambda i: (0, 0))],
        out_shape=[jax.ShapeDtypeStruct((t, D_MODEL), F32), jax.ShapeDtypeStruct((t, D_MODEL), BF16),
                   jax.ShapeDtypeStruct((8, D_MODEL), F32)],
        compiler_params=_params(("arbitrary",)),
    )(d_r2, d_x1_ffn, r1, g1)


def _cast_bf16(x):
    t = x.shape[0]
    tm = _pick(t, 512)

    def body(x_ref, o_ref):
        o_ref[...] = x_ref[...].astype(BF16)

    tile = pl.BlockSpec((tm, D_MODEL), lambda i: (i, 0))
    return pl.pallas_call(
        body, name="cast_x", grid=(t // tm,), in_specs=[tile], out_specs=tile,
        out_shape=jax.ShapeDtypeStruct((t, D_MODEL), BF16), compiler_params=_params(("parallel",)),
    )(x)


def _relayout(name, a, in_block, in_map, out_block, out_map, out_shape):
    def body(a_ref, o_ref):
        o_ref[...] = a_ref[...].astype(o_ref.dtype)

    return pl.pallas_call(
        body, name=name, grid=(N_DEV,), in_specs=[pl.BlockSpec(in_block, in_map)],
        out_specs=pl.BlockSpec(out_block, out_map), out_shape=out_shape, compiler_params=_params(("parallel",)),
    )(a)


def _grad_x(d_r1, d_x_proj):
    t = d_r1.shape[0]
    tm = _pick(t, 512)

    def body(a_ref, b_ref, o_ref):
        o_ref[...] = ALPHA * a_ref[...] + b_ref[...]

    tile = pl.BlockSpec((tm, D_MODEL), lambda i: (i, 0))
    return pl.pallas_call(
        body, name="grad_x", grid=(t // tm,), in_specs=[tile, tile], out_specs=tile,
        out_shape=jax.ShapeDtypeStruct((t, D_MODEL), F32), compiler_params=_params(("parallel",)),
    )(d_r1, d_x_proj)


_GELU_C = math.sqrt(2.0 / math.pi)


def _gelu_parts(u):
    inner = _GELU_C * (u + 0.044715 * u * u * u)
    th = jnp.tanh(inner)
    return th, 0.5 * u * (1.0 + th)


def _ffn_act_fwd(z, w_dw, b_dw):
    t = z.shape[2]
    tm = _pick(t, 256)
    nh = tm // FFN_HALO

    def body(z_ref, zh_ref, w_ref, b_ref, act_ref, us_ref):
        i = pl.program_id(1)
        us_ref[0:FFN_HALO, :] = jnp.where(i == 0, 0.0, zh_ref[...])
        us_ref[FFN_HALO:FFN_HALO + tm, :] = z_ref[0]
        for r in range(tm // ROW_BLOCK):
            base = r * ROW_BLOCK
            uc = jnp.broadcast_to(b_ref[...], (ROW_BLOCK, FF_SHARD))
            for k in range(FFN_K):
                off = base + FFN_HALO - (FFN_K - 1) + k
                uc = uc + w_ref[k:k + 1, :] * us_ref[off:off + ROW_BLOCK, :]
            _, gelu = _gelu_parts(uc)
            act_ref[base:base + ROW_BLOCK, :] = (gelu * z_ref[1, base:base + ROW_BLOCK, :]).astype(BF16)

    return pl.pallas_call(
        body, name="ffn_act_fwd", grid=(4, t // tm),
        in_specs=[pl.BlockSpec((2, None, tm, FF_SHARD), lambda j, i: (0, j, i, 0)),
                  pl.BlockSpec((None, None, FFN_HALO, FF_SHARD), lambda j, i: (0, j, jnp.maximum(i * nh - 1, 0), 0)),
                  pl.BlockSpec((None, FFN_K, FF_SHARD), lambda j, i: (j, 0, 0)),
                  pl.BlockSpec((None, 1, FF_SHARD), lambda j, i: (j, 0, 0))],
        out_specs=pl.BlockSpec((None, tm, FF_SHARD), lambda j, i: (j, i, 0)),
        out_shape=jax.ShapeDtypeStruct((4, t, FF_SHARD), BF16),
        scratch_shapes=[pltpu.VMEM((FFN_HALO + tm, FF_SHARD), F32)],
        compiler_params=_params(("parallel", "arbitrary")),
    )(z, z, w_dw, b_dw)


def _ffn_act_bwd(d_act, z, w_dw, b_dw):
    t = z.shape[2]
    tm = _pick(t, 256)
    nt = t // tm
    nh = tm // FFN_HALO
    last_h = t // FFN_HALO - 1
    pad = FFN_HALO - (FFN_K - 1)

    def gelu_grad(uc):
        th, gelu = _gelu_parts(uc)
        dgelu = 0.5 * (1.0 + th) + 0.5 * uc * (1.0 - th * th) * _GELU_C * (1.0 + 3.0 * 0.044715 * uc * uc)
        return gelu, dgelu

    def body(da_ref, dah_ref, z_ref, zp_ref, zn_ref, w_ref, b_ref, dz_ref, sums_ref, us_ref, ds_ref):
        i = pl.program_id(1)

        @pl.when(i == 0)
        def _():
            sums_ref[...] = jnp.zeros_like(sums_ref)

        us_ref[0:FFN_HALO, :] = jnp.where(i == 0, 0.0, zp_ref[...])
        us_ref[FFN_HALO:FFN_HALO + tm, :] = z_ref[0]
        us_ref[FFN_HALO + tm:2 * FFN_HALO + tm, :] = zn_ref[0]

        def conv(base, n):
            uc = jnp.broadcast_to(b_ref[...], (n, FF_SHARD))
            for k in range(FFN_K):
                uc = uc + w_ref[k:k + 1, :] * us_ref[base + pad + k:base + pad + k + n, :]
            return uc

        for r in range(tm // ROW_BLOCK):
            base = r * ROW_BLOCK
            rows = slice(base, base + ROW_BLOCK)
            gelu, dgelu = gelu_grad(conv(base, ROW_BLOCK))
            da = da_ref[rows, :]
            dz_ref[1, rows, :] = (da * gelu).astype(BF16)
            duc = da * z_ref[1, rows, :] * dgelu
            ds_ref[rows, :] = duc
            for k in range(FFN_K):
                sums_ref[k:k + 1, :] += _colsum(duc * us_ref[base + pad + k:base + pad + k + ROW_BLOCK, :])
            sums_ref[FFN_K:FFN_K + 1, :] += _colsum(duc)
        _, dgelu_h = gelu_grad(conv(tm, FFN_HALO))
        ds_ref[tm:tm + FFN_HALO, :] = jnp.where(i == nt - 1, 0.0, dah_ref[...] * zn_ref[1] * dgelu_h)
        for r in range(tm // ROW_BLOCK):
            base = r * ROW_BLOCK
            du = jnp.zeros((ROW_BLOCK, FF_SHARD), F32)
            for k in range(FFN_K):
                off = base + FFN_K - 1 - k
                du = du + w_ref[k:k + 1, :] * ds_ref[off:off + ROW_BLOCK, :]
            dz_ref[0, base:base + ROW_BLOCK, :] = du.astype(BF16)

    nxt = lambda j, i: jnp.minimum((i + 1) * nh, last_h)
    return pl.pallas_call(
        body, name="ffn_act_bwd", grid=(4, nt),
        in_specs=[pl.BlockSpec((None, tm, FF_SHARD), lambda j, i: (j, i, 0)),
                  pl.BlockSpec((None, FFN_HALO, FF_SHARD), lambda j, i: (j, nxt(j, i), 0)),
                  pl.BlockSpec((2, None, tm, FF_SHARD), lambda j, i: (0, j, i, 0)),
                  pl.BlockSpec((None, None, FFN_HALO, FF_SHARD), lambda j, i: (0, j, jnp.maximum(i * nh - 1, 0), 0)),
                  pl.BlockSpec((2, None, FFN_HALO, FF_SHARD), lambda j, i: (0, j, nxt(j, i), 0)),
                  pl.BlockSpec((None, FFN_K, FF_SHARD), lambda j, i: (j, 0, 0)),
                  pl.BlockSpec((None, 1, FF_SHARD), lambda j, i: (j, 0, 0))],
        out_specs=[pl.BlockSpec((2, None, tm, FF_SHARD), lambda j, i: (0, j, i, 0)),
                   pl.BlockSpec((None, 8, FF_SHARD), lambda j, i: (j, 0, 0))],
        out_shape=[jax.ShapeDtypeStruct((2, 4, t, FF_SHARD), BF16), jax.ShapeDtypeStruct((4, 8, FF_SHARD), F32)],
        scratch_shapes=[pltpu.VMEM((2 * FFN_HALO + tm, FF_SHARD), F32), pltpu.VMEM((tm + FFN_HALO, FF_SHARD), F32)],
        compiler_params=_params(("parallel", "arbitrary")),
    )(d_act, d_act, z, z, z, w_dw, b_dw)


def _local_step(x, target, w_in, rest, small, scatter=None):
    t = x.shape[0]
    tm = _pick(t, 1024)
    tk = _pick(t, 2048)
    nm = t // tm
    nk = t // tk
    d = D_MODEL

    xb = _cast_bf16(x)
    proj = _mm("proj", xb, w_in, (t, IN_COLS), F32, (nm, N_DEV, 1),
               pl.BlockSpec((tm, d), lambda i, j, k: (i, 0)),
               pl.BlockSpec((None, d, IN_SHARD), lambda i, j, k: (j, 0, 0)),
               pl.BlockSpec((tm, IN_SHARD), lambda i, j, k: (i, j)), NN, (tm, IN_SHARD),
               hosted=rest if isinstance(rest, _Hosted) else None)
    if isinstance(rest, _Hosted):
        proj, rest = proj
    w_conv_out8, w_hgrn_out8, w_out8, w_ffn_in, w_ffn_out8, conv_dw8, ffn_dw8 = rest
    w_conv_out = _relayout("w_conv_out_natural", w_conv_out8, (None, CONV_DIM, 128), lambda j: (j, 0, 0),
                           (CONV_DIM, 128), lambda j: (0, j), jax.ShapeDtypeStruct((CONV_DIM, d), BF16))
    w_hgrn_out = w_hgrn_out8.reshape(d, d)
    w_out = w_out8.reshape(d, d)
    w_ffn_out = w_ffn_out8.reshape(4, FF_SHARD, d)
    conv_dw = jnp.transpose(conv_dw8[:, :CONV_K, :CONV_DIM // N_DEV], (1, 0, 2)).reshape(CONV_K, CONV_DIM)
    ffn_dw = jnp.transpose(ffn_dw8[:, :FFN_K, :D_FF // N_DEV], (1, 0, 2)).reshape(FFN_K, 4, FF_SHARD)
    small = dict(small, w_conv_dw=conv_dw, w_ffn_dw=jnp.transpose(ffn_dw, (1, 0, 2)),
                 b_ffn_dw=small["b_ffn_dw"].reshape(4, 1, FF_SHARD))

    c_act, conv_pre = _conv_fwd(proj, small["w_conv_dw"], small["b_conv_dw"], small["conv_ln_g"], small["conv_ln_b"])
    y_conv = _mm("y_conv", c_act, w_conv_out, (t, d), F32, (nm, 1, 1),
                 pl.BlockSpec((tm, CONV_DIM), lambda i, j, k: (i, 0)),
                 pl.BlockSpec((CONV_DIM, d), lambda i, j, k: (0, 0)),
                 pl.BlockSpec((tm, d), lambda i, j, k: (i, 0)), NN, (tm, d))
    o, og, states = _hgrn_fwd(proj, small["hgrn_lb_logits"], small["hgrn_norm_g"])
    sq_w = pl.BlockSpec((d, d), lambda i, j, k: (0, 0))
    row_tile = pl.BlockSpec((tm, d), lambda i, j, k: (i, 0))
    y_hgrn = _mm("y_hgrn", og, w_hgrn_out, (t, d), F32, (nm, 1, 1), row_tile, sq_w, row_tile, NN, (tm, d))
    mixed = _merge_fwd(proj, y_conv, y_hgrn)
    mix = _mm("mix", mixed, w_out, (t, d), F32, (nm, 1, 1), row_tile, sq_w, row_tile, NN, (tm, d))
    r1, x1b = _ln1_fwd(x, mix, small["ln1_g"], small["ln1_b"])
    z = _mm("ffn_in", x1b, w_ffn_in, (N_DEV, t, FF_SHARD), F32, (nm, N_DEV, 1), row_tile,
            pl.BlockSpec((None, d, FF_SHARD), lambda i, j, k: (j, 0, 0)),
            pl.BlockSpec((None, tm, FF_SHARD), lambda i, j, k: (j, i, 0)), NN, (tm, FF_SHARD))
    z = z.reshape(2, 4, t, FF_SHARD)
    act = _ffn_act_fwd(z, small["w_ffn_dw"], small["b_ffn_dw"])
    y_ffn = _mm("ffn_out", act, w_ffn_out, (t, d), F32, (nm, 1, 4),
                pl.BlockSpec((None, tm, FF_SHARD), lambda i, j, k: (k, i, 0)),
                pl.BlockSpec((None, FF_SHARD, d), lambda i, j, k: (k, 0, 0)), row_tile, NN, (tm, d))

    d_r2, d_r2b, sums_ln2 = _ln2_loss_bwd(r1, y_ffn, target, small["ln1_g"], small["ln1_b"], small["ln2_g"], small["ln2_b"])
    d_act = _mm("d_act", d_r2b, w_ffn_out, (4, t, FF_SHARD), F32, (nm, 4, 1), row_tile,
                pl.BlockSpec((None, FF_SHARD, d), lambda i, j, k: (j, 0, 0)),
                pl.BlockSpec((None, tm, FF_SHARD), lambda i, j, k: (j, i, 0)), NT, (tm, FF_SHARD))
    g_w_ffn_out = _mm("g_w_ffn_out", act, d_r2b, (4, FF_SHARD, d), BF16, (4, 1, nk),
                      pl.BlockSpec((None, tk, FF_SHARD), lambda i, j, k: (i, k, 0)),
                      pl.BlockSpec((tk, d), lambda i, j, k: (k, 0)),
                      pl.BlockSpec((None, FF_SHARD, d), lambda i, j, k: (i, 0, 0)), TN, (FF_SHARD, d))
    d_z, sums_ffn = _ffn_act_bwd(d_act, z, small["w_ffn_dw"], small["b_ffn_dw"])
    d_z8 = d_z.reshape(N_DEV, t, FF_SHARD)
    d_x1_ffn = _mm("d_x1_ffn", d_z8, w_ffn_in, (t, d), F32, (nm, 1, N_DEV),
                   pl.BlockSpec((None, tm, FF_SHARD), lambda i, j, k: (k, i, 0)),
                   pl.BlockSpec((None, d, FF_SHARD), lambda i, j, k: (k, 0, 0)), row_tile, NT, (tm, d))
    g_w_ffn_in = _mm("g_w_ffn_in", x1b, d_z8, (N_DEV, d, FF_SHARD), BF16, (N_DEV, 1, nk),
                     pl.BlockSpec((tk, d), lambda i, j, k: (k, 0)),
                     pl.BlockSpec((None, tk, FF_SHARD), lambda i, j, k: (i, k, 0)),
                     pl.BlockSpec((None, d, FF_SHARD), lambda i, j, k: (i, 0, 0)), TN, (d, FF_SHARD))
    d_r1, d_r1b, sums_ln1 = _ln1_bwd(d_r2, d_x1_ffn, r1, small["ln1_g"])
    d_mixed = _mm("d_mixed", d_r1b, w_out, (t, d), F32, (nm, 1, 1), row_tile, sq_w, row_tile, NT, (tm, d))
    k_tile = pl.BlockSpec((tk, d), lambda i, j, k: (k, 0))
    g_w_out = _mm("g_w_out", mixed, d_r1b, (d, d), BF16, (1, 1, nk), k_tile, k_tile, sq_w, TN, (d, d))
    d_y, d_mz = _merge_bwd(d_mixed, proj, y_conv, y_hgrn)
    d_c = _mm("d_c", d_y, w_conv_out, (t, CONV_DIM), F32, (nm, 1, 1),
              pl.BlockSpec((None, tm, d), lambda i, j, k: (0, i, 0)),
              pl.BlockSpec((CONV_DIM, d), lambda i, j, k: (0, 0)),
              pl.BlockSpec((tm, CONV_DIM), lambda i, j, k: (i, 0)), NT, (tm, CONV_DIM))
    g_w_conv_out = _mm("g_w_conv_out", c_act, d_y, (CONV_DIM, d), BF16, (1, 1, nk),
                       pl.BlockSpec((tk, CONV_DIM), lambda i, j, k: (k, 0)),
                       pl.BlockSpec((None, tk, d), lambda i, j, k: (0, k, 0)),
                       pl.BlockSpec((CONV_DIM, d), lambda i, j, k: (0, 0)), TN, (CONV_DIM, d))
    g_w_conv_out = _relayout("g_w_conv_out_shards", g_w_conv_out, (CONV_DIM, 128), lambda j: (0, j),
                             (None, CONV_DIM, 128), lambda j: (j, 0, 0),
                             jax.ShapeDtypeStruct((N_DEV, CONV_DIM, 128), BF16))
    d_og = _mm("d_og", d_y, w_hgrn_out, (t, d), F32, (nm, 1, 1),
               pl.BlockSpec((None, tm, d), lambda i, j, k: (1, i, 0)), sq_w, row_tile, NT, (tm, d))
    g_w_hgrn_out = _mm("g_w_hgrn_out", og, d_y, (d, d), BF16, (1, 1, nk), k_tile,
                       pl.BlockSpec((None, tk, d), lambda i, j, k: (1, k, 0)), sq_w, TN, (d, d))
    d_pre, sums_conv = _conv_bwd_norm(d_c, conv_pre, small["conv_ln_g"], small["conv_ln_b"])
    d_cproj, g_w_conv_dw = _conv_bwd_dw(d_pre, proj, small["w_conv_dw"])

    early = [g_w_conv_out, g_w_hgrn_out.reshape(N_DEV, d // N_DEV, d), g_w_out.reshape(N_DEV, d // N_DEV, d),
             g_w_ffn_in, g_w_ffn_out.reshape(N_DEV, D_FF // N_DEV, d)]
    hgrn_out = _hgrn_bwd(d_og, o, proj, states, small["hgrn_lb_logits"], small["hgrn_norm_g"],
                         hosted=None if scatter is None else scatter(early))
    d_qz, d_fz, d_iv, d_gz, sums_hgrn = hgrn_out[:5]
    early_recv = list(hgrn_out[5:])
    d_proj = jnp.concatenate([d_cproj, d_qz, d_fz, d_iv, d_gz, d_mz], axis=1)
    g_w_in = _mm("g_w_in", xb, d_proj, (N_DEV, d, IN_SHARD), BF16, (N_DEV, 1, nk), k_tile,
                 pl.BlockSpec((tk, IN_SHARD), lambda i, j, k: (k, i)),
                 pl.BlockSpec((None, d, IN_SHARD), lambda i, j, k: (i, 0, 0)), TN, (d, IN_SHARD))
    d_x_proj = _mm("d_x_proj", d_proj, w_in, (t, d), F32, (nm, 1, N_DEV),
                   pl.BlockSpec((tm, IN_SHARD), lambda i, j, k: (i, k)),
                   pl.BlockSpec((None, d, IN_SHARD), lambda i, j, k: (k, 0, 0)), row_tile, NT, (tm, d),
                   hosted=None if scatter is None else scatter([g_w_in]))
    late_recv = []
    if scatter is not None:
        d_x_proj, late_recv = d_x_proj
    grad_x = _grad_x(d_r1, d_x_proj)

    d_l0 = sums_hgrn[1:2]
    small_grads = {
        "loss": sums_ln2[2:3, 0:128],
        "b_conv_dw": sums_conv[2:3], "conv_ln_g": sums_conv[0:1], "conv_ln_b": sums_conv[1:2],
        "hgrn_lb_logits": jnp.concatenate([d_l0, -d_l0], axis=1),
        "hgrn_norm_g": sums_hgrn[0:1],
        "ln1_g": sums_ln1[0:1], "ln1_b": sums_ln1[1:2],
        "b_ffn_dw": sums_ffn[:, FFN_K, :].reshape(1, D_FF),
        "ln2_g": sums_ln2[0:1], "ln2_b": sums_ln2[1:2],
        "w_conv_dw": g_w_conv_dw[0:CONV_K].reshape(1, CONV_K * CONV_DIM),
        "w_ffn_dw": jnp.transpose(sums_ffn[:, 0:FFN_K, :], (1, 0, 2)).reshape(1, FFN_K * D_FF),
    }
    large_grads = [g_w_in] + early
    if scatter is not None:
        large_grads = list(zip(large_grads, late_recv + early_recv))
    return grad_x, large_grads, small_grads


def _coords():
    return lax.axis_index("x"), lax.axis_index("y"), lax.axis_index("c")


def _gather(shards):
    n = len(shards)

    def parts(ins, outs, sems):
        send_sems, recv_sems, local_sems = sems
        x, y, c = _coords()
        me = 4 * x + 2 * y + c
        sibling = (x, y, 1 - c)
        chips = [(1 - x, y), (x, 1 - y), (1 - x, 1 - y)]

        def copy(a, k, block, to, src=None):
            return pltpu.make_async_remote_copy(
                src_ref=outs[a].at[block] if src is None else src, dst_ref=outs[a].at[block],
                send_sem=send_sems.at[a, k], recv_sem=recv_sems.at[a, k], device_id=to, device_id_type=MESH)

        local = [pltpu.make_async_copy(ins[a], outs[a].at[me], local_sems.at[a]) for a in range(n)]
        first = []
        for a in range(n):
            first.append(copy(a, 0, me, sibling, src=ins[a]))
            for j, chip in enumerate(chips):
                first.append(copy(a, 1 + j, me, (*chip, c), src=ins[a]))
        return x, y, c, sibling, chips, copy, local, first

    def start(ins, outs, sems):
        *_, local, first = parts(ins, outs, sems)
        for cp in local + first:
            cp.start()

    def finish(ins, outs, sems):
        x, y, c, sibling, chips, copy, local, first = parts(ins, outs, sems)
        passed = []
        for j, (px, py) in enumerate(chips):
            for a in range(n):
                copy(a, 1 + j, 4 * px + 2 * py + c, sibling).wait_recv()
                cp = copy(a, 4 + j, 4 * px + 2 * py + c, sibling)
                cp.start()
                passed.append(cp)
        for a in range(n):
            copy(a, 0, 4 * x + 2 * y + 1 - c, sibling).wait_recv()
            for j, (px, py) in enumerate(chips):
                copy(a, 4 + j, 4 * px + 2 * py + 1 - c, sibling).wait_recv()
        for cp in first + passed:
            cp.wait_send()
        for cp in local:
            cp.wait()

    return _Hosted(shards, [jax.ShapeDtypeStruct((N_DEV,) + s.shape, s.dtype) for s in shards],
                   [pltpu.SemaphoreType.DMA((n, 7)), pltpu.SemaphoreType.DMA((n, 7)), pltpu.SemaphoreType.DMA((n,))],
                   start, finish)


def _scatter(grads):
    n = len(grads)

    def copies(ins, outs, sems):
        send_sems, recv_sems = sems
        x, y, c = _coords()
        out = []
        for a in range(n):
            for k in range(1, N_DEV):
                px, py, pc = x ^ (k >> 2), y ^ ((k >> 1) & 1), c ^ (k & 1)
                out.append(pltpu.make_async_remote_copy(
                    src_ref=ins[a].at[4 * px + 2 * py + pc], dst_ref=outs[a].at[k - 1],
                    send_sem=send_sems.at[a, k - 1], recv_sem=recv_sems.at[a, k - 1],
                    device_id=(px, py, pc), device_id_type=MESH))
        return out

    def start(ins, outs, sems):
        for cp in copies(ins, outs, sems):
            cp.start()

    def finish(ins, outs, sems):
        for cp in copies(ins, outs, sems):
            cp.wait()

    return _Hosted(grads, [jax.ShapeDtypeStruct((N_DEV - 1,) + g.shape[1:], g.dtype) for g in grads],
                   [pltpu.SemaphoreType.DMA((n, N_DEV - 1)), pltpu.SemaphoreType.DMA((n, N_DEV - 1))], start, finish)


def _row_tile(rows):
    return 256 if rows % 256 == 0 else rows


def _adam_math(w, g, m, v):
    m_new = ADAM_B1 * m + (1.0 - ADAM_B1) * g
    v_new = ADAM_B2 * v + (1.0 - ADAM_B2) * (g * g)
    m_hat = m_new / (1.0 - ADAM_B1 ** ADAM_STEP)
    v_hat = v_new / (1.0 - ADAM_B2 ** ADAM_STEP)
    delta = -ADAM_LR * (m_hat / (jnp.sqrt(v_hat) + ADAM_EPS) + ADAM_WD * w)
    return delta, m_new, v_new


def _adam_large(name, own, recv, me, w, m, v):
    rows, cols = w.shape
    tr = _row_tile(rows)

    def body(me_ref, p_ref, r_ref, w_ref, m_ref, v_ref, g_out, d_out, m_out, v_out):
        g = p_ref[...].astype(F32)
        for k in range(N_DEV - 1):
            g = g + r_ref[k].astype(F32)
        delta, m_new, v_new = _adam_math(w_ref[...], g, m_ref[...], v_ref[...])
        g_out[...] = g
        d_out[...] = delta
        m_out[...] = m_new
        v_out[...] = v_new

    tile = pl.BlockSpec((tr, cols), lambda r, me_ref: (r, 0))
    sds = jax.ShapeDtypeStruct((rows, cols), F32)
    return pl.pallas_call(
        body, name=name,
        grid_spec=pltpu.PrefetchScalarGridSpec(
            num_scalar_prefetch=1, grid=(rows // tr,),
            in_specs=[pl.BlockSpec((None, tr, cols), lambda r, me_ref: (me_ref[0], r, 0)),
                      pl.BlockSpec((N_DEV - 1, tr, cols), lambda r, me_ref: (0, r, 0)), tile, tile, tile],
            out_specs=[tile, tile, tile, tile]),
        out_shape=[sds, sds, sds, sds],
        compiler_params=_params(("parallel",)),
    )(me, own, recv, w, m, v)


def _small_allreduce(vec):
    rows = vec.shape[0]

    def body(v_ref, o_ref, gat_ref, send_sems, recv_sems):
        x, y, c = _coords()
        me = 4 * x + 2 * y + c
        gat_ref[me] = v_ref[...]
        copies = []
        for k in range(1, N_DEV):
            px, py, pc = x ^ (k >> 2), y ^ ((k >> 1) & 1), c ^ (k & 1)
            copies.append(pltpu.make_async_remote_copy(
                src_ref=v_ref, dst_ref=gat_ref.at[me], send_sem=send_sems.at[k - 1], recv_sem=recv_sems.at[k - 1],
                device_id=(px, py, pc), device_id_type=MESH))
        for cp in copies:
            cp.start()
        for k in range(1, N_DEV):
            px, py, pc = x ^ (k >> 2), y ^ ((k >> 1) & 1), c ^ (k & 1)
            pltpu.make_async_remote_copy(
                src_ref=v_ref, dst_ref=gat_ref.at[4 * px + 2 * py + pc], send_sem=send_sems.at[k - 1],
                recv_sem=recv_sems.at[k - 1], device_id=(px, py, pc), device_id_type=MESH).wait_recv()
        for cp in copies:
            cp.wait_send()
        acc = gat_ref[0]
        for dev in range(1, N_DEV):
            acc = acc + gat_ref[dev]
        o_ref[...] = acc

    whole = pl.BlockSpec(memory_space=pltpu.VMEM)
    return pl.pallas_call(
        body, name="small_allreduce", in_specs=[whole], out_specs=whole,
        out_shape=jax.ShapeDtypeStruct((rows, 128), F32),
        scratch_shapes=[pltpu.VMEM((N_DEV, rows, 128), F32), pltpu.SemaphoreType.DMA((N_DEV - 1,)),
                        pltpu.SemaphoreType.DMA((N_DEV - 1,))],
        compiler_params=pltpu.CompilerParams(has_side_effects=True, vmem_limit_bytes=VMEM_LIMIT),
    )(vec)


def _adam_small(w, g, m, v):
    def body(w_ref, g_ref, m_ref, v_ref, d_out, m_out, v_out):
        delta, m_new, v_new = _adam_math(w_ref[...], g_ref[...], m_ref[...], v_ref[...])
        d_out[...] = delta
        m_out[...] = m_new
        v_out[...] = v_new

    whole = pl.BlockSpec(memory_space=pltpu.VMEM)
    sds = jax.ShapeDtypeStruct(w.shape, F32)
    return pl.pallas_call(body, name="adam_small", in_specs=[whole] * 4, out_specs=[whole] * 3,
                          out_shape=[sds, sds, sds])(w, g, m, v)


_SMALL_ORDER = ["loss", "b_conv_dw", "conv_ln_g", "conv_ln_b", "hgrn_lb_logits", "hgrn_norm_g", "ln1_g", "ln1_b",
                "b_ffn_dw", "ln2_g", "ln2_b", "w_conv_dw", "w_ffn_dw"]
_WEIGHTS = ["w_in", "w_conv_dw", "b_conv_dw", "conv_ln_g", "conv_ln_b", "w_conv_out", "hgrn_lb_logits", "hgrn_norm_g",
            "w_hgrn_out", "w_out", "ln1_g", "ln1_b", "w_ffn_in", "w_ffn_dw", "b_ffn_dw", "w_ffn_out", "ln2_g", "ln2_b"]
_LARGE = ["w_in", "w_conv_out", "w_hgrn_out", "w_out", "w_ffn_in", "w_ffn_out"]
_CONV_DW_SHARD = CONV_DIM // N_DEV
_FFN_DW_SHARD = D_FF // N_DEV


def kernel(x, w_in, w_conv_dw, b_conv_dw, conv_ln_g, conv_ln_b, w_conv_out, hgrn_lb_logits, hgrn_norm_g, w_hgrn_out, w_out, ln1_g, ln1_b, w_ffn_in, w_ffn_dw, b_ffn_dw, w_ffn_out, ln2_g, ln2_b, loss_target, m_w_in, m_w_conv_dw, m_b_conv_dw, m_conv_ln_g, m_conv_ln_b, m_w_conv_out, m_hgrn_lb_logits, m_hgrn_norm_g, m_w_hgrn_out, m_w_out, m_ln1_g, m_ln1_b, m_w_ffn_in, m_w_ffn_dw, m_b_ffn_dw, m_w_ffn_out, m_ln2_g, m_ln2_b, v_w_in, v_w_conv_dw, v_b_conv_dw, v_conv_ln_g, v_conv_ln_b, v_w_conv_out, v_hgrn_lb_logits, v_hgrn_norm_g, v_w_hgrn_out, v_w_out, v_ln1_g, v_ln1_b, v_w_ffn_in, v_w_ffn_dw, v_b_ffn_dw, v_w_ffn_out, v_ln2_g, v_ln2_b):
    w = dict(w_in=w_in, w_conv_dw=w_conv_dw, b_conv_dw=b_conv_dw, conv_ln_g=conv_ln_g, conv_ln_b=conv_ln_b,
             w_conv_out=w_conv_out, hgrn_lb_logits=hgrn_lb_logits, hgrn_norm_g=hgrn_norm_g, w_hgrn_out=w_hgrn_out,
             w_out=w_out, ln1_g=ln1_g, ln1_b=ln1_b, w_ffn_in=w_ffn_in, w_ffn_dw=w_ffn_dw, b_ffn_dw=b_ffn_dw,
             w_ffn_out=w_ffn_out, ln2_g=ln2_g, ln2_b=ln2_b)
    m = dict(w_in=m_w_in, w_conv_dw=m_w_conv_dw, b_conv_dw=m_b_conv_dw, conv_ln_g=m_conv_ln_g, conv_ln_b=m_conv_ln_b,
             w_conv_out=m_w_conv_out, hgrn_lb_logits=m_hgrn_lb_logits, hgrn_norm_g=m_hgrn_norm_g,
             w_hgrn_out=m_w_hgrn_out, w_out=m_w_out, ln1_g=m_ln1_g, ln1_b=m_ln1_b, w_ffn_in=m_w_ffn_in,
             w_ffn_dw=m_w_ffn_dw, b_ffn_dw=m_b_ffn_dw, w_ffn_out=m_w_ffn_out, ln2_g=m_ln2_g, ln2_b=m_ln2_b)
    v = dict(w_in=v_w_in, w_conv_dw=v_w_conv_dw, b_conv_dw=v_b_conv_dw, conv_ln_g=v_conv_ln_g, conv_ln_b=v_conv_ln_b,
             w_conv_out=v_w_conv_out, hgrn_lb_logits=v_hgrn_lb_logits, hgrn_norm_g=v_hgrn_norm_g,
             w_hgrn_out=v_w_hgrn_out, w_out=v_w_out, ln1_g=v_ln1_g, ln1_b=v_ln1_b, w_ffn_in=v_w_ffn_in,
             w_ffn_dw=v_w_ffn_dw, b_ffn_dw=v_b_ffn_dw, w_ffn_out=v_w_ffn_out, ln2_g=v_ln2_g, ln2_b=v_ln2_b)
    xi, yi, ci = lax.axis_index("x"), lax.axis_index("y"), lax.axis_index("c")
    me = 4 * xi + 2 * yi + ci
    me_op = jnp.reshape(me, (1,)).astype(jnp.int32)

    shards = [w[name][0].astype(BF16) for name in _LARGE]
    shards.append(jnp.pad(w_conv_dw[0], ((0, 1), (0, 128 - _CONV_DW_SHARD))))
    shards.append(jnp.pad(w_ffn_dw[0], ((0, 8 - FFN_K), (0, 384 - _FFN_DW_SHARD))))
    (w_in_all,) = _run_hosted("all_gather_w_in", _gather(shards[:1]))
    small = dict(b_conv_dw=b_conv_dw, conv_ln_g=conv_ln_g, conv_ln_b=conv_ln_b, hgrn_lb_logits=hgrn_lb_logits,
                 hgrn_norm_g=hgrn_norm_g, ln1_g=ln1_g, ln1_b=ln1_b, ln2_g=ln2_g, ln2_b=ln2_b, b_ffn_dw=b_ffn_dw)

    grad_x, large_grads, small_grads = _local_step(x[0], loss_target[0], w_in_all, _gather(shards[1:]), small, _scatter)

    out = {}
    for name, (own, recv) in zip(_LARGE, large_grads):
        out[name] = _adam_large("adam_" + name, own, recv, me_op, w[name][0], m[name][0], v[name][0])

    vec = jnp.concatenate([small_grads[name] for name in _SMALL_ORDER], axis=1)
    total = _small_allreduce(vec.reshape(-1, 128)).reshape(1, -1)
    sizes = [small_grads[name].shape[1] for name in _SMALL_ORDER]
    offs = [0]
    for s in sizes:
        offs.append(offs[-1] + s)
    summed = {name: total[:, offs[i]:offs[i + 1]] for i, name in enumerate(_SMALL_ORDER)}
    loss = summed["loss"][0, 0]
    conv_dw_g = lax.dynamic_slice_in_dim(summed["w_conv_dw"].reshape(CONV_K, CONV_DIM), me * _CONV_DW_SHARD, _CONV_DW_SHARD, axis=1)
    ffn_dw_g = lax.dynamic_slice_in_dim(summed["w_ffn_dw"].reshape(FFN_K, D_FF), me * _FFN_DW_SHARD, _FFN_DW_SHARD, axis=1)
    small_g = dict(summed, w_conv_dw=conv_dw_g.reshape(1, -1), w_ffn_dw=ffn_dw_g.reshape(1, -1))
    names = [n for n in _SMALL_ORDER if n != "loss"]
    flat = lambda d, n: d[n].reshape(1, -1)
    n_small = sum(small_g[n].shape[1] for n in names)
    pad = (-n_small) % 1024
    pack = lambda pieces: jnp.pad(jnp.concatenate(pieces, axis=1), ((0, 0), (0, pad))).reshape(-1, 128)
    d_s, m_s, v_s = _adam_small(pack([flat(w, n) for n in names]), pack([small_g[n] for n in names]),
                                pack([flat(m, n) for n in names]), pack([flat(v, n) for n in names]))
    pos = 0
    for n in names:
        size = small_g[n].shape[1]
        cut = lambda a: a.reshape(1, -1)[:, pos:pos + size].reshape(w[n].shape)
        out[n] = (small_g[n].reshape(w[n].shape), cut(d_s), cut(m_s), cut(v_s))
        pos += size

    for name in _LARGE:
        out[name] = tuple(a.reshape(w[name].shape) for a in out[name])
    grads = [out[n][0] for n in _WEIGHTS]
    deltas = [out[n][1] for n in _WEIGHTS]
    new_m = [out[n][2] for n in _WEIGHTS]
    new_v = [out[n][3] for n in _WEIGHTS]
    return (loss, grad_x[None], *grads, *deltas, *new_m, *new_v)
```

```python
import functools
import math

import jax
import jax.numpy as jnp
from jax import lax
from jax.experimental import pallas as pl
from jax.experimental.pallas import tpu as pltpu

F32 = jnp.float32
BF16 = jnp.bfloat16

N_DEV = 8
D_MODEL = 1024
CONV_DIM = 512
CONV_K = 31
HGRN_DIM = 1024
HEADS = 8
HEAD_DIM = 128
D_FF = 2816
FFN_K = 3
FF_SHARD = 2 * D_FF // N_DEV
IN_COLS = 7168
IN_SHARD = IN_COLS // N_DEV
LN_EPS = 1e-5
RMS_EPS = 1e-6
ALPHA = 2.0 ** 0.25

ADAM_LR = 0.001
ADAM_B1 = 0.9
ADAM_B2 = 0.999
ADAM_EPS = 1e-08
ADAM_WD = 0.01
ADAM_STEP = 10

CHUNK = 64
CHUNKS_PER_BLOCK = 8
CONV_HALO = 32
FFN_HALO = 8
ROW_BLOCK = 64
VMEM_LIMIT = 48 * 1024 * 1024
MXU_DEPTH = 256

MESH = pl.DeviceIdType.MESH
ANY = pl.BlockSpec(memory_space=pl.ANY)

NN = (((1,), (0,)), ((), ()))
NT = (((1,), (1,)), ((), ()))
TN = (((0,), (0,)), ((), ()))


def _params(sem):
    return pltpu.CompilerParams(dimension_semantics=sem, vmem_limit_bytes=VMEM_LIMIT)


def _dot(a, b, dims):
    return lax.dot_general(a.astype(BF16), b.astype(BF16), dims, preferred_element_type=F32)


def _sigmoid(x):
    return jax.nn.sigmoid(x)


def _ln(r):
    mu = jnp.mean(r, axis=-1, keepdims=True)
    xc = r - mu
    var = jnp.mean(xc * xc, axis=-1, keepdims=True)
    rstd = lax.rsqrt(var + LN_EPS)
    return xc * rstd, rstd


def _ln_bwd(dy, xhat, rstd, g):
    dxh = dy * g
    m1 = jnp.mean(dxh, axis=-1, keepdims=True)
    m2 = jnp.mean(dxh * xhat, axis=-1, keepdims=True)
    return rstd * (dxh - m1 - xhat * m2)


def _colsum(x):
    return jnp.sum(x, axis=0, keepdims=True)


class _Hosted:
    def __init__(self, inputs, out_shapes, sem_shapes, start, finish):
        self.inputs, self.out_shapes, self.sem_shapes = list(inputs), list(out_shapes), list(sem_shapes)
        self.start, self.finish = start, finish


def _call(body, *, name, grid, in_specs, out_specs, out_shape, scratch_shapes, semantics, operands, hosted=None):
    if hosted is None:
        return pl.pallas_call(
            body, name=name, grid=grid, in_specs=list(in_specs), out_specs=list(out_specs), out_shape=list(out_shape),
            scratch_shapes=list(scratch_shapes), compiler_params=_params(semantics))(*operands)
    n_in, n_out, n_scr = len(in_specs), len(out_specs), len(scratch_shapes)
    h_in, h_out = len(hosted.inputs), len(hosted.out_shapes)

    def full_body(*refs):
        ins, refs = refs[:n_in], refs[n_in:]
        h_ins, refs = refs[:h_in], refs[h_in:]
        outs, refs = refs[:n_out], refs[n_out:]
        h_outs, refs = refs[:h_out], refs[h_out:]
        scr, sems = refs[:n_scr], refs[n_scr:]
        first = functools.reduce(jnp.logical_and, [pl.program_id(d) == 0 for d in range(len(grid))])
        last = functools.reduce(jnp.logical_and, [pl.program_id(d) == grid[d] - 1 for d in range(len(grid))])

        @pl.when(first)
        def _():
            hosted.start(h_ins, h_outs, sems)

        body(*ins, *outs, *scr)

        @pl.when(last)
        def _():
            hosted.finish(h_ins, h_outs, sems)

    return pl.pallas_call(
        full_body, name=name, grid=grid, in_specs=list(in_specs) + [ANY] * h_in,
        out_specs=list(out_specs) + [ANY] * h_out, out_shape=list(out_shape) + hosted.out_shapes,
        scratch_shapes=list(scratch_shapes) + hosted.sem_shapes,
        compiler_params=pltpu.CompilerParams(dimension_semantics=("arbitrary",) * len(grid),
                                             vmem_limit_bytes=VMEM_LIMIT, has_side_effects=True),
    )(*operands, *hosted.inputs)


def _run_hosted(name, hosted):
    h_in, h_out = len(hosted.inputs), len(hosted.out_shapes)

    def body(*refs):
        ins, outs, sems = refs[:h_in], refs[h_in:h_in + h_out], refs[h_in + h_out:]
        hosted.start(ins, outs, sems)
        hosted.finish(ins, outs, sems)

    return pl.pallas_call(
        body, name=name, in_specs=[ANY] * h_in, out_specs=[ANY] * h_out, out_shape=hosted.out_shapes,
        scratch_shapes=hosted.sem_shapes, compiler_params=pltpu.CompilerParams(has_side_effects=True),
    )(*hosted.inputs)


def _mm(name, a, b, out_shape, out_dtype, grid, a_spec, b_spec, o_spec, dims, acc_shape, hosted=None):
    nk = grid[2]
    if nk == 1:
        def body(a_ref, b_ref, o_ref):
            o_ref[...] = _dot(a_ref[...], b_ref[...], dims).astype(o_ref.dtype)
        scratch = []
    else:
        def body(a_ref, b_ref, o_ref, acc_ref):
            k = pl.program_id(2)

            @pl.when(k == 0)
            def _():
                acc_ref[...] = jnp.zeros_like(acc_ref)

            acc_ref[...] += _dot(a_ref[...], b_ref[...], dims)

            @pl.when(k == nk - 1)
            def _():
                o_ref[...] = acc_ref[...].astype(o_ref.dtype)
        scratch = [pltpu.VMEM(acc_shape, F32)]

    outs = _call(body, name=name, grid=grid, in_specs=[a_spec, b_spec], out_specs=[o_spec],
                 out_shape=[jax.ShapeDtypeStruct(out_shape, out_dtype)], scratch_shapes=scratch,
                 semantics=("parallel", "parallel", "arbitrary"), operands=(a, b), hosted=hosted)
    return outs[0] if hosted is None else (outs[0], list(outs[1:]))


def _mm_fused(name, grid, operands, in_specs, out_shape, out_specs, dims, acc_shape, epilogue, lhs=None, scratch=(),
              hosted=None):
    nk = grid[2]
    n_in, n_out = len(in_specs), len(out_specs)
    own_scratch = [] if nk == 1 else [pltpu.VMEM(acc_shape, F32)]

    def body(*refs):
        ins, outs, scr = refs[:n_in], refs[n_in:n_in + n_out], refs[n_in + n_out:]
        a = ins[0][...] if lhs is None else lhs(ins, outs)
        part = _dot(a, ins[1][...], dims)
        if nk == 1:
            epilogue(part, ins, outs, scr)
            return
        acc_ref, k = scr[0], pl.program_id(2)

        @pl.when(k == 0)
        def _():
            acc_ref[...] = jnp.zeros_like(acc_ref)

        acc_ref[...] += part

        @pl.when(k == nk - 1)
        def _():
            epilogue(acc_ref[...], ins, outs, scr[1:])

    return _call(body, name=name, grid=grid, in_specs=in_specs, out_specs=out_specs, out_shape=out_shape,
                 scratch_shapes=own_scratch + list(scratch), semantics=("arbitrary",) * 3, operands=operands,
                 hosted=hosted)


def _pick(t, pref):
    return pref if t % pref == 0 else t


def _glu(p):
    return p[:, :CONV_DIM] * _sigmoid(p[:, CONV_DIM:])


def _conv_fwd(proj, w_dw, b_dw, g, b):
    t = proj.shape[0]
    tm = _pick(t, 256)
    nh = tm // CONV_HALO

    def body(p_ref, ph_ref, w_ref, bd_ref, g_ref, b_ref, act_ref, pre_ref, xs_ref):
        i = pl.program_id(0)
        halo = _glu(ph_ref[...])
        xs_ref[0:CONV_HALO, :] = jnp.where(i == 0, 0.0, halo)
        xs_ref[CONV_HALO:CONV_HALO + tm, :] = _glu(p_ref[...])
        for r in range(tm // ROW_BLOCK):
            acc = jnp.broadcast_to(bd_ref[...], (ROW_BLOCK, CONV_DIM))
            for k in range(CONV_K):
                off = r * ROW_BLOCK + CONV_HALO - (CONV_K - 1) + k
                acc = acc + w_ref[k:k + 1, :] * xs_ref[off:off + ROW_BLOCK, :]
            rows = slice(r * ROW_BLOCK, (r + 1) * ROW_BLOCK)
            pre_ref[rows, :] = acc
            xhat, _ = _ln(acc)
            yln = xhat * g_ref[...] + b_ref[...]
            act_ref[rows, :] = (yln * _sigmoid(yln)).astype(BF16)

    full = lambda s: pl.BlockSpec(s, lambda i: (0, 0))
    return pl.pallas_call(
        body, name="conv_fwd", grid=(t // tm,),
        in_specs=[pl.BlockSpec((tm, 2 * CONV_DIM), lambda i: (i, 0)),
                  pl.BlockSpec((CONV_HALO, 2 * CONV_DIM), lambda i: (jnp.maximum(i * nh - 1, 0), 0)),
                  full((CONV_K, CONV_DIM)), full((1, CONV_DIM)), full((1, CONV_DIM)), full((1, CONV_DIM))],
        out_specs=[pl.BlockSpec((tm, CONV_DIM), lambda i: (i, 0)), pl.BlockSpec((tm, CONV_DIM), lambda i: (i, 0))],
        out_shape=[jax.ShapeDtypeStruct((t, CONV_DIM), BF16), jax.ShapeDtypeStruct((t, CONV_DIM), F32)],
        scratch_shapes=[pltpu.VMEM((CONV_HALO + tm, CONV_DIM), F32)],
        compiler_params=_params(("arbitrary",)),
    )(proj, proj, w_dw, b_dw, g, b)


def _d_c_norm_bwd(d_y, w_conv_out, pre, g, b):
    t = pre.shape[0]
    tm = _pick(t, 512)
    d = D_MODEL

    def epilogue(d_c, ins, outs, scr):
        pre_ref, g_ref, b_ref = ins[2:]
        dpre_ref, sums_ref = outs
        i = pl.program_id(0)

        @pl.when(i == 0)
        def _():
            sums_ref[...] = jnp.zeros_like(sums_ref)

        xhat, rstd = _ln(pre_ref[...])
        yln = xhat * g_ref[...] + b_ref[...]
        sg = _sigmoid(yln)
        dyln = d_c * (sg * (1.0 + yln * (1.0 - sg)))
        dpre = _ln_bwd(dyln, xhat, rstd, g_ref[...])
        dpre_ref[...] = dpre
        sums_ref[0:1, :] += _colsum(dyln * xhat)
        sums_ref[1:2, :] += _colsum(dyln)
        sums_ref[2:3, :] += _colsum(dpre)

    full = lambda s: pl.BlockSpec(s, lambda i, j, k: (0, 0))
    tile = pl.BlockSpec((tm, CONV_DIM), lambda i, j, k: (i, 0))
    return _mm_fused(
        "d_c_norm_bwd", (t // tm, 1, 1), (d_y, w_conv_out, pre, g, b),
        [pl.BlockSpec((None, tm, d), lambda i, j, k: (0, i, 0)), full((CONV_DIM, d)), tile,
         full((1, CONV_DIM)), full((1, CONV_DIM))],
        [jax.ShapeDtypeStruct((t, CONV_DIM), F32), jax.ShapeDtypeStruct((8, CONV_DIM), F32)],
        [tile, full((8, CONV_DIM))], NT, (tm, CONV_DIM), epilogue)


def _conv_bwd_dw(d_pre, proj, w_dw):
    t = d_pre.shape[0]
    tm = _pick(t, 256)
    nt = t // tm
    nh = tm // CONV_HALO
    last_h = t // CONV_HALO - 1

    def body(dp_ref, dph_ref, p_ref, ph_ref, w_ref, dproj_ref, dw_ref, xs_ref, ds_ref):
        i = pl.program_id(0)

        @pl.when(i == 0)
        def _():
            dw_ref[...] = jnp.zeros_like(dw_ref)

        halo = _glu(ph_ref[...])
        xs_ref[0:CONV_HALO, :] = jnp.where(i == 0, 0.0, halo)
        xs_ref[CONV_HALO:CONV_HALO + tm, :] = _glu(p_ref[...])
        ds_ref[0:tm, :] = dp_ref[...]
        ds_ref[tm:tm + CONV_HALO, :] = jnp.where(i == nt - 1, 0.0, dph_ref[...])
        for r in range(tm // ROW_BLOCK):
            base = r * ROW_BLOCK
            dpre = ds_ref[base:base + ROW_BLOCK, :]
            acc = jnp.zeros((ROW_BLOCK, CONV_DIM), F32)
            for k in range(CONV_K):
                acc = acc + w_ref[k:k + 1, :] * ds_ref[base + CONV_K - 1 - k:base + CONV_K - 1 - k + ROW_BLOCK, :]
                off = base + CONV_HALO - (CONV_K - 1) + k
                dw_ref[k:k + 1, :] += _colsum(dpre * xs_ref[off:off + ROW_BLOCK, :])
            rows = slice(base, base + ROW_BLOCK)
            cval = p_ref[rows, 0:CONV_DIM]
            sg = _sigmoid(p_ref[rows, CONV_DIM:2 * CONV_DIM])
            dproj_ref[rows, 0:CONV_DIM] = (acc * sg).astype(BF16)
            dproj_ref[rows, CONV_DIM:2 * CONV_DIM] = (acc * cval * sg * (1.0 - sg)).astype(BF16)

    full = lambda s: pl.BlockSpec(s, lambda i: (0, 0))
    return pl.pallas_call(
        body, name="conv_bwd_dw", grid=(nt,),
        in_specs=[pl.BlockSpec((tm, CONV_DIM), lambda i: (i, 0)),
                  pl.BlockSpec((CONV_HALO, CONV_DIM), lambda i: (jnp.minimum((i + 1) * nh, last_h), 0)),
                  pl.BlockSpec((tm, 2 * CONV_DIM), lambda i: (i, 0)),
                  pl.BlockSpec((CONV_HALO, 2 * CONV_DIM), lambda i: (jnp.maximum(i * nh - 1, 0), 0)),
                  full((CONV_K, CONV_DIM))],
        out_specs=[pl.BlockSpec((tm, 2 * CONV_DIM), lambda i: (i, 0)), full((CONV_HALO, CONV_DIM))],
        out_shape=[jax.ShapeDtypeStruct((t, 2 * CONV_DIM), BF16), jax.ShapeDtypeStruct((CONV_HALO, CONV_DIM), F32)],
        scratch_shapes=[pltpu.VMEM((CONV_HALO + tm, CONV_DIM), F32), pltpu.VMEM((tm + CONV_HALO, CONV_DIM), F32)],
        compiler_params=_params(("arbitrary",)),
    )(d_pre, d_pre, proj, proj, w_dw)


def _lower_bound(logit_ref):
    l0 = logit_ref[0:1, :]
    l1 = logit_ref[1:2, :]
    m = jnp.maximum(l0, l1)
    e0 = jnp.exp(l0 - m)
    e1 = jnp.exp(l1 - m)
    return e0 / (e0 + e1)


def _tri(lower):
    r = lax.broadcasted_iota(jnp.int32, (CHUNK, CHUNK), 0)
    c = lax.broadcasted_iota(jnp.int32, (CHUNK, CHUNK), 1)
    return (c <= r) if lower else (c >= r)


def _hgrn_gates(fz, lb):
    s = _sigmoid(fz)
    sn = _sigmoid(-fz)
    f = lb + (1.0 - lb) * s
    return s, sn, f


def _block_tri(rows, lower=True):
    r = lax.broadcasted_iota(jnp.int32, (rows, rows), 0)
    c = lax.broadcasted_iota(jnp.int32, (rows, rows), 1)
    tri = (c <= r) if lower else (c >= r)
    return (tri & (r // CHUNK == c // CHUNK)).astype(BF16)


def _tri_rows(tm):
    return min(tm, MXU_DEPTH)


def _tri_matmul(tri_ref, x):
    hi = x.astype(BF16)
    lo = (x - hi.astype(F32)).astype(BF16)
    tri = tri_ref[...]
    return (lax.dot_general(tri, hi, NN, preferred_element_type=F32)
            + lax.dot_general(tri, lo, NN, preferred_element_type=F32))


def _groups(tm):
    g = _tri_rows(tm)
    return [slice(i * g, (i + 1) * g) for i in range(tm // g)]


def _hgrn_fwd(proj, logits, norm_g):
    t = proj.shape[0]
    tm = CHUNK * CHUNKS_PER_BLOCK if t % (CHUNK * CHUNKS_PER_BLOCK) == 0 else CHUNK
    cpb = tm // CHUNK
    nt = t // tm
    half = CHUNK // 2

    def body(qz_ref, fz_ref, iv_ref, gz_ref, lg_ref, ng_ref, tri_ref, o_ref, og_ref, st_ref,
             state_ref, qe_ref, ke_ref, qb_ref, kl_ref, v_ref, upd_ref, decay_ref, a_ref, q_ref, kk_ref, b_ref):
        j = pl.program_id(1)

        @pl.when(j == 0)
        def _():
            state_ref[...] = jnp.zeros_like(state_ref)

        lb = _lower_bound(lg_ref)
        chunks = [slice(c * CHUNK, (c + 1) * CHUNK) for c in range(cpb)]
        for rows in chunks:
            qz = qz_ref[rows, :]
            q_ref[rows, :] = qz * _sigmoid(qz)
            _, sn, f = _hgrn_gates(fz_ref[rows, :], lb)
            kk_ref[rows, :] = (1.0 - lb) * sn
            b_ref[rows, :] = jnp.log(f)
            v_ref[rows, :] = iv_ref[rows, :].astype(BF16)
        for rows in _groups(tm):
            b_ref[rows, :] = _tri_matmul(tri_ref, b_ref[rows, :])
        for c, rows in enumerate(chunks):
            b = b_ref[rows, :]
            bref = b[half - 1:half, :]
            blast = b[CHUNK - 1:CHUNK, :]
            q = q_ref[rows, :]
            kk = kk_ref[rows, :]
            qb_ref[rows, :] = (q * jnp.exp(b)).astype(BF16)
            qe_ref[rows, :] = (q * jnp.exp(b - bref)).astype(BF16)
            ke_ref[rows, :] = (kk * jnp.exp(bref - b)).astype(BF16)
            kl_ref[rows, :] = (kk * jnp.exp(blast - b)).astype(BF16)
            decay_ref[c:c + 1, :] = jnp.exp(blast)
        causal = _tri(True)
        for c, rows in enumerate(chunks):
            upd_ref[c] = _dot(v_ref[rows, :], kl_ref[rows, :], TN)
            a_ref[c] = jnp.where(causal, _dot(qe_ref[rows, :], ke_ref[rows, :], NT), 0.0).astype(BF16)
        state = state_ref[...]
        for c in range(cpb):
            st_ref[c] = state.astype(BF16)
            state = state * decay_ref[c:c + 1, :] + upd_ref[c]
        state_ref[...] = state
        for c, rows in enumerate(chunks):
            o_ref[rows, :] = _dot(a_ref[c], v_ref[rows, :], NN) + _dot(qb_ref[rows, :], st_ref[c], NT)
        for rows in chunks:
            o = o_ref[rows, :]
            r = lax.rsqrt(jnp.mean(o * o, axis=-1, keepdims=True) + RMS_EPS)
            gz = gz_ref[rows, :]
            og_ref[rows, :] = ((o * r * ng_ref[...]) * (gz * _sigmoid(gz))).astype(BF16)

    col = lambda base: pl.BlockSpec((tm, HEAD_DIM), lambda h, j: (j, base + h))
    tile_bf = pltpu.VMEM((tm, HEAD_DIM), BF16)
    tile_f32 = pltpu.VMEM((tm, HEAD_DIM), F32)
    return pl.pallas_call(
        body, name="hgrn_fwd", grid=(HEADS, nt),
        in_specs=[col(8), col(16), col(24), col(32),
                  pl.BlockSpec((2, HEAD_DIM), lambda h, j: (0, h)), pl.BlockSpec((1, HEAD_DIM), lambda h, j: (0, h)),
                  pl.BlockSpec((_tri_rows(tm), _tri_rows(tm)), lambda h, j: (0, 0))],
        out_specs=[col(0), col(0), pl.BlockSpec((None, cpb, HEAD_DIM, HEAD_DIM), lambda h, j: (h, j, 0, 0))],
        out_shape=[jax.ShapeDtypeStruct((t, HGRN_DIM), F32), jax.ShapeDtypeStruct((t, HGRN_DIM), BF16),
                   jax.ShapeDtypeStruct((HEADS, t // CHUNK, HEAD_DIM, HEAD_DIM), BF16)],
        scratch_shapes=[pltpu.VMEM((HEAD_DIM, HEAD_DIM), F32), tile_bf, tile_bf, tile_bf, tile_bf, tile_bf,
                        pltpu.VMEM((cpb, HEAD_DIM, HEAD_DIM), F32), pltpu.VMEM((max(cpb, 8), HEAD_DIM), F32),
                        pltpu.VMEM((cpb, CHUNK, CHUNK), BF16), tile_f32, tile_f32, tile_f32],
        compiler_params=_params(("parallel", "arbitrary")),
    )(proj, proj, proj, proj, logits, norm_g, _block_tri(_tri_rows(tm)))


def _hgrn_bwd(d_og, o, proj, states, logits, norm_g, hosted=None):
    t = proj.shape[0]
    tm = CHUNK * CHUNKS_PER_BLOCK if t % (CHUNK * CHUNKS_PER_BLOCK) == 0 else CHUNK
    cpb = tm // CHUNK
    nt = t // tm
    half = CHUNK // 2

    def body(dog_ref, o_ref, qz_ref, fz_ref, iv_ref, gz_ref, st_ref, lg_ref, ng_ref, tril_ref, triu_ref,
             dqz_ref, dfz_ref, div_ref, dgz_ref, sums_ref,
             dstate_ref, qe_ref, ke_ref, qb_ref, kl_ref, v_ref, do_ref, upd_ref, dst_ref, a_ref, da_ref,
             decay_ref, through_ref, q_ref, kk_ref, b_ref, dsilu_ref, gs_ref, gf_ref, sn_ref,
             eb_ref, ebr_ref, ekr_ref, ebl_ref, rev_ref, pre_ref, dk_ref):
        j = pl.program_id(1)

        @pl.when(j == 0)
        def _():
            dstate_ref[...] = jnp.zeros_like(dstate_ref)
            sums_ref[...] = jnp.zeros_like(sums_ref)

        lb = _lower_bound(lg_ref)
        ng = ng_ref[...]
        chunks = [slice(c * CHUNK, (c + 1) * CHUNK) for c in range(cpb)]
        for rows in chunks:
            qz = qz_ref[rows, :]
            sq = _sigmoid(qz)
            q_ref[rows, :] = qz * sq
            dsilu_ref[rows, :] = sq * (1.0 + qz * (1.0 - sq))
            s, sn, f = _hgrn_gates(fz_ref[rows, :], lb)
            kk_ref[rows, :] = (1.0 - lb) * sn
            b_ref[rows, :] = jnp.log(f)
            sn_ref[rows, :] = sn
            gf_ref[rows, :] = sn / f
            gs_ref[rows, :] = (1.0 - lb) * s
            v_ref[rows, :] = iv_ref[rows, :].astype(BF16)
            ov = o_ref[rows, :]
            r = lax.rsqrt(jnp.mean(ov * ov, axis=-1, keepdims=True) + RMS_EPS)
            on = ov * r
            gz = gz_ref[rows, :]
            sg = _sigmoid(gz)
            dog = dog_ref[rows, :]
            dgz_ref[rows, :] = (dog * (on * ng) * (sg * (1.0 + gz * (1.0 - sg)))).astype(BF16)
            d_ong = dog * (gz * sg)
            sums_ref[0:1, :] += _colsum(d_ong * on)
            d_on = d_ong * ng
            do_ref[rows, :] = (r * (d_on - on * jnp.mean(d_on * on, axis=-1, keepdims=True))).astype(BF16)
        for rows in _groups(tm):
            b_ref[rows, :] = _tri_matmul(tril_ref, b_ref[rows, :])
        for c, rows in enumerate(chunks):
            b = b_ref[rows, :]
            bref = b[half - 1:half, :]
            blast = b[CHUNK - 1:CHUNK, :]
            q = q_ref[rows, :]
            kk = kk_ref[rows, :]
            eb = jnp.exp(b)
            ebr = jnp.exp(b - bref)
            ekr = jnp.exp(bref - b)
            ebl = jnp.exp(blast - b)
            eb_ref[rows, :] = eb
            ebr_ref[rows, :] = ebr
            ekr_ref[rows, :] = ekr
            ebl_ref[rows, :] = ebl
            qb_ref[rows, :] = (q * eb).astype(BF16)
            qe_ref[rows, :] = (q * ebr).astype(BF16)
            ke_ref[rows, :] = (kk * ekr).astype(BF16)
            kl_ref[rows, :] = (kk * ebl).astype(BF16)
            decay_ref[c:c + 1, :] = jnp.exp(blast)
        causal = _tri(True)
        for c, rows in enumerate(chunks):
            upd_ref[c] = _dot(do_ref[rows, :], qb_ref[rows, :], TN)
            a_ref[c] = jnp.where(causal, _dot(qe_ref[rows, :], ke_ref[rows, :], NT), 0.0).astype(BF16)
            da_ref[c] = jnp.where(causal, _dot(do_ref[rows, :], v_ref[rows, :], NT), 0.0).astype(BF16)
        dstate = dstate_ref[...]
        for c in reversed(range(cpb)):
            dst_ref[c] = dstate.astype(BF16)
            decay = decay_ref[c:c + 1, :]
            through_ref[c:c + 1, :] = decay * _colsum(dstate * st_ref[c].astype(F32))
            dstate = dstate * decay + upd_ref[c]
        dstate_ref[...] = dstate
        for c, rows in enumerate(chunks):
            qe, ke, v, do = qe_ref[rows, :], ke_ref[rows, :], v_ref[rows, :], do_ref[rows, :]
            div_ref[rows, :] = (_dot(a_ref[c], do, TN) + _dot(kl_ref[rows, :], dst_ref[c], NT)).astype(BF16)
            dqe = _dot(da_ref[c], ke, NN)
            dke = _dot(da_ref[c], qe, TN)
            dq_inter = _dot(do, st_ref[c], NN) * eb_ref[rows, :]
            dk_inter = _dot(v, dst_ref[c], NN) * ebl_ref[rows, :]
            dk_ref[rows, :] = dke * ekr_ref[rows, :] + dk_inter
            dqz_ref[rows, :] = ((dqe * ebr_ref[rows, :] + dq_inter) * dsilu_ref[rows, :]).astype(BF16)
            rev_ref[rows, :] = (qe.astype(F32) * dqe - ke.astype(F32) * dke) + q_ref[rows, :] * dq_inter
            pre_ref[rows, :] = kk_ref[rows, :] * dk_inter
        for rows in _groups(tm):
            pre = pre_ref[rows, :]
            rev_ref[rows, :] = _tri_matmul(triu_ref, rev_ref[rows, :]) + (_tri_matmul(tril_ref, pre) - pre)
        for c, rows in enumerate(chunks):
            dlf = rev_ref[rows, :] + through_ref[c:c + 1, :]
            common = gf_ref[rows, :] * dlf - sn_ref[rows, :] * dk_ref[rows, :]
            dfz_ref[rows, :] = (gs_ref[rows, :] * common).astype(BF16)
            sums_ref[1:2, :] += _colsum(common)

        @pl.when(j == nt - 1)
        def _():
            sums_ref[1:2, :] = sums_ref[1:2, :] * lb * (1.0 - lb)

    rev = lambda base: pl.BlockSpec((tm, HEAD_DIM), lambda h, j: (nt - 1 - j, base + h))
    vec = lambda n: pl.BlockSpec((n, HEAD_DIM), lambda h, j: (0, h))
    const = pl.BlockSpec((_tri_rows(tm), _tri_rows(tm)), lambda h, j: (0, 0))
    bf = jax.ShapeDtypeStruct((t, HGRN_DIM), BF16)
    tile_bf = pltpu.VMEM((tm, HEAD_DIM), BF16)
    tile_f32 = pltpu.VMEM((tm, HEAD_DIM), F32)
    square = lambda dtype: pltpu.VMEM((cpb, HEAD_DIM, HEAD_DIM), dtype)
    rows8 = pltpu.VMEM((max(cpb, 8), HEAD_DIM), F32)
    return _call(
        body, name="hgrn_bwd", grid=(HEADS, nt),
        in_specs=[rev(0), rev(0), rev(8), rev(16), rev(24), rev(32),
                  pl.BlockSpec((None, cpb, HEAD_DIM, HEAD_DIM), lambda h, j: (h, nt - 1 - j, 0, 0)),
                  vec(2), vec(1), const, const],
        out_specs=[rev(0), rev(0), rev(0), rev(0), vec(8)],
        out_shape=[bf, bf, bf, bf, jax.ShapeDtypeStruct((8, HGRN_DIM), F32)],
        scratch_shapes=[pltpu.VMEM((HEAD_DIM, HEAD_DIM), F32)] + [tile_bf] * 6 + [square(F32), square(BF16)]
        + [pltpu.VMEM((cpb, CHUNK, CHUNK), BF16)] * 2 + [rows8, rows8] + [tile_f32] * 14,
        semantics=("parallel", "arbitrary"),
        operands=(d_og, o, proj, proj, proj, proj, states, logits, norm_g, _block_tri(_tri_rows(tm)), _block_tri(_tri_rows(tm), lower=False)),
        hosted=hosted)


def _mix_ln1(proj, y_conv, y_hgrn, w_out, x, g, b):
    t = x.shape[0]
    tm = _pick(t, 512)
    d = D_MODEL

    def lhs(ins, outs):
        mixed = (_sigmoid(ins[0][...]) * ins[3][...] + _sigmoid(ins[2][...]) * ins[4][...]).astype(BF16)
        outs[0][...] = mixed
        return mixed

    def epilogue(acc, ins, outs, scr):
        r = ALPHA * ins[5][...] + acc
        outs[1][...] = r
        xhat, _ = _ln(r)
        outs[2][...] = (xhat * ins[6][...] + ins[7][...]).astype(BF16)

    tile = pl.BlockSpec((tm, d), lambda i, j, k: (i, 0))
    vec = pl.BlockSpec((1, d), lambda i, j, k: (0, 0))
    return _mm_fused(
        "mix_ln1", (t // tm, 1, 1), (proj, w_out, proj, y_conv, y_hgrn, x, g, b),
        [pl.BlockSpec((tm, d), lambda i, j, k: (i, 5)), pl.BlockSpec((d, d), lambda i, j, k: (0, 0)),
         pl.BlockSpec((tm, d), lambda i, j, k: (i, 6)), tile, tile, tile, vec, vec],
        [jax.ShapeDtypeStruct((t, d), BF16), jax.ShapeDtypeStruct((t, d), F32), jax.ShapeDtypeStruct((t, d), BF16)],
        [tile, tile, tile], NN, (tm, d), epilogue, lhs=lhs)


def _merge_bwd(d_mixed, proj, y_conv, y_hgrn):
    t = proj.shape[0]
    tm = _pick(t, 512)

    def body(dm_ref, m_ref, yc_ref, yh_ref, dy_ref, dmz_ref):
        br = pl.program_id(1)
        y = jnp.where(br == 0, yc_ref[...], yh_ref[...])
        sg = _sigmoid(m_ref[...])
        dm = dm_ref[...]
        dy_ref[...] = (sg * dm).astype(BF16)
        dmz_ref[...] = (dm * y * sg * (1.0 - sg)).astype(BF16)

    tile = pl.BlockSpec((tm, D_MODEL), lambda i, br: (i, 0))
    return pl.pallas_call(
        body, name="merge_bwd", grid=(t // tm, 2),
        in_specs=[tile, pl.BlockSpec((tm, D_MODEL), lambda i, br: (i, 5 + br)), tile, tile],
        out_specs=[pl.BlockSpec((None, tm, D_MODEL), lambda i, br: (br, i, 0)), pl.BlockSpec((tm, D_MODEL), lambda i, br: (i, br))],
        out_shape=[jax.ShapeDtypeStruct((2, t, D_MODEL), BF16), jax.ShapeDtypeStruct((t, 2 * D_MODEL), BF16)],
        compiler_params=_params(("parallel", "arbitrary")),
    )(d_mixed, proj, y_conv, y_hgrn)


def _ffn_out_ln2(act, w_ffn_out, r1, target, g1, b1, g2, b2):
    t = r1.shape[0]
    tm = _pick(t, 512)
    nt = t // tm
    d = D_MODEL

    def epilogue(y_ffn, ins, outs, scr):
        r1_ref, tg_ref, g1_ref, b1_ref, g2_ref, b2_ref = ins[2:]
        dr_ref, drb_ref, sums_ref = outs
        (sq_ref,) = scr
        i = pl.program_id(0)

        @pl.when(i == 0)
        def _():
            sums_ref[...] = jnp.zeros_like(sums_ref)
            sq_ref[...] = jnp.zeros_like(sq_ref)

        xh1, _ = _ln(r1_ref[...])
        x1 = xh1 * g1_ref[...] + b1_ref[...]
        xh2, rstd2 = _ln(ALPHA * x1 + y_ffn)
        diff = xh2 * g2_ref[...] + b2_ref[...] - tg_ref[...]
        dy = diff * (1.0 / D_MODEL)
        dr = _ln_bwd(dy, xh2, rstd2, g2_ref[...])
        dr_ref[...] = dr
        drb_ref[...] = dr.astype(BF16)
        sums_ref[0:1, :] += _colsum(dy * xh2)
        sums_ref[1:2, :] += _colsum(dy)
        sq_ref[...] += _colsum(diff * diff)

        @pl.when(i == nt - 1)
        def _():
            total = jnp.sum(sq_ref[...], axis=-1, keepdims=True) * (0.5 / D_MODEL)
            sums_ref[2:3, :] = jnp.broadcast_to(total, (1, D_MODEL))

    tile = pl.BlockSpec((tm, d), lambda i, j, k: (i, 0))
    vec = pl.BlockSpec((1, d), lambda i, j, k: (0, 0))
    return _mm_fused(
        "ffn_out_ln2", (nt, 1, 4), (act, w_ffn_out, r1, target, g1, b1, g2, b2),
        [pl.BlockSpec((None, tm, FF_SHARD), lambda i, j, k: (k, i, 0)),
         pl.BlockSpec((None, FF_SHARD, d), lambda i, j, k: (k, 0, 0)), tile, tile, vec, vec, vec, vec],
        [jax.ShapeDtypeStruct((t, d), F32), jax.ShapeDtypeStruct((t, d), BF16), jax.ShapeDtypeStruct((8, d), F32)],
        [tile, tile, pl.BlockSpec((8, d), lambda i, j, k: (0, 0))], NN, (tm, d), epilogue,
        scratch=[pltpu.VMEM((1, d), F32)])


def _d_x1_ln1_bwd(d_z, w_ffn_in, d_r2, r1, g1):
    t = r1.shape[0]
    tm = _pick(t, 512)
    d = D_MODEL

    def epilogue(dx_ffn, ins, outs, scr):
        dr2_ref, r1_ref, g_ref = ins[2:]
        dr1_ref, dr1b_ref, sums_ref = outs
        i = pl.program_id(0)

        @pl.when(i == 0)
        def _():
            sums_ref[...] = jnp.zeros_like(sums_ref)

        xhat, rstd = _ln(r1_ref[...])
        dx1 = ALPHA * dr2_ref[...] + dx_ffn
        dr1 = _ln_bwd(dx1, xhat, rstd, g_ref[...])
        dr1_ref[...] = dr1
        dr1b_ref[...] = dr1.astype(BF16)
        sums_ref[0:1, :] += _colsum(dx1 * xhat)
        sums_ref[1:2, :] += _colsum(dx1)

    tile = pl.BlockSpec((tm, d), lambda i, j, k: (i, 0))
    return _mm_fused(
        "d_x1_ln1_bwd", (t // tm, 1, N_DEV), (d_z, w_ffn_in, d_r2, r1, g1),
        [pl.BlockSpec((None, tm, FF_SHARD), lambda i, j, k: (k, i, 0)),
         pl.BlockSpec((None, d, FF_SHARD), lambda i, j, k: (k, 0, 0)), tile, tile,
         pl.BlockSpec((1, d), lambda i, j, k: (0, 0))],
        [jax.ShapeDtypeStruct((t, d), F32), jax.ShapeDtypeStruct((t, d), BF16), jax.ShapeDtypeStruct((8, d), F32)],
        [tile, tile, pl.BlockSpec((8, d), lambda i, j, k: (0, 0))], NT, (tm, d), epilogue)


def _cast_bf16(x):
    t = x.shape[0]
    tm = _pick(t, 512)

    def body(x_ref, o_ref):
        o_ref[...] = x_ref[...].astype(BF16)

    tile = pl.BlockSpec((tm, D_MODEL), lambda i: (i, 0))
    return pl.pallas_call(
        body, name="cast_x", grid=(t // tm,), in_specs=[tile], out_specs=tile,
        out_shape=jax.ShapeDtypeStruct((t, D_MODEL), BF16), compiler_params=_params(("parallel",)),
    )(x)


def _relayout(name, a, in_block, in_map, out_block, out_map, out_shape):
    def body(a_ref, o_ref):
        o_ref[...] = a_ref[...].astype(o_ref.dtype)

    return pl.pallas_call(
        body, name=name, grid=(N_DEV,), in_specs=[pl.BlockSpec(in_block, in_map)],
        out_specs=pl.BlockSpec(out_block, out_map), out_shape=out_shape, compiler_params=_params(("parallel",)),
    )(a)


_GELU_C = math.sqrt(2.0 / math.pi)


_GELU_CUBIC = 0.044715


def _gelu_parts(u):
    u2 = u * u
    th = jnp.tanh(u * (_GELU_C + (_GELU_C * _GELU_CUBIC) * u2))
    hu = 0.5 * u
    return th, hu + hu * th, u2, hu


def _ffn_act_fwd(z, w_dw, b_dw):
    t = z.shape[2]
    tm = _pick(t, 256)
    nh = tm // FFN_HALO

    def body(z_ref, zh_ref, w_ref, b_ref, act_ref, us_ref):
        i = pl.program_id(1)
        us_ref[0:FFN_HALO, :] = jnp.where(i == 0, 0.0, zh_ref[...])
        us_ref[FFN_HALO:FFN_HALO + tm, :] = z_ref[0]
        for r in range(tm // ROW_BLOCK):
            base = r * ROW_BLOCK
            uc = jnp.broadcast_to(b_ref[...], (ROW_BLOCK, FF_SHARD))
            for k in range(FFN_K):
                off = base + FFN_HALO - (FFN_K - 1) + k
                uc = uc + w_ref[k:k + 1, :] * us_ref[off:off + ROW_BLOCK, :]
            gelu = _gelu_parts(uc)[1]
            act_ref[base:base + ROW_BLOCK, :] = (gelu * z_ref[1, base:base + ROW_BLOCK, :]).astype(BF16)

    return pl.pallas_call(
        body, name="ffn_act_fwd", grid=(4, t // tm),
        in_specs=[pl.BlockSpec((2, None, tm, FF_SHARD), lambda j, i: (0, j, i, 0)),
                  pl.BlockSpec((None, None, FFN_HALO, FF_SHARD), lambda j, i: (0, j, jnp.maximum(i * nh - 1, 0), 0)),
                  pl.BlockSpec((None, FFN_K, FF_SHARD), lambda j, i: (j, 0, 0)),
                  pl.BlockSpec((None, 1, FF_SHARD), lambda j, i: (j, 0, 0))],
        out_specs=pl.BlockSpec((None, tm, FF_SHARD), lambda j, i: (j, i, 0)),
        out_shape=jax.ShapeDtypeStruct((4, t, FF_SHARD), BF16),
        scratch_shapes=[pltpu.VMEM((FFN_HALO + tm, FF_SHARD), F32)],
        compiler_params=_params(("parallel", "arbitrary")),
    )(z, z, w_dw, b_dw)


def _ffn_act_bwd(d_act, z, w_dw, b_dw):
    t = z.shape[2]
    tm = _pick(t, 256)
    nt = t // tm
    nh = tm // FFN_HALO
    last_h = t // FFN_HALO - 1
    pad = FFN_HALO - (FFN_K - 1)

    def gelu_grad(uc):
        th, gelu, u2, hu = _gelu_parts(uc)
        dgelu = (0.5 + 0.5 * th) + (hu - hu * th * th) * (_GELU_C + (3.0 * _GELU_C * _GELU_CUBIC) * u2)
        return gelu, dgelu

    def body(da_ref, dah_ref, z_ref, zp_ref, zn_ref, w_ref, b_ref, dz_ref, sums_ref, us_ref, ds_ref):
        i = pl.program_id(1)

        @pl.when(i == 0)
        def _():
            sums_ref[...] = jnp.zeros_like(sums_ref)

        us_ref[0:FFN_HALO, :] = jnp.where(i == 0, 0.0, zp_ref[...])
        us_ref[FFN_HALO:FFN_HALO + tm, :] = z_ref[0]
        us_ref[FFN_HALO + tm:2 * FFN_HALO + tm, :] = zn_ref[0]

        def conv(base, n):
            taps = [us_ref[base + pad + k:base + pad + k + n, :] for k in range(FFN_K)]
            uc = jnp.broadcast_to(b_ref[...], (n, FF_SHARD))
            for k in range(FFN_K):
                uc = uc + w_ref[k:k + 1, :] * taps[k]
            return uc, taps

        for r in range(tm // ROW_BLOCK):
            base = r * ROW_BLOCK
            rows = slice(base, base + ROW_BLOCK)
            uc, taps = conv(base, ROW_BLOCK)
            gelu, dgelu = gelu_grad(uc)
            da = da_ref[rows, :]
            dz_ref[1, rows, :] = (da * gelu).astype(BF16)
            duc = da * z_ref[1, rows, :] * dgelu
            ds_ref[rows, :] = duc
            for k in range(FFN_K):
                sums_ref[k:k + 1, :] += _colsum(duc * taps[k])
            sums_ref[FFN_K:FFN_K + 1, :] += _colsum(duc)
        _, dgelu_h = gelu_grad(conv(tm, FFN_HALO)[0])
        ds_ref[tm:tm + FFN_HALO, :] = jnp.where(i == nt - 1, 0.0, dah_ref[...] * zn_ref[1] * dgelu_h)
        for r in range(tm // ROW_BLOCK):
            base = r * ROW_BLOCK
            du = jnp.zeros((ROW_BLOCK, FF_SHARD), F32)
            for k in range(FFN_K):
                off = base + FFN_K - 1 - k
                du = du + w_ref[k:k + 1, :] * ds_ref[off:off + ROW_BLOCK, :]
            dz_ref[0, base:base + ROW_BLOCK, :] = du.astype(BF16)

    nxt = lambda j, i: jnp.minimum((i + 1) * nh, last_h)
    return pl.pallas_call(
        body, name="ffn_act_bwd", grid=(4, nt),
        in_specs=[pl.BlockSpec((None, tm, FF_SHARD), lambda j, i: (j, i, 0)),
                  pl.BlockSpec((None, FFN_HALO, FF_SHARD), lambda j, i: (j, nxt(j, i), 0)),
                  pl.BlockSpec((2, None, tm, FF_SHARD), lambda j, i: (0, j, i, 0)),
                  pl.BlockSpec((None, None, FFN_HALO, FF_SHARD), lambda j, i: (0, j, jnp.maximum(i * nh - 1, 0), 0)),
                  pl.BlockSpec((2, None, FFN_HALO, FF_SHARD), lambda j, i: (0, j, nxt(j, i), 0)),
                  pl.BlockSpec((None, FFN_K, FF_SHARD), lambda j, i: (j, 0, 0)),
                  pl.BlockSpec((None, 1, FF_SHARD), lambda j, i: (j, 0, 0))],
        out_specs=[pl.BlockSpec((2, None, tm, FF_SHARD), lambda j, i: (0, j, i, 0)),
                   pl.BlockSpec((None, 8, FF_SHARD), lambda j, i: (j, 0, 0))],
        out_shape=[jax.ShapeDtypeStruct((2, 4, t, FF_SHARD), BF16), jax.ShapeDtypeStruct((4, 8, FF_SHARD), F32)],
        scratch_shapes=[pltpu.VMEM((2 * FFN_HALO + tm, FF_SHARD), F32), pltpu.VMEM((tm + FFN_HALO, FF_SHARD), F32)],
        compiler_params=_params(("parallel", "arbitrary")),
    )(d_act, d_act, z, z, z, w_dw, b_dw)


def _local_step(x, target, w_in, rest, small, scatter=None):
    t = x.shape[0]
    tm = _pick(t, 2048)
    tk = _pick(t, 2048)
    nm = t // tm
    nk = t // tk
    d = D_MODEL

    xb = _cast_bf16(x)
    proj = _mm("proj", xb, w_in, (t, IN_COLS), F32, (nm, N_DEV, 1),
               pl.BlockSpec((tm, d), lambda i, j, k: (i, 0)),
               pl.BlockSpec((None, d, IN_SHARD), lambda i, j, k: (j, 0, 0)),
               pl.BlockSpec((tm, IN_SHARD), lambda i, j, k: (i, j)), NN, (tm, IN_SHARD),
               hosted=rest if isinstance(rest, _Hosted) else None)
    if isinstance(rest, _Hosted):
        proj, rest = proj
    w_conv_out8, w_hgrn_out8, w_out8, w_ffn_in, w_ffn_out8, conv_dw8, ffn_dw8 = rest
    w_conv_out = _relayout("w_conv_out_natural", w_conv_out8, (None, CONV_DIM, 128), lambda j: (j, 0, 0),
                           (CONV_DIM, 128), lambda j: (0, j), jax.ShapeDtypeStruct((CONV_DIM, d), BF16))
    w_hgrn_out = w_hgrn_out8.reshape(d, d)
    w_out = w_out8.reshape(d, d)
    w_ffn_out = w_ffn_out8.reshape(4, FF_SHARD, d)
    conv_dw = jnp.transpose(conv_dw8[:, :CONV_K, :CONV_DIM // N_DEV], (1, 0, 2)).reshape(CONV_K, CONV_DIM)
    ffn_dw = jnp.transpose(ffn_dw8[:, :FFN_K, :D_FF // N_DEV], (1, 0, 2)).reshape(FFN_K, 4, FF_SHARD)
    small = dict(small, w_conv_dw=conv_dw, w_ffn_dw=jnp.transpose(ffn_dw, (1, 0, 2)),
                 b_ffn_dw=small["b_ffn_dw"].reshape(4, 1, FF_SHARD))

    c_act, conv_pre = _conv_fwd(proj, small["w_conv_dw"], small["b_conv_dw"], small["conv_ln_g"], small["conv_ln_b"])
    y_conv = _mm("y_conv", c_act, w_conv_out, (t, d), F32, (nm, 1, 1),
                 pl.BlockSpec((tm, CONV_DIM), lambda i, j, k: (i, 0)),
                 pl.BlockSpec((CONV_DIM, d), lambda i, j, k: (0, 0)),
                 pl.BlockSpec((tm, d), lambda i, j, k: (i, 0)), NN, (tm, d))
    o, og, states = _hgrn_fwd(proj, small["hgrn_lb_logits"], small["hgrn_norm_g"])
    sq_w = pl.BlockSpec((d, d), lambda i, j, k: (0, 0))
    row_tile = pl.BlockSpec((tm, d), lambda i, j, k: (i, 0))
    y_hgrn = _mm("y_hgrn", og, w_hgrn_out, (t, d), F32, (nm, 1, 1), row_tile, sq_w, row_tile, NN, (tm, d))
    mixed, r1, x1b = _mix_ln1(proj, y_conv, y_hgrn, w_out, x, small["ln1_g"], small["ln1_b"])
    z = _mm("ffn_in", x1b, w_ffn_in, (N_DEV, t, FF_SHARD), F32, (nm, N_DEV, 1), row_tile,
            pl.BlockSpec((None, d, FF_SHARD), lambda i, j, k: (j, 0, 0)),
            pl.BlockSpec((None, tm, FF_SHARD), lambda i, j, k: (j, i, 0)), NN, (tm, FF_SHARD))
    z = z.reshape(2, 4, t, FF_SHARD)
    act = _ffn_act_fwd(z, small["w_ffn_dw"], small["b_ffn_dw"])

    d_r2, d_r2b, sums_ln2 = _ffn_out_ln2(act, w_ffn_out, r1, target, small["ln1_g"], small["ln1_b"],
                                         small["ln2_g"], small["ln2_b"])
    d_act = _mm("d_act", d_r2b, w_ffn_out, (4, t, FF_SHARD), F32, (nm, 4, 1), row_tile,
                pl.BlockSpec((None, FF_SHARD, d), lambda i, j, k: (j, 0, 0)),
                pl.BlockSpec((None, tm, FF_SHARD), lambda i, j, k: (j, i, 0)), NT, (tm, FF_SHARD))
    g_w_ffn_out = _mm("g_w_ffn_out", act, d_r2b, (4, FF_SHARD, d), BF16, (4, 1, nk),
                      pl.BlockSpec((None, tk, FF_SHARD), lambda i, j, k: (i, k, 0)),
                      pl.BlockSpec((tk, d), lambda i, j, k: (k, 0)),
                      pl.BlockSpec((None, FF_SHARD, d), lambda i, j, k: (i, 0, 0)), TN, (FF_SHARD, d))
    d_z, sums_ffn = _ffn_act_bwd(d_act, z, small["w_ffn_dw"], small["b_ffn_dw"])
    d_z8 = d_z.reshape(N_DEV, t, FF_SHARD)
    d_r1, d_r1b, sums_ln1 = _d_x1_ln1_bwd(d_z8, w_ffn_in, d_r2, r1, small["ln1_g"])
    g_w_ffn_in = _mm("g_w_ffn_in", x1b, d_z8, (N_DEV, d, FF_SHARD), BF16, (N_DEV, 1, nk),
                     pl.BlockSpec((tk, d), lambda i, j, k: (k, 0)),
                     pl.BlockSpec((None, tk, FF_SHARD), lambda i, j, k: (i, k, 0)),
                     pl.BlockSpec((None, d, FF_SHARD), lambda i, j, k: (i, 0, 0)), TN, (d, FF_SHARD))
    d_mixed = _mm("d_mixed", d_r1b, w_out, (t, d), F32, (nm, 1, 1), row_tile, sq_w, row_tile, NT, (tm, d))
    k_tile = pl.BlockSpec((tk, d), lambda i, j, k: (k, 0))
    g_w_out = _mm("g_w_out", mixed, d_r1b, (d, d), BF16, (1, 1, nk), k_tile, k_tile, sq_w, TN, (d, d))
    d_y, d_mz = _merge_bwd(d_mixed, proj, y_conv, y_hgrn)
    d_pre, sums_conv = _d_c_norm_bwd(d_y, w_conv_out, conv_pre, small["conv_ln_g"], small["conv_ln_b"])
    g_w_conv_out = _mm("g_w_conv_out", c_act, d_y, (CONV_DIM, d), BF16, (1, 1, nk),
                       pl.BlockSpec((tk, CONV_DIM), lambda i, j, k: (k, 0)),
                       pl.BlockSpec((None, tk, d), lambda i, j, k: (0, k, 0)),
                       pl.BlockSpec((CONV_DIM, d), lambda i, j, k: (0, 0)), TN, (CONV_DIM, d))
    g_w_conv_out = _relayout("g_w_conv_out_shards", g_w_conv_out, (CONV_DIM, 128), lambda j: (0, j),
                             (None, CONV_DIM, 128), lambda j: (j, 0, 0),
                             jax.ShapeDtypeStruct((N_DEV, CONV_DIM, 128), BF16))
    d_og = _mm("d_og", d_y, w_hgrn_out, (t, d), F32, (nm, 1, 1),
               pl.BlockSpec((None, tm, d), lambda i, j, k: (1, i, 0)), sq_w, row_tile, NT, (tm, d))
    g_w_hgrn_out = _mm("g_w_hgrn_out", og, d_y, (d, d), BF16, (1, 1, nk), k_tile,
                       pl.BlockSpec((None, tk, d), lambda i, j, k: (1, k, 0)), sq_w, TN, (d, d))
    d_cproj, g_w_conv_dw = _conv_bwd_dw(d_pre, proj, small["w_conv_dw"])

    early = [g_w_conv_out, g_w_hgrn_out.reshape(N_DEV, d // N_DEV, d), g_w_out.reshape(N_DEV, d // N_DEV, d),
             g_w_ffn_in, g_w_ffn_out.reshape(N_DEV, D_FF // N_DEV, d)]
    hgrn_out = _hgrn_bwd(d_og, o, proj, states, small["hgrn_lb_logits"], small["hgrn_norm_g"],
                         hosted=None if scatter is None else scatter(early))
    d_qz, d_fz, d_iv, d_gz, sums_hgrn = hgrn_out[:5]
    early_recv = list(hgrn_out[5:])
    d_proj = jnp.concatenate([d_cproj, d_qz, d_fz, d_iv, d_gz, d_mz], axis=1)
    g_w_in = _mm("g_w_in", xb, d_proj, (N_DEV, d, IN_SHARD), BF16, (N_DEV, 1, nk), k_tile,
                 pl.BlockSpec((tk, IN_SHARD), lambda i, j, k: (k, i)),
                 pl.BlockSpec((None, d, IN_SHARD), lambda i, j, k: (i, 0, 0)), TN, (d, IN_SHARD))
    def add_residual(acc, ins, outs, scr):
        outs[0][...] = ALPHA * ins[2][...] + acc

    ta = _pick(t, 1024)
    acc_tile = pl.BlockSpec((ta, d), lambda i, j, k: (i, 0))
    grad_x, *late_recv = _mm_fused(
        "grad_x", (t // ta, 1, N_DEV), (d_proj, w_in, d_r1),
        [pl.BlockSpec((ta, IN_SHARD), lambda i, j, k: (i, k)),
         pl.BlockSpec((None, d, IN_SHARD), lambda i, j, k: (k, 0, 0)), acc_tile],
        [jax.ShapeDtypeStruct((t, d), F32)], [acc_tile], NT, (ta, d), add_residual,
        hosted=None if scatter is None else scatter([g_w_in]))

    d_l0 = sums_hgrn[1:2]
    small_grads = {
        "loss": sums_ln2[2:3, 0:128],
        "b_conv_dw": sums_conv[2:3], "conv_ln_g": sums_conv[0:1], "conv_ln_b": sums_conv[1:2],
        "hgrn_lb_logits": jnp.concatenate([d_l0, -d_l0], axis=1),
        "hgrn_norm_g": sums_hgrn[0:1],
        "ln1_g": sums_ln1[0:1], "ln1_b": sums_ln1[1:2],
        "b_ffn_dw": sums_ffn[:, FFN_K, :].reshape(1, D_FF),
        "ln2_g": sums_ln2[0:1], "ln2_b": sums_ln2[1:2],
        "w_conv_dw": g_w_conv_dw[0:CONV_K].reshape(1, CONV_K * CONV_DIM),
        "w_ffn_dw": jnp.transpose(sums_ffn[:, 0:FFN_K, :], (1, 0, 2)).reshape(1, FFN_K * D_FF),
    }
    large_grads = [g_w_in] + early
    if scatter is not None:
        large_grads = list(zip(large_grads, late_recv + early_recv))
    return grad_x, large_grads, small_grads


def _coords():
    return lax.axis_index("x"), lax.axis_index("y"), lax.axis_index("c")


def _gather(shards):
    n = len(shards)

    def parts(ins, outs, sems):
        send_sems, recv_sems, local_sems = sems
        x, y, c = _coords()
        me = 4 * x + 2 * y + c
        sibling = (x, y, 1 - c)
        chips = [(1 - x, y), (x, 1 - y), (1 - x, 1 - y)]

        def copy(a, k, block, to, src=None):
            return pltpu.make_async_remote_copy(
                src_ref=outs[a].at[block] if src is None else src, dst_ref=outs[a].at[block],
                send_sem=send_sems.at[a, k], recv_sem=recv_sems.at[a, k], device_id=to, device_id_type=MESH)

        local = [pltpu.make_async_copy(ins[a], outs[a].at[me], local_sems.at[a]) for a in range(n)]
        first = []
        for a in range(n):
            first.append(copy(a, 0, me, sibling, src=ins[a]))
            for j, chip in enumerate(chips):
                first.append(copy(a, 1 + j, me, (*chip, c), src=ins[a]))
        return x, y, c, sibling, chips, copy, local, first

    def start(ins, outs, sems):
        *_, local, first = parts(ins, outs, sems)
        for cp in local + first:
            cp.start()

    def finish(ins, outs, sems):
        x, y, c, sibling, chips, copy, local, first = parts(ins, outs, sems)
        passed = []
        for j, (px, py) in enumerate(chips):
            for a in range(n):
                copy(a, 1 + j, 4 * px + 2 * py + c, sibling).wait_recv()
                cp = copy(a, 4 + j, 4 * px + 2 * py + c, sibling)
                cp.start()
                passed.append(cp)
        for a in range(n):
            copy(a, 0, 4 * x + 2 * y + 1 - c, sibling).wait_recv()
            for j, (px, py) in enumerate(chips):
                copy(a, 4 + j, 4 * px + 2 * py + 1 - c, sibling).wait_recv()
        for cp in first + passed:
            cp.wait_send()
        for cp in local:
            cp.wait()

    return _Hosted(shards, [jax.ShapeDtypeStruct((N_DEV,) + s.shape, s.dtype) for s in shards],
                   [pltpu.SemaphoreType.DMA((n, 7)), pltpu.SemaphoreType.DMA((n, 7)), pltpu.SemaphoreType.DMA((n,))],
                   start, finish)


def _scatter(grads):
    n = len(grads)

    def copies(ins, outs, sems):
        send_sems, recv_sems = sems
        x, y, c = _coords()
        out = []
        for a in range(n):
            for k in range(1, N_DEV):
                px, py, pc = x ^ (k >> 2), y ^ ((k >> 1) & 1), c ^ (k & 1)
                out.append(pltpu.make_async_remote_copy(
                    src_ref=ins[a].at[4 * px + 2 * py + pc], dst_ref=outs[a].at[k - 1],
                    send_sem=send_sems.at[a, k - 1], recv_sem=recv_sems.at[a, k - 1],
                    device_id=(px, py, pc), device_id_type=MESH))
        return out

    def start(ins, outs, sems):
        for cp in copies(ins, outs, sems):
            cp.start()

    def finish(ins, outs, sems):
        for cp in copies(ins, outs, sems):
            cp.wait()

    return _Hosted(grads, [jax.ShapeDtypeStruct((N_DEV - 1,) + g.shape[1:], g.dtype) for g in grads],
                   [pltpu.SemaphoreType.DMA((n, N_DEV - 1)), pltpu.SemaphoreType.DMA((n, N_DEV - 1))], start, finish)


def _row_tile(rows):
    return 256 if rows % 256 == 0 else rows


def _adam_math(w, g, m, v):
    m_new = ADAM_B1 * m + (1.0 - ADAM_B1) * g
    v_new = ADAM_B2 * v + (1.0 - ADAM_B2) * (g * g)
    m_hat = m_new / (1.0 - ADAM_B1 ** ADAM_STEP)
    v_hat = v_new / (1.0 - ADAM_B2 ** ADAM_STEP)
    delta = -ADAM_LR * (m_hat / (jnp.sqrt(v_hat) + ADAM_EPS) + ADAM_WD * w)
    return delta, m_new, v_new


def _adam_large(name, own, recv, me, w, m, v):
    rows, cols = w.shape
    tr = _row_tile(rows)

    def body(me_ref, p_ref, r_ref, w_ref, m_ref, v_ref, g_out, d_out, m_out, v_out):
        g = p_ref[...].astype(F32)
        for k in range(N_DEV - 1):
            g = g + r_ref[k].astype(F32)
        delta, m_new, v_new = _adam_math(w_ref[...], g, m_ref[...], v_ref[...])
        g_out[...] = g
        d_out[...] = delta
        m_out[...] = m_new
        v_out[...] = v_new

    tile = pl.BlockSpec((tr, cols), lambda r, me_ref: (r, 0))
    sds = jax.ShapeDtypeStruct((rows, cols), F32)
    return pl.pallas_call(
        body, name=name,
        grid_spec=pltpu.PrefetchScalarGridSpec(
            num_scalar_prefetch=1, grid=(rows // tr,),
            in_specs=[pl.BlockSpec((None, tr, cols), lambda r, me_ref: (me_ref[0], r, 0)),
                      pl.BlockSpec((N_DEV - 1, tr, cols), lambda r, me_ref: (0, r, 0)), tile, tile, tile],
            out_specs=[tile, tile, tile, tile]),
        out_shape=[sds, sds, sds, sds],
        compiler_params=_params(("parallel",)),
    )(me, own, recv, w, m, v)


def _small_allreduce(vec):
    rows = vec.shape[0]

    def body(v_ref, o_ref, gat_ref, send_sems, recv_sems):
        x, y, c = _coords()
        me = 4 * x + 2 * y + c
        gat_ref[me] = v_ref[...]
        copies = []
        for k in range(1, N_DEV):
            px, py, pc = x ^ (k >> 2), y ^ ((k >> 1) & 1), c ^ (k & 1)
            copies.append(pltpu.make_async_remote_copy(
                src_ref=v_ref, dst_ref=gat_ref.at[me], send_sem=send_sems.at[k - 1], recv_sem=recv_sems.at[k - 1],
                device_id=(px, py, pc), device_id_type=MESH))
        for cp in copies:
            cp.start()
        for k in range(1, N_DEV):
            px, py, pc = x ^ (k >> 2), y ^ ((k >> 1) & 1), c ^ (k & 1)
            pltpu.make_async_remote_copy(
                src_ref=v_ref, dst_ref=gat_ref.at[4 * px + 2 * py + pc], send_sem=send_sems.at[k - 1],
                recv_sem=recv_sems.at[k - 1], device_id=(px, py, pc), device_id_type=MESH).wait_recv()
        for cp in copies:
            cp.wait_send()
        acc = gat_ref[0]
        for dev in range(1, N_DEV):
            acc = acc + gat_ref[dev]
        o_ref[...] = acc

    whole = pl.BlockSpec(memory_space=pltpu.VMEM)
    return pl.pallas_call(
        body, name="small_allreduce", in_specs=[whole], out_specs=whole,
        out_shape=jax.ShapeDtypeStruct((rows, 128), F32),
        scratch_shapes=[pltpu.VMEM((N_DEV, rows, 128), F32), pltpu.SemaphoreType.DMA((N_DEV - 1,)),
                        pltpu.SemaphoreType.DMA((N_DEV - 1,))],
        compiler_params=pltpu.CompilerParams(has_side_effects=True, vmem_limit_bytes=VMEM_LIMIT),
    )(vec)


def _adam_small(w, g, m, v):
    def body(w_ref, g_ref, m_ref, v_ref, d_out, m_out, v_out):
        delta, m_new, v_new = _adam_math(w_ref[...], g_ref[...], m_ref[...], v_ref[...])
        d_out[...] = delta
        m_out[...] = m_new
        v_out[...] = v_new

    whole = pl.BlockSpec(memory_space=pltpu.VMEM)
    sds = jax.ShapeDtypeStruct(w.shape, F32)
    return pl.pallas_call(body, name="adam_small", in_specs=[whole] * 4, out_specs=[whole] * 3,
                          out_shape=[sds, sds, sds])(w, g, m, v)


_SMALL_ORDER = ["loss", "b_conv_dw", "conv_ln_g", "conv_ln_b", "hgrn_lb_logits", "hgrn_norm_g", "ln1_g", "ln1_b",
                "b_ffn_dw", "ln2_g", "ln2_b", "w_conv_dw", "w_ffn_dw"]
_WEIGHTS = ["w_in", "w_conv_dw", "b_conv_dw", "conv_ln_g", "conv_ln_b", "w_conv_out", "hgrn_lb_logits", "hgrn_norm_g",
            "w_hgrn_out", "w_out", "ln1_g", "ln1_b", "w_ffn_in", "w_ffn_dw", "b_ffn_dw", "w_ffn_out", "ln2_g", "ln2_b"]
_LARGE = ["w_in", "w_conv_out", "w_hgrn_out", "w_out", "w_ffn_in", "w_ffn_out"]
_CONV_DW_SHARD = CONV_DIM // N_DEV
_FFN_DW_SHARD = D_FF // N_DEV


def kernel(x, w_in, w_conv_dw, b_conv_dw, conv_ln_g, conv_ln_b, w_conv_out, hgrn_lb_logits, hgrn_norm_g, w_hgrn_out, w_out, ln1_g, ln1_b, w_ffn_in, w_ffn_dw, b_ffn_dw, w_ffn_out, ln2_g, ln2_b, loss_target, m_w_in, m_w_conv_dw, m_b_conv_dw, m_conv_ln_g, m_conv_ln_b, m_w_conv_out, m_hgrn_lb_logits, m_hgrn_norm_g, m_w_hgrn_out, m_w_out, m_ln1_g, m_ln1_b, m_w_ffn_in, m_w_ffn_dw, m_b_ffn_dw, m_w_ffn_out, m_ln2_g, m_ln2_b, v_w_in, v_w_conv_dw, v_b_conv_dw, v_conv_ln_g, v_conv_ln_b, v_w_conv_out, v_hgrn_lb_logits, v_hgrn_norm_g, v_w_hgrn_out, v_w_out, v_ln1_g, v_ln1_b, v_w_ffn_in, v_w_ffn_dw, v_b_ffn_dw, v_w_ffn_out, v_ln2_g, v_ln2_b):
    w = dict(w_in=w_in, w_conv_dw=w_conv_dw, b_conv_dw=b_conv_dw, conv_ln_g=conv_ln_g, conv_ln_b=conv_ln_b,
             w_conv_out=w_conv_out, hgrn_lb_logits=hgrn_lb_logits, hgrn_norm_g=hgrn_norm_g, w_hgrn_out=w_hgrn_out,
             w_out=w_out, ln1_g=ln1_g, ln1_b=ln1_b, w_ffn_in=w_ffn_in, w_ffn_dw=w_ffn_dw, b_ffn_dw=b_ffn_dw,
             w_ffn_out=w_ffn_out, ln2_g=ln2_g, ln2_b=ln2_b)
    m = dict(w_in=m_w_in, w_conv_dw=m_w_conv_dw, b_conv_dw=m_b_conv_dw, conv_ln_g=m_conv_ln_g, conv_ln_b=m_conv_ln_b,
             w_conv_out=m_w_conv_out, hgrn_lb_logits=m_hgrn_lb_logits, hgrn_norm_g=m_hgrn_norm_g,
             w_hgrn_out=m_w_hgrn_out, w_out=m_w_out, ln1_g=m_ln1_g, ln1_b=m_ln1_b, w_ffn_in=m_w_ffn_in,
             w_ffn_dw=m_w_ffn_dw, b_ffn_dw=m_b_ffn_dw, w_ffn_out=m_w_ffn_out, ln2_g=m_ln2_g, ln2_b=m_ln2_b)
    v = dict(w_in=v_w_in, w_conv_dw=v_w_conv_dw, b_conv_dw=v_b_conv_dw, conv_ln_g=v_conv_ln_g, conv_ln_b=v_conv_ln_b,
             w_conv_out=v_w_conv_out, hgrn_lb_logits=v_hgrn_lb_logits, hgrn_norm_g=v_hgrn_norm_g,
             w_hgrn_out=v_w_hgrn_out, w_out=v_w_out, ln1_g=v_ln1_g, ln1_b=v_ln1_b, w_ffn_in=v_w_ffn_in,
             w_ffn_dw=v_w_ffn_dw, b_ffn_dw=v_b_ffn_dw, w_ffn_out=v_w_ffn_out, ln2_g=v_ln2_g, ln2_b=v_ln2_b)
    xi, yi, ci = lax.axis_index("x"), lax.axis_index("y"), lax.axis_index("c")
    me = 4 * xi + 2 * yi + ci
    me_op = jnp.reshape(me, (1,)).astype(jnp.int32)

    shards = [w[name][0].astype(BF16) for name in _LARGE]
    shards.append(jnp.pad(w_conv_dw[0], ((0, 1), (0, 128 - _CONV_DW_SHARD))))
    shards.append(jnp.pad(w_ffn_dw[0], ((0, 8 - FFN_K), (0, 384 - _FFN_DW_SHARD))))
    (w_in_all,) = _run_hosted("all_gather_w_in", _gather(shards[:1]))
    small = dict(b_conv_dw=b_conv_dw, conv_ln_g=conv_ln_g, conv_ln_b=conv_ln_b, hgrn_lb_logits=hgrn_lb_logits,
                 hgrn_norm_g=hgrn_norm_g, ln1_g=ln1_g, ln1_b=ln1_b, ln2_g=ln2_g, ln2_b=ln2_b, b_ffn_dw=b_ffn_dw)

    grad_x, large_grads, small_grads = _local_step(x[0], loss_target[0], w_in_all, _gather(shards[1:]), small, _scatter)

    out = {}
    for name, (own, recv) in zip(_LARGE, large_grads):
        out[name] = _adam_large("adam_" + name, own, recv, me_op, w[name][0], m[name][0], v[name][0])

    vec = jnp.concatenate([small_grads[name] for name in _SMALL_ORDER], axis=1)
    total = _small_allreduce(vec.reshape(-1, 128)).reshape(1, -1)
    sizes = [small_grads[name].shape[1] for name in _SMALL_ORDER]
    offs = [0]
    for s in sizes:
        offs.append(offs[-1] + s)
    summed = {name: total[:, offs[i]:offs[i + 1]] for i, name in enumerate(_SMALL_ORDER)}
    loss = summed["loss"][0, 0]
    conv_dw_g = lax.dynamic_slice_in_dim(summed["w_conv_dw"].reshape(CONV_K, CONV_DIM), me * _CONV_DW_SHARD, _CONV_DW_SHARD, axis=1)
    ffn_dw_g = lax.dynamic_slice_in_dim(summed["w_ffn_dw"].reshape(FFN_K, D_FF), me * _FFN_DW_SHARD, _FFN_DW_SHARD, axis=1)
    small_g = dict(summed, w_conv_dw=conv_dw_g.reshape(1, -1), w_ffn_dw=ffn_dw_g.reshape(1, -1))
    names = [n for n in _SMALL_ORDER if n != "loss"]
    flat = lambda d, n: d[n].reshape(1, -1)
    n_small = sum(small_g[n].shape[1] for n in names)
    pad = (-n_small) % 1024
    pack = lambda pieces: jnp.pad(jnp.concatenate(pieces, axis=1), ((0, 0), (0, pad))).reshape(-1, 128)
    d_s, m_s, v_s = _adam_small(pack([flat(w, n) for n in names]), pack([small_g[n] for n in names]),
                                pack([flat(m, n) for n in names]), pack([flat(v, n) for n in names]))
    pos = 0
    for n in names:
        size = small_g[n].shape[1]
        cut = lambda a: a.reshape(1, -1)[:, pos:pos + size].reshape(w[n].shape)
        out[n] = (small_g[n].reshape(w[n].shape), cut(d_s), cut(m_s), cut(v_s))
        pos += size

    for name in _LARGE:
        out[name] = tuple(a.reshape(w[name].shape) for a in out[name])
    grads = [out[n][0] for n in _WEIGHTS]
    deltas = [out[n][1] for n in _WEIGHTS]
    new_m = [out[n][2] for n in _WEIGHTS]
    new_v = [out[n][3] for n in _WEIGHTS]
    return (loss, grad_x[None], *grads, *deltas, *new_m, *new_v)
```

```python
import functools
import math

import jax
import jax.numpy as jnp
from jax import lax
from jax.experimental import pallas as pl
from jax.experimental.pallas import tpu as pltpu

F32 = jnp.float32
BF16 = jnp.bfloat16

N_DEV = 8
D_MODEL = 1024
CONV_DIM = 512
CONV_K = 31
HGRN_DIM = 1024
HEADS = 8
HEAD_DIM = 128
D_FF = 2816
FFN_K = 3
FF_SHARD = 2 * D_FF // N_DEV
IN_COLS = 7168
IN_SHARD = IN_COLS // N_DEV
LN_EPS = 1e-5
RMS_EPS = 1e-6
ALPHA = 2.0 ** 0.25

ADAM_LR = 0.001
ADAM_B1 = 0.9
ADAM_B2 = 0.999
ADAM_EPS = 1e-08
ADAM_WD = 0.01
ADAM_STEP = 10

CHUNK = 64
CHUNKS_PER_BLOCK = 8
CONV_HALO = 32
FFN_HALO = 8
ROW_BLOCK = 64
VMEM_LIMIT = 48 * 1024 * 1024
MXU_DEPTH = 256

MESH = pl.DeviceIdType.MESH
ANY = pl.BlockSpec(memory_space=pl.ANY)

NN = (((1,), (0,)), ((), ()))
NT = (((1,), (1,)), ((), ()))
TN = (((0,), (0,)), ((), ()))


def _params(sem):
    return pltpu.CompilerParams(dimension_semantics=sem, vmem_limit_bytes=VMEM_LIMIT)


def _dot(a, b, dims):
    return lax.dot_general(a.astype(BF16), b.astype(BF16), dims, preferred_element_type=F32)


def _sigmoid(x):
    return jax.nn.sigmoid(x)


def _ln(r):
    mu = jnp.mean(r, axis=-1, keepdims=True)
    xc = r - mu
    var = jnp.mean(xc * xc, axis=-1, keepdims=True)
    rstd = lax.rsqrt(var + LN_EPS)
    return xc * rstd, rstd


def _ln_bwd(dy, xhat, rstd, g):
    dxh = dy * g
    m1 = jnp.mean(dxh, axis=-1, keepdims=True)
    m2 = jnp.mean(dxh * xhat, axis=-1, keepdims=True)
    return rstd * (dxh - m1 - xhat * m2)


def _colsum(x):
    return jnp.sum(x, axis=0, keepdims=True)


class _Hosted:
    def __init__(self, inputs, out_shapes, sem_shapes, start, finish):
        self.inputs, self.out_shapes, self.sem_shapes = list(inputs), list(out_shapes), list(sem_shapes)
        self.start, self.finish = start, finish


def _call(body, *, name, grid, in_specs, out_specs, out_shape, scratch_shapes, semantics, operands, hosted=None):
    if hosted is None:
        return pl.pallas_call(
            body, name=name, grid=grid, in_specs=list(in_specs), out_specs=list(out_specs), out_shape=list(out_shape),
            scratch_shapes=list(scratch_shapes), compiler_params=_params(semantics))(*operands)
    n_in, n_out, n_scr = len(in_specs), len(out_specs), len(scratch_shapes)
    h_in, h_out = len(hosted.inputs), len(hosted.out_shapes)

    def full_body(*refs):
        ins, refs = refs[:n_in], refs[n_in:]
        h_ins, refs = refs[:h_in], refs[h_in:]
        outs, refs = refs[:n_out], refs[n_out:]
        h_outs, refs = refs[:h_out], refs[h_out:]
        scr, sems = refs[:n_scr], refs[n_scr:]
        first = functools.reduce(jnp.logical_and, [pl.program_id(d) == 0 for d in range(len(grid))])
        last = functools.reduce(jnp.logical_and, [pl.program_id(d) == grid[d] - 1 for d in range(len(grid))])

        @pl.when(first)
        def _():
            hosted.start(h_ins, h_outs, sems)

        body(*ins, *outs, *scr)

        @pl.when(last)
        def _():
            hosted.finish(h_ins, h_outs, sems)

    return pl.pallas_call(
        full_body, name=name, grid=grid, in_specs=list(in_specs) + [ANY] * h_in,
        out_specs=list(out_specs) + [ANY] * h_out, out_shape=list(out_shape) + hosted.out_shapes,
        scratch_shapes=list(scratch_shapes) + hosted.sem_shapes,
        compiler_params=pltpu.CompilerParams(dimension_semantics=("arbitrary",) * len(grid),
                                             vmem_limit_bytes=VMEM_LIMIT, has_side_effects=True),
    )(*operands, *hosted.inputs)


def _run_hosted(name, hosted):
    h_in, h_out = len(hosted.inputs), len(hosted.out_shapes)

    def body(*refs):
        ins, outs, sems = refs[:h_in], refs[h_in:h_in + h_out], refs[h_in + h_out:]
        hosted.start(ins, outs, sems)
        hosted.finish(ins, outs, sems)

    return pl.pallas_call(
        body, name=name, in_specs=[ANY] * h_in, out_specs=[ANY] * h_out, out_shape=hosted.out_shapes,
        scratch_shapes=hosted.sem_shapes, compiler_params=pltpu.CompilerParams(has_side_effects=True),
    )(*hosted.inputs)


def _mm(name, a, b, out_shape, out_dtype, grid, a_spec, b_spec, o_spec, dims, acc_shape, hosted=None):
    nk = grid[2]
    if nk == 1:
        def body(a_ref, b_ref, o_ref):
            o_ref[...] = _dot(a_ref[...], b_ref[...], dims).astype(o_ref.dtype)
        scratch = []
    else:
        def body(a_ref, b_ref, o_ref, acc_ref):
            k = pl.program_id(2)

            @pl.when(k == 0)
            def _():
                acc_ref[...] = jnp.zeros_like(acc_ref)

            acc_ref[...] += _dot(a_ref[...], b_ref[...], dims)

            @pl.when(k == nk - 1)
            def _():
                o_ref[...] = acc_ref[...].astype(o_ref.dtype)
        scratch = [pltpu.VMEM(acc_shape, F32)]

    outs = _call(body, name=name, grid=grid, in_specs=[a_spec, b_spec], out_specs=[o_spec],
                 out_shape=[jax.ShapeDtypeStruct(out_shape, out_dtype)], scratch_shapes=scratch,
                 semantics=("parallel", "parallel", "arbitrary"), operands=(a, b), hosted=hosted)
    return outs[0] if hosted is None else (outs[0], list(outs[1:]))


def _mm_fused(name, grid, operands, in_specs, out_shape, out_specs, dims, acc_shape, epilogue, lhs=None, scratch=(),
              hosted=None):
    nk = grid[2]
    n_in, n_out = len(in_specs), len(out_specs)

    def body(*refs):
        ins, outs, scr = refs[:n_in], refs[n_in:n_in + n_out], refs[n_in + n_out:]
        acc_ref, k = scr[0], pl.program_id(2)
        a = ins[0][...] if lhs is None else lhs(ins, outs)
        part = _dot(a, ins[1][...], dims)
        if nk == 1:
            acc_ref[...] = part
            epilogue(acc_ref, ins, outs, scr[1:])
            return

        @pl.when(k == 0)
        def _():
            acc_ref[...] = jnp.zeros_like(acc_ref)

        acc_ref[...] += part

        @pl.when(k == nk - 1)
        def _():
            epilogue(acc_ref, ins, outs, scr[1:])

    return _call(body, name=name, grid=grid, in_specs=in_specs, out_specs=out_specs, out_shape=out_shape,
                 scratch_shapes=[pltpu.VMEM(acc_shape, F32)] + list(scratch), semantics=("arbitrary",) * 3,
                 operands=operands, hosted=hosted)


def _row_blocks(rows, block=256):
    block = block if rows % block == 0 else rows
    return [slice(r, r + block) for r in range(0, rows, block)]


def _pick(t, pref):
    return pref if t % pref == 0 else t


def _glu(p):
    return p[:, :CONV_DIM] * _sigmoid(p[:, CONV_DIM:])


def _conv_fwd(proj, w_dw, b_dw, g, b):
    t = proj.shape[0]
    tm = _pick(t, 256)
    nh = tm // CONV_HALO

    def body(p_ref, ph_ref, w_ref, bd_ref, g_ref, b_ref, act_ref, pre_ref, xs_ref):
        i = pl.program_id(0)
        halo = _glu(ph_ref[...])
        xs_ref[0:CONV_HALO, :] = jnp.where(i == 0, 0.0, halo)
        xs_ref[CONV_HALO:CONV_HALO + tm, :] = _glu(p_ref[...])
        for r in range(tm // ROW_BLOCK):
            acc = jnp.broadcast_to(bd_ref[...], (ROW_BLOCK, CONV_DIM))
            for k in range(CONV_K):
                off = r * ROW_BLOCK + CONV_HALO - (CONV_K - 1) + k
                acc = acc + w_ref[k:k + 1, :] * xs_ref[off:off + ROW_BLOCK, :]
            rows = slice(r * ROW_BLOCK, (r + 1) * ROW_BLOCK)
            pre_ref[rows, :] = acc
            xhat, _ = _ln(acc)
            yln = xhat * g_ref[...] + b_ref[...]
            act_ref[rows, :] = (yln * _sigmoid(yln)).astype(BF16)

    full = lambda s: pl.BlockSpec(s, lambda i: (0, 0))
    return pl.pallas_call(
        body, name="conv_fwd", grid=(t // tm,),
        in_specs=[pl.BlockSpec((tm, 2 * CONV_DIM), lambda i: (i, 0)),
                  pl.BlockSpec((CONV_HALO, 2 * CONV_DIM), lambda i: (jnp.maximum(i * nh - 1, 0), 0)),
                  full((CONV_K, CONV_DIM)), full((1, CONV_DIM)), full((1, CONV_DIM)), full((1, CONV_DIM))],
        out_specs=[pl.BlockSpec((tm, CONV_DIM), lambda i: (i, 0)), pl.BlockSpec((tm, CONV_DIM), lambda i: (i, 0))],
        out_shape=[jax.ShapeDtypeStruct((t, CONV_DIM), BF16), jax.ShapeDtypeStruct((t, CONV_DIM), F32)],
        scratch_shapes=[pltpu.VMEM((CONV_HALO + tm, CONV_DIM), F32)],
        compiler_params=_params(("arbitrary",)),
    )(proj, proj, w_dw, b_dw, g, b)


def _d_c_norm_bwd(d_y, w_conv_out, pre, g, b):
    t = pre.shape[0]
    tm = _pick(t, 512)
    d = D_MODEL

    def epilogue(d_c, ins, outs, scr):
        pre_ref, g_ref, b_ref = ins[2:]
        dpre_ref, sums_ref = outs
        i = pl.program_id(0)

        @pl.when(i == 0)
        def _():
            sums_ref[...] = jnp.zeros_like(sums_ref)

        for rows in _row_blocks(tm):
            xhat, rstd = _ln(pre_ref[rows, :])
            yln = xhat * g_ref[...] + b_ref[...]
            sg = _sigmoid(yln)
            dyln = d_c[rows, :] * (sg * (1.0 + yln * (1.0 - sg)))
            dpre = _ln_bwd(dyln, xhat, rstd, g_ref[...])
            dpre_ref[rows, :] = dpre
            sums_ref[0:1, :] += _colsum(dyln * xhat)
            sums_ref[1:2, :] += _colsum(dyln)
            sums_ref[2:3, :] += _colsum(dpre)

    full = lambda s: pl.BlockSpec(s, lambda i, j, k: (0, 0))
    tile = pl.BlockSpec((tm, CONV_DIM), lambda i, j, k: (i, 0))
    return _mm_fused(
        "d_c_norm_bwd", (t // tm, 1, 1), (d_y, w_conv_out, pre, g, b),
        [pl.BlockSpec((None, tm, d), lambda i, j, k: (0, i, 0)), full((CONV_DIM, d)), tile,
         full((1, CONV_DIM)), full((1, CONV_DIM))],
        [jax.ShapeDtypeStruct((t, CONV_DIM), F32), jax.ShapeDtypeStruct((8, CONV_DIM), F32)],
        [tile, full((8, CONV_DIM))], NT, (tm, CONV_DIM), epilogue)


def _conv_bwd_dw(d_pre, proj, w_dw):
    t = d_pre.shape[0]
    tm = _pick(t, 256)
    nt = t // tm
    nh = tm // CONV_HALO
    last_h = t // CONV_HALO - 1

    def body(dp_ref, dph_ref, p_ref, ph_ref, w_ref, dproj_ref, dw_ref, xs_ref, ds_ref):
        i = pl.program_id(0)

        @pl.when(i == 0)
        def _():
            dw_ref[...] = jnp.zeros_like(dw_ref)

        halo = _glu(ph_ref[...])
        xs_ref[0:CONV_HALO, :] = jnp.where(i == 0, 0.0, halo)
        xs_ref[CONV_HALO:CONV_HALO + tm, :] = _glu(p_ref[...])
        ds_ref[0:tm, :] = dp_ref[...]
        ds_ref[tm:tm + CONV_HALO, :] = jnp.where(i == nt - 1, 0.0, dph_ref[...])
        for r in range(tm // ROW_BLOCK):
            base = r * ROW_BLOCK
            dpre = ds_ref[base:base + ROW_BLOCK, :]
            acc = jnp.zeros((ROW_BLOCK, CONV_DIM), F32)
            for k in range(CONV_K):
                acc = acc + w_ref[k:k + 1, :] * ds_ref[base + CONV_K - 1 - k:base + CONV_K - 1 - k + ROW_BLOCK, :]
                off = base + CONV_HALO - (CONV_K - 1) + k
                dw_ref[k:k + 1, :] += _colsum(dpre * xs_ref[off:off + ROW_BLOCK, :])
            rows = slice(base, base + ROW_BLOCK)
            cval = p_ref[rows, 0:CONV_DIM]
            sg = _sigmoid(p_ref[rows, CONV_DIM:2 * CONV_DIM])
            dproj_ref[rows, 0:CONV_DIM] = (acc * sg).astype(BF16)
            dproj_ref[rows, CONV_DIM:2 * CONV_DIM] = (acc * cval * sg * (1.0 - sg)).astype(BF16)

    full = lambda s: pl.BlockSpec(s, lambda i: (0, 0))
    return pl.pallas_call(
        body, name="conv_bwd_dw", grid=(nt,),
        in_specs=[pl.BlockSpec((tm, CONV_DIM), lambda i: (i, 0)),
                  pl.BlockSpec((CONV_HALO, CONV_DIM), lambda i: (jnp.minimum((i + 1) * nh, last_h), 0)),
                  pl.BlockSpec((tm, 2 * CONV_DIM), lambda i: (i, 0)),
                  pl.BlockSpec((CONV_HALO, 2 * CONV_DIM), lambda i: (jnp.maximum(i * nh - 1, 0), 0)),
                  full((CONV_K, CONV_DIM))],
        out_specs=[pl.BlockSpec((tm, 2 * CONV_DIM), lambda i: (i, 0)), full((CONV_HALO, CONV_DIM))],
        out_shape=[jax.ShapeDtypeStruct((t, 2 * CONV_DIM), BF16), jax.ShapeDtypeStruct((CONV_HALO, CONV_DIM), F32)],
        scratch_shapes=[pltpu.VMEM((CONV_HALO + tm, CONV_DIM), F32), pltpu.VMEM((tm + CONV_HALO, CONV_DIM), F32)],
        compiler_params=_params(("arbitrary",)),
    )(d_pre, d_pre, proj, proj, w_dw)


def _lower_bound(logit_ref):
    l0 = logit_ref[0:1, :]
    l1 = logit_ref[1:2, :]
    m = jnp.maximum(l0, l1)
    e0 = jnp.exp(l0 - m)
    e1 = jnp.exp(l1 - m)
    return e0 / (e0 + e1)


def _tri(lower):
    r = lax.broadcasted_iota(jnp.int32, (CHUNK, CHUNK), 0)
    c = lax.broadcasted_iota(jnp.int32, (CHUNK, CHUNK), 1)
    return (c <= r) if lower else (c >= r)


def _hgrn_gates(fz, lb):
    s = _sigmoid(fz)
    sn = _sigmoid(-fz)
    f = lb + (1.0 - lb) * s
    return s, sn, f


def _block_tri(rows, lower=True):
    r = lax.broadcasted_iota(jnp.int32, (rows, rows), 0)
    c = lax.broadcasted_iota(jnp.int32, (rows, rows), 1)
    tri = (c <= r) if lower else (c >= r)
    return (tri & (r // CHUNK == c // CHUNK)).astype(BF16)


def _tri_rows(tm):
    return min(tm, MXU_DEPTH)


def _tri_matmul(tri_ref, x):
    hi = x.astype(BF16)
    lo = (x - hi.astype(F32)).astype(BF16)
    tri = tri_ref[...]
    return (lax.dot_general(tri, hi, NN, preferred_element_type=F32)
            + lax.dot_general(tri, lo, NN, preferred_element_type=F32))


def _groups(tm):
    g = _tri_rows(tm)
    return [slice(i * g, (i + 1) * g) for i in range(tm // g)]


def _hgrn_fwd(proj, logits, norm_g):
    t = proj.shape[0]
    tm = CHUNK * CHUNKS_PER_BLOCK if t % (CHUNK * CHUNKS_PER_BLOCK) == 0 else CHUNK
    cpb = tm // CHUNK
    nt = t // tm
    half = CHUNK // 2

    def body(qz_ref, fz_ref, iv_ref, gz_ref, lg_ref, ng_ref, tri_ref, o_ref, og_ref, st_ref,
             state_ref, qe_ref, ke_ref, qb_ref, kl_ref, v_ref, upd_ref, decay_ref, a_ref, q_ref, kk_ref, b_ref):
        j = pl.program_id(1)

        @pl.when(j == 0)
        def _():
            state_ref[...] = jnp.zeros_like(state_ref)

        lb = _lower_bound(lg_ref)
        chunks = [slice(c * CHUNK, (c + 1) * CHUNK) for c in range(cpb)]
        for rows in chunks:
            qz = qz_ref[rows, :]
            q_ref[rows, :] = qz * _sigmoid(qz)
            _, sn, f = _hgrn_gates(fz_ref[rows, :], lb)
            kk_ref[rows, :] = (1.0 - lb) * sn
            b_ref[rows, :] = jnp.log(f)
            v_ref[rows, :] = iv_ref[rows, :].astype(BF16)
        for rows in _groups(tm):
            b_ref[rows, :] = _tri_matmul(tri_ref, b_ref[rows, :])
        for c, rows in enumerate(chunks):
            b = b_ref[rows, :]
            bref = b[half - 1:half, :]
            blast = b[CHUNK - 1:CHUNK, :]
            q = q_ref[rows, :]
            kk = kk_ref[rows, :]
            qb_ref[rows, :] = (q * jnp.exp(b)).astype(BF16)
            qe_ref[rows, :] = (q * jnp.exp(b - bref)).astype(BF16)
            ke_ref[rows, :] = (kk * jnp.exp(bref - b)).astype(BF16)
            kl_ref[rows, :] = (kk * jnp.exp(blast - b)).astype(BF16)
            decay_ref[c:c + 1, :] = jnp.exp(blast)
        causal = _tri(True)
        for c, rows in enumerate(chunks):
            upd_ref[c] = _dot(v_ref[rows, :], kl_ref[rows, :], TN)
            a_ref[c] = jnp.where(causal, _dot(qe_ref[rows, :], ke_ref[rows, :], NT), 0.0).astype(BF16)
        state = state_ref[...]
        for c in range(cpb):
            st_ref[c] = state.astype(BF16)
            state = state * decay_ref[c:c + 1, :] + upd_ref[c]
        state_ref[...] = state
        for c, rows in enumerate(chunks):
            o_ref[rows, :] = _dot(a_ref[c], v_ref[rows, :], NN) + _dot(qb_ref[rows, :], st_ref[c], NT)
        for rows in chunks:
            o = o_ref[rows, :]
            r = lax.rsqrt(jnp.mean(o * o, axis=-1, keepdims=True) + RMS_EPS)
            gz = gz_ref[rows, :]
            og_ref[rows, :] = ((o * r * ng_ref[...]) * (gz * _sigmoid(gz))).astype(BF16)

    col = lambda base: pl.BlockSpec((tm, HEAD_DIM), lambda h, j: (j, base + h))
    tile_bf = pltpu.VMEM((tm, HEAD_DIM), BF16)
    tile_f32 = pltpu.VMEM((tm, HEAD_DIM), F32)
    return pl.pallas_call(
        body, name="hgrn_fwd", grid=(HEADS, nt),
        in_specs=[col(8), col(16), col(24), col(32),
                  pl.BlockSpec((2, HEAD_DIM), lambda h, j: (0, h)), pl.BlockSpec((1, HEAD_DIM), lambda h, j: (0, h)),
                  pl.BlockSpec((_tri_rows(tm), _tri_rows(tm)), lambda h, j: (0, 0))],
        out_specs=[col(0), col(0), pl.BlockSpec((None, cpb, HEAD_DIM, HEAD_DIM), lambda h, j: (h, j, 0, 0))],
        out_shape=[jax.ShapeDtypeStruct((t, HGRN_DIM), F32), jax.ShapeDtypeStruct((t, HGRN_DIM), BF16),
                   jax.ShapeDtypeStruct((HEADS, t // CHUNK, HEAD_DIM, HEAD_DIM), BF16)],
        scratch_shapes=[pltpu.VMEM((HEAD_DIM, HEAD_DIM), F32), tile_bf, tile_bf, tile_bf, tile_bf, tile_bf,
                        pltpu.VMEM((cpb, HEAD_DIM, HEAD_DIM), F32), pltpu.VMEM((max(cpb, 8), HEAD_DIM), F32),
                        pltpu.VMEM((cpb, CHUNK, CHUNK), BF16), tile_f32, tile_f32, tile_f32],
        compiler_params=_params(("parallel", "arbitrary")),
    )(proj, proj, proj, proj, logits, norm_g, _block_tri(_tri_rows(tm)))


def _hgrn_bwd(d_og, o, proj, states, logits, norm_g, hosted=None):
    t = proj.shape[0]
    tm = CHUNK * CHUNKS_PER_BLOCK if t % (CHUNK * CHUNKS_PER_BLOCK) == 0 else CHUNK
    cpb = tm // CHUNK
    nt = t // tm
    half = CHUNK // 2

    def body(dog_ref, o_ref, qz_ref, fz_ref, iv_ref, gz_ref, st_ref, lg_ref, ng_ref, tril_ref, triu_ref,
             dqz_ref, dfz_ref, div_ref, dgz_ref, sums_ref,
             dstate_ref, qe_ref, ke_ref, qb_ref, kl_ref, v_ref, do_ref, upd_ref, dst_ref, a_ref, da_ref,
             decay_ref, through_ref, q_ref, kk_ref, b_ref, dsilu_ref, gs_ref, gf_ref, sn_ref,
             eb_ref, ebr_ref, ekr_ref, ebl_ref, rev_ref, pre_ref, dk_ref):
        j = pl.program_id(1)

        @pl.when(j == 0)
        def _():
            dstate_ref[...] = jnp.zeros_like(dstate_ref)
            sums_ref[...] = jnp.zeros_like(sums_ref)

        lb = _lower_bound(lg_ref)
        ng = ng_ref[...]
        chunks = [slice(c * CHUNK, (c + 1) * CHUNK) for c in range(cpb)]
        for rows in chunks:
            qz = qz_ref[rows, :]
            sq = _sigmoid(qz)
            q_ref[rows, :] = qz * sq
            dsilu_ref[rows, :] = sq * (1.0 + qz * (1.0 - sq))
            s, sn, f = _hgrn_gates(fz_ref[rows, :], lb)
            kk_ref[rows, :] = (1.0 - lb) * sn
            b_ref[rows, :] = jnp.log(f)
            sn_ref[rows, :] = sn
            gf_ref[rows, :] = sn / f
            gs_ref[rows, :] = (1.0 - lb) * s
            v_ref[rows, :] = iv_ref[rows, :].astype(BF16)
            ov = o_ref[rows, :]
            r = lax.rsqrt(jnp.mean(ov * ov, axis=-1, keepdims=True) + RMS_EPS)
            on = ov * r
            gz = gz_ref[rows, :]
            sg = _sigmoid(gz)
            dog = dog_ref[rows, :]
            dgz_ref[rows, :] = (dog * (on * ng) * (sg * (1.0 + gz * (1.0 - sg)))).astype(BF16)
            d_ong = dog * (gz * sg)
            sums_ref[0:1, :] += _colsum(d_ong * on)
            d_on = d_ong * ng
            do_ref[rows, :] = (r * (d_on - on * jnp.mean(d_on * on, axis=-1, keepdims=True))).astype(BF16)
        for rows in _groups(tm):
            b_ref[rows, :] = _tri_matmul(tril_ref, b_ref[rows, :])
        for c, rows in enumerate(chunks):
            b = b_ref[rows, :]
            bref = b[half - 1:half, :]
            blast = b[CHUNK - 1:CHUNK, :]
            q = q_ref[rows, :]
            kk = kk_ref[rows, :]
            eb = jnp.exp(b)
            ebr = jnp.exp(b - bref)
            ekr = jnp.exp(bref - b)
            ebl = jnp.exp(blast - b)
            eb_ref[rows, :] = eb
            ebr_ref[rows, :] = ebr
            ekr_ref[rows, :] = ekr
            ebl_ref[rows, :] = ebl
            qb_ref[rows, :] = (q * eb).astype(BF16)
            qe_ref[rows, :] = (q * ebr).astype(BF16)
            ke_ref[rows, :] = (kk * ekr).astype(BF16)
            kl_ref[rows, :] = (kk * ebl).astype(BF16)
            decay_ref[c:c + 1, :] = jnp.exp(blast)
        causal = _tri(True)
        for c, rows in enumerate(chunks):
            upd_ref[c] = _dot(do_ref[rows, :], qb_ref[rows, :], TN)
            a_ref[c] = jnp.where(causal, _dot(qe_ref[rows, :], ke_ref[rows, :], NT), 0.0).astype(BF16)
            da_ref[c] = jnp.where(causal, _dot(do_ref[rows, :], v_ref[rows, :], NT), 0.0).astype(BF16)
        dstate = dstate_ref[...]
        for c in reversed(range(cpb)):
            dst_ref[c] = dstate.astype(BF16)
            decay = decay_ref[c:c + 1, :]
            through_ref[c:c + 1, :] = decay * _colsum(dstate * st_ref[c].astype(F32))
            dstate = dstate * decay + upd_ref[c]
        dstate_ref[...] = dstate
        for c, rows in enumerate(chunks):
            qe, ke, v, do = qe_ref[rows, :], ke_ref[rows, :], v_ref[rows, :], do_ref[rows, :]
            div_ref[rows, :] = (_dot(a_ref[c], do, TN) + _dot(kl_ref[rows, :], dst_ref[c], NT)).astype(BF16)
            dqe = _dot(da_ref[c], ke, NN)
            dke = _dot(da_ref[c], qe, TN)
            dq_inter = _dot(do, st_ref[c], NN) * eb_ref[rows, :]
            dk_inter = _dot(v, dst_ref[c], NN) * ebl_ref[rows, :]
            dk_ref[rows, :] = dke * ekr_ref[rows, :] + dk_inter
            dqz_ref[rows, :] = ((dqe * ebr_ref[rows, :] + dq_inter) * dsilu_ref[rows, :]).astype(BF16)
            rev_ref[rows, :] = (qe.astype(F32) * dqe - ke.astype(F32) * dke) + q_ref[rows, :] * dq_inter
            pre_ref[rows, :] = kk_ref[rows, :] * dk_inter
        for rows in _groups(tm):
            pre = pre_ref[rows, :]
            rev_ref[rows, :] = _tri_matmul(triu_ref, rev_ref[rows, :]) + (_tri_matmul(tril_ref, pre) - pre)
        for c, rows in enumerate(chunks):
            dlf = rev_ref[rows, :] + through_ref[c:c + 1, :]
            common = gf_ref[rows, :] * dlf - sn_ref[rows, :] * dk_ref[rows, :]
            dfz_ref[rows, :] = (gs_ref[rows, :] * common).astype(BF16)
            sums_ref[1:2, :] += _colsum(common)

        @pl.when(j == nt - 1)
        def _():
            sums_ref[1:2, :] = sums_ref[1:2, :] * lb * (1.0 - lb)

    rev = lambda base: pl.BlockSpec((tm, HEAD_DIM), lambda h, j: (nt - 1 - j, base + h))
    vec = lambda n: pl.BlockSpec((n, HEAD_DIM), lambda h, j: (0, h))
    const = pl.BlockSpec((_tri_rows(tm), _tri_rows(tm)), lambda h, j: (0, 0))
    bf = jax.ShapeDtypeStruct((t, HGRN_DIM), BF16)
    tile_bf = pltpu.VMEM((tm, HEAD_DIM), BF16)
    tile_f32 = pltpu.VMEM((tm, HEAD_DIM), F32)
    square = lambda dtype: pltpu.VMEM((cpb, HEAD_DIM, HEAD_DIM), dtype)
    rows8 = pltpu.VMEM((max(cpb, 8), HEAD_DIM), F32)
    return _call(
        body, name="hgrn_bwd", grid=(HEADS, nt),
        in_specs=[rev(0), rev(0), rev(8), rev(16), rev(24), rev(32),
                  pl.BlockSpec((None, cpb, HEAD_DIM, HEAD_DIM), lambda h, j: (h, nt - 1 - j, 0, 0)),
                  vec(2), vec(1), const, const],
        out_specs=[rev(0), rev(0), rev(0), rev(0), vec(8)],
        out_shape=[bf, bf, bf, bf, jax.ShapeDtypeStruct((8, HGRN_DIM), F32)],
        scratch_shapes=[pltpu.VMEM((HEAD_DIM, HEAD_DIM), F32)] + [tile_bf] * 6 + [square(F32), square(BF16)]
        + [pltpu.VMEM((cpb, CHUNK, CHUNK), BF16)] * 2 + [rows8, rows8] + [tile_f32] * 14,
        semantics=("parallel", "arbitrary"),
        operands=(d_og, o, proj, proj, proj, proj, states, logits, norm_g, _block_tri(_tri_rows(tm)), _block_tri(_tri_rows(tm), lower=False)),
        hosted=hosted)


def _mix_ln1(proj, y_conv, y_hgrn, w_out, x, g, b):
    t = x.shape[0]
    tm = _pick(t, 512)
    d = D_MODEL

    def lhs(ins, outs):
        for rows in _row_blocks(tm):
            outs[0][rows, :] = (_sigmoid(ins[0][rows, :]) * ins[3][rows, :]
                                + _sigmoid(ins[2][rows, :]) * ins[4][rows, :]).astype(BF16)
        return outs[0][...]

    def epilogue(acc, ins, outs, scr):
        for rows in _row_blocks(tm):
            r = ALPHA * ins[5][rows, :] + acc[rows, :]
            outs[1][rows, :] = r
            xhat, _ = _ln(r)
            outs[2][rows, :] = (xhat * ins[6][...] + ins[7][...]).astype(BF16)

    tile = pl.BlockSpec((tm, d), lambda i, j, k: (i, 0))
    vec = pl.BlockSpec((1, d), lambda i, j, k: (0, 0))
    return _mm_fused(
        "mix_ln1", (t // tm, 1, 1), (proj, w_out, proj, y_conv, y_hgrn, x, g, b),
        [pl.BlockSpec((tm, d), lambda i, j, k: (i, 5)), pl.BlockSpec((d, d), lambda i, j, k: (0, 0)),
         pl.BlockSpec((tm, d), lambda i, j, k: (i, 6)), tile, tile, tile, vec, vec],
        [jax.ShapeDtypeStruct((t, d), BF16), jax.ShapeDtypeStruct((t, d), F32), jax.ShapeDtypeStruct((t, d), BF16)],
        [tile, tile, tile], NN, (tm, d), epilogue, lhs=lhs)


def _d_mixed_merge_bwd(d_r1b, w_out, proj, y_conv, y_hgrn):
    t = proj.shape[0]
    tm = _pick(t, 512)
    d = D_MODEL

    def epilogue(d_mixed, ins, outs, scr):
        dy_ref, dmz_ref = outs
        for rows in _row_blocks(tm):
            dm = d_mixed[rows, :]
            for br in range(2):
                sg = _sigmoid(ins[2 + br][rows, :])
                dy_ref[br, rows, :] = (sg * dm).astype(BF16)
                dmz_ref[rows, br * d:(br + 1) * d] = (dm * ins[4 + br][rows, :] * sg * (1.0 - sg)).astype(BF16)

    tile = pl.BlockSpec((tm, d), lambda i, j, k: (i, 0))
    return _mm_fused(
        "d_mixed_merge_bwd", (t // tm, 1, 1), (d_r1b, w_out, proj, proj, y_conv, y_hgrn),
        [tile, pl.BlockSpec((d, d), lambda i, j, k: (0, 0)), pl.BlockSpec((tm, d), lambda i, j, k: (i, 5)),
         pl.BlockSpec((tm, d), lambda i, j, k: (i, 6)), tile, tile],
        [jax.ShapeDtypeStruct((2, t, d), BF16), jax.ShapeDtypeStruct((t, 2 * d), BF16)],
        [pl.BlockSpec((2, tm, d), lambda i, j, k: (0, i, 0)), pl.BlockSpec((tm, 2 * d), lambda i, j, k: (i, 0))],
        NT, (tm, d), epilogue)


def _ffn_out_ln2(act, w_ffn_out, r1, target, g1, b1, g2, b2):
    t = r1.shape[0]
    tm = _pick(t, 1024)
    nt = t // tm
    d = D_MODEL

    def epilogue(y_ffn, ins, outs, scr):
        r1_ref, tg_ref, g1_ref, b1_ref, g2_ref, b2_ref = ins[2:]
        dr_ref, drb_ref, sums_ref = outs
        (sq_ref,) = scr
        i = pl.program_id(0)

        @pl.when(i == 0)
        def _():
            sums_ref[...] = jnp.zeros_like(sums_ref)
            sq_ref[...] = jnp.zeros_like(sq_ref)

        for rows in _row_blocks(tm):
            xh1, _ = _ln(r1_ref[rows, :])
            x1 = xh1 * g1_ref[...] + b1_ref[...]
            xh2, rstd2 = _ln(ALPHA * x1 + y_ffn[rows, :])
            diff = xh2 * g2_ref[...] + b2_ref[...] - tg_ref[rows, :]
            dy = diff * (1.0 / D_MODEL)
            dr = _ln_bwd(dy, xh2, rstd2, g2_ref[...])
            dr_ref[rows, :] = dr
            drb_ref[rows, :] = dr.astype(BF16)
            sums_ref[0:1, :] += _colsum(dy * xh2)
            sums_ref[1:2, :] += _colsum(dy)
            sq_ref[...] += _colsum(diff * diff)

        @pl.when(i == nt - 1)
        def _():
            total = jnp.sum(sq_ref[...], axis=-1, keepdims=True) * (0.5 / D_MODEL)
            sums_ref[2:3, :] = jnp.broadcast_to(total, (1, D_MODEL))

    tile = pl.BlockSpec((tm, d), lambda i, j, k: (i, 0))
    vec = pl.BlockSpec((1, d), lambda i, j, k: (0, 0))
    return _mm_fused(
        "ffn_out_ln2", (nt, 1, 4), (act, w_ffn_out, r1, target, g1, b1, g2, b2),
        [pl.BlockSpec((None, tm, FF_SHARD), lambda i, j, k: (k, i, 0)),
         pl.BlockSpec((None, FF_SHARD, d), lambda i, j, k: (k, 0, 0)), tile, tile, vec, vec, vec, vec],
        [jax.ShapeDtypeStruct((t, d), F32), jax.ShapeDtypeStruct((t, d), BF16), jax.ShapeDtypeStruct((8, d), F32)],
        [tile, tile, pl.BlockSpec((8, d), lambda i, j, k: (0, 0))], NN, (tm, d), epilogue,
        scratch=[pltpu.VMEM((1, d), F32)])


def _d_x1_ln1_bwd(d_z, w_ffn_in, d_r2, r1, g1):
    t = r1.shape[0]
    tm = _pick(t, 1024)
    d = D_MODEL

    def epilogue(dx_ffn, ins, outs, scr):
        dr2_ref, r1_ref, g_ref = ins[2:]
        dr1_ref, dr1b_ref, sums_ref = outs
        i = pl.program_id(0)

        @pl.when(i == 0)
        def _():
            sums_ref[...] = jnp.zeros_like(sums_ref)

        for rows in _row_blocks(tm):
            xhat, rstd = _ln(r1_ref[rows, :])
            dx1 = ALPHA * dr2_ref[rows, :] + dx_ffn[rows, :]
            dr1 = _ln_bwd(dx1, xhat, rstd, g_ref[...])
            dr1_ref[rows, :] = dr1
            dr1b_ref[rows, :] = dr1.astype(BF16)
            sums_ref[0:1, :] += _colsum(dx1 * xhat)
            sums_ref[1:2, :] += _colsum(dx1)

    tile = pl.BlockSpec((tm, d), lambda i, j, k: (i, 0))
    return _mm_fused(
        "d_x1_ln1_bwd", (t // tm, 1, N_DEV), (d_z, w_ffn_in, d_r2, r1, g1),
        [pl.BlockSpec((None, tm, FF_SHARD), lambda i, j, k: (k, i, 0)),
         pl.BlockSpec((None, d, FF_SHARD), lambda i, j, k: (k, 0, 0)), tile, tile,
         pl.BlockSpec((1, d), lambda i, j, k: (0, 0))],
        [jax.ShapeDtypeStruct((t, d), F32), jax.ShapeDtypeStruct((t, d), BF16), jax.ShapeDtypeStruct((8, d), F32)],
        [tile, tile, pl.BlockSpec((8, d), lambda i, j, k: (0, 0))], NT, (tm, d), epilogue)


def _cast_bf16(x):
    t = x.shape[0]
    tm = _pick(t, 512)

    def body(x_ref, o_ref):
        o_ref[...] = x_ref[...].astype(BF16)

    tile = pl.BlockSpec((tm, D_MODEL), lambda i: (i, 0))
    return pl.pallas_call(
        body, name="cast_x", grid=(t // tm,), in_specs=[tile], out_specs=tile,
        out_shape=jax.ShapeDtypeStruct((t, D_MODEL), BF16), compiler_params=_params(("parallel",)),
    )(x)


def _relayout(name, a, in_block, in_map, out_block, out_map, out_shape):
    def body(a_ref, o_ref):
        o_ref[...] = a_ref[...].astype(o_ref.dtype)

    return pl.pallas_call(
        body, name=name, grid=(N_DEV,), in_specs=[pl.BlockSpec(in_block, in_map)],
        out_specs=pl.BlockSpec(out_block, out_map), out_shape=out_shape, compiler_params=_params(("parallel",)),
    )(a)


_GELU_C = math.sqrt(2.0 / math.pi)


_GELU_CUBIC = 0.044715


def _gelu_parts(u):
    u2 = u * u
    th = jnp.tanh(u * (_GELU_C + (_GELU_C * _GELU_CUBIC) * u2))
    hu = 0.5 * u
    return th, hu + hu * th, u2, hu


BF16_ROWS = 16


def _ffn_act_fwd(z, w_dw, b_dw):
    t = z.shape[2]
    tm = _pick(t, 256)
    nh = tm // FFN_HALO

    def body(z_ref, zh_ref, w_ref, b_ref, act_ref, gd_ref, us_ref):
        i = pl.program_id(1)
        us_ref[0:FFN_HALO, :] = jnp.where(i == 0, 0.0, zh_ref[...])
        us_ref[FFN_HALO:FFN_HALO + tm, :] = z_ref[0]
        for r in range(tm // ROW_BLOCK):
            base = r * ROW_BLOCK
            rows = slice(base, base + ROW_BLOCK)
            uc = jnp.broadcast_to(b_ref[...], (ROW_BLOCK, FF_SHARD))
            for k in range(FFN_K):
                off = base + FFN_HALO - (FFN_K - 1) + k
                uc = uc + w_ref[k:k + 1, :] * us_ref[off:off + ROW_BLOCK, :]
            th, gelu, u2, hu = _gelu_parts(uc)
            dgelu = (0.5 + 0.5 * th) + (hu - hu * th * th) * (_GELU_C + (3.0 * _GELU_C * _GELU_CUBIC) * u2)
            act_ref[rows, :] = (gelu * z_ref[1, rows, :]).astype(BF16)
            gd_ref[0, rows, :] = gelu.astype(BF16)
            gd_ref[1, rows, :] = dgelu.astype(BF16)

    return pl.pallas_call(
        body, name="ffn_act_fwd", grid=(4, t // tm),
        in_specs=[pl.BlockSpec((2, None, tm, FF_SHARD), lambda j, i: (0, j, i, 0)),
                  pl.BlockSpec((None, None, FFN_HALO, FF_SHARD), lambda j, i: (0, j, jnp.maximum(i * nh - 1, 0), 0)),
                  pl.BlockSpec((None, FFN_K, FF_SHARD), lambda j, i: (j, 0, 0)),
                  pl.BlockSpec((None, 1, FF_SHARD), lambda j, i: (j, 0, 0))],
        out_specs=[pl.BlockSpec((None, tm, FF_SHARD), lambda j, i: (j, i, 0)),
                   pl.BlockSpec((2, None, tm, FF_SHARD), lambda j, i: (0, j, i, 0))],
        out_shape=[jax.ShapeDtypeStruct((4, t, FF_SHARD), BF16), jax.ShapeDtypeStruct((2, 4, t, FF_SHARD), BF16)],
        scratch_shapes=[pltpu.VMEM((FFN_HALO + tm, FF_SHARD), F32)],
        compiler_params=_params(("parallel", "arbitrary")),
    )(z, z, w_dw, b_dw)


def _ffn_act_bwd(d_act, z, gd, w_dw):
    t = z.shape[2]
    tm = _pick(t, 256)
    nt = t // tm
    nh = tm // FFN_HALO
    last_h = t // FFN_HALO - 1
    pad = FFN_HALO - (FFN_K - 1)

    def body(da_ref, dah_ref, z_ref, zp_ref, gn_ref, gd_ref, gdn_ref, w_ref, dz_ref, sums_ref, us_ref, ds_ref):
        i = pl.program_id(1)

        @pl.when(i == 0)
        def _():
            sums_ref[...] = jnp.zeros_like(sums_ref)

        us_ref[0:FFN_HALO, :] = jnp.where(i == 0, 0.0, zp_ref[...])
        us_ref[FFN_HALO:FFN_HALO + tm, :] = z_ref[0]
        for r in range(tm // ROW_BLOCK):
            base = r * ROW_BLOCK
            rows = slice(base, base + ROW_BLOCK)
            da = da_ref[rows, :]
            dz_ref[1, rows, :] = (da * gd_ref[0, rows, :].astype(F32)).astype(BF16)
            duc = da * z_ref[1, rows, :] * gd_ref[1, rows, :].astype(F32)
            ds_ref[rows, :] = duc
            for k in range(FFN_K):
                sums_ref[k:k + 1, :] += _colsum(duc * us_ref[base + pad + k:base + pad + k + ROW_BLOCK, :])
            sums_ref[FFN_K:FFN_K + 1, :] += _colsum(duc)
        duc_next = dah_ref[...] * gn_ref[...] * gdn_ref[0:FFN_HALO, :].astype(F32)
        ds_ref[tm:tm + FFN_HALO, :] = jnp.where(i == nt - 1, 0.0, duc_next)
        for r in range(tm // ROW_BLOCK):
            base = r * ROW_BLOCK
            du = jnp.zeros((ROW_BLOCK, FF_SHARD), F32)
            for k in range(FFN_K):
                off = base + FFN_K - 1 - k
                du = du + w_ref[k:k + 1, :] * ds_ref[off:off + ROW_BLOCK, :]
            dz_ref[0, base:base + ROW_BLOCK, :] = du.astype(BF16)

    nxt = lambda i: jnp.minimum((i + 1) * nh, last_h)
    nxt_bf = lambda i: jnp.minimum((i + 1) * (tm // BF16_ROWS), t // BF16_ROWS - 1)
    return pl.pallas_call(
        body, name="ffn_act_bwd", grid=(4, nt),
        in_specs=[pl.BlockSpec((None, tm, FF_SHARD), lambda j, i: (j, i, 0)),
                  pl.BlockSpec((None, FFN_HALO, FF_SHARD), lambda j, i: (j, nxt(i), 0)),
                  pl.BlockSpec((2, None, tm, FF_SHARD), lambda j, i: (0, j, i, 0)),
                  pl.BlockSpec((None, None, FFN_HALO, FF_SHARD), lambda j, i: (0, j, jnp.maximum(i * nh - 1, 0), 0)),
                  pl.BlockSpec((None, None, FFN_HALO, FF_SHARD), lambda j, i: (1, j, nxt(i), 0)),
                  pl.BlockSpec((2, None, tm, FF_SHARD), lambda j, i: (0, j, i, 0)),
                  pl.BlockSpec((None, None, BF16_ROWS, FF_SHARD), lambda j, i: (1, j, nxt_bf(i), 0)),
                  pl.BlockSpec((None, FFN_K, FF_SHARD), lambda j, i: (j, 0, 0))],
        out_specs=[pl.BlockSpec((2, None, tm, FF_SHARD), lambda j, i: (0, j, i, 0)),
                   pl.BlockSpec((None, 8, FF_SHARD), lambda j, i: (j, 0, 0))],
        out_shape=[jax.ShapeDtypeStruct((2, 4, t, FF_SHARD), BF16), jax.ShapeDtypeStruct((4, 8, FF_SHARD), F32)],
        scratch_shapes=[pltpu.VMEM((FFN_HALO + tm, FF_SHARD), F32), pltpu.VMEM((tm + FFN_HALO, FF_SHARD), F32)],
        compiler_params=_params(("parallel", "arbitrary")),
    )(d_act, d_act, z, z, z, gd, gd, w_dw)


def _local_step(x, target, w_in, rest, small, scatter=None):
    t = x.shape[0]
    tm = _pick(t, 2048)
    tk = _pick(t, 2048)
    nm = t // tm
    nk = t // tk
    d = D_MODEL

    xb = _cast_bf16(x)
    proj = _mm("proj", xb, w_in, (t, IN_COLS), F32, (nm, N_DEV, 1),
               pl.BlockSpec((tm, d), lambda i, j, k: (i, 0)),
               pl.BlockSpec((None, d, IN_SHARD), lambda i, j, k: (j, 0, 0)),
               pl.BlockSpec((tm, IN_SHARD), lambda i, j, k: (i, j)), NN, (tm, IN_SHARD),
               hosted=rest if isinstance(rest, _Hosted) else None)
    if isinstance(rest, _Hosted):
        proj, rest = proj
    w_conv_out8, w_hgrn_out8, w_out8, w_ffn_in, w_ffn_out8, conv_dw8, ffn_dw8 = rest
    w_conv_out = _relayout("w_conv_out_natural", w_conv_out8, (None, CONV_DIM, 128), lambda j: (j, 0, 0),
                           (CONV_DIM, 128), lambda j: (0, j), jax.ShapeDtypeStruct((CONV_DIM, d), BF16))
    w_hgrn_out = w_hgrn_out8.reshape(d, d)
    w_out = w_out8.reshape(d, d)
    w_ffn_out = w_ffn_out8.reshape(4, FF_SHARD, d)
    conv_dw = jnp.transpose(conv_dw8[:, :CONV_K, :CONV_DIM // N_DEV], (1, 0, 2)).reshape(CONV_K, CONV_DIM)
    ffn_dw = jnp.transpose(ffn_dw8[:, :FFN_K, :D_FF // N_DEV], (1, 0, 2)).reshape(FFN_K, 4, FF_SHARD)
    small = dict(small, w_conv_dw=conv_dw, w_ffn_dw=jnp.transpose(ffn_dw, (1, 0, 2)),
                 b_ffn_dw=small["b_ffn_dw"].reshape(4, 1, FF_SHARD))

    c_act, conv_pre = _conv_fwd(proj, small["w_conv_dw"], small["b_conv_dw"], small["conv_ln_g"], small["conv_ln_b"])
    y_conv = _mm("y_conv", c_act, w_conv_out, (t, d), F32, (nm, 1, 1),
                 pl.BlockSpec((tm, CONV_DIM), lambda i, j, k: (i, 0)),
                 pl.BlockSpec((CONV_DIM, d), lambda i, j, k: (0, 0)),
                 pl.BlockSpec((tm, d), lambda i, j, k: (i, 0)), NN, (tm, d))
    o, og, states = _hgrn_fwd(proj, small["hgrn_lb_logits"], small["hgrn_norm_g"])
    sq_w = pl.BlockSpec((d, d), lambda i, j, k: (0, 0))
    row_tile = pl.BlockSpec((tm, d), lambda i, j, k: (i, 0))
    y_hgrn = _mm("y_hgrn", og, w_hgrn_out, (t, d), F32, (nm, 1, 1), row_tile, sq_w, row_tile, NN, (tm, d))
    mixed, r1, x1b = _mix_ln1(proj, y_conv, y_hgrn, w_out, x, small["ln1_g"], small["ln1_b"])
    z = _mm("ffn_in", x1b, w_ffn_in, (N_DEV, t, FF_SHARD), F32, (nm, N_DEV, 1), row_tile,
            pl.BlockSpec((None, d, FF_SHARD), lambda i, j, k: (j, 0, 0)),
            pl.BlockSpec((None, tm, FF_SHARD), lambda i, j, k: (j, i, 0)), NN, (tm, FF_SHARD))
    z = z.reshape(2, 4, t, FF_SHARD)
    act, gelu_and_slope = _ffn_act_fwd(z, small["w_ffn_dw"], small["b_ffn_dw"])

    d_r2, d_r2b, sums_ln2 = _ffn_out_ln2(act, w_ffn_out, r1, target, small["ln1_g"], small["ln1_b"],
                                         small["ln2_g"], small["ln2_b"])
    d_act = _mm("d_act", d_r2b, w_ffn_out, (4, t, FF_SHARD), F32, (nm, 4, 1), row_tile,
                pl.BlockSpec((None, FF_SHARD, d), lambda i, j, k: (j, 0, 0)),
                pl.BlockSpec((None, tm, FF_SHARD), lambda i, j, k: (j, i, 0)), NT, (tm, FF_SHARD))
    g_w_ffn_out = _mm("g_w_ffn_out", act, d_r2b, (4, FF_SHARD, d), BF16, (4, 1, nk),
                      pl.BlockSpec((None, tk, FF_SHARD), lambda i, j, k: (i, k, 0)),
                      pl.BlockSpec((tk, d), lambda i, j, k: (k, 0)),
                      pl.BlockSpec((None, FF_SHARD, d), lambda i, j, k: (i, 0, 0)), TN, (FF_SHARD, d))
    d_z, sums_ffn = _ffn_act_bwd(d_act, z, gelu_and_slope, small["w_ffn_dw"])
    d_z8 = d_z.reshape(N_DEV, t, FF_SHARD)
    d_r1, d_r1b, sums_ln1 = _d_x1_ln1_bwd(d_z8, w_ffn_in, d_r2, r1, small["ln1_g"])
    g_w_ffn_in = _mm("g_w_ffn_in", x1b, d_z8, (N_DEV, d, FF_SHARD), BF16, (N_DEV, 1, nk),
                     pl.BlockSpec((tk, d), lambda i, j, k: (k, 0)),
                     pl.BlockSpec((None, tk, FF_SHARD), lambda i, j, k: (i, k, 0)),
                     pl.BlockSpec((None, d, FF_SHARD), lambda i, j, k: (i, 0, 0)), TN, (d, FF_SHARD))
    k_tile = pl.BlockSpec((tk, d), lambda i, j, k: (k, 0))
    g_w_out = _mm("g_w_out", mixed, d_r1b, (d, d), BF16, (1, 1, nk), k_tile, k_tile, sq_w, TN, (d, d))
    d_y, d_mz = _d_mixed_merge_bwd(d_r1b, w_out, proj, y_conv, y_hgrn)
    d_pre, sums_conv = _d_c_norm_bwd(d_y, w_conv_out, conv_pre, small["conv_ln_g"], small["conv_ln_b"])
    g_w_conv_out = _mm("g_w_conv_out", c_act, d_y, (CONV_DIM, d), BF16, (1, 1, nk),
                       pl.BlockSpec((tk, CONV_DIM), lambda i, j, k: (k, 0)),
                       pl.BlockSpec((None, tk, d), lambda i, j, k: (0, k, 0)),
                       pl.BlockSpec((CONV_DIM, d), lambda i, j, k: (0, 0)), TN, (CONV_DIM, d))
    g_w_conv_out = _relayout("g_w_conv_out_shards", g_w_conv_out, (CONV_DIM, 128), lambda j: (0, j),
                             (None, CONV_DIM, 128), lambda j: (j, 0, 0),
                             jax.ShapeDtypeStruct((N_DEV, CONV_DIM, 128), BF16))
    d_og = _mm("d_og", d_y, w_hgrn_out, (t, d), F32, (nm, 1, 1),
               pl.BlockSpec((None, tm, d), lambda i, j, k: (1, i, 0)), sq_w, row_tile, NT, (tm, d))
    g_w_hgrn_out = _mm("g_w_hgrn_out", og, d_y, (d, d), BF16, (1, 1, nk), k_tile,
                       pl.BlockSpec((None, tk, d), lambda i, j, k: (1, k, 0)), sq_w, TN, (d, d))
    d_cproj, g_w_conv_dw = _conv_bwd_dw(d_pre, proj, small["w_conv_dw"])

    early = [g_w_conv_out, g_w_hgrn_out.reshape(N_DEV, d // N_DEV, d), g_w_out.reshape(N_DEV, d // N_DEV, d),
             g_w_ffn_in, g_w_ffn_out.reshape(N_DEV, D_FF // N_DEV, d)]
    hgrn_out = _hgrn_bwd(d_og, o, proj, states, small["hgrn_lb_logits"], small["hgrn_norm_g"],
                         hosted=None if scatter is None else scatter(early))
    d_qz, d_fz, d_iv, d_gz, sums_hgrn = hgrn_out[:5]
    early_recv = list(hgrn_out[5:])
    d_proj = jnp.concatenate([d_cproj, d_qz, d_fz, d_iv, d_gz, d_mz], axis=1)
    g_w_in = _mm("g_w_in", xb, d_proj, (N_DEV, d, IN_SHARD), BF16, (N_DEV, 1, nk), k_tile,
                 pl.BlockSpec((tk, IN_SHARD), lambda i, j, k: (k, i)),
                 pl.BlockSpec((None, d, IN_SHARD), lambda i, j, k: (i, 0, 0)), TN, (d, IN_SHARD))
    def add_residual(acc, ins, outs, scr):
        for rows in _row_blocks(ta):
            outs[0][rows, :] = ALPHA * ins[2][rows, :] + acc[rows, :]

    ta = _pick(t, 1024)
    acc_tile = pl.BlockSpec((ta, d), lambda i, j, k: (i, 0))
    grad_x, *late_recv = _mm_fused(
        "grad_x", (t // ta, 1, N_DEV), (d_proj, w_in, d_r1),
        [pl.BlockSpec((ta, IN_SHARD), lambda i, j, k: (i, k)),
         pl.BlockSpec((None, d, IN_SHARD), lambda i, j, k: (k, 0, 0)), acc_tile],
        [jax.ShapeDtypeStruct((t, d), F32)], [acc_tile], NT, (ta, d), add_residual,
        hosted=None if scatter is None else scatter([g_w_in]))

    d_l0 = sums_hgrn[1:2]
    small_grads = {
        "loss": sums_ln2[2:3, 0:128],
        "b_conv_dw": sums_conv[2:3], "conv_ln_g": sums_conv[0:1], "conv_ln_b": sums_conv[1:2],
        "hgrn_lb_logits": jnp.concatenate([d_l0, -d_l0], axis=1),
        "hgrn_norm_g": sums_hgrn[0:1],
        "ln1_g": sums_ln1[0:1], "ln1_b": sums_ln1[1:2],
        "b_ffn_dw": sums_ffn[:, FFN_K, :].reshape(1, D_FF),
        "ln2_g": sums_ln2[0:1], "ln2_b": sums_ln2[1:2],
        "w_conv_dw": g_w_conv_dw[0:CONV_K].reshape(1, CONV_K * CONV_DIM),
        "w_ffn_dw": jnp.transpose(sums_ffn[:, 0:FFN_K, :], (1, 0, 2)).reshape(1, FFN_K * D_FF),
    }
    large_grads = [g_w_in] + early
    if scatter is not None:
        large_grads = list(zip(large_grads, late_recv + early_recv))
    return grad_x, large_grads, small_grads


def _coords():
    return lax.axis_index("x"), lax.axis_index("y"), lax.axis_index("c")


def _gather(shards):
    n = len(shards)

    def parts(ins, outs, sems):
        send_sems, recv_sems, local_sems = sems
        x, y, c = _coords()
        me = 4 * x + 2 * y + c
        sibling = (x, y, 1 - c)
        chips = [(1 - x, y), (x, 1 - y), (1 - x, 1 - y)]

        def copy(a, k, block, to, src=None):
            return pltpu.make_async_remote_copy(
                src_ref=outs[a].at[block] if src is None else src, dst_ref=outs[a].at[block],
                send_sem=send_sems.at[a, k], recv_sem=recv_sems.at[a, k], device_id=to, device_id_type=MESH)

        local = [pltpu.make_async_copy(ins[a], outs[a].at[me], local_sems.at[a]) for a in range(n)]
        first = []
        for a in range(n):
            first.append(copy(a, 0, me, sibling, src=ins[a]))
            for j, chip in enumerate(chips):
                first.append(copy(a, 1 + j, me, (*chip, c), src=ins[a]))
        return x, y, c, sibling, chips, copy, local, first

    def start(ins, outs, sems):
        *_, local, first = parts(ins, outs, sems)
        for cp in local + first:
            cp.start()

    def finish(ins, outs, sems):
        x, y, c, sibling, chips, copy, local, first = parts(ins, outs, sems)
        passed = []
        for j, (px, py) in enumerate(chips):
            for a in range(n):
                copy(a, 1 + j, 4 * px + 2 * py + c, sibling).wait_recv()
                cp = copy(a, 4 + j, 4 * px + 2 * py + c, sibling)
                cp.start()
                passed.append(cp)
        for a in range(n):
            copy(a, 0, 4 * x + 2 * y + 1 - c, sibling).wait_recv()
            for j, (px, py) in enumerate(chips):
                copy(a, 4 + j, 4 * px + 2 * py + 1 - c, sibling).wait_recv()
        for cp in first + passed:
            cp.wait_send()
        for cp in local:
            cp.wait()

    return _Hosted(shards, [jax.ShapeDtypeStruct((N_DEV,) + s.shape, s.dtype) for s in shards],
                   [pltpu.SemaphoreType.DMA((n, 7)), pltpu.SemaphoreType.DMA((n, 7)), pltpu.SemaphoreType.DMA((n,))],
                   start, finish)


def _scatter(grads):
    n = len(grads)

    def copies(ins, outs, sems):
        send_sems, recv_sems = sems
        x, y, c = _coords()
        out = []
        for a in range(n):
            for k in range(1, N_DEV):
                px, py, pc = x ^ (k >> 2), y ^ ((k >> 1) & 1), c ^ (k & 1)
                out.append(pltpu.make_async_remote_copy(
                    src_ref=ins[a].at[4 * px + 2 * py + pc], dst_ref=outs[a].at[k - 1],
                    send_sem=send_sems.at[a, k - 1], recv_sem=recv_sems.at[a, k - 1],
                    device_id=(px, py, pc), device_id_type=MESH))
        return out

    def start(ins, outs, sems):
        for cp in copies(ins, outs, sems):
            cp.start()

    def finish(ins, outs, sems):
        for cp in copies(ins, outs, sems):
            cp.wait()

    return _Hosted(grads, [jax.ShapeDtypeStruct((N_DEV - 1,) + g.shape[1:], g.dtype) for g in grads],
                   [pltpu.SemaphoreType.DMA((n, N_DEV - 1)), pltpu.SemaphoreType.DMA((n, N_DEV - 1))], start, finish)


def _row_tile(rows):
    return 256 if rows % 256 == 0 else rows


def _adam_math(w, g, m, v):
    m_new = ADAM_B1 * m + (1.0 - ADAM_B1) * g
    v_new = ADAM_B2 * v + (1.0 - ADAM_B2) * (g * g)
    m_hat = m_new / (1.0 - ADAM_B1 ** ADAM_STEP)
    v_hat = v_new / (1.0 - ADAM_B2 ** ADAM_STEP)
    delta = -ADAM_LR * (m_hat / (jnp.sqrt(v_hat) + ADAM_EPS) + ADAM_WD * w)
    return delta, m_new, v_new


def _adam_large(name, own, recv, me, w, m, v):
    rows, cols = w.shape
    tr = _row_tile(rows)

    def body(me_ref, p_ref, r_ref, w_ref, m_ref, v_ref, g_out, d_out, m_out, v_out):
        g = p_ref[...].astype(F32)
        for k in range(N_DEV - 1):
            g = g + r_ref[k].astype(F32)
        delta, m_new, v_new = _adam_math(w_ref[...], g, m_ref[...], v_ref[...])
        g_out[...] = g
        d_out[...] = delta
        m_out[...] = m_new
        v_out[...] = v_new

    tile = pl.BlockSpec((tr, cols), lambda r, me_ref: (r, 0))
    sds = jax.ShapeDtypeStruct((rows, cols), F32)
    return pl.pallas_call(
        body, name=name,
        grid_spec=pltpu.PrefetchScalarGridSpec(
            num_scalar_prefetch=1, grid=(rows // tr,),
            in_specs=[pl.BlockSpec((None, tr, cols), lambda r, me_ref: (me_ref[0], r, 0)),
                      pl.BlockSpec((N_DEV - 1, tr, cols), lambda r, me_ref: (0, r, 0)), tile, tile, tile],
            out_specs=[tile, tile, tile, tile]),
        out_shape=[sds, sds, sds, sds],
        compiler_params=_params(("parallel",)),
    )(me, own, recv, w, m, v)


def _small_allreduce(vec):
    rows = vec.shape[0]

    def body(v_ref, o_ref, gat_ref, send_sems, recv_sems):
        x, y, c = _coords()
        me = 4 * x + 2 * y + c
        gat_ref[me] = v_ref[...]
        copies = []
        for k in range(1, N_DEV):
            px, py, pc = x ^ (k >> 2), y ^ ((k >> 1) & 1), c ^ (k & 1)
            copies.append(pltpu.make_async_remote_copy(
                src_ref=v_ref, dst_ref=gat_ref.at[me], send_sem=send_sems.at[k - 1], recv_sem=recv_sems.at[k - 1],
                device_id=(px, py, pc), device_id_type=MESH))
        for cp in copies:
            cp.start()
        for k in range(1, N_DEV):
            px, py, pc = x ^ (k >> 2), y ^ ((k >> 1) & 1), c ^ (k & 1)
            pltpu.make_async_remote_copy(
                src_ref=v_ref, dst_ref=gat_ref.at[4 * px + 2 * py + pc], send_sem=send_sems.at[k - 1],
                recv_sem=recv_sems.at[k - 1], device_id=(px, py, pc), device_id_type=MESH).wait_recv()
        for cp in copies:
            cp.wait_send()
        acc = gat_ref[0]
        for dev in range(1, N_DEV):
            acc = acc + gat_ref[dev]
        o_ref[...] = acc

    whole = pl.BlockSpec(memory_space=pltpu.VMEM)
    return pl.pallas_call(
        body, name="small_allreduce", in_specs=[whole], out_specs=whole,
        out_shape=jax.ShapeDtypeStruct((rows, 128), F32),
        scratch_shapes=[pltpu.VMEM((N_DEV, rows, 128), F32), pltpu.SemaphoreType.DMA((N_DEV - 1,)),
                        pltpu.SemaphoreType.DMA((N_DEV - 1,))],
        compiler_params=pltpu.CompilerParams(has_side_effects=True, vmem_limit_bytes=VMEM_LIMIT),
    )(vec)


def _adam_small(w, g, m, v):
    def body(w_ref, g_ref, m_ref, v_ref, d_out, m_out, v_out):
        delta, m_new, v_new = _adam_math(w_ref[...], g_ref[...], m_ref[...], v_ref[...])
        d_out[...] = delta
        m_out[...] = m_new
        v_out[...] = v_new

    whole = pl.BlockSpec(memory_space=pltpu.VMEM)
    sds = jax.ShapeDtypeStruct(w.shape, F32)
    return pl.pallas_call(body, name="adam_small", in_specs=[whole] * 4, out_specs=[whole] * 3,
                          out_shape=[sds, sds, sds])(w, g, m, v)


_SMALL_ORDER = ["loss", "b_conv_dw", "conv_ln_g", "conv_ln_b", "hgrn_lb_logits", "hgrn_norm_g", "ln1_g", "ln1_b",
                "b_ffn_dw", "ln2_g", "ln2_b", "w_conv_dw", "w_ffn_dw"]
_WEIGHTS = ["w_in", "w_conv_dw", "b_conv_dw", "conv_ln_g", "conv_ln_b", "w_conv_out", "hgrn_lb_logits", "hgrn_norm_g",
            "w_hgrn_out", "w_out", "ln1_g", "ln1_b", "w_ffn_in", "w_ffn_dw", "b_ffn_dw", "w_ffn_out", "ln2_g", "ln2_b"]
_LARGE = ["w_in", "w_conv_out", "w_hgrn_out", "w_out", "w_ffn_in", "w_ffn_out"]
_CONV_DW_SHARD = CONV_DIM // N_DEV
_FFN_DW_SHARD = D_FF // N_DEV


def kernel(x, w_in, w_conv_dw, b_conv_dw, conv_ln_g, conv_ln_b, w_conv_out, hgrn_lb_logits, hgrn_norm_g, w_hgrn_out, w_out, ln1_g, ln1_b, w_ffn_in, w_ffn_dw, b_ffn_dw, w_ffn_out, ln2_g, ln2_b, loss_target, m_w_in, m_w_conv_dw, m_b_conv_dw, m_conv_ln_g, m_conv_ln_b, m_w_conv_out, m_hgrn_lb_logits, m_hgrn_norm_g, m_w_hgrn_out, m_w_out, m_ln1_g, m_ln1_b, m_w_ffn_in, m_w_ffn_dw, m_b_ffn_dw, m_w_ffn_out, m_ln2_g, m_ln2_b, v_w_in, v_w_conv_dw, v_b_conv_dw, v_conv_ln_g, v_conv_ln_b, v_w_conv_out, v_hgrn_lb_logits, v_hgrn_norm_g, v_w_hgrn_out, v_w_out, v_ln1_g, v_ln1_b, v_w_ffn_in, v_w_ffn_dw, v_b_ffn_dw, v_w_ffn_out, v_ln2_g, v_ln2_b):
    w = dict(w_in=w_in, w_conv_dw=w_conv_dw, b_conv_dw=b_conv_dw, conv_ln_g=conv_ln_g, conv_ln_b=conv_ln_b,
             w_conv_out=w_conv_out, hgrn_lb_logits=hgrn_lb_logits, hgrn_norm_g=hgrn_norm_g, w_hgrn_out=w_hgrn_out,
             w_out=w_out, ln1_g=ln1_g, ln1_b=ln1_b, w_ffn_in=w_ffn_in, w_ffn_dw=w_ffn_dw, b_ffn_dw=b_ffn_dw,
             w_ffn_out=w_ffn_out, ln2_g=ln2_g, ln2_b=ln2_b)
    m = dict(w_in=m_w_in, w_conv_dw=m_w_conv_dw, b_conv_dw=m_b_conv_dw, conv_ln_g=m_conv_ln_g, conv_ln_b=m_conv_ln_b,
             w_conv_out=m_w_conv_out, hgrn_lb_logits=m_hgrn_lb_logits, hgrn_norm_g=m_hgrn_norm_g,
             w_hgrn_out=m_w_hgrn_out, w_out=m_w_out, ln1_g=m_ln1_g, ln1_b=m_ln1_b, w_ffn_in=m_w_ffn_in,
             w_ffn_dw=m_w_ffn_dw, b_ffn_dw=m_b_ffn_dw, w_ffn_out=m_w_ffn_out, ln2_g=m_ln2_g, ln2_b=m_ln2_b)
    v = dict(w_in=v_w_in, w_conv_dw=v_w_conv_dw, b_conv_dw=v_b_conv_dw, conv_ln_g=v_conv_ln_g, conv_ln_b=v_conv_ln_b,
             w_conv_out=v_w_conv_out, hgrn_lb_logits=v_hgrn_lb_logits, hgrn_norm_g=v_hgrn_norm_g,
             w_hgrn_out=v_w_hgrn_out, w_out=v_w_out, ln1_g=v_ln1_g, ln1_b=v_ln1_b, w_ffn_in=v_w_ffn_in,
             w_ffn_dw=v_w_ffn_dw, b_ffn_dw=v_b_ffn_dw, w_ffn_out=v_w_ffn_out, ln2_g=v_ln2_g, ln2_b=v_ln2_b)
    xi, yi, ci = lax.axis_index("x"), lax.axis_index("y"), lax.axis_index("c")
    me = 4 * xi + 2 * yi + ci
    me_op = jnp.reshape(me, (1,)).astype(jnp.int32)

    shards = [w[name][0].astype(BF16) for name in _LARGE]
    shards.append(jnp.pad(w_conv_dw[0], ((0, 1), (0, 128 - _CONV_DW_SHARD))))
    shards.append(jnp.pad(w_ffn_dw[0], ((0, 8 - FFN_K), (0, 384 - _FFN_DW_SHARD))))
    (w_in_all,) = _run_hosted("all_gather_w_in", _gather(shards[:1]))
    small = dict(b_conv_dw=b_conv_dw, conv_ln_g=conv_ln_g, conv_ln_b=conv_ln_b, hgrn_lb_logits=hgrn_lb_logits,
                 hgrn_norm_g=hgrn_norm_g, ln1_g=ln1_g, ln1_b=ln1_b, ln2_g=ln2_g, ln2_b=ln2_b, b_ffn_dw=b_ffn_dw)

    grad_x, large_grads, small_grads = _local_step(x[0], loss_target[0], w_in_all, _gather(shards[1:]), small, _scatter)

    out = {}
    for name, (own, recv) in zip(_LARGE, large_grads):
        out[name] = _adam_large("adam_" + name, own, recv, me_op, w[name][0], m[name][0], v[name][0])

    vec = jnp.concatenate([small_grads[name] for name in _SMALL_ORDER], axis=1)
    total = _small_allreduce(vec.reshape(-1, 128)).reshape(1, -1)
    sizes = [small_grads[name].shape[1] for name in _SMALL_ORDER]
    offs = [0]
    for s in sizes:
        offs.append(offs[-1] + s)
    summed = {name: total[:, offs[i]:offs[i + 1]] for i, name in enumerate(_SMALL_ORDER)}
    loss = summed["loss"][0, 0]
    conv_dw_g = lax.dynamic_slice_in_dim(summed["w_conv_dw"].reshape(CONV_K, CONV_DIM), me * _CONV_DW_SHARD, _CONV_DW_SHARD, axis=1)
    ffn_dw_g = lax.dynamic_slice_in_dim(summed["w_ffn_dw"].reshape(FFN_K, D_FF), me * _FFN_DW_SHARD, _FFN_DW_SHARD, axis=1)
    small_g = dict(summed, w_conv_dw=conv_dw_g.reshape(1, -1), w_ffn_dw=ffn_dw_g.reshape(1, -1))
    names = [n for n in _SMALL_ORDER if n != "loss"]
    flat = lambda d, n: d[n].reshape(1, -1)
    n_small = sum(small_g[n].shape[1] for n in names)
    pad = (-n_small) % 1024
    pack = lambda pieces: jnp.pad(jnp.concatenate(pieces, axis=1), ((0, 0), (0, pad))).reshape(-1, 128)
    d_s, m_s, v_s = _adam_small(pack([flat(w, n) for n in names]), pack([small_g[n] for n in names]),
                                pack([flat(m, n) for n in names]), pack([flat(v, n) for n in names]))
    pos = 0
    for n in names:
        size = small_g[n].shape[1]
        cut = lambda a: a.reshape(1, -1)[:, pos:pos + size].reshape(w[n].shape)
        out[n] = (small_g[n].reshape(w[n].shape), cut(d_s), cut(m_s), cut(v_s))
        pos += size

    for name in _LARGE:
        out[name] = tuple(a.reshape(w[name].shape) for a in out[name])
    grads = [out[n][0] for n in _WEIGHTS]
    deltas = [out[n][1] for n in _WEIGHTS]
    new_m = [out[n][2] for n in _WEIGHTS]
    new_v = [out[n][3] for n in _WEIGHTS]
    return (loss, grad_x[None], *grads, *deltas, *new_m, *new_v)
```

```python
import functools
import math

import jax
import jax.numpy as jnp
from jax import lax
from jax.experimental import pallas as pl
from jax.experimental.pallas import tpu as pltpu

F32 = jnp.float32
BF16 = jnp.bfloat16

N_DEV = 8
D_MODEL = 1024
CONV_DIM = 512
CONV_K = 31
HGRN_DIM = 1024
HEADS = 8
HEAD_DIM = 128
D_FF = 2816
FFN_K = 3
FF_SHARD = 2 * D_FF // N_DEV
IN_COLS = 7168
IN_SHARD = IN_COLS // N_DEV
LN_EPS = 1e-5
RMS_EPS = 1e-6
ALPHA = 2.0 ** 0.25

ADAM_LR = 0.001
ADAM_B1 = 0.9
ADAM_B2 = 0.999
ADAM_EPS = 1e-08
ADAM_WD = 0.01
ADAM_STEP = 10

CHUNK = 64
CHUNKS_PER_BLOCK = 8
CONV_HALO = 32
FFN_HALO = 8
ROW_BLOCK = 64
SUBLANES = 8
VMEM_LIMIT = 48 * 1024 * 1024
MXU_DEPTH = 256

MESH = pl.DeviceIdType.MESH
ANY = pl.BlockSpec(memory_space=pl.ANY)

NN = (((1,), (0,)), ((), ()))
NT = (((1,), (1,)), ((), ()))
TN = (((0,), (0,)), ((), ()))


def _params(sem):
    return pltpu.CompilerParams(dimension_semantics=sem, vmem_limit_bytes=VMEM_LIMIT)


def _dot(a, b, dims):
    return lax.dot_general(a.astype(BF16), b.astype(BF16), dims, preferred_element_type=F32)


def _sigmoid(x):
    return jax.nn.sigmoid(x)


def _ln(r):
    mu = jnp.mean(r, axis=-1, keepdims=True)
    xc = r - mu
    var = jnp.mean(xc * xc, axis=-1, keepdims=True)
    rstd = lax.rsqrt(var + LN_EPS)
    return xc * rstd, rstd


def _ln_bwd(dy, xhat, rstd, g):
    dxh = dy * g
    m1 = jnp.mean(dxh, axis=-1, keepdims=True)
    m2 = jnp.mean(dxh * xhat, axis=-1, keepdims=True)
    return rstd * (dxh - m1 - xhat * m2)


def _colsum(x):
    return jnp.sum(x, axis=0, keepdims=True)


class _Hosted:
    def __init__(self, inputs, out_shapes, sem_shapes, start, finish):
        self.inputs, self.out_shapes, self.sem_shapes = list(inputs), list(out_shapes), list(sem_shapes)
        self.start, self.finish = start, finish


def _call(body, *, name, grid, in_specs, out_specs, out_shape, scratch_shapes, semantics, operands, hosted=None):
    if hosted is None:
        return pl.pallas_call(
            body, name=name, grid=grid, in_specs=list(in_specs), out_specs=list(out_specs), out_shape=list(out_shape),
            scratch_shapes=list(scratch_shapes), compiler_params=_params(semantics))(*operands)
    n_in, n_out, n_scr = len(in_specs), len(out_specs), len(scratch_shapes)
    h_in, h_out = len(hosted.inputs), len(hosted.out_shapes)

    def full_body(*refs):
        ins, refs = refs[:n_in], refs[n_in:]
        h_ins, refs = refs[:h_in], refs[h_in:]
        outs, refs = refs[:n_out], refs[n_out:]
        h_outs, refs = refs[:h_out], refs[h_out:]
        scr, sems = refs[:n_scr], refs[n_scr:]
        first = functools.reduce(jnp.logical_and, [pl.program_id(d) == 0 for d in range(len(grid))])
        last = functools.reduce(jnp.logical_and, [pl.program_id(d) == grid[d] - 1 for d in range(len(grid))])

        @pl.when(first)
        def _():
            hosted.start(h_ins, h_outs, sems)

        body(*ins, *outs, *scr)

        @pl.when(last)
        def _():
            hosted.finish(h_ins, h_outs, sems)

    return pl.pallas_call(
        full_body, name=name, grid=grid, in_specs=list(in_specs) + [ANY] * h_in,
        out_specs=list(out_specs) + [ANY] * h_out, out_shape=list(out_shape) + hosted.out_shapes,
        scratch_shapes=list(scratch_shapes) + hosted.sem_shapes,
        compiler_params=pltpu.CompilerParams(dimension_semantics=("arbitrary",) * len(grid),
                                             vmem_limit_bytes=VMEM_LIMIT, has_side_effects=True),
    )(*operands, *hosted.inputs)


def _run_hosted(name, hosted):
    h_in, h_out = len(hosted.inputs), len(hosted.out_shapes)

    def body(*refs):
        ins, outs, sems = refs[:h_in], refs[h_in:h_in + h_out], refs[h_in + h_out:]
        hosted.start(ins, outs, sems)
        hosted.finish(ins, outs, sems)

    return pl.pallas_call(
        body, name=name, in_specs=[ANY] * h_in, out_specs=[ANY] * h_out, out_shape=hosted.out_shapes,
        scratch_shapes=hosted.sem_shapes, compiler_params=pltpu.CompilerParams(has_side_effects=True),
    )(*hosted.inputs)


def _mm(name, a, b, out_shape, out_dtype, grid, a_spec, b_spec, o_spec, dims, acc_shape, hosted=None):
    nk = grid[2]
    if nk == 1:
        def body(a_ref, b_ref, o_ref):
            o_ref[...] = _dot(a_ref[...], b_ref[...], dims).astype(o_ref.dtype)
        scratch = []
    else:
        def body(a_ref, b_ref, o_ref, acc_ref):
            k = pl.program_id(2)

            @pl.when(k == 0)
            def _():
                acc_ref[...] = jnp.zeros_like(acc_ref)

            acc_ref[...] += _dot(a_ref[...], b_ref[...], dims)

            @pl.when(k == nk - 1)
            def _():
                o_ref[...] = acc_ref[...].astype(o_ref.dtype)
        scratch = [pltpu.VMEM(acc_shape, F32)]

    outs = _call(body, name=name, grid=grid, in_specs=[a_spec, b_spec], out_specs=[o_spec],
                 out_shape=[jax.ShapeDtypeStruct(out_shape, out_dtype)], scratch_shapes=scratch,
                 semantics=("parallel", "parallel", "arbitrary"), operands=(a, b), hosted=hosted)
    return outs[0] if hosted is None else (outs[0], list(outs[1:]))


def _mm_fused(name, grid, operands, in_specs, out_shape, out_specs, dims, acc_shape, epilogue, lhs=None, scratch=(),
              hosted=None):
    nk = grid[2]
    n_in, n_out = len(in_specs), len(out_specs)

    def body(*refs):
        ins, outs, scr = refs[:n_in], refs[n_in:n_in + n_out], refs[n_in + n_out:]
        acc_ref, k = scr[0], pl.program_id(2)
        a = ins[0][...] if lhs is None else lhs(ins, outs)
        part = _dot(a, ins[1][...], dims)
        if nk == 1:
            acc_ref[...] = part
            epilogue(acc_ref, ins, outs, scr[1:])
            return

        @pl.when(k == 0)
        def _():
            acc_ref[...] = jnp.zeros_like(acc_ref)

        acc_ref[...] += part

        @pl.when(k == nk - 1)
        def _():
            epilogue(acc_ref, ins, outs, scr[1:])

    return _call(body, name=name, grid=grid, in_specs=in_specs, out_specs=out_specs, out_shape=out_shape,
                 scratch_shapes=[pltpu.VMEM(acc_shape, F32)] + list(scratch), semantics=("arbitrary",) * 3,
                 operands=operands, hosted=hosted)


def _row_blocks(rows, block=256):
    block = block if rows % block == 0 else rows
    return [slice(r, r + block) for r in range(0, rows, block)]


def _pick(t, pref):
    return pref if t % pref == 0 else t


def _glu(p):
    return p[:, :CONV_DIM] * _sigmoid(p[:, CONV_DIM:])


def _by_phase(taps):
    phases = {}
    for off, payload in taps:
        phases.setdefault(off % SUBLANES, []).append((off - off % SUBLANES, payload))
    return sorted(phases.items())


def _tap_sum(src_ref, base, taps, rows, lanes):
    acc = None
    for phase, items in _by_phase(taps):
        n = rows if phase == 0 else rows + SUBLANES
        part = None
        for off, (w_ref, k) in items:
            term = w_ref[k:k + 1, lanes] * src_ref[base + off:base + off + n, lanes]
            part = term if part is None else part + term
        if phase:
            part = part[phase:phase + rows, :]
        acc = part if acc is None else acc + part
    return acc


def _tap_products(x, src_ref, base, taps, lanes):
    rows, cols = x.shape
    pad = jnp.zeros((SUBLANES, cols), x.dtype)
    padded = jnp.concatenate([pad, x, pad], axis=0)
    out = []
    for phase, items in _by_phase(taps):
        n = rows if phase == 0 else rows + SUBLANES
        shifted = x if phase == 0 else padded[SUBLANES - phase:SUBLANES - phase + n, :]
        for off, key in items:
            out.append((key, _colsum(shifted * src_ref[base + off:base + off + n, lanes])))
    return out


def _lane_blocks(cols, block=256):
    return [slice(c, c + block) for c in range(0, cols, block)]


def _conv_fwd(proj, w_dw, b_dw, g, b):
    t = proj.shape[0]
    tm = _pick(t, 256)
    nh = tm // CONV_HALO

    def body(p_ref, ph_ref, w_ref, bd_ref, g_ref, b_ref, act_ref, pre_ref, xs_ref):
        i = pl.program_id(0)
        halo = _glu(ph_ref[...])
        xs_ref[0:CONV_HALO, :] = jnp.where(i == 0, 0.0, halo)
        xs_ref[CONV_HALO:CONV_HALO + tm, :] = _glu(p_ref[...])
        taps = [(CONV_HALO - (CONV_K - 1) + k, (w_ref, k)) for k in range(CONV_K)]
        for r in range(tm // ROW_BLOCK):
            rows = slice(r * ROW_BLOCK, (r + 1) * ROW_BLOCK)
            for lanes in _lane_blocks(CONV_DIM):
                pre_ref[rows, lanes] = bd_ref[:, lanes] + _tap_sum(xs_ref, r * ROW_BLOCK, taps, ROW_BLOCK, lanes)
            acc = pre_ref[rows, :]
            xhat, _ = _ln(acc)
            yln = xhat * g_ref[...] + b_ref[...]
            act_ref[rows, :] = (yln * _sigmoid(yln)).astype(BF16)

    full = lambda s: pl.BlockSpec(s, lambda i: (0, 0))
    return pl.pallas_call(
        body, name="conv_fwd", grid=(t // tm,),
        in_specs=[pl.BlockSpec((tm, 2 * CONV_DIM), lambda i: (i, 0)),
                  pl.BlockSpec((CONV_HALO, 2 * CONV_DIM), lambda i: (jnp.maximum(i * nh - 1, 0), 0)),
                  full((CONV_K, CONV_DIM)), full((1, CONV_DIM)), full((1, CONV_DIM)), full((1, CONV_DIM))],
        out_specs=[pl.BlockSpec((tm, CONV_DIM), lambda i: (i, 0)), pl.BlockSpec((tm, CONV_DIM), lambda i: (i, 0))],
        out_shape=[jax.ShapeDtypeStruct((t, CONV_DIM), BF16), jax.ShapeDtypeStruct((t, CONV_DIM), F32)],
        scratch_shapes=[pltpu.VMEM((CONV_HALO + tm, CONV_DIM), F32)],
        compiler_params=_params(("arbitrary",)),
    )(proj, proj, w_dw, b_dw, g, b)


def _d_c_norm_bwd(d_y, w_conv_out, pre, g, b):
    t = pre.shape[0]
    tm = _pick(t, 512)
    d = D_MODEL

    def epilogue(d_c, ins, outs, scr):
        pre_ref, g_ref, b_ref = ins[2:]
        dpre_ref, sums_ref = outs
        i = pl.program_id(0)

        @pl.when(i == 0)
        def _():
            sums_ref[...] = jnp.zeros_like(sums_ref)

        for rows in _row_blocks(tm):
            xhat, rstd = _ln(pre_ref[rows, :])
            yln = xhat * g_ref[...] + b_ref[...]
            sg = _sigmoid(yln)
            dyln = d_c[rows, :] * (sg * (1.0 + yln * (1.0 - sg)))
            dpre = _ln_bwd(dyln, xhat, rstd, g_ref[...])
            dpre_ref[rows, :] = dpre
            sums_ref[0:1, :] += _colsum(dyln * xhat)
            sums_ref[1:2, :] += _colsum(dyln)
            sums_ref[2:3, :] += _colsum(dpre)

    full = lambda s: pl.BlockSpec(s, lambda i, j, k: (0, 0))
    tile = pl.BlockSpec((tm, CONV_DIM), lambda i, j, k: (i, 0))
    return _mm_fused(
        "d_c_norm_bwd", (t // tm, 1, 1), (d_y, w_conv_out, pre, g, b),
        [pl.BlockSpec((None, tm, d), lambda i, j, k: (0, i, 0)), full((CONV_DIM, d)), tile,
         full((1, CONV_DIM)), full((1, CONV_DIM))],
        [jax.ShapeDtypeStruct((t, CONV_DIM), F32), jax.ShapeDtypeStruct((8, CONV_DIM), F32)],
        [tile, full((8, CONV_DIM))], NT, (tm, CONV_DIM), epilogue)


def _conv_bwd_dw(d_pre, proj, w_dw):
    t = d_pre.shape[0]
    tm = _pick(t, 256)
    nt = t // tm
    nh = tm // CONV_HALO
    last_h = t // CONV_HALO - 1

    def body(dp_ref, dph_ref, p_ref, ph_ref, w_ref, dproj_ref, dw_ref, xs_ref, ds_ref):
        i = pl.program_id(0)

        @pl.when(i == 0)
        def _():
            dw_ref[...] = jnp.zeros_like(dw_ref)

        halo = _glu(ph_ref[...])
        xs_ref[0:CONV_HALO, :] = jnp.where(i == 0, 0.0, halo)
        xs_ref[CONV_HALO:CONV_HALO + tm, :] = _glu(p_ref[...])
        ds_ref[0:tm, :] = dp_ref[...]
        ds_ref[tm:tm + CONV_HALO, :] = jnp.where(i == nt - 1, 0.0, dph_ref[...])
        back_taps = [(CONV_K - 1 - k, (w_ref, k)) for k in range(CONV_K)]
        grad_taps = [(CONV_HALO - (CONV_K - 1) + k, k) for k in range(CONV_K)]
        for r in range(tm // ROW_BLOCK):
            base = r * ROW_BLOCK
            rows = slice(base, base + ROW_BLOCK)
            for lanes in _lane_blocks(CONV_DIM):
                gate_lanes = slice(CONV_DIM + lanes.start, CONV_DIM + lanes.stop)
                acc = _tap_sum(ds_ref, base, back_taps, ROW_BLOCK, lanes)
                for k, total in _tap_products(ds_ref[rows, lanes], xs_ref, base, grad_taps, lanes):
                    dw_ref[k:k + 1, lanes] += total
                cval = p_ref[rows, lanes]
                sg = _sigmoid(p_ref[rows, gate_lanes])
                dproj_ref[rows, lanes] = (acc * sg).astype(BF16)
                dproj_ref[rows, gate_lanes] = (acc * cval * sg * (1.0 - sg)).astype(BF16)

    full = lambda s: pl.BlockSpec(s, lambda i: (0, 0))
    return pl.pallas_call(
        body, name="conv_bwd_dw", grid=(nt,),
        in_specs=[pl.BlockSpec((tm, CONV_DIM), lambda i: (i, 0)),
                  pl.BlockSpec((CONV_HALO, CONV_DIM), lambda i: (jnp.minimum((i + 1) * nh, last_h), 0)),
                  pl.BlockSpec((tm, 2 * CONV_DIM), lambda i: (i, 0)),
                  pl.BlockSpec((CONV_HALO, 2 * CONV_DIM), lambda i: (jnp.maximum(i * nh - 1, 0), 0)),
                  full((CONV_K, CONV_DIM))],
        out_specs=[pl.BlockSpec((tm, 2 * CONV_DIM), lambda i: (i, 0)), full((CONV_HALO, CONV_DIM))],
        out_shape=[jax.ShapeDtypeStruct((t, 2 * CONV_DIM), BF16), jax.ShapeDtypeStruct((CONV_HALO, CONV_DIM), F32)],
        scratch_shapes=[pltpu.VMEM((CONV_HALO + tm, CONV_DIM), F32), pltpu.VMEM((tm + CONV_HALO, CONV_DIM), F32)],
        compiler_params=_params(("arbitrary",)),
    )(d_pre, d_pre, proj, proj, w_dw)


def _lower_bound(logit_ref):
    l0 = logit_ref[0:1, :]
    l1 = logit_ref[1:2, :]
    m = jnp.maximum(l0, l1)
    e0 = jnp.exp(l0 - m)
    e1 = jnp.exp(l1 - m)
    return e0 / (e0 + e1)


def _tri(lower):
    r = lax.broadcasted_iota(jnp.int32, (CHUNK, CHUNK), 0)
    c = lax.broadcasted_iota(jnp.int32, (CHUNK, CHUNK), 1)
    return (c <= r) if lower else (c >= r)


def _hgrn_gates(fz, lb):
    s = _sigmoid(fz)
    sn = _sigmoid(-fz)
    f = lb + (1.0 - lb) * s
    return s, sn, f


def _block_tri(rows, lower=True):
    r = lax.broadcasted_iota(jnp.int32, (rows, rows), 0)
    c = lax.broadcasted_iota(jnp.int32, (rows, rows), 1)
    tri = (c <= r) if lower else (c >= r)
    return (tri & (r // CHUNK == c // CHUNK)).astype(BF16)


def _tri_rows(tm):
    return min(tm, MXU_DEPTH)


def _tri_matmul(tri_ref, x):
    hi = x.astype(BF16)
    lo = (x - hi.astype(F32)).astype(BF16)
    tri = tri_ref[...]
    return (lax.dot_general(tri, hi, NN, preferred_element_type=F32)
            + lax.dot_general(tri, lo, NN, preferred_element_type=F32))


def _groups(tm):
    g = _tri_rows(tm)
    return [slice(i * g, (i + 1) * g) for i in range(tm // g)]


def _hgrn_fwd(proj, logits, norm_g):
    t = proj.shape[0]
    tm = CHUNK * CHUNKS_PER_BLOCK if t % (CHUNK * CHUNKS_PER_BLOCK) == 0 else CHUNK
    cpb = tm // CHUNK
    nt = t // tm
    half = CHUNK // 2

    def body(qz_ref, fz_ref, iv_ref, gz_ref, lg_ref, ng_ref, tri_ref, o_ref, og_ref, st_ref,
             state_ref, qe_ref, ke_ref, qb_ref, kl_ref, v_ref, upd_ref, decay_ref, a_ref, q_ref, kk_ref, b_ref):
        j = pl.program_id(1)

        @pl.when(j == 0)
        def _():
            state_ref[...] = jnp.zeros_like(state_ref)

        lb = _lower_bound(lg_ref)
        chunks = [slice(c * CHUNK, (c + 1) * CHUNK) for c in range(cpb)]
        for rows in chunks:
            qz = qz_ref[rows, :]
            q_ref[rows, :] = qz * _sigmoid(qz)
            _, sn, f = _hgrn_gates(fz_ref[rows, :], lb)
            kk_ref[rows, :] = (1.0 - lb) * sn
            b_ref[rows, :] = jnp.log(f)
            v_ref[rows, :] = iv_ref[rows, :].astype(BF16)
        for rows in _groups(tm):
            b_ref[rows, :] = _tri_matmul(tri_ref, b_ref[rows, :])
        for c, rows in enumerate(chunks):
            b = b_ref[rows, :]
            bref = b[half - 1:half, :]
            blast = b[CHUNK - 1:CHUNK, :]
            q = q_ref[rows, :]
            kk = kk_ref[rows, :]
            qb_ref[rows, :] = (q * jnp.exp(b)).astype(BF16)
            qe_ref[rows, :] = (q * jnp.exp(b - bref)).astype(BF16)
            ke_ref[rows, :] = (kk * jnp.exp(bref - b)).astype(BF16)
            kl_ref[rows, :] = (kk * jnp.exp(blast - b)).astype(BF16)
            decay_ref[c:c + 1, :] = jnp.exp(blast)
        causal = _tri(True)
        for c, rows in enumerate(chunks):
            upd_ref[c] = _dot(v_ref[rows, :], kl_ref[rows, :], TN)
            a_ref[c] = jnp.where(causal, _dot(qe_ref[rows, :], ke_ref[rows, :], NT), 0.0).astype(BF16)
        state = state_ref[...]
        for c in range(cpb):
            st_ref[c] = state.astype(BF16)
            state = state * decay_ref[c:c + 1, :] + upd_ref[c]
        state_ref[...] = state
        for c, rows in enumerate(chunks):
            o_ref[rows, :] = _dot(a_ref[c], v_ref[rows, :], NN) + _dot(qb_ref[rows, :], st_ref[c], NT)
        for rows in chunks:
            o = o_ref[rows, :]
            r = lax.rsqrt(jnp.mean(o * o, axis=-1, keepdims=True) + RMS_EPS)
            gz = gz_ref[rows, :]
            og_ref[rows, :] = ((o * r * ng_ref[...]) * (gz * _sigmoid(gz))).astype(BF16)

    col = lambda base: pl.BlockSpec((tm, HEAD_DIM), lambda h, j: (j, base + h))
    tile_bf = pltpu.VMEM((tm, HEAD_DIM), BF16)
    tile_f32 = pltpu.VMEM((tm, HEAD_DIM), F32)
    return pl.pallas_call(
        body, name="hgrn_fwd", grid=(HEADS, nt),
        in_specs=[col(8), col(16), col(24), col(32),
                  pl.BlockSpec((2, HEAD_DIM), lambda h, j: (0, h)), pl.BlockSpec((1, HEAD_DIM), lambda h, j: (0, h)),
                  pl.BlockSpec((_tri_rows(tm), _tri_rows(tm)), lambda h, j: (0, 0))],
        out_specs=[col(0), col(0), pl.BlockSpec((None, cpb, HEAD_DIM, HEAD_DIM), lambda h, j: (h, j, 0, 0))],
        out_shape=[jax.ShapeDtypeStruct((t, HGRN_DIM), F32), jax.ShapeDtypeStruct((t, HGRN_DIM), BF16),
                   jax.ShapeDtypeStruct((HEADS, t // CHUNK, HEAD_DIM, HEAD_DIM), BF16)],
        scratch_shapes=[pltpu.VMEM((HEAD_DIM, HEAD_DIM), F32), tile_bf, tile_bf, tile_bf, tile_bf, tile_bf,
                        pltpu.VMEM((cpb, HEAD_DIM, HEAD_DIM), F32), pltpu.VMEM((max(cpb, 8), HEAD_DIM), F32),
                        pltpu.VMEM((cpb, CHUNK, CHUNK), BF16), tile_f32, tile_f32, tile_f32],
        compiler_params=_params(("parallel", "arbitrary")),
    )(proj, proj, proj, proj, logits, norm_g, _block_tri(_tri_rows(tm)))


def _hgrn_bwd(d_og, o, proj, states, logits, norm_g, hosted=None):
    t = proj.shape[0]
    tm = CHUNK * CHUNKS_PER_BLOCK if t % (CHUNK * CHUNKS_PER_BLOCK) == 0 else CHUNK
    cpb = tm // CHUNK
    nt = t // tm
    half = CHUNK // 2

    def body(dog_ref, o_ref, qz_ref, fz_ref, iv_ref, gz_ref, st_ref, lg_ref, ng_ref, tril_ref, triu_ref,
             dqz_ref, dfz_ref, div_ref, dgz_ref, sums_ref,
             dstate_ref, qe_ref, ke_ref, qb_ref, kl_ref, v_ref, do_ref, upd_ref, dst_ref, a_ref, da_ref,
             decay_ref, through_ref, q_ref, kk_ref, b_ref, dsilu_ref, gs_ref, gf_ref, sn_ref,
             eb_ref, ebr_ref, ekr_ref, ebl_ref, rev_ref, pre_ref, dk_ref):
        j = pl.program_id(1)

        @pl.when(j == 0)
        def _():
            dstate_ref[...] = jnp.zeros_like(dstate_ref)
            sums_ref[...] = jnp.zeros_like(sums_ref)

        lb = _lower_bound(lg_ref)
        ng = ng_ref[...]
        chunks = [slice(c * CHUNK, (c + 1) * CHUNK) for c in range(cpb)]
        for rows in chunks:
            qz = qz_ref[rows, :]
            sq = _sigmoid(qz)
            q_ref[rows, :] = qz * sq
            dsilu_ref[rows, :] = sq * (1.0 + qz * (1.0 - sq))
            s, sn, f = _hgrn_gates(fz_ref[rows, :], lb)
            kk_ref[rows, :] = (1.0 - lb) * sn
            b_ref[rows, :] = jnp.log(f)
            sn_ref[rows, :] = sn
            gf_ref[rows, :] = sn / f
            gs_ref[rows, :] = (1.0 - lb) * s
            v_ref[rows, :] = iv_ref[rows, :].astype(BF16)
            ov = o_ref[rows, :]
            r = lax.rsqrt(jnp.mean(ov * ov, axis=-1, keepdims=True) + RMS_EPS)
            on = ov * r
            gz = gz_ref[rows, :]
            sg = _sigmoid(gz)
            dog = dog_ref[rows, :]
            dgz_ref[rows, :] = (dog * (on * ng) * (sg * (1.0 + gz * (1.0 - sg)))).astype(BF16)
            d_ong = dog * (gz * sg)
            sums_ref[0:1, :] += _colsum(d_ong * on)
            d_on = d_ong * ng
            do_ref[rows, :] = (r * (d_on - on * jnp.mean(d_on * on, axis=-1, keepdims=True))).astype(BF16)
        for rows in _groups(tm):
            b_ref[rows, :] = _tri_matmul(tril_ref, b_ref[rows, :])
        for c, rows in enumerate(chunks):
            b = b_ref[rows, :]
            bref = b[half - 1:half, :]
            blast = b[CHUNK - 1:CHUNK, :]
            q = q_ref[rows, :]
            kk = kk_ref[rows, :]
            eb = jnp.exp(b)
            ebr = jnp.exp(b - bref)
            ekr = jnp.exp(bref - b)
            ebl = jnp.exp(blast - b)
            eb_ref[rows, :] = eb
            ebr_ref[rows, :] = ebr
            ekr_ref[rows, :] = ekr
            ebl_ref[rows, :] = ebl
            qb_ref[rows, :] = (q * eb).astype(BF16)
            qe_ref[rows, :] = (q * ebr).astype(BF16)
            ke_ref[rows, :] = (kk * ekr).astype(BF16)
            kl_ref[rows, :] = (kk * ebl).astype(BF16)
            decay_ref[c:c + 1, :] = jnp.exp(blast)
        causal = _tri(True)
        for c, rows in enumerate(chunks):
            upd_ref[c] = _dot(do_ref[rows, :], qb_ref[rows, :], TN)
            a_ref[c] = jnp.where(causal, _dot(qe_ref[rows, :], ke_ref[rows, :], NT), 0.0).astype(BF16)
            da_ref[c] = jnp.where(causal, _dot(do_ref[rows, :], v_ref[rows, :], NT), 0.0).astype(BF16)
        dstate = dstate_ref[...]
        for c in reversed(range(cpb)):
            dst_ref[c] = dstate.astype(BF16)
            decay = decay_ref[c:c + 1, :]
            through_ref[c:c + 1, :] = decay * _colsum(dstate * st_ref[c].astype(F32))
            dstate = dstate * decay + upd_ref[c]
        dstate_ref[...] = dstate
        for c, rows in enumerate(chunks):
            qe, ke, v, do = qe_ref[rows, :], ke_ref[rows, :], v_ref[rows, :], do_ref[rows, :]
            div_ref[rows, :] = (_dot(a_ref[c], do, TN) + _dot(kl_ref[rows, :], dst_ref[c], NT)).astype(BF16)
            dqe = _dot(da_ref[c], ke, NN)
            dke = _dot(da_ref[c], qe, TN)
            dq_inter = _dot(do, st_ref[c], NN) * eb_ref[rows, :]
            dk_inter = _dot(v, dst_ref[c], NN) * ebl_ref[rows, :]
            dk_ref[rows, :] = dke * ekr_ref[rows, :] + dk_inter
            dqz_ref[rows, :] = ((dqe * ebr_ref[rows, :] + dq_inter) * dsilu_ref[rows, :]).astype(BF16)
            rev_ref[rows, :] = (qe.astype(F32) * dqe - ke.astype(F32) * dke) + q_ref[rows, :] * dq_inter
            pre_ref[rows, :] = kk_ref[rows, :] * dk_inter
        for rows in _groups(tm):
            pre = pre_ref[rows, :]
            rev_ref[rows, :] = _tri_matmul(triu_ref, rev_ref[rows, :]) + (_tri_matmul(tril_ref, pre) - pre)
        for c, rows in enumerate(chunks):
            dlf = rev_ref[rows, :] + through_ref[c:c + 1, :]
            common = gf_ref[rows, :] * dlf - sn_ref[rows, :] * dk_ref[rows, :]
            dfz_ref[rows, :] = (gs_ref[rows, :] * common).astype(BF16)
            sums_ref[1:2, :] += _colsum(common)

        @pl.when(j == nt - 1)
        def _():
            sums_ref[1:2, :] = sums_ref[1:2, :] * lb * (1.0 - lb)

    rev = lambda base: pl.BlockSpec((tm, HEAD_DIM), lambda h, j: (nt - 1 - j, base + h))
    vec = lambda n: pl.BlockSpec((n, HEAD_DIM), lambda h, j: (0, h))
    const = pl.BlockSpec((_tri_rows(tm), _tri_rows(tm)), lambda h, j: (0, 0))
    bf = jax.ShapeDtypeStruct((t, HGRN_DIM), BF16)
    tile_bf = pltpu.VMEM((tm, HEAD_DIM), BF16)
    tile_f32 = pltpu.VMEM((tm, HEAD_DIM), F32)
    square = lambda dtype: pltpu.VMEM((cpb, HEAD_DIM, HEAD_DIM), dtype)
    rows8 = pltpu.VMEM((max(cpb, 8), HEAD_DIM), F32)
    return _call(
        body, name="hgrn_bwd", grid=(HEADS, nt),
        in_specs=[rev(0), rev(0), rev(8), rev(16), rev(24), rev(32),
                  pl.BlockSpec((None, cpb, HEAD_DIM, HEAD_DIM), lambda h, j: (h, nt - 1 - j, 0, 0)),
                  vec(2), vec(1), const, const],
        out_specs=[rev(0), rev(0), rev(0), rev(0), vec(8)],
        out_shape=[bf, bf, bf, bf, jax.ShapeDtypeStruct((8, HGRN_DIM), F32)],
        scratch_shapes=[pltpu.VMEM((HEAD_DIM, HEAD_DIM), F32)] + [tile_bf] * 6 + [square(F32), square(BF16)]
        + [pltpu.VMEM((cpb, CHUNK, CHUNK), BF16)] * 2 + [rows8, rows8] + [tile_f32] * 14,
        semantics=("parallel", "arbitrary"),
        operands=(d_og, o, proj, proj, proj, proj, states, logits, norm_g, _block_tri(_tri_rows(tm)), _block_tri(_tri_rows(tm), lower=False)),
        hosted=hosted)


def _mix_ln1(proj, y_conv, y_hgrn, w_out, x, g, b):
    t = x.shape[0]
    tm = _pick(t, 512)
    d = D_MODEL

    def lhs(ins, outs):
        for rows in _row_blocks(tm):
            outs[0][rows, :] = (_sigmoid(ins[0][rows, :]) * ins[3][rows, :]
                                + _sigmoid(ins[2][rows, :]) * ins[4][rows, :]).astype(BF16)
        return outs[0][...]

    def epilogue(acc, ins, outs, scr):
        for rows in _row_blocks(tm):
            r = ALPHA * ins[5][rows, :] + acc[rows, :]
            outs[1][rows, :] = r
            xhat, _ = _ln(r)
            outs[2][rows, :] = (xhat * ins[6][...] + ins[7][...]).astype(BF16)

    tile = pl.BlockSpec((tm, d), lambda i, j, k: (i, 0))
    vec = pl.BlockSpec((1, d), lambda i, j, k: (0, 0))
    return _mm_fused(
        "mix_ln1", (t // tm, 1, 1), (proj, w_out, proj, y_conv, y_hgrn, x, g, b),
        [pl.BlockSpec((tm, d), lambda i, j, k: (i, 5)), pl.BlockSpec((d, d), lambda i, j, k: (0, 0)),
         pl.BlockSpec((tm, d), lambda i, j, k: (i, 6)), tile, tile, tile, vec, vec],
        [jax.ShapeDtypeStruct((t, d), BF16), jax.ShapeDtypeStruct((t, d), F32), jax.ShapeDtypeStruct((t, d), BF16)],
        [tile, tile, tile], NN, (tm, d), epilogue, lhs=lhs)


def _d_mixed_merge_bwd(d_r1b, w_out, proj, y_conv, y_hgrn):
    t = proj.shape[0]
    tm = _pick(t, 512)
    d = D_MODEL

    def epilogue(d_mixed, ins, outs, scr):
        dy_ref, dmz_ref = outs
        for rows in _row_blocks(tm):
            dm = d_mixed[rows, :]
            for br in range(2):
                sg = _sigmoid(ins[2 + br][rows, :])
                dy_ref[br, rows, :] = (sg * dm).astype(BF16)
                dmz_ref[rows, br * d:(br + 1) * d] = (dm * ins[4 + br][rows, :] * sg * (1.0 - sg)).astype(BF16)

    tile = pl.BlockSpec((tm, d), lambda i, j, k: (i, 0))
    return _mm_fused(
        "d_mixed_merge_bwd", (t // tm, 1, 1), (d_r1b, w_out, proj, proj, y_conv, y_hgrn),
        [tile, pl.BlockSpec((d, d), lambda i, j, k: (0, 0)), pl.BlockSpec((tm, d), lambda i, j, k: (i, 5)),
         pl.BlockSpec((tm, d), lambda i, j, k: (i, 6)), tile, tile],
        [jax.ShapeDtypeStruct((2, t, d), BF16), jax.ShapeDtypeStruct((t, 2 * d), BF16)],
        [pl.BlockSpec((2, tm, d), lambda i, j, k: (0, i, 0)), pl.BlockSpec((tm, 2 * d), lambda i, j, k: (i, 0))],
        NT, (tm, d), epilogue)


def _ffn_out_ln2(act, w_ffn_out, r1, target, g1, b1, g2, b2):
    t = r1.shape[0]
    tm = _pick(t, 1024)
    nt = t // tm
    d = D_MODEL

    def epilogue(y_ffn, ins, outs, scr):
        r1_ref, tg_ref, g1_ref, b1_ref, g2_ref, b2_ref = ins[2:]
        dr_ref, drb_ref, sums_ref = outs
        (sq_ref,) = scr
        i = pl.program_id(0)

        @pl.when(i == 0)
        def _():
            sums_ref[...] = jnp.zeros_like(sums_ref)
            sq_ref[...] = jnp.zeros_like(sq_ref)

        for rows in _row_blocks(tm):
            xh1, _ = _ln(r1_ref[rows, :])
            x1 = xh1 * g1_ref[...] + b1_ref[...]
            xh2, rstd2 = _ln(ALPHA * x1 + y_ffn[rows, :])
            diff = xh2 * g2_ref[...] + b2_ref[...] - tg_ref[rows, :]
            dy = diff * (1.0 / D_MODEL)
            dr = _ln_bwd(dy, xh2, rstd2, g2_ref[...])
            dr_ref[rows, :] = dr
            drb_ref[rows, :] = dr.astype(BF16)
            sums_ref[0:1, :] += _colsum(dy * xh2)
            sums_ref[1:2, :] += _colsum(dy)
            sq_ref[...] += _colsum(diff * diff)

        @pl.when(i == nt - 1)
        def _():
            total = jnp.sum(sq_ref[...], axis=-1, keepdims=True) * (0.5 / D_MODEL)
            sums_ref[2:3, :] = jnp.broadcast_to(total, (1, D_MODEL))

    tile = pl.BlockSpec((tm, d), lambda i, j, k: (i, 0))
    vec = pl.BlockSpec((1, d), lambda i, j, k: (0, 0))
    return _mm_fused(
        "ffn_out_ln2", (nt, 1, 4), (act, w_ffn_out, r1, target, g1, b1, g2, b2),
        [pl.BlockSpec((None, tm, FF_SHARD), lambda i, j, k: (k, i, 0)),
         pl.BlockSpec((None, FF_SHARD, d), lambda i, j, k: (k, 0, 0)), tile, tile, vec, vec, vec, vec],
        [jax.ShapeDtypeStruct((t, d), F32), jax.ShapeDtypeStruct((t, d), BF16), jax.ShapeDtypeStruct((8, d), F32)],
        [tile, tile, pl.BlockSpec((8, d), lambda i, j, k: (0, 0))], NN, (tm, d), epilogue,
        scratch=[pltpu.VMEM((1, d), F32)])


def _d_x1_ln1_bwd(d_z, w_ffn_in, d_r2, r1, g1):
    t = r1.shape[0]
    tm = _pick(t, 1024)
    d = D_MODEL

    def epilogue(dx_ffn, ins, outs, scr):
        dr2_ref, r1_ref, g_ref = ins[2:]
        dr1_ref, dr1b_ref, sums_ref = outs
        i = pl.program_id(0)

        @pl.when(i == 0)
        def _():
            sums_ref[...] = jnp.zeros_like(sums_ref)

        for rows in _row_blocks(tm):
            xhat, rstd = _ln(r1_ref[rows, :])
            dx1 = ALPHA * dr2_ref[rows, :] + dx_ffn[rows, :]
            dr1 = _ln_bwd(dx1, xhat, rstd, g_ref[...])
            dr1_ref[rows, :] = dr1
            dr1b_ref[rows, :] = dr1.astype(BF16)
            sums_ref[0:1, :] += _colsum(dx1 * xhat)
            sums_ref[1:2, :] += _colsum(dx1)

    tile = pl.BlockSpec((tm, d), lambda i, j, k: (i, 0))
    return _mm_fused(
        "d_x1_ln1_bwd", (t // tm, 1, N_DEV), (d_z, w_ffn_in, d_r2, r1, g1),
        [pl.BlockSpec((None, tm, FF_SHARD), lambda i, j, k: (k, i, 0)),
         pl.BlockSpec((None, d, FF_SHARD), lambda i, j, k: (k, 0, 0)), tile, tile,
         pl.BlockSpec((1, d), lambda i, j, k: (0, 0))],
        [jax.ShapeDtypeStruct((t, d), F32), jax.ShapeDtypeStruct((t, d), BF16), jax.ShapeDtypeStruct((8, d), F32)],
        [tile, tile, pl.BlockSpec((8, d), lambda i, j, k: (0, 0))], NT, (tm, d), epilogue)


def _cast_bf16(x):
    t = x.shape[0]
    tm = _pick(t, 512)

    def body(x_ref, o_ref):
        o_ref[...] = x_ref[...].astype(BF16)

    tile = pl.BlockSpec((tm, D_MODEL), lambda i: (i, 0))
    return pl.pallas_call(
        body, name="cast_x", grid=(t // tm,), in_specs=[tile], out_specs=tile,
        out_shape=jax.ShapeDtypeStruct((t, D_MODEL), BF16), compiler_params=_params(("parallel",)),
    )(x)


def _relayout(name, a, in_block, in_map, out_block, out_map, out_shape):
    def body(a_ref, o_ref):
        o_ref[...] = a_ref[...].astype(o_ref.dtype)

    return pl.pallas_call(
        body, name=name, grid=(N_DEV,), in_specs=[pl.BlockSpec(in_block, in_map)],
        out_specs=pl.BlockSpec(out_block, out_map), out_shape=out_shape, compiler_params=_params(("parallel",)),
    )(a)


_GELU_C = math.sqrt(2.0 / math.pi)


_GELU_CUBIC = 0.044715


def _gelu_parts(u):
    u2 = u * u
    th = jnp.tanh(u * (_GELU_C + (_GELU_C * _GELU_CUBIC) * u2))
    hu = 0.5 * u
    return th, hu + hu * th, u2, hu


BF16_ROWS = 16


def _ffn_act_fwd(z, w_dw, b_dw):
    t = z.shape[2]
    tm = _pick(t, 256)
    nh = tm // FFN_HALO

    def body(z_ref, zh_ref, w_ref, b_ref, act_ref, gd_ref, us_ref):
        i = pl.program_id(1)
        us_ref[0:FFN_HALO, :] = jnp.where(i == 0, 0.0, zh_ref[...])
        us_ref[FFN_HALO:FFN_HALO + tm, :] = z_ref[0]
        for r in range(tm // ROW_BLOCK):
            base = r * ROW_BLOCK
            rows = slice(base, base + ROW_BLOCK)
            uc = jnp.broadcast_to(b_ref[...], (ROW_BLOCK, FF_SHARD))
            for k in range(FFN_K):
                off = base + FFN_HALO - (FFN_K - 1) + k
                uc = uc + w_ref[k:k + 1, :] * us_ref[off:off + ROW_BLOCK, :]
            th, gelu, u2, hu = _gelu_parts(uc)
            dgelu = (0.5 + 0.5 * th) + (hu - hu * th * th) * (_GELU_C + (3.0 * _GELU_C * _GELU_CUBIC) * u2)
            act_ref[rows, :] = (gelu * z_ref[1, rows, :]).astype(BF16)
            gd_ref[0, rows, :] = gelu.astype(BF16)
            gd_ref[1, rows, :] = dgelu.astype(BF16)

    return pl.pallas_call(
        body, name="ffn_act_fwd", grid=(4, t // tm),
        in_specs=[pl.BlockSpec((2, None, tm, FF_SHARD), lambda j, i: (0, j, i, 0)),
                  pl.BlockSpec((None, None, FFN_HALO, FF_SHARD), lambda j, i: (0, j, jnp.maximum(i * nh - 1, 0), 0)),
                  pl.BlockSpec((None, FFN_K, FF_SHARD), lambda j, i: (j, 0, 0)),
                  pl.BlockSpec((None, 1, FF_SHARD), lambda j, i: (j, 0, 0))],
        out_specs=[pl.BlockSpec((None, tm, FF_SHARD), lambda j, i: (j, i, 0)),
                   pl.BlockSpec((2, None, tm, FF_SHARD), lambda j, i: (0, j, i, 0))],
        out_shape=[jax.ShapeDtypeStruct((4, t, FF_SHARD), BF16), jax.ShapeDtypeStruct((2, 4, t, FF_SHARD), BF16)],
        scratch_shapes=[pltpu.VMEM((FFN_HALO + tm, FF_SHARD), F32)],
        compiler_params=_params(("parallel", "arbitrary")),
    )(z, z, w_dw, b_dw)


def _ffn_act_bwd(d_act, z, gd, w_dw):
    t = z.shape[2]
    tm = _pick(t, 256)
    nt = t // tm
    nh = tm // FFN_HALO
    last_h = t // FFN_HALO - 1
    pad = FFN_HALO - (FFN_K - 1)

    def body(da_ref, dah_ref, z_ref, zp_ref, gn_ref, gd_ref, gdn_ref, w_ref, dz_ref, sums_ref, us_ref, ds_ref):
        i = pl.program_id(1)

        @pl.when(i == 0)
        def _():
            sums_ref[...] = jnp.zeros_like(sums_ref)

        us_ref[0:FFN_HALO, :] = jnp.where(i == 0, 0.0, zp_ref[...])
        us_ref[FFN_HALO:FFN_HALO + tm, :] = z_ref[0]
        for r in range(tm // ROW_BLOCK):
            base = r * ROW_BLOCK
            rows = slice(base, base + ROW_BLOCK)
            da = da_ref[rows, :]
            dz_ref[1, rows, :] = (da * gd_ref[0, rows, :].astype(F32)).astype(BF16)
            duc = da * z_ref[1, rows, :] * gd_ref[1, rows, :].astype(F32)
            ds_ref[rows, :] = duc
            for k in range(FFN_K):
                sums_ref[k:k + 1, :] += _colsum(duc * us_ref[base + pad + k:base + pad + k + ROW_BLOCK, :])
            sums_ref[FFN_K:FFN_K + 1, :] += _colsum(duc)
        duc_next = dah_ref[...] * gn_ref[...] * gdn_ref[0:FFN_HALO, :].astype(F32)
        ds_ref[tm:tm + FFN_HALO, :] = jnp.where(i == nt - 1, 0.0, duc_next)
        for r in range(tm // ROW_BLOCK):
            base = r * ROW_BLOCK
            du = jnp.zeros((ROW_BLOCK, FF_SHARD), F32)
            for k in range(FFN_K):
                off = base + FFN_K - 1 - k
                du = du + w_ref[k:k + 1, :] * ds_ref[off:off + ROW_BLOCK, :]
            dz_ref[0, base:base + ROW_BLOCK, :] = du.astype(BF16)

    nxt = lambda i: jnp.minimum((i + 1) * nh, last_h)
    nxt_bf = lambda i: jnp.minimum((i + 1) * (tm // BF16_ROWS), t // BF16_ROWS - 1)
    return pl.pallas_call(
        body, name="ffn_act_bwd", grid=(4, nt),
        in_specs=[pl.BlockSpec((None, tm, FF_SHARD), lambda j, i: (j, i, 0)),
                  pl.BlockSpec((None, FFN_HALO, FF_SHARD), lambda j, i: (j, nxt(i), 0)),
                  pl.BlockSpec((2, None, tm, FF_SHARD), lambda j, i: (0, j, i, 0)),
                  pl.BlockSpec((None, None, FFN_HALO, FF_SHARD), lambda j, i: (0, j, jnp.maximum(i * nh - 1, 0), 0)),
                  pl.BlockSpec((None, None, FFN_HALO, FF_SHARD), lambda j, i: (1, j, nxt(i), 0)),
                  pl.BlockSpec((2, None, tm, FF_SHARD), lambda j, i: (0, j, i, 0)),
                  pl.BlockSpec((None, None, BF16_ROWS, FF_SHARD), lambda j, i: (1, j, nxt_bf(i), 0)),
                  pl.BlockSpec((None, FFN_K, FF_SHARD), lambda j, i: (j, 0, 0))],
        out_specs=[pl.BlockSpec((2, None, tm, FF_SHARD), lambda j, i: (0, j, i, 0)),
                   pl.BlockSpec((None, 8, FF_SHARD), lambda j, i: (j, 0, 0))],
        out_shape=[jax.ShapeDtypeStruct((2, 4, t, FF_SHARD), BF16), jax.ShapeDtypeStruct((4, 8, FF_SHARD), F32)],
        scratch_shapes=[pltpu.VMEM((FFN_HALO + tm, FF_SHARD), F32), pltpu.VMEM((tm + FFN_HALO, FF_SHARD), F32)],
        compiler_params=_params(("parallel", "arbitrary")),
    )(d_act, d_act, z, z, z, gd, gd, w_dw)


def _local_step(x, target, w_in, rest, small, scatter=None):
    t = x.shape[0]
    tm = _pick(t, 2048)
    tk = _pick(t, 2048)
    nm = t // tm
    nk = t // tk
    d = D_MODEL

    xb = _cast_bf16(x)
    proj = _mm("proj", xb, w_in, (t, IN_COLS), F32, (nm, N_DEV, 1),
               pl.BlockSpec((tm, d), lambda i, j, k: (i, 0)),
               pl.BlockSpec((None, d, IN_SHARD), lambda i, j, k: (j, 0, 0)),
               pl.BlockSpec((tm, IN_SHARD), lambda i, j, k: (i, j)), NN, (tm, IN_SHARD),
               hosted=rest if isinstance(rest, _Hosted) else None)
    if isinstance(rest, _Hosted):
        proj, rest = proj
    w_conv_out8, w_hgrn_out8, w_out8, w_ffn_in, w_ffn_out8, conv_dw8, ffn_dw8 = rest
    w_conv_out = _relayout("w_conv_out_natural", w_conv_out8, (None, CONV_DIM, 128), lambda j: (j, 0, 0),
                           (CONV_DIM, 128), lambda j: (0, j), jax.ShapeDtypeStruct((CONV_DIM, d), BF16))
    w_hgrn_out = w_hgrn_out8.reshape(d, d)
    w_out = w_out8.reshape(d, d)
    w_ffn_out = w_ffn_out8.reshape(4, FF_SHARD, d)
    conv_dw = jnp.transpose(conv_dw8[:, :CONV_K, :CONV_DIM // N_DEV], (1, 0, 2)).reshape(CONV_K, CONV_DIM)
    ffn_dw = jnp.transpose(ffn_dw8[:, :FFN_K, :D_FF // N_DEV], (1, 0, 2)).reshape(FFN_K, 4, FF_SHARD)
    small = dict(small, w_conv_dw=conv_dw, w_ffn_dw=jnp.transpose(ffn_dw, (1, 0, 2)),
                 b_ffn_dw=small["b_ffn_dw"].reshape(4, 1, FF_SHARD))

    c_act, conv_pre = _conv_fwd(proj, small["w_conv_dw"], small["b_conv_dw"], small["conv_ln_g"], small["conv_ln_b"])
    y_conv = _mm("y_conv", c_act, w_conv_out, (t, d), F32, (nm, 1, 1),
                 pl.BlockSpec((tm, CONV_DIM), lambda i, j, k: (i, 0)),
                 pl.BlockSpec((CONV_DIM, d), lambda i, j, k: (0, 0)),
                 pl.BlockSpec((tm, d), lambda i, j, k: (i, 0)), NN, (tm, d))
    o, og, states = _hgrn_fwd(proj, small["hgrn_lb_logits"], small["hgrn_norm_g"])
    sq_w = pl.BlockSpec((d, d), lambda i, j, k: (0, 0))
    row_tile = pl.BlockSpec((tm, d), lambda i, j, k: (i, 0))
    y_hgrn = _mm("y_hgrn", og, w_hgrn_out, (t, d), F32, (nm, 1, 1), row_tile, sq_w, row_tile, NN, (tm, d))
    mixed, r1, x1b = _mix_ln1(proj, y_conv, y_hgrn, w_out, x, small["ln1_g"], small["ln1_b"])
    z = _mm("ffn_in", x1b, w_ffn_in, (N_DEV, t, FF_SHARD), F32, (nm, N_DEV, 1), row_tile,
            pl.BlockSpec((None, d, FF_SHARD), lambda i, j, k: (j, 0, 0)),
            pl.BlockSpec((None, tm, FF_SHARD), lambda i, j, k: (j, i, 0)), NN, (tm, FF_SHARD))
    z = z.reshape(2, 4, t, FF_SHARD)
    act, gelu_and_slope = _ffn_act_fwd(z, small["w_ffn_dw"], small["b_ffn_dw"])

    d_r2, d_r2b, sums_ln2 = _ffn_out_ln2(act, w_ffn_out, r1, target, small["ln1_g"], small["ln1_b"],
                                         small["ln2_g"], small["ln2_b"])
    d_act = _mm("d_act", d_r2b, w_ffn_out, (4, t, FF_SHARD), F32, (nm, 4, 1), row_tile,
                pl.BlockSpec((None, FF_SHARD, d), lambda i, j, k: (j, 0, 0)),
                pl.BlockSpec((None, tm, FF_SHARD), lambda i, j, k: (j, i, 0)), NT, (tm, FF_SHARD))
    g_w_ffn_out = _mm("g_w_ffn_out", act, d_r2b, (4, FF_SHARD, d), BF16, (4, 1, nk),
                      pl.BlockSpec((None, tk, FF_SHARD), lambda i, j, k: (i, k, 0)),
                      pl.BlockSpec((tk, d), lambda i, j, k: (k, 0)),
                      pl.BlockSpec((None, FF_SHARD, d), lambda i, j, k: (i, 0, 0)), TN, (FF_SHARD, d))
    d_z, sums_ffn = _ffn_act_bwd(d_act, z, gelu_and_slope, small["w_ffn_dw"])
    d_z8 = d_z.reshape(N_DEV, t, FF_SHARD)
    d_r1, d_r1b, sums_ln1 = _d_x1_ln1_bwd(d_z8, w_ffn_in, d_r2, r1, small["ln1_g"])
    g_w_ffn_in = _mm("g_w_ffn_in", x1b, d_z8, (N_DEV, d, FF_SHARD), BF16, (N_DEV, 1, nk),
                     pl.BlockSpec((tk, d), lambda i, j, k: (k, 0)),
                     pl.BlockSpec((None, tk, FF_SHARD), lambda i, j, k: (i, k, 0)),
                     pl.BlockSpec((None, d, FF_SHARD), lambda i, j, k: (i, 0, 0)), TN, (d, FF_SHARD))
    k_tile = pl.BlockSpec((tk, d), lambda i, j, k: (k, 0))
    g_w_out = _mm("g_w_out", mixed, d_r1b, (d, d), BF16, (1, 1, nk), k_tile, k_tile, sq_w, TN, (d, d))
    d_y, d_mz = _d_mixed_merge_bwd(d_r1b, w_out, proj, y_conv, y_hgrn)
    d_pre, sums_conv = _d_c_norm_bwd(d_y, w_conv_out, conv_pre, small["conv_ln_g"], small["conv_ln_b"])
    g_w_conv_out = _mm("g_w_conv_out", c_act, d_y, (CONV_DIM, d), BF16, (1, 1, nk),
                       pl.BlockSpec((tk, CONV_DIM), lambda i, j, k: (k, 0)),
                       pl.BlockSpec((None, tk, d), lambda i, j, k: (0, k, 0)),
                       pl.BlockSpec((CONV_DIM, d), lambda i, j, k: (0, 0)), TN, (CONV_DIM, d))
    g_w_conv_out = _relayout("g_w_conv_out_shards", g_w_conv_out, (CONV_DIM, 128), lambda j: (0, j),
                             (None, CONV_DIM, 128), lambda j: (j, 0, 0),
                             jax.ShapeDtypeStruct((N_DEV, CONV_DIM, 128), BF16))
    d_og = _mm("d_og", d_y, w_hgrn_out, (t, d), F32, (nm, 1, 1),
               pl.BlockSpec((None, tm, d), lambda i, j, k: (1, i, 0)), sq_w, row_tile, NT, (tm, d))
    g_w_hgrn_out = _mm("g_w_hgrn_out", og, d_y, (d, d), BF16, (1, 1, nk), k_tile,
                       pl.BlockSpec((None, tk, d), lambda i, j, k: (1, k, 0)), sq_w, TN, (d, d))
    d_cproj, g_w_conv_dw = _conv_bwd_dw(d_pre, proj, small["w_conv_dw"])

    early = [g_w_conv_out, g_w_hgrn_out.reshape(N_DEV, d // N_DEV, d), g_w_out.reshape(N_DEV, d // N_DEV, d),
             g_w_ffn_in, g_w_ffn_out.reshape(N_DEV, D_FF // N_DEV, d)]
    hgrn_out = _hgrn_bwd(d_og, o, proj, states, small["hgrn_lb_logits"], small["hgrn_norm_g"],
                         hosted=None if scatter is None else scatter(early))
    d_qz, d_fz, d_iv, d_gz, sums_hgrn = hgrn_out[:5]
    early_recv = list(hgrn_out[5:])
    d_proj = jnp.concatenate([d_cproj, d_qz, d_fz, d_iv, d_gz, d_mz], axis=1)
    g_w_in = _mm("g_w_in", xb, d_proj, (N_DEV, d, IN_SHARD), BF16, (N_DEV, 1, nk), k_tile,
                 pl.BlockSpec((tk, IN_SHARD), lambda i, j, k: (k, i)),
                 pl.BlockSpec((None, d, IN_SHARD), lambda i, j, k: (i, 0, 0)), TN, (d, IN_SHARD))
    def add_residual(acc, ins, outs, scr):
        for rows in _row_blocks(ta):
            outs[0][rows, :] = ALPHA * ins[2][rows, :] + acc[rows, :]

    ta = _pick(t, 1024)
    acc_tile = pl.BlockSpec((ta, d), lambda i, j, k: (i, 0))
    grad_x, *late_recv = _mm_fused(
        "grad_x", (t // ta, 1, N_DEV), (d_proj, w_in, d_r1),
        [pl.BlockSpec((ta, IN_SHARD), lambda i, j, k: (i, k)),
         pl.BlockSpec((None, d, IN_SHARD), lambda i, j, k: (k, 0, 0)), acc_tile],
        [jax.ShapeDtypeStruct((t, d), F32)], [acc_tile], NT, (ta, d), add_residual,
        hosted=None if scatter is None else scatter([g_w_in]))

    d_l0 = sums_hgrn[1:2]
    small_grads = {
        "loss": sums_ln2[2:3, 0:128],
        "b_conv_dw": sums_conv[2:3], "conv_ln_g": sums_conv[0:1], "conv_ln_b": sums_conv[1:2],
        "hgrn_lb_logits": jnp.concatenate([d_l0, -d_l0], axis=1),
        "hgrn_norm_g": sums_hgrn[0:1],
        "ln1_g": sums_ln1[0:1], "ln1_b": sums_ln1[1:2],
        "b_ffn_dw": sums_ffn[:, FFN_K, :].reshape(1, D_FF),
        "ln2_g": sums_ln2[0:1], "ln2_b": sums_ln2[1:2],
        "w_conv_dw": g_w_conv_dw[0:CONV_K].reshape(1, CONV_K * CONV_DIM),
        "w_ffn_dw": jnp.transpose(sums_ffn[:, 0:FFN_K, :], (1, 0, 2)).reshape(1, FFN_K * D_FF),
    }
    large_grads = [g_w_in] + early
    if scatter is not None:
        large_grads = list(zip(large_grads, late_recv + early_recv))
    return grad_x, large_grads, small_grads


def _coords():
    return lax.axis_index("x"), lax.axis_index("y"), lax.axis_index("c")


def _gather(shards):
    n = len(shards)

    def parts(ins, outs, sems):
        send_sems, recv_sems, local_sems = sems
        x, y, c = _coords()
        me = 4 * x + 2 * y + c
        sibling = (x, y, 1 - c)
        chips = [(1 - x, y), (x, 1 - y), (1 - x, 1 - y)]

        def copy(a, k, block, to, src=None):
            return pltpu.make_async_remote_copy(
                src_ref=outs[a].at[block] if src is None else src, dst_ref=outs[a].at[block],
                send_sem=send_sems.at[a, k], recv_sem=recv_sems.at[a, k], device_id=to, device_id_type=MESH)

        local = [pltpu.make_async_copy(ins[a], outs[a].at[me], local_sems.at[a]) for a in range(n)]
        first = []
        for a in range(n):
            first.append(copy(a, 0, me, sibling, src=ins[a]))
            for j, chip in enumerate(chips):
                first.append(copy(a, 1 + j, me, (*chip, c), src=ins[a]))
        return x, y, c, sibling, chips, copy, local, first

    def start(ins, outs, sems):
        *_, local, first = parts(ins, outs, sems)
        for cp in local + first:
            cp.start()

    def finish(ins, outs, sems):
        x, y, c, sibling, chips, copy, local, first = parts(ins, outs, sems)
        passed = []
        for j, (px, py) in enumerate(chips):
            for a in range(n):
                copy(a, 1 + j, 4 * px + 2 * py + c, sibling).wait_recv()
                cp = copy(a, 4 + j, 4 * px + 2 * py + c, sibling)
                cp.start()
                passed.append(cp)
        for a in range(n):
            copy(a, 0, 4 * x + 2 * y + 1 - c, sibling).wait_recv()
            for j, (px, py) in enumerate(chips):
                copy(a, 4 + j, 4 * px + 2 * py + 1 - c, sibling).wait_recv()
        for cp in first + passed:
            cp.wait_send()
        for cp in local:
            cp.wait()

    return _Hosted(shards, [jax.ShapeDtypeStruct((N_DEV,) + s.shape, s.dtype) for s in shards],
                   [pltpu.SemaphoreType.DMA((n, 7)), pltpu.SemaphoreType.DMA((n, 7)), pltpu.SemaphoreType.DMA((n,))],
                   start, finish)


def _scatter(grads):
    n = len(grads)

    def copies(ins, outs, sems):
        send_sems, recv_sems = sems
        x, y, c = _coords()
        out = []
        for a in range(n):
            for k in range(1, N_DEV):
                px, py, pc = x ^ (k >> 2), y ^ ((k >> 1) & 1), c ^ (k & 1)
                out.append(pltpu.make_async_remote_copy(
                    src_ref=ins[a].at[4 * px + 2 * py + pc], dst_ref=outs[a].at[k - 1],
                    send_sem=send_sems.at[a, k - 1], recv_sem=recv_sems.at[a, k - 1],
                    device_id=(px, py, pc), device_id_type=MESH))
        return out

    def start(ins, outs, sems):
        for cp in copies(ins, outs, sems):
            cp.start()

    def finish(ins, outs, sems):
        for cp in copies(ins, outs, sems):
            cp.wait()

    return _Hosted(grads, [jax.ShapeDtypeStruct((N_DEV - 1,) + g.shape[1:], g.dtype) for g in grads],
                   [pltpu.SemaphoreType.DMA((n, N_DEV - 1)), pltpu.SemaphoreType.DMA((n, N_DEV - 1))], start, finish)


def _row_tile(rows):
    return 256 if rows % 256 == 0 else rows


def _adam_math(w, g, m, v):
    m_new = ADAM_B1 * m + (1.0 - ADAM_B1) * g
    v_new = ADAM_B2 * v + (1.0 - ADAM_B2) * (g * g)
    m_hat = m_new / (1.0 - ADAM_B1 ** ADAM_STEP)
    v_hat = v_new / (1.0 - ADAM_B2 ** ADAM_STEP)
    delta = -ADAM_LR * (m_hat / (jnp.sqrt(v_hat) + ADAM_EPS) + ADAM_WD * w)
    return delta, m_new, v_new


def _adam_large(name, own, recv, me, w, m, v):
    rows, cols = w.shape
    tr = _row_tile(rows)

    def body(me_ref, p_ref, r_ref, w_ref, m_ref, v_ref, g_out, d_out, m_out, v_out):
        g = p_ref[...].astype(F32)
        for k in range(N_DEV - 1):
            g = g + r_ref[k].astype(F32)
        delta, m_new, v_new = _adam_math(w_ref[...], g, m_ref[...], v_ref[...])
        g_out[...] = g
        d_out[...] = delta
        m_out[...] = m_new
        v_out[...] = v_new

    tile = pl.BlockSpec((tr, cols), lambda r, me_ref: (r, 0))
    sds = jax.ShapeDtypeStruct((rows, cols), F32)
    return pl.pallas_call(
        body, name=name,
        grid_spec=pltpu.PrefetchScalarGridSpec(
            num_scalar_prefetch=1, grid=(rows // tr,),
            in_specs=[pl.BlockSpec((None, tr, cols), lambda r, me_ref: (me_ref[0], r, 0)),
                      pl.BlockSpec((N_DEV - 1, tr, cols), lambda r, me_ref: (0, r, 0)), tile, tile, tile],
            out_specs=[tile, tile, tile, tile]),
        out_shape=[sds, sds, sds, sds],
        compiler_params=_params(("parallel",)),
    )(me, own, recv, w, m, v)


def _small_allreduce(vec):
    rows = vec.shape[0]

    def body(v_ref, o_ref, gat_ref, send_sems, recv_sems):
        x, y, c = _coords()
        me = 4 * x + 2 * y + c
        gat_ref[me] = v_ref[...]
        copies = []
        for k in range(1, N_DEV):
            px, py, pc = x ^ (k >> 2), y ^ ((k >> 1) & 1), c ^ (k & 1)
            copies.append(pltpu.make_async_remote_copy(
                src_ref=v_ref, dst_ref=gat_ref.at[me], send_sem=send_sems.at[k - 1], recv_sem=recv_sems.at[k - 1],
                device_id=(px, py, pc), device_id_type=MESH))
        for cp in copies:
            cp.start()
        for k in range(1, N_DEV):
            px, py, pc = x ^ (k >> 2), y ^ ((k >> 1) & 1), c ^ (k & 1)
            pltpu.make_async_remote_copy(
                src_ref=v_ref, dst_ref=gat_ref.at[4 * px + 2 * py + pc], send_sem=send_sems.at[k - 1],
                recv_sem=recv_sems.at[k - 1], device_id=(px, py, pc), device_id_type=MESH).wait_recv()
        for cp in copies:
            cp.wait_send()
        acc = gat_ref[0]
        for dev in range(1, N_DEV):
            acc = acc + gat_ref[dev]
        o_ref[...] = acc

    whole = pl.BlockSpec(memory_space=pltpu.VMEM)
    return pl.pallas_call(
        body, name="small_allreduce", in_specs=[whole], out_specs=whole,
        out_shape=jax.ShapeDtypeStruct((rows, 128), F32),
        scratch_shapes=[pltpu.VMEM((N_DEV, rows, 128), F32), pltpu.SemaphoreType.DMA((N_DEV - 1,)),
                        pltpu.SemaphoreType.DMA((N_DEV - 1,))],
        compiler_params=pltpu.CompilerParams(has_side_effects=True, vmem_limit_bytes=VMEM_LIMIT),
    )(vec)


def _adam_small(w, g, m, v):
    def body(w_ref, g_ref, m_ref, v_ref, d_out, m_out, v_out):
        delta, m_new, v_new = _adam_math(w_ref[...], g_ref[...], m_ref[...], v_ref[...])
        d_out[...] = delta
        m_out[...] = m_new
        v_out[...] = v_new

    whole = pl.BlockSpec(memory_space=pltpu.VMEM)
    sds = jax.ShapeDtypeStruct(w.shape, F32)
    return pl.pallas_call(body, name="adam_small", in_specs=[whole] * 4, out_specs=[whole] * 3,
                          out_shape=[sds, sds, sds])(w, g, m, v)


_SMALL_ORDER = ["loss", "b_conv_dw", "conv_ln_g", "conv_ln_b", "hgrn_lb_logits", "hgrn_norm_g", "ln1_g", "ln1_b",
                "b_ffn_dw", "ln2_g", "ln2_b", "w_conv_dw", "w_ffn_dw"]
_WEIGHTS = ["w_in", "w_conv_dw", "b_conv_dw", "conv_ln_g", "conv_ln_b", "w_conv_out", "hgrn_lb_logits", "hgrn_norm_g",
            "w_hgrn_out", "w_out", "ln1_g", "ln1_b", "w_ffn_in", "w_ffn_dw", "b_ffn_dw", "w_ffn_out", "ln2_g", "ln2_b"]
_LARGE = ["w_in", "w_conv_out", "w_hgrn_out", "w_out", "w_ffn_in", "w_ffn_out"]
_CONV_DW_SHARD = CONV_DIM // N_DEV
_FFN_DW_SHARD = D_FF // N_DEV


def kernel(x, w_in, w_conv_dw, b_conv_dw, conv_ln_g, conv_ln_b, w_conv_out, hgrn_lb_logits, hgrn_norm_g, w_hgrn_out, w_out, ln1_g, ln1_b, w_ffn_in, w_ffn_dw, b_ffn_dw, w_ffn_out, ln2_g, ln2_b, loss_target, m_w_in, m_w_conv_dw, m_b_conv_dw, m_conv_ln_g, m_conv_ln_b, m_w_conv_out, m_hgrn_lb_logits, m_hgrn_norm_g, m_w_hgrn_out, m_w_out, m_ln1_g, m_ln1_b, m_w_ffn_in, m_w_ffn_dw, m_b_ffn_dw, m_w_ffn_out, m_ln2_g, m_ln2_b, v_w_in, v_w_conv_dw, v_b_conv_dw, v_conv_ln_g, v_conv_ln_b, v_w_conv_out, v_hgrn_lb_logits, v_hgrn_norm_g, v_w_hgrn_out, v_w_out, v_ln1_g, v_ln1_b, v_w_ffn_in, v_w_ffn_dw, v_b_ffn_dw, v_w_ffn_out, v_ln2_g, v_ln2_b):
    w = dict(w_in=w_in, w_conv_dw=w_conv_dw, b_conv_dw=b_conv_dw, conv_ln_g=conv_ln_g, conv_ln_b=conv_ln_b,
             w_conv_out=w_conv_out, hgrn_lb_logits=hgrn_lb_logits, hgrn_norm_g=hgrn_norm_g, w_hgrn_out=w_hgrn_out,
             w_out=w_out, ln1_g=ln1_g, ln1_b=ln1_b, w_ffn_in=w_ffn_in, w_ffn_dw=w_ffn_dw, b_ffn_dw=b_ffn_dw,
             w_ffn_out=w_ffn_out, ln2_g=ln2_g, ln2_b=ln2_b)
    m = dict(w_in=m_w_in, w_conv_dw=m_w_conv_dw, b_conv_dw=m_b_conv_dw, conv_ln_g=m_conv_ln_g, conv_ln_b=m_conv_ln_b,
             w_conv_out=m_w_conv_out, hgrn_lb_logits=m_hgrn_lb_logits, hgrn_norm_g=m_hgrn_norm_g,
             w_hgrn_out=m_w_hgrn_out, w_out=m_w_out, ln1_g=m_ln1_g, ln1_b=m_ln1_b, w_ffn_in=m_w_ffn_in,
             w_ffn_dw=m_w_ffn_dw, b_ffn_dw=m_b_ffn_dw, w_ffn_out=m_w_ffn_out, ln2_g=m_ln2_g, ln2_b=m_ln2_b)
    v = dict(w_in=v_w_in, w_conv_dw=v_w_conv_dw, b_conv_dw=v_b_conv_dw, conv_ln_g=v_conv_ln_g, conv_ln_b=v_conv_ln_b,
             w_conv_out=v_w_conv_out, hgrn_lb_logits=v_hgrn_lb_logits, hgrn_norm_g=v_hgrn_norm_g,
             w_hgrn_out=v_w_hgrn_out, w_out=v_w_out, ln1_g=v_ln1_g, ln1_b=v_ln1_b, w_ffn_in=v_w_ffn_in,
             w_ffn_dw=v_w_ffn_dw, b_ffn_dw=v_b_ffn_dw, w_ffn_out=v_w_ffn_out, ln2_g=v_ln2_g, ln2_b=v_ln2_b)
    xi, yi, ci = lax.axis_index("x"), lax.axis_index("y"), lax.axis_index("c")
    me = 4 * xi + 2 * yi + ci
    me_op = jnp.reshape(me, (1,)).astype(jnp.int32)

    shards = [w[name][0].astype(BF16) for name in _LARGE]
    shards.append(jnp.pad(w_conv_dw[0], ((0, 1), (0, 128 - _CONV_DW_SHARD))))
    shards.append(jnp.pad(w_ffn_dw[0], ((0, 8 - FFN_K), (0, 384 - _FFN_DW_SHARD))))
    (w_in_all,) = _run_hosted("all_gather_w_in", _gather(shards[:1]))
    small = dict(b_conv_dw=b_conv_dw, conv_ln_g=conv_ln_g, conv_ln_b=conv_ln_b, hgrn_lb_logits=hgrn_lb_logits,
                 hgrn_norm_g=hgrn_norm_g, ln1_g=ln1_g, ln1_b=ln1_b, ln2_g=ln2_g, ln2_b=ln2_b, b_ffn_dw=b_ffn_dw)

    grad_x, large_grads, small_grads = _local_step(x[0], loss_target[0], w_in_all, _gather(shards[1:]), small, _scatter)

    out = {}
    for name, (own, recv) in zip(_LARGE, large_grads):
        out[name] = _adam_large("adam_" + name, own, recv, me_op, w[name][0], m[name][0], v[name][0])

    vec = jnp.concatenate([small_grads[name] for name in _SMALL_ORDER], axis=1)
    total = _small_allreduce(vec.reshape(-1, 128)).reshape(1, -1)
    sizes = [small_grads[name].shape[1] for name in _SMALL_ORDER]
    offs = [0]
    for s in sizes:
        offs.append(offs[-1] + s)
    summed = {name: total[:, offs[i]:offs[i + 1]] for i, name in enumerate(_SMALL_ORDER)}
    loss = summed["loss"][0, 0]
    conv_dw_g = lax.dynamic_slice_in_dim(summed["w_conv_dw"].reshape(CONV_K, CONV_DIM), me * _CONV_DW_SHARD, _CONV_DW_SHARD, axis=1)
    ffn_dw_g = lax.dynamic_slice_in_dim(summed["w_ffn_dw"].reshape(FFN_K, D_FF), me * _FFN_DW_SHARD, _FFN_DW_SHARD, axis=1)
    small_g = dict(summed, w_conv_dw=conv_dw_g.reshape(1, -1), w_ffn_dw=ffn_dw_g.reshape(1, -1))
    names = [n for n in _SMALL_ORDER if n != "loss"]
    flat = lambda d, n: d[n].reshape(1, -1)
    n_small = sum(small_g[n].shape[1] for n in names)
    pad = (-n_small) % 1024
    pack = lambda pieces: jnp.pad(jnp.concatenate(pieces, axis=1), ((0, 0), (0, pad))).reshape(-1, 128)
    d_s, m_s, v_s = _adam_small(pack([flat(w, n) for n in names]), pack([small_g[n] for n in names]),
                                pack([flat(m, n) for n in names]), pack([flat(v, n) for n in names]))
    pos = 0
    for n in names:
        size = small_g[n].shape[1]
        cut = lambda a: a.reshape(1, -1)[:, pos:pos + size].reshape(w[n].shape)
        out[n] = (small_g[n].reshape(w[n].shape), cut(d_s), cut(m_s), cut(v_s))
        pos += size

    for name in _LARGE:
        out[name] = tuple(a.reshape(w[name].shape) for a in out[name])
    grads = [out[n][0] for n in _WEIGHTS]
    deltas = [out[n][1] for n in _WEIGHTS]
    new_m = [out[n][2] for n in _WEIGHTS]
    new_v = [out[n][3] for n in _WEIGHTS]
    return (loss, grad_x[None], *grads, *deltas, *new_m, *new_v)
```

```python
import functools
import math

import jax
import jax.numpy as jnp
from jax import lax
from jax.experimental import pallas as pl
from jax.experimental.pallas import tpu as pltpu

F32 = jnp.float32
BF16 = jnp.bfloat16

N_DEV = 8
D_MODEL = 1024
CONV_DIM = 512
CONV_K = 31
HGRN_DIM = 1024
HEADS = 8
HEAD_DIM = 128
D_FF = 2816
FFN_K = 3
FF_SHARD = 2 * D_FF // N_DEV
IN_COLS = 7168
IN_SHARD = IN_COLS // N_DEV
LN_EPS = 1e-5
RMS_EPS = 1e-6
ALPHA = 2.0 ** 0.25

ADAM_LR = 0.001
ADAM_B1 = 0.9
ADAM_B2 = 0.999
ADAM_EPS = 1e-08
ADAM_WD = 0.01
ADAM_STEP = 10

CHUNK = 64
CHUNKS_PER_BLOCK = 8
CONV_HALO = 32
FFN_HALO = 8
ROW_BLOCK = 64
SUBLANES = 8
VMEM_LIMIT = 48 * 1024 * 1024
MXU_DEPTH = 256

MESH = pl.DeviceIdType.MESH
ANY = pl.BlockSpec(memory_space=pl.ANY)

NN = (((1,), (0,)), ((), ()))
NT = (((1,), (1,)), ((), ()))
TN = (((0,), (0,)), ((), ()))


def _params(sem):
    return pltpu.CompilerParams(dimension_semantics=sem, vmem_limit_bytes=VMEM_LIMIT)


def _dot(a, b, dims):
    return lax.dot_general(a.astype(BF16), b.astype(BF16), dims, preferred_element_type=F32)


def _sigmoid(x):
    return jax.nn.sigmoid(x)


def _ln(r):
    mu = jnp.mean(r, axis=-1, keepdims=True)
    xc = r - mu
    var = jnp.mean(xc * xc, axis=-1, keepdims=True)
    rstd = lax.rsqrt(var + LN_EPS)
    return xc * rstd, rstd


def _ln_bwd(dy, xhat, rstd, g):
    dxh = dy * g
    m1 = jnp.mean(dxh, axis=-1, keepdims=True)
    m2 = jnp.mean(dxh * xhat, axis=-1, keepdims=True)
    return rstd * (dxh - m1 - xhat * m2)


def _colsum(x):
    return jnp.sum(x, axis=0, keepdims=True)


class _Hosted:
    def __init__(self, inputs, out_shapes, sem_shapes, start, finish):
        self.inputs, self.out_shapes, self.sem_shapes = list(inputs), list(out_shapes), list(sem_shapes)
        self.start, self.finish = start, finish


def _call(body, *, name, grid, in_specs, out_specs, out_shape, scratch_shapes, semantics, operands, hosted=None):
    if hosted is None:
        return pl.pallas_call(
            body, name=name, grid=grid, in_specs=list(in_specs), out_specs=list(out_specs), out_shape=list(out_shape),
            scratch_shapes=list(scratch_shapes), compiler_params=_params(semantics))(*operands)
    n_in, n_out, n_scr = len(in_specs), len(out_specs), len(scratch_shapes)
    h_in, h_out = len(hosted.inputs), len(hosted.out_shapes)

    def full_body(*refs):
        ins, refs = refs[:n_in], refs[n_in:]
        h_ins, refs = refs[:h_in], refs[h_in:]
        outs, refs = refs[:n_out], refs[n_out:]
        h_outs, refs = refs[:h_out], refs[h_out:]
        scr, sems = refs[:n_scr], refs[n_scr:]
        first = functools.reduce(jnp.logical_and, [pl.program_id(d) == 0 for d in range(len(grid))])
        last = functools.reduce(jnp.logical_and, [pl.program_id(d) == grid[d] - 1 for d in range(len(grid))])

        @pl.when(first)
        def _():
            hosted.start(h_ins, h_outs, sems)

        body(*ins, *outs, *scr)

        @pl.when(last)
        def _():
            hosted.finish(h_ins, h_outs, sems)

    return pl.pallas_call(
        full_body, name=name, grid=grid, in_specs=list(in_specs) + [ANY] * h_in,
        out_specs=list(out_specs) + [ANY] * h_out, out_shape=list(out_shape) + hosted.out_shapes,
        scratch_shapes=list(scratch_shapes) + hosted.sem_shapes,
        compiler_params=pltpu.CompilerParams(dimension_semantics=("arbitrary",) * len(grid),
                                             vmem_limit_bytes=VMEM_LIMIT, has_side_effects=True),
    )(*operands, *hosted.inputs)


def _run_hosted(name, hosted):
    h_in, h_out = len(hosted.inputs), len(hosted.out_shapes)

    def body(*refs):
        ins, outs, sems = refs[:h_in], refs[h_in:h_in + h_out], refs[h_in + h_out:]
        hosted.start(ins, outs, sems)
        hosted.finish(ins, outs, sems)

    return pl.pallas_call(
        body, name=name, in_specs=[ANY] * h_in, out_specs=[ANY] * h_out, out_shape=hosted.out_shapes,
        scratch_shapes=hosted.sem_shapes, compiler_params=pltpu.CompilerParams(has_side_effects=True),
    )(*hosted.inputs)


def _mm(name, a, b, out_shape, out_dtype, grid, a_spec, b_spec, o_spec, dims, acc_shape, hosted=None):
    nk = grid[2]
    if nk == 1:
        def body(a_ref, b_ref, o_ref):
            o_ref[...] = _dot(a_ref[...], b_ref[...], dims).astype(o_ref.dtype)
        scratch = []
    else:
        def body(a_ref, b_ref, o_ref, acc_ref):
            k = pl.program_id(2)

            @pl.when(k == 0)
            def _():
                acc_ref[...] = jnp.zeros_like(acc_ref)

            acc_ref[...] += _dot(a_ref[...], b_ref[...], dims)

            @pl.when(k == nk - 1)
            def _():
                o_ref[...] = acc_ref[...].astype(o_ref.dtype)
        scratch = [pltpu.VMEM(acc_shape, F32)]

    outs = _call(body, name=name, grid=grid, in_specs=[a_spec, b_spec], out_specs=[o_spec],
                 out_shape=[jax.ShapeDtypeStruct(out_shape, out_dtype)], scratch_shapes=scratch,
                 semantics=("parallel", "parallel", "arbitrary"), operands=(a, b), hosted=hosted)
    return outs[0] if hosted is None else (outs[0], list(outs[1:]))


def _mm_fused(name, grid, operands, in_specs, out_shape, out_specs, dims, acc_shape, epilogue, lhs=None, scratch=(),
              hosted=None):
    nk = grid[2]
    n_in, n_out = len(in_specs), len(out_specs)

    def body(*refs):
        ins, outs, scr = refs[:n_in], refs[n_in:n_in + n_out], refs[n_in + n_out:]
        acc_ref, k = scr[0], pl.program_id(2)
        a = ins[0][...] if lhs is None else lhs(ins, outs)
        part = _dot(a, ins[1][...], dims)
        if nk == 1:
            acc_ref[...] = part
            epilogue(acc_ref, ins, outs, scr[1:])
            return

        @pl.when(k == 0)
        def _():
            acc_ref[...] = jnp.zeros_like(acc_ref)

        acc_ref[...] += part

        @pl.when(k == nk - 1)
        def _():
            epilogue(acc_ref, ins, outs, scr[1:])

    return _call(body, name=name, grid=grid, in_specs=in_specs, out_specs=out_specs, out_shape=out_shape,
                 scratch_shapes=[pltpu.VMEM(acc_shape, F32)] + list(scratch), semantics=("arbitrary",) * 3,
                 operands=operands, hosted=hosted)


def _row_blocks(rows, block=256):
    block = block if rows % block == 0 else rows
    return [slice(r, r + block) for r in range(0, rows, block)]


def _pick(t, pref):
    return pref if t % pref == 0 else t


def _glu(p):
    return p[:, :CONV_DIM] * _sigmoid(p[:, CONV_DIM:])


def _by_phase(taps):
    phases = {}
    for off, payload in taps:
        phases.setdefault(off % SUBLANES, []).append((off - off % SUBLANES, payload))
    return sorted(phases.items())


def _tap_sum(src_ref, base, taps, rows, lanes):
    acc = None
    for phase, items in _by_phase(taps):
        n = rows if phase == 0 else rows + SUBLANES
        part = None
        for off, (w_ref, k) in items:
            term = w_ref[k:k + 1, lanes] * src_ref[base + off:base + off + n, lanes]
            part = term if part is None else part + term
        if phase:
            part = part[phase:phase + rows, :]
        acc = part if acc is None else acc + part
    return acc


def _tap_products(x, src_ref, base, taps, lanes):
    rows, cols = x.shape
    pad = jnp.zeros((SUBLANES, cols), x.dtype)
    padded = jnp.concatenate([pad, x, pad], axis=0)
    out = []
    for phase, items in _by_phase(taps):
        n = rows if phase == 0 else rows + SUBLANES
        shifted = x if phase == 0 else padded[SUBLANES - phase:SUBLANES - phase + n, :]
        for off, key in items:
            out.append((key, _colsum(shifted * src_ref[base + off:base + off + n, lanes])))
    return out


def _lane_blocks(cols, block=256):
    return [slice(c, min(c + block, cols)) for c in range(0, cols, block)]


def _conv_fwd(proj, w_dw, b_dw, g, b):
    t = proj.shape[0]
    tm = _pick(t, 256)
    nh = tm // CONV_HALO

    def body(p_ref, ph_ref, w_ref, bd_ref, g_ref, b_ref, act_ref, pre_ref, xs_ref):
        i = pl.program_id(0)
        halo = _glu(ph_ref[...])
        xs_ref[0:CONV_HALO, :] = jnp.where(i == 0, 0.0, halo)
        xs_ref[CONV_HALO:CONV_HALO + tm, :] = _glu(p_ref[...])
        taps = [(CONV_HALO - (CONV_K - 1) + k, (w_ref, k)) for k in range(CONV_K)]
        for r in range(tm // ROW_BLOCK):
            rows = slice(r * ROW_BLOCK, (r + 1) * ROW_BLOCK)
            for lanes in _lane_blocks(CONV_DIM):
                pre_ref[rows, lanes] = bd_ref[:, lanes] + _tap_sum(xs_ref, r * ROW_BLOCK, taps, ROW_BLOCK, lanes)
            acc = pre_ref[rows, :]
            xhat, _ = _ln(acc)
            yln = xhat * g_ref[...] + b_ref[...]
            act_ref[rows, :] = (yln * _sigmoid(yln)).astype(BF16)

    full = lambda s: pl.BlockSpec(s, lambda i: (0, 0))
    return pl.pallas_call(
        body, name="conv_fwd", grid=(t // tm,),
        in_specs=[pl.BlockSpec((tm, 2 * CONV_DIM), lambda i: (i, 0)),
                  pl.BlockSpec((CONV_HALO, 2 * CONV_DIM), lambda i: (jnp.maximum(i * nh - 1, 0), 0)),
                  full((CONV_K, CONV_DIM)), full((1, CONV_DIM)), full((1, CONV_DIM)), full((1, CONV_DIM))],
        out_specs=[pl.BlockSpec((tm, CONV_DIM), lambda i: (i, 0)), pl.BlockSpec((tm, CONV_DIM), lambda i: (i, 0))],
        out_shape=[jax.ShapeDtypeStruct((t, CONV_DIM), BF16), jax.ShapeDtypeStruct((t, CONV_DIM), F32)],
        scratch_shapes=[pltpu.VMEM((CONV_HALO + tm, CONV_DIM), F32)],
        compiler_params=_params(("arbitrary",)),
    )(proj, proj, w_dw, b_dw, g, b)


def _d_c_norm_bwd(d_y, w_conv_out, pre, g, b):
    t = pre.shape[0]
    tm = _pick(t, 512)
    d = D_MODEL

    def epilogue(d_c, ins, outs, scr):
        pre_ref, g_ref, b_ref = ins[2:]
        dpre_ref, sums_ref = outs
        i = pl.program_id(0)

        @pl.when(i == 0)
        def _():
            sums_ref[...] = jnp.zeros_like(sums_ref)

        for rows in _row_blocks(tm):
            xhat, rstd = _ln(pre_ref[rows, :])
            yln = xhat * g_ref[...] + b_ref[...]
            sg = _sigmoid(yln)
            dyln = d_c[rows, :] * (sg * (1.0 + yln * (1.0 - sg)))
            dpre = _ln_bwd(dyln, xhat, rstd, g_ref[...])
            dpre_ref[rows, :] = dpre
            sums_ref[0:1, :] += _colsum(dyln * xhat)
            sums_ref[1:2, :] += _colsum(dyln)
            sums_ref[2:3, :] += _colsum(dpre)

    full = lambda s: pl.BlockSpec(s, lambda i, j, k: (0, 0))
    tile = pl.BlockSpec((tm, CONV_DIM), lambda i, j, k: (i, 0))
    return _mm_fused(
        "d_c_norm_bwd", (t // tm, 1, 1), (d_y, w_conv_out, pre, g, b),
        [pl.BlockSpec((None, tm, d), lambda i, j, k: (0, i, 0)), full((CONV_DIM, d)), tile,
         full((1, CONV_DIM)), full((1, CONV_DIM))],
        [jax.ShapeDtypeStruct((t, CONV_DIM), F32), jax.ShapeDtypeStruct((8, CONV_DIM), F32)],
        [tile, full((8, CONV_DIM))], NT, (tm, CONV_DIM), epilogue)


def _conv_bwd_dw(d_pre, proj, w_dw):
    t = d_pre.shape[0]
    tm = _pick(t, 256)
    nt = t // tm
    nh = tm // CONV_HALO
    last_h = t // CONV_HALO - 1

    def body(dp_ref, dph_ref, p_ref, ph_ref, w_ref, dproj_ref, dw_ref, xs_ref, ds_ref):
        i = pl.program_id(0)

        @pl.when(i == 0)
        def _():
            dw_ref[...] = jnp.zeros_like(dw_ref)

        halo = _glu(ph_ref[...])
        xs_ref[0:CONV_HALO, :] = jnp.where(i == 0, 0.0, halo)
        xs_ref[CONV_HALO:CONV_HALO + tm, :] = _glu(p_ref[...])
        ds_ref[0:tm, :] = dp_ref[...]
        ds_ref[tm:tm + CONV_HALO, :] = jnp.where(i == nt - 1, 0.0, dph_ref[...])
        back_taps = [(CONV_K - 1 - k, (w_ref, k)) for k in range(CONV_K)]
        grad_taps = [(CONV_HALO - (CONV_K - 1) + k, k) for k in range(CONV_K)]
        for r in range(tm // ROW_BLOCK):
            base = r * ROW_BLOCK
            rows = slice(base, base + ROW_BLOCK)
            for lanes in _lane_blocks(CONV_DIM):
                gate_lanes = slice(CONV_DIM + lanes.start, CONV_DIM + lanes.stop)
                acc = _tap_sum(ds_ref, base, back_taps, ROW_BLOCK, lanes)
                for k, total in _tap_products(ds_ref[rows, lanes], xs_ref, base, grad_taps, lanes):
                    dw_ref[k:k + 1, lanes] += total
                cval = p_ref[rows, lanes]
                sg = _sigmoid(p_ref[rows, gate_lanes])
                dproj_ref[rows, lanes] = (acc * sg).astype(BF16)
                dproj_ref[rows, gate_lanes] = (acc * cval * sg * (1.0 - sg)).astype(BF16)

    full = lambda s: pl.BlockSpec(s, lambda i: (0, 0))
    return pl.pallas_call(
        body, name="conv_bwd_dw", grid=(nt,),
        in_specs=[pl.BlockSpec((tm, CONV_DIM), lambda i: (i, 0)),
                  pl.BlockSpec((CONV_HALO, CONV_DIM), lambda i: (jnp.minimum((i + 1) * nh, last_h), 0)),
                  pl.BlockSpec((tm, 2 * CONV_DIM), lambda i: (i, 0)),
                  pl.BlockSpec((CONV_HALO, 2 * CONV_DIM), lambda i: (jnp.maximum(i * nh - 1, 0), 0)),
                  full((CONV_K, CONV_DIM))],
        out_specs=[pl.BlockSpec((tm, 2 * CONV_DIM), lambda i: (i, 0)), full((CONV_HALO, CONV_DIM))],
        out_shape=[jax.ShapeDtypeStruct((t, 2 * CONV_DIM), BF16), jax.ShapeDtypeStruct((CONV_HALO, CONV_DIM), F32)],
        scratch_shapes=[pltpu.VMEM((CONV_HALO + tm, CONV_DIM), F32), pltpu.VMEM((tm + CONV_HALO, CONV_DIM), F32)],
        compiler_params=_params(("arbitrary",)),
    )(d_pre, d_pre, proj, proj, w_dw)


def _lower_bound(logit_ref):
    l0 = logit_ref[0:1, :]
    l1 = logit_ref[1:2, :]
    m = jnp.maximum(l0, l1)
    e0 = jnp.exp(l0 - m)
    e1 = jnp.exp(l1 - m)
    return e0 / (e0 + e1)


def _tri(lower):
    r = lax.broadcasted_iota(jnp.int32, (CHUNK, CHUNK), 0)
    c = lax.broadcasted_iota(jnp.int32, (CHUNK, CHUNK), 1)
    return (c <= r) if lower else (c >= r)


def _hgrn_gates(fz, lb):
    s = _sigmoid(fz)
    sn = _sigmoid(-fz)
    f = lb + (1.0 - lb) * s
    return s, sn, f


def _block_tri(rows, lower=True):
    r = lax.broadcasted_iota(jnp.int32, (rows, rows), 0)
    c = lax.broadcasted_iota(jnp.int32, (rows, rows), 1)
    tri = (c <= r) if lower else (c >= r)
    return (tri & (r // CHUNK == c // CHUNK)).astype(BF16)


def _tri_rows(tm):
    return min(tm, MXU_DEPTH)


def _tri_matmul(tri_ref, x):
    hi = x.astype(BF16)
    lo = (x - hi.astype(F32)).astype(BF16)
    tri = tri_ref[...]
    return (lax.dot_general(tri, hi, NN, preferred_element_type=F32)
            + lax.dot_general(tri, lo, NN, preferred_element_type=F32))


def _groups(tm):
    g = _tri_rows(tm)
    return [slice(i * g, (i + 1) * g) for i in range(tm // g)]


def _hgrn_fwd(proj, logits, norm_g):
    t = proj.shape[0]
    tm = CHUNK * CHUNKS_PER_BLOCK if t % (CHUNK * CHUNKS_PER_BLOCK) == 0 else CHUNK
    cpb = tm // CHUNK
    nt = t // tm
    half = CHUNK // 2

    def body(qz_ref, fz_ref, iv_ref, gz_ref, lg_ref, ng_ref, tri_ref, o_ref, og_ref, st_ref,
             state_ref, qe_ref, ke_ref, qb_ref, kl_ref, v_ref, upd_ref, decay_ref, a_ref, q_ref, kk_ref, b_ref):
        j = pl.program_id(1)

        @pl.when(j == 0)
        def _():
            state_ref[...] = jnp.zeros_like(state_ref)

        lb = _lower_bound(lg_ref)
        chunks = [slice(c * CHUNK, (c + 1) * CHUNK) for c in range(cpb)]
        for rows in chunks:
            qz = qz_ref[rows, :]
            q_ref[rows, :] = qz * _sigmoid(qz)
            _, sn, f = _hgrn_gates(fz_ref[rows, :], lb)
            kk_ref[rows, :] = (1.0 - lb) * sn
            b_ref[rows, :] = jnp.log(f)
            v_ref[rows, :] = iv_ref[rows, :].astype(BF16)
        for rows in _groups(tm):
            b_ref[rows, :] = _tri_matmul(tri_ref, b_ref[rows, :])
        for c, rows in enumerate(chunks):
            b = b_ref[rows, :]
            bref = b[half - 1:half, :]
            blast = b[CHUNK - 1:CHUNK, :]
            q = q_ref[rows, :]
            kk = kk_ref[rows, :]
            qb_ref[rows, :] = (q * jnp.exp(b)).astype(BF16)
            qe_ref[rows, :] = (q * jnp.exp(b - bref)).astype(BF16)
            ke_ref[rows, :] = (kk * jnp.exp(bref - b)).astype(BF16)
            kl_ref[rows, :] = (kk * jnp.exp(blast - b)).astype(BF16)
            decay_ref[c:c + 1, :] = jnp.exp(blast)
        causal = _tri(True)
        for c, rows in enumerate(chunks):
            upd_ref[c] = _dot(v_ref[rows, :], kl_ref[rows, :], TN)
            a_ref[c] = jnp.where(causal, _dot(qe_ref[rows, :], ke_ref[rows, :], NT), 0.0).astype(BF16)
        state = state_ref[...]
        for c in range(cpb):
            st_ref[c] = state.astype(BF16)
            state = state * decay_ref[c:c + 1, :] + upd_ref[c]
        state_ref[...] = state
        for c, rows in enumerate(chunks):
            o_ref[rows, :] = _dot(a_ref[c], v_ref[rows, :], NN) + _dot(qb_ref[rows, :], st_ref[c], NT)
        for rows in chunks:
            o = o_ref[rows, :]
            r = lax.rsqrt(jnp.mean(o * o, axis=-1, keepdims=True) + RMS_EPS)
            gz = gz_ref[rows, :]
            og_ref[rows, :] = ((o * r * ng_ref[...]) * (gz * _sigmoid(gz))).astype(BF16)

    col = lambda base: pl.BlockSpec((tm, HEAD_DIM), lambda h, j: (j, base + h))
    tile_bf = pltpu.VMEM((tm, HEAD_DIM), BF16)
    tile_f32 = pltpu.VMEM((tm, HEAD_DIM), F32)
    return pl.pallas_call(
        body, name="hgrn_fwd", grid=(HEADS, nt),
        in_specs=[col(8), col(16), col(24), col(32),
                  pl.BlockSpec((2, HEAD_DIM), lambda h, j: (0, h)), pl.BlockSpec((1, HEAD_DIM), lambda h, j: (0, h)),
                  pl.BlockSpec((_tri_rows(tm), _tri_rows(tm)), lambda h, j: (0, 0))],
        out_specs=[col(0), col(0), pl.BlockSpec((None, cpb, HEAD_DIM, HEAD_DIM), lambda h, j: (h, j, 0, 0))],
        out_shape=[jax.ShapeDtypeStruct((t, HGRN_DIM), F32), jax.ShapeDtypeStruct((t, HGRN_DIM), BF16),
                   jax.ShapeDtypeStruct((HEADS, t // CHUNK, HEAD_DIM, HEAD_DIM), BF16)],
        scratch_shapes=[pltpu.VMEM((HEAD_DIM, HEAD_DIM), F32), tile_bf, tile_bf, tile_bf, tile_bf, tile_bf,
                        pltpu.VMEM((cpb, HEAD_DIM, HEAD_DIM), F32), pltpu.VMEM((max(cpb, 8), HEAD_DIM), F32),
                        pltpu.VMEM((cpb, CHUNK, CHUNK), BF16), tile_f32, tile_f32, tile_f32],
        compiler_params=_params(("parallel", "arbitrary")),
    )(proj, proj, proj, proj, logits, norm_g, _block_tri(_tri_rows(tm)))


def _hgrn_bwd(d_og, o, proj, states, logits, norm_g, hosted=None):
    t = proj.shape[0]
    tm = CHUNK * CHUNKS_PER_BLOCK if t % (CHUNK * CHUNKS_PER_BLOCK) == 0 else CHUNK
    cpb = tm // CHUNK
    nt = t // tm
    half = CHUNK // 2

    def body(dog_ref, o_ref, qz_ref, fz_ref, iv_ref, gz_ref, st_ref, lg_ref, ng_ref, tril_ref, triu_ref,
             dqz_ref, dfz_ref, div_ref, dgz_ref, sums_ref,
             dstate_ref, qe_ref, ke_ref, qb_ref, kl_ref, v_ref, do_ref, upd_ref, dst_ref, a_ref, da_ref,
             decay_ref, through_ref, q_ref, kk_ref, b_ref, dsilu_ref, gs_ref, gf_ref, sn_ref,
             eb_ref, ebr_ref, ekr_ref, ebl_ref, rev_ref, pre_ref, dk_ref):
        j = pl.program_id(1)

        @pl.when(j == 0)
        def _():
            dstate_ref[...] = jnp.zeros_like(dstate_ref)
            sums_ref[...] = jnp.zeros_like(sums_ref)

        lb = _lower_bound(lg_ref)
        ng = ng_ref[...]
        chunks = [slice(c * CHUNK, (c + 1) * CHUNK) for c in range(cpb)]
        for rows in chunks:
            qz = qz_ref[rows, :]
            sq = _sigmoid(qz)
            q_ref[rows, :] = qz * sq
            dsilu_ref[rows, :] = sq * (1.0 + qz * (1.0 - sq))
            s, sn, f = _hgrn_gates(fz_ref[rows, :], lb)
            kk_ref[rows, :] = (1.0 - lb) * sn
            b_ref[rows, :] = jnp.log(f)
            sn_ref[rows, :] = sn
            gf_ref[rows, :] = sn / f
            gs_ref[rows, :] = (1.0 - lb) * s
            v_ref[rows, :] = iv_ref[rows, :].astype(BF16)
            ov = o_ref[rows, :]
            r = lax.rsqrt(jnp.mean(ov * ov, axis=-1, keepdims=True) + RMS_EPS)
            on = ov * r
            gz = gz_ref[rows, :]
            sg = _sigmoid(gz)
            dog = dog_ref[rows, :]
            dgz_ref[rows, :] = (dog * (on * ng) * (sg * (1.0 + gz * (1.0 - sg)))).astype(BF16)
            d_ong = dog * (gz * sg)
            sums_ref[0:1, :] += _colsum(d_ong * on)
            d_on = d_ong * ng
            do_ref[rows, :] = (r * (d_on - on * jnp.mean(d_on * on, axis=-1, keepdims=True))).astype(BF16)
        for rows in _groups(tm):
            b_ref[rows, :] = _tri_matmul(tril_ref, b_ref[rows, :])
        for c, rows in enumerate(chunks):
            b = b_ref[rows, :]
            bref = b[half - 1:half, :]
            blast = b[CHUNK - 1:CHUNK, :]
            q = q_ref[rows, :]
            kk = kk_ref[rows, :]
            eb = jnp.exp(b)
            ebr = jnp.exp(b - bref)
            ekr = jnp.exp(bref - b)
            ebl = jnp.exp(blast - b)
            eb_ref[rows, :] = eb
            ebr_ref[rows, :] = ebr
            ekr_ref[rows, :] = ekr
            ebl_ref[rows, :] = ebl
            qb_ref[rows, :] = (q * eb).astype(BF16)
            qe_ref[rows, :] = (q * ebr).astype(BF16)
            ke_ref[rows, :] = (kk * ekr).astype(BF16)
            kl_ref[rows, :] = (kk * ebl).astype(BF16)
            decay_ref[c:c + 1, :] = jnp.exp(blast)
        causal = _tri(True)
        for c, rows in enumerate(chunks):
            upd_ref[c] = _dot(do_ref[rows, :], qb_ref[rows, :], TN)
            a_ref[c] = jnp.where(causal, _dot(qe_ref[rows, :], ke_ref[rows, :], NT), 0.0).astype(BF16)
            da_ref[c] = jnp.where(causal, _dot(do_ref[rows, :], v_ref[rows, :], NT), 0.0).astype(BF16)
        dstate = dstate_ref[...]
        for c in reversed(range(cpb)):
            dst_ref[c] = dstate.astype(BF16)
            decay = decay_ref[c:c + 1, :]
            through_ref[c:c + 1, :] = decay * _colsum(dstate * st_ref[c].astype(F32))
            dstate = dstate * decay + upd_ref[c]
        dstate_ref[...] = dstate
        for c, rows in enumerate(chunks):
            div_ref[rows, :] = (_dot(a_ref[c], do_ref[rows, :], TN)
                                + _dot(kl_ref[rows, :], dst_ref[c], NT)).astype(BF16)
        for c, rows in enumerate(chunks):
            dqe = _dot(da_ref[c], ke_ref[rows, :], NN)
            dq_inter = _dot(do_ref[rows, :], st_ref[c], NN) * eb_ref[rows, :]
            dqz_ref[rows, :] = ((dqe * ebr_ref[rows, :] + dq_inter) * dsilu_ref[rows, :]).astype(BF16)
            rev_ref[rows, :] = qe_ref[rows, :].astype(F32) * dqe + q_ref[rows, :] * dq_inter
        for c, rows in enumerate(chunks):
            dke = _dot(da_ref[c], qe_ref[rows, :], TN)
            dk_inter = _dot(v_ref[rows, :], dst_ref[c], NN) * ebl_ref[rows, :]
            dk_ref[rows, :] = dke * ekr_ref[rows, :] + dk_inter
            rev_ref[rows, :] -= ke_ref[rows, :].astype(F32) * dke
            pre_ref[rows, :] = kk_ref[rows, :] * dk_inter
        for rows in _groups(tm):
            pre = pre_ref[rows, :]
            rev_ref[rows, :] = _tri_matmul(triu_ref, rev_ref[rows, :]) + (_tri_matmul(tril_ref, pre) - pre)
        for c, rows in enumerate(chunks):
            dlf = rev_ref[rows, :] + through_ref[c:c + 1, :]
            common = gf_ref[rows, :] * dlf - sn_ref[rows, :] * dk_ref[rows, :]
            dfz_ref[rows, :] = (gs_ref[rows, :] * common).astype(BF16)
            sums_ref[1:2, :] += _colsum(common)

        @pl.when(j == nt - 1)
        def _():
            sums_ref[1:2, :] = sums_ref[1:2, :] * lb * (1.0 - lb)

    rev = lambda base: pl.BlockSpec((tm, HEAD_DIM), lambda h, j: (nt - 1 - j, base + h))
    vec = lambda n: pl.BlockSpec((n, HEAD_DIM), lambda h, j: (0, h))
    const = pl.BlockSpec((_tri_rows(tm), _tri_rows(tm)), lambda h, j: (0, 0))
    bf = jax.ShapeDtypeStruct((t, HGRN_DIM), BF16)
    tile_bf = pltpu.VMEM((tm, HEAD_DIM), BF16)
    tile_f32 = pltpu.VMEM((tm, HEAD_DIM), F32)
    square = lambda dtype: pltpu.VMEM((cpb, HEAD_DIM, HEAD_DIM), dtype)
    rows8 = pltpu.VMEM((max(cpb, 8), HEAD_DIM), F32)
    return _call(
        body, name="hgrn_bwd", grid=(HEADS, nt),
        in_specs=[rev(0), rev(0), rev(8), rev(16), rev(24), rev(32),
                  pl.BlockSpec((None, cpb, HEAD_DIM, HEAD_DIM), lambda h, j: (h, nt - 1 - j, 0, 0)),
                  vec(2), vec(1), const, const],
        out_specs=[rev(0), rev(0), rev(0), rev(0), vec(8)],
        out_shape=[bf, bf, bf, bf, jax.ShapeDtypeStruct((8, HGRN_DIM), F32)],
        scratch_shapes=[pltpu.VMEM((HEAD_DIM, HEAD_DIM), F32)] + [tile_bf] * 6 + [square(F32), square(BF16)]
        + [pltpu.VMEM((cpb, CHUNK, CHUNK), BF16)] * 2 + [rows8, rows8] + [tile_f32] * 14,
        semantics=("parallel", "arbitrary"),
        operands=(d_og, o, proj, proj, proj, proj, states, logits, norm_g, _block_tri(_tri_rows(tm)), _block_tri(_tri_rows(tm), lower=False)),
        hosted=hosted)


def _mix_ln1(proj, y_conv, y_hgrn, w_out, x, g, b):
    t = x.shape[0]
    tm = _pick(t, 512)
    d = D_MODEL

    def lhs(ins, outs):
        for rows in _row_blocks(tm):
            outs[0][rows, :] = (_sigmoid(ins[0][rows, :]) * ins[3][rows, :]
                                + _sigmoid(ins[2][rows, :]) * ins[4][rows, :]).astype(BF16)
        return outs[0][...]

    def epilogue(acc, ins, outs, scr):
        for rows in _row_blocks(tm):
            r = ALPHA * ins[5][rows, :] + acc[rows, :]
            outs[1][rows, :] = r
            xhat, _ = _ln(r)
            outs[2][rows, :] = (xhat * ins[6][...] + ins[7][...]).astype(BF16)

    tile = pl.BlockSpec((tm, d), lambda i, j, k: (i, 0))
    vec = pl.BlockSpec((1, d), lambda i, j, k: (0, 0))
    return _mm_fused(
        "mix_ln1", (t // tm, 1, 1), (proj, w_out, proj, y_conv, y_hgrn, x, g, b),
        [pl.BlockSpec((tm, d), lambda i, j, k: (i, 5)), pl.BlockSpec((d, d), lambda i, j, k: (0, 0)),
         pl.BlockSpec((tm, d), lambda i, j, k: (i, 6)), tile, tile, tile, vec, vec],
        [jax.ShapeDtypeStruct((t, d), BF16), jax.ShapeDtypeStruct((t, d), F32), jax.ShapeDtypeStruct((t, d), BF16)],
        [tile, tile, tile], NN, (tm, d), epilogue, lhs=lhs)


def _d_mixed_merge_bwd(d_r1b, w_out, proj, y_conv, y_hgrn):
    t = proj.shape[0]
    tm = _pick(t, 512)
    d = D_MODEL

    def epilogue(d_mixed, ins, outs, scr):
        dy_ref, dmz_ref = outs
        for rows in _row_blocks(tm):
            dm = d_mixed[rows, :]
            for br in range(2):
                sg = _sigmoid(ins[2 + br][rows, :])
                dy_ref[br, rows, :] = (sg * dm).astype(BF16)
                dmz_ref[rows, br * d:(br + 1) * d] = (dm * ins[4 + br][rows, :] * sg * (1.0 - sg)).astype(BF16)

    tile = pl.BlockSpec((tm, d), lambda i, j, k: (i, 0))
    return _mm_fused(
        "d_mixed_merge_bwd", (t // tm, 1, 1), (d_r1b, w_out, proj, proj, y_conv, y_hgrn),
        [tile, pl.BlockSpec((d, d), lambda i, j, k: (0, 0)), pl.BlockSpec((tm, d), lambda i, j, k: (i, 5)),
         pl.BlockSpec((tm, d), lambda i, j, k: (i, 6)), tile, tile],
        [jax.ShapeDtypeStruct((2, t, d), BF16), jax.ShapeDtypeStruct((t, 2 * d), BF16)],
        [pl.BlockSpec((2, tm, d), lambda i, j, k: (0, i, 0)), pl.BlockSpec((tm, 2 * d), lambda i, j, k: (i, 0))],
        NT, (tm, d), epilogue)


def _ffn_out_ln2(act, w_ffn_out, r1, target, g1, b1, g2, b2):
    t = r1.shape[0]
    tm = _pick(t, 1024)
    nt = t // tm
    d = D_MODEL

    def epilogue(y_ffn, ins, outs, scr):
        r1_ref, tg_ref, g1_ref, b1_ref, g2_ref, b2_ref = ins[2:]
        dr_ref, drb_ref, sums_ref = outs
        (sq_ref,) = scr
        i = pl.program_id(0)

        @pl.when(i == 0)
        def _():
            sums_ref[...] = jnp.zeros_like(sums_ref)
            sq_ref[...] = jnp.zeros_like(sq_ref)

        for rows in _row_blocks(tm):
            xh1, _ = _ln(r1_ref[rows, :])
            x1 = xh1 * g1_ref[...] + b1_ref[...]
            xh2, rstd2 = _ln(ALPHA * x1 + y_ffn[rows, :])
            diff = xh2 * g2_ref[...] + b2_ref[...] - tg_ref[rows, :]
            dy = diff * (1.0 / D_MODEL)
            dr = _ln_bwd(dy, xh2, rstd2, g2_ref[...])
            dr_ref[rows, :] = dr
            drb_ref[rows, :] = dr.astype(BF16)
            sums_ref[0:1, :] += _colsum(dy * xh2)
            sums_ref[1:2, :] += _colsum(dy)
            sq_ref[...] += _colsum(diff * diff)

        @pl.when(i == nt - 1)
        def _():
            total = jnp.sum(sq_ref[...], axis=-1, keepdims=True) * (0.5 / D_MODEL)
            sums_ref[2:3, :] = jnp.broadcast_to(total, (1, D_MODEL))

    tile = pl.BlockSpec((tm, d), lambda i, j, k: (i, 0))
    vec = pl.BlockSpec((1, d), lambda i, j, k: (0, 0))
    return _mm_fused(
        "ffn_out_ln2", (nt, 1, 4), (act, w_ffn_out, r1, target, g1, b1, g2, b2),
        [pl.BlockSpec((None, tm, FF_SHARD), lambda i, j, k: (k, i, 0)),
         pl.BlockSpec((None, FF_SHARD, d), lambda i, j, k: (k, 0, 0)), tile, tile, vec, vec, vec, vec],
        [jax.ShapeDtypeStruct((t, d), F32), jax.ShapeDtypeStruct((t, d), BF16), jax.ShapeDtypeStruct((8, d), F32)],
        [tile, tile, pl.BlockSpec((8, d), lambda i, j, k: (0, 0))], NN, (tm, d), epilogue,
        scratch=[pltpu.VMEM((1, d), F32)])


def _d_x1_ln1_bwd(d_z, w_ffn_in, d_r2, r1, g1):
    t = r1.shape[0]
    tm = _pick(t, 1024)
    d = D_MODEL

    def epilogue(dx_ffn, ins, outs, scr):
        dr2_ref, r1_ref, g_ref = ins[2:]
        dr1_ref, dr1b_ref, sums_ref = outs
        i = pl.program_id(0)

        @pl.when(i == 0)
        def _():
            sums_ref[...] = jnp.zeros_like(sums_ref)

        for rows in _row_blocks(tm):
            xhat, rstd = _ln(r1_ref[rows, :])
            dx1 = ALPHA * dr2_ref[rows, :] + dx_ffn[rows, :]
            dr1 = _ln_bwd(dx1, xhat, rstd, g_ref[...])
            dr1_ref[rows, :] = dr1
            dr1b_ref[rows, :] = dr1.astype(BF16)
            sums_ref[0:1, :] += _colsum(dx1 * xhat)
            sums_ref[1:2, :] += _colsum(dx1)

    tile = pl.BlockSpec((tm, d), lambda i, j, k: (i, 0))
    return _mm_fused(
        "d_x1_ln1_bwd", (t // tm, 1, N_DEV), (d_z, w_ffn_in, d_r2, r1, g1),
        [pl.BlockSpec((None, tm, FF_SHARD), lambda i, j, k: (k, i, 0)),
         pl.BlockSpec((None, d, FF_SHARD), lambda i, j, k: (k, 0, 0)), tile, tile,
         pl.BlockSpec((1, d), lambda i, j, k: (0, 0))],
        [jax.ShapeDtypeStruct((t, d), F32), jax.ShapeDtypeStruct((t, d), BF16), jax.ShapeDtypeStruct((8, d), F32)],
        [tile, tile, pl.BlockSpec((8, d), lambda i, j, k: (0, 0))], NT, (tm, d), epilogue)


def _cast_bf16(x):
    t = x.shape[0]
    tm = _pick(t, 512)

    def body(x_ref, o_ref):
        o_ref[...] = x_ref[...].astype(BF16)

    tile = pl.BlockSpec((tm, D_MODEL), lambda i: (i, 0))
    return pl.pallas_call(
        body, name="cast_x", grid=(t // tm,), in_specs=[tile], out_specs=tile,
        out_shape=jax.ShapeDtypeStruct((t, D_MODEL), BF16), compiler_params=_params(("parallel",)),
    )(x)


def _relayout(name, a, in_block, in_map, out_block, out_map, out_shape):
    def body(a_ref, o_ref):
        o_ref[...] = a_ref[...].astype(o_ref.dtype)

    return pl.pallas_call(
        body, name=name, grid=(N_DEV,), in_specs=[pl.BlockSpec(in_block, in_map)],
        out_specs=pl.BlockSpec(out_block, out_map), out_shape=out_shape, compiler_params=_params(("parallel",)),
    )(a)


_GELU_C = math.sqrt(2.0 / math.pi)


_GELU_CUBIC = 0.044715


def _gelu_parts(u):
    u2 = u * u
    th = jnp.tanh(u * (_GELU_C + (_GELU_C * _GELU_CUBIC) * u2))
    hu = 0.5 * u
    return th, hu + hu * th, u2, hu


BF16_ROWS = 16


def _ffn_act_fwd(z, w_dw, b_dw):
    t = z.shape[2]
    tm = _pick(t, 256)
    nh = tm // FFN_HALO

    def body(z_ref, zh_ref, w_ref, b_ref, act_ref, gd_ref, us_ref):
        i = pl.program_id(1)
        us_ref[0:FFN_HALO, :] = jnp.where(i == 0, 0.0, zh_ref[...])
        us_ref[FFN_HALO:FFN_HALO + tm, :] = z_ref[0]
        for r in range(tm // ROW_BLOCK):
            base = r * ROW_BLOCK
            rows = slice(base, base + ROW_BLOCK)
            for lanes in _lane_blocks(FF_SHARD):
                uc = b_ref[:, lanes]
                for k in range(FFN_K):
                    off = base + FFN_HALO - (FFN_K - 1) + k
                    uc = uc + w_ref[k:k + 1, lanes] * us_ref[off:off + ROW_BLOCK, lanes]
                th, gelu, u2, hu = _gelu_parts(uc)
                dgelu = (0.5 + 0.5 * th) + (hu - hu * th * th) * (_GELU_C + (3.0 * _GELU_C * _GELU_CUBIC) * u2)
                act_ref[rows, lanes] = (gelu * z_ref[1, rows, lanes]).astype(BF16)
                gd_ref[0, rows, lanes] = gelu.astype(BF16)
                gd_ref[1, rows, lanes] = dgelu.astype(BF16)

    return pl.pallas_call(
        body, name="ffn_act_fwd", grid=(4, t // tm),
        in_specs=[pl.BlockSpec((2, None, tm, FF_SHARD), lambda j, i: (0, j, i, 0)),
                  pl.BlockSpec((None, None, FFN_HALO, FF_SHARD), lambda j, i: (0, j, jnp.maximum(i * nh - 1, 0), 0)),
                  pl.BlockSpec((None, FFN_K, FF_SHARD), lambda j, i: (j, 0, 0)),
                  pl.BlockSpec((None, 1, FF_SHARD), lambda j, i: (j, 0, 0))],
        out_specs=[pl.BlockSpec((None, tm, FF_SHARD), lambda j, i: (j, i, 0)),
                   pl.BlockSpec((2, None, tm, FF_SHARD), lambda j, i: (0, j, i, 0))],
        out_shape=[jax.ShapeDtypeStruct((4, t, FF_SHARD), BF16), jax.ShapeDtypeStruct((2, 4, t, FF_SHARD), BF16)],
        scratch_shapes=[pltpu.VMEM((FFN_HALO + tm, FF_SHARD), F32)],
        compiler_params=_params(("parallel", "arbitrary")),
    )(z, z, w_dw, b_dw)


def _ffn_act_bwd(d_act, z, gd, w_dw):
    t = z.shape[2]
    tm = _pick(t, 256)
    nt = t // tm
    nh = tm // FFN_HALO
    last_h = t // FFN_HALO - 1
    pad = FFN_HALO - (FFN_K - 1)

    def fold(x):
        return functools.reduce(jnp.add, [x[r:r + SUBLANES, :] for r in range(0, x.shape[0], SUBLANES)])

    def body(da_ref, dah_ref, z_ref, zp_ref, gn_ref, gd_ref, gdn_ref, w_ref, dz_ref, sums_ref, us_ref, ds_ref,
             part_ref):
        i = pl.program_id(1)

        @pl.when(i == 0)
        def _():
            part_ref[...] = jnp.zeros_like(part_ref)

        us_ref[0:FFN_HALO, :] = jnp.where(i == 0, 0.0, zp_ref[...])
        us_ref[FFN_HALO:FFN_HALO + tm, :] = z_ref[0]
        for r in range(tm // ROW_BLOCK):
            base = r * ROW_BLOCK
            rows = slice(base, base + ROW_BLOCK)
            for lanes in _lane_blocks(FF_SHARD):
                da = da_ref[rows, lanes]
                dz_ref[1, rows, lanes] = (da * gd_ref[0, rows, lanes].astype(F32)).astype(BF16)
                duc = da * z_ref[1, rows, lanes] * gd_ref[1, rows, lanes].astype(F32)
                ds_ref[rows, lanes] = duc
                for k in range(FFN_K):
                    part_ref[k, :, lanes] += fold(duc * us_ref[base + pad + k:base + pad + k + ROW_BLOCK, lanes])
                part_ref[FFN_K, :, lanes] += fold(duc)
        duc_next = dah_ref[...] * gn_ref[...] * gdn_ref[0:FFN_HALO, :].astype(F32)
        ds_ref[tm:tm + FFN_HALO, :] = jnp.where(i == nt - 1, 0.0, duc_next)
        for r in range(tm // ROW_BLOCK):
            base = r * ROW_BLOCK
            for lanes in _lane_blocks(FF_SHARD):
                du = None
                for k in range(FFN_K):
                    off = base + FFN_K - 1 - k
                    term = w_ref[k:k + 1, lanes] * ds_ref[off:off + ROW_BLOCK, lanes]
                    du = term if du is None else du + term
                dz_ref[0, base:base + ROW_BLOCK, lanes] = du.astype(BF16)

        @pl.when(i == nt - 1)
        def _():
            sums_ref[...] = jnp.zeros_like(sums_ref)
            for k in range(FFN_K + 1):
                sums_ref[k:k + 1, :] = _colsum(part_ref[k])

    nxt = lambda i: jnp.minimum((i + 1) * nh, last_h)
    nxt_bf = lambda i: jnp.minimum((i + 1) * (tm // BF16_ROWS), t // BF16_ROWS - 1)
    return pl.pallas_call(
        body, name="ffn_act_bwd", grid=(4, nt),
        in_specs=[pl.BlockSpec((None, tm, FF_SHARD), lambda j, i: (j, i, 0)),
                  pl.BlockSpec((None, FFN_HALO, FF_SHARD), lambda j, i: (j, nxt(i), 0)),
                  pl.BlockSpec((2, None, tm, FF_SHARD), lambda j, i: (0, j, i, 0)),
                  pl.BlockSpec((None, None, FFN_HALO, FF_SHARD), lambda j, i: (0, j, jnp.maximum(i * nh - 1, 0), 0)),
                  pl.BlockSpec((None, None, FFN_HALO, FF_SHARD), lambda j, i: (1, j, nxt(i), 0)),
                  pl.BlockSpec((2, None, tm, FF_SHARD), lambda j, i: (0, j, i, 0)),
                  pl.BlockSpec((None, None, BF16_ROWS, FF_SHARD), lambda j, i: (1, j, nxt_bf(i), 0)),
                  pl.BlockSpec((None, FFN_K, FF_SHARD), lambda j, i: (j, 0, 0))],
        out_specs=[pl.BlockSpec((2, None, tm, FF_SHARD), lambda j, i: (0, j, i, 0)),
                   pl.BlockSpec((None, 8, FF_SHARD), lambda j, i: (j, 0, 0))],
        out_shape=[jax.ShapeDtypeStruct((2, 4, t, FF_SHARD), BF16), jax.ShapeDtypeStruct((4, 8, FF_SHARD), F32)],
        scratch_shapes=[pltpu.VMEM((FFN_HALO + tm, FF_SHARD), F32), pltpu.VMEM((tm + FFN_HALO, FF_SHARD), F32),
                        pltpu.VMEM((FFN_K + 1, SUBLANES, FF_SHARD), F32)],
        compiler_params=_params(("parallel", "arbitrary")),
    )(d_act, d_act, z, z, z, gd, gd, w_dw)


def _local_step(x, target, w_in, rest, small, scatter=None):
    t = x.shape[0]
    tm = _pick(t, 2048)
    tk = _pick(t, 2048)
    nm = t // tm
    nk = t // tk
    d = D_MODEL

    xb = _cast_bf16(x)
    proj = _mm("proj", xb, w_in, (t, IN_COLS), F32, (nm, N_DEV, 1),
               pl.BlockSpec((tm, d), lambda i, j, k: (i, 0)),
               pl.BlockSpec((None, d, IN_SHARD), lambda i, j, k: (j, 0, 0)),
               pl.BlockSpec((tm, IN_SHARD), lambda i, j, k: (i, j)), NN, (tm, IN_SHARD),
               hosted=rest if isinstance(rest, _Hosted) else None)
    if isinstance(rest, _Hosted):
        proj, rest = proj
    w_conv_out8, w_hgrn_out8, w_out8, w_ffn_in, w_ffn_out8, conv_dw8, ffn_dw8 = rest
    w_conv_out = _relayout("w_conv_out_natural", w_conv_out8, (None, CONV_DIM, 128), lambda j: (j, 0, 0),
                           (CONV_DIM, 128), lambda j: (0, j), jax.ShapeDtypeStruct((CONV_DIM, d), BF16))
    w_hgrn_out = w_hgrn_out8.reshape(d, d)
    w_out = w_out8.reshape(d, d)
    w_ffn_out = w_ffn_out8.reshape(4, FF_SHARD, d)
    conv_dw = jnp.transpose(conv_dw8[:, :CONV_K, :CONV_DIM // N_DEV], (1, 0, 2)).reshape(CONV_K, CONV_DIM)
    ffn_dw = jnp.transpose(ffn_dw8[:, :FFN_K, :D_FF // N_DEV], (1, 0, 2)).reshape(FFN_K, 4, FF_SHARD)
    small = dict(small, w_conv_dw=conv_dw, w_ffn_dw=jnp.transpose(ffn_dw, (1, 0, 2)),
                 b_ffn_dw=small["b_ffn_dw"].reshape(4, 1, FF_SHARD))

    c_act, conv_pre = _conv_fwd(proj, small["w_conv_dw"], small["b_conv_dw"], small["conv_ln_g"], small["conv_ln_b"])
    y_conv = _mm("y_conv", c_act, w_conv_out, (t, d), F32, (nm, 1, 1),
                 pl.BlockSpec((tm, CONV_DIM), lambda i, j, k: (i, 0)),
                 pl.BlockSpec((CONV_DIM, d), lambda i, j, k: (0, 0)),
                 pl.BlockSpec((tm, d), lambda i, j, k: (i, 0)), NN, (tm, d))
    o, og, states = _hgrn_fwd(proj, small["hgrn_lb_logits"], small["hgrn_norm_g"])
    sq_w = pl.BlockSpec((d, d), lambda i, j, k: (0, 0))
    row_tile = pl.BlockSpec((tm, d), lambda i, j, k: (i, 0))
    y_hgrn = _mm("y_hgrn", og, w_hgrn_out, (t, d), F32, (nm, 1, 1), row_tile, sq_w, row_tile, NN, (tm, d))
    mixed, r1, x1b = _mix_ln1(proj, y_conv, y_hgrn, w_out, x, small["ln1_g"], small["ln1_b"])
    z = _mm("ffn_in", x1b, w_ffn_in, (N_DEV, t, FF_SHARD), F32, (nm, N_DEV, 1), row_tile,
            pl.BlockSpec((None, d, FF_SHARD), lambda i, j, k: (j, 0, 0)),
            pl.BlockSpec((None, tm, FF_SHARD), lambda i, j, k: (j, i, 0)), NN, (tm, FF_SHARD))
    z = z.reshape(2, 4, t, FF_SHARD)
    act, gelu_and_slope = _ffn_act_fwd(z, small["w_ffn_dw"], small["b_ffn_dw"])

    d_r2, d_r2b, sums_ln2 = _ffn_out_ln2(act, w_ffn_out, r1, target, small["ln1_g"], small["ln1_b"],
                                         small["ln2_g"], small["ln2_b"])
    d_act = _mm("d_act", d_r2b, w_ffn_out, (4, t, FF_SHARD), F32, (nm, 4, 1), row_tile,
                pl.BlockSpec((None, FF_SHARD, d), lambda i, j, k: (j, 0, 0)),
                pl.BlockSpec((None, tm, FF_SHARD), lambda i, j, k: (j, i, 0)), NT, (tm, FF_SHARD))
    g_w_ffn_out = _mm("g_w_ffn_out", act, d_r2b, (4, FF_SHARD, d), BF16, (4, 1, nk),
                      pl.BlockSpec((None, tk, FF_SHARD), lambda i, j, k: (i, k, 0)),
                      pl.BlockSpec((tk, d), lambda i, j, k: (k, 0)),
                      pl.BlockSpec((None, FF_SHARD, d), lambda i, j, k: (i, 0, 0)), TN, (FF_SHARD, d))
    d_z, sums_ffn = _ffn_act_bwd(d_act, z, gelu_and_slope, small["w_ffn_dw"])
    d_z8 = d_z.reshape(N_DEV, t, FF_SHARD)
    d_r1, d_r1b, sums_ln1 = _d_x1_ln1_bwd(d_z8, w_ffn_in, d_r2, r1, small["ln1_g"])
    g_w_ffn_in = _mm("g_w_ffn_in", x1b, d_z8, (N_DEV, d, FF_SHARD), BF16, (N_DEV, 1, nk),
                     pl.BlockSpec((tk, d), lambda i, j, k: (k, 0)),
                     pl.BlockSpec((None, tk, FF_SHARD), lambda i, j, k: (i, k, 0)),
                     pl.BlockSpec((None, d, FF_SHARD), lambda i, j, k: (i, 0, 0)), TN, (d, FF_SHARD))
    k_tile = pl.BlockSpec((tk, d), lambda i, j, k: (k, 0))
    g_w_out = _mm("g_w_out", mixed, d_r1b, (d, d), BF16, (1, 1, nk), k_tile, k_tile, sq_w, TN, (d, d))
    d_y, d_mz = _d_mixed_merge_bwd(d_r1b, w_out, proj, y_conv, y_hgrn)
    d_pre, sums_conv = _d_c_norm_bwd(d_y, w_conv_out, conv_pre, small["conv_ln_g"], small["conv_ln_b"])
    g_w_conv_out = _mm("g_w_conv_out", c_act, d_y, (CONV_DIM, d), BF16, (1, 1, nk),
                       pl.BlockSpec((tk, CONV_DIM), lambda i, j, k: (k, 0)),
                       pl.BlockSpec((None, tk, d), lambda i, j, k: (0, k, 0)),
                       pl.BlockSpec((CONV_DIM, d), lambda i, j, k: (0, 0)), TN, (CONV_DIM, d))
    g_w_conv_out = _relayout("g_w_conv_out_shards", g_w_conv_out, (CONV_DIM, 128), lambda j: (0, j),
                             (None, CONV_DIM, 128), lambda j: (j, 0, 0),
                             jax.ShapeDtypeStruct((N_DEV, CONV_DIM, 128), BF16))
    d_og = _mm("d_og", d_y, w_hgrn_out, (t, d), F32, (nm, 1, 1),
               pl.BlockSpec((None, tm, d), lambda i, j, k: (1, i, 0)), sq_w, row_tile, NT, (tm, d))
    g_w_hgrn_out = _mm("g_w_hgrn_out", og, d_y, (d, d), BF16, (1, 1, nk), k_tile,
                       pl.BlockSpec((None, tk, d), lambda i, j, k: (1, k, 0)), sq_w, TN, (d, d))
    d_cproj, g_w_conv_dw = _conv_bwd_dw(d_pre, proj, small["w_conv_dw"])

    early = [g_w_conv_out, g_w_hgrn_out.reshape(N_DEV, d // N_DEV, d), g_w_out.reshape(N_DEV, d // N_DEV, d),
             g_w_ffn_in, g_w_ffn_out.reshape(N_DEV, D_FF // N_DEV, d)]
    hgrn_out = _hgrn_bwd(d_og, o, proj, states, small["hgrn_lb_logits"], small["hgrn_norm_g"],
                         hosted=None if scatter is None else scatter(early))
    d_qz, d_fz, d_iv, d_gz, sums_hgrn = hgrn_out[:5]
    early_recv = list(hgrn_out[5:])
    d_proj = jnp.concatenate([d_cproj, d_qz, d_fz, d_iv, d_gz, d_mz], axis=1)
    g_w_in = _mm("g_w_in", xb, d_proj, (N_DEV, d, IN_SHARD), BF16, (N_DEV, 1, nk), k_tile,
                 pl.BlockSpec((tk, IN_SHARD), lambda i, j, k: (k, i)),
                 pl.BlockSpec((None, d, IN_SHARD), lambda i, j, k: (i, 0, 0)), TN, (d, IN_SHARD))
    def add_residual(acc, ins, outs, scr):
        for rows in _row_blocks(ta):
            outs[0][rows, :] = ALPHA * ins[2][rows, :] + acc[rows, :]

    ta = _pick(t, 1024)
    acc_tile = pl.BlockSpec((ta, d), lambda i, j, k: (i, 0))
    grad_x, *late_recv = _mm_fused(
        "grad_x", (t // ta, 1, N_DEV), (d_proj, w_in, d_r1),
        [pl.BlockSpec((ta, IN_SHARD), lambda i, j, k: (i, k)),
         pl.BlockSpec((None, d, IN_SHARD), lambda i, j, k: (k, 0, 0)), acc_tile],
        [jax.ShapeDtypeStruct((t, d), F32)], [acc_tile], NT, (ta, d), add_residual,
        hosted=None if scatter is None else scatter([g_w_in]))

    d_l0 = sums_hgrn[1:2]
    small_grads = {
        "loss": sums_ln2[2:3, 0:128],
        "b_conv_dw": sums_conv[2:3], "conv_ln_g": sums_conv[0:1], "conv_ln_b": sums_conv[1:2],
        "hgrn_lb_logits": jnp.concatenate([d_l0, -d_l0], axis=1),
        "hgrn_norm_g": sums_hgrn[0:1],
        "ln1_g": sums_ln1[0:1], "ln1_b": sums_ln1[1:2],
        "b_ffn_dw": sums_ffn[:, FFN_K, :].reshape(1, D_FF),
        "ln2_g": sums_ln2[0:1], "ln2_b": sums_ln2[1:2],
        "w_conv_dw": g_w_conv_dw[0:CONV_K].reshape(1, CONV_K * CONV_DIM),
        "w_ffn_dw": jnp.transpose(sums_ffn[:, 0:FFN_K, :], (1, 0, 2)).reshape(1, FFN_K * D_FF),
    }
    large_grads = [g_w_in] + early
    if scatter is not None:
        large_grads = list(zip(large_grads, late_recv + early_recv))
    return grad_x, large_grads, small_grads


def _coords():
    return lax.axis_index("x"), lax.axis_index("y"), lax.axis_index("c")


def _gather(shards):
    n = len(shards)

    def parts(ins, outs, sems):
        send_sems, recv_sems, local_sems = sems
        x, y, c = _coords()
        me = 4 * x + 2 * y + c
        sibling = (x, y, 1 - c)
        chips = [(1 - x, y), (x, 1 - y), (1 - x, 1 - y)]

        def copy(a, k, block, to, src=None):
            return pltpu.make_async_remote_copy(
                src_ref=outs[a].at[block] if src is None else src, dst_ref=outs[a].at[block],
                send_sem=send_sems.at[a, k], recv_sem=recv_sems.at[a, k], device_id=to, device_id_type=MESH)

        local = [pltpu.make_async_copy(ins[a], outs[a].at[me], local_sems.at[a]) for a in range(n)]
        first = []
        for a in range(n):
            first.append(copy(a, 0, me, sibling, src=ins[a]))
            for j, chip in enumerate(chips):
                first.append(copy(a, 1 + j, me, (*chip, c), src=ins[a]))
        return x, y, c, sibling, chips, copy, local, first

    def start(ins, outs, sems):
        *_, local, first = parts(ins, outs, sems)
        for cp in local + first:
            cp.start()

    def finish(ins, outs, sems):
        x, y, c, sibling, chips, copy, local, first = parts(ins, outs, sems)
        passed = []
        for j, (px, py) in enumerate(chips):
            for a in range(n):
                copy(a, 1 + j, 4 * px + 2 * py + c, sibling).wait_recv()
                cp = copy(a, 4 + j, 4 * px + 2 * py + c, sibling)
                cp.start()
                passed.append(cp)
        for a in range(n):
            copy(a, 0, 4 * x + 2 * y + 1 - c, sibling).wait_recv()
            for j, (px, py) in enumerate(chips):
                copy(a, 4 + j, 4 * px + 2 * py + 1 - c, sibling).wait_recv()
        for cp in first + passed:
            cp.wait_send()
        for cp in local:
            cp.wait()

    return _Hosted(shards, [jax.ShapeDtypeStruct((N_DEV,) + s.shape, s.dtype) for s in shards],
                   [pltpu.SemaphoreType.DMA((n, 7)), pltpu.SemaphoreType.DMA((n, 7)), pltpu.SemaphoreType.DMA((n,))],
                   start, finish)


def _scatter(grads):
    n = len(grads)

    def copies(ins, outs, sems):
        send_sems, recv_sems = sems
        x, y, c = _coords()
        out = []
        for a in range(n):
            for k in range(1, N_DEV):
                px, py, pc = x ^ (k >> 2), y ^ ((k >> 1) & 1), c ^ (k & 1)
                out.append(pltpu.make_async_remote_copy(
                    src_ref=ins[a].at[4 * px + 2 * py + pc], dst_ref=outs[a].at[k - 1],
                    send_sem=send_sems.at[a, k - 1], recv_sem=recv_sems.at[a, k - 1],
                    device_id=(px, py, pc), device_id_type=MESH))
        return out

    def start(ins, outs, sems):
        for cp in copies(ins, outs, sems):
            cp.start()

    def finish(ins, outs, sems):
        for cp in copies(ins, outs, sems):
            cp.wait()

    return _Hosted(grads, [jax.ShapeDtypeStruct((N_DEV - 1,) + g.shape[1:], g.dtype) for g in grads],
                   [pltpu.SemaphoreType.DMA((n, N_DEV - 1)), pltpu.SemaphoreType.DMA((n, N_DEV - 1))], start, finish)


def _row_tile(rows):
    return 256 if rows % 256 == 0 else rows


def _adam_math(w, g, m, v):
    m_new = ADAM_B1 * m + (1.0 - ADAM_B1) * g
    v_new = ADAM_B2 * v + (1.0 - ADAM_B2) * (g * g)
    m_hat = m_new / (1.0 - ADAM_B1 ** ADAM_STEP)
    v_hat = v_new / (1.0 - ADAM_B2 ** ADAM_STEP)
    delta = -ADAM_LR * (m_hat / (jnp.sqrt(v_hat) + ADAM_EPS) + ADAM_WD * w)
    return delta, m_new, v_new


def _adam_large(name, own, recv, me, w, m, v):
    rows, cols = w.shape
    tr = _row_tile(rows)

    def body(me_ref, p_ref, r_ref, w_ref, m_ref, v_ref, g_out, d_out, m_out, v_out):
        g = p_ref[...].astype(F32)
        for k in range(N_DEV - 1):
            g = g + r_ref[k].astype(F32)
        delta, m_new, v_new = _adam_math(w_ref[...], g, m_ref[...], v_ref[...])
        g_out[...] = g
        d_out[...] = delta
        m_out[...] = m_new
        v_out[...] = v_new

    tile = pl.BlockSpec((tr, cols), lambda r, me_ref: (r, 0))
    sds = jax.ShapeDtypeStruct((rows, cols), F32)
    return pl.pallas_call(
        body, name=name,
        grid_spec=pltpu.PrefetchScalarGridSpec(
            num_scalar_prefetch=1, grid=(rows // tr,),
            in_specs=[pl.BlockSpec((None, tr, cols), lambda r, me_ref: (me_ref[0], r, 0)),
                      pl.BlockSpec((N_DEV - 1, tr, cols), lambda r, me_ref: (0, r, 0)), tile, tile, tile],
            out_specs=[tile, tile, tile, tile]),
        out_shape=[sds, sds, sds, sds],
        compiler_params=_params(("parallel",)),
    )(me, own, recv, w, m, v)


def _small_allreduce(vec):
    rows = vec.shape[0]

    def body(v_ref, o_ref, gat_ref, send_sems, recv_sems):
        x, y, c = _coords()
        me = 4 * x + 2 * y + c
        gat_ref[me] = v_ref[...]
        copies = []
        for k in range(1, N_DEV):
            px, py, pc = x ^ (k >> 2), y ^ ((k >> 1) & 1), c ^ (k & 1)
            copies.append(pltpu.make_async_remote_copy(
                src_ref=v_ref, dst_ref=gat_ref.at[me], send_sem=send_sems.at[k - 1], recv_sem=recv_sems.at[k - 1],
                device_id=(px, py, pc), device_id_type=MESH))
        for cp in copies:
            cp.start()
        for k in range(1, N_DEV):
            px, py, pc = x ^ (k >> 2), y ^ ((k >> 1) & 1), c ^ (k & 1)
            pltpu.make_async_remote_copy(
                src_ref=v_ref, dst_ref=gat_ref.at[4 * px + 2 * py + pc], send_sem=send_sems.at[k - 1],
                recv_sem=recv_sems.at[k - 1], device_id=(px, py, pc), device_id_type=MESH).wait_recv()
        for cp in copies:
            cp.wait_send()
        acc = gat_ref[0]
        for dev in range(1, N_DEV):
            acc = acc + gat_ref[dev]
        o_ref[...] = acc

    whole = pl.BlockSpec(memory_space=pltpu.VMEM)
    return pl.pallas_call(
        body, name="small_allreduce", in_specs=[whole], out_specs=whole,
        out_shape=jax.ShapeDtypeStruct((rows, 128), F32),
        scratch_shapes=[pltpu.VMEM((N_DEV, rows, 128), F32), pltpu.SemaphoreType.DMA((N_DEV - 1,)),
                        pltpu.SemaphoreType.DMA((N_DEV - 1,))],
        compiler_params=pltpu.CompilerParams(has_side_effects=True, vmem_limit_bytes=VMEM_LIMIT),
    )(vec)


def _adam_small(w, g, m, v):
    def body(w_ref, g_ref, m_ref, v_ref, d_out, m_out, v_out):
        delta, m_new, v_new = _adam_math(w_ref[...], g_ref[...], m_ref[...], v_ref[...])
        d_out[...] = delta
        m_out[...] = m_new
        v_out[...] = v_new

    whole = pl.BlockSpec(memory_space=pltpu.VMEM)
    sds = jax.ShapeDtypeStruct(w.shape, F32)
    return pl.pallas_call(body, name="adam_small", in_specs=[whole] * 4, out_specs=[whole] * 3,
                          out_shape=[sds, sds, sds])(w, g, m, v)


_SMALL_ORDER = ["loss", "b_conv_dw", "conv_ln_g", "conv_ln_b", "hgrn_lb_logits", "hgrn_norm_g", "ln1_g", "ln1_b",
                "b_ffn_dw", "ln2_g", "ln2_b", "w_conv_dw", "w_ffn_dw"]
_WEIGHTS = ["w_in", "w_conv_dw", "b_conv_dw", "conv_ln_g", "conv_ln_b", "w_conv_out", "hgrn_lb_logits", "hgrn_norm_g",
            "w_hgrn_out", "w_out", "ln1_g", "ln1_b", "w_ffn_in", "w_ffn_dw", "b_ffn_dw", "w_ffn_out", "ln2_g", "ln2_b"]
_LARGE = ["w_in", "w_conv_out", "w_hgrn_out", "w_out", "w_ffn_in", "w_ffn_out"]
_CONV_DW_SHARD = CONV_DIM // N_DEV
_FFN_DW_SHARD = D_FF // N_DEV


def kernel(x, w_in, w_conv_dw, b_conv_dw, conv_ln_g, conv_ln_b, w_conv_out, hgrn_lb_logits, hgrn_norm_g, w_hgrn_out, w_out, ln1_g, ln1_b, w_ffn_in, w_ffn_dw, b_ffn_dw, w_ffn_out, ln2_g, ln2_b, loss_target, m_w_in, m_w_conv_dw, m_b_conv_dw, m_conv_ln_g, m_conv_ln_b, m_w_conv_out, m_hgrn_lb_logits, m_hgrn_norm_g, m_w_hgrn_out, m_w_out, m_ln1_g, m_ln1_b, m_w_ffn_in, m_w_ffn_dw, m_b_ffn_dw, m_w_ffn_out, m_ln2_g, m_ln2_b, v_w_in, v_w_conv_dw, v_b_conv_dw, v_conv_ln_g, v_conv_ln_b, v_w_conv_out, v_hgrn_lb_logits, v_hgrn_norm_g, v_w_hgrn_out, v_w_out, v_ln1_g, v_ln1_b, v_w_ffn_in, v_w_ffn_dw, v_b_ffn_dw, v_w_ffn_out, v_ln2_g, v_ln2_b):
    w = dict(w_in=w_in, w_conv_dw=w_conv_dw, b_conv_dw=b_conv_dw, conv_ln_g=conv_ln_g, conv_ln_b=conv_ln_b,
             w_conv_out=w_conv_out, hgrn_lb_logits=hgrn_lb_logits, hgrn_norm_g=hgrn_norm_g, w_hgrn_out=w_hgrn_out,
             w_out=w_out, ln1_g=ln1_g, ln1_b=ln1_b, w_ffn_in=w_ffn_in, w_ffn_dw=w_ffn_dw, b_ffn_dw=b_ffn_dw,
             w_ffn_out=w_ffn_out, ln2_g=ln2_g, ln2_b=ln2_b)
    m = dict(w_in=m_w_in, w_conv_dw=m_w_conv_dw, b_conv_dw=m_b_conv_dw, conv_ln_g=m_conv_ln_g, conv_ln_b=m_conv_ln_b,
             w_conv_out=m_w_conv_out, hgrn_lb_logits=m_hgrn_lb_logits, hgrn_norm_g=m_hgrn_norm_g,
             w_hgrn_out=m_w_hgrn_out, w_out=m_w_out, ln1_g=m_ln1_g, ln1_b=m_ln1_b, w_ffn_in=m_w_ffn_in,
             w_ffn_dw=m_w_ffn_dw, b_ffn_dw=m_b_ffn_dw, w_ffn_out=m_w_ffn_out, ln2_g=m_ln2_g, ln2_b=m_ln2_b)
    v = dict(w_in=v_w_in, w_conv_dw=v_w_conv_dw, b_conv_dw=v_b_conv_dw, conv_ln_g=v_conv_ln_g, conv_ln_b=v_conv_ln_b,
             w_conv_out=v_w_conv_out, hgrn_lb_logits=v_hgrn_lb_logits, hgrn_norm_g=v_hgrn_norm_g,
             w_hgrn_out=v_w_hgrn_out, w_out=v_w_out, ln1_g=v_ln1_g, ln1_b=v_ln1_b, w_ffn_in=v_w_ffn_in,
             w_ffn_dw=v_w_ffn_dw, b_ffn_dw=v_b_ffn_dw, w_ffn_out=v_w_ffn_out, ln2_g=v_ln2_g, ln2_b=v_ln2_b)
    xi, yi, ci = lax.axis_index("x"), lax.axis_index("y"), lax.axis_index("c")
    me = 4 * xi + 2 * yi + ci
    me_op = jnp.reshape(me, (1,)).astype(jnp.int32)

    shards = [w[name][0].astype(BF16) for name in _LARGE]
    shards.append(jnp.pad(w_conv_dw[0], ((0, 1), (0, 128 - _CONV_DW_SHARD))))
    shards.append(jnp.pad(w_ffn_dw[0], ((0, 8 - FFN_K), (0, 384 - _FFN_DW_SHARD))))
    (w_in_all,) = _run_hosted("all_gather_w_in", _gather(shards[:1]))
    small = dict(b_conv_dw=b_conv_dw, conv_ln_g=conv_ln_g, conv_ln_b=conv_ln_b, hgrn_lb_logits=hgrn_lb_logits,
                 hgrn_norm_g=hgrn_norm_g, ln1_g=ln1_g, ln1_b=ln1_b, ln2_g=ln2_g, ln2_b=ln2_b, b_ffn_dw=b_ffn_dw)

    grad_x, large_grads, small_grads = _local_step(x[0], loss_target[0], w_in_all, _gather(shards[1:]), small, _scatter)

    out = {}
    for name, (own, recv) in zip(_LARGE, large_grads):
        out[name] = _adam_large("adam_" + name, own, recv, me_op, w[name][0], m[name][0], v[name][0])

    vec = jnp.concatenate([small_grads[name] for name in _SMALL_ORDER], axis=1)
    total = _small_allreduce(vec.reshape(-1, 128)).reshape(1, -1)
    sizes = [small_grads[name].shape[1] for name in _SMALL_ORDER]
    offs = [0]
    for s in sizes:
        offs.append(offs[-1] + s)
    summed = {name: total[:, offs[i]:offs[i + 1]] for i, name in enumerate(_SMALL_ORDER)}
    loss = summed["loss"][0, 0]
    conv_dw_g = lax.dynamic_slice_in_dim(summed["w_conv_dw"].reshape(CONV_K, CONV_DIM), me * _CONV_DW_SHARD, _CONV_DW_SHARD, axis=1)
    ffn_dw_g = lax.dynamic_slice_in_dim(summed["w_ffn_dw"].reshape(FFN_K, D_FF), me * _FFN_DW_SHARD, _FFN_DW_SHARD, axis=1)
    small_g = dict(summed, w_conv_dw=conv_dw_g.reshape(1, -1), w_ffn_dw=ffn_dw_g.reshape(1, -1))
    names = [n for n in _SMALL_ORDER if n != "loss"]
    flat = lambda d, n: d[n].reshape(1, -1)
    n_small = sum(small_g[n].shape[1] for n in names)
    pad = (-n_small) % 1024
    pack = lambda pieces: jnp.pad(jnp.concatenate(pieces, axis=1), ((0, 0), (0, pad))).reshape(-1, 128)
    d_s, m_s, v_s = _adam_small(pack([flat(w, n) for n in names]), pack([small_g[n] for n in names]),
                                pack([flat(m, n) for n in names]), pack([flat(v, n) for n in names]))
    pos = 0
    for n in names:
        size = small_g[n].shape[1]
        cut = lambda a: a.reshape(1, -1)[:, pos:pos + size].reshape(w[n].shape)
        out[n] = (small_g[n].reshape(w[n].shape), cut(d_s), cut(m_s), cut(v_s))
        pos += size

    for name in _LARGE:
        out[name] = tuple(a.reshape(w[name].shape) for a in out[name])
    grads = [out[n][0] for n in _WEIGHTS]
    deltas = [out[n][1] for n in _WEIGHTS]
    new_m = [out[n][2] for n in _WEIGHTS]
    new_v = [out[n][3] for n in _WEIGHTS]
    return (loss, grad_x[None], *grads, *deltas, *new_m, *new_v)
```

```python
import functools
import math

import jax
import jax.numpy as jnp
from jax import lax
from jax.experimental import pallas as pl
from jax.experimental.pallas import tpu as pltpu

F32 = jnp.float32
BF16 = jnp.bfloat16

N_DEV = 8
D_MODEL = 1024
CONV_DIM = 512
CONV_K = 31
HGRN_DIM = 1024
HEADS = 8
HEAD_DIM = 128
D_FF = 2816
FFN_K = 3
FF_SHARD = 2 * D_FF // N_DEV
IN_COLS = 7168
IN_SHARD = IN_COLS // N_DEV
LN_EPS = 1e-5
RMS_EPS = 1e-6
ALPHA = 2.0 ** 0.25

ADAM_LR = 0.001
ADAM_B1 = 0.9
ADAM_B2 = 0.999
ADAM_EPS = 1e-08
ADAM_WD = 0.01
ADAM_STEP = 10

CHUNK = 64
CHUNKS_PER_BLOCK = 16
CONV_HALO = 32
FFN_HALO = 8
ROW_BLOCK = 64
SUBLANES = 8
VMEM_LIMIT = 48 * 1024 * 1024
MXU_DEPTH = 256

MESH = pl.DeviceIdType.MESH
ANY = pl.BlockSpec(memory_space=pl.ANY)

NN = (((1,), (0,)), ((), ()))
NT = (((1,), (1,)), ((), ()))
TN = (((0,), (0,)), ((), ()))


def _params(sem):
    return pltpu.CompilerParams(dimension_semantics=sem, vmem_limit_bytes=VMEM_LIMIT)


def _dot(a, b, dims):
    return lax.dot_general(a.astype(BF16), b.astype(BF16), dims, preferred_element_type=F32)


def _sigmoid(x):
    return jax.nn.sigmoid(x)


def _ln(r):
    mu = jnp.mean(r, axis=-1, keepdims=True)
    xc = r - mu
    var = jnp.mean(xc * xc, axis=-1, keepdims=True)
    rstd = lax.rsqrt(var + LN_EPS)
    return xc * rstd, rstd


def _ln_bwd(dy, xhat, rstd, g):
    dxh = dy * g
    m1 = jnp.mean(dxh, axis=-1, keepdims=True)
    m2 = jnp.mean(dxh * xhat, axis=-1, keepdims=True)
    return rstd * (dxh - m1 - xhat * m2)


def _colsum(x):
    return jnp.sum(x, axis=0, keepdims=True)


class _Hosted:
    def __init__(self, inputs, out_shapes, sem_shapes, start, finish):
        self.inputs, self.out_shapes, self.sem_shapes = list(inputs), list(out_shapes), list(sem_shapes)
        self.start, self.finish = start, finish


def _call(body, *, name, grid, in_specs, out_specs, out_shape, scratch_shapes, semantics, operands, hosted=None):
    if hosted is None:
        return pl.pallas_call(
            body, name=name, grid=grid, in_specs=list(in_specs), out_specs=list(out_specs), out_shape=list(out_shape),
            scratch_shapes=list(scratch_shapes), compiler_params=_params(semantics))(*operands)
    n_in, n_out, n_scr = len(in_specs), len(out_specs), len(scratch_shapes)
    h_in, h_out = len(hosted.inputs), len(hosted.out_shapes)

    def full_body(*refs):
        ins, refs = refs[:n_in], refs[n_in:]
        h_ins, refs = refs[:h_in], refs[h_in:]
        outs, refs = refs[:n_out], refs[n_out:]
        h_outs, refs = refs[:h_out], refs[h_out:]
        scr, sems = refs[:n_scr], refs[n_scr:]
        first = functools.reduce(jnp.logical_and, [pl.program_id(d) == 0 for d in range(len(grid))])
        last = functools.reduce(jnp.logical_and, [pl.program_id(d) == grid[d] - 1 for d in range(len(grid))])

        @pl.when(first)
        def _():
            hosted.start(h_ins, h_outs, sems)

        body(*ins, *outs, *scr)

        @pl.when(last)
        def _():
            hosted.finish(h_ins, h_outs, sems)

    return pl.pallas_call(
        full_body, name=name, grid=grid, in_specs=list(in_specs) + [ANY] * h_in,
        out_specs=list(out_specs) + [ANY] * h_out, out_shape=list(out_shape) + hosted.out_shapes,
        scratch_shapes=list(scratch_shapes) + hosted.sem_shapes,
        compiler_params=pltpu.CompilerParams(dimension_semantics=("arbitrary",) * len(grid),
                                             vmem_limit_bytes=VMEM_LIMIT, has_side_effects=True),
    )(*operands, *hosted.inputs)


def _mm(name, a, b, out_shape, out_dtype, grid, a_spec, b_spec, o_spec, dims, acc_shape, hosted=None):
    nk = grid[2]
    if nk == 1:
        def body(a_ref, b_ref, o_ref):
            o_ref[...] = _dot(a_ref[...], b_ref[...], dims).astype(o_ref.dtype)
        scratch = []
    else:
        def body(a_ref, b_ref, o_ref, acc_ref):
            k = pl.program_id(2)

            @pl.when(k == 0)
            def _():
                acc_ref[...] = jnp.zeros_like(acc_ref)

            acc_ref[...] += _dot(a_ref[...], b_ref[...], dims)

            @pl.when(k == nk - 1)
            def _():
                o_ref[...] = acc_ref[...].astype(o_ref.dtype)
        scratch = [pltpu.VMEM(acc_shape, F32)]

    outs = _call(body, name=name, grid=grid, in_specs=[a_spec, b_spec], out_specs=[o_spec],
                 out_shape=[jax.ShapeDtypeStruct(out_shape, out_dtype)], scratch_shapes=scratch,
                 semantics=("parallel", "parallel", "arbitrary"), operands=(a, b), hosted=hosted)
    return outs[0] if hosted is None else (outs[0], list(outs[1:]))


def _mm_fused(name, grid, operands, in_specs, out_shape, out_specs, dims, acc_shape, epilogue, lhs=None, scratch=(),
              hosted=None):
    nk = grid[2]
    n_in, n_out = len(in_specs), len(out_specs)

    def body(*refs):
        ins, outs, scr = refs[:n_in], refs[n_in:n_in + n_out], refs[n_in + n_out:]
        acc_ref, k = scr[0], pl.program_id(2)
        a = ins[0][...] if lhs is None else lhs(ins, outs)
        part = _dot(a, ins[1][...], dims)
        if nk == 1:
            acc_ref[...] = part
            epilogue(acc_ref, ins, outs, scr[1:])
            return

        @pl.when(k == 0)
        def _():
            acc_ref[...] = jnp.zeros_like(acc_ref)

        acc_ref[...] += part

        @pl.when(k == nk - 1)
        def _():
            epilogue(acc_ref, ins, outs, scr[1:])

    return _call(body, name=name, grid=grid, in_specs=in_specs, out_specs=out_specs, out_shape=out_shape,
                 scratch_shapes=[pltpu.VMEM(acc_shape, F32)] + list(scratch), semantics=("arbitrary",) * 3,
                 operands=operands, hosted=hosted)


def _row_blocks(rows, block=256):
    block = block if rows % block == 0 else rows
    return [slice(r, r + block) for r in range(0, rows, block)]


def _pick(t, pref):
    return pref if t % pref == 0 else t


def _glu(p):
    return p[:, :CONV_DIM] * _sigmoid(p[:, CONV_DIM:])


def _by_phase(taps):
    phases = {}
    for off, payload in taps:
        phases.setdefault(off % SUBLANES, []).append((off - off % SUBLANES, payload))
    return sorted(phases.items())


def _tap_sum(src_ref, base, taps, rows, lanes):
    acc = None
    for phase, items in _by_phase(taps):
        n = rows if phase == 0 else rows + SUBLANES
        part = None
        for off, (w_ref, k) in items:
            term = w_ref[k:k + 1, lanes] * src_ref[base + off:base + off + n, lanes]
            part = term if part is None else part + term
        if phase:
            part = part[phase:phase + rows, :]
        acc = part if acc is None else acc + part
    return acc


def _tap_products(x, src_ref, base, taps, lanes):
    rows, cols = x.shape
    pad = jnp.zeros((SUBLANES, cols), x.dtype)
    padded = jnp.concatenate([pad, x, pad], axis=0)
    out = []
    for phase, items in _by_phase(taps):
        n = rows if phase == 0 else rows + SUBLANES
        shifted = x if phase == 0 else padded[SUBLANES - phase:SUBLANES - phase + n, :]
        for off, key in items:
            out.append((key, _colsum(shifted * src_ref[base + off:base + off + n, lanes])))
    return out


def _lane_blocks(cols, block=256):
    return [slice(c, min(c + block, cols)) for c in range(0, cols, block)]


def _conv_fwd(proj, w_dw, b_dw, g, b):
    t = proj.shape[0]
    tm = _pick(t, 512)
    nh = tm // CONV_HALO

    def body(p_ref, ph_ref, w_ref, bd_ref, g_ref, b_ref, act_ref, pre_ref, xs_ref):
        i = pl.program_id(0)
        halo = _glu(ph_ref[...])
        xs_ref[0:CONV_HALO, :] = jnp.where(i == 0, 0.0, halo)
        xs_ref[CONV_HALO:CONV_HALO + tm, :] = _glu(p_ref[...])
        taps = [(CONV_HALO - (CONV_K - 1) + k, (w_ref, k)) for k in range(CONV_K)]
        for r in range(tm // ROW_BLOCK):
            rows = slice(r * ROW_BLOCK, (r + 1) * ROW_BLOCK)
            for lanes in _lane_blocks(CONV_DIM):
                pre_ref[rows, lanes] = bd_ref[:, lanes] + _tap_sum(xs_ref, r * ROW_BLOCK, taps, ROW_BLOCK, lanes)
            acc = pre_ref[rows, :]
            xhat, _ = _ln(acc)
            yln = xhat * g_ref[...] + b_ref[...]
            act_ref[rows, :] = (yln * _sigmoid(yln)).astype(BF16)

    full = lambda s: pl.BlockSpec(s, lambda i: (0, 0))
    return pl.pallas_call(
        body, name="conv_fwd", grid=(t // tm,),
        in_specs=[pl.BlockSpec((tm, 2 * CONV_DIM), lambda i: (i, 0)),
                  pl.BlockSpec((CONV_HALO, 2 * CONV_DIM), lambda i: (jnp.maximum(i * nh - 1, 0), 0)),
                  full((CONV_K, CONV_DIM)), full((1, CONV_DIM)), full((1, CONV_DIM)), full((1, CONV_DIM))],
        out_specs=[pl.BlockSpec((tm, CONV_DIM), lambda i: (i, 0)), pl.BlockSpec((tm, CONV_DIM), lambda i: (i, 0))],
        out_shape=[jax.ShapeDtypeStruct((t, CONV_DIM), BF16), jax.ShapeDtypeStruct((t, CONV_DIM), F32)],
        scratch_shapes=[pltpu.VMEM((CONV_HALO + tm, CONV_DIM), F32)],
        compiler_params=_params(("arbitrary",)),
    )(proj, proj, w_dw, b_dw, g, b)


def _d_c_norm_bwd(d_y, w_conv_out, pre, g, b):
    t = pre.shape[0]
    tm = _pick(t, 512)
    d = D_MODEL

    def epilogue(d_c, ins, outs, scr):
        pre_ref, g_ref, b_ref = ins[2:]
        dpre_ref, sums_ref = outs
        i = pl.program_id(0)

        @pl.when(i == 0)
        def _():
            sums_ref[...] = jnp.zeros_like(sums_ref)

        for rows in _row_blocks(tm):
            xhat, rstd = _ln(pre_ref[rows, :])
            yln = xhat * g_ref[...] + b_ref[...]
            sg = _sigmoid(yln)
            dyln = d_c[rows, :] * (sg * (1.0 + yln * (1.0 - sg)))
            dpre = _ln_bwd(dyln, xhat, rstd, g_ref[...])
            dpre_ref[rows, :] = dpre
            sums_ref[0:1, :] += _colsum(dyln * xhat)
            sums_ref[1:2, :] += _colsum(dyln)
            sums_ref[2:3, :] += _colsum(dpre)

    full = lambda s: pl.BlockSpec(s, lambda i, j, k: (0, 0))
    tile = pl.BlockSpec((tm, CONV_DIM), lambda i, j, k: (i, 0))
    return _mm_fused(
        "d_c_norm_bwd", (t // tm, 1, 1), (d_y, w_conv_out, pre, g, b),
        [pl.BlockSpec((None, tm, d), lambda i, j, k: (0, i, 0)), full((CONV_DIM, d)), tile,
         full((1, CONV_DIM)), full((1, CONV_DIM))],
        [jax.ShapeDtypeStruct((t, CONV_DIM), F32), jax.ShapeDtypeStruct((8, CONV_DIM), F32)],
        [tile, full((8, CONV_DIM))], NT, (tm, CONV_DIM), epilogue)


def _conv_bwd_dw(d_pre, proj, w_dw):
    t = d_pre.shape[0]
    tm = _pick(t, 512)
    nt = t // tm
    nh = tm // CONV_HALO
    last_h = t // CONV_HALO - 1

    def body(dp_ref, dph_ref, p_ref, ph_ref, w_ref, dproj_ref, dw_ref, xs_ref, ds_ref):
        i = pl.program_id(0)

        @pl.when(i == 0)
        def _():
            dw_ref[...] = jnp.zeros_like(dw_ref)

        halo = _glu(ph_ref[...])
        xs_ref[0:CONV_HALO, :] = jnp.where(i == 0, 0.0, halo)
        xs_ref[CONV_HALO:CONV_HALO + tm, :] = _glu(p_ref[...])
        ds_ref[0:tm, :] = dp_ref[...]
        ds_ref[tm:tm + CONV_HALO, :] = jnp.where(i == nt - 1, 0.0, dph_ref[...])
        back_taps = [(CONV_K - 1 - k, (w_ref, k)) for k in range(CONV_K)]
        grad_taps = [(CONV_HALO - (CONV_K - 1) + k, k) for k in range(CONV_K)]
        for r in range(tm // ROW_BLOCK):
            base = r * ROW_BLOCK
            rows = slice(base, base + ROW_BLOCK)
            for lanes in _lane_blocks(CONV_DIM):
                gate_lanes = slice(CONV_DIM + lanes.start, CONV_DIM + lanes.stop)
                acc = _tap_sum(ds_ref, base, back_taps, ROW_BLOCK, lanes)
                for k, total in _tap_products(ds_ref[rows, lanes], xs_ref, base, grad_taps, lanes):
                    dw_ref[k:k + 1, lanes] += total
                cval = p_ref[rows, lanes]
                sg = _sigmoid(p_ref[rows, gate_lanes])
                dproj_ref[rows, lanes] = (acc * sg).astype(BF16)
                dproj_ref[rows, gate_lanes] = (acc * cval * sg * (1.0 - sg)).astype(BF16)

    full = lambda s: pl.BlockSpec(s, lambda i: (0, 0))
    return pl.pallas_call(
        body, name="conv_bwd_dw", grid=(nt,),
        in_specs=[pl.BlockSpec((tm, CONV_DIM), lambda i: (i, 0)),
                  pl.BlockSpec((CONV_HALO, CONV_DIM), lambda i: (jnp.minimum((i + 1) * nh, last_h), 0)),
                  pl.BlockSpec((tm, 2 * CONV_DIM), lambda i: (i, 0)),
                  pl.BlockSpec((CONV_HALO, 2 * CONV_DIM), lambda i: (jnp.maximum(i * nh - 1, 0), 0)),
                  full((CONV_K, CONV_DIM))],
        out_specs=[pl.BlockSpec((tm, 2 * CONV_DIM), lambda i: (i, 0)), full((CONV_HALO, CONV_DIM))],
        out_shape=[jax.ShapeDtypeStruct((t, 2 * CONV_DIM), BF16), jax.ShapeDtypeStruct((CONV_HALO, CONV_DIM), F32)],
        scratch_shapes=[pltpu.VMEM((CONV_HALO + tm, CONV_DIM), F32), pltpu.VMEM((tm + CONV_HALO, CONV_DIM), F32)],
        compiler_params=_params(("arbitrary",)),
    )(d_pre, d_pre, proj, proj, w_dw)


def _lower_bound(logit_ref):
    l0 = logit_ref[0:1, :]
    l1 = logit_ref[1:2, :]
    m = jnp.maximum(l0, l1)
    e0 = jnp.exp(l0 - m)
    e1 = jnp.exp(l1 - m)
    return e0 / (e0 + e1)


def _tri(lower):
    r = lax.broadcasted_iota(jnp.int32, (CHUNK, CHUNK), 0)
    c = lax.broadcasted_iota(jnp.int32, (CHUNK, CHUNK), 1)
    return (c <= r) if lower else (c >= r)


def _hgrn_gates(fz, lb):
    s = _sigmoid(fz)
    sn = _sigmoid(-fz)
    f = lb + (1.0 - lb) * s
    return s, sn, f


def _block_tri(rows, lower=True):
    r = lax.broadcasted_iota(jnp.int32, (rows, rows), 0)
    c = lax.broadcasted_iota(jnp.int32, (rows, rows), 1)
    tri = (c <= r) if lower else (c >= r)
    return (tri & (r // CHUNK == c // CHUNK)).astype(BF16)


def _tri_rows(tm):
    return min(tm, MXU_DEPTH)


def _tri_matmul(tri_ref, x):
    hi = x.astype(BF16)
    lo = (x - hi.astype(F32)).astype(BF16)
    tri = tri_ref[...]
    return (lax.dot_general(tri, hi, NN, preferred_element_type=F32)
            + lax.dot_general(tri, lo, NN, preferred_element_type=F32))


def _groups(tm):
    g = _tri_rows(tm)
    return [slice(i * g, (i + 1) * g) for i in range(tm // g)]


def _hgrn_fwd(proj, logits, norm_g):
    t = proj.shape[0]
    tm = CHUNK * CHUNKS_PER_BLOCK if t % (CHUNK * CHUNKS_PER_BLOCK) == 0 else CHUNK
    cpb = tm // CHUNK
    nt = t // tm
    half = CHUNK // 2

    def body(qz_ref, fz_ref, iv_ref, gz_ref, lg_ref, ng_ref, tri_ref, o_ref, og_ref, st_ref,
             state_ref, qe_ref, ke_ref, qb_ref, kl_ref, v_ref, upd_ref, decay_ref, a_ref, q_ref, kk_ref, b_ref):
        j = pl.program_id(1)

        @pl.when(j == 0)
        def _():
            state_ref[...] = jnp.zeros_like(state_ref)

        lb = _lower_bound(lg_ref)
        chunks = [slice(c * CHUNK, (c + 1) * CHUNK) for c in range(cpb)]
        for rows in chunks:
            qz = qz_ref[rows, :]
            q_ref[rows, :] = qz * _sigmoid(qz)
            _, sn, f = _hgrn_gates(fz_ref[rows, :], lb)
            kk_ref[rows, :] = (1.0 - lb) * sn
            b_ref[rows, :] = jnp.log(f)
            v_ref[rows, :] = iv_ref[rows, :].astype(BF16)
        for rows in _groups(tm):
            b_ref[rows, :] = _tri_matmul(tri_ref, b_ref[rows, :])
        for c, rows in enumerate(chunks):
            b = b_ref[rows, :]
            bref = b[half - 1:half, :]
            blast = b[CHUNK - 1:CHUNK, :]
            q = q_ref[rows, :]
            kk = kk_ref[rows, :]
            qb_ref[rows, :] = (q * jnp.exp(b)).astype(BF16)
            qe_ref[rows, :] = (q * jnp.exp(b - bref)).astype(BF16)
            ke_ref[rows, :] = (kk * jnp.exp(bref - b)).astype(BF16)
            kl_ref[rows, :] = (kk * jnp.exp(blast - b)).astype(BF16)
            decay_ref[c:c + 1, :] = jnp.exp(blast)
        causal = _tri(True)
        for c, rows in enumerate(chunks):
            upd_ref[c] = _dot(v_ref[rows, :], kl_ref[rows, :], TN)
            a_ref[c] = jnp.where(causal, _dot(qe_ref[rows, :], ke_ref[rows, :], NT), 0.0).astype(BF16)
        state = state_ref[...]
        for c in range(cpb):
            st_ref[c] = state.astype(BF16)
            state = state * decay_ref[c:c + 1, :] + upd_ref[c]
        state_ref[...] = state
        for c, rows in enumerate(chunks):
            o_ref[rows, :] = _dot(a_ref[c], v_ref[rows, :], NN) + _dot(qb_ref[rows, :], st_ref[c], NT)
        for rows in chunks:
            o = o_ref[rows, :]
            r = lax.rsqrt(jnp.mean(o * o, axis=-1, keepdims=True) + RMS_EPS)
            gz = gz_ref[rows, :]
            og_ref[rows, :] = ((o * r * ng_ref[...]) * (gz * _sigmoid(gz))).astype(BF16)

    col = lambda base: pl.BlockSpec((tm, HEAD_DIM), lambda h, j: (j, base + h))
    tile_bf = pltpu.VMEM((tm, HEAD_DIM), BF16)
    tile_f32 = pltpu.VMEM((tm, HEAD_DIM), F32)
    return pl.pallas_call(
        body, name="hgrn_fwd", grid=(HEADS, nt),
        in_specs=[col(8), col(16), col(24), col(32),
                  pl.BlockSpec((2, HEAD_DIM), lambda h, j: (0, h)), pl.BlockSpec((1, HEAD_DIM), lambda h, j: (0, h)),
                  pl.BlockSpec((_tri_rows(tm), _tri_rows(tm)), lambda h, j: (0, 0))],
        out_specs=[col(0), col(0), pl.BlockSpec((None, cpb, HEAD_DIM, HEAD_DIM), lambda h, j: (h, j, 0, 0))],
        out_shape=[jax.ShapeDtypeStruct((t, HGRN_DIM), F32), jax.ShapeDtypeStruct((t, HGRN_DIM), BF16),
                   jax.ShapeDtypeStruct((HEADS, t // CHUNK, HEAD_DIM, HEAD_DIM), BF16)],
        scratch_shapes=[pltpu.VMEM((HEAD_DIM, HEAD_DIM), F32), tile_bf, tile_bf, tile_bf, tile_bf, tile_bf,
                        pltpu.VMEM((cpb, HEAD_DIM, HEAD_DIM), F32), pltpu.VMEM((max(cpb, 8), HEAD_DIM), F32),
                        pltpu.VMEM((cpb, CHUNK, CHUNK), BF16), tile_f32, tile_f32, tile_f32],
        compiler_params=_params(("parallel", "arbitrary")),
    )(proj, proj, proj, proj, logits, norm_g, _block_tri(_tri_rows(tm)))


def _hgrn_bwd(d_og, o, proj, states, logits, norm_g, hosted=None):
    t = proj.shape[0]
    tm = CHUNK * CHUNKS_PER_BLOCK if t % (CHUNK * CHUNKS_PER_BLOCK) == 0 else CHUNK
    cpb = tm // CHUNK
    nt = t // tm
    half = CHUNK // 2

    def body(dog_ref, o_ref, qz_ref, fz_ref, iv_ref, gz_ref, st_ref, lg_ref, ng_ref, tril_ref, triu_ref,
             dqz_ref, dfz_ref, div_ref, dgz_ref, sums_ref,
             dstate_ref, qe_ref, ke_ref, qb_ref, kl_ref, v_ref, do_ref, upd_ref, dst_ref, a_ref, da_ref,
             decay_ref, through_ref, q_ref, kk_ref, b_ref, dsilu_ref, gs_ref, gf_ref, sn_ref,
             eb_ref, ebr_ref, ekr_ref, ebl_ref, rev_ref, pre_ref, dk_ref):
        j = pl.program_id(1)

        @pl.when(j == 0)
        def _():
            dstate_ref[...] = jnp.zeros_like(dstate_ref)
            sums_ref[...] = jnp.zeros_like(sums_ref)

        lb = _lower_bound(lg_ref)
        ng = ng_ref[...]
        chunks = [slice(c * CHUNK, (c + 1) * CHUNK) for c in range(cpb)]
        for rows in chunks:
            qz = qz_ref[rows, :]
            sq = _sigmoid(qz)
            q_ref[rows, :] = qz * sq
            dsilu_ref[rows, :] = sq * (1.0 + qz * (1.0 - sq))
            s, sn, f = _hgrn_gates(fz_ref[rows, :], lb)
            kk_ref[rows, :] = (1.0 - lb) * sn
            b_ref[rows, :] = jnp.log(f)
            sn_ref[rows, :] = sn
            gf_ref[rows, :] = sn / f
            gs_ref[rows, :] = (1.0 - lb) * s
            v_ref[rows, :] = iv_ref[rows, :].astype(BF16)
            ov = o_ref[rows, :]
            r = lax.rsqrt(jnp.mean(ov * ov, axis=-1, keepdims=True) + RMS_EPS)
            on = ov * r
            gz = gz_ref[rows, :]
            sg = _sigmoid(gz)
            dog = dog_ref[rows, :]
            dgz_ref[rows, :] = (dog * (on * ng) * (sg * (1.0 + gz * (1.0 - sg)))).astype(BF16)
            d_ong = dog * (gz * sg)
            sums_ref[0:1, :] += _colsum(d_ong * on)
            d_on = d_ong * ng
            do_ref[rows, :] = (r * (d_on - on * jnp.mean(d_on * on, axis=-1, keepdims=True))).astype(BF16)
        for rows in _groups(tm):
            b_ref[rows, :] = _tri_matmul(tril_ref, b_ref[rows, :])
        for c, rows in enumerate(chunks):
            b = b_ref[rows, :]
            bref = b[half - 1:half, :]
            blast = b[CHUNK - 1:CHUNK, :]
            q = q_ref[rows, :]
            kk = kk_ref[rows, :]
            eb = jnp.exp(b)
            ebr = jnp.exp(b - bref)
            ekr = jnp.exp(bref - b)
            ebl = jnp.exp(blast - b)
            eb_ref[rows, :] = eb
            ebr_ref[rows, :] = ebr
            ekr_ref[rows, :] = ekr
            ebl_ref[rows, :] = ebl
            qb_ref[rows, :] = (q * eb).astype(BF16)
            qe_ref[rows, :] = (q * ebr).astype(BF16)
            ke_ref[rows, :] = (kk * ekr).astype(BF16)
            kl_ref[rows, :] = (kk * ebl).astype(BF16)
            decay_ref[c:c + 1, :] = jnp.exp(blast)
        causal = _tri(True)
        for c, rows in enumerate(chunks):
            upd_ref[c] = _dot(do_ref[rows, :], qb_ref[rows, :], TN)
            a_ref[c] = jnp.where(causal, _dot(qe_ref[rows, :], ke_ref[rows, :], NT), 0.0).astype(BF16)
            da_ref[c] = jnp.where(causal, _dot(do_ref[rows, :], v_ref[rows, :], NT), 0.0).astype(BF16)
        dstate = dstate_ref[...]
        for c in reversed(range(cpb)):
            dst_ref[c] = dstate.astype(BF16)
            decay = decay_ref[c:c + 1, :]
            through_ref[c:c + 1, :] = decay * _colsum(dstate * st_ref[c].astype(F32))
            dstate = dstate * decay + upd_ref[c]
        dstate_ref[...] = dstate
        for c, rows in enumerate(chunks):
            div_ref[rows, :] = (_dot(a_ref[c], do_ref[rows, :], TN)
                                + _dot(kl_ref[rows, :], dst_ref[c], NT)).astype(BF16)
        for c, rows in enumerate(chunks):
            dqe = _dot(da_ref[c], ke_ref[rows, :], NN)
            dq_inter = _dot(do_ref[rows, :], st_ref[c], NN) * eb_ref[rows, :]
            dqz_ref[rows, :] = ((dqe * ebr_ref[rows, :] + dq_inter) * dsilu_ref[rows, :]).astype(BF16)
            rev_ref[rows, :] = qe_ref[rows, :].astype(F32) * dqe + q_ref[rows, :] * dq_inter
        for c, rows in enumerate(chunks):
            dke = _dot(da_ref[c], qe_ref[rows, :], TN)
            dk_inter = _dot(v_ref[rows, :], dst_ref[c], NN) * ebl_ref[rows, :]
            dk_ref[rows, :] = dke * ekr_ref[rows, :] + dk_inter
            rev_ref[rows, :] -= ke_ref[rows, :].astype(F32) * dke
            pre_ref[rows, :] = kk_ref[rows, :] * dk_inter
        for rows in _groups(tm):
            pre = pre_ref[rows, :]
            rev_ref[rows, :] = _tri_matmul(triu_ref, rev_ref[rows, :]) + (_tri_matmul(tril_ref, pre) - pre)
        for c, rows in enumerate(chunks):
            dlf = rev_ref[rows, :] + through_ref[c:c + 1, :]
            common = gf_ref[rows, :] * dlf - sn_ref[rows, :] * dk_ref[rows, :]
            dfz_ref[rows, :] = (gs_ref[rows, :] * common).astype(BF16)
            sums_ref[1:2, :] += _colsum(common)

        @pl.when(j == nt - 1)
        def _():
            sums_ref[1:2, :] = sums_ref[1:2, :] * lb * (1.0 - lb)

    rev = lambda base: pl.BlockSpec((tm, HEAD_DIM), lambda h, j: (nt - 1 - j, base + h))
    vec = lambda n: pl.BlockSpec((n, HEAD_DIM), lambda h, j: (0, h))
    const = pl.BlockSpec((_tri_rows(tm), _tri_rows(tm)), lambda h, j: (0, 0))
    bf = jax.ShapeDtypeStruct((t, HGRN_DIM), BF16)
    tile_bf = pltpu.VMEM((tm, HEAD_DIM), BF16)
    tile_f32 = pltpu.VMEM((tm, HEAD_DIM), F32)
    square = lambda dtype: pltpu.VMEM((cpb, HEAD_DIM, HEAD_DIM), dtype)
    rows8 = pltpu.VMEM((max(cpb, 8), HEAD_DIM), F32)
    return _call(
        body, name="hgrn_bwd", grid=(HEADS, nt),
        in_specs=[rev(0), rev(0), rev(8), rev(16), rev(24), rev(32),
                  pl.BlockSpec((None, cpb, HEAD_DIM, HEAD_DIM), lambda h, j: (h, nt - 1 - j, 0, 0)),
                  vec(2), vec(1), const, const],
        out_specs=[rev(0), rev(0), rev(0), rev(0), vec(8)],
        out_shape=[bf, bf, bf, bf, jax.ShapeDtypeStruct((8, HGRN_DIM), F32)],
        scratch_shapes=[pltpu.VMEM((HEAD_DIM, HEAD_DIM), F32)] + [tile_bf] * 6 + [square(F32), square(BF16)]
        + [pltpu.VMEM((cpb, CHUNK, CHUNK), BF16)] * 2 + [rows8, rows8] + [tile_f32] * 14,
        semantics=("parallel", "arbitrary"),
        operands=(d_og, o, proj, proj, proj, proj, states, logits, norm_g, _block_tri(_tri_rows(tm)), _block_tri(_tri_rows(tm), lower=False)),
        hosted=hosted)


def _mix_ln1(proj, y_conv, y_hgrn, w_out, x, g, b):
    t = x.shape[0]
    tm = _pick(t, 512)
    d = D_MODEL

    def lhs(ins, outs):
        for rows in _row_blocks(tm):
            outs[0][rows, :] = (_sigmoid(ins[0][rows, :]) * ins[3][rows, :]
                                + _sigmoid(ins[2][rows, :]) * ins[4][rows, :]).astype(BF16)
        return outs[0][...]

    def epilogue(acc, ins, outs, scr):
        for rows in _row_blocks(tm):
            r = ALPHA * ins[5][rows, :] + acc[rows, :]
            outs[1][rows, :] = r
            xhat, _ = _ln(r)
            outs[2][rows, :] = (xhat * ins[6][...] + ins[7][...]).astype(BF16)

    tile = pl.BlockSpec((tm, d), lambda i, j, k: (i, 0))
    vec = pl.BlockSpec((1, d), lambda i, j, k: (0, 0))
    return _mm_fused(
        "mix_ln1", (t // tm, 1, 1), (proj, w_out, proj, y_conv, y_hgrn, x, g, b),
        [pl.BlockSpec((tm, d), lambda i, j, k: (i, 5)), pl.BlockSpec((d, d), lambda i, j, k: (0, 0)),
         pl.BlockSpec((tm, d), lambda i, j, k: (i, 6)), tile, tile, tile, vec, vec],
        [jax.ShapeDtypeStruct((t, d), BF16), jax.ShapeDtypeStruct((t, d), F32), jax.ShapeDtypeStruct((t, d), BF16)],
        [tile, tile, tile], NN, (tm, d), epilogue, lhs=lhs)


def _d_mixed_merge_bwd(d_r1b, w_out, proj, y_conv, y_hgrn):
    t = proj.shape[0]
    tm = _pick(t, 512)
    d = D_MODEL

    def epilogue(d_mixed, ins, outs, scr):
        dy_ref, dmz_ref = outs
        for rows in _row_blocks(tm):
            dm = d_mixed[rows, :]
            for br in range(2):
                sg = _sigmoid(ins[2 + br][rows, :])
                dy_ref[br, rows, :] = (sg * dm).astype(BF16)
                dmz_ref[rows, br * d:(br + 1) * d] = (dm * ins[4 + br][rows, :] * sg * (1.0 - sg)).astype(BF16)

    tile = pl.BlockSpec((tm, d), lambda i, j, k: (i, 0))
    return _mm_fused(
        "d_mixed_merge_bwd", (t // tm, 1, 1), (d_r1b, w_out, proj, proj, y_conv, y_hgrn),
        [tile, pl.BlockSpec((d, d), lambda i, j, k: (0, 0)), pl.BlockSpec((tm, d), lambda i, j, k: (i, 5)),
         pl.BlockSpec((tm, d), lambda i, j, k: (i, 6)), tile, tile],
        [jax.ShapeDtypeStruct((2, t, d), BF16), jax.ShapeDtypeStruct((t, 2 * d), BF16)],
        [pl.BlockSpec((2, tm, d), lambda i, j, k: (0, i, 0)), pl.BlockSpec((tm, 2 * d), lambda i, j, k: (i, 0))],
        NT, (tm, d), epilogue)


def _ffn_out_ln2(act, w_ffn_out, r1, target, g1, b1, g2, b2):
    t = r1.shape[0]
    tm = _pick(t, 1024)
    nt = t // tm
    d = D_MODEL

    def epilogue(y_ffn, ins, outs, scr):
        r1_ref, tg_ref, g1_ref, b1_ref, g2_ref, b2_ref = ins[2:]
        dr_ref, drb_ref, sums_ref = outs
        (sq_ref,) = scr
        i = pl.program_id(0)

        @pl.when(i == 0)
        def _():
            sums_ref[...] = jnp.zeros_like(sums_ref)
            sq_ref[...] = jnp.zeros_like(sq_ref)

        for rows in _row_blocks(tm):
            xh1, _ = _ln(r1_ref[rows, :])
            x1 = xh1 * g1_ref[...] + b1_ref[...]
            xh2, rstd2 = _ln(ALPHA * x1 + y_ffn[rows, :])
            diff = xh2 * g2_ref[...] + b2_ref[...] - tg_ref[rows, :]
            dy = diff * (1.0 / D_MODEL)
            dr = _ln_bwd(dy, xh2, rstd2, g2_ref[...])
            dr_ref[rows, :] = dr
            drb_ref[rows, :] = dr.astype(BF16)
            sums_ref[0:1, :] += _colsum(dy * xh2)
            sums_ref[1:2, :] += _colsum(dy)
            sq_ref[...] += _colsum(diff * diff)

        @pl.when(i == nt - 1)
        def _():
            total = jnp.sum(sq_ref[...], axis=-1, keepdims=True) * (0.5 / D_MODEL)
            sums_ref[2:3, :] = jnp.broadcast_to(total, (1, D_MODEL))

    tile = pl.BlockSpec((tm, d), lambda i, j, k: (i, 0))
    vec = pl.BlockSpec((1, d), lambda i, j, k: (0, 0))
    return _mm_fused(
        "ffn_out_ln2", (nt, 1, 4), (act, w_ffn_out, r1, target, g1, b1, g2, b2),
        [pl.BlockSpec((None, tm, FF_SHARD), lambda i, j, k: (k, i, 0)),
         pl.BlockSpec((None, FF_SHARD, d), lambda i, j, k: (k, 0, 0)), tile, tile, vec, vec, vec, vec],
        [jax.ShapeDtypeStruct((t, d), F32), jax.ShapeDtypeStruct((t, d), BF16), jax.ShapeDtypeStruct((8, d), F32)],
        [tile, tile, pl.BlockSpec((8, d), lambda i, j, k: (0, 0))], NN, (tm, d), epilogue,
        scratch=[pltpu.VMEM((1, d), F32)])


def _d_x1_ln1_bwd(d_z, w_ffn_in, d_r2, r1, g1):
    t = r1.shape[0]
    tm = _pick(t, 1024)
    d = D_MODEL

    def epilogue(dx_ffn, ins, outs, scr):
        dr2_ref, r1_ref, g_ref = ins[2:]
        dr1_ref, dr1b_ref, sums_ref = outs
        i = pl.program_id(0)

        @pl.when(i == 0)
        def _():
            sums_ref[...] = jnp.zeros_like(sums_ref)

        for rows in _row_blocks(tm):
            xhat, rstd = _ln(r1_ref[rows, :])
            dx1 = ALPHA * dr2_ref[rows, :] + dx_ffn[rows, :]
            dr1 = _ln_bwd(dx1, xhat, rstd, g_ref[...])
            dr1_ref[rows, :] = dr1
            dr1b_ref[rows, :] = dr1.astype(BF16)
            sums_ref[0:1, :] += _colsum(dx1 * xhat)
            sums_ref[1:2, :] += _colsum(dx1)

    tile = pl.BlockSpec((tm, d), lambda i, j, k: (i, 0))
    return _mm_fused(
        "d_x1_ln1_bwd", (t // tm, 1, N_DEV), (d_z, w_ffn_in, d_r2, r1, g1),
        [pl.BlockSpec((None, tm, FF_SHARD), lambda i, j, k: (k, i, 0)),
         pl.BlockSpec((None, d, FF_SHARD), lambda i, j, k: (k, 0, 0)), tile, tile,
         pl.BlockSpec((1, d), lambda i, j, k: (0, 0))],
        [jax.ShapeDtypeStruct((t, d), F32), jax.ShapeDtypeStruct((t, d), BF16), jax.ShapeDtypeStruct((8, d), F32)],
        [tile, tile, pl.BlockSpec((8, d), lambda i, j, k: (0, 0))], NT, (tm, d), epilogue)


def _cast_bf16(x):
    t = x.shape[0]
    tm = _pick(t, 512)

    def body(x_ref, o_ref):
        o_ref[...] = x_ref[...].astype(BF16)

    tile = pl.BlockSpec((tm, D_MODEL), lambda i: (i, 0))
    return pl.pallas_call(
        body, name="cast_x", grid=(t // tm,), in_specs=[tile], out_specs=tile,
        out_shape=jax.ShapeDtypeStruct((t, D_MODEL), BF16), compiler_params=_params(("parallel",)),
    )(x)


def _relayout(name, a, in_block, in_map, out_block, out_map, out_shape):
    def body(a_ref, o_ref):
        o_ref[...] = a_ref[...].astype(o_ref.dtype)

    return pl.pallas_call(
        body, name=name, grid=(N_DEV,), in_specs=[pl.BlockSpec(in_block, in_map)],
        out_specs=pl.BlockSpec(out_block, out_map), out_shape=out_shape, compiler_params=_params(("parallel",)),
    )(a)


_GELU_C = math.sqrt(2.0 / math.pi)


_GELU_CUBIC = 0.044715


def _gelu_parts(u):
    u2 = u * u
    th = jnp.tanh(u * (_GELU_C + (_GELU_C * _GELU_CUBIC) * u2))
    hu = 0.5 * u
    return th, hu + hu * th, u2, hu


BF16_ROWS = 16


def _ffn_act_fwd(z, w_dw, b_dw):
    t = z.shape[2]
    tm = _pick(t, 512)
    nh = tm // FFN_HALO

    def body(z_ref, zh_ref, w_ref, b_ref, act_ref, gd_ref, us_ref):
        i = pl.program_id(1)
        us_ref[0:FFN_HALO, :] = jnp.where(i == 0, 0.0, zh_ref[...])
        us_ref[FFN_HALO:FFN_HALO + tm, :] = z_ref[0]
        for r in range(tm // ROW_BLOCK):
            base = r * ROW_BLOCK
            rows = slice(base, base + ROW_BLOCK)
            for lanes in _lane_blocks(FF_SHARD):
                uc = b_ref[:, lanes]
                for k in range(FFN_K):
                    off = base + FFN_HALO - (FFN_K - 1) + k
                    uc = uc + w_ref[k:k + 1, lanes] * us_ref[off:off + ROW_BLOCK, lanes]
                th, gelu, u2, hu = _gelu_parts(uc)
                dgelu = (0.5 + 0.5 * th) + (hu - hu * th * th) * (_GELU_C + (3.0 * _GELU_C * _GELU_CUBIC) * u2)
                act_ref[rows, lanes] = (gelu * z_ref[1, rows, lanes]).astype(BF16)
                gd_ref[0, rows, lanes] = gelu.astype(BF16)
                gd_ref[1, rows, lanes] = dgelu.astype(BF16)

    return pl.pallas_call(
        body, name="ffn_act_fwd", grid=(4, t // tm),
        in_specs=[pl.BlockSpec((2, None, tm, FF_SHARD), lambda j, i: (0, j, i, 0)),
                  pl.BlockSpec((None, None, FFN_HALO, FF_SHARD), lambda j, i: (0, j, jnp.maximum(i * nh - 1, 0), 0)),
                  pl.BlockSpec((None, FFN_K, FF_SHARD), lambda j, i: (j, 0, 0)),
                  pl.BlockSpec((None, 1, FF_SHARD), lambda j, i: (j, 0, 0))],
        out_specs=[pl.BlockSpec((None, tm, FF_SHARD), lambda j, i: (j, i, 0)),
                   pl.BlockSpec((2, None, tm, FF_SHARD), lambda j, i: (0, j, i, 0))],
        out_shape=[jax.ShapeDtypeStruct((4, t, FF_SHARD), BF16), jax.ShapeDtypeStruct((2, 4, t, FF_SHARD), BF16)],
        scratch_shapes=[pltpu.VMEM((FFN_HALO + tm, FF_SHARD), F32)],
        compiler_params=_params(("parallel", "arbitrary")),
    )(z, z, w_dw, b_dw)


def _ffn_act_bwd(d_act, z, gd, w_dw):
    t = z.shape[2]
    tm = _pick(t, 512)
    nt = t // tm
    nh = tm // FFN_HALO
    last_h = t // FFN_HALO - 1
    pad = FFN_HALO - (FFN_K - 1)

    def fold(x):
        return functools.reduce(jnp.add, [x[r:r + SUBLANES, :] for r in range(0, x.shape[0], SUBLANES)])

    def body(da_ref, dah_ref, z_ref, zp_ref, gn_ref, gd_ref, gdn_ref, w_ref, dz_ref, sums_ref, us_ref, ds_ref,
             part_ref):
        i = pl.program_id(1)

        @pl.when(i == 0)
        def _():
            part_ref[...] = jnp.zeros_like(part_ref)

        us_ref[0:FFN_HALO, :] = jnp.where(i == 0, 0.0, zp_ref[...])
        us_ref[FFN_HALO:FFN_HALO + tm, :] = z_ref[0]
        for r in range(tm // ROW_BLOCK):
            base = r * ROW_BLOCK
            rows = slice(base, base + ROW_BLOCK)
            for lanes in _lane_blocks(FF_SHARD):
                da = da_ref[rows, lanes]
                dz_ref[1, rows, lanes] = (da * gd_ref[0, rows, lanes].astype(F32)).astype(BF16)
                duc = da * z_ref[1, rows, lanes] * gd_ref[1, rows, lanes].astype(F32)
                ds_ref[rows, lanes] = duc
                for k in range(FFN_K):
                    part_ref[k, :, lanes] += fold(duc * us_ref[base + pad + k:base + pad + k + ROW_BLOCK, lanes])
                part_ref[FFN_K, :, lanes] += fold(duc)
        duc_next = dah_ref[...] * gn_ref[...] * gdn_ref[0:FFN_HALO, :].astype(F32)
        ds_ref[tm:tm + FFN_HALO, :] = jnp.where(i == nt - 1, 0.0, duc_next)
        for r in range(tm // ROW_BLOCK):
            base = r * ROW_BLOCK
            for lanes in _lane_blocks(FF_SHARD):
                du = None
                for k in range(FFN_K):
                    off = base + FFN_K - 1 - k
                    term = w_ref[k:k + 1, lanes] * ds_ref[off:off + ROW_BLOCK, lanes]
                    du = term if du is None else du + term
                dz_ref[0, base:base + ROW_BLOCK, lanes] = du.astype(BF16)

        @pl.when(i == nt - 1)
        def _():
            sums_ref[...] = jnp.zeros_like(sums_ref)
            for k in range(FFN_K + 1):
                sums_ref[k:k + 1, :] = _colsum(part_ref[k])

    nxt = lambda i: jnp.minimum((i + 1) * nh, last_h)
    nxt_bf = lambda i: jnp.minimum((i + 1) * (tm // BF16_ROWS), t // BF16_ROWS - 1)
    return pl.pallas_call(
        body, name="ffn_act_bwd", grid=(4, nt),
        in_specs=[pl.BlockSpec((None, tm, FF_SHARD), lambda j, i: (j, i, 0)),
                  pl.BlockSpec((None, FFN_HALO, FF_SHARD), lambda j, i: (j, nxt(i), 0)),
                  pl.BlockSpec((2, None, tm, FF_SHARD), lambda j, i: (0, j, i, 0)),
                  pl.BlockSpec((None, None, FFN_HALO, FF_SHARD), lambda j, i: (0, j, jnp.maximum(i * nh - 1, 0), 0)),
                  pl.BlockSpec((None, None, FFN_HALO, FF_SHARD), lambda j, i: (1, j, nxt(i), 0)),
                  pl.BlockSpec((2, None, tm, FF_SHARD), lambda j, i: (0, j, i, 0)),
                  pl.BlockSpec((None, None, BF16_ROWS, FF_SHARD), lambda j, i: (1, j, nxt_bf(i), 0)),
                  pl.BlockSpec((None, FFN_K, FF_SHARD), lambda j, i: (j, 0, 0))],
        out_specs=[pl.BlockSpec((2, None, tm, FF_SHARD), lambda j, i: (0, j, i, 0)),
                   pl.BlockSpec((None, 8, FF_SHARD), lambda j, i: (j, 0, 0))],
        out_shape=[jax.ShapeDtypeStruct((2, 4, t, FF_SHARD), BF16), jax.ShapeDtypeStruct((4, 8, FF_SHARD), F32)],
        scratch_shapes=[pltpu.VMEM((FFN_HALO + tm, FF_SHARD), F32), pltpu.VMEM((tm + FFN_HALO, FF_SHARD), F32),
                        pltpu.VMEM((FFN_K + 1, SUBLANES, FF_SHARD), F32)],
        compiler_params=_params(("parallel", "arbitrary")),
    )(d_act, d_act, z, z, z, gd, gd, w_dw)


def _local_step(x, target, weights, small, scatter=None, order=None):
    t = x.shape[0]
    tm = _pick(t, 2048)
    tk = _pick(t, 2048)
    nm = t // tm
    nk = t // tk
    d = D_MODEL

    xb = _cast_bf16(x)
    if isinstance(weights, _Hosted):
        proj, weights = _proj_gather(xb, weights, order)
    else:
        proj = _mm("proj", xb, weights[0], (t, IN_COLS), F32, (nm, N_DEV, 1),
                   pl.BlockSpec((tm, d), lambda i, j, k: (i, 0)),
                   pl.BlockSpec((None, d, IN_SHARD), lambda i, j, k: (j, 0, 0)),
                   pl.BlockSpec((tm, IN_SHARD), lambda i, j, k: (i, j)), NN, (tm, IN_SHARD))
    w_in, w_conv_out8, w_hgrn_out8, w_out8, w_ffn_in, w_ffn_out8, conv_dw8, ffn_dw8 = weights
    w_conv_out = _relayout("w_conv_out_natural", w_conv_out8, (None, CONV_DIM, 128), lambda j: (j, 0, 0),
                           (CONV_DIM, 128), lambda j: (0, j), jax.ShapeDtypeStruct((CONV_DIM, d), BF16))
    w_hgrn_out = w_hgrn_out8.reshape(d, d)
    w_out = w_out8.reshape(d, d)
    w_ffn_out = w_ffn_out8.reshape(4, FF_SHARD, d)
    conv_dw = jnp.transpose(conv_dw8[:, :CONV_K, :CONV_DIM // N_DEV], (1, 0, 2)).reshape(CONV_K, CONV_DIM)
    ffn_dw = jnp.transpose(ffn_dw8[:, :FFN_K, :D_FF // N_DEV], (1, 0, 2)).reshape(FFN_K, 4, FF_SHARD)
    small = dict(small, w_conv_dw=conv_dw, w_ffn_dw=jnp.transpose(ffn_dw, (1, 0, 2)),
                 b_ffn_dw=small["b_ffn_dw"].reshape(4, 1, FF_SHARD))

    c_act, conv_pre = _conv_fwd(proj, small["w_conv_dw"], small["b_conv_dw"], small["conv_ln_g"], small["conv_ln_b"])
    y_conv = _mm("y_conv", c_act, w_conv_out, (t, d), F32, (nm, 1, 1),
                 pl.BlockSpec((tm, CONV_DIM), lambda i, j, k: (i, 0)),
                 pl.BlockSpec((CONV_DIM, d), lambda i, j, k: (0, 0)),
                 pl.BlockSpec((tm, d), lambda i, j, k: (i, 0)), NN, (tm, d))
    o, og, states = _hgrn_fwd(proj, small["hgrn_lb_logits"], small["hgrn_norm_g"])
    sq_w = pl.BlockSpec((d, d), lambda i, j, k: (0, 0))
    row_tile = pl.BlockSpec((tm, d), lambda i, j, k: (i, 0))
    y_hgrn = _mm("y_hgrn", og, w_hgrn_out, (t, d), F32, (nm, 1, 1), row_tile, sq_w, row_tile, NN, (tm, d))
    mixed, r1, x1b = _mix_ln1(proj, y_conv, y_hgrn, w_out, x, small["ln1_g"], small["ln1_b"])
    z = _mm("ffn_in", x1b, w_ffn_in, (N_DEV, t, FF_SHARD), F32, (nm, N_DEV, 1), row_tile,
            pl.BlockSpec((None, d, FF_SHARD), lambda i, j, k: (j, 0, 0)),
            pl.BlockSpec((None, tm, FF_SHARD), lambda i, j, k: (j, i, 0)), NN, (tm, FF_SHARD))
    z = z.reshape(2, 4, t, FF_SHARD)
    act, gelu_and_slope = _ffn_act_fwd(z, small["w_ffn_dw"], small["b_ffn_dw"])

    d_r2, d_r2b, sums_ln2 = _ffn_out_ln2(act, w_ffn_out, r1, target, small["ln1_g"], small["ln1_b"],
                                         small["ln2_g"], small["ln2_b"])
    d_act = _mm("d_act", d_r2b, w_ffn_out, (4, t, FF_SHARD), F32, (nm, 4, 1), row_tile,
                pl.BlockSpec((None, FF_SHARD, d), lambda i, j, k: (j, 0, 0)),
                pl.BlockSpec((None, tm, FF_SHARD), lambda i, j, k: (j, i, 0)), NT, (tm, FF_SHARD))
    g_w_ffn_out = _mm("g_w_ffn_out", act, d_r2b, (4, FF_SHARD, d), BF16, (4, 1, nk),
                      pl.BlockSpec((None, tk, FF_SHARD), lambda i, j, k: (i, k, 0)),
                      pl.BlockSpec((tk, d), lambda i, j, k: (k, 0)),
                      pl.BlockSpec((None, FF_SHARD, d), lambda i, j, k: (i, 0, 0)), TN, (FF_SHARD, d))
    d_z, sums_ffn = _ffn_act_bwd(d_act, z, gelu_and_slope, small["w_ffn_dw"])
    d_z8 = d_z.reshape(N_DEV, t, FF_SHARD)
    d_r1, d_r1b, sums_ln1 = _d_x1_ln1_bwd(d_z8, w_ffn_in, d_r2, r1, small["ln1_g"])
    g_w_ffn_in = _mm("g_w_ffn_in", x1b, d_z8, (N_DEV, d, FF_SHARD), BF16, (N_DEV, 1, nk),
                     pl.BlockSpec((tk, d), lambda i, j, k: (k, 0)),
                     pl.BlockSpec((None, tk, FF_SHARD), lambda i, j, k: (i, k, 0)),
                     pl.BlockSpec((None, d, FF_SHARD), lambda i, j, k: (i, 0, 0)), TN, (d, FF_SHARD))
    k_tile = pl.BlockSpec((tk, d), lambda i, j, k: (k, 0))
    g_w_out = _mm("g_w_out", mixed, d_r1b, (d, d), BF16, (1, 1, nk), k_tile, k_tile, sq_w, TN, (d, d))
    d_y, d_mz = _d_mixed_merge_bwd(d_r1b, w_out, proj, y_conv, y_hgrn)
    d_pre, sums_conv = _d_c_norm_bwd(d_y, w_conv_out, conv_pre, small["conv_ln_g"], small["conv_ln_b"])
    g_w_conv_out = _mm("g_w_conv_out", c_act, d_y, (CONV_DIM, d), BF16, (1, 1, nk),
                       pl.BlockSpec((tk, CONV_DIM), lambda i, j, k: (k, 0)),
                       pl.BlockSpec((None, tk, d), lambda i, j, k: (0, k, 0)),
                       pl.BlockSpec((CONV_DIM, d), lambda i, j, k: (0, 0)), TN, (CONV_DIM, d))
    g_w_conv_out = _relayout("g_w_conv_out_shards", g_w_conv_out, (CONV_DIM, 128), lambda j: (0, j),
                             (None, CONV_DIM, 128), lambda j: (j, 0, 0),
                             jax.ShapeDtypeStruct((N_DEV, CONV_DIM, 128), BF16))
    d_og = _mm("d_og", d_y, w_hgrn_out, (t, d), F32, (nm, 1, 1),
               pl.BlockSpec((None, tm, d), lambda i, j, k: (1, i, 0)), sq_w, row_tile, NT, (tm, d))
    g_w_hgrn_out = _mm("g_w_hgrn_out", og, d_y, (d, d), BF16, (1, 1, nk), k_tile,
                       pl.BlockSpec((None, tk, d), lambda i, j, k: (1, k, 0)), sq_w, TN, (d, d))
    d_cproj, g_w_conv_dw = _conv_bwd_dw(d_pre, proj, small["w_conv_dw"])

    early = [g_w_conv_out, g_w_hgrn_out.reshape(N_DEV, d // N_DEV, d), g_w_out.reshape(N_DEV, d // N_DEV, d),
             g_w_ffn_in, g_w_ffn_out.reshape(N_DEV, D_FF // N_DEV, d)]
    hgrn_out = _hgrn_bwd(d_og, o, proj, states, small["hgrn_lb_logits"], small["hgrn_norm_g"],
                         hosted=None if scatter is None else scatter(early))
    d_qz, d_fz, d_iv, d_gz, sums_hgrn = hgrn_out[:5]
    early_recv = list(hgrn_out[5:])
    d_proj = jnp.concatenate([d_cproj, d_qz, d_fz, d_iv, d_gz, d_mz], axis=1)
    g_w_in = _mm("g_w_in", xb, d_proj, (N_DEV, d, IN_SHARD), BF16, (N_DEV, 1, nk), k_tile,
                 pl.BlockSpec((tk, IN_SHARD), lambda i, j, k: (k, i)),
                 pl.BlockSpec((None, d, IN_SHARD), lambda i, j, k: (i, 0, 0)), TN, (d, IN_SHARD))
    def add_residual(acc, ins, outs, scr):
        for rows in _row_blocks(ta):
            outs[0][rows, :] = ALPHA * ins[2][rows, :] + acc[rows, :]

    ta = _pick(t, 1024)
    acc_tile = pl.BlockSpec((ta, d), lambda i, j, k: (i, 0))
    grad_x, *late_recv = _mm_fused(
        "grad_x", (t // ta, 1, N_DEV), (d_proj, w_in, d_r1),
        [pl.BlockSpec((ta, IN_SHARD), lambda i, j, k: (i, k)),
         pl.BlockSpec((None, d, IN_SHARD), lambda i, j, k: (k, 0, 0)), acc_tile],
        [jax.ShapeDtypeStruct((t, d), F32)], [acc_tile], NT, (ta, d), add_residual,
        hosted=None if scatter is None else scatter([g_w_in]))

    d_l0 = sums_hgrn[1:2]
    small_grads = {
        "loss": sums_ln2[2:3, 0:128],
        "b_conv_dw": sums_conv[2:3], "conv_ln_g": sums_conv[0:1], "conv_ln_b": sums_conv[1:2],
        "hgrn_lb_logits": jnp.concatenate([d_l0, -d_l0], axis=1),
        "hgrn_norm_g": sums_hgrn[0:1],
        "ln1_g": sums_ln1[0:1], "ln1_b": sums_ln1[1:2],
        "b_ffn_dw": sums_ffn[:, FFN_K, :].reshape(1, D_FF),
        "ln2_g": sums_ln2[0:1], "ln2_b": sums_ln2[1:2],
        "w_conv_dw": g_w_conv_dw[0:CONV_K].reshape(1, CONV_K * CONV_DIM),
        "w_ffn_dw": jnp.transpose(sums_ffn[:, 0:FFN_K, :], (1, 0, 2)).reshape(1, FFN_K * D_FF),
    }
    large_grads = [g_w_in] + early
    if scatter is not None:
        large_grads = list(zip(large_grads, late_recv + early_recv))
    return grad_x, large_grads, small_grads


def _coords():
    return lax.axis_index("x"), lax.axis_index("y"), lax.axis_index("c")


def _gather(shards, staged=False):
    n = len(shards)
    later = range(1 if staged else 0, n)

    def parts(ins, outs, sems):
        send_sems, recv_sems, local_sems = sems
        x, y, c = _coords()
        me = 4 * x + 2 * y + c
        sibling = (x, y, 1 - c)
        chips = [(1 - x, y), (x, 1 - y), (1 - x, 1 - y)]

        def copy(a, k, block, to, src=None):
            return pltpu.make_async_remote_copy(
                src_ref=outs[a].at[block] if src is None else src, dst_ref=outs[a].at[block],
                send_sem=send_sems.at[a, k], recv_sem=recv_sems.at[a, k], device_id=to, device_id_type=MESH)

        local = [pltpu.make_async_copy(ins[a], outs[a].at[me], local_sems.at[a]) for a in range(n)]
        first = []
        for a in range(n):
            first.append(copy(a, 0, me, sibling, src=ins[a]))
            for j, chip in enumerate(chips):
                first.append(copy(a, 1 + j, me, (*chip, c), src=ins[a]))
        return x, y, c, sibling, chips, copy, local, first

    def start(ins, outs, sems):
        *_, local, first = parts(ins, outs, sems)
        for cp in local + first:
            cp.start()

    def arrive(ins, outs, sems, s):
        x, y, c, sibling, chips, copy, _, _ = parts(ins, outs, sems)
        if s == 1:
            block = 4 * x + 2 * y + 1 - c
            copy(0, 0, block, sibling).wait_recv()
        elif s <= 4:
            px, py = chips[s - 2]
            block = 4 * px + 2 * py + c
            copy(0, s - 1, block, sibling).wait_recv()
            copy(0, s + 2, block, sibling).start()
        else:
            px, py = chips[s - 5]
            block = 4 * px + 2 * py + 1 - c
            copy(0, s - 1, block, sibling).wait_recv()
        return block

    def finish(ins, outs, sems):
        x, y, c, sibling, chips, copy, local, first = parts(ins, outs, sems)
        passed = [copy(0, 4 + j, 4 * px + 2 * py + c, sibling) for j, (px, py) in enumerate(chips)] if staged else []
        for j, (px, py) in enumerate(chips):
            for a in later:
                copy(a, 1 + j, 4 * px + 2 * py + c, sibling).wait_recv()
                cp = copy(a, 4 + j, 4 * px + 2 * py + c, sibling)
                cp.start()
                passed.append(cp)
        for a in later:
            copy(a, 0, 4 * x + 2 * y + 1 - c, sibling).wait_recv()
            for j, (px, py) in enumerate(chips):
                copy(a, 4 + j, 4 * px + 2 * py + 1 - c, sibling).wait_recv()
        for cp in first + passed:
            cp.wait_send()
        for cp in local:
            cp.wait()

    hosted = _Hosted(shards, [jax.ShapeDtypeStruct((N_DEV,) + s.shape, s.dtype) for s in shards],
                     [pltpu.SemaphoreType.DMA((n, 7)), pltpu.SemaphoreType.DMA((n, 7)), pltpu.SemaphoreType.DMA((n,))],
                     start, finish)
    hosted.arrive = arrive
    return hosted


def _proj_gather(xb, gather, order):
    t, d = xb.shape
    tm = _pick(t, 2048)
    nm = t // tm
    n_in, n_out = len(gather.inputs), len(gather.out_shapes)

    def body(order_ref, x_ref, *refs):
        ins, refs = refs[:n_in], refs[n_in:]
        o_ref, outs, refs = refs[0], refs[1:1 + n_out], refs[1 + n_out:]
        w_buf, w_sem, sems = refs[0], refs[1], refs[2:]
        s, i = pl.program_id(0), pl.program_id(1)

        @pl.when((s == 0) & (i == 0))
        def _():
            gather.start(ins, outs, sems)

        for step in range(N_DEV):
            @pl.when((s == step) & (i == 0))
            def _(step=step):
                src = ins[0] if step == 0 else outs[0].at[gather.arrive(ins, outs, sems, step)]
                stage = pltpu.make_async_copy(src, w_buf, w_sem)
                stage.start()
                stage.wait()

        o_ref[...] = _dot(x_ref[...], w_buf[...], NN)

        @pl.when((s == N_DEV - 1) & (i == nm - 1))
        def _():
            gather.finish(ins, outs, sems)

    outs = pl.pallas_call(
        body, name="proj_gather",
        grid_spec=pltpu.PrefetchScalarGridSpec(
            num_scalar_prefetch=1, grid=(N_DEV, nm),
            in_specs=[pl.BlockSpec((tm, d), lambda s, i, order_ref: (i, 0))] + [ANY] * n_in,
            out_specs=[pl.BlockSpec((tm, IN_SHARD), lambda s, i, order_ref: (i, order_ref[s]))] + [ANY] * n_out,
            scratch_shapes=[pltpu.VMEM((d, IN_SHARD), BF16), pltpu.SemaphoreType.DMA] + gather.sem_shapes),
        out_shape=[jax.ShapeDtypeStruct((t, IN_COLS), F32)] + gather.out_shapes,
        compiler_params=pltpu.CompilerParams(dimension_semantics=("arbitrary", "arbitrary"),
                                             vmem_limit_bytes=VMEM_LIMIT, has_side_effects=True),
    )(order, xb, *gather.inputs)
    return outs[0], list(outs[1:])


def _scatter(grads):
    n = len(grads)

    def copies(ins, outs, sems):
        send_sems, recv_sems = sems
        x, y, c = _coords()
        out = []
        for a in range(n):
            for k in range(1, N_DEV):
                px, py, pc = x ^ (k >> 2), y ^ ((k >> 1) & 1), c ^ (k & 1)
                out.append(pltpu.make_async_remote_copy(
                    src_ref=ins[a].at[4 * px + 2 * py + pc], dst_ref=outs[a].at[k - 1],
                    send_sem=send_sems.at[a, k - 1], recv_sem=recv_sems.at[a, k - 1],
                    device_id=(px, py, pc), device_id_type=MESH))
        return out

    def start(ins, outs, sems):
        for cp in copies(ins, outs, sems):
            cp.start()

    def finish(ins, outs, sems):
        for cp in copies(ins, outs, sems):
            cp.wait()

    return _Hosted(grads, [jax.ShapeDtypeStruct((N_DEV - 1,) + g.shape[1:], g.dtype) for g in grads],
                   [pltpu.SemaphoreType.DMA((n, N_DEV - 1)), pltpu.SemaphoreType.DMA((n, N_DEV - 1))], start, finish)


def _row_tile(rows):
    return 256 if rows % 256 == 0 else rows


def _adam_math(w, g, m, v):
    m_new = ADAM_B1 * m + (1.0 - ADAM_B1) * g
    v_new = ADAM_B2 * v + (1.0 - ADAM_B2) * (g * g)
    m_hat = m_new / (1.0 - ADAM_B1 ** ADAM_STEP)
    v_hat = v_new / (1.0 - ADAM_B2 ** ADAM_STEP)
    delta = -ADAM_LR * (m_hat / (jnp.sqrt(v_hat) + ADAM_EPS) + ADAM_WD * w)
    return delta, m_new, v_new


def _adam_large(name, own, recv, me, w, m, v):
    rows, cols = w.shape
    tr = _row_tile(rows)

    def body(me_ref, p_ref, r_ref, w_ref, m_ref, v_ref, g_out, d_out, m_out, v_out):
        g = p_ref[...].astype(F32)
        for k in range(N_DEV - 1):
            g = g + r_ref[k].astype(F32)
        delta, m_new, v_new = _adam_math(w_ref[...], g, m_ref[...], v_ref[...])
        g_out[...] = g
        d_out[...] = delta
        m_out[...] = m_new
        v_out[...] = v_new

    tile = pl.BlockSpec((tr, cols), lambda r, me_ref: (r, 0))
    sds = jax.ShapeDtypeStruct((rows, cols), F32)
    return pl.pallas_call(
        body, name=name,
        grid_spec=pltpu.PrefetchScalarGridSpec(
            num_scalar_prefetch=1, grid=(rows // tr,),
            in_specs=[pl.BlockSpec((None, tr, cols), lambda r, me_ref: (me_ref[0], r, 0)),
                      pl.BlockSpec((N_DEV - 1, tr, cols), lambda r, me_ref: (0, r, 0)), tile, tile, tile],
            out_specs=[tile, tile, tile, tile]),
        out_shape=[sds, sds, sds, sds],
        compiler_params=_params(("parallel",)),
    )(me, own, recv, w, m, v)


def _small_allreduce(vec):
    rows = vec.shape[0]

    def body(v_ref, o_ref, gat_ref, send_sems, recv_sems):
        x, y, c = _coords()
        me = 4 * x + 2 * y + c
        gat_ref[me] = v_ref[...]
        copies = []
        for k in range(1, N_DEV):
            px, py, pc = x ^ (k >> 2), y ^ ((k >> 1) & 1), c ^ (k & 1)
            copies.append(pltpu.make_async_remote_copy(
                src_ref=v_ref, dst_ref=gat_ref.at[me], send_sem=send_sems.at[k - 1], recv_sem=recv_sems.at[k - 1],
                device_id=(px, py, pc), device_id_type=MESH))
        for cp in copies:
            cp.start()
        for k in range(1, N_DEV):
            px, py, pc = x ^ (k >> 2), y ^ ((k >> 1) & 1), c ^ (k & 1)
            pltpu.make_async_remote_copy(
                src_ref=v_ref, dst_ref=gat_ref.at[4 * px + 2 * py + pc], send_sem=send_sems.at[k - 1],
                recv_sem=recv_sems.at[k - 1], device_id=(px, py, pc), device_id_type=MESH).wait_recv()
        for cp in copies:
            cp.wait_send()
        acc = gat_ref[0]
        for dev in range(1, N_DEV):
            acc = acc + gat_ref[dev]
        o_ref[...] = acc

    whole = pl.BlockSpec(memory_space=pltpu.VMEM)
    return pl.pallas_call(
        body, name="small_allreduce", in_specs=[whole], out_specs=whole,
        out_shape=jax.ShapeDtypeStruct((rows, 128), F32),
        scratch_shapes=[pltpu.VMEM((N_DEV, rows, 128), F32), pltpu.SemaphoreType.DMA((N_DEV - 1,)),
                        pltpu.SemaphoreType.DMA((N_DEV - 1,))],
        compiler_params=pltpu.CompilerParams(has_side_effects=True, vmem_limit_bytes=VMEM_LIMIT),
    )(vec)


def _adam_small(w, g, m, v):
    def body(w_ref, g_ref, m_ref, v_ref, d_out, m_out, v_out):
        delta, m_new, v_new = _adam_math(w_ref[...], g_ref[...], m_ref[...], v_ref[...])
        d_out[...] = delta
        m_out[...] = m_new
        v_out[...] = v_new

    whole = pl.BlockSpec(memory_space=pltpu.VMEM)
    sds = jax.ShapeDtypeStruct(w.shape, F32)
    return pl.pallas_call(body, name="adam_small", in_specs=[whole] * 4, out_specs=[whole] * 3,
                          out_shape=[sds, sds, sds])(w, g, m, v)


_SMALL_ORDER = ["loss", "b_conv_dw", "conv_ln_g", "conv_ln_b", "hgrn_lb_logits", "hgrn_norm_g", "ln1_g", "ln1_b",
                "b_ffn_dw", "ln2_g", "ln2_b", "w_conv_dw", "w_ffn_dw"]
_WEIGHTS = ["w_in", "w_conv_dw", "b_conv_dw", "conv_ln_g", "conv_ln_b", "w_conv_out", "hgrn_lb_logits", "hgrn_norm_g",
            "w_hgrn_out", "w_out", "ln1_g", "ln1_b", "w_ffn_in", "w_ffn_dw", "b_ffn_dw", "w_ffn_out", "ln2_g", "ln2_b"]
_LARGE = ["w_in", "w_conv_out", "w_hgrn_out", "w_out", "w_ffn_in", "w_ffn_out"]
_CONV_DW_SHARD = CONV_DIM // N_DEV
_FFN_DW_SHARD = D_FF // N_DEV


def kernel(x, w_in, w_conv_dw, b_conv_dw, conv_ln_g, conv_ln_b, w_conv_out, hgrn_lb_logits, hgrn_norm_g, w_hgrn_out, w_out, ln1_g, ln1_b, w_ffn_in, w_ffn_dw, b_ffn_dw, w_ffn_out, ln2_g, ln2_b, loss_target, m_w_in, m_w_conv_dw, m_b_conv_dw, m_conv_ln_g, m_conv_ln_b, m_w_conv_out, m_hgrn_lb_logits, m_hgrn_norm_g, m_w_hgrn_out, m_w_out, m_ln1_g, m_ln1_b, m_w_ffn_in, m_w_ffn_dw, m_b_ffn_dw, m_w_ffn_out, m_ln2_g, m_ln2_b, v_w_in, v_w_conv_dw, v_b_conv_dw, v_conv_ln_g, v_conv_ln_b, v_w_conv_out, v_hgrn_lb_logits, v_hgrn_norm_g, v_w_hgrn_out, v_w_out, v_ln1_g, v_ln1_b, v_w_ffn_in, v_w_ffn_dw, v_b_ffn_dw, v_w_ffn_out, v_ln2_g, v_ln2_b):
    w = dict(w_in=w_in, w_conv_dw=w_conv_dw, b_conv_dw=b_conv_dw, conv_ln_g=conv_ln_g, conv_ln_b=conv_ln_b,
             w_conv_out=w_conv_out, hgrn_lb_logits=hgrn_lb_logits, hgrn_norm_g=hgrn_norm_g, w_hgrn_out=w_hgrn_out,
             w_out=w_out, ln1_g=ln1_g, ln1_b=ln1_b, w_ffn_in=w_ffn_in, w_ffn_dw=w_ffn_dw, b_ffn_dw=b_ffn_dw,
             w_ffn_out=w_ffn_out, ln2_g=ln2_g, ln2_b=ln2_b)
    m = dict(w_in=m_w_in, w_conv_dw=m_w_conv_dw, b_conv_dw=m_b_conv_dw, conv_ln_g=m_conv_ln_g, conv_ln_b=m_conv_ln_b,
             w_conv_out=m_w_conv_out, hgrn_lb_logits=m_hgrn_lb_logits, hgrn_norm_g=m_hgrn_norm_g,
             w_hgrn_out=m_w_hgrn_out, w_out=m_w_out, ln1_g=m_ln1_g, ln1_b=m_ln1_b, w_ffn_in=m_w_ffn_in,
             w_ffn_dw=m_w_ffn_dw, b_ffn_dw=m_b_ffn_dw, w_ffn_out=m_w_ffn_out, ln2_g=m_ln2_g, ln2_b=m_ln2_b)
    v = dict(w_in=v_w_in, w_conv_dw=v_w_conv_dw, b_conv_dw=v_b_conv_dw, conv_ln_g=v_conv_ln_g, conv_ln_b=v_conv_ln_b,
             w_conv_out=v_w_conv_out, hgrn_lb_logits=v_hgrn_lb_logits, hgrn_norm_g=v_hgrn_norm_g,
             w_hgrn_out=v_w_hgrn_out, w_out=v_w_out, ln1_g=v_ln1_g, ln1_b=v_ln1_b, w_ffn_in=v_w_ffn_in,
             w_ffn_dw=v_w_ffn_dw, b_ffn_dw=v_b_ffn_dw, w_ffn_out=v_w_ffn_out, ln2_g=v_ln2_g, ln2_b=v_ln2_b)
    xi, yi, ci = lax.axis_index("x"), lax.axis_index("y"), lax.axis_index("c")
    me = 4 * xi + 2 * yi + ci
    me_op = jnp.reshape(me, (1,)).astype(jnp.int32)

    shards = [w[name][0].astype(BF16) for name in _LARGE]
    shards.append(jnp.pad(w_conv_dw[0], ((0, 1), (0, 128 - _CONV_DW_SHARD))))
    shards.append(jnp.pad(w_ffn_dw[0], ((0, 8 - FFN_K), (0, 384 - _FFN_DW_SHARD))))
    chips = [(1 - xi, yi), (xi, 1 - yi), (1 - xi, 1 - yi)]
    order = jnp.stack([me, me ^ 1] + [4 * px + 2 * py + ci for px, py in chips]
                      + [4 * px + 2 * py + 1 - ci for px, py in chips]).astype(jnp.int32)
    small = dict(b_conv_dw=b_conv_dw, conv_ln_g=conv_ln_g, conv_ln_b=conv_ln_b, hgrn_lb_logits=hgrn_lb_logits,
                 hgrn_norm_g=hgrn_norm_g, ln1_g=ln1_g, ln1_b=ln1_b, ln2_g=ln2_g, ln2_b=ln2_b, b_ffn_dw=b_ffn_dw)

    grad_x, large_grads, small_grads = _local_step(x[0], loss_target[0], _gather(shards, staged=True), small,
                                                   _scatter, order)

    out = {}
    for name, (own, recv) in zip(_LARGE, large_grads):
        out[name] = _adam_large("adam_" + name, own, recv, me_op, w[name][0], m[name][0], v[name][0])

    vec = jnp.concatenate([small_grads[name] for name in _SMALL_ORDER], axis=1)
    total = _small_allreduce(vec.reshape(-1, 128)).reshape(1, -1)
    sizes = [small_grads[name].shape[1] for name in _SMALL_ORDER]
    offs = [0]
    for s in sizes:
        offs.append(offs[-1] + s)
    summed = {name: total[:, offs[i]:offs[i + 1]] for i, name in enumerate(_SMALL_ORDER)}
    loss = summed["loss"][0, 0]
    conv_dw_g = lax.dynamic_slice_in_dim(summed["w_conv_dw"].reshape(CONV_K, CONV_DIM), me * _CONV_DW_SHARD, _CONV_DW_SHARD, axis=1)
    ffn_dw_g = lax.dynamic_slice_in_dim(summed["w_ffn_dw"].reshape(FFN_K, D_FF), me * _FFN_DW_SHARD, _FFN_DW_SHARD, axis=1)
    small_g = dict(summed, w_conv_dw=conv_dw_g.reshape(1, -1), w_ffn_dw=ffn_dw_g.reshape(1, -1))
    names = [n for n in _SMALL_ORDER if n != "loss"]
    flat = lambda d, n: d[n].reshape(1, -1)
    n_small = sum(small_g[n].shape[1] for n in names)
    pad = (-n_small) % 1024
    pack = lambda pieces: jnp.pad(jnp.concatenate(pieces, axis=1), ((0, 0), (0, pad))).reshape(-1, 128)
    d_s, m_s, v_s = _adam_small(pack([flat(w, n) for n in names]), pack([small_g[n] for n in names]),
                                pack([flat(m, n) for n in names]), pack([flat(v, n) for n in names]))
    pos = 0
    for n in names:
        size = small_g[n].shape[1]
        cut = lambda a: a.reshape(1, -1)[:, pos:pos + size].reshape(w[n].shape)
        out[n] = (small_g[n].reshape(w[n].shape), cut(d_s), cut(m_s), cut(v_s))
        pos += size

    for name in _LARGE:
        out[name] = tuple(a.reshape(w[name].shape) for a in out[name])
    grads = [out[n][0] for n in _WEIGHTS]
    deltas = [out[n][1] for n in _WEIGHTS]
    new_m = [out[n][2] for n in _WEIGHTS]
    new_v = [out[n][3] for n in _WEIGHTS]
    return (loss, grad_x[None], *grads, *deltas, *new_m, *new_v)
```

```python
import functools
import math

import jax
import jax.numpy as jnp
from jax import lax
from jax.experimental import pallas as pl
from jax.experimental.pallas import tpu as pltpu

F32 = jnp.float32
BF16 = jnp.bfloat16

N_DEV = 8
D_MODEL = 1024
CONV_DIM = 512
CONV_K = 31
HGRN_DIM = 1024
HEADS = 8
HEAD_DIM = 128
D_FF = 2816
FFN_K = 3
FF_SHARD = 2 * D_FF // N_DEV
IN_COLS = 7168
IN_SHARD = IN_COLS // N_DEV
LN_EPS = 1e-5
RMS_EPS = 1e-6
ALPHA = 2.0 ** 0.25

ADAM_LR = 0.001
ADAM_B1 = 0.9
ADAM_B2 = 0.999
ADAM_EPS = 1e-08
ADAM_WD = 0.01
ADAM_STEP = 10

CHUNK = 64
CHUNKS_PER_BLOCK = 16
CONV_HALO = 32
FFN_HALO = 8
ROW_BLOCK = 64
SUBLANES = 8
VMEM_LIMIT = 48 * 1024 * 1024
MXU_DEPTH = 256

MESH = pl.DeviceIdType.MESH
ANY = pl.BlockSpec(memory_space=pl.ANY)

NN = (((1,), (0,)), ((), ()))
NT = (((1,), (1,)), ((), ()))
TN = (((0,), (0,)), ((), ()))


def _params(sem):
    return pltpu.CompilerParams(dimension_semantics=sem, vmem_limit_bytes=VMEM_LIMIT)


def _dot(a, b, dims):
    return lax.dot_general(a.astype(BF16), b.astype(BF16), dims, preferred_element_type=F32)


def _sigmoid(x):
    return jax.nn.sigmoid(x)


def _ln(r):
    mu = jnp.mean(r, axis=-1, keepdims=True)
    xc = r - mu
    var = jnp.mean(xc * xc, axis=-1, keepdims=True)
    rstd = lax.rsqrt(var + LN_EPS)
    return xc * rstd, rstd


def _ln_bwd(dy, xhat, rstd, g):
    dxh = dy * g
    m1 = jnp.mean(dxh, axis=-1, keepdims=True)
    m2 = jnp.mean(dxh * xhat, axis=-1, keepdims=True)
    return rstd * (dxh - m1 - xhat * m2)


def _colsum(x):
    return jnp.sum(x, axis=0, keepdims=True)


class _Hosted:
    def __init__(self, inputs, out_shapes, sem_shapes, start, finish):
        self.inputs, self.out_shapes, self.sem_shapes = list(inputs), list(out_shapes), list(sem_shapes)
        self.start, self.finish = start, finish


def _call(body, *, name, grid, in_specs, out_specs, out_shape, scratch_shapes, semantics, operands, hosted=None):
    if hosted is None:
        return pl.pallas_call(
            body, name=name, grid=grid, in_specs=list(in_specs), out_specs=list(out_specs), out_shape=list(out_shape),
            scratch_shapes=list(scratch_shapes), compiler_params=_params(semantics))(*operands)
    n_in, n_out, n_scr = len(in_specs), len(out_specs), len(scratch_shapes)
    h_in, h_out = len(hosted.inputs), len(hosted.out_shapes)

    def full_body(*refs):
        ins, refs = refs[:n_in], refs[n_in:]
        h_ins, refs = refs[:h_in], refs[h_in:]
        outs, refs = refs[:n_out], refs[n_out:]
        h_outs, refs = refs[:h_out], refs[h_out:]
        scr, sems = refs[:n_scr], refs[n_scr:]
        first = functools.reduce(jnp.logical_and, [pl.program_id(d) == 0 for d in range(len(grid))])
        last = functools.reduce(jnp.logical_and, [pl.program_id(d) == grid[d] - 1 for d in range(len(grid))])

        @pl.when(first)
        def _():
            hosted.start(h_ins, h_outs, sems)

        body(*ins, *outs, *scr)

        @pl.when(last)
        def _():
            hosted.finish(h_ins, h_outs, sems)

    return pl.pallas_call(
        full_body, name=name, grid=grid, in_specs=list(in_specs) + [ANY] * h_in,
        out_specs=list(out_specs) + [ANY] * h_out, out_shape=list(out_shape) + hosted.out_shapes,
        scratch_shapes=list(scratch_shapes) + hosted.sem_shapes,
        compiler_params=pltpu.CompilerParams(dimension_semantics=("arbitrary",) * len(grid),
                                             vmem_limit_bytes=VMEM_LIMIT, has_side_effects=True),
    )(*operands, *hosted.inputs)


def _mm(name, a, b, out_shape, out_dtype, grid, a_spec, b_spec, o_spec, dims, acc_shape, hosted=None):
    nk = grid[2]
    if nk == 1:
        def body(a_ref, b_ref, o_ref):
            o_ref[...] = _dot(a_ref[...], b_ref[...], dims).astype(o_ref.dtype)
        scratch = []
    else:
        def body(a_ref, b_ref, o_ref, acc_ref):
            k = pl.program_id(2)

            @pl.when(k == 0)
            def _():
                acc_ref[...] = jnp.zeros_like(acc_ref)

            acc_ref[...] += _dot(a_ref[...], b_ref[...], dims)

            @pl.when(k == nk - 1)
            def _():
                o_ref[...] = acc_ref[...].astype(o_ref.dtype)
        scratch = [pltpu.VMEM(acc_shape, F32)]

    outs = _call(body, name=name, grid=grid, in_specs=[a_spec, b_spec], out_specs=[o_spec],
                 out_shape=[jax.ShapeDtypeStruct(out_shape, out_dtype)], scratch_shapes=scratch,
                 semantics=("parallel", "parallel", "arbitrary"), operands=(a, b), hosted=hosted)
    return outs[0] if hosted is None else (outs[0], list(outs[1:]))


def _mm_fused(name, grid, operands, in_specs, out_shape, out_specs, dims, acc_shape, epilogue, lhs=None, scratch=(),
              hosted=None):
    nk = grid[2]
    n_in, n_out = len(in_specs), len(out_specs)

    def body(*refs):
        ins, outs, scr = refs[:n_in], refs[n_in:n_in + n_out], refs[n_in + n_out:]
        acc_ref, k = scr[0], pl.program_id(2)
        a = ins[0][...] if lhs is None else lhs(ins, outs)
        part = _dot(a, ins[1][...], dims)
        if nk == 1:
            acc_ref[...] = part
            epilogue(acc_ref, ins, outs, scr[1:])
            return

        @pl.when(k == 0)
        def _():
            acc_ref[...] = jnp.zeros_like(acc_ref)

        acc_ref[...] += part

        @pl.when(k == nk - 1)
        def _():
            epilogue(acc_ref, ins, outs, scr[1:])

    return _call(body, name=name, grid=grid, in_specs=in_specs, out_specs=out_specs, out_shape=out_shape,
                 scratch_shapes=[pltpu.VMEM(acc_shape, F32)] + list(scratch), semantics=("arbitrary",) * 3,
                 operands=operands, hosted=hosted)


def _row_blocks(rows, block=256):
    block = block if rows % block == 0 else rows
    return [slice(r, r + block) for r in range(0, rows, block)]


def _pick(t, pref):
    return pref if t % pref == 0 else t


def _glu(p):
    return p[:, :CONV_DIM] * _sigmoid(p[:, CONV_DIM:])


def _by_phase(taps):
    phases = {}
    for off, payload in taps:
        phases.setdefault(off % SUBLANES, []).append((off - off % SUBLANES, payload))
    return sorted(phases.items())


def _tap_sum(src_ref, base, taps, rows, lanes):
    acc = None
    for phase, items in _by_phase(taps):
        n = rows if phase == 0 else rows + SUBLANES
        part = None
        for off, (w_ref, k) in items:
            term = w_ref[k:k + 1, lanes] * src_ref[base + off:base + off + n, lanes]
            part = term if part is None else part + term
        if phase:
            part = part[phase:phase + rows, :]
        acc = part if acc is None else acc + part
    return acc


def _tap_products(x, src_ref, base, taps, lanes):
    rows, cols = x.shape
    pad = jnp.zeros((SUBLANES, cols), x.dtype)
    padded = jnp.concatenate([pad, x, pad], axis=0)
    out = []
    for phase, items in _by_phase(taps):
        n = rows if phase == 0 else rows + SUBLANES
        shifted = x if phase == 0 else padded[SUBLANES - phase:SUBLANES - phase + n, :]
        for off, key in items:
            out.append((key, _colsum(shifted * src_ref[base + off:base + off + n, lanes])))
    return out


def _lane_blocks(cols, block=256):
    return [slice(c, min(c + block, cols)) for c in range(0, cols, block)]


def _conv_fwd(proj, w_dw, b_dw, g, b):
    t = proj.shape[0]
    tm = _pick(t, 512)
    nh = tm // CONV_HALO

    def body(p_ref, ph_ref, w_ref, bd_ref, g_ref, b_ref, act_ref, pre_ref, xs_ref):
        i = pl.program_id(0)
        halo = _glu(ph_ref[...])
        xs_ref[0:CONV_HALO, :] = jnp.where(i == 0, 0.0, halo)
        xs_ref[CONV_HALO:CONV_HALO + tm, :] = _glu(p_ref[...])
        taps = [(CONV_HALO - (CONV_K - 1) + k, (w_ref, k)) for k in range(CONV_K)]
        for r in range(tm // ROW_BLOCK):
            rows = slice(r * ROW_BLOCK, (r + 1) * ROW_BLOCK)
            for lanes in _lane_blocks(CONV_DIM):
                pre_ref[rows, lanes] = bd_ref[:, lanes] + _tap_sum(xs_ref, r * ROW_BLOCK, taps, ROW_BLOCK, lanes)
            acc = pre_ref[rows, :]
            xhat, _ = _ln(acc)
            yln = xhat * g_ref[...] + b_ref[...]
            act_ref[rows, :] = (yln * _sigmoid(yln)).astype(BF16)

    full = lambda s: pl.BlockSpec(s, lambda i: (0, 0))
    return pl.pallas_call(
        body, name="conv_fwd", grid=(t // tm,),
        in_specs=[pl.BlockSpec((tm, 2 * CONV_DIM), lambda i: (i, 0)),
                  pl.BlockSpec((CONV_HALO, 2 * CONV_DIM), lambda i: (jnp.maximum(i * nh - 1, 0), 0)),
                  full((CONV_K, CONV_DIM)), full((1, CONV_DIM)), full((1, CONV_DIM)), full((1, CONV_DIM))],
        out_specs=[pl.BlockSpec((tm, CONV_DIM), lambda i: (i, 0)), pl.BlockSpec((tm, CONV_DIM), lambda i: (i, 0))],
        out_shape=[jax.ShapeDtypeStruct((t, CONV_DIM), BF16), jax.ShapeDtypeStruct((t, CONV_DIM), F32)],
        scratch_shapes=[pltpu.VMEM((CONV_HALO + tm, CONV_DIM), F32)],
        compiler_params=_params(("arbitrary",)),
    )(proj, proj, w_dw, b_dw, g, b)


def _d_c_norm_bwd(d_y, w_conv_out, pre, g, b):
    t = pre.shape[0]
    tm = _pick(t, 512)
    d = D_MODEL

    def epilogue(d_c, ins, outs, scr):
        pre_ref, g_ref, b_ref = ins[2:]
        dpre_ref, sums_ref = outs
        i = pl.program_id(0)

        @pl.when(i == 0)
        def _():
            sums_ref[...] = jnp.zeros_like(sums_ref)

        for rows in _row_blocks(tm):
            xhat, rstd = _ln(pre_ref[rows, :])
            yln = xhat * g_ref[...] + b_ref[...]
            sg = _sigmoid(yln)
            dyln = d_c[rows, :] * (sg * (1.0 + yln * (1.0 - sg)))
            dpre = _ln_bwd(dyln, xhat, rstd, g_ref[...])
            dpre_ref[rows, :] = dpre
            sums_ref[0:1, :] += _colsum(dyln * xhat)
            sums_ref[1:2, :] += _colsum(dyln)
            sums_ref[2:3, :] += _colsum(dpre)

    full = lambda s: pl.BlockSpec(s, lambda i, j, k: (0, 0))
    tile = pl.BlockSpec((tm, CONV_DIM), lambda i, j, k: (i, 0))
    return _mm_fused(
        "d_c_norm_bwd", (t // tm, 1, 1), (d_y, w_conv_out, pre, g, b),
        [pl.BlockSpec((None, tm, d), lambda i, j, k: (0, i, 0)), full((CONV_DIM, d)), tile,
         full((1, CONV_DIM)), full((1, CONV_DIM))],
        [jax.ShapeDtypeStruct((t, CONV_DIM), F32), jax.ShapeDtypeStruct((8, CONV_DIM), F32)],
        [tile, full((8, CONV_DIM))], NT, (tm, CONV_DIM), epilogue)


def _conv_bwd_dw(d_pre, proj, w_dw):
    t = d_pre.shape[0]
    tm = _pick(t, 512)
    nt = t // tm
    nh = tm // CONV_HALO
    last_h = t // CONV_HALO - 1

    def body(dp_ref, dph_ref, p_ref, ph_ref, w_ref, dproj_ref, dw_ref, xs_ref, ds_ref):
        i = pl.program_id(0)

        @pl.when(i == 0)
        def _():
            dw_ref[...] = jnp.zeros_like(dw_ref)

        halo = _glu(ph_ref[...])
        xs_ref[0:CONV_HALO, :] = jnp.where(i == 0, 0.0, halo)
        xs_ref[CONV_HALO:CONV_HALO + tm, :] = _glu(p_ref[...])
        ds_ref[0:tm, :] = dp_ref[...]
        ds_ref[tm:tm + CONV_HALO, :] = jnp.where(i == nt - 1, 0.0, dph_ref[...])
        back_taps = [(CONV_K - 1 - k, (w_ref, k)) for k in range(CONV_K)]
        grad_taps = [(CONV_HALO - (CONV_K - 1) + k, k) for k in range(CONV_K)]
        for r in range(tm // ROW_BLOCK):
            base = r * ROW_BLOCK
            rows = slice(base, base + ROW_BLOCK)
            for lanes in _lane_blocks(CONV_DIM):
                gate_lanes = slice(CONV_DIM + lanes.start, CONV_DIM + lanes.stop)
                acc = _tap_sum(ds_ref, base, back_taps, ROW_BLOCK, lanes)
                for k, total in _tap_products(ds_ref[rows, lanes], xs_ref, base, grad_taps, lanes):
                    dw_ref[k:k + 1, lanes] += total
                cval = p_ref[rows, lanes]
                sg = _sigmoid(p_ref[rows, gate_lanes])
                dproj_ref[rows, lanes] = (acc * sg).astype(BF16)
                dproj_ref[rows, gate_lanes] = (acc * cval * sg * (1.0 - sg)).astype(BF16)

    full = lambda s: pl.BlockSpec(s, lambda i: (0, 0))
    return pl.pallas_call(
        body, name="conv_bwd_dw", grid=(nt,),
        in_specs=[pl.BlockSpec((tm, CONV_DIM), lambda i: (i, 0)),
                  pl.BlockSpec((CONV_HALO, CONV_DIM), lambda i: (jnp.minimum((i + 1) * nh, last_h), 0)),
                  pl.BlockSpec((tm, 2 * CONV_DIM), lambda i: (i, 0)),
                  pl.BlockSpec((CONV_HALO, 2 * CONV_DIM), lambda i: (jnp.maximum(i * nh - 1, 0), 0)),
                  full((CONV_K, CONV_DIM))],
        out_specs=[pl.BlockSpec((tm, 2 * CONV_DIM), lambda i: (i, 0)), full((CONV_HALO, CONV_DIM))],
        out_shape=[jax.ShapeDtypeStruct((t, 2 * CONV_DIM), BF16), jax.ShapeDtypeStruct((CONV_HALO, CONV_DIM), F32)],
        scratch_shapes=[pltpu.VMEM((CONV_HALO + tm, CONV_DIM), F32), pltpu.VMEM((tm + CONV_HALO, CONV_DIM), F32)],
        compiler_params=_params(("arbitrary",)),
    )(d_pre, d_pre, proj, proj, w_dw)


def _lower_bound(logit_ref):
    l0 = logit_ref[0:1, :]
    l1 = logit_ref[1:2, :]
    m = jnp.maximum(l0, l1)
    e0 = jnp.exp(l0 - m)
    e1 = jnp.exp(l1 - m)
    return e0 / (e0 + e1)


def _tri(lower):
    r = lax.broadcasted_iota(jnp.int32, (CHUNK, CHUNK), 0)
    c = lax.broadcasted_iota(jnp.int32, (CHUNK, CHUNK), 1)
    return (c <= r) if lower else (c >= r)


def _hgrn_gates(fz, lb):
    s = _sigmoid(fz)
    sn = _sigmoid(-fz)
    f = lb + (1.0 - lb) * s
    return s, sn, f


def _block_tri(rows, lower=True):
    r = lax.broadcasted_iota(jnp.int32, (rows, rows), 0)
    c = lax.broadcasted_iota(jnp.int32, (rows, rows), 1)
    tri = (c <= r) if lower else (c >= r)
    return (tri & (r // CHUNK == c // CHUNK)).astype(BF16)


def _tri_rows(tm):
    return min(tm, MXU_DEPTH)


def _tri_matmul(tri_ref, x):
    hi = x.astype(BF16)
    lo = (x - hi.astype(F32)).astype(BF16)
    tri = tri_ref[...]
    return (lax.dot_general(tri, hi, NN, preferred_element_type=F32)
            + lax.dot_general(tri, lo, NN, preferred_element_type=F32))


def _groups(tm):
    g = _tri_rows(tm)
    return [slice(i * g, (i + 1) * g) for i in range(tm // g)]


def _hgrn_fwd(proj, logits, norm_g):
    t = proj.shape[0]
    tm = CHUNK * CHUNKS_PER_BLOCK if t % (CHUNK * CHUNKS_PER_BLOCK) == 0 else CHUNK
    cpb = tm // CHUNK
    nt = t // tm
    half = CHUNK // 2

    def body(qz_ref, fz_ref, iv_ref, gz_ref, lg_ref, ng_ref, tri_ref, o_ref, og_ref, st_ref,
             state_ref, qe_ref, ke_ref, qb_ref, kl_ref, v_ref, upd_ref, decay_ref, a_ref, q_ref, kk_ref, b_ref):
        j = pl.program_id(1)

        @pl.when(j == 0)
        def _():
            state_ref[...] = jnp.zeros_like(state_ref)

        lb = _lower_bound(lg_ref)
        chunks = [slice(c * CHUNK, (c + 1) * CHUNK) for c in range(cpb)]
        for rows in chunks:
            qz = qz_ref[rows, :]
            q_ref[rows, :] = qz * _sigmoid(qz)
            _, sn, f = _hgrn_gates(fz_ref[rows, :], lb)
            kk_ref[rows, :] = (1.0 - lb) * sn
            b_ref[rows, :] = jnp.log(f)
            v_ref[rows, :] = iv_ref[rows, :].astype(BF16)
        for rows in _groups(tm):
            b_ref[rows, :] = _tri_matmul(tri_ref, b_ref[rows, :])
        for c, rows in enumerate(chunks):
            b = b_ref[rows, :]
            bref = b[half - 1:half, :]
            blast = b[CHUNK - 1:CHUNK, :]
            q = q_ref[rows, :]
            kk = kk_ref[rows, :]
            qb_ref[rows, :] = (q * jnp.exp(b)).astype(BF16)
            qe_ref[rows, :] = (q * jnp.exp(b - bref)).astype(BF16)
            ke_ref[rows, :] = (kk * jnp.exp(bref - b)).astype(BF16)
            kl_ref[rows, :] = (kk * jnp.exp(blast - b)).astype(BF16)
            decay_ref[c:c + 1, :] = jnp.exp(blast)
        causal = _tri(True)
        for c, rows in enumerate(chunks):
            upd_ref[c] = _dot(v_ref[rows, :], kl_ref[rows, :], TN)
            a_ref[c] = jnp.where(causal, _dot(qe_ref[rows, :], ke_ref[rows, :], NT), 0.0).astype(BF16)
        state = state_ref[...]
        for c in range(cpb):
            st_ref[c] = state.astype(BF16)
            state = state * decay_ref[c:c + 1, :] + upd_ref[c]
        state_ref[...] = state
        for c, rows in enumerate(chunks):
            o_ref[rows, :] = _dot(a_ref[c], v_ref[rows, :], NN) + _dot(qb_ref[rows, :], st_ref[c], NT)
        for rows in chunks:
            o = o_ref[rows, :]
            r = lax.rsqrt(jnp.mean(o * o, axis=-1, keepdims=True) + RMS_EPS)
            gz = gz_ref[rows, :]
            og_ref[rows, :] = ((o * r * ng_ref[...]) * (gz * _sigmoid(gz))).astype(BF16)

    col = lambda base: pl.BlockSpec((tm, HEAD_DIM), lambda h, j: (j, base + h))
    tile_bf = pltpu.VMEM((tm, HEAD_DIM), BF16)
    tile_f32 = pltpu.VMEM((tm, HEAD_DIM), F32)
    return pl.pallas_call(
        body, name="hgrn_fwd", grid=(HEADS, nt),
        in_specs=[col(8), col(16), col(24), col(32),
                  pl.BlockSpec((2, HEAD_DIM), lambda h, j: (0, h)), pl.BlockSpec((1, HEAD_DIM), lambda h, j: (0, h)),
                  pl.BlockSpec((_tri_rows(tm), _tri_rows(tm)), lambda h, j: (0, 0))],
        out_specs=[col(0), col(0), pl.BlockSpec((None, cpb, HEAD_DIM, HEAD_DIM), lambda h, j: (h, j, 0, 0))],
        out_shape=[jax.ShapeDtypeStruct((t, HGRN_DIM), F32), jax.ShapeDtypeStruct((t, HGRN_DIM), BF16),
                   jax.ShapeDtypeStruct((HEADS, t // CHUNK, HEAD_DIM, HEAD_DIM), BF16)],
        scratch_shapes=[pltpu.VMEM((HEAD_DIM, HEAD_DIM), F32), tile_bf, tile_bf, tile_bf, tile_bf, tile_bf,
                        pltpu.VMEM((cpb, HEAD_DIM, HEAD_DIM), F32), pltpu.VMEM((max(cpb, 8), HEAD_DIM), F32),
                        pltpu.VMEM((cpb, CHUNK, CHUNK), BF16), tile_f32, tile_f32, tile_f32],
        compiler_params=_params(("parallel", "arbitrary")),
    )(proj, proj, proj, proj, logits, norm_g, _block_tri(_tri_rows(tm)))


def _hgrn_bwd(d_og, o, proj, states, logits, norm_g, hosted=None):
    t = proj.shape[0]
    tm = CHUNK * CHUNKS_PER_BLOCK if t % (CHUNK * CHUNKS_PER_BLOCK) == 0 else CHUNK
    cpb = tm // CHUNK
    nt = t // tm
    half = CHUNK // 2

    def body(dog_ref, o_ref, qz_ref, fz_ref, iv_ref, gz_ref, st_ref, lg_ref, ng_ref, tril_ref, triu_ref,
             dqz_ref, dfz_ref, div_ref, dgz_ref, sums_ref,
             dstate_ref, qe_ref, ke_ref, qb_ref, kl_ref, v_ref, do_ref, upd_ref, dst_ref, a_ref, da_ref,
             decay_ref, through_ref, q_ref, kk_ref, b_ref, dsilu_ref, gs_ref, gf_ref, sn_ref,
             eb_ref, ebr_ref, ekr_ref, ebl_ref, rev_ref, pre_ref, dk_ref):
        j = pl.program_id(1)

        @pl.when(j == 0)
        def _():
            dstate_ref[...] = jnp.zeros_like(dstate_ref)
            sums_ref[...] = jnp.zeros_like(sums_ref)

        lb = _lower_bound(lg_ref)
        ng = ng_ref[...]
        chunks = [slice(c * CHUNK, (c + 1) * CHUNK) for c in range(cpb)]
        for rows in chunks:
            qz = qz_ref[rows, :]
            sq = _sigmoid(qz)
            q_ref[rows, :] = qz * sq
            dsilu_ref[rows, :] = sq * (1.0 + qz * (1.0 - sq))
            s, sn, f = _hgrn_gates(fz_ref[rows, :], lb)
            kk_ref[rows, :] = (1.0 - lb) * sn
            b_ref[rows, :] = jnp.log(f)
            sn_ref[rows, :] = sn
            gf_ref[rows, :] = sn / f
            gs_ref[rows, :] = (1.0 - lb) * s
            v_ref[rows, :] = iv_ref[rows, :].astype(BF16)
            ov = o_ref[rows, :]
            r = lax.rsqrt(jnp.mean(ov * ov, axis=-1, keepdims=True) + RMS_EPS)
            on = ov * r
            gz = gz_ref[rows, :]
            sg = _sigmoid(gz)
            dog = dog_ref[rows, :]
            dgz_ref[rows, :] = (dog * (on * ng) * (sg * (1.0 + gz * (1.0 - sg)))).astype(BF16)
            d_ong = dog * (gz * sg)
            sums_ref[0:1, :] += _colsum(d_ong * on)
            d_on = d_ong * ng
            do_ref[rows, :] = (r * (d_on - on * jnp.mean(d_on * on, axis=-1, keepdims=True))).astype(BF16)
        for rows in _groups(tm):
            b_ref[rows, :] = _tri_matmul(tril_ref, b_ref[rows, :])
        for c, rows in enumerate(chunks):
            b = b_ref[rows, :]
            bref = b[half - 1:half, :]
            blast = b[CHUNK - 1:CHUNK, :]
            q = q_ref[rows, :]
            kk = kk_ref[rows, :]
            eb = jnp.exp(b)
            ebr = jnp.exp(b - bref)
            ekr = jnp.exp(bref - b)
            ebl = jnp.exp(blast - b)
            eb_ref[rows, :] = eb
            ebr_ref[rows, :] = ebr
            ekr_ref[rows, :] = ekr
            ebl_ref[rows, :] = ebl
            qb_ref[rows, :] = (q * eb).astype(BF16)
            qe_ref[rows, :] = (q * ebr).astype(BF16)
            ke_ref[rows, :] = (kk * ekr).astype(BF16)
            kl_ref[rows, :] = (kk * ebl).astype(BF16)
            decay_ref[c:c + 1, :] = jnp.exp(blast)
        causal = _tri(True)
        for c, rows in enumerate(chunks):
            upd_ref[c] = _dot(do_ref[rows, :], qb_ref[rows, :], TN)
            a_ref[c] = jnp.where(causal, _dot(qe_ref[rows, :], ke_ref[rows, :], NT), 0.0).astype(BF16)
            da_ref[c] = jnp.where(causal, _dot(do_ref[rows, :], v_ref[rows, :], NT), 0.0).astype(BF16)
        dstate = dstate_ref[...]
        for c in reversed(range(cpb)):
            dst_ref[c] = dstate.astype(BF16)
            decay = decay_ref[c:c + 1, :]
            through_ref[c:c + 1, :] = decay * _colsum(dstate * st_ref[c].astype(F32))
            dstate = dstate * decay + upd_ref[c]
        dstate_ref[...] = dstate
        for c, rows in enumerate(chunks):
            div_ref[rows, :] = (_dot(a_ref[c], do_ref[rows, :], TN)
                                + _dot(kl_ref[rows, :], dst_ref[c], NT)).astype(BF16)
        for c, rows in enumerate(chunks):
            dqe = _dot(da_ref[c], ke_ref[rows, :], NN)
            dq_inter = _dot(do_ref[rows, :], st_ref[c], NN) * eb_ref[rows, :]
            dqz_ref[rows, :] = ((dqe * ebr_ref[rows, :] + dq_inter) * dsilu_ref[rows, :]).astype(BF16)
            rev_ref[rows, :] = qe_ref[rows, :].astype(F32) * dqe + q_ref[rows, :] * dq_inter
        for c, rows in enumerate(chunks):
            dke = _dot(da_ref[c], qe_ref[rows, :], TN)
            dk_inter = _dot(v_ref[rows, :], dst_ref[c], NN) * ebl_ref[rows, :]
            dk_ref[rows, :] = dke * ekr_ref[rows, :] + dk_inter
            rev_ref[rows, :] -= ke_ref[rows, :].astype(F32) * dke
            pre_ref[rows, :] = kk_ref[rows, :] * dk_inter
        for rows in _groups(tm):
            pre = pre_ref[rows, :]
            rev_ref[rows, :] = _tri_matmul(triu_ref, rev_ref[rows, :]) + (_tri_matmul(tril_ref, pre) - pre)
        for c, rows in enumerate(chunks):
            dlf = rev_ref[rows, :] + through_ref[c:c + 1, :]
            common = gf_ref[rows, :] * dlf - sn_ref[rows, :] * dk_ref[rows, :]
            dfz_ref[rows, :] = (gs_ref[rows, :] * common).astype(BF16)
            sums_ref[1:2, :] += _colsum(common)

        @pl.when(j == nt - 1)
        def _():
            sums_ref[1:2, :] = sums_ref[1:2, :] * lb * (1.0 - lb)

    rev = lambda base: pl.BlockSpec((tm, HEAD_DIM), lambda h, j: (nt - 1 - j, base + h))
    vec = lambda n: pl.BlockSpec((n, HEAD_DIM), lambda h, j: (0, h))
    const = pl.BlockSpec((_tri_rows(tm), _tri_rows(tm)), lambda h, j: (0, 0))
    bf = jax.ShapeDtypeStruct((t, HGRN_DIM), BF16)
    tile_bf = pltpu.VMEM((tm, HEAD_DIM), BF16)
    tile_f32 = pltpu.VMEM((tm, HEAD_DIM), F32)
    square = lambda dtype: pltpu.VMEM((cpb, HEAD_DIM, HEAD_DIM), dtype)
    rows8 = pltpu.VMEM((max(cpb, 8), HEAD_DIM), F32)
    return _call(
        body, name="hgrn_bwd", grid=(HEADS, nt),
        in_specs=[rev(0), rev(0), rev(8), rev(16), rev(24), rev(32),
                  pl.BlockSpec((None, cpb, HEAD_DIM, HEAD_DIM), lambda h, j: (h, nt - 1 - j, 0, 0)),
                  vec(2), vec(1), const, const],
        out_specs=[rev(0), rev(0), rev(0), rev(0), vec(8)],
        out_shape=[bf, bf, bf, bf, jax.ShapeDtypeStruct((8, HGRN_DIM), F32)],
        scratch_shapes=[pltpu.VMEM((HEAD_DIM, HEAD_DIM), F32)] + [tile_bf] * 6 + [square(F32), square(BF16)]
        + [pltpu.VMEM((cpb, CHUNK, CHUNK), BF16)] * 2 + [rows8, rows8] + [tile_f32] * 14,
        semantics=("parallel", "arbitrary"),
        operands=(d_og, o, proj, proj, proj, proj, states, logits, norm_g, _block_tri(_tri_rows(tm)), _block_tri(_tri_rows(tm), lower=False)),
        hosted=hosted)


def _mix_ln1(proj, y_conv, y_hgrn, w_out, x, g, b):
    t = x.shape[0]
    tm = _pick(t, 512)
    d = D_MODEL

    def lhs(ins, outs):
        for rows in _row_blocks(tm):
            outs[0][rows, :] = (_sigmoid(ins[0][rows, :]) * ins[3][rows, :]
                                + _sigmoid(ins[2][rows, :]) * ins[4][rows, :]).astype(BF16)
        return outs[0][...]

    def epilogue(acc, ins, outs, scr):
        for rows in _row_blocks(tm):
            r = ALPHA * ins[5][rows, :] + acc[rows, :]
            outs[1][rows, :] = r
            xhat, _ = _ln(r)
            outs[2][rows, :] = (xhat * ins[6][...] + ins[7][...]).astype(BF16)

    tile = pl.BlockSpec((tm, d), lambda i, j, k: (i, 0))
    vec = pl.BlockSpec((1, d), lambda i, j, k: (0, 0))
    return _mm_fused(
        "mix_ln1", (t // tm, 1, 1), (proj, w_out, proj, y_conv, y_hgrn, x, g, b),
        [pl.BlockSpec((tm, d), lambda i, j, k: (i, 5)), pl.BlockSpec((d, d), lambda i, j, k: (0, 0)),
         pl.BlockSpec((tm, d), lambda i, j, k: (i, 6)), tile, tile, tile, vec, vec],
        [jax.ShapeDtypeStruct((t, d), BF16), jax.ShapeDtypeStruct((t, d), F32), jax.ShapeDtypeStruct((t, d), BF16)],
        [tile, tile, tile], NN, (tm, d), epilogue, lhs=lhs)


def _d_mixed_merge_bwd(d_r1b, w_out, proj, y_conv, y_hgrn):
    t = proj.shape[0]
    tm = _pick(t, 512)
    d = D_MODEL

    def epilogue(d_mixed, ins, outs, scr):
        dy_ref, dmz_ref = outs
        for rows in _row_blocks(tm):
            dm = d_mixed[rows, :]
            for br in range(2):
                sg = _sigmoid(ins[2 + br][rows, :])
                dy_ref[br, rows, :] = (sg * dm).astype(BF16)
                dmz_ref[rows, br * d:(br + 1) * d] = (dm * ins[4 + br][rows, :] * sg * (1.0 - sg)).astype(BF16)

    tile = pl.BlockSpec((tm, d), lambda i, j, k: (i, 0))
    return _mm_fused(
        "d_mixed_merge_bwd", (t // tm, 1, 1), (d_r1b, w_out, proj, proj, y_conv, y_hgrn),
        [tile, pl.BlockSpec((d, d), lambda i, j, k: (0, 0)), pl.BlockSpec((tm, d), lambda i, j, k: (i, 5)),
         pl.BlockSpec((tm, d), lambda i, j, k: (i, 6)), tile, tile],
        [jax.ShapeDtypeStruct((2, t, d), BF16), jax.ShapeDtypeStruct((t, 2 * d), BF16)],
        [pl.BlockSpec((2, tm, d), lambda i, j, k: (0, i, 0)), pl.BlockSpec((tm, 2 * d), lambda i, j, k: (i, 0))],
        NT, (tm, d), epilogue)


def _ffn_out_ln2(act, w_ffn_out, r1, target, g1, b1, g2, b2):
    t = r1.shape[0]
    tm = _pick(t, 1024)
    nt = t // tm
    d = D_MODEL

    def epilogue(y_ffn, ins, outs, scr):
        r1_ref, tg_ref, g1_ref, b1_ref, g2_ref, b2_ref = ins[2:]
        dr_ref, drb_ref, sums_ref = outs
        (sq_ref,) = scr
        i = pl.program_id(0)

        @pl.when(i == 0)
        def _():
            sums_ref[...] = jnp.zeros_like(sums_ref)
            sq_ref[...] = jnp.zeros_like(sq_ref)

        for rows in _row_blocks(tm):
            xh1, _ = _ln(r1_ref[rows, :])
            x1 = xh1 * g1_ref[...] + b1_ref[...]
            xh2, rstd2 = _ln(ALPHA * x1 + y_ffn[rows, :])
            diff = xh2 * g2_ref[...] + b2_ref[...] - tg_ref[rows, :]
            dy = diff * (1.0 / D_MODEL)
            dr = _ln_bwd(dy, xh2, rstd2, g2_ref[...])
            dr_ref[rows, :] = dr
            drb_ref[rows, :] = dr.astype(BF16)
            sums_ref[0:1, :] += _colsum(dy * xh2)
            sums_ref[1:2, :] += _colsum(dy)
            sq_ref[...] += _colsum(diff * diff)

        @pl.when(i == nt - 1)
        def _():
            total = jnp.sum(sq_ref[...], axis=-1, keepdims=True) * (0.5 / D_MODEL)
            sums_ref[2:3, :] = jnp.broadcast_to(total, (1, D_MODEL))

    tile = pl.BlockSpec((tm, d), lambda i, j, k: (i, 0))
    vec = pl.BlockSpec((1, d), lambda i, j, k: (0, 0))
    return _mm_fused(
        "ffn_out_ln2", (nt, 1, 4), (act, w_ffn_out, r1, target, g1, b1, g2, b2),
        [pl.BlockSpec((None, tm, FF_SHARD), lambda i, j, k: (k, i, 0)),
         pl.BlockSpec((None, FF_SHARD, d), lambda i, j, k: (k, 0, 0)), tile, tile, vec, vec, vec, vec],
        [jax.ShapeDtypeStruct((t, d), F32), jax.ShapeDtypeStruct((t, d), BF16), jax.ShapeDtypeStruct((8, d), F32)],
        [tile, tile, pl.BlockSpec((8, d), lambda i, j, k: (0, 0))], NN, (tm, d), epilogue,
        scratch=[pltpu.VMEM((1, d), F32)])


def _d_x1_ln1_bwd(d_z, w_ffn_in, d_r2, r1, g1):
    t = r1.shape[0]
    tm = _pick(t, 1024)
    d = D_MODEL

    def epilogue(dx_ffn, ins, outs, scr):
        dr2_ref, r1_ref, g_ref = ins[2:]
        dr1_ref, dr1b_ref, sums_ref = outs
        i = pl.program_id(0)

        @pl.when(i == 0)
        def _():
            sums_ref[...] = jnp.zeros_like(sums_ref)

        for rows in _row_blocks(tm):
            xhat, rstd = _ln(r1_ref[rows, :])
            dx1 = ALPHA * dr2_ref[rows, :] + dx_ffn[rows, :]
            dr1 = _ln_bwd(dx1, xhat, rstd, g_ref[...])
            dr1_ref[rows, :] = dr1
            dr1b_ref[rows, :] = dr1.astype(BF16)
            sums_ref[0:1, :] += _colsum(dx1 * xhat)
            sums_ref[1:2, :] += _colsum(dx1)

    tile = pl.BlockSpec((tm, d), lambda i, j, k: (i, 0))
    return _mm_fused(
        "d_x1_ln1_bwd", (t // tm, 1, N_DEV), (d_z, w_ffn_in, d_r2, r1, g1),
        [pl.BlockSpec((None, tm, FF_SHARD), lambda i, j, k: (k, i, 0)),
         pl.BlockSpec((None, d, FF_SHARD), lambda i, j, k: (k, 0, 0)), tile, tile,
         pl.BlockSpec((1, d), lambda i, j, k: (0, 0))],
        [jax.ShapeDtypeStruct((t, d), F32), jax.ShapeDtypeStruct((t, d), BF16), jax.ShapeDtypeStruct((8, d), F32)],
        [tile, tile, pl.BlockSpec((8, d), lambda i, j, k: (0, 0))], NT, (tm, d), epilogue)


def _cast_bf16(x):
    t = x.shape[0]
    tm = _pick(t, 512)

    def body(x_ref, o_ref):
        o_ref[...] = x_ref[...].astype(BF16)

    tile = pl.BlockSpec((tm, D_MODEL), lambda i: (i, 0))
    return pl.pallas_call(
        body, name="cast_x", grid=(t // tm,), in_specs=[tile], out_specs=tile,
        out_shape=jax.ShapeDtypeStruct((t, D_MODEL), BF16), compiler_params=_params(("parallel",)),
    )(x)


def _relayout(name, a, in_block, in_map, out_block, out_map, out_shape):
    def body(a_ref, o_ref):
        o_ref[...] = a_ref[...].astype(o_ref.dtype)

    return pl.pallas_call(
        body, name=name, grid=(N_DEV,), in_specs=[pl.BlockSpec(in_block, in_map)],
        out_specs=pl.BlockSpec(out_block, out_map), out_shape=out_shape, compiler_params=_params(("parallel",)),
    )(a)


_GELU_C = math.sqrt(2.0 / math.pi)


_GELU_CUBIC = 0.044715


def _gelu_parts(u):
    u2 = u * u
    th = jnp.tanh(u * (_GELU_C + (_GELU_C * _GELU_CUBIC) * u2))
    hu = 0.5 * u
    return th, hu + hu * th, u2, hu


BF16_ROWS = 16


def _ffn_act_fwd(z, w_dw, b_dw):
    t = z.shape[2]
    tm = _pick(t, 512)
    nh = tm // FFN_HALO

    def body(z_ref, zh_ref, w_ref, b_ref, act_ref, gd_ref, us_ref):
        i = pl.program_id(1)
        us_ref[0:FFN_HALO, :] = jnp.where(i == 0, 0.0, zh_ref[...])
        us_ref[FFN_HALO:FFN_HALO + tm, :] = z_ref[0]
        for r in range(tm // ROW_BLOCK):
            base = r * ROW_BLOCK
            rows = slice(base, base + ROW_BLOCK)
            for lanes in _lane_blocks(FF_SHARD):
                uc = b_ref[:, lanes]
                for k in range(FFN_K):
                    off = base + FFN_HALO - (FFN_K - 1) + k
                    uc = uc + w_ref[k:k + 1, lanes] * us_ref[off:off + ROW_BLOCK, lanes]
                th, gelu, u2, hu = _gelu_parts(uc)
                dgelu = (0.5 + 0.5 * th) + (hu - hu * th * th) * (_GELU_C + (3.0 * _GELU_C * _GELU_CUBIC) * u2)
                act_ref[rows, lanes] = (gelu * z_ref[1, rows, lanes]).astype(BF16)
                gd_ref[0, rows, lanes] = gelu.astype(BF16)
                gd_ref[1, rows, lanes] = dgelu.astype(BF16)

    return pl.pallas_call(
        body, name="ffn_act_fwd", grid=(4, t // tm),
        in_specs=[pl.BlockSpec((2, None, tm, FF_SHARD), lambda j, i: (0, j, i, 0)),
                  pl.BlockSpec((None, None, FFN_HALO, FF_SHARD), lambda j, i: (0, j, jnp.maximum(i * nh - 1, 0), 0)),
                  pl.BlockSpec((None, FFN_K, FF_SHARD), lambda j, i: (j, 0, 0)),
                  pl.BlockSpec((None, 1, FF_SHARD), lambda j, i: (j, 0, 0))],
        out_specs=[pl.BlockSpec((None, tm, FF_SHARD), lambda j, i: (j, i, 0)),
                   pl.BlockSpec((2, None, tm, FF_SHARD), lambda j, i: (0, j, i, 0))],
        out_shape=[jax.ShapeDtypeStruct((4, t, FF_SHARD), BF16), jax.ShapeDtypeStruct((2, 4, t, FF_SHARD), BF16)],
        scratch_shapes=[pltpu.VMEM((FFN_HALO + tm, FF_SHARD), F32)],
        compiler_params=_params(("parallel", "arbitrary")),
    )(z, z, w_dw, b_dw)


def _ffn_act_bwd(d_act, z, gd, w_dw):
    t = z.shape[2]
    tm = _pick(t, 512)
    nt = t // tm
    nh = tm // FFN_HALO
    last_h = t // FFN_HALO - 1
    pad = FFN_HALO - (FFN_K - 1)

    def fold(x):
        return functools.reduce(jnp.add, [x[r:r + SUBLANES, :] for r in range(0, x.shape[0], SUBLANES)])

    def body(da_ref, dah_ref, z_ref, zp_ref, gn_ref, gd_ref, gdn_ref, w_ref, dz_ref, sums_ref, us_ref, ds_ref,
             part_ref):
        i = pl.program_id(1)

        @pl.when(i == 0)
        def _():
            part_ref[...] = jnp.zeros_like(part_ref)

        us_ref[0:FFN_HALO, :] = jnp.where(i == 0, 0.0, zp_ref[...])
        us_ref[FFN_HALO:FFN_HALO + tm, :] = z_ref[0]
        for r in range(tm // ROW_BLOCK):
            base = r * ROW_BLOCK
            rows = slice(base, base + ROW_BLOCK)
            for lanes in _lane_blocks(FF_SHARD):
                da = da_ref[rows, lanes]
                dz_ref[1, rows, lanes] = (da * gd_ref[0, rows, lanes].astype(F32)).astype(BF16)
                duc = da * z_ref[1, rows, lanes] * gd_ref[1, rows, lanes].astype(F32)
                ds_ref[rows, lanes] = duc
                for k in range(FFN_K):
                    part_ref[k, :, lanes] += fold(duc * us_ref[base + pad + k:base + pad + k + ROW_BLOCK, lanes])
                part_ref[FFN_K, :, lanes] += fold(duc)
        duc_next = dah_ref[...] * gn_ref[...] * gdn_ref[0:FFN_HALO, :].astype(F32)
        ds_ref[tm:tm + FFN_HALO, :] = jnp.where(i == nt - 1, 0.0, duc_next)
        for r in range(tm // ROW_BLOCK):
            base = r * ROW_BLOCK
            for lanes in _lane_blocks(FF_SHARD):
                du = None
                for k in range(FFN_K):
                    off = base + FFN_K - 1 - k
                    term = w_ref[k:k + 1, lanes] * ds_ref[off:off + ROW_BLOCK, lanes]
                    du = term if du is None else du + term
                dz_ref[0, base:base + ROW_BLOCK, lanes] = du.astype(BF16)

        @pl.when(i == nt - 1)
        def _():
            sums_ref[...] = jnp.zeros_like(sums_ref)
            for k in range(FFN_K + 1):
                sums_ref[k:k + 1, :] = _colsum(part_ref[k])

    nxt = lambda i: jnp.minimum((i + 1) * nh, last_h)
    nxt_bf = lambda i: jnp.minimum((i + 1) * (tm // BF16_ROWS), t // BF16_ROWS - 1)
    return pl.pallas_call(
        body, name="ffn_act_bwd", grid=(4, nt),
        in_specs=[pl.BlockSpec((None, tm, FF_SHARD), lambda j, i: (j, i, 0)),
                  pl.BlockSpec((None, FFN_HALO, FF_SHARD), lambda j, i: (j, nxt(i), 0)),
                  pl.BlockSpec((2, None, tm, FF_SHARD), lambda j, i: (0, j, i, 0)),
                  pl.BlockSpec((None, None, FFN_HALO, FF_SHARD), lambda j, i: (0, j, jnp.maximum(i * nh - 1, 0), 0)),
                  pl.BlockSpec((None, None, FFN_HALO, FF_SHARD), lambda j, i: (1, j, nxt(i), 0)),
                  pl.BlockSpec((2, None, tm, FF_SHARD), lambda j, i: (0, j, i, 0)),
                  pl.BlockSpec((None, None, BF16_ROWS, FF_SHARD), lambda j, i: (1, j, nxt_bf(i), 0)),
                  pl.BlockSpec((None, FFN_K, FF_SHARD), lambda j, i: (j, 0, 0))],
        out_specs=[pl.BlockSpec((2, None, tm, FF_SHARD), lambda j, i: (0, j, i, 0)),
                   pl.BlockSpec((None, 8, FF_SHARD), lambda j, i: (j, 0, 0))],
        out_shape=[jax.ShapeDtypeStruct((2, 4, t, FF_SHARD), BF16), jax.ShapeDtypeStruct((4, 8, FF_SHARD), F32)],
        scratch_shapes=[pltpu.VMEM((FFN_HALO + tm, FF_SHARD), F32), pltpu.VMEM((tm + FFN_HALO, FF_SHARD), F32),
                        pltpu.VMEM((FFN_K + 1, SUBLANES, FF_SHARD), F32)],
        compiler_params=_params(("parallel", "arbitrary")),
    )(d_act, d_act, z, z, z, gd, gd, w_dw)


def _local_step(x, target, weights, small, scatter=None, order=None):
    t = x.shape[0]
    tm = _pick(t, 2048)
    tk = _pick(t, 2048)
    nm = t // tm
    nk = t // tk
    d = D_MODEL

    xb = _cast_bf16(x)
    if isinstance(weights, _Hosted):
        proj, weights = _proj_gather(xb, weights, order)
    else:
        proj = _mm("proj", xb, weights[0], (t, IN_COLS), F32, (nm, N_DEV, 1),
                   pl.BlockSpec((tm, d), lambda i, j, k: (i, 0)),
                   pl.BlockSpec((None, d, IN_SHARD), lambda i, j, k: (j, 0, 0)),
                   pl.BlockSpec((tm, IN_SHARD), lambda i, j, k: (i, j)), NN, (tm, IN_SHARD))
    w_in, w_conv_out8, w_hgrn_out8, w_out8, w_ffn_in, w_ffn_out8, conv_dw8, ffn_dw8 = weights
    w_conv_out = _relayout("w_conv_out_natural", w_conv_out8, (None, CONV_DIM, 128), lambda j: (j, 0, 0),
                           (CONV_DIM, 128), lambda j: (0, j), jax.ShapeDtypeStruct((CONV_DIM, d), BF16))
    w_hgrn_out = w_hgrn_out8.reshape(d, d)
    w_out = w_out8.reshape(d, d)
    w_ffn_out = w_ffn_out8.reshape(4, FF_SHARD, d)
    conv_dw = jnp.transpose(conv_dw8[:, :CONV_K, :CONV_DIM // N_DEV], (1, 0, 2)).reshape(CONV_K, CONV_DIM)
    ffn_dw = jnp.transpose(ffn_dw8[:, :FFN_K, :D_FF // N_DEV], (1, 0, 2)).reshape(FFN_K, 4, FF_SHARD)
    small = dict(small, w_conv_dw=conv_dw, w_ffn_dw=jnp.transpose(ffn_dw, (1, 0, 2)),
                 b_ffn_dw=small["b_ffn_dw"].reshape(4, 1, FF_SHARD))

    c_act, conv_pre = _conv_fwd(proj, small["w_conv_dw"], small["b_conv_dw"], small["conv_ln_g"], small["conv_ln_b"])
    y_conv = _mm("y_conv", c_act, w_conv_out, (t, d), F32, (nm, 1, 1),
                 pl.BlockSpec((tm, CONV_DIM), lambda i, j, k: (i, 0)),
                 pl.BlockSpec((CONV_DIM, d), lambda i, j, k: (0, 0)),
                 pl.BlockSpec((tm, d), lambda i, j, k: (i, 0)), NN, (tm, d))
    o, og, states = _hgrn_fwd(proj, small["hgrn_lb_logits"], small["hgrn_norm_g"])
    sq_w = pl.BlockSpec((d, d), lambda i, j, k: (0, 0))
    row_tile = pl.BlockSpec((tm, d), lambda i, j, k: (i, 0))
    y_hgrn = _mm("y_hgrn", og, w_hgrn_out, (t, d), F32, (nm, 1, 1), row_tile, sq_w, row_tile, NN, (tm, d))
    mixed, r1, x1b = _mix_ln1(proj, y_conv, y_hgrn, w_out, x, small["ln1_g"], small["ln1_b"])
    z = _mm("ffn_in", x1b, w_ffn_in, (N_DEV, t, FF_SHARD), F32, (nm, N_DEV, 1), row_tile,
            pl.BlockSpec((None, d, FF_SHARD), lambda i, j, k: (j, 0, 0)),
            pl.BlockSpec((None, tm, FF_SHARD), lambda i, j, k: (j, i, 0)), NN, (tm, FF_SHARD))
    z = z.reshape(2, 4, t, FF_SHARD)
    act, gelu_and_slope = _ffn_act_fwd(z, small["w_ffn_dw"], small["b_ffn_dw"])

    d_r2, d_r2b, sums_ln2 = _ffn_out_ln2(act, w_ffn_out, r1, target, small["ln1_g"], small["ln1_b"],
                                         small["ln2_g"], small["ln2_b"])
    d_act = _mm("d_act", d_r2b, w_ffn_out, (4, t, FF_SHARD), F32, (nm, 4, 1), row_tile,
                pl.BlockSpec((None, FF_SHARD, d), lambda i, j, k: (j, 0, 0)),
                pl.BlockSpec((None, tm, FF_SHARD), lambda i, j, k: (j, i, 0)), NT, (tm, FF_SHARD))
    g_w_ffn_out = _mm("g_w_ffn_out", act, d_r2b, (4, FF_SHARD, d), BF16, (4, 1, nk),
                      pl.BlockSpec((None, tk, FF_SHARD), lambda i, j, k: (i, k, 0)),
                      pl.BlockSpec((tk, d), lambda i, j, k: (k, 0)),
                      pl.BlockSpec((None, FF_SHARD, d), lambda i, j, k: (i, 0, 0)), TN, (FF_SHARD, d))
    d_z, sums_ffn = _ffn_act_bwd(d_act, z, gelu_and_slope, small["w_ffn_dw"])
    d_z8 = d_z.reshape(N_DEV, t, FF_SHARD)
    d_r1, d_r1b, sums_ln1 = _d_x1_ln1_bwd(d_z8, w_ffn_in, d_r2, r1, small["ln1_g"])
    g_w_ffn_in = _mm("g_w_ffn_in", x1b, d_z8, (N_DEV, d, FF_SHARD), BF16, (N_DEV, 1, nk),
                     pl.BlockSpec((tk, d), lambda i, j, k: (k, 0)),
                     pl.BlockSpec((None, tk, FF_SHARD), lambda i, j, k: (i, k, 0)),
                     pl.BlockSpec((None, d, FF_SHARD), lambda i, j, k: (i, 0, 0)), TN, (d, FF_SHARD))
    k_tile = pl.BlockSpec((tk, d), lambda i, j, k: (k, 0))
    g_w_out = _mm("g_w_out", mixed, d_r1b, (d, d), BF16, (1, 1, nk), k_tile, k_tile, sq_w, TN, (d, d))
    d_y, d_mz = _d_mixed_merge_bwd(d_r1b, w_out, proj, y_conv, y_hgrn)
    d_pre, sums_conv = _d_c_norm_bwd(d_y, w_conv_out, conv_pre, small["conv_ln_g"], small["conv_ln_b"])
    g_w_conv_out = _mm("g_w_conv_out", c_act, d_y, (CONV_DIM, d), BF16, (1, 1, nk),
                       pl.BlockSpec((tk, CONV_DIM), lambda i, j, k: (k, 0)),
                       pl.BlockSpec((None, tk, d), lambda i, j, k: (0, k, 0)),
                       pl.BlockSpec((CONV_DIM, d), lambda i, j, k: (0, 0)), TN, (CONV_DIM, d))
    g_w_conv_out = _relayout("g_w_conv_out_shards", g_w_conv_out, (CONV_DIM, 128), lambda j: (0, j),
                             (None, CONV_DIM, 128), lambda j: (j, 0, 0),
                             jax.ShapeDtypeStruct((N_DEV, CONV_DIM, 128), BF16))
    d_og = _mm("d_og", d_y, w_hgrn_out, (t, d), F32, (nm, 1, 1),
               pl.BlockSpec((None, tm, d), lambda i, j, k: (1, i, 0)), sq_w, row_tile, NT, (tm, d))
    g_w_hgrn_out = _mm("g_w_hgrn_out", og, d_y, (d, d), BF16, (1, 1, nk), k_tile,
                       pl.BlockSpec((None, tk, d), lambda i, j, k: (1, k, 0)), sq_w, TN, (d, d))
    d_cproj, g_w_conv_dw = _conv_bwd_dw(d_pre, proj, small["w_conv_dw"])

    early = [g_w_conv_out, g_w_hgrn_out.reshape(N_DEV, d // N_DEV, d), g_w_out.reshape(N_DEV, d // N_DEV, d),
             g_w_ffn_in, g_w_ffn_out.reshape(N_DEV, D_FF // N_DEV, d)]
    hgrn_out = _hgrn_bwd(d_og, o, proj, states, small["hgrn_lb_logits"], small["hgrn_norm_g"],
                         hosted=None if scatter is None else scatter(early))
    d_qz, d_fz, d_iv, d_gz, sums_hgrn = hgrn_out[:5]
    early_recv = list(hgrn_out[5:])
    d_proj = jnp.concatenate([d_cproj, d_qz, d_fz, d_iv, d_gz, d_mz], axis=1)
    g_w_in = _mm("g_w_in", xb, d_proj, (N_DEV, d, IN_SHARD), BF16, (N_DEV, 1, nk), k_tile,
                 pl.BlockSpec((tk, IN_SHARD), lambda i, j, k: (k, i)),
                 pl.BlockSpec((None, d, IN_SHARD), lambda i, j, k: (i, 0, 0)), TN, (d, IN_SHARD))
    def add_residual(acc, ins, outs, scr):
        for rows in _row_blocks(ta):
            outs[0][rows, :] = ALPHA * ins[2][rows, :] + acc[rows, :]

    ta = _pick(t, 1024)
    acc_tile = pl.BlockSpec((ta, d), lambda i, j, k: (i, 0))
    grad_x, *late_recv = _mm_fused(
        "grad_x", (t // ta, 1, N_DEV), (d_proj, w_in, d_r1),
        [pl.BlockSpec((ta, IN_SHARD), lambda i, j, k: (i, k)),
         pl.BlockSpec((None, d, IN_SHARD), lambda i, j, k: (k, 0, 0)), acc_tile],
        [jax.ShapeDtypeStruct((t, d), F32)], [acc_tile], NT, (ta, d), add_residual,
        hosted=None if scatter is None else scatter([g_w_in]))

    large_grads = [g_w_in] + early
    if scatter is not None:
        large_grads = list(zip(large_grads, late_recv + early_recv))
    return grad_x, large_grads, (sums_ln2, sums_conv, sums_hgrn, sums_ln1, sums_ffn, g_w_conv_dw)


def _small_views(sums):
    sums_ln2, sums_conv, sums_hgrn, sums_ln1, sums_ffn, g_w_conv_dw = sums
    d_l0 = sums_hgrn[1:2]
    return {
        "loss": sums_ln2[2:3, 0:128],
        "b_conv_dw": sums_conv[2:3], "conv_ln_g": sums_conv[0:1], "conv_ln_b": sums_conv[1:2],
        "hgrn_lb_logits": jnp.concatenate([d_l0, -d_l0], axis=1),
        "hgrn_norm_g": sums_hgrn[0:1],
        "ln1_g": sums_ln1[0:1], "ln1_b": sums_ln1[1:2],
        "b_ffn_dw": sums_ffn[:, FFN_K, :].reshape(1, D_FF),
        "ln2_g": sums_ln2[0:1], "ln2_b": sums_ln2[1:2],
        "w_conv_dw": g_w_conv_dw[0:CONV_K].reshape(1, CONV_K * CONV_DIM),
        "w_ffn_dw": jnp.transpose(sums_ffn[:, 0:FFN_K, :], (1, 0, 2)).reshape(1, FFN_K * D_FF),
    }


def _coords():
    return lax.axis_index("x"), lax.axis_index("y"), lax.axis_index("c")


def _gather(shards, staged=False):
    n = len(shards)
    later = range(1 if staged else 0, n)

    def parts(ins, outs, sems):
        send_sems, recv_sems, local_sems = sems
        x, y, c = _coords()
        me = 4 * x + 2 * y + c
        sibling = (x, y, 1 - c)
        chips = [(1 - x, y), (x, 1 - y), (1 - x, 1 - y)]

        def copy(a, k, block, to, src=None):
            return pltpu.make_async_remote_copy(
                src_ref=outs[a].at[block] if src is None else src, dst_ref=outs[a].at[block],
                send_sem=send_sems.at[a, k], recv_sem=recv_sems.at[a, k], device_id=to, device_id_type=MESH)

        local = [pltpu.make_async_copy(ins[a], outs[a].at[me], local_sems.at[a]) for a in range(n)]
        first = []
        for a in range(n):
            first.append(copy(a, 0, me, sibling, src=ins[a]))
            for j, chip in enumerate(chips):
                first.append(copy(a, 1 + j, me, (*chip, c), src=ins[a]))
        return x, y, c, sibling, chips, copy, local, first

    def start(ins, outs, sems):
        *_, local, first = parts(ins, outs, sems)
        for cp in local + first:
            cp.start()

    def arrive(ins, outs, sems, s):
        x, y, c, sibling, chips, copy, _, _ = parts(ins, outs, sems)
        if s == 1:
            block = 4 * x + 2 * y + 1 - c
            copy(0, 0, block, sibling).wait_recv()
        elif s <= 4:
            px, py = chips[s - 2]
            block = 4 * px + 2 * py + c
            copy(0, s - 1, block, sibling).wait_recv()
            copy(0, s + 2, block, sibling).start()
        else:
            px, py = chips[s - 5]
            block = 4 * px + 2 * py + 1 - c
            copy(0, s - 1, block, sibling).wait_recv()
        return block

    def finish(ins, outs, sems):
        x, y, c, sibling, chips, copy, local, first = parts(ins, outs, sems)
        passed = [copy(0, 4 + j, 4 * px + 2 * py + c, sibling) for j, (px, py) in enumerate(chips)] if staged else []
        for j, (px, py) in enumerate(chips):
            for a in later:
                copy(a, 1 + j, 4 * px + 2 * py + c, sibling).wait_recv()
                cp = copy(a, 4 + j, 4 * px + 2 * py + c, sibling)
                cp.start()
                passed.append(cp)
        for a in later:
            copy(a, 0, 4 * x + 2 * y + 1 - c, sibling).wait_recv()
            for j, (px, py) in enumerate(chips):
                copy(a, 4 + j, 4 * px + 2 * py + 1 - c, sibling).wait_recv()
        for cp in first + passed:
            cp.wait_send()
        for cp in local:
            cp.wait()

    hosted = _Hosted(shards, [jax.ShapeDtypeStruct((N_DEV,) + s.shape, s.dtype) for s in shards],
                     [pltpu.SemaphoreType.DMA((n, 7)), pltpu.SemaphoreType.DMA((n, 7)), pltpu.SemaphoreType.DMA((n,))],
                     start, finish)
    hosted.arrive = arrive
    return hosted


def _proj_gather(xb, gather, order):
    t, d = xb.shape
    tm = _pick(t, 2048)
    nm = t // tm
    n_in, n_out = len(gather.inputs), len(gather.out_shapes)

    def body(order_ref, x_ref, *refs):
        ins, refs = refs[:n_in], refs[n_in:]
        o_ref, outs, refs = refs[0], refs[1:1 + n_out], refs[1 + n_out:]
        w_buf, w_sem, sems = refs[0], refs[1], refs[2:]
        s, i = pl.program_id(0), pl.program_id(1)

        @pl.when((s == 0) & (i == 0))
        def _():
            gather.start(ins, outs, sems)

        def staging(step, src):
            return pltpu.make_async_copy(src, w_buf.at[step % 2], w_sem.at[step % 2])

        @pl.when((s == 0) & (i == 0))
        def _():
            staging(0, ins[0]).start()

        for step in range(1, N_DEV):
            @pl.when((s == step - 1) & (i == nm - 1))
            def _(step=step):
                staging(step, outs[0].at[gather.arrive(ins, outs, sems, step)]).start()

        for step in range(N_DEV):
            @pl.when((s == step) & (i == 0))
            def _(step=step):
                staging(step, ins[0]).wait()

        o_ref[...] = _dot(x_ref[...], w_buf[s % 2], NN)

        @pl.when((s == N_DEV - 1) & (i == nm - 1))
        def _():
            gather.finish(ins, outs, sems)

    outs = pl.pallas_call(
        body, name="proj_gather",
        grid_spec=pltpu.PrefetchScalarGridSpec(
            num_scalar_prefetch=1, grid=(N_DEV, nm),
            in_specs=[pl.BlockSpec((tm, d), lambda s, i, order_ref: (i, 0))] + [ANY] * n_in,
            out_specs=[pl.BlockSpec((tm, IN_SHARD), lambda s, i, order_ref: (i, order_ref[s]))] + [ANY] * n_out,
            scratch_shapes=[pltpu.VMEM((2, d, IN_SHARD), BF16), pltpu.SemaphoreType.DMA((2,))] + gather.sem_shapes),
        out_shape=[jax.ShapeDtypeStruct((t, IN_COLS), F32)] + gather.out_shapes,
        compiler_params=pltpu.CompilerParams(dimension_semantics=("arbitrary", "arbitrary"),
                                             vmem_limit_bytes=VMEM_LIMIT, has_side_effects=True),
    )(order, xb, *gather.inputs)
    return outs[0], list(outs[1:])


def _scatter(grads):
    n = len(grads)

    def copies(ins, outs, sems):
        send_sems, recv_sems = sems
        x, y, c = _coords()
        out = []
        for a in range(n):
            for k in range(1, N_DEV):
                px, py, pc = x ^ (k >> 2), y ^ ((k >> 1) & 1), c ^ (k & 1)
                out.append(pltpu.make_async_remote_copy(
                    src_ref=ins[a].at[4 * px + 2 * py + pc], dst_ref=outs[a].at[k - 1],
                    send_sem=send_sems.at[a, k - 1], recv_sem=recv_sems.at[a, k - 1],
                    device_id=(px, py, pc), device_id_type=MESH))
        return out

    def start(ins, outs, sems):
        for cp in copies(ins, outs, sems):
            cp.start()

    def finish(ins, outs, sems):
        for cp in copies(ins, outs, sems):
            cp.wait()

    return _Hosted(grads, [jax.ShapeDtypeStruct((N_DEV - 1,) + g.shape[1:], g.dtype) for g in grads],
                   [pltpu.SemaphoreType.DMA((n, N_DEV - 1)), pltpu.SemaphoreType.DMA((n, N_DEV - 1))], start, finish)


def _row_tile(rows):
    return 256 if rows % 256 == 0 else rows


def _adam_math(w, g, m, v):
    m_new = ADAM_B1 * m + (1.0 - ADAM_B1) * g
    v_new = ADAM_B2 * v + (1.0 - ADAM_B2) * (g * g)
    m_hat = m_new / (1.0 - ADAM_B1 ** ADAM_STEP)
    v_hat = v_new / (1.0 - ADAM_B2 ** ADAM_STEP)
    delta = -ADAM_LR * (m_hat / (jnp.sqrt(v_hat) + ADAM_EPS) + ADAM_WD * w)
    return delta, m_new, v_new


def _adam_large(name, own, recv, me, w, m, v):
    rows, cols = w.shape
    tr = _row_tile(rows)

    def body(me_ref, p_ref, r_ref, w_ref, m_ref, v_ref, g_out, d_out, m_out, v_out):
        g = p_ref[...].astype(F32)
        for k in range(N_DEV - 1):
            g = g + r_ref[k].astype(F32)
        delta, m_new, v_new = _adam_math(w_ref[...], g, m_ref[...], v_ref[...])
        g_out[...] = g
        d_out[...] = delta
        m_out[...] = m_new
        v_out[...] = v_new

    tile = pl.BlockSpec((tr, cols), lambda r, me_ref: (r, 0))
    sds = jax.ShapeDtypeStruct((rows, cols), F32)
    return pl.pallas_call(
        body, name=name,
        grid_spec=pltpu.PrefetchScalarGridSpec(
            num_scalar_prefetch=1, grid=(rows // tr,),
            in_specs=[pl.BlockSpec((None, tr, cols), lambda r, me_ref: (me_ref[0], r, 0)),
                      pl.BlockSpec((N_DEV - 1, tr, cols), lambda r, me_ref: (0, r, 0)), tile, tile, tile],
            out_specs=[tile, tile, tile, tile]),
        out_shape=[sds, sds, sds, sds],
        compiler_params=_params(("parallel",)),
    )(me, own, recv, w, m, v)


def _small_allreduce(arrays):
    n = len(arrays)

    def body(*refs):
        ins, outs, gats = refs[:n], refs[n:2 * n], refs[2 * n:3 * n]
        send_sems, recv_sems = refs[3 * n:]
        x, y, c = _coords()
        me = 4 * x + 2 * y + c
        peers = [(x ^ (k >> 2), y ^ ((k >> 1) & 1), c ^ (k & 1)) for k in range(1, N_DEV)]

        def copy(a, k, slot):
            return pltpu.make_async_remote_copy(
                src_ref=ins[a], dst_ref=gats[a].at[slot], send_sem=send_sems.at[a, k], recv_sem=recv_sems.at[a, k],
                device_id=peers[k], device_id_type=MESH)

        sends = [copy(a, k, me) for a in range(n) for k in range(N_DEV - 1)]
        for a in range(n):
            gats[a][me] = ins[a][...]
        for cp in sends:
            cp.start()
        for a in range(n):
            for k, (px, py, pc) in enumerate(peers):
                copy(a, k, 4 * px + 2 * py + pc).wait_recv()
        for cp in sends:
            cp.wait_send()
        for a in range(n):
            acc = gats[a][0]
            for dev in range(1, N_DEV):
                acc = acc + gats[a][dev]
            outs[a][...] = acc

    whole = pl.BlockSpec(memory_space=pltpu.VMEM)
    return pl.pallas_call(
        body, name="small_allreduce", in_specs=[whole] * n, out_specs=[whole] * n,
        out_shape=[jax.ShapeDtypeStruct(a.shape, F32) for a in arrays],
        scratch_shapes=[pltpu.VMEM((N_DEV,) + a.shape, F32) for a in arrays]
        + [pltpu.SemaphoreType.DMA((n, N_DEV - 1)), pltpu.SemaphoreType.DMA((n, N_DEV - 1))],
        compiler_params=pltpu.CompilerParams(has_side_effects=True, vmem_limit_bytes=VMEM_LIMIT),
    )(*arrays)


def _adam_replicated(sums, w, m, v):
    rows_of = {"conv_ln_g": (1, 0), "conv_ln_b": (1, 1), "b_conv_dw": (1, 2), "hgrn_norm_g": (2, 0),
               "ln1_g": (3, 0), "ln1_b": (3, 1), "ln2_g": (0, 0), "ln2_b": (0, 1)}
    names = list(rows_of) + ["hgrn_lb_logits"]
    n = len(names)

    def body(*refs):
        sum_refs, refs = refs[:4], refs[4:]
        w_refs, m_refs, v_refs, outs = refs[:n], refs[n:2 * n], refs[2 * n:3 * n], refs[3 * n:]
        for j, name in enumerate(names):
            if name == "hgrn_lb_logits":
                d_l0 = sum_refs[2][1:2, :]
                grads = [d_l0, -d_l0]
            else:
                a, row = rows_of[name]
                grads = [sum_refs[a][row:row + 1, :]]
            g_out, d_out, m_out, v_out = outs[4 * j:4 * j + 4]
            for r, g in enumerate(grads):
                rows = slice(r, r + 1)
                delta, m_new, v_new = _adam_math(w_refs[j][rows, :], g, m_refs[j][rows, :], v_refs[j][rows, :])
                g_out[rows, :] = g
                d_out[rows, :] = delta
                m_out[rows, :] = m_new
                v_out[rows, :] = v_new

    whole = pl.BlockSpec(memory_space=pltpu.VMEM)
    operands = list(sums) + [w[k] for k in names] + [m[k] for k in names] + [v[k] for k in names]
    outs = pl.pallas_call(
        body, name="adam_replicated", in_specs=[whole] * len(operands), out_specs=[whole] * (4 * n),
        out_shape=[jax.ShapeDtypeStruct(w[k].shape, F32) for k in names for _ in range(4)],
    )(*operands)
    return {name: tuple(outs[4 * j:4 * j + 4]) for j, name in enumerate(names)}


def _adam_small(w, g, m, v):
    def body(w_ref, g_ref, m_ref, v_ref, d_out, m_out, v_out):
        delta, m_new, v_new = _adam_math(w_ref[...], g_ref[...], m_ref[...], v_ref[...])
        d_out[...] = delta
        m_out[...] = m_new
        v_out[...] = v_new

    whole = pl.BlockSpec(memory_space=pltpu.VMEM)
    sds = jax.ShapeDtypeStruct(w.shape, F32)
    return pl.pallas_call(body, name="adam_small", in_specs=[whole] * 4, out_specs=[whole] * 3,
                          out_shape=[sds, sds, sds])(w, g, m, v)


_WEIGHTS = ["w_in", "w_conv_dw", "b_conv_dw", "conv_ln_g", "conv_ln_b", "w_conv_out", "hgrn_lb_logits", "hgrn_norm_g",
            "w_hgrn_out", "w_out", "ln1_g", "ln1_b", "w_ffn_in", "w_ffn_dw", "b_ffn_dw", "w_ffn_out", "ln2_g", "ln2_b"]
_LARGE = ["w_in", "w_conv_out", "w_hgrn_out", "w_out", "w_ffn_in", "w_ffn_out"]
_CONV_DW_SHARD = CONV_DIM // N_DEV
_FFN_DW_SHARD = D_FF // N_DEV


def kernel(x, w_in, w_conv_dw, b_conv_dw, conv_ln_g, conv_ln_b, w_conv_out, hgrn_lb_logits, hgrn_norm_g, w_hgrn_out, w_out, ln1_g, ln1_b, w_ffn_in, w_ffn_dw, b_ffn_dw, w_ffn_out, ln2_g, ln2_b, loss_target, m_w_in, m_w_conv_dw, m_b_conv_dw, m_conv_ln_g, m_conv_ln_b, m_w_conv_out, m_hgrn_lb_logits, m_hgrn_norm_g, m_w_hgrn_out, m_w_out, m_ln1_g, m_ln1_b, m_w_ffn_in, m_w_ffn_dw, m_b_ffn_dw, m_w_ffn_out, m_ln2_g, m_ln2_b, v_w_in, v_w_conv_dw, v_b_conv_dw, v_conv_ln_g, v_conv_ln_b, v_w_conv_out, v_hgrn_lb_logits, v_hgrn_norm_g, v_w_hgrn_out, v_w_out, v_ln1_g, v_ln1_b, v_w_ffn_in, v_w_ffn_dw, v_b_ffn_dw, v_w_ffn_out, v_ln2_g, v_ln2_b):
    w = dict(w_in=w_in, w_conv_dw=w_conv_dw, b_conv_dw=b_conv_dw, conv_ln_g=conv_ln_g, conv_ln_b=conv_ln_b,
             w_conv_out=w_conv_out, hgrn_lb_logits=hgrn_lb_logits, hgrn_norm_g=hgrn_norm_g, w_hgrn_out=w_hgrn_out,
             w_out=w_out, ln1_g=ln1_g, ln1_b=ln1_b, w_ffn_in=w_ffn_in, w_ffn_dw=w_ffn_dw, b_ffn_dw=b_ffn_dw,
             w_ffn_out=w_ffn_out, ln2_g=ln2_g, ln2_b=ln2_b)
    m = dict(w_in=m_w_in, w_conv_dw=m_w_conv_dw, b_conv_dw=m_b_conv_dw, conv_ln_g=m_conv_ln_g, conv_ln_b=m_conv_ln_b,
             w_conv_out=m_w_conv_out, hgrn_lb_logits=m_hgrn_lb_logits, hgrn_norm_g=m_hgrn_norm_g,
             w_hgrn_out=m_w_hgrn_out, w_out=m_w_out, ln1_g=m_ln1_g, ln1_b=m_ln1_b, w_ffn_in=m_w_ffn_in,
             w_ffn_dw=m_w_ffn_dw, b_ffn_dw=m_b_ffn_dw, w_ffn_out=m_w_ffn_out, ln2_g=m_ln2_g, ln2_b=m_ln2_b)
    v = dict(w_in=v_w_in, w_conv_dw=v_w_conv_dw, b_conv_dw=v_b_conv_dw, conv_ln_g=v_conv_ln_g, conv_ln_b=v_conv_ln_b,
             w_conv_out=v_w_conv_out, hgrn_lb_logits=v_hgrn_lb_logits, hgrn_norm_g=v_hgrn_norm_g,
             w_hgrn_out=v_w_hgrn_out, w_out=v_w_out, ln1_g=v_ln1_g, ln1_b=v_ln1_b, w_ffn_in=v_w_ffn_in,
             w_ffn_dw=v_w_ffn_dw, b_ffn_dw=v_b_ffn_dw, w_ffn_out=v_w_ffn_out, ln2_g=v_ln2_g, ln2_b=v_ln2_b)
    xi, yi, ci = lax.axis_index("x"), lax.axis_index("y"), lax.axis_index("c")
    me = 4 * xi + 2 * yi + ci
    me_op = jnp.reshape(me, (1,)).astype(jnp.int32)

    shards = [w[name][0].astype(BF16) for name in _LARGE]
    shards.append(jnp.pad(w_conv_dw[0], ((0, 1), (0, 128 - _CONV_DW_SHARD))))
    shards.append(jnp.pad(w_ffn_dw[0], ((0, 8 - FFN_K), (0, 384 - _FFN_DW_SHARD))))
    chips = [(1 - xi, yi), (xi, 1 - yi), (1 - xi, 1 - yi)]
    order = jnp.stack([me, me ^ 1] + [4 * px + 2 * py + ci for px, py in chips]
                      + [4 * px + 2 * py + 1 - ci for px, py in chips]).astype(jnp.int32)
    small = dict(b_conv_dw=b_conv_dw, conv_ln_g=conv_ln_g, conv_ln_b=conv_ln_b, hgrn_lb_logits=hgrn_lb_logits,
                 hgrn_norm_g=hgrn_norm_g, ln1_g=ln1_g, ln1_b=ln1_b, ln2_g=ln2_g, ln2_b=ln2_b, b_ffn_dw=b_ffn_dw)

    grad_x, large_grads, small_sums = _local_step(x[0], loss_target[0], _gather(shards, staged=True), small,
                                                   _scatter, order)

    out = {}
    for name, (own, recv) in zip(_LARGE, large_grads):
        out[name] = _adam_large("adam_" + name, own, recv, me_op, w[name][0], m[name][0], v[name][0])

    totals = _small_allreduce(list(small_sums))
    out.update(_adam_replicated(totals[:4], w, m, v))
    summed = _small_views(totals)
    loss = summed["loss"][0, 0]
    conv_dw_g = lax.dynamic_slice_in_dim(summed["w_conv_dw"].reshape(CONV_K, CONV_DIM), me * _CONV_DW_SHARD, _CONV_DW_SHARD, axis=1)
    ffn_dw_g = lax.dynamic_slice_in_dim(summed["w_ffn_dw"].reshape(FFN_K, D_FF), me * _FFN_DW_SHARD, _FFN_DW_SHARD, axis=1)
    small_g = dict(b_ffn_dw=summed["b_ffn_dw"], w_conv_dw=conv_dw_g.reshape(1, -1), w_ffn_dw=ffn_dw_g.reshape(1, -1))
    names = list(small_g)
    flat = lambda d, n: d[n].reshape(1, -1)
    n_small = sum(small_g[n].shape[1] for n in names)
    pad = (-n_small) % 1024
    pack = lambda pieces: jnp.pad(jnp.concatenate(pieces, axis=1), ((0, 0), (0, pad))).reshape(-1, 128)
    d_s, m_s, v_s = _adam_small(pack([flat(w, n) for n in names]), pack([small_g[n] for n in names]),
                                pack([flat(m, n) for n in names]), pack([flat(v, n) for n in names]))
    pos = 0
    for n in names:
        size = small_g[n].shape[1]
        cut = lambda a: a.reshape(1, -1)[:, pos:pos + size].reshape(w[n].shape)
        out[n] = (small_g[n].reshape(w[n].shape), cut(d_s), cut(m_s), cut(v_s))
        pos += size

    for name in _LARGE:
        out[name] = tuple(a.reshape(w[name].shape) for a in out[name])
    grads = [out[n][0] for n in _WEIGHTS]
    deltas = [out[n][1] for n in _WEIGHTS]
    new_m = [out[n][2] for n in _WEIGHTS]
    new_v = [out[n][3] for n in _WEIGHTS]
    return (loss, grad_x[None], *grads, *deltas, *new_m, *new_v)
```

```python
import functools
import math

import jax
import jax.numpy as jnp
from jax import lax
from jax.experimental import pallas as pl
from jax.experimental.pallas import tpu as pltpu

F32 = jnp.float32
BF16 = jnp.bfloat16

N_DEV = 8
D_MODEL = 1024
CONV_DIM = 512
CONV_K = 31
HGRN_DIM = 1024
HEADS = 8
HEAD_DIM = 128
D_FF = 2816
FFN_K = 3
FF_SHARD = 2 * D_FF // N_DEV
IN_COLS = 7168
IN_SHARD = IN_COLS // N_DEV
LN_EPS = 1e-5
RMS_EPS = 1e-6
ALPHA = 2.0 ** 0.25

ADAM_LR = 0.001
ADAM_B1 = 0.9
ADAM_B2 = 0.999
ADAM_EPS = 1e-08
ADAM_WD = 0.01
ADAM_STEP = 10

CHUNK = 64
CHUNKS_PER_BLOCK = 16
CONV_HALO = 32
FFN_HALO = 8
ROW_BLOCK = 64
SUBLANES = 8
VMEM_LIMIT = 48 * 1024 * 1024
MXU_DEPTH = 256

DP_MERGE_BLOCK = 0
DP_CONV_BLOCK = 2
DP_HEAD_BLOCK = 6

MESH = pl.DeviceIdType.MESH
ANY = pl.BlockSpec(memory_space=pl.ANY)

NN = (((1,), (0,)), ((), ()))
NT = (((1,), (1,)), ((), ()))
TN = (((0,), (0,)), ((), ()))


def _params(sem):
    return pltpu.CompilerParams(dimension_semantics=sem, vmem_limit_bytes=VMEM_LIMIT)


def _dot(a, b, dims):
    return lax.dot_general(a.astype(BF16), b.astype(BF16), dims, preferred_element_type=F32)


def _sigmoid(x):
    return jax.nn.sigmoid(x)


def _ln(r):
    mu = jnp.mean(r, axis=-1, keepdims=True)
    xc = r - mu
    var = jnp.mean(xc * xc, axis=-1, keepdims=True)
    rstd = lax.rsqrt(var + LN_EPS)
    return xc * rstd, rstd


def _ln_bwd(dy, xhat, rstd, g):
    dxh = dy * g
    m1 = jnp.mean(dxh, axis=-1, keepdims=True)
    m2 = jnp.mean(dxh * xhat, axis=-1, keepdims=True)
    return rstd * (dxh - m1 - xhat * m2)


def _colsum(x):
    return jnp.sum(x, axis=0, keepdims=True)


class _Hosted:
    def __init__(self, inputs, out_shapes, sem_shapes, start, finish):
        self.inputs, self.out_shapes, self.sem_shapes = list(inputs), list(out_shapes), list(sem_shapes)
        self.start, self.finish = start, finish


def _call(body, *, name, grid, in_specs, out_specs, out_shape, scratch_shapes, semantics, operands, hosted=None,
          aliases=None):
    aliases = aliases or {}
    if hosted is None:
        return pl.pallas_call(
            body, name=name, grid=grid, in_specs=list(in_specs), out_specs=list(out_specs), out_shape=list(out_shape),
            scratch_shapes=list(scratch_shapes), input_output_aliases=aliases,
            compiler_params=_params(semantics))(*operands)
    n_in, n_out, n_scr = len(in_specs), len(out_specs), len(scratch_shapes)
    h_in, h_out = len(hosted.inputs), len(hosted.out_shapes)

    def full_body(*refs):
        ins, refs = refs[:n_in], refs[n_in:]
        h_ins, refs = refs[:h_in], refs[h_in:]
        outs, refs = refs[:n_out], refs[n_out:]
        h_outs, refs = refs[:h_out], refs[h_out:]
        scr, sems = refs[:n_scr], refs[n_scr:]
        first = functools.reduce(jnp.logical_and, [pl.program_id(d) == 0 for d in range(len(grid))])
        last = functools.reduce(jnp.logical_and, [pl.program_id(d) == grid[d] - 1 for d in range(len(grid))])

        @pl.when(first)
        def _():
            hosted.start(h_ins, h_outs, sems)

        body(*ins, *outs, *scr)

        @pl.when(last)
        def _():
            hosted.finish(h_ins, h_outs, sems)

    return pl.pallas_call(
        full_body, name=name, grid=grid, in_specs=list(in_specs) + [ANY] * h_in,
        out_specs=list(out_specs) + [ANY] * h_out, out_shape=list(out_shape) + hosted.out_shapes,
        scratch_shapes=list(scratch_shapes) + hosted.sem_shapes, input_output_aliases=aliases,
        compiler_params=pltpu.CompilerParams(dimension_semantics=("arbitrary",) * len(grid),
                                             vmem_limit_bytes=VMEM_LIMIT, has_side_effects=True),
    )(*operands, *hosted.inputs)


def _mm(name, a, b, out_shape, out_dtype, grid, a_spec, b_spec, o_spec, dims, acc_shape, hosted=None):
    nk = grid[2]
    if nk == 1:
        def body(a_ref, b_ref, o_ref):
            o_ref[...] = _dot(a_ref[...], b_ref[...], dims).astype(o_ref.dtype)
        scratch = []
    else:
        def body(a_ref, b_ref, o_ref, acc_ref):
            k = pl.program_id(2)

            @pl.when(k == 0)
            def _():
                acc_ref[...] = jnp.zeros_like(acc_ref)

            acc_ref[...] += _dot(a_ref[...], b_ref[...], dims)

            @pl.when(k == nk - 1)
            def _():
                o_ref[...] = acc_ref[...].astype(o_ref.dtype)
        scratch = [pltpu.VMEM(acc_shape, F32)]

    outs = _call(body, name=name, grid=grid, in_specs=[a_spec, b_spec], out_specs=[o_spec],
                 out_shape=[jax.ShapeDtypeStruct(out_shape, out_dtype)], scratch_shapes=scratch,
                 semantics=("parallel", "parallel", "arbitrary"), operands=(a, b), hosted=hosted)
    return outs[0] if hosted is None else (outs[0], list(outs[1:]))


def _mm_fused(name, grid, operands, in_specs, out_shape, out_specs, dims, acc_shape, epilogue, lhs=None, scratch=(),
              hosted=None):
    nk = grid[2]
    n_in, n_out = len(in_specs), len(out_specs)

    def body(*refs):
        ins, outs, scr = refs[:n_in], refs[n_in:n_in + n_out], refs[n_in + n_out:]
        acc_ref, k = scr[0], pl.program_id(2)
        a = ins[0][...] if lhs is None else lhs(ins, outs)
        part = _dot(a, ins[1][...], dims)
        if nk == 1:
            acc_ref[...] = part
            epilogue(acc_ref, ins, outs, scr[1:])
            return

        @pl.when(k == 0)
        def _():
            acc_ref[...] = jnp.zeros_like(acc_ref)

        acc_ref[...] += part

        @pl.when(k == nk - 1)
        def _():
            epilogue(acc_ref, ins, outs, scr[1:])

    return _call(body, name=name, grid=grid, in_specs=in_specs, out_specs=out_specs, out_shape=out_shape,
                 scratch_shapes=[pltpu.VMEM(acc_shape, F32)] + list(scratch), semantics=("arbitrary",) * 3,
                 operands=operands, hosted=hosted)


def _row_blocks(rows, block=256):
    block = block if rows % block == 0 else rows
    return [slice(r, r + block) for r in range(0, rows, block)]


def _pick(t, pref):
    return pref if t % pref == 0 else t


def _glu(p):
    return p[:, :CONV_DIM] * _sigmoid(p[:, CONV_DIM:])


def _by_phase(taps):
    phases = {}
    for off, payload in taps:
        phases.setdefault(off % SUBLANES, []).append((off - off % SUBLANES, payload))
    return sorted(phases.items())


def _tap_sum(src_ref, base, taps, rows, lanes):
    acc = None
    for phase, items in _by_phase(taps):
        n = rows if phase == 0 else rows + SUBLANES
        part = None
        for off, (w_ref, k) in items:
            term = w_ref[k:k + 1, lanes] * src_ref[base + off:base + off + n, lanes]
            part = term if part is None else part + term
        if phase:
            part = part[phase:phase + rows, :]
        acc = part if acc is None else acc + part
    return acc


def _tap_products(x, src_ref, base, taps, lanes):
    rows, cols = x.shape
    pad = jnp.zeros((SUBLANES, cols), x.dtype)
    padded = jnp.concatenate([pad, x, pad], axis=0)
    out = []
    for phase, items in _by_phase(taps):
        n = rows if phase == 0 else rows + SUBLANES
        shifted = x if phase == 0 else padded[SUBLANES - phase:SUBLANES - phase + n, :]
        for off, key in items:
            out.append((key, _colsum(shifted * src_ref[base + off:base + off + n, lanes])))
    return out


def _lane_blocks(cols, block=256):
    return [slice(c, min(c + block, cols)) for c in range(0, cols, block)]


def _conv_fwd(proj, w_dw, b_dw, g, b):
    t = proj.shape[0]
    tm = _pick(t, 512)
    nh = tm // CONV_HALO

    def body(p_ref, ph_ref, w_ref, bd_ref, g_ref, b_ref, act_ref, pre_ref, xs_ref):
        i = pl.program_id(0)
        halo = _glu(ph_ref[...])
        xs_ref[0:CONV_HALO, :] = jnp.where(i == 0, 0.0, halo)
        xs_ref[CONV_HALO:CONV_HALO + tm, :] = _glu(p_ref[...])
        taps = [(CONV_HALO - (CONV_K - 1) + k, (w_ref, k)) for k in range(CONV_K)]
        for r in range(tm // ROW_BLOCK):
            rows = slice(r * ROW_BLOCK, (r + 1) * ROW_BLOCK)
            for lanes in _lane_blocks(CONV_DIM):
                pre_ref[rows, lanes] = bd_ref[:, lanes] + _tap_sum(xs_ref, r * ROW_BLOCK, taps, ROW_BLOCK, lanes)
            acc = pre_ref[rows, :]
            xhat, _ = _ln(acc)
            yln = xhat * g_ref[...] + b_ref[...]
            act_ref[rows, :] = (yln * _sigmoid(yln)).astype(BF16)

    full = lambda s: pl.BlockSpec(s, lambda i: (0, 0))
    return pl.pallas_call(
        body, name="conv_fwd", grid=(t // tm,),
        in_specs=[pl.BlockSpec((tm, 2 * CONV_DIM), lambda i: (i, 0)),
                  pl.BlockSpec((CONV_HALO, 2 * CONV_DIM), lambda i: (jnp.maximum(i * nh - 1, 0), 0)),
                  full((CONV_K, CONV_DIM)), full((1, CONV_DIM)), full((1, CONV_DIM)), full((1, CONV_DIM))],
        out_specs=[pl.BlockSpec((tm, CONV_DIM), lambda i: (i, 0)), pl.BlockSpec((tm, CONV_DIM), lambda i: (i, 0))],
        out_shape=[jax.ShapeDtypeStruct((t, CONV_DIM), BF16), jax.ShapeDtypeStruct((t, CONV_DIM), F32)],
        scratch_shapes=[pltpu.VMEM((CONV_HALO + tm, CONV_DIM), F32)],
        compiler_params=_params(("arbitrary",)),
    )(proj, proj, w_dw, b_dw, g, b)


def _d_c_norm_bwd(d_y, w_conv_out, pre, g, b):
    t = pre.shape[0]
    tm = _pick(t, 512)
    d = D_MODEL

    def epilogue(d_c, ins, outs, scr):
        pre_ref, g_ref, b_ref = ins[2:]
        dpre_ref, sums_ref = outs
        i = pl.program_id(0)

        @pl.when(i == 0)
        def _():
            sums_ref[...] = jnp.zeros_like(sums_ref)

        for rows in _row_blocks(tm):
            xhat, rstd = _ln(pre_ref[rows, :])
            yln = xhat * g_ref[...] + b_ref[...]
            sg = _sigmoid(yln)
            dyln = d_c[rows, :] * (sg * (1.0 + yln * (1.0 - sg)))
            dpre = _ln_bwd(dyln, xhat, rstd, g_ref[...])
            dpre_ref[rows, :] = dpre
            sums_ref[0:1, :] += _colsum(dyln * xhat)
            sums_ref[1:2, :] += _colsum(dyln)
            sums_ref[2:3, :] += _colsum(dpre)

    full = lambda s: pl.BlockSpec(s, lambda i, j, k: (0, 0))
    tile = pl.BlockSpec((tm, CONV_DIM), lambda i, j, k: (i, 0))
    return _mm_fused(
        "d_c_norm_bwd", (t // tm, 1, 1), (d_y, w_conv_out, pre, g, b),
        [pl.BlockSpec((None, tm, d), lambda i, j, k: (0, i, 0)), full((CONV_DIM, d)), tile,
         full((1, CONV_DIM)), full((1, CONV_DIM))],
        [jax.ShapeDtypeStruct((t, CONV_DIM), F32), jax.ShapeDtypeStruct((8, CONV_DIM), F32)],
        [tile, full((8, CONV_DIM))], NT, (tm, CONV_DIM), epilogue)


def _conv_bwd_dw(d_pre, proj, w_dw, d_proj):
    t = d_pre.shape[0]
    tm = _pick(t, 512)
    nt = t // tm
    nh = tm // CONV_HALO
    last_h = t // CONV_HALO - 1

    def body(dp_ref, dph_ref, p_ref, ph_ref, w_ref, _, dproj_ref, dw_ref, xs_ref, ds_ref):
        i = pl.program_id(0)

        @pl.when(i == 0)
        def _():
            dw_ref[...] = jnp.zeros_like(dw_ref)

        halo = _glu(ph_ref[...])
        xs_ref[0:CONV_HALO, :] = jnp.where(i == 0, 0.0, halo)
        xs_ref[CONV_HALO:CONV_HALO + tm, :] = _glu(p_ref[...])
        ds_ref[0:tm, :] = dp_ref[...]
        ds_ref[tm:tm + CONV_HALO, :] = jnp.where(i == nt - 1, 0.0, dph_ref[...])
        back_taps = [(CONV_K - 1 - k, (w_ref, k)) for k in range(CONV_K)]
        grad_taps = [(CONV_HALO - (CONV_K - 1) + k, k) for k in range(CONV_K)]
        for r in range(tm // ROW_BLOCK):
            base = r * ROW_BLOCK
            rows = slice(base, base + ROW_BLOCK)
            for lanes in _lane_blocks(CONV_DIM):
                gate_lanes = slice(CONV_DIM + lanes.start, CONV_DIM + lanes.stop)
                acc = _tap_sum(ds_ref, base, back_taps, ROW_BLOCK, lanes)
                for k, total in _tap_products(ds_ref[rows, lanes], xs_ref, base, grad_taps, lanes):
                    dw_ref[k:k + 1, lanes] += total
                cval = p_ref[rows, lanes]
                sg = _sigmoid(p_ref[rows, gate_lanes])
                dproj_ref[rows, lanes] = (acc * sg).astype(BF16)
                dproj_ref[rows, gate_lanes] = (acc * cval * sg * (1.0 - sg)).astype(BF16)

    full = lambda s: pl.BlockSpec(s, lambda i: (0, 0))
    return pl.pallas_call(
        body, name="conv_bwd_dw", grid=(nt,),
        in_specs=[pl.BlockSpec((tm, CONV_DIM), lambda i: (i, 0)),
                  pl.BlockSpec((CONV_HALO, CONV_DIM), lambda i: (jnp.minimum((i + 1) * nh, last_h), 0)),
                  pl.BlockSpec((tm, 2 * CONV_DIM), lambda i: (i, 0)),
                  pl.BlockSpec((CONV_HALO, 2 * CONV_DIM), lambda i: (jnp.maximum(i * nh - 1, 0), 0)),
                  full((CONV_K, CONV_DIM)), ANY],
        out_specs=[pl.BlockSpec((tm, 2 * CONV_DIM), lambda i: (i, DP_CONV_BLOCK)), full((CONV_HALO, CONV_DIM))],
        out_shape=[jax.ShapeDtypeStruct(d_proj.shape, BF16), jax.ShapeDtypeStruct((CONV_HALO, CONV_DIM), F32)],
        scratch_shapes=[pltpu.VMEM((CONV_HALO + tm, CONV_DIM), F32), pltpu.VMEM((tm + CONV_HALO, CONV_DIM), F32)],
        input_output_aliases={5: 0},
        compiler_params=_params(("arbitrary",)),
    )(d_pre, d_pre, proj, proj, w_dw, d_proj)


def _lower_bound(logit_ref):
    l0 = logit_ref[0:1, :]
    l1 = logit_ref[1:2, :]
    m = jnp.maximum(l0, l1)
    e0 = jnp.exp(l0 - m)
    e1 = jnp.exp(l1 - m)
    return e0 / (e0 + e1)


def _tri(lower):
    r = lax.broadcasted_iota(jnp.int32, (CHUNK, CHUNK), 0)
    c = lax.broadcasted_iota(jnp.int32, (CHUNK, CHUNK), 1)
    return (c <= r) if lower else (c >= r)


def _hgrn_gates(fz, lb):
    s = _sigmoid(fz)
    sn = _sigmoid(-fz)
    f = lb + (1.0 - lb) * s
    return s, sn, f


def _block_tri(rows, lower=True):
    r = lax.broadcasted_iota(jnp.int32, (rows, rows), 0)
    c = lax.broadcasted_iota(jnp.int32, (rows, rows), 1)
    tri = (c <= r) if lower else (c >= r)
    return (tri & (r // CHUNK == c // CHUNK)).astype(BF16)


def _tri_rows(tm):
    return min(tm, MXU_DEPTH)


def _tri_matmul(tri_ref, x):
    hi = x.astype(BF16)
    lo = (x - hi.astype(F32)).astype(BF16)
    tri = tri_ref[...]
    return (lax.dot_general(tri, hi, NN, preferred_element_type=F32)
            + lax.dot_general(tri, lo, NN, preferred_element_type=F32))


def _groups(tm):
    g = _tri_rows(tm)
    return [slice(i * g, (i + 1) * g) for i in range(tm // g)]


def _hgrn_fwd(proj, logits, norm_g):
    t = proj.shape[0]
    tm = CHUNK * CHUNKS_PER_BLOCK if t % (CHUNK * CHUNKS_PER_BLOCK) == 0 else CHUNK
    cpb = tm // CHUNK
    nt = t // tm
    half = CHUNK // 2

    def body(qz_ref, fz_ref, iv_ref, gz_ref, lg_ref, ng_ref, tri_ref, o_ref, og_ref, st_ref,
             state_ref, qe_ref, ke_ref, qb_ref, kl_ref, v_ref, upd_ref, decay_ref, a_ref, q_ref, kk_ref, b_ref):
        j = pl.program_id(1)

        @pl.when(j == 0)
        def _():
            state_ref[...] = jnp.zeros_like(state_ref)

        lb = _lower_bound(lg_ref)
        chunks = [slice(c * CHUNK, (c + 1) * CHUNK) for c in range(cpb)]
        for rows in chunks:
            qz = qz_ref[rows, :]
            q_ref[rows, :] = qz * _sigmoid(qz)
            _, sn, f = _hgrn_gates(fz_ref[rows, :], lb)
            kk_ref[rows, :] = (1.0 - lb) * sn
            b_ref[rows, :] = jnp.log(f)
            v_ref[rows, :] = iv_ref[rows, :].astype(BF16)
        for rows in _groups(tm):
            b_ref[rows, :] = _tri_matmul(tri_ref, b_ref[rows, :])
        for c, rows in enumerate(chunks):
            b = b_ref[rows, :]
            bref = b[half - 1:half, :]
            blast = b[CHUNK - 1:CHUNK, :]
            q = q_ref[rows, :]
            kk = kk_ref[rows, :]
            qb_ref[rows, :] = (q * jnp.exp(b)).astype(BF16)
            qe_ref[rows, :] = (q * jnp.exp(b - bref)).astype(BF16)
            ke_ref[rows, :] = (kk * jnp.exp(bref - b)).astype(BF16)
            kl_ref[rows, :] = (kk * jnp.exp(blast - b)).astype(BF16)
            decay_ref[c:c + 1, :] = jnp.exp(blast)
        causal = _tri(True)
        for c, rows in enumerate(chunks):
            upd_ref[c] = _dot(v_ref[rows, :], kl_ref[rows, :], TN)
            a_ref[c] = jnp.where(causal, _dot(qe_ref[rows, :], ke_ref[rows, :], NT), 0.0).astype(BF16)
        state = state_ref[...]
        for c in range(cpb):
            st_ref[c] = state.astype(BF16)
            state = state * decay_ref[c:c + 1, :] + upd_ref[c]
        state_ref[...] = state
        for c, rows in enumerate(chunks):
            o_ref[rows, :] = _dot(a_ref[c], v_ref[rows, :], NN) + _dot(qb_ref[rows, :], st_ref[c], NT)
        for rows in chunks:
            o = o_ref[rows, :]
            r = lax.rsqrt(jnp.mean(o * o, axis=-1, keepdims=True) + RMS_EPS)
            gz = gz_ref[rows, :]
            og_ref[rows, :] = ((o * r * ng_ref[...]) * (gz * _sigmoid(gz))).astype(BF16)

    col = lambda base: pl.BlockSpec((tm, HEAD_DIM), lambda h, j: (j, base + h))
    tile_bf = pltpu.VMEM((tm, HEAD_DIM), BF16)
    tile_f32 = pltpu.VMEM((tm, HEAD_DIM), F32)
    return pl.pallas_call(
        body, name="hgrn_fwd", grid=(HEADS, nt),
        in_specs=[col(8), col(16), col(24), col(32),
                  pl.BlockSpec((2, HEAD_DIM), lambda h, j: (0, h)), pl.BlockSpec((1, HEAD_DIM), lambda h, j: (0, h)),
                  pl.BlockSpec((_tri_rows(tm), _tri_rows(tm)), lambda h, j: (0, 0))],
        out_specs=[col(0), col(0), pl.BlockSpec((None, cpb, HEAD_DIM, HEAD_DIM), lambda h, j: (h, j, 0, 0))],
        out_shape=[jax.ShapeDtypeStruct((t, HGRN_DIM), F32), jax.ShapeDtypeStruct((t, HGRN_DIM), BF16),
                   jax.ShapeDtypeStruct((HEADS, t // CHUNK, HEAD_DIM, HEAD_DIM), BF16)],
        scratch_shapes=[pltpu.VMEM((HEAD_DIM, HEAD_DIM), F32), tile_bf, tile_bf, tile_bf, tile_bf, tile_bf,
                        pltpu.VMEM((cpb, HEAD_DIM, HEAD_DIM), F32), pltpu.VMEM((max(cpb, 8), HEAD_DIM), F32),
                        pltpu.VMEM((cpb, CHUNK, CHUNK), BF16), tile_f32, tile_f32, tile_f32],
        compiler_params=_params(("parallel", "arbitrary")),
    )(proj, proj, proj, proj, logits, norm_g, _block_tri(_tri_rows(tm)))


def _hgrn_bwd(d_og, o, proj, states, logits, norm_g, d_proj, hosted=None):
    t = proj.shape[0]
    tm = CHUNK * CHUNKS_PER_BLOCK if t % (CHUNK * CHUNKS_PER_BLOCK) == 0 else CHUNK
    cpb = tm // CHUNK
    nt = t // tm
    half = CHUNK // 2

    def body(dog_ref, o_ref, qz_ref, fz_ref, iv_ref, gz_ref, st_ref, lg_ref, ng_ref, tril_ref, triu_ref,
             _, dp_ref, sums_ref,
             dstate_ref, qe_ref, ke_ref, qb_ref, kl_ref, v_ref, do_ref, upd_ref, dst_ref, a_ref, da_ref,
             decay_ref, through_ref, q_ref, kk_ref, b_ref, dsilu_ref, gs_ref, gf_ref, sn_ref,
             eb_ref, ebr_ref, ekr_ref, ebl_ref, rev_ref, pre_ref, dk_ref):
        j = pl.program_id(1)

        @pl.when(j == 0)
        def _():
            dstate_ref[...] = jnp.zeros_like(dstate_ref)
            sums_ref[...] = jnp.zeros_like(sums_ref)

        lb = _lower_bound(lg_ref)
        ng = ng_ref[...]
        chunks = [slice(c * CHUNK, (c + 1) * CHUNK) for c in range(cpb)]
        for rows in chunks:
            qz = qz_ref[rows, :]
            sq = _sigmoid(qz)
            q_ref[rows, :] = qz * sq
            dsilu_ref[rows, :] = sq * (1.0 + qz * (1.0 - sq))
            s, sn, f = _hgrn_gates(fz_ref[rows, :], lb)
            kk_ref[rows, :] = (1.0 - lb) * sn
            b_ref[rows, :] = jnp.log(f)
            sn_ref[rows, :] = sn
            gf_ref[rows, :] = sn / f
            gs_ref[rows, :] = (1.0 - lb) * s
            v_ref[rows, :] = iv_ref[rows, :].astype(BF16)
            ov = o_ref[rows, :]
            r = lax.rsqrt(jnp.mean(ov * ov, axis=-1, keepdims=True) + RMS_EPS)
            on = ov * r
            gz = gz_ref[rows, :]
            sg = _sigmoid(gz)
            dog = dog_ref[rows, :]
            dp_ref[rows, 3 * HEAD_DIM:4 * HEAD_DIM] =(dog * (on * ng) * (sg * (1.0 + gz * (1.0 - sg)))).astype(BF16)
            d_ong = dog * (gz * sg)
            sums_ref[0:1, :] += _colsum(d_ong * on)
            d_on = d_ong * ng
            do_ref[rows, :] = (r * (d_on - on * jnp.mean(d_on * on, axis=-1, keepdims=True))).astype(BF16)
        for rows in _groups(tm):
            b_ref[rows, :] = _tri_matmul(tril_ref, b_ref[rows, :])
        for c, rows in enumerate(chunks):
            b = b_ref[rows, :]
            bref = b[half - 1:half, :]
            blast = b[CHUNK - 1:CHUNK, :]
            q = q_ref[rows, :]
            kk = kk_ref[rows, :]
            eb = jnp.exp(b)
            ebr = jnp.exp(b - bref)
            ekr = jnp.exp(bref - b)
            ebl = jnp.exp(blast - b)
            eb_ref[rows, :] = eb
            ebr_ref[rows, :] = ebr
            ekr_ref[rows, :] = ekr
            ebl_ref[rows, :] = ebl
            qb_ref[rows, :] = (q * eb).astype(BF16)
            qe_ref[rows, :] = (q * ebr).astype(BF16)
            ke_ref[rows, :] = (kk * ekr).astype(BF16)
            kl_ref[rows, :] = (kk * ebl).astype(BF16)
            decay_ref[c:c + 1, :] = jnp.exp(blast)
        causal = _tri(True)
        for c, rows in enumerate(chunks):
            upd_ref[c] = _dot(do_ref[rows, :], qb_ref[rows, :], TN)
            a_ref[c] = jnp.where(causal, _dot(qe_ref[rows, :], ke_ref[rows, :], NT), 0.0).astype(BF16)
            da_ref[c] = jnp.where(causal, _dot(do_ref[rows, :], v_ref[rows, :], NT), 0.0).astype(BF16)
        dstate = dstate_ref[...]
        for c in reversed(range(cpb)):
            dst_ref[c] = dstate.astype(BF16)
            decay = decay_ref[c:c + 1, :]
            through_ref[c:c + 1, :] = decay * _colsum(dstate * st_ref[c].astype(F32))
            dstate = dstate * decay + upd_ref[c]
        dstate_ref[...] = dstate
        for c, rows in enumerate(chunks):
            dp_ref[rows, 2 * HEAD_DIM:3 * HEAD_DIM] =(_dot(a_ref[c], do_ref[rows, :], TN)
                                + _dot(kl_ref[rows, :], dst_ref[c], NT)).astype(BF16)
        for c, rows in enumerate(chunks):
            dqe = _dot(da_ref[c], ke_ref[rows, :], NN)
            dq_inter = _dot(do_ref[rows, :], st_ref[c], NN) * eb_ref[rows, :]
            dp_ref[rows, 0:HEAD_DIM] =((dqe * ebr_ref[rows, :] + dq_inter) * dsilu_ref[rows, :]).astype(BF16)
            rev_ref[rows, :] = qe_ref[rows, :].astype(F32) * dqe + q_ref[rows, :] * dq_inter
        for c, rows in enumerate(chunks):
            dke = _dot(da_ref[c], qe_ref[rows, :], TN)
            dk_inter = _dot(v_ref[rows, :], dst_ref[c], NN) * ebl_ref[rows, :]
            dk_ref[rows, :] = dke * ekr_ref[rows, :] + dk_inter
            rev_ref[rows, :] -= ke_ref[rows, :].astype(F32) * dke
            pre_ref[rows, :] = kk_ref[rows, :] * dk_inter
        for rows in _groups(tm):
            pre = pre_ref[rows, :]
            rev_ref[rows, :] = _tri_matmul(triu_ref, rev_ref[rows, :]) + (_tri_matmul(tril_ref, pre) - pre)
        for c, rows in enumerate(chunks):
            dlf = rev_ref[rows, :] + through_ref[c:c + 1, :]
            common = gf_ref[rows, :] * dlf - sn_ref[rows, :] * dk_ref[rows, :]
            dp_ref[rows, HEAD_DIM:2 * HEAD_DIM] =(gs_ref[rows, :] * common).astype(BF16)
            sums_ref[1:2, :] += _colsum(common)

        @pl.when(j == nt - 1)
        def _():
            sums_ref[1:2, :] = sums_ref[1:2, :] * lb * (1.0 - lb)

    rev = lambda base: pl.BlockSpec((tm, HEAD_DIM), lambda h, j: (nt - 1 - j, base + h))
    vec = lambda n: pl.BlockSpec((n, HEAD_DIM), lambda h, j: (0, h))
    const = pl.BlockSpec((_tri_rows(tm), _tri_rows(tm)), lambda h, j: (0, 0))
    tile_bf = pltpu.VMEM((tm, HEAD_DIM), BF16)
    tile_f32 = pltpu.VMEM((tm, HEAD_DIM), F32)
    square = lambda dtype: pltpu.VMEM((cpb, HEAD_DIM, HEAD_DIM), dtype)
    rows8 = pltpu.VMEM((max(cpb, 8), HEAD_DIM), F32)
    operands = (d_og, o, proj, proj, proj, proj, states, logits, norm_g, _block_tri(_tri_rows(tm)),
                _block_tri(_tri_rows(tm), lower=False), d_proj)
    return _call(
        body, name="hgrn_bwd", grid=(HEADS, nt),
        in_specs=[rev(0), rev(0), rev(8), rev(16), rev(24), rev(32),
                  pl.BlockSpec((None, cpb, HEAD_DIM, HEAD_DIM), lambda h, j: (h, nt - 1 - j, 0, 0)),
                  vec(2), vec(1), const, const, ANY],
        out_specs=[pl.BlockSpec((tm, 4 * HEAD_DIM), lambda h, j: (nt - 1 - j, DP_HEAD_BLOCK + h)), vec(8)],
        out_shape=[jax.ShapeDtypeStruct(d_proj.shape, BF16), jax.ShapeDtypeStruct((8, HGRN_DIM), F32)],
        scratch_shapes=[pltpu.VMEM((HEAD_DIM, HEAD_DIM), F32)] + [tile_bf] * 6 + [square(F32), square(BF16)]
        + [pltpu.VMEM((cpb, CHUNK, CHUNK), BF16)] * 2 + [rows8, rows8] + [tile_f32] * 14,
        semantics=("parallel", "arbitrary"), operands=operands, hosted=hosted, aliases={len(operands) - 1: 0})


def _mix_ln1(proj, y_conv, y_hgrn, w_out, x, g, b):
    t = x.shape[0]
    tm = _pick(t, 512)
    d = D_MODEL

    def lhs(ins, outs):
        for rows in _row_blocks(tm):
            outs[0][rows, :] = (_sigmoid(ins[0][rows, :]) * ins[3][rows, :]
                                + _sigmoid(ins[2][rows, :]) * ins[4][rows, :]).astype(BF16)
        return outs[0][...]

    def epilogue(acc, ins, outs, scr):
        for rows in _row_blocks(tm):
            r = ALPHA * ins[5][rows, :] + acc[rows, :]
            outs[1][rows, :] = r
            xhat, _ = _ln(r)
            outs[2][rows, :] = (xhat * ins[6][...] + ins[7][...]).astype(BF16)

    tile = pl.BlockSpec((tm, d), lambda i, j, k: (i, 0))
    vec = pl.BlockSpec((1, d), lambda i, j, k: (0, 0))
    return _mm_fused(
        "mix_ln1", (t // tm, 1, 1), (proj, w_out, proj, y_conv, y_hgrn, x, g, b),
        [pl.BlockSpec((tm, d), lambda i, j, k: (i, 5)), pl.BlockSpec((d, d), lambda i, j, k: (0, 0)),
         pl.BlockSpec((tm, d), lambda i, j, k: (i, 6)), tile, tile, tile, vec, vec],
        [jax.ShapeDtypeStruct((t, d), BF16), jax.ShapeDtypeStruct((t, d), F32), jax.ShapeDtypeStruct((t, d), BF16)],
        [tile, tile, tile], NN, (tm, d), epilogue, lhs=lhs)


def _d_mixed_merge_bwd(d_r1b, w_out, proj, y_conv, y_hgrn):
    t = proj.shape[0]
    tm = _pick(t, 512)
    d = D_MODEL

    def epilogue(d_mixed, ins, outs, scr):
        dy_ref, dmz_ref = outs
        for rows in _row_blocks(tm):
            dm = d_mixed[rows, :]
            for br in range(2):
                sg = _sigmoid(ins[2 + br][rows, :])
                dy_ref[br, rows, :] = (sg * dm).astype(BF16)
                dmz_ref[rows, br * d:(br + 1) * d] = (dm * ins[4 + br][rows, :] * sg * (1.0 - sg)).astype(BF16)

    tile = pl.BlockSpec((tm, d), lambda i, j, k: (i, 0))
    return _mm_fused(
        "d_mixed_merge_bwd", (t // tm, 1, 1), (d_r1b, w_out, proj, proj, y_conv, y_hgrn),
        [tile, pl.BlockSpec((d, d), lambda i, j, k: (0, 0)), pl.BlockSpec((tm, d), lambda i, j, k: (i, 5)),
         pl.BlockSpec((tm, d), lambda i, j, k: (i, 6)), tile, tile],
        [jax.ShapeDtypeStruct((2, t, d), BF16), jax.ShapeDtypeStruct((t, IN_COLS), BF16)],
        [pl.BlockSpec((2, tm, d), lambda i, j, k: (0, i, 0)),
         pl.BlockSpec((tm, 2 * d), lambda i, j, k: (i, DP_MERGE_BLOCK))],
        NT, (tm, d), epilogue)


def _ffn_out_ln2(act, w_ffn_out, r1, target, g1, b1, g2, b2):
    t = r1.shape[0]
    tm = _pick(t, 1024)
    nt = t // tm
    d = D_MODEL

    def epilogue(y_ffn, ins, outs, scr):
        r1_ref, tg_ref, g1_ref, b1_ref, g2_ref, b2_ref = ins[2:]
        dr_ref, drb_ref, sums_ref = outs
        (sq_ref,) = scr
        i = pl.program_id(0)

        @pl.when(i == 0)
        def _():
            sums_ref[...] = jnp.zeros_like(sums_ref)
            sq_ref[...] = jnp.zeros_like(sq_ref)

        for rows in _row_blocks(tm):
            xh1, _ = _ln(r1_ref[rows, :])
            x1 = xh1 * g1_ref[...] + b1_ref[...]
            xh2, rstd2 = _ln(ALPHA * x1 + y_ffn[rows, :])
            diff = xh2 * g2_ref[...] + b2_ref[...] - tg_ref[rows, :]
            dy = diff * (1.0 / D_MODEL)
            dr = _ln_bwd(dy, xh2, rstd2, g2_ref[...])
            dr_ref[rows, :] = dr
            drb_ref[rows, :] = dr.astype(BF16)
            sums_ref[0:1, :] += _colsum(dy * xh2)
            sums_ref[1:2, :] += _colsum(dy)
            sq_ref[...] += _colsum(diff * diff)

        @pl.when(i == nt - 1)
        def _():
            total = jnp.sum(sq_ref[...], axis=-1, keepdims=True) * (0.5 / D_MODEL)
            sums_ref[2:3, :] = jnp.broadcast_to(total, (1, D_MODEL))

    tile = pl.BlockSpec((tm, d), lambda i, j, k: (i, 0))
    vec = pl.BlockSpec((1, d), lambda i, j, k: (0, 0))
    return _mm_fused(
        "ffn_out_ln2", (nt, 1, 4), (act, w_ffn_out, r1, target, g1, b1, g2, b2),
        [pl.BlockSpec((None, tm, FF_SHARD), lambda i, j, k: (k, i, 0)),
         pl.BlockSpec((None, FF_SHARD, d), lambda i, j, k: (k, 0, 0)), tile, tile, vec, vec, vec, vec],
        [jax.ShapeDtypeStruct((t, d), F32), jax.ShapeDtypeStruct((t, d), BF16), jax.ShapeDtypeStruct((8, d), F32)],
        [tile, tile, pl.BlockSpec((8, d), lambda i, j, k: (0, 0))], NN, (tm, d), epilogue,
        scratch=[pltpu.VMEM((1, d), F32)])


def _d_x1_ln1_bwd(d_z, w_ffn_in, d_r2, r1, g1):
    t = r1.shape[0]
    tm = _pick(t, 1024)
    d = D_MODEL

    def epilogue(dx_ffn, ins, outs, scr):
        dr2_ref, r1_ref, g_ref = ins[2:]
        dr1_ref, dr1b_ref, sums_ref = outs
        i = pl.program_id(0)

        @pl.when(i == 0)
        def _():
            sums_ref[...] = jnp.zeros_like(sums_ref)

        for rows in _row_blocks(tm):
            xhat, rstd = _ln(r1_ref[rows, :])
            dx1 = ALPHA * dr2_ref[rows, :] + dx_ffn[rows, :]
            dr1 = _ln_bwd(dx1, xhat, rstd, g_ref[...])
            dr1_ref[rows, :] = dr1
            dr1b_ref[rows, :] = dr1.astype(BF16)
            sums_ref[0:1, :] += _colsum(dx1 * xhat)
            sums_ref[1:2, :] += _colsum(dx1)

    tile = pl.BlockSpec((tm, d), lambda i, j, k: (i, 0))
    return _mm_fused(
        "d_x1_ln1_bwd", (t // tm, 1, N_DEV), (d_z, w_ffn_in, d_r2, r1, g1),
        [pl.BlockSpec((None, tm, FF_SHARD), lambda i, j, k: (k, i, 0)),
         pl.BlockSpec((None, d, FF_SHARD), lambda i, j, k: (k, 0, 0)), tile, tile,
         pl.BlockSpec((1, d), lambda i, j, k: (0, 0))],
        [jax.ShapeDtypeStruct((t, d), F32), jax.ShapeDtypeStruct((t, d), BF16), jax.ShapeDtypeStruct((8, d), F32)],
        [tile, tile, pl.BlockSpec((8, d), lambda i, j, k: (0, 0))], NT, (tm, d), epilogue)


def _cast_bf16(x):
    t = x.shape[0]
    tm = _pick(t, 512)

    def body(x_ref, o_ref):
        o_ref[...] = x_ref[...].astype(BF16)

    tile = pl.BlockSpec((tm, D_MODEL), lambda i: (i, 0))
    return pl.pallas_call(
        body, name="cast_x", grid=(t // tm,), in_specs=[tile], out_specs=tile,
        out_shape=jax.ShapeDtypeStruct((t, D_MODEL), BF16), compiler_params=_params(("parallel",)),
    )(x)


def _relayout(name, a, in_block, in_map, out_block, out_map, out_shape):
    def body(a_ref, o_ref):
        o_ref[...] = a_ref[...].astype(o_ref.dtype)

    return pl.pallas_call(
        body, name=name, grid=(N_DEV,), in_specs=[pl.BlockSpec(in_block, in_map)],
        out_specs=pl.BlockSpec(out_block, out_map), out_shape=out_shape, compiler_params=_params(("parallel",)),
    )(a)


_GELU_C = math.sqrt(2.0 / math.pi)


_GELU_CUBIC = 0.044715


def _gelu_parts(u):
    u2 = u * u
    th = jnp.tanh(u * (_GELU_C + (_GELU_C * _GELU_CUBIC) * u2))
    hu = 0.5 * u
    return th, hu + hu * th, u2, hu


BF16_ROWS = 16


def _ffn_act_fwd(z, w_dw, b_dw):
    t = z.shape[2]
    tm = _pick(t, 512)
    nh = tm // FFN_HALO

    def body(z_ref, zh_ref, w_ref, b_ref, act_ref, gd_ref, us_ref):
        i = pl.program_id(1)
        us_ref[0:FFN_HALO, :] = jnp.where(i == 0, 0.0, zh_ref[...])
        us_ref[FFN_HALO:FFN_HALO + tm, :] = z_ref[0]
        for r in range(tm // ROW_BLOCK):
            base = r * ROW_BLOCK
            rows = slice(base, base + ROW_BLOCK)
            for lanes in _lane_blocks(FF_SHARD):
                uc = b_ref[:, lanes]
                for k in range(FFN_K):
                    off = base + FFN_HALO - (FFN_K - 1) + k
                    uc = uc + w_ref[k:k + 1, lanes] * us_ref[off:off + ROW_BLOCK, lanes]
                th, gelu, u2, hu = _gelu_parts(uc)
                dgelu = (0.5 + 0.5 * th) + (hu - hu * th * th) * (_GELU_C + (3.0 * _GELU_C * _GELU_CUBIC) * u2)
                act_ref[rows, lanes] = (gelu * z_ref[1, rows, lanes]).astype(BF16)
                gd_ref[0, rows, lanes] = gelu.astype(BF16)
                gd_ref[1, rows, lanes] = dgelu.astype(BF16)

    return pl.pallas_call(
        body, name="ffn_act_fwd", grid=(4, t // tm),
        in_specs=[pl.BlockSpec((2, None, tm, FF_SHARD), lambda j, i: (0, j, i, 0)),
                  pl.BlockSpec((None, None, FFN_HALO, FF_SHARD), lambda j, i: (0, j, jnp.maximum(i * nh - 1, 0), 0)),
                  pl.BlockSpec((None, FFN_K, FF_SHARD), lambda j, i: (j, 0, 0)),
                  pl.BlockSpec((None, 1, FF_SHARD), lambda j, i: (j, 0, 0))],
        out_specs=[pl.BlockSpec((None, tm, FF_SHARD), lambda j, i: (j, i, 0)),
                   pl.BlockSpec((2, None, tm, FF_SHARD), lambda j, i: (0, j, i, 0))],
        out_shape=[jax.ShapeDtypeStruct((4, t, FF_SHARD), BF16), jax.ShapeDtypeStruct((2, 4, t, FF_SHARD), BF16)],
        scratch_shapes=[pltpu.VMEM((FFN_HALO + tm, FF_SHARD), F32)],
        compiler_params=_params(("parallel", "arbitrary")),
    )(z, z, w_dw, b_dw)


def _ffn_act_bwd(d_act, z, gd, w_dw):
    t = z.shape[2]
    tm = _pick(t, 512)
    nt = t // tm
    nh = tm // FFN_HALO
    last_h = t // FFN_HALO - 1
    pad = FFN_HALO - (FFN_K - 1)

    def fold(x):
        return functools.reduce(jnp.add, [x[r:r + SUBLANES, :] for r in range(0, x.shape[0], SUBLANES)])

    def body(da_ref, dah_ref, z_ref, zp_ref, gn_ref, gd_ref, gdn_ref, w_ref, dz_ref, sums_ref, us_ref, ds_ref,
             part_ref):
        i = pl.program_id(1)

        @pl.when(i == 0)
        def _():
            part_ref[...] = jnp.zeros_like(part_ref)

        us_ref[0:FFN_HALO, :] = jnp.where(i == 0, 0.0, zp_ref[...])
        us_ref[FFN_HALO:FFN_HALO + tm, :] = z_ref[0]
        for r in range(tm // ROW_BLOCK):
            base = r * ROW_BLOCK
            rows = slice(base, base + ROW_BLOCK)
            for lanes in _lane_blocks(FF_SHARD):
                da = da_ref[rows, lanes]
                dz_ref[1, rows, lanes] = (da * gd_ref[0, rows, lanes].astype(F32)).astype(BF16)
                duc = da * z_ref[1, rows, lanes] * gd_ref[1, rows, lanes].astype(F32)
                ds_ref[rows, lanes] = duc
                for k in range(FFN_K):
                    part_ref[k, :, lanes] += fold(duc * us_ref[base + pad + k:base + pad + k + ROW_BLOCK, lanes])
                part_ref[FFN_K, :, lanes] += fold(duc)
        duc_next = dah_ref[...] * gn_ref[...] * gdn_ref[0:FFN_HALO, :].astype(F32)
        ds_ref[tm:tm + FFN_HALO, :] = jnp.where(i == nt - 1, 0.0, duc_next)
        for r in range(tm // ROW_BLOCK):
            base = r * ROW_BLOCK
            for lanes in _lane_blocks(FF_SHARD):
                du = None
                for k in range(FFN_K):
                    off = base + FFN_K - 1 - k
                    term = w_ref[k:k + 1, lanes] * ds_ref[off:off + ROW_BLOCK, lanes]
                    du = term if du is None else du + term
                dz_ref[0, base:base + ROW_BLOCK, lanes] = du.astype(BF16)

        @pl.when(i == nt - 1)
        def _():
            sums_ref[...] = jnp.zeros_like(sums_ref)
            for k in range(FFN_K + 1):
                sums_ref[k:k + 1, :] = _colsum(part_ref[k])

    nxt = lambda i: jnp.minimum((i + 1) * nh, last_h)
    nxt_bf = lambda i: jnp.minimum((i + 1) * (tm // BF16_ROWS), t // BF16_ROWS - 1)
    return pl.pallas_call(
        body, name="ffn_act_bwd", grid=(4, nt),
        in_specs=[pl.BlockSpec((None, tm, FF_SHARD), lambda j, i: (j, i, 0)),
                  pl.BlockSpec((None, FFN_HALO, FF_SHARD), lambda j, i: (j, nxt(i), 0)),
                  pl.BlockSpec((2, None, tm, FF_SHARD), lambda j, i: (0, j, i, 0)),
                  pl.BlockSpec((None, None, FFN_HALO, FF_SHARD), lambda j, i: (0, j, jnp.maximum(i * nh - 1, 0), 0)),
                  pl.BlockSpec((None, None, FFN_HALO, FF_SHARD), lambda j, i: (1, j, nxt(i), 0)),
                  pl.BlockSpec((2, None, tm, FF_SHARD), lambda j, i: (0, j, i, 0)),
                  pl.BlockSpec((None, None, BF16_ROWS, FF_SHARD), lambda j, i: (1, j, nxt_bf(i), 0)),
                  pl.BlockSpec((None, FFN_K, FF_SHARD), lambda j, i: (j, 0, 0))],
        out_specs=[pl.BlockSpec((2, None, tm, FF_SHARD), lambda j, i: (0, j, i, 0)),
                   pl.BlockSpec((None, 8, FF_SHARD), lambda j, i: (j, 0, 0))],
        out_shape=[jax.ShapeDtypeStruct((2, 4, t, FF_SHARD), BF16), jax.ShapeDtypeStruct((4, 8, FF_SHARD), F32)],
        scratch_shapes=[pltpu.VMEM((FFN_HALO + tm, FF_SHARD), F32), pltpu.VMEM((tm + FFN_HALO, FF_SHARD), F32),
                        pltpu.VMEM((FFN_K + 1, SUBLANES, FF_SHARD), F32)],
        compiler_params=_params(("parallel", "arbitrary")),
    )(d_act, d_act, z, z, z, gd, gd, w_dw)


_HGRN_COLS = 4 * HGRN_DIM


def _to_backward_order(w):
    heads = w[:, 2 * CONV_DIM:2 * CONV_DIM + _HGRN_COLS].reshape(-1, 4, HEADS, HEAD_DIM)
    heads = jnp.swapaxes(heads, 1, 2).reshape(-1, _HGRN_COLS)
    return jnp.concatenate([w[:, 2 * CONV_DIM + _HGRN_COLS:], w[:, :2 * CONV_DIM], heads], axis=1)


def _from_backward_order(w):
    heads = w[:, 2 * D_MODEL + 2 * CONV_DIM:].reshape(-1, HEADS, 4, HEAD_DIM)
    heads = jnp.swapaxes(heads, 1, 2).reshape(-1, _HGRN_COLS)
    return jnp.concatenate([w[:, 2 * D_MODEL:2 * D_MODEL + 2 * CONV_DIM], heads, w[:, :2 * D_MODEL]], axis=1)
def _local_step(x, target, weights, small, scatter=None, order=None):
    t = x.shape[0]
    tm = _pick(t, 2048)
    tk = _pick(t, 2048)
    nm = t // tm
    nk = t // tk
    d = D_MODEL

    xb = _cast_bf16(x)
    if isinstance(weights, _Hosted):
        proj, weights = _proj_gather(xb, weights, order)
    else:
        proj = _mm("proj", xb, weights[0], (t, IN_COLS), F32, (nm, N_DEV, 1),
                   pl.BlockSpec((tm, d), lambda i, j, k: (i, 0)),
                   pl.BlockSpec((None, d, IN_SHARD), lambda i, j, k: (j, 0, 0)),
                   pl.BlockSpec((tm, IN_SHARD), lambda i, j, k: (i, j)), NN, (tm, IN_SHARD))
    w_in, w_conv_out8, w_hgrn_out8, w_out8, w_ffn_in, w_ffn_out8, conv_dw8, ffn_dw8 = weights
    w_conv_out = _relayout("w_conv_out_natural", w_conv_out8, (None, CONV_DIM, 128), lambda j: (j, 0, 0),
                           (CONV_DIM, 128), lambda j: (0, j), jax.ShapeDtypeStruct((CONV_DIM, d), BF16))
    w_hgrn_out = w_hgrn_out8.reshape(d, d)
    w_out = w_out8.reshape(d, d)
    w_ffn_out = w_ffn_out8.reshape(4, FF_SHARD, d)
    conv_dw = jnp.transpose(conv_dw8[:, :CONV_K, :CONV_DIM // N_DEV], (1, 0, 2)).reshape(CONV_K, CONV_DIM)
    ffn_dw = jnp.transpose(ffn_dw8[:, :FFN_K, :D_FF // N_DEV], (1, 0, 2)).reshape(FFN_K, 4, FF_SHARD)
    small = dict(small, w_conv_dw=conv_dw, w_ffn_dw=jnp.transpose(ffn_dw, (1, 0, 2)),
                 b_ffn_dw=small["b_ffn_dw"].reshape(4, 1, FF_SHARD))

    c_act, conv_pre = _conv_fwd(proj, small["w_conv_dw"], small["b_conv_dw"], small["conv_ln_g"], small["conv_ln_b"])
    y_conv = _mm("y_conv", c_act, w_conv_out, (t, d), F32, (nm, 1, 1),
                 pl.BlockSpec((tm, CONV_DIM), lambda i, j, k: (i, 0)),
                 pl.BlockSpec((CONV_DIM, d), lambda i, j, k: (0, 0)),
                 pl.BlockSpec((tm, d), lambda i, j, k: (i, 0)), NN, (tm, d))
    o, og, states = _hgrn_fwd(proj, small["hgrn_lb_logits"], small["hgrn_norm_g"])
    sq_w = pl.BlockSpec((d, d), lambda i, j, k: (0, 0))
    row_tile = pl.BlockSpec((tm, d), lambda i, j, k: (i, 0))
    y_hgrn = _mm("y_hgrn", og, w_hgrn_out, (t, d), F32, (nm, 1, 1), row_tile, sq_w, row_tile, NN, (tm, d))
    mixed, r1, x1b = _mix_ln1(proj, y_conv, y_hgrn, w_out, x, small["ln1_g"], small["ln1_b"])
    z = _mm("ffn_in", x1b, w_ffn_in, (N_DEV, t, FF_SHARD), F32, (nm, N_DEV, 1), row_tile,
            pl.BlockSpec((None, d, FF_SHARD), lambda i, j, k: (j, 0, 0)),
            pl.BlockSpec((None, tm, FF_SHARD), lambda i, j, k: (j, i, 0)), NN, (tm, FF_SHARD))
    z = z.reshape(2, 4, t, FF_SHARD)
    act, gelu_and_slope = _ffn_act_fwd(z, small["w_ffn_dw"], small["b_ffn_dw"])

    d_r2, d_r2b, sums_ln2 = _ffn_out_ln2(act, w_ffn_out, r1, target, small["ln1_g"], small["ln1_b"],
                                         small["ln2_g"], small["ln2_b"])
    d_act = _mm("d_act", d_r2b, w_ffn_out, (4, t, FF_SHARD), F32, (nm, 4, 1), row_tile,
                pl.BlockSpec((None, FF_SHARD, d), lambda i, j, k: (j, 0, 0)),
                pl.BlockSpec((None, tm, FF_SHARD), lambda i, j, k: (j, i, 0)), NT, (tm, FF_SHARD))
    g_w_ffn_out = _mm("g_w_ffn_out", act, d_r2b, (4, FF_SHARD, d), BF16, (4, 1, nk),
                      pl.BlockSpec((None, tk, FF_SHARD), lambda i, j, k: (i, k, 0)),
                      pl.BlockSpec((tk, d), lambda i, j, k: (k, 0)),
                      pl.BlockSpec((None, FF_SHARD, d), lambda i, j, k: (i, 0, 0)), TN, (FF_SHARD, d))
    d_z, sums_ffn = _ffn_act_bwd(d_act, z, gelu_and_slope, small["w_ffn_dw"])
    d_z8 = d_z.reshape(N_DEV, t, FF_SHARD)
    d_r1, d_r1b, sums_ln1 = _d_x1_ln1_bwd(d_z8, w_ffn_in, d_r2, r1, small["ln1_g"])
    g_w_ffn_in = _mm("g_w_ffn_in", x1b, d_z8, (N_DEV, d, FF_SHARD), BF16, (N_DEV, 1, nk),
                     pl.BlockSpec((tk, d), lambda i, j, k: (k, 0)),
                     pl.BlockSpec((None, tk, FF_SHARD), lambda i, j, k: (i, k, 0)),
                     pl.BlockSpec((None, d, FF_SHARD), lambda i, j, k: (i, 0, 0)), TN, (d, FF_SHARD))
    k_tile = pl.BlockSpec((tk, d), lambda i, j, k: (k, 0))
    g_w_out = _mm("g_w_out", mixed, d_r1b, (d, d), BF16, (1, 1, nk), k_tile, k_tile, sq_w, TN, (d, d))
    d_y, d_proj = _d_mixed_merge_bwd(d_r1b, w_out, proj, y_conv, y_hgrn)
    d_pre, sums_conv = _d_c_norm_bwd(d_y, w_conv_out, conv_pre, small["conv_ln_g"], small["conv_ln_b"])
    g_w_conv_out = _mm("g_w_conv_out", c_act, d_y, (CONV_DIM, d), BF16, (1, 1, nk),
                       pl.BlockSpec((tk, CONV_DIM), lambda i, j, k: (k, 0)),
                       pl.BlockSpec((None, tk, d), lambda i, j, k: (0, k, 0)),
                       pl.BlockSpec((CONV_DIM, d), lambda i, j, k: (0, 0)), TN, (CONV_DIM, d))
    g_w_conv_out = _relayout("g_w_conv_out_shards", g_w_conv_out, (CONV_DIM, 128), lambda j: (0, j),
                             (None, CONV_DIM, 128), lambda j: (j, 0, 0),
                             jax.ShapeDtypeStruct((N_DEV, CONV_DIM, 128), BF16))
    d_og = _mm("d_og", d_y, w_hgrn_out, (t, d), F32, (nm, 1, 1),
               pl.BlockSpec((None, tm, d), lambda i, j, k: (1, i, 0)), sq_w, row_tile, NT, (tm, d))
    g_w_hgrn_out = _mm("g_w_hgrn_out", og, d_y, (d, d), BF16, (1, 1, nk), k_tile,
                       pl.BlockSpec((None, tk, d), lambda i, j, k: (1, k, 0)), sq_w, TN, (d, d))
    d_proj, g_w_conv_dw = _conv_bwd_dw(d_pre, proj, small["w_conv_dw"], d_proj)

    early = [g_w_conv_out, g_w_hgrn_out.reshape(N_DEV, d // N_DEV, d), g_w_out.reshape(N_DEV, d // N_DEV, d),
             g_w_ffn_in, g_w_ffn_out.reshape(N_DEV, D_FF // N_DEV, d)]
    d_proj, sums_hgrn, *early_recv = _hgrn_bwd(d_og, o, proj, states, small["hgrn_lb_logits"], small["hgrn_norm_g"],
                                               d_proj, hosted=None if scatter is None else scatter(early))
    wide = IN_COLS // 4
    w_in_bwd = _to_backward_order(jnp.transpose(w_in, (1, 0, 2)).reshape(d, IN_COLS))
    g_w_in = _mm("g_w_in", xb, d_proj, (d, IN_COLS), BF16, (4, 1, nk), k_tile,
                 pl.BlockSpec((tk, wide), lambda i, j, k: (k, i)),
                 pl.BlockSpec((d, wide), lambda i, j, k: (0, i)), TN, (d, wide))
    g_w_in = jnp.transpose(_from_backward_order(g_w_in).reshape(d, N_DEV, IN_SHARD), (1, 0, 2))
    def add_residual(acc, ins, outs, scr):
        for rows in _row_blocks(ta):
            outs[0][rows, :] = ALPHA * ins[2][rows, :] + acc[rows, :]

    ta = _pick(t, 1024)
    acc_tile = pl.BlockSpec((ta, d), lambda i, j, k: (i, 0))
    grad_x, *late_recv = _mm_fused(
        "grad_x", (t // ta, 1, 4), (d_proj, w_in_bwd, d_r1),
        [pl.BlockSpec((ta, wide), lambda i, j, k: (i, k)), pl.BlockSpec((d, wide), lambda i, j, k: (0, k)), acc_tile],
        [jax.ShapeDtypeStruct((t, d), F32)], [acc_tile], NT, (ta, d), add_residual,
        hosted=None if scatter is None else scatter([g_w_in]))

    large_grads = [g_w_in] + early
    if scatter is not None:
        large_grads = list(zip(large_grads, late_recv + early_recv))
    return grad_x, large_grads, (sums_ln2, sums_conv, sums_hgrn, sums_ln1, sums_ffn, g_w_conv_dw)


def _small_views(sums):
    sums_ln2, sums_conv, sums_hgrn, sums_ln1, sums_ffn, g_w_conv_dw = sums
    d_l0 = sums_hgrn[1:2]
    return {
        "loss": sums_ln2[2:3, 0:128],
        "b_conv_dw": sums_conv[2:3], "conv_ln_g": sums_conv[0:1], "conv_ln_b": sums_conv[1:2],
        "hgrn_lb_logits": jnp.concatenate([d_l0, -d_l0], axis=1),
        "hgrn_norm_g": sums_hgrn[0:1],
        "ln1_g": sums_ln1[0:1], "ln1_b": sums_ln1[1:2],
        "b_ffn_dw": sums_ffn[:, FFN_K, :].reshape(1, D_FF),
        "ln2_g": sums_ln2[0:1], "ln2_b": sums_ln2[1:2],
        "w_conv_dw": g_w_conv_dw[0:CONV_K].reshape(1, CONV_K * CONV_DIM),
        "w_ffn_dw": jnp.transpose(sums_ffn[:, 0:FFN_K, :], (1, 0, 2)).reshape(1, FFN_K * D_FF),
    }


def _coords():
    return lax.axis_index("x"), lax.axis_index("y"), lax.axis_index("c")


def _gather(shards, staged=False):
    n = len(shards)
    later = range(1 if staged else 0, n)

    def parts(ins, outs, sems):
        send_sems, recv_sems, local_sems = sems
        x, y, c = _coords()
        me = 4 * x + 2 * y + c
        sibling = (x, y, 1 - c)
        chips = [(1 - x, y), (x, 1 - y), (1 - x, 1 - y)]

        def copy(a, k, block, to, src=None):
            return pltpu.make_async_remote_copy(
                src_ref=outs[a].at[block] if src is None else src, dst_ref=outs[a].at[block],
                send_sem=send_sems.at[a, k], recv_sem=recv_sems.at[a, k], device_id=to, device_id_type=MESH)

        local = [pltpu.make_async_copy(ins[a], outs[a].at[me], local_sems.at[a]) for a in range(n)]
        first = []
        for a in range(n):
            first.append(copy(a, 0, me, sibling, src=ins[a]))
            for j, chip in enumerate(chips):
                first.append(copy(a, 1 + j, me, (*chip, c), src=ins[a]))
        return x, y, c, sibling, chips, copy, local, first

    def start(ins, outs, sems):
        *_, local, first = parts(ins, outs, sems)
        for cp in local + first:
            cp.start()

    def arrive(ins, outs, sems, s):
        x, y, c, sibling, chips, copy, _, _ = parts(ins, outs, sems)
        if s == 1:
            block = 4 * x + 2 * y + 1 - c
            copy(0, 0, block, sibling).wait_recv()
        elif s <= 4:
            px, py = chips[s - 2]
            block = 4 * px + 2 * py + c
            copy(0, s - 1, block, sibling).wait_recv()
            copy(0, s + 2, block, sibling).start()
        else:
            px, py = chips[s - 5]
            block = 4 * px + 2 * py + 1 - c
            copy(0, s - 1, block, sibling).wait_recv()
        return block

    def finish(ins, outs, sems):
        x, y, c, sibling, chips, copy, local, first = parts(ins, outs, sems)
        passed = [copy(0, 4 + j, 4 * px + 2 * py + c, sibling) for j, (px, py) in enumerate(chips)] if staged else []
        for j, (px, py) in enumerate(chips):
            for a in later:
                copy(a, 1 + j, 4 * px + 2 * py + c, sibling).wait_recv()
                cp = copy(a, 4 + j, 4 * px + 2 * py + c, sibling)
                cp.start()
                passed.append(cp)
        for a in later:
            copy(a, 0, 4 * x + 2 * y + 1 - c, sibling).wait_recv()
            for j, (px, py) in enumerate(chips):
                copy(a, 4 + j, 4 * px + 2 * py + 1 - c, sibling).wait_recv()
        for cp in first + passed:
            cp.wait_send()
        for cp in local:
            cp.wait()

    hosted = _Hosted(shards, [jax.ShapeDtypeStruct((N_DEV,) + s.shape, s.dtype) for s in shards],
                     [pltpu.SemaphoreType.DMA((n, 7)), pltpu.SemaphoreType.DMA((n, 7)), pltpu.SemaphoreType.DMA((n,))],
                     start, finish)
    hosted.arrive = arrive
    return hosted


def _proj_gather(xb, gather, order):
    t, d = xb.shape
    tm = _pick(t, 2048)
    nm = t // tm
    n_in, n_out = len(gather.inputs), len(gather.out_shapes)

    def body(order_ref, x_ref, *refs):
        ins, refs = refs[:n_in], refs[n_in:]
        o_ref, outs, refs = refs[0], refs[1:1 + n_out], refs[1 + n_out:]
        w_buf, w_sem, sems = refs[0], refs[1], refs[2:]
        s, i = pl.program_id(0), pl.program_id(1)

        @pl.when((s == 0) & (i == 0))
        def _():
            gather.start(ins, outs, sems)

        def staging(step, src):
            return pltpu.make_async_copy(src, w_buf.at[step % 2], w_sem.at[step % 2])

        @pl.when((s == 0) & (i == 0))
        def _():
            staging(0, ins[0]).start()

        for step in range(1, N_DEV):
            @pl.when((s == step - 1) & (i == nm - 1))
            def _(step=step):
                staging(step, outs[0].at[gather.arrive(ins, outs, sems, step)]).start()

        for step in range(N_DEV):
            @pl.when((s == step) & (i == 0))
            def _(step=step):
                staging(step, ins[0]).wait()

        o_ref[...] = _dot(x_ref[...], w_buf[s % 2], NN)

        @pl.when((s == N_DEV - 1) & (i == nm - 1))
        def _():
            gather.finish(ins, outs, sems)

    outs = pl.pallas_call(
        body, name="proj_gather",
        grid_spec=pltpu.PrefetchScalarGridSpec(
            num_scalar_prefetch=1, grid=(N_DEV, nm),
            in_specs=[pl.BlockSpec((tm, d), lambda s, i, order_ref: (i, 0))] + [ANY] * n_in,
            out_specs=[pl.BlockSpec((tm, IN_SHARD), lambda s, i, order_ref: (i, order_ref[s]))] + [ANY] * n_out,
            scratch_shapes=[pltpu.VMEM((2, d, IN_SHARD), BF16), pltpu.SemaphoreType.DMA((2,))] + gather.sem_shapes),
        out_shape=[jax.ShapeDtypeStruct((t, IN_COLS), F32)] + gather.out_shapes,
        compiler_params=pltpu.CompilerParams(dimension_semantics=("arbitrary", "arbitrary"),
                                             vmem_limit_bytes=VMEM_LIMIT, has_side_effects=True),
    )(order, xb, *gather.inputs)
    return outs[0], list(outs[1:])


def _scatter(grads):
    n = len(grads)

    def copies(ins, outs, sems):
        send_sems, recv_sems = sems
        x, y, c = _coords()
        out = []
        for a in range(n):
            for k in range(1, N_DEV):
                px, py, pc = x ^ (k >> 2), y ^ ((k >> 1) & 1), c ^ (k & 1)
                out.append(pltpu.make_async_remote_copy(
                    src_ref=ins[a].at[4 * px + 2 * py + pc], dst_ref=outs[a].at[k - 1],
                    send_sem=send_sems.at[a, k - 1], recv_sem=recv_sems.at[a, k - 1],
                    device_id=(px, py, pc), device_id_type=MESH))
        return out

    def start(ins, outs, sems):
        for cp in copies(ins, outs, sems):
            cp.start()

    def finish(ins, outs, sems):
        for cp in copies(ins, outs, sems):
            cp.wait()

    return _Hosted(grads, [jax.ShapeDtypeStruct((N_DEV - 1,) + g.shape[1:], g.dtype) for g in grads],
                   [pltpu.SemaphoreType.DMA((n, N_DEV - 1)), pltpu.SemaphoreType.DMA((n, N_DEV - 1))], start, finish)


def _row_tile(rows):
    return 256 if rows % 256 == 0 else rows


def _adam_math(w, g, m, v):
    m_new = ADAM_B1 * m + (1.0 - ADAM_B1) * g
    v_new = ADAM_B2 * v + (1.0 - ADAM_B2) * (g * g)
    m_hat = m_new / (1.0 - ADAM_B1 ** ADAM_STEP)
    v_hat = v_new / (1.0 - ADAM_B2 ** ADAM_STEP)
    delta = -ADAM_LR * (m_hat / (jnp.sqrt(v_hat) + ADAM_EPS) + ADAM_WD * w)
    return delta, m_new, v_new


def _adam_large(name, own, recv, me, w, m, v):
    rows, cols = w.shape
    tr = _row_tile(rows)

    def body(me_ref, p_ref, r_ref, w_ref, m_ref, v_ref, g_out, d_out, m_out, v_out):
        g = p_ref[...].astype(F32)
        for k in range(N_DEV - 1):
            g = g + r_ref[k].astype(F32)
        delta, m_new, v_new = _adam_math(w_ref[...], g, m_ref[...], v_ref[...])
        g_out[...] = g
        d_out[...] = delta
        m_out[...] = m_new
        v_out[...] = v_new

    tile = pl.BlockSpec((tr, cols), lambda r, me_ref: (r, 0))
    sds = jax.ShapeDtypeStruct((rows, cols), F32)
    return pl.pallas_call(
        body, name=name,
        grid_spec=pltpu.PrefetchScalarGridSpec(
            num_scalar_prefetch=1, grid=(rows // tr,),
            in_specs=[pl.BlockSpec((None, tr, cols), lambda r, me_ref: (me_ref[0], r, 0)),
                      pl.BlockSpec((N_DEV - 1, tr, cols), lambda r, me_ref: (0, r, 0)), tile, tile, tile],
            out_specs=[tile, tile, tile, tile]),
        out_shape=[sds, sds, sds, sds],
        compiler_params=_params(("parallel",)),
    )(me, own, recv, w, m, v)


def _small_allreduce(arrays):
    n = len(arrays)

    def body(*refs):
        ins, outs, gats = refs[:n], refs[n:2 * n], refs[2 * n:3 * n]
        send_sems, recv_sems = refs[3 * n:]
        x, y, c = _coords()
        me = 4 * x + 2 * y + c
        peers = [(x ^ (k >> 2), y ^ ((k >> 1) & 1), c ^ (k & 1)) for k in range(1, N_DEV)]

        def copy(a, k, slot):
            return pltpu.make_async_remote_copy(
                src_ref=ins[a], dst_ref=gats[a].at[slot], send_sem=send_sems.at[a, k], recv_sem=recv_sems.at[a, k],
                device_id=peers[k], device_id_type=MESH)

        sends = [copy(a, k, me) for a in range(n) for k in range(N_DEV - 1)]
        for a in range(n):
            gats[a][me] = ins[a][...]
        for cp in sends:
            cp.start()
        for a in range(n):
            for k, (px, py, pc) in enumerate(peers):
                copy(a, k, 4 * px + 2 * py + pc).wait_recv()
        for cp in sends:
            cp.wait_send()
        for a in range(n):
            acc = gats[a][0]
            for dev in range(1, N_DEV):
                acc = acc + gats[a][dev]
            outs[a][...] = acc

    whole = pl.BlockSpec(memory_space=pltpu.VMEM)
    return pl.pallas_call(
        body, name="small_allreduce", in_specs=[whole] * n, out_specs=[whole] * n,
        out_shape=[jax.ShapeDtypeStruct(a.shape, F32) for a in arrays],
        scratch_shapes=[pltpu.VMEM((N_DEV,) + a.shape, F32) for a in arrays]
        + [pltpu.SemaphoreType.DMA((n, N_DEV - 1)), pltpu.SemaphoreType.DMA((n, N_DEV - 1))],
        compiler_params=pltpu.CompilerParams(has_side_effects=True, vmem_limit_bytes=VMEM_LIMIT),
    )(*arrays)


def _adam_replicated(sums, w, m, v):
    rows_of = {"conv_ln_g": (1, 0), "conv_ln_b": (1, 1), "b_conv_dw": (1, 2), "hgrn_norm_g": (2, 0),
               "ln1_g": (3, 0), "ln1_b": (3, 1), "ln2_g": (0, 0), "ln2_b": (0, 1)}
    names = list(rows_of) + ["hgrn_lb_logits"]
    n = len(names)

    def body(*refs):
        sum_refs, refs = refs[:4], refs[4:]
        w_refs, m_refs, v_refs, outs = refs[:n], refs[n:2 * n], refs[2 * n:3 * n], refs[3 * n:]
        for j, name in enumerate(names):
            if name == "hgrn_lb_logits":
                d_l0 = sum_refs[2][1:2, :]
                grads = [d_l0, -d_l0]
            else:
                a, row = rows_of[name]
                grads = [sum_refs[a][row:row + 1, :]]
            g_out, d_out, m_out, v_out = outs[4 * j:4 * j + 4]
            for r, g in enumerate(grads):
                rows = slice(r, r + 1)
                delta, m_new, v_new = _adam_math(w_refs[j][rows, :], g, m_refs[j][rows, :], v_refs[j][rows, :])
                g_out[rows, :] = g
                d_out[rows, :] = delta
                m_out[rows, :] = m_new
                v_out[rows, :] = v_new

    whole = pl.BlockSpec(memory_space=pltpu.VMEM)
    operands = list(sums) + [w[k] for k in names] + [m[k] for k in names] + [v[k] for k in names]
    outs = pl.pallas_call(
        body, name="adam_replicated", in_specs=[whole] * len(operands), out_specs=[whole] * (4 * n),
        out_shape=[jax.ShapeDtypeStruct(w[k].shape, F32) for k in names for _ in range(4)],
    )(*operands)
    return {name: tuple(outs[4 * j:4 * j + 4]) for j, name in enumerate(names)}


def _adam_small(w, g, m, v):
    def body(w_ref, g_ref, m_ref, v_ref, d_out, m_out, v_out):
        delta, m_new, v_new = _adam_math(w_ref[...], g_ref[...], m_ref[...], v_ref[...])
        d_out[...] = delta
        m_out[...] = m_new
        v_out[...] = v_new

    whole = pl.BlockSpec(memory_space=pltpu.VMEM)
    sds = jax.ShapeDtypeStruct(w.shape, F32)
    return pl.pallas_call(body, name="adam_small", in_specs=[whole] * 4, out_specs=[whole] * 3,
                          out_shape=[sds, sds, sds])(w, g, m, v)


_WEIGHTS = ["w_in", "w_conv_dw", "b_conv_dw", "conv_ln_g", "conv_ln_b", "w_conv_out", "hgrn_lb_logits", "hgrn_norm_g",
            "w_hgrn_out", "w_out", "ln1_g", "ln1_b", "w_ffn_in", "w_ffn_dw", "b_ffn_dw", "w_ffn_out", "ln2_g", "ln2_b"]
_LARGE = ["w_in", "w_conv_out", "w_hgrn_out", "w_out", "w_ffn_in", "w_ffn_out"]
_CONV_DW_SHARD = CONV_DIM // N_DEV
_FFN_DW_SHARD = D_FF // N_DEV


def kernel(x, w_in, w_conv_dw, b_conv_dw, conv_ln_g, conv_ln_b, w_conv_out, hgrn_lb_logits, hgrn_norm_g, w_hgrn_out, w_out, ln1_g, ln1_b, w_ffn_in, w_ffn_dw, b_ffn_dw, w_ffn_out, ln2_g, ln2_b, loss_target, m_w_in, m_w_conv_dw, m_b_conv_dw, m_conv_ln_g, m_conv_ln_b, m_w_conv_out, m_hgrn_lb_logits, m_hgrn_norm_g, m_w_hgrn_out, m_w_out, m_ln1_g, m_ln1_b, m_w_ffn_in, m_w_ffn_dw, m_b_ffn_dw, m_w_ffn_out, m_ln2_g, m_ln2_b, v_w_in, v_w_conv_dw, v_b_conv_dw, v_conv_ln_g, v_conv_ln_b, v_w_conv_out, v_hgrn_lb_logits, v_hgrn_norm_g, v_w_hgrn_out, v_w_out, v_ln1_g, v_ln1_b, v_w_ffn_in, v_w_ffn_dw, v_b_ffn_dw, v_w_ffn_out, v_ln2_g, v_ln2_b):
    w = dict(w_in=w_in, w_conv_dw=w_conv_dw, b_conv_dw=b_conv_dw, conv_ln_g=conv_ln_g, conv_ln_b=conv_ln_b,
             w_conv_out=w_conv_out, hgrn_lb_logits=hgrn_lb_logits, hgrn_norm_g=hgrn_norm_g, w_hgrn_out=w_hgrn_out,
             w_out=w_out, ln1_g=ln1_g, ln1_b=ln1_b, w_ffn_in=w_ffn_in, w_ffn_dw=w_ffn_dw, b_ffn_dw=b_ffn_dw,
             w_ffn_out=w_ffn_out, ln2_g=ln2_g, ln2_b=ln2_b)
    m = dict(w_in=m_w_in, w_conv_dw=m_w_conv_dw, b_conv_dw=m_b_conv_dw, conv_ln_g=m_conv_ln_g, conv_ln_b=m_conv_ln_b,
             w_conv_out=m_w_conv_out, hgrn_lb_logits=m_hgrn_lb_logits, hgrn_norm_g=m_hgrn_norm_g,
             w_hgrn_out=m_w_hgrn_out, w_out=m_w_out, ln1_g=m_ln1_g, ln1_b=m_ln1_b, w_ffn_in=m_w_ffn_in,
             w_ffn_dw=m_w_ffn_dw, b_ffn_dw=m_b_ffn_dw, w_ffn_out=m_w_ffn_out, ln2_g=m_ln2_g, ln2_b=m_ln2_b)
    v = dict(w_in=v_w_in, w_conv_dw=v_w_conv_dw, b_conv_dw=v_b_conv_dw, conv_ln_g=v_conv_ln_g, conv_ln_b=v_conv_ln_b,
             w_conv_out=v_w_conv_out, hgrn_lb_logits=v_hgrn_lb_logits, hgrn_norm_g=v_hgrn_norm_g,
             w_hgrn_out=v_w_hgrn_out, w_out=v_w_out, ln1_g=v_ln1_g, ln1_b=v_ln1_b, w_ffn_in=v_w_ffn_in,
             w_ffn_dw=v_w_ffn_dw, b_ffn_dw=v_b_ffn_dw, w_ffn_out=v_w_ffn_out, ln2_g=v_ln2_g, ln2_b=v_ln2_b)
    xi, yi, ci = lax.axis_index("x"), lax.axis_index("y"), lax.axis_index("c")
    me = 4 * xi + 2 * yi + ci
    me_op = jnp.reshape(me, (1,)).astype(jnp.int32)

    shards = [w[name][0].astype(BF16) for name in _LARGE]
    shards.append(jnp.pad(w_conv_dw[0], ((0, 1), (0, 128 - _CONV_DW_SHARD))))
    shards.append(jnp.pad(w_ffn_dw[0], ((0, 8 - FFN_K), (0, 384 - _FFN_DW_SHARD))))
    chips = [(1 - xi, yi), (xi, 1 - yi), (1 - xi, 1 - yi)]
    order = jnp.stack([me, me ^ 1] + [4 * px + 2 * py + ci for px, py in chips]
                      + [4 * px + 2 * py + 1 - ci for px, py in chips]).astype(jnp.int32)
    small = dict(b_conv_dw=b_conv_dw, conv_ln_g=conv_ln_g, conv_ln_b=conv_ln_b, hgrn_lb_logits=hgrn_lb_logits,
                 hgrn_norm_g=hgrn_norm_g, ln1_g=ln1_g, ln1_b=ln1_b, ln2_g=ln2_g, ln2_b=ln2_b, b_ffn_dw=b_ffn_dw)

    grad_x, large_grads, small_sums = _local_step(x[0], loss_target[0], _gather(shards, staged=True), small,
                                                   _scatter, order)

    out = {}
    for name, (own, recv) in zip(_LARGE, large_grads):
        out[name] = _adam_large("adam_" + name, own, recv, me_op, w[name][0], m[name][0], v[name][0])

    totals = _small_allreduce(list(small_sums))
    out.update(_adam_replicated(totals[:4], w, m, v))
    summed = _small_views(totals)
    loss = summed["loss"][0, 0]
    conv_dw_g = lax.dynamic_slice_in_dim(summed["w_conv_dw"].reshape(CONV_K, CONV_DIM), me * _CONV_DW_SHARD, _CONV_DW_SHARD, axis=1)
    ffn_dw_g = lax.dynamic_slice_in_dim(summed["w_ffn_dw"].reshape(FFN_K, D_FF), me * _FFN_DW_SHARD, _FFN_DW_SHARD, axis=1)
    small_g = dict(b_ffn_dw=summed["b_ffn_dw"], w_conv_dw=conv_dw_g.reshape(1, -1), w_ffn_dw=ffn_dw_g.reshape(1, -1))
    names = list(small_g)
    flat = lambda d, n: d[n].reshape(1, -1)
    n_small = sum(small_g[n].shape[1] for n in names)
    pad = (-n_small) % 1024
    pack = lambda pieces: jnp.pad(jnp.concatenate(pieces, axis=1), ((0, 0), (0, pad))).reshape(-1, 128)
    d_s, m_s, v_s = _adam_small(pack([flat(w, n) for n in names]), pack([small_g[n] for n in names]),
                                pack([flat(m, n) for n in names]), pack([flat(v, n) for n in names]))
    pos = 0
    for n in names:
        size = small_g[n].shape[1]
        cut = lambda a: a.reshape(1, -1)[:, pos:pos + size].reshape(w[n].shape)
        out[n] = (small_g[n].reshape(w[n].shape), cut(d_s), cut(m_s), cut(v_s))
        pos += size

    for name in _LARGE:
        out[name] = tuple(a.reshape(w[name].shape) for a in out[name])
    grads = [out[n][0] for n in _WEIGHTS]
    deltas = [out[n][1] for n in _WEIGHTS]
    new_m = [out[n][2] for n in _WEIGHTS]
    new_v = [out[n][3] for n in _WEIGHTS]
    return (loss, grad_x[None], *grads, *deltas, *new_m, *new_v)
```

```python
import functools
import math

import jax
import jax.numpy as jnp
from jax import lax
from jax.experimental import pallas as pl
from jax.experimental.pallas import tpu as pltpu

F32 = jnp.float32
BF16 = jnp.bfloat16

N_DEV = 8
D_MODEL = 1024
CONV_DIM = 512
CONV_K = 31
HGRN_DIM = 1024
HEADS = 8
HEAD_DIM = 128
D_FF = 2816
FFN_K = 3
FF_SHARD = 2 * D_FF // N_DEV
IN_COLS = 7168
IN_SHARD = IN_COLS // N_DEV
LN_EPS = 1e-5
RMS_EPS = 1e-6
ALPHA = 2.0 ** 0.25

ADAM_LR = 0.001
ADAM_B1 = 0.9
ADAM_B2 = 0.999
ADAM_EPS = 1e-08
ADAM_WD = 0.01
ADAM_STEP = 10

CHUNK = 64
CHUNKS_PER_BLOCK = 32
CONV_HALO = 32
FFN_HALO = 8
ROW_BLOCK = 64
SUBLANES = 8
VMEM_LIMIT = 48 * 1024 * 1024
MXU_DEPTH = 256

DP_MERGE_BLOCK = 0
DP_CONV_BLOCK = 2
DP_HEAD_BLOCK = 6

MESH = pl.DeviceIdType.MESH
ANY = pl.BlockSpec(memory_space=pl.ANY)

NN = (((1,), (0,)), ((), ()))
NT = (((1,), (1,)), ((), ()))
TN = (((0,), (0,)), ((), ()))


def _params(sem):
    return pltpu.CompilerParams(dimension_semantics=sem, vmem_limit_bytes=VMEM_LIMIT)


def _dot(a, b, dims):
    return lax.dot_general(a.astype(BF16), b.astype(BF16), dims, preferred_element_type=F32)


def _sigmoid(x):
    return jax.nn.sigmoid(x)


def _ln(r):
    mu = jnp.mean(r, axis=-1, keepdims=True)
    xc = r - mu
    var = jnp.mean(xc * xc, axis=-1, keepdims=True)
    rstd = lax.rsqrt(var + LN_EPS)
    return xc * rstd, rstd


def _ln_bwd(dy, xhat, rstd, g):
    dxh = dy * g
    m1 = jnp.mean(dxh, axis=-1, keepdims=True)
    m2 = jnp.mean(dxh * xhat, axis=-1, keepdims=True)
    return rstd * (dxh - m1 - xhat * m2)


def _colsum(x):
    return jnp.sum(x, axis=0, keepdims=True)


class _Hosted:
    def __init__(self, inputs, out_shapes, sem_shapes, start, finish):
        self.inputs, self.out_shapes, self.sem_shapes = list(inputs), list(out_shapes), list(sem_shapes)
        self.start, self.finish = start, finish


def _call(body, *, name, grid, in_specs, out_specs, out_shape, scratch_shapes, semantics, operands, hosted=None,
          aliases=None):
    aliases = aliases or {}
    if hosted is None:
        return pl.pallas_call(
            body, name=name, grid=grid, in_specs=list(in_specs), out_specs=list(out_specs), out_shape=list(out_shape),
            scratch_shapes=list(scratch_shapes), input_output_aliases=aliases,
            compiler_params=_params(semantics))(*operands)
    n_in, n_out, n_scr = len(in_specs), len(out_specs), len(scratch_shapes)
    h_in, h_out = len(hosted.inputs), len(hosted.out_shapes)

    def full_body(*refs):
        ins, refs = refs[:n_in], refs[n_in:]
        h_ins, refs = refs[:h_in], refs[h_in:]
        outs, refs = refs[:n_out], refs[n_out:]
        h_outs, refs = refs[:h_out], refs[h_out:]
        scr, sems = refs[:n_scr], refs[n_scr:]
        first = functools.reduce(jnp.logical_and, [pl.program_id(d) == 0 for d in range(len(grid))])
        last = functools.reduce(jnp.logical_and, [pl.program_id(d) == grid[d] - 1 for d in range(len(grid))])

        @pl.when(first)
        def _():
            hosted.start(h_ins, h_outs, sems)

        body(*ins, *outs, *scr)

        @pl.when(last)
        def _():
            hosted.finish(h_ins, h_outs, sems)

    return pl.pallas_call(
        full_body, name=name, grid=grid, in_specs=list(in_specs) + [ANY] * h_in,
        out_specs=list(out_specs) + [ANY] * h_out, out_shape=list(out_shape) + hosted.out_shapes,
        scratch_shapes=list(scratch_shapes) + hosted.sem_shapes, input_output_aliases=aliases,
        compiler_params=pltpu.CompilerParams(dimension_semantics=("arbitrary",) * len(grid),
                                             vmem_limit_bytes=VMEM_LIMIT, has_side_effects=True),
    )(*operands, *hosted.inputs)


def _mm(name, a, b, out_shape, out_dtype, grid, a_spec, b_spec, o_spec, dims, acc_shape, hosted=None):
    nk = grid[2]
    if nk == 1:
        def body(a_ref, b_ref, o_ref):
            o_ref[...] = _dot(a_ref[...], b_ref[...], dims).astype(o_ref.dtype)
        scratch = []
    else:
        def body(a_ref, b_ref, o_ref, acc_ref):
            k = pl.program_id(2)

            @pl.when(k == 0)
            def _():
                acc_ref[...] = jnp.zeros_like(acc_ref)

            acc_ref[...] += _dot(a_ref[...], b_ref[...], dims)

            @pl.when(k == nk - 1)
            def _():
                o_ref[...] = acc_ref[...].astype(o_ref.dtype)
        scratch = [pltpu.VMEM(acc_shape, F32)]

    outs = _call(body, name=name, grid=grid, in_specs=[a_spec, b_spec], out_specs=[o_spec],
                 out_shape=[jax.ShapeDtypeStruct(out_shape, out_dtype)], scratch_shapes=scratch,
                 semantics=("parallel", "parallel", "arbitrary"), operands=(a, b), hosted=hosted)
    return outs[0] if hosted is None else (outs[0], list(outs[1:]))


def _mm_fused(name, grid, operands, in_specs, out_shape, out_specs, dims, acc_shape, epilogue, lhs=None, scratch=(),
              hosted=None):
    nk = grid[2]
    n_in, n_out = len(in_specs), len(out_specs)

    def body(*refs):
        ins, outs, scr = refs[:n_in], refs[n_in:n_in + n_out], refs[n_in + n_out:]
        acc_ref, k = scr[0], pl.program_id(2)
        a = ins[0][...] if lhs is None else lhs(ins, outs)
        part = _dot(a, ins[1][...], dims)
        if nk == 1:
            acc_ref[...] = part
            epilogue(acc_ref, ins, outs, scr[1:])
            return

        @pl.when(k == 0)
        def _():
            acc_ref[...] = jnp.zeros_like(acc_ref)

        acc_ref[...] += part

        @pl.when(k == nk - 1)
        def _():
            epilogue(acc_ref, ins, outs, scr[1:])

    return _call(body, name=name, grid=grid, in_specs=in_specs, out_specs=out_specs, out_shape=out_shape,
                 scratch_shapes=[pltpu.VMEM(acc_shape, F32)] + list(scratch), semantics=("arbitrary",) * 3,
                 operands=operands, hosted=hosted)


def _row_blocks(rows, block=256):
    block = block if rows % block == 0 else rows
    return [slice(r, r + block) for r in range(0, rows, block)]


def _pick(t, pref):
    return pref if t % pref == 0 else t


def _glu(p):
    return p[:, :CONV_DIM] * _sigmoid(p[:, CONV_DIM:])


def _by_phase(taps):
    phases = {}
    for off, payload in taps:
        phases.setdefault(off % SUBLANES, []).append((off - off % SUBLANES, payload))
    return sorted(phases.items())


def _tap_sum(src_ref, base, taps, rows, lanes):
    acc = None
    for phase, items in _by_phase(taps):
        n = rows if phase == 0 else rows + SUBLANES
        part = None
        for off, (w_ref, k) in items:
            term = w_ref[k:k + 1, lanes] * src_ref[base + off:base + off + n, lanes]
            part = term if part is None else part + term
        if phase:
            part = part[phase:phase + rows, :]
        acc = part if acc is None else acc + part
    return acc


def _tap_products(x, src_ref, base, taps, lanes):
    rows, cols = x.shape
    pad = jnp.zeros((SUBLANES, cols), x.dtype)
    padded = jnp.concatenate([pad, x, pad], axis=0)
    out = []
    for phase, items in _by_phase(taps):
        n = rows if phase == 0 else rows + SUBLANES
        shifted = x if phase == 0 else padded[SUBLANES - phase:SUBLANES - phase + n, :]
        for off, key in items:
            out.append((key, _colsum(shifted * src_ref[base + off:base + off + n, lanes])))
    return out


def _lane_blocks(cols, block=256):
    return [slice(c, min(c + block, cols)) for c in range(0, cols, block)]


def _conv_fwd(proj, w_dw, b_dw, g, b):
    t = proj.shape[0]
    tm = _pick(t, 512)
    nh = tm // CONV_HALO

    def body(p_ref, ph_ref, w_ref, bd_ref, g_ref, b_ref, act_ref, pre_ref, xs_ref):
        i = pl.program_id(0)
        halo = _glu(ph_ref[...])
        xs_ref[0:CONV_HALO, :] = jnp.where(i == 0, 0.0, halo)
        xs_ref[CONV_HALO:CONV_HALO + tm, :] = _glu(p_ref[...])
        taps = [(CONV_HALO - (CONV_K - 1) + k, (w_ref, k)) for k in range(CONV_K)]
        for r in range(tm // ROW_BLOCK):
            rows = slice(r * ROW_BLOCK, (r + 1) * ROW_BLOCK)
            for lanes in _lane_blocks(CONV_DIM):
                pre_ref[rows, lanes] = bd_ref[:, lanes] + _tap_sum(xs_ref, r * ROW_BLOCK, taps, ROW_BLOCK, lanes)
            acc = pre_ref[rows, :]
            xhat, _ = _ln(acc)
            yln = xhat * g_ref[...] + b_ref[...]
            act_ref[rows, :] = (yln * _sigmoid(yln)).astype(BF16)

    full = lambda s: pl.BlockSpec(s, lambda i: (0, 0))
    return pl.pallas_call(
        body, name="conv_fwd", grid=(t // tm,),
        in_specs=[pl.BlockSpec((tm, 2 * CONV_DIM), lambda i: (i, 0)),
                  pl.BlockSpec((CONV_HALO, 2 * CONV_DIM), lambda i: (jnp.maximum(i * nh - 1, 0), 0)),
                  full((CONV_K, CONV_DIM)), full((1, CONV_DIM)), full((1, CONV_DIM)), full((1, CONV_DIM))],
        out_specs=[pl.BlockSpec((tm, CONV_DIM), lambda i: (i, 0)), pl.BlockSpec((tm, CONV_DIM), lambda i: (i, 0))],
        out_shape=[jax.ShapeDtypeStruct((t, CONV_DIM), BF16), jax.ShapeDtypeStruct((t, CONV_DIM), F32)],
        scratch_shapes=[pltpu.VMEM((CONV_HALO + tm, CONV_DIM), F32)],
        compiler_params=_params(("arbitrary",)),
    )(proj, proj, w_dw, b_dw, g, b)


def _d_c_norm_bwd(d_y, w_conv_out, pre, g, b):
    t = pre.shape[0]
    tm = _pick(t, 512)
    d = D_MODEL

    def epilogue(d_c, ins, outs, scr):
        pre_ref, g_ref, b_ref = ins[2:]
        dpre_ref, sums_ref = outs
        i = pl.program_id(0)

        @pl.when(i == 0)
        def _():
            sums_ref[...] = jnp.zeros_like(sums_ref)

        for rows in _row_blocks(tm):
            xhat, rstd = _ln(pre_ref[rows, :])
            yln = xhat * g_ref[...] + b_ref[...]
            sg = _sigmoid(yln)
            dyln = d_c[rows, :] * (sg * (1.0 + yln * (1.0 - sg)))
            dpre = _ln_bwd(dyln, xhat, rstd, g_ref[...])
            dpre_ref[rows, :] = dpre
            sums_ref[0:1, :] += _colsum(dyln * xhat)
            sums_ref[1:2, :] += _colsum(dyln)
            sums_ref[2:3, :] += _colsum(dpre)

    full = lambda s: pl.BlockSpec(s, lambda i, j, k: (0, 0))
    tile = pl.BlockSpec((tm, CONV_DIM), lambda i, j, k: (i, 0))
    return _mm_fused(
        "d_c_norm_bwd", (t // tm, 1, 1), (d_y, w_conv_out, pre, g, b),
        [pl.BlockSpec((None, tm, d), lambda i, j, k: (0, i, 0)), full((CONV_DIM, d)), tile,
         full((1, CONV_DIM)), full((1, CONV_DIM))],
        [jax.ShapeDtypeStruct((t, CONV_DIM), F32), jax.ShapeDtypeStruct((8, CONV_DIM), F32)],
        [tile, full((8, CONV_DIM))], NT, (tm, CONV_DIM), epilogue)


def _conv_bwd_dw(d_pre, proj, w_dw, d_proj):
    t = d_pre.shape[0]
    tm = _pick(t, 512)
    nt = t // tm
    nh = tm // CONV_HALO
    last_h = t // CONV_HALO - 1

    def body(dp_ref, dph_ref, p_ref, ph_ref, w_ref, _, dproj_ref, dw_ref, xs_ref, ds_ref):
        i = pl.program_id(0)

        @pl.when(i == 0)
        def _():
            dw_ref[...] = jnp.zeros_like(dw_ref)

        halo = _glu(ph_ref[...])
        xs_ref[0:CONV_HALO, :] = jnp.where(i == 0, 0.0, halo)
        xs_ref[CONV_HALO:CONV_HALO + tm, :] = _glu(p_ref[...])
        ds_ref[0:tm, :] = dp_ref[...]
        ds_ref[tm:tm + CONV_HALO, :] = jnp.where(i == nt - 1, 0.0, dph_ref[...])
        back_taps = [(CONV_K - 1 - k, (w_ref, k)) for k in range(CONV_K)]
        grad_taps = [(CONV_HALO - (CONV_K - 1) + k, k) for k in range(CONV_K)]
        for r in range(tm // ROW_BLOCK):
            base = r * ROW_BLOCK
            rows = slice(base, base + ROW_BLOCK)
            for lanes in _lane_blocks(CONV_DIM):
                gate_lanes = slice(CONV_DIM + lanes.start, CONV_DIM + lanes.stop)
                acc = _tap_sum(ds_ref, base, back_taps, ROW_BLOCK, lanes)
                for k, total in _tap_products(ds_ref[rows, lanes], xs_ref, base, grad_taps, lanes):
                    dw_ref[k:k + 1, lanes] += total
                cval = p_ref[rows, lanes]
                sg = _sigmoid(p_ref[rows, gate_lanes])
                dproj_ref[rows, lanes] = (acc * sg).astype(BF16)
                dproj_ref[rows, gate_lanes] = (acc * cval * sg * (1.0 - sg)).astype(BF16)

    full = lambda s: pl.BlockSpec(s, lambda i: (0, 0))
    return pl.pallas_call(
        body, name="conv_bwd_dw", grid=(nt,),
        in_specs=[pl.BlockSpec((tm, CONV_DIM), lambda i: (i, 0)),
                  pl.BlockSpec((CONV_HALO, CONV_DIM), lambda i: (jnp.minimum((i + 1) * nh, last_h), 0)),
                  pl.BlockSpec((tm, 2 * CONV_DIM), lambda i: (i, 0)),
                  pl.BlockSpec((CONV_HALO, 2 * CONV_DIM), lambda i: (jnp.maximum(i * nh - 1, 0), 0)),
                  full((CONV_K, CONV_DIM)), ANY],
        out_specs=[pl.BlockSpec((tm, 2 * CONV_DIM), lambda i: (i, DP_CONV_BLOCK)), full((CONV_HALO, CONV_DIM))],
        out_shape=[jax.ShapeDtypeStruct(d_proj.shape, BF16), jax.ShapeDtypeStruct((CONV_HALO, CONV_DIM), F32)],
        scratch_shapes=[pltpu.VMEM((CONV_HALO + tm, CONV_DIM), F32), pltpu.VMEM((tm + CONV_HALO, CONV_DIM), F32)],
        input_output_aliases={5: 0},
        compiler_params=_params(("arbitrary",)),
    )(d_pre, d_pre, proj, proj, w_dw, d_proj)


def _lower_bound(logit_ref):
    l0 = logit_ref[0:1, :]
    l1 = logit_ref[1:2, :]
    m = jnp.maximum(l0, l1)
    e0 = jnp.exp(l0 - m)
    e1 = jnp.exp(l1 - m)
    return e0 / (e0 + e1)


def _tri(lower):
    r = lax.broadcasted_iota(jnp.int32, (CHUNK, CHUNK), 0)
    c = lax.broadcasted_iota(jnp.int32, (CHUNK, CHUNK), 1)
    return (c <= r) if lower else (c >= r)


def _hgrn_gates(fz, lb):
    s = _sigmoid(fz)
    sn = _sigmoid(-fz)
    f = lb + (1.0 - lb) * s
    return s, sn, f


def _block_tri(rows, lower=True):
    r = lax.broadcasted_iota(jnp.int32, (rows, rows), 0)
    c = lax.broadcasted_iota(jnp.int32, (rows, rows), 1)
    tri = (c <= r) if lower else (c >= r)
    return (tri & (r // CHUNK == c // CHUNK)).astype(BF16)


def _tri_rows(tm):
    return min(tm, MXU_DEPTH)


def _tri_matmul(tri_ref, x):
    hi = x.astype(BF16)
    lo = (x - hi.astype(F32)).astype(BF16)
    tri = tri_ref[...]
    return (lax.dot_general(tri, hi, NN, preferred_element_type=F32)
            + lax.dot_general(tri, lo, NN, preferred_element_type=F32))


def _groups(tm):
    g = _tri_rows(tm)
    return [slice(i * g, (i + 1) * g) for i in range(tm // g)]


def _hgrn_fwd(proj, logits, norm_g):
    t = proj.shape[0]
    tm = CHUNK * CHUNKS_PER_BLOCK if t % (CHUNK * CHUNKS_PER_BLOCK) == 0 else CHUNK
    cpb = tm // CHUNK
    nt = t // tm
    half = CHUNK // 2

    def body(qz_ref, fz_ref, iv_ref, gz_ref, lg_ref, ng_ref, tri_ref, o_ref, og_ref, st_ref,
             state_ref, qe_ref, ke_ref, qb_ref, kl_ref, v_ref, upd_ref, decay_ref, a_ref, q_ref, kk_ref, b_ref):
        j = pl.program_id(1)

        @pl.when(j == 0)
        def _():
            state_ref[...] = jnp.zeros_like(state_ref)

        lb = _lower_bound(lg_ref)
        chunks = [slice(c * CHUNK, (c + 1) * CHUNK) for c in range(cpb)]
        for rows in chunks:
            qz = qz_ref[rows, :]
            q_ref[rows, :] = qz * _sigmoid(qz)
            _, sn, f = _hgrn_gates(fz_ref[rows, :], lb)
            kk_ref[rows, :] = (1.0 - lb) * sn
            b_ref[rows, :] = jnp.log(f)
            v_ref[rows, :] = iv_ref[rows, :].astype(BF16)
        for rows in _groups(tm):
            b_ref[rows, :] = _tri_matmul(tri_ref, b_ref[rows, :])
        for c, rows in enumerate(chunks):
            b = b_ref[rows, :]
            bref = b[half - 1:half, :]
            blast = b[CHUNK - 1:CHUNK, :]
            q = q_ref[rows, :]
            kk = kk_ref[rows, :]
            qb_ref[rows, :] = (q * jnp.exp(b)).astype(BF16)
            qe_ref[rows, :] = (q * jnp.exp(b - bref)).astype(BF16)
            ke_ref[rows, :] = (kk * jnp.exp(bref - b)).astype(BF16)
            kl_ref[rows, :] = (kk * jnp.exp(blast - b)).astype(BF16)
            decay_ref[c:c + 1, :] = jnp.exp(blast)
        causal = _tri(True)
        for c, rows in enumerate(chunks):
            upd_ref[c] = _dot(v_ref[rows, :], kl_ref[rows, :], TN)
            a_ref[c] = jnp.where(causal, _dot(qe_ref[rows, :], ke_ref[rows, :], NT), 0.0).astype(BF16)
        state = state_ref[...]
        for c in range(cpb):
            st_ref[c] = state.astype(BF16)
            state = state * decay_ref[c:c + 1, :] + upd_ref[c]
        state_ref[...] = state
        for c, rows in enumerate(chunks):
            o_ref[rows, :] = _dot(a_ref[c], v_ref[rows, :], NN) + _dot(qb_ref[rows, :], st_ref[c], NT)
        for rows in chunks:
            o = o_ref[rows, :]
            r = lax.rsqrt(jnp.mean(o * o, axis=-1, keepdims=True) + RMS_EPS)
            gz = gz_ref[rows, :]
            og_ref[rows, :] = ((o * r * ng_ref[...]) * (gz * _sigmoid(gz))).astype(BF16)

    col = lambda base: pl.BlockSpec((tm, HEAD_DIM), lambda h, j: (j, base + h))
    tile_bf = pltpu.VMEM((tm, HEAD_DIM), BF16)
    tile_f32 = pltpu.VMEM((tm, HEAD_DIM), F32)
    return pl.pallas_call(
        body, name="hgrn_fwd", grid=(HEADS, nt),
        in_specs=[col(8), col(16), col(24), col(32),
                  pl.BlockSpec((2, HEAD_DIM), lambda h, j: (0, h)), pl.BlockSpec((1, HEAD_DIM), lambda h, j: (0, h)),
                  pl.BlockSpec((_tri_rows(tm), _tri_rows(tm)), lambda h, j: (0, 0))],
        out_specs=[col(0), col(0), pl.BlockSpec((None, cpb, HEAD_DIM, HEAD_DIM), lambda h, j: (h, j, 0, 0))],
        out_shape=[jax.ShapeDtypeStruct((t, HGRN_DIM), F32), jax.ShapeDtypeStruct((t, HGRN_DIM), BF16),
                   jax.ShapeDtypeStruct((HEADS, t // CHUNK, HEAD_DIM, HEAD_DIM), BF16)],
        scratch_shapes=[pltpu.VMEM((HEAD_DIM, HEAD_DIM), F32), tile_bf, tile_bf, tile_bf, tile_bf, tile_bf,
                        pltpu.VMEM((cpb, HEAD_DIM, HEAD_DIM), F32), pltpu.VMEM((max(cpb, 8), HEAD_DIM), F32),
                        pltpu.VMEM((cpb, CHUNK, CHUNK), BF16), tile_f32, tile_f32, tile_f32],
        compiler_params=_params(("parallel", "arbitrary")),
    )(proj, proj, proj, proj, logits, norm_g, _block_tri(_tri_rows(tm)))


def _hgrn_bwd(d_og, o, proj, states, logits, norm_g, d_proj, hosted=None):
    t = proj.shape[0]
    tm = CHUNK * CHUNKS_PER_BLOCK if t % (CHUNK * CHUNKS_PER_BLOCK) == 0 else CHUNK
    cpb = tm // CHUNK
    nt = t // tm
    half = CHUNK // 2

    def body(dog_ref, o_ref, qz_ref, fz_ref, iv_ref, gz_ref, st_ref, lg_ref, ng_ref, tril_ref, triu_ref,
             _, dp_ref, sums_ref,
             dstate_ref, qe_ref, ke_ref, qb_ref, kl_ref, v_ref, do_ref, upd_ref, dst_ref, a_ref, da_ref,
             decay_ref, through_ref, q_ref, kk_ref, b_ref, dsilu_ref, gs_ref, gf_ref, sn_ref,
             eb_ref, ebr_ref, ekr_ref, ebl_ref, rev_ref, pre_ref, dk_ref):
        j = pl.program_id(1)

        @pl.when(j == 0)
        def _():
            dstate_ref[...] = jnp.zeros_like(dstate_ref)
            sums_ref[...] = jnp.zeros_like(sums_ref)

        lb = _lower_bound(lg_ref)
        ng = ng_ref[...]
        chunks = [slice(c * CHUNK, (c + 1) * CHUNK) for c in range(cpb)]
        for rows in chunks:
            qz = qz_ref[rows, :]
            sq = _sigmoid(qz)
            q_ref[rows, :] = qz * sq
            dsilu_ref[rows, :] = sq * (1.0 + qz * (1.0 - sq))
            s, sn, f = _hgrn_gates(fz_ref[rows, :], lb)
            kk_ref[rows, :] = (1.0 - lb) * sn
            b_ref[rows, :] = jnp.log(f)
            sn_ref[rows, :] = sn
            gf_ref[rows, :] = sn / f
            gs_ref[rows, :] = (1.0 - lb) * s
            v_ref[rows, :] = iv_ref[rows, :].astype(BF16)
            ov = o_ref[rows, :]
            r = lax.rsqrt(jnp.mean(ov * ov, axis=-1, keepdims=True) + RMS_EPS)
            on = ov * r
            gz = gz_ref[rows, :]
            sg = _sigmoid(gz)
            dog = dog_ref[rows, :]
            dp_ref[rows, 3 * HEAD_DIM:4 * HEAD_DIM] =(dog * (on * ng) * (sg * (1.0 + gz * (1.0 - sg)))).astype(BF16)
            d_ong = dog * (gz * sg)
            sums_ref[0:1, :] += _colsum(d_ong * on)
            d_on = d_ong * ng
            do_ref[rows, :] = (r * (d_on - on * jnp.mean(d_on * on, axis=-1, keepdims=True))).astype(BF16)
        for rows in _groups(tm):
            b_ref[rows, :] = _tri_matmul(tril_ref, b_ref[rows, :])
        for c, rows in enumerate(chunks):
            b = b_ref[rows, :]
            bref = b[half - 1:half, :]
            blast = b[CHUNK - 1:CHUNK, :]
            q = q_ref[rows, :]
            kk = kk_ref[rows, :]
            eb = jnp.exp(b)
            ebr = jnp.exp(b - bref)
            ekr = jnp.exp(bref - b)
            ebl = jnp.exp(blast - b)
            eb_ref[rows, :] = eb
            ebr_ref[rows, :] = ebr
            ekr_ref[rows, :] = ekr
            ebl_ref[rows, :] = ebl
            qb_ref[rows, :] = (q * eb).astype(BF16)
            qe_ref[rows, :] = (q * ebr).astype(BF16)
            ke_ref[rows, :] = (kk * ekr).astype(BF16)
            kl_ref[rows, :] = (kk * ebl).astype(BF16)
            decay_ref[c:c + 1, :] = jnp.exp(blast)
        causal = _tri(True)
        for c, rows in enumerate(chunks):
            upd_ref[c] = _dot(do_ref[rows, :], qb_ref[rows, :], TN)
            a_ref[c] = jnp.where(causal, _dot(qe_ref[rows, :], ke_ref[rows, :], NT), 0.0).astype(BF16)
            da_ref[c] = jnp.where(causal, _dot(do_ref[rows, :], v_ref[rows, :], NT), 0.0).astype(BF16)
        dstate = dstate_ref[...]
        for c in reversed(range(cpb)):
            dst_ref[c] = dstate.astype(BF16)
            decay = decay_ref[c:c + 1, :]
            through_ref[c:c + 1, :] = decay * _colsum(dstate * st_ref[c].astype(F32))
            dstate = dstate * decay + upd_ref[c]
        dstate_ref[...] = dstate
        for c, rows in enumerate(chunks):
            dp_ref[rows, 2 * HEAD_DIM:3 * HEAD_DIM] =(_dot(a_ref[c], do_ref[rows, :], TN)
                                + _dot(kl_ref[rows, :], dst_ref[c], NT)).astype(BF16)
        for c, rows in enumerate(chunks):
            dqe = _dot(da_ref[c], ke_ref[rows, :], NN)
            dq_inter = _dot(do_ref[rows, :], st_ref[c], NN) * eb_ref[rows, :]
            dp_ref[rows, 0:HEAD_DIM] =((dqe * ebr_ref[rows, :] + dq_inter) * dsilu_ref[rows, :]).astype(BF16)
            rev_ref[rows, :] = qe_ref[rows, :].astype(F32) * dqe + q_ref[rows, :] * dq_inter
        for c, rows in enumerate(chunks):
            dke = _dot(da_ref[c], qe_ref[rows, :], TN)
            dk_inter = _dot(v_ref[rows, :], dst_ref[c], NN) * ebl_ref[rows, :]
            dk_ref[rows, :] = dke * ekr_ref[rows, :] + dk_inter
            rev_ref[rows, :] -= ke_ref[rows, :].astype(F32) * dke
            pre_ref[rows, :] = kk_ref[rows, :] * dk_inter
        for rows in _groups(tm):
            pre = pre_ref[rows, :]
            rev_ref[rows, :] = _tri_matmul(triu_ref, rev_ref[rows, :]) + (_tri_matmul(tril_ref, pre) - pre)
        for c, rows in enumerate(chunks):
            dlf = rev_ref[rows, :] + through_ref[c:c + 1, :]
            common = gf_ref[rows, :] * dlf - sn_ref[rows, :] * dk_ref[rows, :]
            dp_ref[rows, HEAD_DIM:2 * HEAD_DIM] =(gs_ref[rows, :] * common).astype(BF16)
            sums_ref[1:2, :] += _colsum(common)

        @pl.when(j == nt - 1)
        def _():
            sums_ref[1:2, :] = sums_ref[1:2, :] * lb * (1.0 - lb)

    rev = lambda base: pl.BlockSpec((tm, HEAD_DIM), lambda h, j: (nt - 1 - j, base + h))
    vec = lambda n: pl.BlockSpec((n, HEAD_DIM), lambda h, j: (0, h))
    const = pl.BlockSpec((_tri_rows(tm), _tri_rows(tm)), lambda h, j: (0, 0))
    tile_bf = pltpu.VMEM((tm, HEAD_DIM), BF16)
    tile_f32 = pltpu.VMEM((tm, HEAD_DIM), F32)
    square = lambda dtype: pltpu.VMEM((cpb, HEAD_DIM, HEAD_DIM), dtype)
    rows8 = pltpu.VMEM((max(cpb, 8), HEAD_DIM), F32)
    operands = (d_og, o, proj, proj, proj, proj, states, logits, norm_g, _block_tri(_tri_rows(tm)),
                _block_tri(_tri_rows(tm), lower=False), d_proj)
    return _call(
        body, name="hgrn_bwd", grid=(HEADS, nt),
        in_specs=[rev(0), rev(0), rev(8), rev(16), rev(24), rev(32),
                  pl.BlockSpec((None, cpb, HEAD_DIM, HEAD_DIM), lambda h, j: (h, nt - 1 - j, 0, 0)),
                  vec(2), vec(1), const, const, ANY],
        out_specs=[pl.BlockSpec((tm, 4 * HEAD_DIM), lambda h, j: (nt - 1 - j, DP_HEAD_BLOCK + h)), vec(8)],
        out_shape=[jax.ShapeDtypeStruct(d_proj.shape, BF16), jax.ShapeDtypeStruct((8, HGRN_DIM), F32)],
        scratch_shapes=[pltpu.VMEM((HEAD_DIM, HEAD_DIM), F32)] + [tile_bf] * 6 + [square(F32), square(BF16)]
        + [pltpu.VMEM((cpb, CHUNK, CHUNK), BF16)] * 2 + [rows8, rows8] + [tile_f32] * 14,
        semantics=("parallel", "arbitrary"), operands=operands, hosted=hosted, aliases={len(operands) - 1: 0})


def _mix_ln1(proj, y_conv, y_hgrn, w_out, x, g, b):
    t = x.shape[0]
    tm = _pick(t, 512)
    d = D_MODEL

    def lhs(ins, outs):
        for rows in _row_blocks(tm):
            outs[0][rows, :] = (_sigmoid(ins[0][rows, :]) * ins[3][rows, :]
                                + _sigmoid(ins[2][rows, :]) * ins[4][rows, :]).astype(BF16)
        return outs[0][...]

    def epilogue(acc, ins, outs, scr):
        for rows in _row_blocks(tm):
            r = ALPHA * ins[5][rows, :] + acc[rows, :]
            outs[1][rows, :] = r
            xhat, _ = _ln(r)
            outs[2][rows, :] = (xhat * ins[6][...] + ins[7][...]).astype(BF16)

    tile = pl.BlockSpec((tm, d), lambda i, j, k: (i, 0))
    vec = pl.BlockSpec((1, d), lambda i, j, k: (0, 0))
    return _mm_fused(
        "mix_ln1", (t // tm, 1, 1), (proj, w_out, proj, y_conv, y_hgrn, x, g, b),
        [pl.BlockSpec((tm, d), lambda i, j, k: (i, 5)), pl.BlockSpec((d, d), lambda i, j, k: (0, 0)),
         pl.BlockSpec((tm, d), lambda i, j, k: (i, 6)), tile, tile, tile, vec, vec],
        [jax.ShapeDtypeStruct((t, d), BF16), jax.ShapeDtypeStruct((t, d), F32), jax.ShapeDtypeStruct((t, d), BF16)],
        [tile, tile, tile], NN, (tm, d), epilogue, lhs=lhs)


def _d_mixed_merge_bwd(d_r1b, w_out, proj, y_conv, y_hgrn):
    t = proj.shape[0]
    tm = _pick(t, 512)
    d = D_MODEL

    def epilogue(d_mixed, ins, outs, scr):
        dy_ref, dmz_ref = outs
        for rows in _row_blocks(tm):
            dm = d_mixed[rows, :]
            for br in range(2):
                sg = _sigmoid(ins[2 + br][rows, :])
                dy_ref[br, rows, :] = (sg * dm).astype(BF16)
                dmz_ref[rows, br * d:(br + 1) * d] = (dm * ins[4 + br][rows, :] * sg * (1.0 - sg)).astype(BF16)

    tile = pl.BlockSpec((tm, d), lambda i, j, k: (i, 0))
    return _mm_fused(
        "d_mixed_merge_bwd", (t // tm, 1, 1), (d_r1b, w_out, proj, proj, y_conv, y_hgrn),
        [tile, pl.BlockSpec((d, d), lambda i, j, k: (0, 0)), pl.BlockSpec((tm, d), lambda i, j, k: (i, 5)),
         pl.BlockSpec((tm, d), lambda i, j, k: (i, 6)), tile, tile],
        [jax.ShapeDtypeStruct((2, t, d), BF16), jax.ShapeDtypeStruct((t, IN_COLS), BF16)],
        [pl.BlockSpec((2, tm, d), lambda i, j, k: (0, i, 0)),
         pl.BlockSpec((tm, 2 * d), lambda i, j, k: (i, DP_MERGE_BLOCK))],
        NT, (tm, d), epilogue)


def _ffn_out_ln2(act, w_ffn_out, r1, target, g1, b1, g2, b2):
    t = r1.shape[0]
    tm = _pick(t, 1024)
    nt = t // tm
    d = D_MODEL

    def epilogue(y_ffn, ins, outs, scr):
        r1_ref, tg_ref, g1_ref, b1_ref, g2_ref, b2_ref = ins[2:]
        dr_ref, drb_ref, sums_ref = outs
        (sq_ref,) = scr
        i = pl.program_id(0)

        @pl.when(i == 0)
        def _():
            sums_ref[...] = jnp.zeros_like(sums_ref)
            sq_ref[...] = jnp.zeros_like(sq_ref)

        for rows in _row_blocks(tm):
            xh1, _ = _ln(r1_ref[rows, :])
            x1 = xh1 * g1_ref[...] + b1_ref[...]
            xh2, rstd2 = _ln(ALPHA * x1 + y_ffn[rows, :])
            diff = xh2 * g2_ref[...] + b2_ref[...] - tg_ref[rows, :]
            dy = diff * (1.0 / D_MODEL)
            dr = _ln_bwd(dy, xh2, rstd2, g2_ref[...])
            dr_ref[rows, :] = dr
            drb_ref[rows, :] = dr.astype(BF16)
            sums_ref[0:1, :] += _colsum(dy * xh2)
            sums_ref[1:2, :] += _colsum(dy)
            sq_ref[...] += _colsum(diff * diff)

        @pl.when(i == nt - 1)
        def _():
            total = jnp.sum(sq_ref[...], axis=-1, keepdims=True) * (0.5 / D_MODEL)
            sums_ref[2:3, :] = jnp.broadcast_to(total, (1, D_MODEL))

    tile = pl.BlockSpec((tm, d), lambda i, j, k: (i, 0))
    vec = pl.BlockSpec((1, d), lambda i, j, k: (0, 0))
    return _mm_fused(
        "ffn_out_ln2", (nt, 1, 4), (act, w_ffn_out, r1, target, g1, b1, g2, b2),
        [pl.BlockSpec((None, tm, FF_SHARD), lambda i, j, k: (k, i, 0)),
         pl.BlockSpec((None, FF_SHARD, d), lambda i, j, k: (k, 0, 0)), tile, tile, vec, vec, vec, vec],
        [jax.ShapeDtypeStruct((t, d), F32), jax.ShapeDtypeStruct((t, d), BF16), jax.ShapeDtypeStruct((8, d), F32)],
        [tile, tile, pl.BlockSpec((8, d), lambda i, j, k: (0, 0))], NN, (tm, d), epilogue,
        scratch=[pltpu.VMEM((1, d), F32)])


def _d_x1_ln1_bwd(d_z, w_ffn_in, d_r2, r1, g1):
    t = r1.shape[0]
    tm = _pick(t, 1024)
    d = D_MODEL

    def epilogue(dx_ffn, ins, outs, scr):
        dr2_ref, r1_ref, g_ref = ins[2:]
        dr1_ref, dr1b_ref, sums_ref = outs
        i = pl.program_id(0)

        @pl.when(i == 0)
        def _():
            sums_ref[...] = jnp.zeros_like(sums_ref)

        for rows in _row_blocks(tm):
            xhat, rstd = _ln(r1_ref[rows, :])
            dx1 = ALPHA * dr2_ref[rows, :] + dx_ffn[rows, :]
            dr1 = _ln_bwd(dx1, xhat, rstd, g_ref[...])
            dr1_ref[rows, :] = dr1
            dr1b_ref[rows, :] = dr1.astype(BF16)
            sums_ref[0:1, :] += _colsum(dx1 * xhat)
            sums_ref[1:2, :] += _colsum(dx1)

    tile = pl.BlockSpec((tm, d), lambda i, j, k: (i, 0))
    return _mm_fused(
        "d_x1_ln1_bwd", (t // tm, 1, N_DEV), (d_z, w_ffn_in, d_r2, r1, g1),
        [pl.BlockSpec((None, tm, FF_SHARD), lambda i, j, k: (k, i, 0)),
         pl.BlockSpec((None, d, FF_SHARD), lambda i, j, k: (k, 0, 0)), tile, tile,
         pl.BlockSpec((1, d), lambda i, j, k: (0, 0))],
        [jax.ShapeDtypeStruct((t, d), F32), jax.ShapeDtypeStruct((t, d), BF16), jax.ShapeDtypeStruct((8, d), F32)],
        [tile, tile, pl.BlockSpec((8, d), lambda i, j, k: (0, 0))], NT, (tm, d), epilogue)


def _cast_bf16(x):
    t = x.shape[0]
    tm = _pick(t, 512)

    def body(x_ref, o_ref):
        o_ref[...] = x_ref[...].astype(BF16)

    tile = pl.BlockSpec((tm, D_MODEL), lambda i: (i, 0))
    return pl.pallas_call(
        body, name="cast_x", grid=(t // tm,), in_specs=[tile], out_specs=tile,
        out_shape=jax.ShapeDtypeStruct((t, D_MODEL), BF16), compiler_params=_params(("parallel",)),
    )(x)


def _relayout(name, a, in_block, in_map, out_block, out_map, out_shape):
    def body(a_ref, o_ref):
        o_ref[...] = a_ref[...].astype(o_ref.dtype)

    return pl.pallas_call(
        body, name=name, grid=(N_DEV,), in_specs=[pl.BlockSpec(in_block, in_map)],
        out_specs=pl.BlockSpec(out_block, out_map), out_shape=out_shape, compiler_params=_params(("parallel",)),
    )(a)


_GELU_C = math.sqrt(2.0 / math.pi)


_GELU_CUBIC = 0.044715


def _gelu_parts(u):
    u2 = u * u
    th = jnp.tanh(u * (_GELU_C + (_GELU_C * _GELU_CUBIC) * u2))
    hu = 0.5 * u
    return th, hu + hu * th, u2, hu


BF16_ROWS = 16


def _ffn_act_fwd(z, w_dw, b_dw):
    t = z.shape[2]
    tm = _pick(t, 1024)
    nh = tm // FFN_HALO

    def body(z_ref, zh_ref, w_ref, b_ref, act_ref, gd_ref, us_ref):
        i = pl.program_id(1)
        us_ref[0:FFN_HALO, :] = jnp.where(i == 0, 0.0, zh_ref[...])
        us_ref[FFN_HALO:FFN_HALO + tm, :] = z_ref[0]
        for r in range(tm // ROW_BLOCK):
            base = r * ROW_BLOCK
            rows = slice(base, base + ROW_BLOCK)
            for lanes in _lane_blocks(FF_SHARD):
                uc = b_ref[:, lanes]
                for k in range(FFN_K):
                    off = base + FFN_HALO - (FFN_K - 1) + k
                    uc = uc + w_ref[k:k + 1, lanes] * us_ref[off:off + ROW_BLOCK, lanes]
                th, gelu, u2, hu = _gelu_parts(uc)
                dgelu = (0.5 + 0.5 * th) + (hu - hu * th * th) * (_GELU_C + (3.0 * _GELU_C * _GELU_CUBIC) * u2)
                act_ref[rows, lanes] = (gelu * z_ref[1, rows, lanes]).astype(BF16)
                gd_ref[0, rows, lanes] = gelu.astype(BF16)
                gd_ref[1, rows, lanes] = dgelu.astype(BF16)

    return pl.pallas_call(
        body, name="ffn_act_fwd", grid=(4, t // tm),
        in_specs=[pl.BlockSpec((2, None, tm, FF_SHARD), lambda j, i: (0, j, i, 0)),
                  pl.BlockSpec((None, None, FFN_HALO, FF_SHARD), lambda j, i: (0, j, jnp.maximum(i * nh - 1, 0), 0)),
                  pl.BlockSpec((None, FFN_K, FF_SHARD), lambda j, i: (j, 0, 0)),
                  pl.BlockSpec((None, 1, FF_SHARD), lambda j, i: (j, 0, 0))],
        out_specs=[pl.BlockSpec((None, tm, FF_SHARD), lambda j, i: (j, i, 0)),
                   pl.BlockSpec((2, None, tm, FF_SHARD), lambda j, i: (0, j, i, 0))],
        out_shape=[jax.ShapeDtypeStruct((4, t, FF_SHARD), BF16), jax.ShapeDtypeStruct((2, 4, t, FF_SHARD), BF16)],
        scratch_shapes=[pltpu.VMEM((FFN_HALO + tm, FF_SHARD), F32)],
        compiler_params=_params(("parallel", "arbitrary")),
    )(z, z, w_dw, b_dw)


def _ffn_act_bwd(d_act, z, gd, w_dw):
    t = z.shape[2]
    tm = _pick(t, 1024)
    nt = t // tm
    nh = tm // FFN_HALO
    last_h = t // FFN_HALO - 1
    pad = FFN_HALO - (FFN_K - 1)

    def fold(x):
        return functools.reduce(jnp.add, [x[r:r + SUBLANES, :] for r in range(0, x.shape[0], SUBLANES)])

    def body(da_ref, dah_ref, z_ref, zp_ref, gn_ref, gd_ref, gdn_ref, w_ref, dz_ref, sums_ref, us_ref, ds_ref,
             part_ref):
        i = pl.program_id(1)

        @pl.when(i == 0)
        def _():
            part_ref[...] = jnp.zeros_like(part_ref)

        us_ref[0:FFN_HALO, :] = jnp.where(i == 0, 0.0, zp_ref[...])
        us_ref[FFN_HALO:FFN_HALO + tm, :] = z_ref[0]
        for r in range(tm // ROW_BLOCK):
            base = r * ROW_BLOCK
            rows = slice(base, base + ROW_BLOCK)
            for lanes in _lane_blocks(FF_SHARD):
                da = da_ref[rows, lanes]
                dz_ref[1, rows, lanes] = (da * gd_ref[0, rows, lanes].astype(F32)).astype(BF16)
                duc = da * z_ref[1, rows, lanes] * gd_ref[1, rows, lanes].astype(F32)
                ds_ref[rows, lanes] = duc
                for k in range(FFN_K):
                    part_ref[k, :, lanes] += fold(duc * us_ref[base + pad + k:base + pad + k + ROW_BLOCK, lanes])
                part_ref[FFN_K, :, lanes] += fold(duc)
        duc_next = dah_ref[...] * gn_ref[...] * gdn_ref[0:FFN_HALO, :].astype(F32)
        ds_ref[tm:tm + FFN_HALO, :] = jnp.where(i == nt - 1, 0.0, duc_next)
        for r in range(tm // ROW_BLOCK):
            base = r * ROW_BLOCK
            for lanes in _lane_blocks(FF_SHARD):
                du = None
                for k in range(FFN_K):
                    off = base + FFN_K - 1 - k
                    term = w_ref[k:k + 1, lanes] * ds_ref[off:off + ROW_BLOCK, lanes]
                    du = term if du is None else du + term
                dz_ref[0, base:base + ROW_BLOCK, lanes] = du.astype(BF16)

        @pl.when(i == nt - 1)
        def _():
            sums_ref[...] = jnp.zeros_like(sums_ref)
            for k in range(FFN_K + 1):
                sums_ref[k:k + 1, :] = _colsum(part_ref[k])

    nxt = lambda i: jnp.minimum((i + 1) * nh, last_h)
    nxt_bf = lambda i: jnp.minimum((i + 1) * (tm // BF16_ROWS), t // BF16_ROWS - 1)
    return pl.pallas_call(
        body, name="ffn_act_bwd", grid=(4, nt),
        in_specs=[pl.BlockSpec((None, tm, FF_SHARD), lambda j, i: (j, i, 0)),
                  pl.BlockSpec((None, FFN_HALO, FF_SHARD), lambda j, i: (j, nxt(i), 0)),
                  pl.BlockSpec((2, None, tm, FF_SHARD), lambda j, i: (0, j, i, 0)),
                  pl.BlockSpec((None, None, FFN_HALO, FF_SHARD), lambda j, i: (0, j, jnp.maximum(i * nh - 1, 0), 0)),
                  pl.BlockSpec((None, None, FFN_HALO, FF_SHARD), lambda j, i: (1, j, nxt(i), 0)),
                  pl.BlockSpec((2, None, tm, FF_SHARD), lambda j, i: (0, j, i, 0)),
                  pl.BlockSpec((None, None, BF16_ROWS, FF_SHARD), lambda j, i: (1, j, nxt_bf(i), 0)),
                  pl.BlockSpec((None, FFN_K, FF_SHARD), lambda j, i: (j, 0, 0))],
        out_specs=[pl.BlockSpec((2, None, tm, FF_SHARD), lambda j, i: (0, j, i, 0)),
                   pl.BlockSpec((None, 8, FF_SHARD), lambda j, i: (j, 0, 0))],
        out_shape=[jax.ShapeDtypeStruct((2, 4, t, FF_SHARD), BF16), jax.ShapeDtypeStruct((4, 8, FF_SHARD), F32)],
        scratch_shapes=[pltpu.VMEM((FFN_HALO + tm, FF_SHARD), F32), pltpu.VMEM((tm + FFN_HALO, FF_SHARD), F32),
                        pltpu.VMEM((FFN_K + 1, SUBLANES, FF_SHARD), F32)],
        compiler_params=_params(("parallel", "arbitrary")),
    )(d_act, d_act, z, z, z, gd, gd, w_dw)


_HGRN_COLS = 4 * HGRN_DIM


def _to_backward_order(w):
    heads = w[:, 2 * CONV_DIM:2 * CONV_DIM + _HGRN_COLS].reshape(-1, 4, HEADS, HEAD_DIM)
    heads = jnp.swapaxes(heads, 1, 2).reshape(-1, _HGRN_COLS)
    return jnp.concatenate([w[:, 2 * CONV_DIM + _HGRN_COLS:], w[:, :2 * CONV_DIM], heads], axis=1)


def _from_backward_order(w):
    heads = w[:, 2 * D_MODEL + 2 * CONV_DIM:].reshape(-1, HEADS, 4, HEAD_DIM)
    heads = jnp.swapaxes(heads, 1, 2).reshape(-1, _HGRN_COLS)
    return jnp.concatenate([w[:, 2 * D_MODEL:2 * D_MODEL + 2 * CONV_DIM], heads, w[:, :2 * D_MODEL]], axis=1)
def _local_step(x, target, weights, small, scatter=None, order=None):
    t = x.shape[0]
    tm = _pick(t, 2048)
    tk = _pick(t, 2048)
    nm = t // tm
    nk = t // tk
    d = D_MODEL

    xb = _cast_bf16(x)
    if isinstance(weights, _Hosted):
        proj, weights = _proj_gather(xb, weights, order)
    else:
        proj = _mm("proj", xb, weights[0], (t, IN_COLS), F32, (nm, N_DEV, 1),
                   pl.BlockSpec((tm, d), lambda i, j, k: (i, 0)),
                   pl.BlockSpec((None, d, IN_SHARD), lambda i, j, k: (j, 0, 0)),
                   pl.BlockSpec((tm, IN_SHARD), lambda i, j, k: (i, j)), NN, (tm, IN_SHARD))
    w_in, w_conv_out8, w_hgrn_out8, w_out8, w_ffn_in, w_ffn_out8, conv_dw8, ffn_dw8 = weights
    w_conv_out = _relayout("w_conv_out_natural", w_conv_out8, (None, CONV_DIM, 128), lambda j: (j, 0, 0),
                           (CONV_DIM, 128), lambda j: (0, j), jax.ShapeDtypeStruct((CONV_DIM, d), BF16))
    w_hgrn_out = w_hgrn_out8.reshape(d, d)
    w_out = w_out8.reshape(d, d)
    w_ffn_out = w_ffn_out8.reshape(4, FF_SHARD, d)
    conv_dw = jnp.transpose(conv_dw8[:, :CONV_K, :CONV_DIM // N_DEV], (1, 0, 2)).reshape(CONV_K, CONV_DIM)
    ffn_dw = jnp.transpose(ffn_dw8[:, :FFN_K, :D_FF // N_DEV], (1, 0, 2)).reshape(FFN_K, 4, FF_SHARD)
    small = dict(small, w_conv_dw=conv_dw, w_ffn_dw=jnp.transpose(ffn_dw, (1, 0, 2)),
                 b_ffn_dw=small["b_ffn_dw"].reshape(4, 1, FF_SHARD))

    c_act, conv_pre = _conv_fwd(proj, small["w_conv_dw"], small["b_conv_dw"], small["conv_ln_g"], small["conv_ln_b"])
    y_conv = _mm("y_conv", c_act, w_conv_out, (t, d), F32, (nm, 1, 1),
                 pl.BlockSpec((tm, CONV_DIM), lambda i, j, k: (i, 0)),
                 pl.BlockSpec((CONV_DIM, d), lambda i, j, k: (0, 0)),
                 pl.BlockSpec((tm, d), lambda i, j, k: (i, 0)), NN, (tm, d))
    o, og, states = _hgrn_fwd(proj, small["hgrn_lb_logits"], small["hgrn_norm_g"])
    sq_w = pl.BlockSpec((d, d), lambda i, j, k: (0, 0))
    row_tile = pl.BlockSpec((tm, d), lambda i, j, k: (i, 0))
    y_hgrn = _mm("y_hgrn", og, w_hgrn_out, (t, d), F32, (nm, 1, 1), row_tile, sq_w, row_tile, NN, (tm, d))
    mixed, r1, x1b = _mix_ln1(proj, y_conv, y_hgrn, w_out, x, small["ln1_g"], small["ln1_b"])
    z = _mm("ffn_in", x1b, w_ffn_in, (N_DEV, t, FF_SHARD), F32, (nm, N_DEV, 1), row_tile,
            pl.BlockSpec((None, d, FF_SHARD), lambda i, j, k: (j, 0, 0)),
            pl.BlockSpec((None, tm, FF_SHARD), lambda i, j, k: (j, i, 0)), NN, (tm, FF_SHARD))
    z = z.reshape(2, 4, t, FF_SHARD)
    act, gelu_and_slope = _ffn_act_fwd(z, small["w_ffn_dw"], small["b_ffn_dw"])

    d_r2, d_r2b, sums_ln2 = _ffn_out_ln2(act, w_ffn_out, r1, target, small["ln1_g"], small["ln1_b"],
                                         small["ln2_g"], small["ln2_b"])
    d_act = _mm("d_act", d_r2b, w_ffn_out, (4, t, FF_SHARD), F32, (nm, 4, 1), row_tile,
                pl.BlockSpec((None, FF_SHARD, d), lambda i, j, k: (j, 0, 0)),
                pl.BlockSpec((None, tm, FF_SHARD), lambda i, j, k: (j, i, 0)), NT, (tm, FF_SHARD))
    g_w_ffn_out = _mm("g_w_ffn_out", act, d_r2b, (4, FF_SHARD, d), BF16, (4, 1, nk),
                      pl.BlockSpec((None, tk, FF_SHARD), lambda i, j, k: (i, k, 0)),
                      pl.BlockSpec((tk, d), lambda i, j, k: (k, 0)),
                      pl.BlockSpec((None, FF_SHARD, d), lambda i, j, k: (i, 0, 0)), TN, (FF_SHARD, d))
    d_z, sums_ffn = _ffn_act_bwd(d_act, z, gelu_and_slope, small["w_ffn_dw"])
    d_z8 = d_z.reshape(N_DEV, t, FF_SHARD)
    d_r1, d_r1b, sums_ln1 = _d_x1_ln1_bwd(d_z8, w_ffn_in, d_r2, r1, small["ln1_g"])
    g_w_ffn_in = _mm("g_w_ffn_in", x1b, d_z8, (N_DEV, d, FF_SHARD), BF16, (N_DEV, 1, nk),
                     pl.BlockSpec((tk, d), lambda i, j, k: (k, 0)),
                     pl.BlockSpec((None, tk, FF_SHARD), lambda i, j, k: (i, k, 0)),
                     pl.BlockSpec((None, d, FF_SHARD), lambda i, j, k: (i, 0, 0)), TN, (d, FF_SHARD))
    k_tile = pl.BlockSpec((tk, d), lambda i, j, k: (k, 0))
    g_w_out = _mm("g_w_out", mixed, d_r1b, (d, d), BF16, (1, 1, nk), k_tile, k_tile, sq_w, TN, (d, d))
    d_y, d_proj = _d_mixed_merge_bwd(d_r1b, w_out, proj, y_conv, y_hgrn)
    d_pre, sums_conv = _d_c_norm_bwd(d_y, w_conv_out, conv_pre, small["conv_ln_g"], small["conv_ln_b"])
    g_w_conv_out = _mm("g_w_conv_out", c_act, d_y, (CONV_DIM, d), BF16, (1, 1, nk),
                       pl.BlockSpec((tk, CONV_DIM), lambda i, j, k: (k, 0)),
                       pl.BlockSpec((None, tk, d), lambda i, j, k: (0, k, 0)),
                       pl.BlockSpec((CONV_DIM, d), lambda i, j, k: (0, 0)), TN, (CONV_DIM, d))
    g_w_conv_out = _relayout("g_w_conv_out_shards", g_w_conv_out, (CONV_DIM, 128), lambda j: (0, j),
                             (None, CONV_DIM, 128), lambda j: (j, 0, 0),
                             jax.ShapeDtypeStruct((N_DEV, CONV_DIM, 128), BF16))
    d_og = _mm("d_og", d_y, w_hgrn_out, (t, d), F32, (nm, 1, 1),
               pl.BlockSpec((None, tm, d), lambda i, j, k: (1, i, 0)), sq_w, row_tile, NT, (tm, d))
    g_w_hgrn_out = _mm("g_w_hgrn_out", og, d_y, (d, d), BF16, (1, 1, nk), k_tile,
                       pl.BlockSpec((None, tk, d), lambda i, j, k: (1, k, 0)), sq_w, TN, (d, d))
    d_proj, g_w_conv_dw = _conv_bwd_dw(d_pre, proj, small["w_conv_dw"], d_proj)

    early = [g_w_conv_out, g_w_hgrn_out.reshape(N_DEV, d // N_DEV, d), g_w_out.reshape(N_DEV, d // N_DEV, d),
             g_w_ffn_in, g_w_ffn_out.reshape(N_DEV, D_FF // N_DEV, d)]
    d_proj, sums_hgrn, *early_recv = _hgrn_bwd(d_og, o, proj, states, small["hgrn_lb_logits"], small["hgrn_norm_g"],
                                               d_proj, hosted=None if scatter is None else scatter(early))
    wide = IN_COLS // 4
    w_in_bwd = _to_backward_order(jnp.transpose(w_in, (1, 0, 2)).reshape(d, IN_COLS))
    g_w_in = _mm("g_w_in", xb, d_proj, (d, IN_COLS), BF16, (4, 1, nk), k_tile,
                 pl.BlockSpec((tk, wide), lambda i, j, k: (k, i)),
                 pl.BlockSpec((d, wide), lambda i, j, k: (0, i)), TN, (d, wide))
    g_w_in = jnp.transpose(_from_backward_order(g_w_in).reshape(d, N_DEV, IN_SHARD), (1, 0, 2))
    def add_residual(acc, ins, outs, scr):
        for rows in _row_blocks(ta):
            outs[0][rows, :] = ALPHA * ins[2][rows, :] + acc[rows, :]

    ta = _pick(t, 1024)
    acc_tile = pl.BlockSpec((ta, d), lambda i, j, k: (i, 0))
    grad_x, *late_recv = _mm_fused(
        "grad_x", (t // ta, 1, 4), (d_proj, w_in_bwd, d_r1),
        [pl.BlockSpec((ta, wide), lambda i, j, k: (i, k)), pl.BlockSpec((d, wide), lambda i, j, k: (0, k)), acc_tile],
        [jax.ShapeDtypeStruct((t, d), F32)], [acc_tile], NT, (ta, d), add_residual,
        hosted=None if scatter is None else scatter([g_w_in]))

    large_grads = [g_w_in] + early
    if scatter is not None:
        large_grads = list(zip(large_grads, late_recv + early_recv))
    return grad_x, large_grads, (sums_ln2, sums_conv, sums_hgrn, sums_ln1, sums_ffn, g_w_conv_dw)


def _small_views(sums):
    sums_ln2, sums_conv, sums_hgrn, sums_ln1, sums_ffn, g_w_conv_dw = sums
    d_l0 = sums_hgrn[1:2]
    return {
        "loss": sums_ln2[2:3, 0:128],
        "b_conv_dw": sums_conv[2:3], "conv_ln_g": sums_conv[0:1], "conv_ln_b": sums_conv[1:2],
        "hgrn_lb_logits": jnp.concatenate([d_l0, -d_l0], axis=1),
        "hgrn_norm_g": sums_hgrn[0:1],
        "ln1_g": sums_ln1[0:1], "ln1_b": sums_ln1[1:2],
        "b_ffn_dw": sums_ffn[:, FFN_K, :].reshape(1, D_FF),
        "ln2_g": sums_ln2[0:1], "ln2_b": sums_ln2[1:2],
        "w_conv_dw": g_w_conv_dw[0:CONV_K].reshape(1, CONV_K * CONV_DIM),
        "w_ffn_dw": jnp.transpose(sums_ffn[:, 0:FFN_K, :], (1, 0, 2)).reshape(1, FFN_K * D_FF),
    }


def _coords():
    return lax.axis_index("x"), lax.axis_index("y"), lax.axis_index("c")


def _gather(shards, staged=False):
    n = len(shards)
    later = range(1 if staged else 0, n)

    def parts(ins, outs, sems):
        send_sems, recv_sems, local_sems = sems
        x, y, c = _coords()
        me = 4 * x + 2 * y + c
        sibling = (x, y, 1 - c)
        chips = [(1 - x, y), (x, 1 - y), (1 - x, 1 - y)]

        def copy(a, k, block, to, src=None):
            return pltpu.make_async_remote_copy(
                src_ref=outs[a].at[block] if src is None else src, dst_ref=outs[a].at[block],
                send_sem=send_sems.at[a, k], recv_sem=recv_sems.at[a, k], device_id=to, device_id_type=MESH)

        local = [pltpu.make_async_copy(ins[a], outs[a].at[me], local_sems.at[a]) for a in range(n)]
        first = []
        for a in range(n):
            first.append(copy(a, 0, me, sibling, src=ins[a]))
            for j, chip in enumerate(chips):
                first.append(copy(a, 1 + j, me, (*chip, c), src=ins[a]))
        return x, y, c, sibling, chips, copy, local, first

    def start(ins, outs, sems):
        *_, local, first = parts(ins, outs, sems)
        for cp in local + first:
            cp.start()

    def arrive(ins, outs, sems, s):
        x, y, c, sibling, chips, copy, _, _ = parts(ins, outs, sems)
        if s == 1:
            block = 4 * x + 2 * y + 1 - c
            copy(0, 0, block, sibling).wait_recv()
        elif s <= 4:
            px, py = chips[s - 2]
            block = 4 * px + 2 * py + c
            copy(0, s - 1, block, sibling).wait_recv()
            copy(0, s + 2, block, sibling).start()
        else:
            px, py = chips[s - 5]
            block = 4 * px + 2 * py + 1 - c
            copy(0, s - 1, block, sibling).wait_recv()
        return block

    def finish(ins, outs, sems):
        x, y, c, sibling, chips, copy, local, first = parts(ins, outs, sems)
        passed = [copy(0, 4 + j, 4 * px + 2 * py + c, sibling) for j, (px, py) in enumerate(chips)] if staged else []
        for j, (px, py) in enumerate(chips):
            for a in later:
                copy(a, 1 + j, 4 * px + 2 * py + c, sibling).wait_recv()
                cp = copy(a, 4 + j, 4 * px + 2 * py + c, sibling)
                cp.start()
                passed.append(cp)
        for a in later:
            copy(a, 0, 4 * x + 2 * y + 1 - c, sibling).wait_recv()
            for j, (px, py) in enumerate(chips):
                copy(a, 4 + j, 4 * px + 2 * py + 1 - c, sibling).wait_recv()
        for cp in first + passed:
            cp.wait_send()
        for cp in local:
            cp.wait()

    hosted = _Hosted(shards, [jax.ShapeDtypeStruct((N_DEV,) + s.shape, s.dtype) for s in shards],
                     [pltpu.SemaphoreType.DMA((n, 7)), pltpu.SemaphoreType.DMA((n, 7)), pltpu.SemaphoreType.DMA((n,))],
                     start, finish)
    hosted.arrive = arrive
    return hosted


def _proj_gather(xb, gather, order):
    t, d = xb.shape
    tm = _pick(t, 2048)
    nm = t // tm
    n_in, n_out = len(gather.inputs), len(gather.out_shapes)

    def body(order_ref, x_ref, *refs):
        ins, refs = refs[:n_in], refs[n_in:]
        o_ref, outs, refs = refs[0], refs[1:1 + n_out], refs[1 + n_out:]
        w_buf, w_sem, sems = refs[0], refs[1], refs[2:]
        s, i = pl.program_id(0), pl.program_id(1)

        @pl.when((s == 0) & (i == 0))
        def _():
            gather.start(ins, outs, sems)

        def staging(step, src):
            return pltpu.make_async_copy(src, w_buf.at[step % 2], w_sem.at[step % 2])

        @pl.when((s == 0) & (i == 0))
        def _():
            staging(0, ins[0]).start()

        for step in range(1, N_DEV):
            @pl.when((s == step - 1) & (i == nm - 1))
            def _(step=step):
                staging(step, outs[0].at[gather.arrive(ins, outs, sems, step)]).start()

        for step in range(N_DEV):
            @pl.when((s == step) & (i == 0))
            def _(step=step):
                staging(step, ins[0]).wait()

        o_ref[...] = _dot(x_ref[...], w_buf[s % 2], NN)

        @pl.when((s == N_DEV - 1) & (i == nm - 1))
        def _():
            gather.finish(ins, outs, sems)

    outs = pl.pallas_call(
        body, name="proj_gather",
        grid_spec=pltpu.PrefetchScalarGridSpec(
            num_scalar_prefetch=1, grid=(N_DEV, nm),
            in_specs=[pl.BlockSpec((tm, d), lambda s, i, order_ref: (i, 0))] + [ANY] * n_in,
            out_specs=[pl.BlockSpec((tm, IN_SHARD), lambda s, i, order_ref: (i, order_ref[s]))] + [ANY] * n_out,
            scratch_shapes=[pltpu.VMEM((2, d, IN_SHARD), BF16), pltpu.SemaphoreType.DMA((2,))] + gather.sem_shapes),
        out_shape=[jax.ShapeDtypeStruct((t, IN_COLS), F32)] + gather.out_shapes,
        compiler_params=pltpu.CompilerParams(dimension_semantics=("arbitrary", "arbitrary"),
                                             vmem_limit_bytes=VMEM_LIMIT, has_side_effects=True),
    )(order, xb, *gather.inputs)
    return outs[0], list(outs[1:])


def _scatter(grads):
    n = len(grads)

    def copies(ins, outs, sems):
        send_sems, recv_sems = sems
        x, y, c = _coords()
        out = []
        for a in range(n):
            for k in range(1, N_DEV):
                px, py, pc = x ^ (k >> 2), y ^ ((k >> 1) & 1), c ^ (k & 1)
                out.append(pltpu.make_async_remote_copy(
                    src_ref=ins[a].at[4 * px + 2 * py + pc], dst_ref=outs[a].at[k - 1],
                    send_sem=send_sems.at[a, k - 1], recv_sem=recv_sems.at[a, k - 1],
                    device_id=(px, py, pc), device_id_type=MESH))
        return out

    def start(ins, outs, sems):
        for cp in copies(ins, outs, sems):
            cp.start()

    def finish(ins, outs, sems):
        for cp in copies(ins, outs, sems):
            cp.wait()

    return _Hosted(grads, [jax.ShapeDtypeStruct((N_DEV - 1,) + g.shape[1:], g.dtype) for g in grads],
                   [pltpu.SemaphoreType.DMA((n, N_DEV - 1)), pltpu.SemaphoreType.DMA((n, N_DEV - 1))], start, finish)


def _row_tile(rows):
    return 256 if rows % 256 == 0 else rows


def _adam_math(w, g, m, v):
    m_new = ADAM_B1 * m + (1.0 - ADAM_B1) * g
    v_new = ADAM_B2 * v + (1.0 - ADAM_B2) * (g * g)
    m_hat = m_new / (1.0 - ADAM_B1 ** ADAM_STEP)
    v_hat = v_new / (1.0 - ADAM_B2 ** ADAM_STEP)
    delta = -ADAM_LR * (m_hat / (jnp.sqrt(v_hat) + ADAM_EPS) + ADAM_WD * w)
    return delta, m_new, v_new


def _adam_large(name, own, recv, me, w, m, v):
    rows, cols = w.shape
    tr = _row_tile(rows)

    def body(me_ref, p_ref, r_ref, w_ref, m_ref, v_ref, g_out, d_out, m_out, v_out):
        g = p_ref[...].astype(F32)
        for k in range(N_DEV - 1):
            g = g + r_ref[k].astype(F32)
        delta, m_new, v_new = _adam_math(w_ref[...], g, m_ref[...], v_ref[...])
        g_out[...] = g
        d_out[...] = delta
        m_out[...] = m_new
        v_out[...] = v_new

    tile = pl.BlockSpec((tr, cols), lambda r, me_ref: (r, 0))
    sds = jax.ShapeDtypeStruct((rows, cols), F32)
    return pl.pallas_call(
        body, name=name,
        grid_spec=pltpu.PrefetchScalarGridSpec(
            num_scalar_prefetch=1, grid=(rows // tr,),
            in_specs=[pl.BlockSpec((None, tr, cols), lambda r, me_ref: (me_ref[0], r, 0)),
                      pl.BlockSpec((N_DEV - 1, tr, cols), lambda r, me_ref: (0, r, 0)), tile, tile, tile],
            out_specs=[tile, tile, tile, tile]),
        out_shape=[sds, sds, sds, sds],
        compiler_params=_params(("parallel",)),
    )(me, own, recv, w, m, v)


def _small_allreduce(arrays):
    n = len(arrays)

    def body(*refs):
        ins, outs, gats = refs[:n], refs[n:2 * n], refs[2 * n:3 * n]
        send_sems, recv_sems = refs[3 * n:]
        x, y, c = _coords()
        me = 4 * x + 2 * y + c
        peers = [(x ^ (k >> 2), y ^ ((k >> 1) & 1), c ^ (k & 1)) for k in range(1, N_DEV)]

        def copy(a, k, slot):
            return pltpu.make_async_remote_copy(
                src_ref=ins[a], dst_ref=gats[a].at[slot], send_sem=send_sems.at[a, k], recv_sem=recv_sems.at[a, k],
                device_id=peers[k], device_id_type=MESH)

        sends = [copy(a, k, me) for a in range(n) for k in range(N_DEV - 1)]
        for a in range(n):
            gats[a][me] = ins[a][...]
        for cp in sends:
            cp.start()
        for a in range(n):
            for k, (px, py, pc) in enumerate(peers):
                copy(a, k, 4 * px + 2 * py + pc).wait_recv()
        for cp in sends:
            cp.wait_send()
        for a in range(n):
            acc = gats[a][0]
            for dev in range(1, N_DEV):
                acc = acc + gats[a][dev]
            outs[a][...] = acc

    whole = pl.BlockSpec(memory_space=pltpu.VMEM)
    return pl.pallas_call(
        body, name="small_allreduce", in_specs=[whole] * n, out_specs=[whole] * n,
        out_shape=[jax.ShapeDtypeStruct(a.shape, F32) for a in arrays],
        scratch_shapes=[pltpu.VMEM((N_DEV,) + a.shape, F32) for a in arrays]
        + [pltpu.SemaphoreType.DMA((n, N_DEV - 1)), pltpu.SemaphoreType.DMA((n, N_DEV - 1))],
        compiler_params=pltpu.CompilerParams(has_side_effects=True, vmem_limit_bytes=VMEM_LIMIT),
    )(*arrays)


def _adam_replicated(sums, w, m, v):
    rows_of = {"conv_ln_g": (1, 0), "conv_ln_b": (1, 1), "b_conv_dw": (1, 2), "hgrn_norm_g": (2, 0),
               "ln1_g": (3, 0), "ln1_b": (3, 1), "ln2_g": (0, 0), "ln2_b": (0, 1)}
    names = list(rows_of) + ["hgrn_lb_logits"]
    n = len(names)

    def body(*refs):
        sum_refs, refs = refs[:4], refs[4:]
        w_refs, m_refs, v_refs, outs = refs[:n], refs[n:2 * n], refs[2 * n:3 * n], refs[3 * n:]
        for j, name in enumerate(names):
            if name == "hgrn_lb_logits":
                d_l0 = sum_refs[2][1:2, :]
                grads = [d_l0, -d_l0]
            else:
                a, row = rows_of[name]
                grads = [sum_refs[a][row:row + 1, :]]
            g_out, d_out, m_out, v_out = outs[4 * j:4 * j + 4]
            for r, g in enumerate(grads):
                rows = slice(r, r + 1)
                delta, m_new, v_new = _adam_math(w_refs[j][rows, :], g, m_refs[j][rows, :], v_refs[j][rows, :])
                g_out[rows, :] = g
                d_out[rows, :] = delta
                m_out[rows, :] = m_new
                v_out[rows, :] = v_new

    whole = pl.BlockSpec(memory_space=pltpu.VMEM)
    operands = list(sums) + [w[k] for k in names] + [m[k] for k in names] + [v[k] for k in names]
    outs = pl.pallas_call(
        body, name="adam_replicated", in_specs=[whole] * len(operands), out_specs=[whole] * (4 * n),
        out_shape=[jax.ShapeDtypeStruct(w[k].shape, F32) for k in names for _ in range(4)],
    )(*operands)
    return {name: tuple(outs[4 * j:4 * j + 4]) for j, name in enumerate(names)}


def _adam_small(w, g, m, v):
    def body(w_ref, g_ref, m_ref, v_ref, d_out, m_out, v_out):
        delta, m_new, v_new = _adam_math(w_ref[...], g_ref[...], m_ref[...], v_ref[...])
        d_out[...] = delta
        m_out[...] = m_new
        v_out[...] = v_new

    whole = pl.BlockSpec(memory_space=pltpu.VMEM)
    sds = jax.ShapeDtypeStruct(w.shape, F32)
    return pl.pallas_call(body, name="adam_small", in_specs=[whole] * 4, out_specs=[whole] * 3,
                          out_shape=[sds, sds, sds])(w, g, m, v)


_WEIGHTS = ["w_in", "w_conv_dw", "b_conv_dw", "conv_ln_g", "conv_ln_b", "w_conv_out", "hgrn_lb_logits", "hgrn_norm_g",
            "w_hgrn_out", "w_out", "ln1_g", "ln1_b", "w_ffn_in", "w_ffn_dw", "b_ffn_dw", "w_ffn_out", "ln2_g", "ln2_b"]
_LARGE = ["w_in", "w_conv_out", "w_hgrn_out", "w_out", "w_ffn_in", "w_ffn_out"]
_CONV_DW_SHARD = CONV_DIM // N_DEV
_FFN_DW_SHARD = D_FF // N_DEV


def kernel(x, w_in, w_conv_dw, b_conv_dw, conv_ln_g, conv_ln_b, w_conv_out, hgrn_lb_logits, hgrn_norm_g, w_hgrn_out, w_out, ln1_g, ln1_b, w_ffn_in, w_ffn_dw, b_ffn_dw, w_ffn_out, ln2_g, ln2_b, loss_target, m_w_in, m_w_conv_dw, m_b_conv_dw, m_conv_ln_g, m_conv_ln_b, m_w_conv_out, m_hgrn_lb_logits, m_hgrn_norm_g, m_w_hgrn_out, m_w_out, m_ln1_g, m_ln1_b, m_w_ffn_in, m_w_ffn_dw, m_b_ffn_dw, m_w_ffn_out, m_ln2_g, m_ln2_b, v_w_in, v_w_conv_dw, v_b_conv_dw, v_conv_ln_g, v_conv_ln_b, v_w_conv_out, v_hgrn_lb_logits, v_hgrn_norm_g, v_w_hgrn_out, v_w_out, v_ln1_g, v_ln1_b, v_w_ffn_in, v_w_ffn_dw, v_b_ffn_dw, v_w_ffn_out, v_ln2_g, v_ln2_b):
    w = dict(w_in=w_in, w_conv_dw=w_conv_dw, b_conv_dw=b_conv_dw, conv_ln_g=conv_ln_g, conv_ln_b=conv_ln_b,
             w_conv_out=w_conv_out, hgrn_lb_logits=hgrn_lb_logits, hgrn_norm_g=hgrn_norm_g, w_hgrn_out=w_hgrn_out,
             w_out=w_out, ln1_g=ln1_g, ln1_b=ln1_b, w_ffn_in=w_ffn_in, w_ffn_dw=w_ffn_dw, b_ffn_dw=b_ffn_dw,
             w_ffn_out=w_ffn_out, ln2_g=ln2_g, ln2_b=ln2_b)
    m = dict(w_in=m_w_in, w_conv_dw=m_w_conv_dw, b_conv_dw=m_b_conv_dw, conv_ln_g=m_conv_ln_g, conv_ln_b=m_conv_ln_b,
             w_conv_out=m_w_conv_out, hgrn_lb_logits=m_hgrn_lb_logits, hgrn_norm_g=m_hgrn_norm_g,
             w_hgrn_out=m_w_hgrn_out, w_out=m_w_out, ln1_g=m_ln1_g, ln1_b=m_ln1_b, w_ffn_in=m_w_ffn_in,
             w_ffn_dw=m_w_ffn_dw, b_ffn_dw=m_b_ffn_dw, w_ffn_out=m_w_ffn_out, ln2_g=m_ln2_g, ln2_b=m_ln2_b)
    v = dict(w_in=v_w_in, w_conv_dw=v_w_conv_dw, b_conv_dw=v_b_conv_dw, conv_ln_g=v_conv_ln_g, conv_ln_b=v_conv_ln_b,
             w_conv_out=v_w_conv_out, hgrn_lb_logits=v_hgrn_lb_logits, hgrn_norm_g=v_hgrn_norm_g,
             w_hgrn_out=v_w_hgrn_out, w_out=v_w_out, ln1_g=v_ln1_g, ln1_b=v_ln1_b, w_ffn_in=v_w_ffn_in,
             w_ffn_dw=v_w_ffn_dw, b_ffn_dw=v_b_ffn_dw, w_ffn_out=v_w_ffn_out, ln2_g=v_ln2_g, ln2_b=v_ln2_b)
    xi, yi, ci = lax.axis_index("x"), lax.axis_index("y"), lax.axis_index("c")
    me = 4 * xi + 2 * yi + ci
    me_op = jnp.reshape(me, (1,)).astype(jnp.int32)

    shards = [w[name][0].astype(BF16) for name in _LARGE]
    shards.append(jnp.pad(w_conv_dw[0], ((0, 1), (0, 128 - _CONV_DW_SHARD))))
    shards.append(jnp.pad(w_ffn_dw[0], ((0, 8 - FFN_K), (0, 384 - _FFN_DW_SHARD))))
    chips = [(1 - xi, yi), (xi, 1 - yi), (1 - xi, 1 - yi)]
    order = jnp.stack([me, me ^ 1] + [4 * px + 2 * py + ci for px, py in chips]
                      + [4 * px + 2 * py + 1 - ci for px, py in chips]).astype(jnp.int32)
    small = dict(b_conv_dw=b_conv_dw, conv_ln_g=conv_ln_g, conv_ln_b=conv_ln_b, hgrn_lb_logits=hgrn_lb_logits,
                 hgrn_norm_g=hgrn_norm_g, ln1_g=ln1_g, ln1_b=ln1_b, ln2_g=ln2_g, ln2_b=ln2_b, b_ffn_dw=b_ffn_dw)

    grad_x, large_grads, small_sums = _local_step(x[0], loss_target[0], _gather(shards, staged=True), small,
                                                   _scatter, order)

    out = {}
    for name, (own, recv) in zip(_LARGE, large_grads):
        out[name] = _adam_large("adam_" + name, own, recv, me_op, w[name][0], m[name][0], v[name][0])

    totals = _small_allreduce(list(small_sums))
    out.update(_adam_replicated(totals[:4], w, m, v))
    summed = _small_views(totals)
    loss = summed["loss"][0, 0]
    conv_dw_g = lax.dynamic_slice_in_dim(summed["w_conv_dw"].reshape(CONV_K, CONV_DIM), me * _CONV_DW_SHARD, _CONV_DW_SHARD, axis=1)
    ffn_dw_g = lax.dynamic_slice_in_dim(summed["w_ffn_dw"].reshape(FFN_K, D_FF), me * _FFN_DW_SHARD, _FFN_DW_SHARD, axis=1)
    small_g = dict(b_ffn_dw=summed["b_ffn_dw"], w_conv_dw=conv_dw_g.reshape(1, -1), w_ffn_dw=ffn_dw_g.reshape(1, -1))
    names = list(small_g)
    flat = lambda d, n: d[n].reshape(1, -1)
    n_small = sum(small_g[n].shape[1] for n in names)
    pad = (-n_small) % 1024
    pack = lambda pieces: jnp.pad(jnp.concatenate(pieces, axis=1), ((0, 0), (0, pad))).reshape(-1, 128)
    d_s, m_s, v_s = _adam_small(pack([flat(w, n) for n in names]), pack([small_g[n] for n in names]),
                                pack([flat(m, n) for n in names]), pack([flat(v, n) for n in names]))
    pos = 0
    for n in names:
        size = small_g[n].shape[1]
        cut = lambda a: a.reshape(1, -1)[:, pos:pos + size].reshape(w[n].shape)
        out[n] = (small_g[n].reshape(w[n].shape), cut(d_s), cut(m_s), cut(v_s))
        pos += size

    for name in _LARGE:
        out[name] = tuple(a.reshape(w[name].shape) for a in out[name])
    grads = [out[n][0] for n in _WEIGHTS]
    deltas = [out[n][1] for n in _WEIGHTS]
    new_m = [out[n][2] for n in _WEIGHTS]
    new_v = [out[n][3] for n in _WEIGHTS]
    return (loss, grad_x[None], *grads, *deltas, *new_m, *new_v)
```

```python
import functools
import math

import jax
import jax.numpy as jnp
from jax import lax
from jax.experimental import pallas as pl
from jax.experimental.pallas import tpu as pltpu

F32 = jnp.float32
BF16 = jnp.bfloat16

N_DEV = 8
D_MODEL = 1024
CONV_DIM = 512
CONV_K = 31
HGRN_DIM = 1024
HEADS = 8
HEAD_DIM = 128
D_FF = 2816
FFN_K = 3
FF_SHARD = 2 * D_FF // N_DEV
IN_COLS = 7168
IN_SHARD = IN_COLS // N_DEV
LN_EPS = 1e-5
RMS_EPS = 1e-6
ALPHA = 2.0 ** 0.25

ADAM_LR = 0.001
ADAM_B1 = 0.9
ADAM_B2 = 0.999
ADAM_EPS = 1e-08
ADAM_WD = 0.01
ADAM_STEP = 10

CHUNK = 64
CHUNKS_PER_BLOCK = 32
CONV_HALO = 32
FFN_HALO = 8
ROW_BLOCK = 64
SUBLANES = 8
VMEM_LIMIT = 48 * 1024 * 1024
MXU_DEPTH = 256

DP_MERGE_BLOCK = 0
DP_CONV_BLOCK = 2
DP_HEAD_BLOCK = 6

MESH = pl.DeviceIdType.MESH
ANY = pl.BlockSpec(memory_space=pl.ANY)

NN = (((1,), (0,)), ((), ()))
NT = (((1,), (1,)), ((), ()))
TN = (((0,), (0,)), ((), ()))


def _params(sem):
    return pltpu.CompilerParams(dimension_semantics=sem, vmem_limit_bytes=VMEM_LIMIT)


def _dot(a, b, dims):
    return lax.dot_general(a.astype(BF16), b.astype(BF16), dims, preferred_element_type=F32)


def _sigmoid(x):
    return jax.nn.sigmoid(x)


def _ln(r):
    mu = jnp.mean(r, axis=-1, keepdims=True)
    xc = r - mu
    var = jnp.mean(xc * xc, axis=-1, keepdims=True)
    rstd = lax.rsqrt(var + LN_EPS)
    return xc * rstd, rstd


def _ln_bwd(dy, xhat, rstd, g):
    dxh = dy * g
    m1 = jnp.mean(dxh, axis=-1, keepdims=True)
    m2 = jnp.mean(dxh * xhat, axis=-1, keepdims=True)
    return rstd * (dxh - m1 - xhat * m2)


def _colsum(x):
    return jnp.sum(x, axis=0, keepdims=True)


class _Hosted:
    def __init__(self, inputs, out_shapes, sem_shapes, start, finish):
        self.inputs, self.out_shapes, self.sem_shapes = list(inputs), list(out_shapes), list(sem_shapes)
        self.start, self.finish = start, finish


def _call(body, *, name, grid, in_specs, out_specs, out_shape, scratch_shapes, semantics, operands, hosted=None,
          aliases=None):
    aliases = aliases or {}
    if hosted is None:
        return pl.pallas_call(
            body, name=name, grid=grid, in_specs=list(in_specs), out_specs=list(out_specs), out_shape=list(out_shape),
            scratch_shapes=list(scratch_shapes), input_output_aliases=aliases,
            compiler_params=_params(semantics))(*operands)
    n_in, n_out, n_scr = len(in_specs), len(out_specs), len(scratch_shapes)
    h_in, h_out = len(hosted.inputs), len(hosted.out_shapes)

    def full_body(*refs):
        ins, refs = refs[:n_in], refs[n_in:]
        h_ins, refs = refs[:h_in], refs[h_in:]
        outs, refs = refs[:n_out], refs[n_out:]
        h_outs, refs = refs[:h_out], refs[h_out:]
        scr, sems = refs[:n_scr], refs[n_scr:]
        first = functools.reduce(jnp.logical_and, [pl.program_id(d) == 0 for d in range(len(grid))])
        last = functools.reduce(jnp.logical_and, [pl.program_id(d) == grid[d] - 1 for d in range(len(grid))])

        @pl.when(first)
        def _():
            hosted.start(h_ins, h_outs, sems)

        body(*ins, *outs, *scr)

        @pl.when(last)
        def _():
            hosted.finish(h_ins, h_outs, sems)

    return pl.pallas_call(
        full_body, name=name, grid=grid, in_specs=list(in_specs) + [ANY] * h_in,
        out_specs=list(out_specs) + [ANY] * h_out, out_shape=list(out_shape) + hosted.out_shapes,
        scratch_shapes=list(scratch_shapes) + hosted.sem_shapes, input_output_aliases=aliases,
        compiler_params=pltpu.CompilerParams(dimension_semantics=("arbitrary",) * len(grid),
                                             vmem_limit_bytes=VMEM_LIMIT, has_side_effects=True),
    )(*operands, *hosted.inputs)


def _mm(name, a, b, out_shape, out_dtype, grid, a_spec, b_spec, o_spec, dims, acc_shape, hosted=None):
    nk = grid[2]
    if nk == 1:
        def body(a_ref, b_ref, o_ref):
            o_ref[...] = _dot(a_ref[...], b_ref[...], dims).astype(o_ref.dtype)
        scratch = []
    else:
        def body(a_ref, b_ref, o_ref, acc_ref):
            k = pl.program_id(2)

            @pl.when(k == 0)
            def _():
                acc_ref[...] = jnp.zeros_like(acc_ref)

            acc_ref[...] += _dot(a_ref[...], b_ref[...], dims)

            @pl.when(k == nk - 1)
            def _():
                o_ref[...] = acc_ref[...].astype(o_ref.dtype)
        scratch = [pltpu.VMEM(acc_shape, F32)]

    outs = _call(body, name=name, grid=grid, in_specs=[a_spec, b_spec], out_specs=[o_spec],
                 out_shape=[jax.ShapeDtypeStruct(out_shape, out_dtype)], scratch_shapes=scratch,
                 semantics=("parallel", "parallel", "arbitrary"), operands=(a, b), hosted=hosted)
    return outs[0] if hosted is None else (outs[0], list(outs[1:]))


def _mm_fused(name, grid, operands, in_specs, out_shape, out_specs, dims, acc_shape, epilogue, lhs=None, scratch=(),
              hosted=None):
    nk = grid[2]
    n_in, n_out = len(in_specs), len(out_specs)

    def body(*refs):
        ins, outs, scr = refs[:n_in], refs[n_in:n_in + n_out], refs[n_in + n_out:]
        acc_ref, k = scr[0], pl.program_id(2)
        a = ins[0][...] if lhs is None else lhs(ins, outs)
        part = _dot(a, ins[1][...], dims)
        if nk == 1:
            acc_ref[...] = part
            epilogue(acc_ref, ins, outs, scr[1:])
            return

        @pl.when(k == 0)
        def _():
            acc_ref[...] = jnp.zeros_like(acc_ref)

        acc_ref[...] += part

        @pl.when(k == nk - 1)
        def _():
            epilogue(acc_ref, ins, outs, scr[1:])

    return _call(body, name=name, grid=grid, in_specs=in_specs, out_specs=out_specs, out_shape=out_shape,
                 scratch_shapes=[pltpu.VMEM(acc_shape, F32)] + list(scratch), semantics=("arbitrary",) * 3,
                 operands=operands, hosted=hosted)


def _row_blocks(rows, block=256):
    block = block if rows % block == 0 else rows
    return [slice(r, r + block) for r in range(0, rows, block)]


def _pick(t, pref):
    return pref if t % pref == 0 else t


def _glu(p):
    return p[:, :CONV_DIM] * _sigmoid(p[:, CONV_DIM:])


def _by_phase(taps):
    phases = {}
    for off, payload in taps:
        phases.setdefault(off % SUBLANES, []).append((off - off % SUBLANES, payload))
    return sorted(phases.items())


def _tap_sum(src_ref, base, taps, rows, lanes):
    acc = None
    for phase, items in _by_phase(taps):
        n = rows if phase == 0 else rows + SUBLANES
        part = None
        for off, (w_ref, k) in items:
            term = w_ref[k:k + 1, lanes] * src_ref[base + off:base + off + n, lanes]
            part = term if part is None else part + term
        if phase:
            part = part[phase:phase + rows, :]
        acc = part if acc is None else acc + part
    return acc


def _tap_products(x, src_ref, base, taps, lanes):
    rows, cols = x.shape
    pad = jnp.zeros((SUBLANES, cols), x.dtype)
    padded = jnp.concatenate([pad, x, pad], axis=0)
    out = []
    for phase, items in _by_phase(taps):
        n = rows if phase == 0 else rows + SUBLANES
        shifted = x if phase == 0 else padded[SUBLANES - phase:SUBLANES - phase + n, :]
        for off, key in items:
            out.append((key, _colsum(shifted * src_ref[base + off:base + off + n, lanes])))
    return out


def _lane_blocks(cols, block=256):
    return [slice(c, min(c + block, cols)) for c in range(0, cols, block)]


def _conv_fwd(proj, w_dw, b_dw, g, b):
    t = proj.shape[0]
    tm = _pick(t, 512)
    nh = tm // CONV_HALO

    def body(p_ref, ph_ref, w_ref, bd_ref, g_ref, b_ref, act_ref, pre_ref, xs_ref):
        i = pl.program_id(0)
        halo = _glu(ph_ref[...])
        xs_ref[0:CONV_HALO, :] = jnp.where(i == 0, 0.0, halo)
        xs_ref[CONV_HALO:CONV_HALO + tm, :] = _glu(p_ref[...])
        taps = [(CONV_HALO - (CONV_K - 1) + k, (w_ref, k)) for k in range(CONV_K)]
        for r in range(tm // ROW_BLOCK):
            rows = slice(r * ROW_BLOCK, (r + 1) * ROW_BLOCK)
            for lanes in _lane_blocks(CONV_DIM):
                pre_ref[rows, lanes] = bd_ref[:, lanes] + _tap_sum(xs_ref, r * ROW_BLOCK, taps, ROW_BLOCK, lanes)
            acc = pre_ref[rows, :]
            xhat, _ = _ln(acc)
            yln = xhat * g_ref[...] + b_ref[...]
            act_ref[rows, :] = (yln * _sigmoid(yln)).astype(BF16)

    full = lambda s: pl.BlockSpec(s, lambda i: (0, 0))
    return pl.pallas_call(
        body, name="conv_fwd", grid=(t // tm,),
        in_specs=[pl.BlockSpec((tm, 2 * CONV_DIM), lambda i: (i, 0)),
                  pl.BlockSpec((CONV_HALO, 2 * CONV_DIM), lambda i: (jnp.maximum(i * nh - 1, 0), 0)),
                  full((CONV_K, CONV_DIM)), full((1, CONV_DIM)), full((1, CONV_DIM)), full((1, CONV_DIM))],
        out_specs=[pl.BlockSpec((tm, CONV_DIM), lambda i: (i, 0)), pl.BlockSpec((tm, CONV_DIM), lambda i: (i, 0))],
        out_shape=[jax.ShapeDtypeStruct((t, CONV_DIM), BF16), jax.ShapeDtypeStruct((t, CONV_DIM), F32)],
        scratch_shapes=[pltpu.VMEM((CONV_HALO + tm, CONV_DIM), F32)],
        compiler_params=_params(("arbitrary",)),
    )(proj, proj, w_dw, b_dw, g, b)


def _d_c_norm_bwd(d_y, w_conv_out, pre, g, b):
    t = pre.shape[0]
    tm = _pick(t, 512)
    d = D_MODEL

    def epilogue(d_c, ins, outs, scr):
        pre_ref, g_ref, b_ref = ins[2:]
        dpre_ref, sums_ref = outs
        i = pl.program_id(0)

        @pl.when(i == 0)
        def _():
            sums_ref[...] = jnp.zeros_like(sums_ref)

        for rows in _row_blocks(tm):
            xhat, rstd = _ln(pre_ref[rows, :])
            yln = xhat * g_ref[...] + b_ref[...]
            sg = _sigmoid(yln)
            dyln = d_c[rows, :] * (sg * (1.0 + yln * (1.0 - sg)))
            dpre = _ln_bwd(dyln, xhat, rstd, g_ref[...])
            dpre_ref[rows, :] = dpre
            sums_ref[0:1, :] += _colsum(dyln * xhat)
            sums_ref[1:2, :] += _colsum(dyln)
            sums_ref[2:3, :] += _colsum(dpre)

    full = lambda s: pl.BlockSpec(s, lambda i, j, k: (0, 0))
    tile = pl.BlockSpec((tm, CONV_DIM), lambda i, j, k: (i, 0))
    return _mm_fused(
        "d_c_norm_bwd", (t // tm, 1, 1), (d_y, w_conv_out, pre, g, b),
        [pl.BlockSpec((None, tm, d), lambda i, j, k: (0, i, 0)), full((CONV_DIM, d)), tile,
         full((1, CONV_DIM)), full((1, CONV_DIM))],
        [jax.ShapeDtypeStruct((t, CONV_DIM), F32), jax.ShapeDtypeStruct((8, CONV_DIM), F32)],
        [tile, full((8, CONV_DIM))], NT, (tm, CONV_DIM), epilogue)


def _conv_bwd_dw(d_pre, proj, w_dw, d_proj, hosted=None):
    t = d_pre.shape[0]
    tm = _pick(t, 512)
    nt = t // tm
    nh = tm // CONV_HALO
    last_h = t // CONV_HALO - 1

    def body(dp_ref, dph_ref, p_ref, ph_ref, w_ref, _, dproj_ref, dw_ref, xs_ref, ds_ref):
        i = pl.program_id(0)

        @pl.when(i == 0)
        def _():
            dw_ref[...] = jnp.zeros_like(dw_ref)

        halo = _glu(ph_ref[...])
        xs_ref[0:CONV_HALO, :] = jnp.where(i == 0, 0.0, halo)
        xs_ref[CONV_HALO:CONV_HALO + tm, :] = _glu(p_ref[...])
        ds_ref[0:tm, :] = dp_ref[...]
        ds_ref[tm:tm + CONV_HALO, :] = jnp.where(i == nt - 1, 0.0, dph_ref[...])
        back_taps = [(CONV_K - 1 - k, (w_ref, k)) for k in range(CONV_K)]
        grad_taps = [(CONV_HALO - (CONV_K - 1) + k, k) for k in range(CONV_K)]
        for r in range(tm // ROW_BLOCK):
            base = r * ROW_BLOCK
            rows = slice(base, base + ROW_BLOCK)
            for lanes in _lane_blocks(CONV_DIM):
                gate_lanes = slice(CONV_DIM + lanes.start, CONV_DIM + lanes.stop)
                acc = _tap_sum(ds_ref, base, back_taps, ROW_BLOCK, lanes)
                for k, total in _tap_products(ds_ref[rows, lanes], xs_ref, base, grad_taps, lanes):
                    dw_ref[k:k + 1, lanes] += total
                cval = p_ref[rows, lanes]
                sg = _sigmoid(p_ref[rows, gate_lanes])
                dproj_ref[rows, lanes] = (acc * sg).astype(BF16)
                dproj_ref[rows, gate_lanes] = (acc * cval * sg * (1.0 - sg)).astype(BF16)

    full = lambda s: pl.BlockSpec(s, lambda i: (0, 0))
    return _call(
        body, name="conv_bwd_dw", grid=(nt,),
        in_specs=[pl.BlockSpec((tm, CONV_DIM), lambda i: (i, 0)),
                  pl.BlockSpec((CONV_HALO, CONV_DIM), lambda i: (jnp.minimum((i + 1) * nh, last_h), 0)),
                  pl.BlockSpec((tm, 2 * CONV_DIM), lambda i: (i, 0)),
                  pl.BlockSpec((CONV_HALO, 2 * CONV_DIM), lambda i: (jnp.maximum(i * nh - 1, 0), 0)),
                  full((CONV_K, CONV_DIM)), ANY],
        out_specs=[pl.BlockSpec((tm, 2 * CONV_DIM), lambda i: (i, DP_CONV_BLOCK)), full((CONV_HALO, CONV_DIM))],
        out_shape=[jax.ShapeDtypeStruct(d_proj.shape, BF16), jax.ShapeDtypeStruct((CONV_HALO, CONV_DIM), F32)],
        scratch_shapes=[pltpu.VMEM((CONV_HALO + tm, CONV_DIM), F32), pltpu.VMEM((tm + CONV_HALO, CONV_DIM), F32)],
        semantics=("arbitrary",), operands=(d_pre, d_pre, proj, proj, w_dw, d_proj), hosted=hosted, aliases={5: 0})


def _lower_bound(logit_ref):
    l0 = logit_ref[0:1, :]
    l1 = logit_ref[1:2, :]
    m = jnp.maximum(l0, l1)
    e0 = jnp.exp(l0 - m)
    e1 = jnp.exp(l1 - m)
    return e0 / (e0 + e1)


def _tri(lower):
    r = lax.broadcasted_iota(jnp.int32, (CHUNK, CHUNK), 0)
    c = lax.broadcasted_iota(jnp.int32, (CHUNK, CHUNK), 1)
    return (c <= r) if lower else (c >= r)


def _hgrn_gates(fz, lb):
    s = _sigmoid(fz)
    sn = _sigmoid(-fz)
    f = lb + (1.0 - lb) * s
    return s, sn, f


def _block_tri(rows, lower=True):
    r = lax.broadcasted_iota(jnp.int32, (rows, rows), 0)
    c = lax.broadcasted_iota(jnp.int32, (rows, rows), 1)
    tri = (c <= r) if lower else (c >= r)
    return (tri & (r // CHUNK == c // CHUNK)).astype(BF16)


def _tri_rows(tm):
    return min(tm, MXU_DEPTH)


def _tri_matmul(tri_ref, x):
    hi = x.astype(BF16)
    lo = (x - hi.astype(F32)).astype(BF16)
    tri = tri_ref[...]
    return (lax.dot_general(tri, hi, NN, preferred_element_type=F32)
            + lax.dot_general(tri, lo, NN, preferred_element_type=F32))


def _groups(tm):
    g = _tri_rows(tm)
    return [slice(i * g, (i + 1) * g) for i in range(tm // g)]


def _hgrn_fwd(proj, logits, norm_g, hosted=None):
    t = proj.shape[0]
    tm = CHUNK * CHUNKS_PER_BLOCK if t % (CHUNK * CHUNKS_PER_BLOCK) == 0 else CHUNK
    cpb = tm // CHUNK
    nt = t // tm
    half = CHUNK // 2

    def body(qz_ref, fz_ref, iv_ref, gz_ref, lg_ref, ng_ref, tri_ref, o_ref, og_ref, st_ref,
             state_ref, qe_ref, ke_ref, qb_ref, kl_ref, v_ref, upd_ref, decay_ref, a_ref, q_ref, kk_ref, b_ref):
        j = pl.program_id(1)

        @pl.when(j == 0)
        def _():
            state_ref[...] = jnp.zeros_like(state_ref)

        lb = _lower_bound(lg_ref)
        chunks = [slice(c * CHUNK, (c + 1) * CHUNK) for c in range(cpb)]
        for rows in chunks:
            qz = qz_ref[rows, :]
            q_ref[rows, :] = qz * _sigmoid(qz)
            _, sn, f = _hgrn_gates(fz_ref[rows, :], lb)
            kk_ref[rows, :] = (1.0 - lb) * sn
            b_ref[rows, :] = jnp.log(f)
            v_ref[rows, :] = iv_ref[rows, :].astype(BF16)
        for rows in _groups(tm):
            b_ref[rows, :] = _tri_matmul(tri_ref, b_ref[rows, :])
        for c, rows in enumerate(chunks):
            b = b_ref[rows, :]
            bref = b[half - 1:half, :]
            blast = b[CHUNK - 1:CHUNK, :]
            q = q_ref[rows, :]
            kk = kk_ref[rows, :]
            qb_ref[rows, :] = (q * jnp.exp(b)).astype(BF16)
            qe_ref[rows, :] = (q * jnp.exp(b - bref)).astype(BF16)
            ke_ref[rows, :] = (kk * jnp.exp(bref - b)).astype(BF16)
            kl_ref[rows, :] = (kk * jnp.exp(blast - b)).astype(BF16)
            decay_ref[c:c + 1, :] = jnp.exp(blast)
        causal = _tri(True)
        for c, rows in enumerate(chunks):
            upd_ref[c] = _dot(v_ref[rows, :], kl_ref[rows, :], TN)
            a_ref[c] = jnp.where(causal, _dot(qe_ref[rows, :], ke_ref[rows, :], NT), 0.0).astype(BF16)
        state = state_ref[...]
        for c in range(cpb):
            st_ref[c] = state.astype(BF16)
            state = state * decay_ref[c:c + 1, :] + upd_ref[c]
        state_ref[...] = state
        for c, rows in enumerate(chunks):
            o_ref[rows, :] = _dot(a_ref[c], v_ref[rows, :], NN) + _dot(qb_ref[rows, :], st_ref[c], NT)
        for rows in chunks:
            o = o_ref[rows, :]
            r = lax.rsqrt(jnp.mean(o * o, axis=-1, keepdims=True) + RMS_EPS)
            gz = gz_ref[rows, :]
            og_ref[rows, :] = ((o * r * ng_ref[...]) * (gz * _sigmoid(gz))).astype(BF16)

    col = lambda base: pl.BlockSpec((tm, HEAD_DIM), lambda h, j: (j, base + h))
    tile_bf = pltpu.VMEM((tm, HEAD_DIM), BF16)
    tile_f32 = pltpu.VMEM((tm, HEAD_DIM), F32)
    return _call(
        body, name="hgrn_fwd", grid=(HEADS, nt),
        in_specs=[col(8), col(16), col(24), col(32),
                  pl.BlockSpec((2, HEAD_DIM), lambda h, j: (0, h)), pl.BlockSpec((1, HEAD_DIM), lambda h, j: (0, h)),
                  pl.BlockSpec((_tri_rows(tm), _tri_rows(tm)), lambda h, j: (0, 0))],
        out_specs=[col(0), col(0), pl.BlockSpec((None, cpb, HEAD_DIM, HEAD_DIM), lambda h, j: (h, j, 0, 0))],
        out_shape=[jax.ShapeDtypeStruct((t, HGRN_DIM), F32), jax.ShapeDtypeStruct((t, HGRN_DIM), BF16),
                   jax.ShapeDtypeStruct((HEADS, t // CHUNK, HEAD_DIM, HEAD_DIM), BF16)],
        scratch_shapes=[pltpu.VMEM((HEAD_DIM, HEAD_DIM), F32), tile_bf, tile_bf, tile_bf, tile_bf, tile_bf,
                        pltpu.VMEM((cpb, HEAD_DIM, HEAD_DIM), F32), pltpu.VMEM((max(cpb, 8), HEAD_DIM), F32),
                        pltpu.VMEM((cpb, CHUNK, CHUNK), BF16), tile_f32, tile_f32, tile_f32],
        semantics=("parallel", "arbitrary"),
        operands=(proj, proj, proj, proj, logits, norm_g, _block_tri(_tri_rows(tm))), hosted=hosted)


def _hgrn_bwd(d_og, o, proj, states, logits, norm_g, d_proj, hosted=None):
    t = proj.shape[0]
    tm = CHUNK * CHUNKS_PER_BLOCK if t % (CHUNK * CHUNKS_PER_BLOCK) == 0 else CHUNK
    cpb = tm // CHUNK
    nt = t // tm
    half = CHUNK // 2

    def body(dog_ref, o_ref, qz_ref, fz_ref, iv_ref, gz_ref, st_ref, lg_ref, ng_ref, tril_ref, triu_ref,
             _, dp_ref, sums_ref,
             dstate_ref, qe_ref, ke_ref, qb_ref, kl_ref, v_ref, do_ref, upd_ref, dst_ref, a_ref, da_ref,
             decay_ref, through_ref, q_ref, kk_ref, b_ref, dsilu_ref, gs_ref, gf_ref, sn_ref,
             eb_ref, ebr_ref, ekr_ref, ebl_ref, rev_ref, pre_ref, dk_ref):
        j = pl.program_id(1)

        @pl.when(j == 0)
        def _():
            dstate_ref[...] = jnp.zeros_like(dstate_ref)
            sums_ref[...] = jnp.zeros_like(sums_ref)

        lb = _lower_bound(lg_ref)
        ng = ng_ref[...]
        chunks = [slice(c * CHUNK, (c + 1) * CHUNK) for c in range(cpb)]
        for rows in chunks:
            qz = qz_ref[rows, :]
            sq = _sigmoid(qz)
            q_ref[rows, :] = qz * sq
            dsilu_ref[rows, :] = sq * (1.0 + qz * (1.0 - sq))
            s, sn, f = _hgrn_gates(fz_ref[rows, :], lb)
            kk_ref[rows, :] = (1.0 - lb) * sn
            b_ref[rows, :] = jnp.log(f)
            sn_ref[rows, :] = sn
            gf_ref[rows, :] = sn / f
            gs_ref[rows, :] = (1.0 - lb) * s
            v_ref[rows, :] = iv_ref[rows, :].astype(BF16)
            ov = o_ref[rows, :]
            r = lax.rsqrt(jnp.mean(ov * ov, axis=-1, keepdims=True) + RMS_EPS)
            on = ov * r
            gz = gz_ref[rows, :]
            sg = _sigmoid(gz)
            dog = dog_ref[rows, :]
            dp_ref[rows, 3 * HEAD_DIM:4 * HEAD_DIM] =(dog * (on * ng) * (sg * (1.0 + gz * (1.0 - sg)))).astype(BF16)
            d_ong = dog * (gz * sg)
            sums_ref[0:1, :] += _colsum(d_ong * on)
            d_on = d_ong * ng
            do_ref[rows, :] = (r * (d_on - on * jnp.mean(d_on * on, axis=-1, keepdims=True))).astype(BF16)
        for rows in _groups(tm):
            b_ref[rows, :] = _tri_matmul(tril_ref, b_ref[rows, :])
        for c, rows in enumerate(chunks):
            b = b_ref[rows, :]
            bref = b[half - 1:half, :]
            blast = b[CHUNK - 1:CHUNK, :]
            q = q_ref[rows, :]
            kk = kk_ref[rows, :]
            eb = jnp.exp(b)
            ebr = jnp.exp(b - bref)
            ekr = jnp.exp(bref - b)
            ebl = jnp.exp(blast - b)
            eb_ref[rows, :] = eb
            ebr_ref[rows, :] = ebr
            ekr_ref[rows, :] = ekr
            ebl_ref[rows, :] = ebl
            qb_ref[rows, :] = (q * eb).astype(BF16)
            qe_ref[rows, :] = (q * ebr).astype(BF16)
            ke_ref[rows, :] = (kk * ekr).astype(BF16)
            kl_ref[rows, :] = (kk * ebl).astype(BF16)
            decay_ref[c:c + 1, :] = jnp.exp(blast)
        causal = _tri(True)
        for c, rows in enumerate(chunks):
            upd_ref[c] = _dot(do_ref[rows, :], qb_ref[rows, :], TN)
            a_ref[c] = jnp.where(causal, _dot(qe_ref[rows, :], ke_ref[rows, :], NT), 0.0).astype(BF16)
            da_ref[c] = jnp.where(causal, _dot(do_ref[rows, :], v_ref[rows, :], NT), 0.0).astype(BF16)
        dstate = dstate_ref[...]
        for c in reversed(range(cpb)):
            dst_ref[c] = dstate.astype(BF16)
            decay = decay_ref[c:c + 1, :]
            through_ref[c:c + 1, :] = decay * _colsum(dstate * st_ref[c].astype(F32))
            dstate = dstate * decay + upd_ref[c]
        dstate_ref[...] = dstate
        for c, rows in enumerate(chunks):
            dp_ref[rows, 2 * HEAD_DIM:3 * HEAD_DIM] =(_dot(a_ref[c], do_ref[rows, :], TN)
                                + _dot(kl_ref[rows, :], dst_ref[c], NT)).astype(BF16)
        for c, rows in enumerate(chunks):
            dqe = _dot(da_ref[c], ke_ref[rows, :], NN)
            dq_inter = _dot(do_ref[rows, :], st_ref[c], NN) * eb_ref[rows, :]
            dp_ref[rows, 0:HEAD_DIM] =((dqe * ebr_ref[rows, :] + dq_inter) * dsilu_ref[rows, :]).astype(BF16)
            rev_ref[rows, :] = qe_ref[rows, :].astype(F32) * dqe + q_ref[rows, :] * dq_inter
        for c, rows in enumerate(chunks):
            dke = _dot(da_ref[c], qe_ref[rows, :], TN)
            dk_inter = _dot(v_ref[rows, :], dst_ref[c], NN) * ebl_ref[rows, :]
            dk_ref[rows, :] = dke * ekr_ref[rows, :] + dk_inter
            rev_ref[rows, :] -= ke_ref[rows, :].astype(F32) * dke
            pre_ref[rows, :] = kk_ref[rows, :] * dk_inter
        for rows in _groups(tm):
            pre = pre_ref[rows, :]
            rev_ref[rows, :] = _tri_matmul(triu_ref, rev_ref[rows, :]) + (_tri_matmul(tril_ref, pre) - pre)
        for c, rows in enumerate(chunks):
            dlf = rev_ref[rows, :] + through_ref[c:c + 1, :]
            common = gf_ref[rows, :] * dlf - sn_ref[rows, :] * dk_ref[rows, :]
            dp_ref[rows, HEAD_DIM:2 * HEAD_DIM] =(gs_ref[rows, :] * common).astype(BF16)
            sums_ref[1:2, :] += _colsum(common)

        @pl.when(j == nt - 1)
        def _():
            sums_ref[1:2, :] = sums_ref[1:2, :] * lb * (1.0 - lb)

    rev = lambda base: pl.BlockSpec((tm, HEAD_DIM), lambda h, j: (nt - 1 - j, base + h))
    vec = lambda n: pl.BlockSpec((n, HEAD_DIM), lambda h, j: (0, h))
    const = pl.BlockSpec((_tri_rows(tm), _tri_rows(tm)), lambda h, j: (0, 0))
    tile_bf = pltpu.VMEM((tm, HEAD_DIM), BF16)
    tile_f32 = pltpu.VMEM((tm, HEAD_DIM), F32)
    square = lambda dtype: pltpu.VMEM((cpb, HEAD_DIM, HEAD_DIM), dtype)
    rows8 = pltpu.VMEM((max(cpb, 8), HEAD_DIM), F32)
    operands = (d_og, o, proj, proj, proj, proj, states, logits, norm_g, _block_tri(_tri_rows(tm)),
                _block_tri(_tri_rows(tm), lower=False), d_proj)
    return _call(
        body, name="hgrn_bwd", grid=(HEADS, nt),
        in_specs=[rev(0), rev(0), rev(8), rev(16), rev(24), rev(32),
                  pl.BlockSpec((None, cpb, HEAD_DIM, HEAD_DIM), lambda h, j: (h, nt - 1 - j, 0, 0)),
                  vec(2), vec(1), const, const, ANY],
        out_specs=[pl.BlockSpec((tm, 4 * HEAD_DIM), lambda h, j: (nt - 1 - j, DP_HEAD_BLOCK + h)), vec(8)],
        out_shape=[jax.ShapeDtypeStruct(d_proj.shape, BF16), jax.ShapeDtypeStruct((8, HGRN_DIM), F32)],
        scratch_shapes=[pltpu.VMEM((HEAD_DIM, HEAD_DIM), F32)] + [tile_bf] * 6 + [square(F32), square(BF16)]
        + [pltpu.VMEM((cpb, CHUNK, CHUNK), BF16)] * 2 + [rows8, rows8] + [tile_f32] * 14,
        semantics=("parallel", "arbitrary"), operands=operands, hosted=hosted, aliases={len(operands) - 1: 0})


def _mix_ln1(proj, y_conv, y_hgrn, w_out, x, g, b):
    t = x.shape[0]
    tm = _pick(t, 512)
    d = D_MODEL

    def lhs(ins, outs):
        for rows in _row_blocks(tm):
            outs[0][rows, :] = (_sigmoid(ins[0][rows, :]) * ins[3][rows, :]
                                + _sigmoid(ins[2][rows, :]) * ins[4][rows, :]).astype(BF16)
        return outs[0][...]

    def epilogue(acc, ins, outs, scr):
        for rows in _row_blocks(tm):
            r = ALPHA * ins[5][rows, :] + acc[rows, :]
            outs[1][rows, :] = r
            xhat, _ = _ln(r)
            outs[2][rows, :] = (xhat * ins[6][...] + ins[7][...]).astype(BF16)

    tile = pl.BlockSpec((tm, d), lambda i, j, k: (i, 0))
    vec = pl.BlockSpec((1, d), lambda i, j, k: (0, 0))
    return _mm_fused(
        "mix_ln1", (t // tm, 1, 1), (proj, w_out, proj, y_conv, y_hgrn, x, g, b),
        [pl.BlockSpec((tm, d), lambda i, j, k: (i, 5)), pl.BlockSpec((d, d), lambda i, j, k: (0, 0)),
         pl.BlockSpec((tm, d), lambda i, j, k: (i, 6)), tile, tile, tile, vec, vec],
        [jax.ShapeDtypeStruct((t, d), BF16), jax.ShapeDtypeStruct((t, d), F32), jax.ShapeDtypeStruct((t, d), BF16)],
        [tile, tile, tile], NN, (tm, d), epilogue, lhs=lhs)


def _d_mixed_merge_bwd(d_r1b, w_out, proj, y_conv, y_hgrn, hosted=None):
    t = proj.shape[0]
    tm = _pick(t, 512)
    d = D_MODEL

    def epilogue(d_mixed, ins, outs, scr):
        dy_ref, dmz_ref = outs
        for rows in _row_blocks(tm):
            dm = d_mixed[rows, :]
            for br in range(2):
                sg = _sigmoid(ins[2 + br][rows, :])
                dy_ref[br, rows, :] = (sg * dm).astype(BF16)
                dmz_ref[rows, br * d:(br + 1) * d] = (dm * ins[4 + br][rows, :] * sg * (1.0 - sg)).astype(BF16)

    tile = pl.BlockSpec((tm, d), lambda i, j, k: (i, 0))
    return _mm_fused(
        "d_mixed_merge_bwd", (t // tm, 1, 1), (d_r1b, w_out, proj, proj, y_conv, y_hgrn),
        [tile, pl.BlockSpec((d, d), lambda i, j, k: (0, 0)), pl.BlockSpec((tm, d), lambda i, j, k: (i, 5)),
         pl.BlockSpec((tm, d), lambda i, j, k: (i, 6)), tile, tile],
        [jax.ShapeDtypeStruct((2, t, d), BF16), jax.ShapeDtypeStruct((t, IN_COLS), BF16)],
        [pl.BlockSpec((2, tm, d), lambda i, j, k: (0, i, 0)),
         pl.BlockSpec((tm, 2 * d), lambda i, j, k: (i, DP_MERGE_BLOCK))],
        NT, (tm, d), epilogue, hosted=hosted)


def _ffn_out_ln2(act, w_ffn_out, r1, target, g1, b1, g2, b2):
    t = r1.shape[0]
    tm = _pick(t, 1024)
    nt = t // tm
    d = D_MODEL

    def epilogue(y_ffn, ins, outs, scr):
        r1_ref, tg_ref, g1_ref, b1_ref, g2_ref, b2_ref = ins[2:]
        dr_ref, drb_ref, sums_ref = outs
        (sq_ref,) = scr
        i = pl.program_id(0)

        @pl.when(i == 0)
        def _():
            sums_ref[...] = jnp.zeros_like(sums_ref)
            sq_ref[...] = jnp.zeros_like(sq_ref)

        for rows in _row_blocks(tm):
            xh1, _ = _ln(r1_ref[rows, :])
            x1 = xh1 * g1_ref[...] + b1_ref[...]
            xh2, rstd2 = _ln(ALPHA * x1 + y_ffn[rows, :])
            diff = xh2 * g2_ref[...] + b2_ref[...] - tg_ref[rows, :]
            dy = diff * (1.0 / D_MODEL)
            dr = _ln_bwd(dy, xh2, rstd2, g2_ref[...])
            dr_ref[rows, :] = dr
            drb_ref[rows, :] = dr.astype(BF16)
            sums_ref[0:1, :] += _colsum(dy * xh2)
            sums_ref[1:2, :] += _colsum(dy)
            sq_ref[...] += _colsum(diff * diff)

        @pl.when(i == nt - 1)
        def _():
            total = jnp.sum(sq_ref[...], axis=-1, keepdims=True) * (0.5 / D_MODEL)
            sums_ref[2:3, :] = jnp.broadcast_to(total, (1, D_MODEL))

    tile = pl.BlockSpec((tm, d), lambda i, j, k: (i, 0))
    vec = pl.BlockSpec((1, d), lambda i, j, k: (0, 0))
    return _mm_fused(
        "ffn_out_ln2", (nt, 1, 4), (act, w_ffn_out, r1, target, g1, b1, g2, b2),
        [pl.BlockSpec((None, tm, FF_SHARD), lambda i, j, k: (k, i, 0)),
         pl.BlockSpec((None, FF_SHARD, d), lambda i, j, k: (k, 0, 0)), tile, tile, vec, vec, vec, vec],
        [jax.ShapeDtypeStruct((t, d), F32), jax.ShapeDtypeStruct((t, d), BF16), jax.ShapeDtypeStruct((8, d), F32)],
        [tile, tile, pl.BlockSpec((8, d), lambda i, j, k: (0, 0))], NN, (tm, d), epilogue,
        scratch=[pltpu.VMEM((1, d), F32)])


def _d_x1_ln1_bwd(d_z, w_ffn_in, d_r2, r1, g1):
    t = r1.shape[0]
    tm = _pick(t, 1024)
    d = D_MODEL

    def epilogue(dx_ffn, ins, outs, scr):
        dr2_ref, r1_ref, g_ref = ins[2:]
        dr1_ref, dr1b_ref, sums_ref = outs
        i = pl.program_id(0)

        @pl.when(i == 0)
        def _():
            sums_ref[...] = jnp.zeros_like(sums_ref)

        for rows in _row_blocks(tm):
            xhat, rstd = _ln(r1_ref[rows, :])
            dx1 = ALPHA * dr2_ref[rows, :] + dx_ffn[rows, :]
            dr1 = _ln_bwd(dx1, xhat, rstd, g_ref[...])
            dr1_ref[rows, :] = dr1
            dr1b_ref[rows, :] = dr1.astype(BF16)
            sums_ref[0:1, :] += _colsum(dx1 * xhat)
            sums_ref[1:2, :] += _colsum(dx1)

    tile = pl.BlockSpec((tm, d), lambda i, j, k: (i, 0))
    return _mm_fused(
        "d_x1_ln1_bwd", (t // tm, 1, N_DEV), (d_z, w_ffn_in, d_r2, r1, g1),
        [pl.BlockSpec((None, tm, FF_SHARD), lambda i, j, k: (k, i, 0)),
         pl.BlockSpec((None, d, FF_SHARD), lambda i, j, k: (k, 0, 0)), tile, tile,
         pl.BlockSpec((1, d), lambda i, j, k: (0, 0))],
        [jax.ShapeDtypeStruct((t, d), F32), jax.ShapeDtypeStruct((t, d), BF16), jax.ShapeDtypeStruct((8, d), F32)],
        [tile, tile, pl.BlockSpec((8, d), lambda i, j, k: (0, 0))], NT, (tm, d), epilogue)


def _cast_bf16(x):
    t = x.shape[0]
    tm = _pick(t, 512)

    def body(x_ref, o_ref):
        o_ref[...] = x_ref[...].astype(BF16)

    tile = pl.BlockSpec((tm, D_MODEL), lambda i: (i, 0))
    return pl.pallas_call(
        body, name="cast_x", grid=(t // tm,), in_specs=[tile], out_specs=tile,
        out_shape=jax.ShapeDtypeStruct((t, D_MODEL), BF16), compiler_params=_params(("parallel",)),
    )(x)


def _relayout(name, a, in_block, in_map, out_block, out_map, out_shape):
    def body(a_ref, o_ref):
        o_ref[...] = a_ref[...].astype(o_ref.dtype)

    return pl.pallas_call(
        body, name=name, grid=(N_DEV,), in_specs=[pl.BlockSpec(in_block, in_map)],
        out_specs=pl.BlockSpec(out_block, out_map), out_shape=out_shape, compiler_params=_params(("parallel",)),
    )(a)


_GELU_C = math.sqrt(2.0 / math.pi)


_GELU_CUBIC = 0.044715


def _gelu_parts(u):
    u2 = u * u
    th = jnp.tanh(u * (_GELU_C + (_GELU_C * _GELU_CUBIC) * u2))
    hu = 0.5 * u
    return th, hu + hu * th, u2, hu


BF16_ROWS = 16


def _ffn_act_fwd(z, w_dw, b_dw):
    t = z.shape[2]
    tm = _pick(t, 1024)
    nh = tm // FFN_HALO

    def body(z_ref, zh_ref, w_ref, b_ref, act_ref, gd_ref, us_ref):
        i = pl.program_id(1)
        us_ref[0:FFN_HALO, :] = jnp.where(i == 0, 0.0, zh_ref[...])
        us_ref[FFN_HALO:FFN_HALO + tm, :] = z_ref[0]
        for r in range(tm // ROW_BLOCK):
            base = r * ROW_BLOCK
            rows = slice(base, base + ROW_BLOCK)
            for lanes in _lane_blocks(FF_SHARD):
                uc = b_ref[:, lanes]
                for k in range(FFN_K):
                    off = base + FFN_HALO - (FFN_K - 1) + k
                    uc = uc + w_ref[k:k + 1, lanes] * us_ref[off:off + ROW_BLOCK, lanes]
                th, gelu, u2, hu = _gelu_parts(uc)
                dgelu = (0.5 + 0.5 * th) + (hu - hu * th * th) * (_GELU_C + (3.0 * _GELU_C * _GELU_CUBIC) * u2)
                act_ref[rows, lanes] = (gelu * z_ref[1, rows, lanes]).astype(BF16)
                gd_ref[0, rows, lanes] = gelu.astype(BF16)
                gd_ref[1, rows, lanes] = dgelu.astype(BF16)

    return pl.pallas_call(
        body, name="ffn_act_fwd", grid=(4, t // tm),
        in_specs=[pl.BlockSpec((2, None, tm, FF_SHARD), lambda j, i: (0, j, i, 0)),
                  pl.BlockSpec((None, None, FFN_HALO, FF_SHARD), lambda j, i: (0, j, jnp.maximum(i * nh - 1, 0), 0)),
                  pl.BlockSpec((None, FFN_K, FF_SHARD), lambda j, i: (j, 0, 0)),
                  pl.BlockSpec((None, 1, FF_SHARD), lambda j, i: (j, 0, 0))],
        out_specs=[pl.BlockSpec((None, tm, FF_SHARD), lambda j, i: (j, i, 0)),
                   pl.BlockSpec((2, None, tm, FF_SHARD), lambda j, i: (0, j, i, 0))],
        out_shape=[jax.ShapeDtypeStruct((4, t, FF_SHARD), BF16), jax.ShapeDtypeStruct((2, 4, t, FF_SHARD), BF16)],
        scratch_shapes=[pltpu.VMEM((FFN_HALO + tm, FF_SHARD), F32)],
        compiler_params=_params(("parallel", "arbitrary")),
    )(z, z, w_dw, b_dw)


def _ffn_act_bwd(d_act, z, gd, w_dw):
    t = z.shape[2]
    tm = _pick(t, 1024)
    nt = t // tm
    nh = tm // FFN_HALO
    last_h = t // FFN_HALO - 1
    pad = FFN_HALO - (FFN_K - 1)

    def fold(x):
        return functools.reduce(jnp.add, [x[r:r + SUBLANES, :] for r in range(0, x.shape[0], SUBLANES)])

    def body(da_ref, dah_ref, z_ref, zp_ref, gn_ref, gd_ref, gdn_ref, w_ref, dz_ref, sums_ref, us_ref, ds_ref,
             part_ref):
        i = pl.program_id(1)

        @pl.when(i == 0)
        def _():
            part_ref[...] = jnp.zeros_like(part_ref)

        us_ref[0:FFN_HALO, :] = jnp.where(i == 0, 0.0, zp_ref[...])
        us_ref[FFN_HALO:FFN_HALO + tm, :] = z_ref[0]
        for r in range(tm // ROW_BLOCK):
            base = r * ROW_BLOCK
            rows = slice(base, base + ROW_BLOCK)
            for lanes in _lane_blocks(FF_SHARD):
                da = da_ref[rows, lanes]
                dz_ref[1, rows, lanes] = (da * gd_ref[0, rows, lanes].astype(F32)).astype(BF16)
                duc = da * z_ref[1, rows, lanes] * gd_ref[1, rows, lanes].astype(F32)
                ds_ref[rows, lanes] = duc
                for k in range(FFN_K):
                    part_ref[k, :, lanes] += fold(duc * us_ref[base + pad + k:base + pad + k + ROW_BLOCK, lanes])
                part_ref[FFN_K, :, lanes] += fold(duc)
        duc_next = dah_ref[...] * gn_ref[...] * gdn_ref[0:FFN_HALO, :].astype(F32)
        ds_ref[tm:tm + FFN_HALO, :] = jnp.where(i == nt - 1, 0.0, duc_next)
        for r in range(tm // ROW_BLOCK):
            base = r * ROW_BLOCK
            for lanes in _lane_blocks(FF_SHARD):
                du = None
                for k in range(FFN_K):
                    off = base + FFN_K - 1 - k
                    term = w_ref[k:k + 1, lanes] * ds_ref[off:off + ROW_BLOCK, lanes]
                    du = term if du is None else du + term
                dz_ref[0, base:base + ROW_BLOCK, lanes] = du.astype(BF16)

        @pl.when(i == nt - 1)
        def _():
            sums_ref[...] = jnp.zeros_like(sums_ref)
            for k in range(FFN_K + 1):
                sums_ref[k:k + 1, :] = _colsum(part_ref[k])

    nxt = lambda i: jnp.minimum((i + 1) * nh, last_h)
    nxt_bf = lambda i: jnp.minimum((i + 1) * (tm // BF16_ROWS), t // BF16_ROWS - 1)
    return pl.pallas_call(
        body, name="ffn_act_bwd", grid=(4, nt),
        in_specs=[pl.BlockSpec((None, tm, FF_SHARD), lambda j, i: (j, i, 0)),
                  pl.BlockSpec((None, FFN_HALO, FF_SHARD), lambda j, i: (j, nxt(i), 0)),
                  pl.BlockSpec((2, None, tm, FF_SHARD), lambda j, i: (0, j, i, 0)),
                  pl.BlockSpec((None, None, FFN_HALO, FF_SHARD), lambda j, i: (0, j, jnp.maximum(i * nh - 1, 0), 0)),
                  pl.BlockSpec((None, None, FFN_HALO, FF_SHARD), lambda j, i: (1, j, nxt(i), 0)),
                  pl.BlockSpec((2, None, tm, FF_SHARD), lambda j, i: (0, j, i, 0)),
                  pl.BlockSpec((None, None, BF16_ROWS, FF_SHARD), lambda j, i: (1, j, nxt_bf(i), 0)),
                  pl.BlockSpec((None, FFN_K, FF_SHARD), lambda j, i: (j, 0, 0))],
        out_specs=[pl.BlockSpec((2, None, tm, FF_SHARD), lambda j, i: (0, j, i, 0)),
                   pl.BlockSpec((None, 8, FF_SHARD), lambda j, i: (j, 0, 0))],
        out_shape=[jax.ShapeDtypeStruct((2, 4, t, FF_SHARD), BF16), jax.ShapeDtypeStruct((4, 8, FF_SHARD), F32)],
        scratch_shapes=[pltpu.VMEM((FFN_HALO + tm, FF_SHARD), F32), pltpu.VMEM((tm + FFN_HALO, FF_SHARD), F32),
                        pltpu.VMEM((FFN_K + 1, SUBLANES, FF_SHARD), F32)],
        compiler_params=_params(("parallel", "arbitrary")),
    )(d_act, d_act, z, z, z, gd, gd, w_dw)


_HGRN_COLS = 4 * HGRN_DIM


def _to_backward_order(w):
    heads = w[:, 2 * CONV_DIM:2 * CONV_DIM + _HGRN_COLS].reshape(-1, 4, HEADS, HEAD_DIM)
    heads = jnp.swapaxes(heads, 1, 2).reshape(-1, _HGRN_COLS)
    return jnp.concatenate([w[:, 2 * CONV_DIM + _HGRN_COLS:], w[:, :2 * CONV_DIM], heads], axis=1)


def _from_backward_order(w):
    heads = w[:, 2 * D_MODEL + 2 * CONV_DIM:].reshape(-1, HEADS, 4, HEAD_DIM)
    heads = jnp.swapaxes(heads, 1, 2).reshape(-1, _HGRN_COLS)
    return jnp.concatenate([w[:, 2 * D_MODEL:2 * D_MODEL + 2 * CONV_DIM], heads, w[:, :2 * D_MODEL]], axis=1)
def _local_step(x, target, weights, small, scatter=None, order=None):
    t = x.shape[0]
    tm = _pick(t, 2048)
    tk = _pick(t, 2048)
    nm = t // tm
    nk = t // tk
    d = D_MODEL

    xb = _cast_bf16(x)
    ffn_gather = None
    if isinstance(weights, tuple) and isinstance(weights[0], _Hosted):
        first_gather, ffn_gather = weights
        proj, (w_in, w_conv_out8, w_hgrn_out8, w_out8, conv_dw8, ffn_dw8) = _proj_gather(xb, first_gather, order)
    else:
        w_in, w_conv_out8, w_hgrn_out8, w_out8, w_ffn_in, w_ffn_out8, conv_dw8, ffn_dw8 = weights
        proj = _mm("proj", xb, w_in, (t, IN_COLS), F32, (nm, N_DEV, 1),
                   pl.BlockSpec((tm, d), lambda i, j, k: (i, 0)),
                   pl.BlockSpec((None, d, IN_SHARD), lambda i, j, k: (j, 0, 0)),
                   pl.BlockSpec((tm, IN_SHARD), lambda i, j, k: (i, j)), NN, (tm, IN_SHARD))
    o, og, states, *late = _hgrn_fwd(proj, small["hgrn_lb_logits"], small["hgrn_norm_g"], hosted=ffn_gather)
    if ffn_gather is not None:
        w_ffn_in, w_ffn_out8 = late
    w_conv_out = _relayout("w_conv_out_natural", w_conv_out8, (None, CONV_DIM, 128), lambda j: (j, 0, 0),
                           (CONV_DIM, 128), lambda j: (0, j), jax.ShapeDtypeStruct((CONV_DIM, d), BF16))
    w_hgrn_out = w_hgrn_out8.reshape(d, d)
    w_out = w_out8.reshape(d, d)
    w_ffn_out = w_ffn_out8.reshape(4, FF_SHARD, d)
    conv_dw = jnp.transpose(conv_dw8[:, :CONV_K, :CONV_DIM // N_DEV], (1, 0, 2)).reshape(CONV_K, CONV_DIM)
    ffn_dw = jnp.transpose(ffn_dw8[:, :FFN_K, :D_FF // N_DEV], (1, 0, 2)).reshape(FFN_K, 4, FF_SHARD)
    small = dict(small, w_conv_dw=conv_dw, w_ffn_dw=jnp.transpose(ffn_dw, (1, 0, 2)),
                 b_ffn_dw=small["b_ffn_dw"].reshape(4, 1, FF_SHARD))

    c_act, conv_pre = _conv_fwd(proj, small["w_conv_dw"], small["b_conv_dw"], small["conv_ln_g"], small["conv_ln_b"])
    y_conv = _mm("y_conv", c_act, w_conv_out, (t, d), F32, (nm, 1, 1),
                 pl.BlockSpec((tm, CONV_DIM), lambda i, j, k: (i, 0)),
                 pl.BlockSpec((CONV_DIM, d), lambda i, j, k: (0, 0)),
                 pl.BlockSpec((tm, d), lambda i, j, k: (i, 0)), NN, (tm, d))
    sq_w = pl.BlockSpec((d, d), lambda i, j, k: (0, 0))
    row_tile = pl.BlockSpec((tm, d), lambda i, j, k: (i, 0))
    y_hgrn = _mm("y_hgrn", og, w_hgrn_out, (t, d), F32, (nm, 1, 1), row_tile, sq_w, row_tile, NN, (tm, d))
    mixed, r1, x1b = _mix_ln1(proj, y_conv, y_hgrn, w_out, x, small["ln1_g"], small["ln1_b"])
    z = _mm("ffn_in", x1b, w_ffn_in, (N_DEV, t, FF_SHARD), F32, (nm, N_DEV, 1), row_tile,
            pl.BlockSpec((None, d, FF_SHARD), lambda i, j, k: (j, 0, 0)),
            pl.BlockSpec((None, tm, FF_SHARD), lambda i, j, k: (j, i, 0)), NN, (tm, FF_SHARD))
    z = z.reshape(2, 4, t, FF_SHARD)
    act, gelu_and_slope = _ffn_act_fwd(z, small["w_ffn_dw"], small["b_ffn_dw"])

    d_r2, d_r2b, sums_ln2 = _ffn_out_ln2(act, w_ffn_out, r1, target, small["ln1_g"], small["ln1_b"],
                                         small["ln2_g"], small["ln2_b"])
    d_act = _mm("d_act", d_r2b, w_ffn_out, (4, t, FF_SHARD), F32, (nm, 4, 1), row_tile,
                pl.BlockSpec((None, FF_SHARD, d), lambda i, j, k: (j, 0, 0)),
                pl.BlockSpec((None, tm, FF_SHARD), lambda i, j, k: (j, i, 0)), NT, (tm, FF_SHARD))
    g_w_ffn_out = _mm("g_w_ffn_out", act, d_r2b, (4, FF_SHARD, d), BF16, (4, 1, nk),
                      pl.BlockSpec((None, tk, FF_SHARD), lambda i, j, k: (i, k, 0)),
                      pl.BlockSpec((tk, d), lambda i, j, k: (k, 0)),
                      pl.BlockSpec((None, FF_SHARD, d), lambda i, j, k: (i, 0, 0)), TN, (FF_SHARD, d))
    d_z, sums_ffn = _ffn_act_bwd(d_act, z, gelu_and_slope, small["w_ffn_dw"])
    d_z8 = d_z.reshape(N_DEV, t, FF_SHARD)
    d_r1, d_r1b, sums_ln1 = _d_x1_ln1_bwd(d_z8, w_ffn_in, d_r2, r1, small["ln1_g"])
    g_w_ffn_in = _mm("g_w_ffn_in", x1b, d_z8, (N_DEV, d, FF_SHARD), BF16, (N_DEV, 1, nk),
                     pl.BlockSpec((tk, d), lambda i, j, k: (k, 0)),
                     pl.BlockSpec((None, tk, FF_SHARD), lambda i, j, k: (i, k, 0)),
                     pl.BlockSpec((None, d, FF_SHARD), lambda i, j, k: (i, 0, 0)), TN, (d, FF_SHARD))
    k_tile = pl.BlockSpec((tk, d), lambda i, j, k: (k, 0))
    g_w_out = _mm("g_w_out", mixed, d_r1b, (d, d), BF16, (1, 1, nk), k_tile, k_tile, sq_w, TN, (d, d))
    send = (lambda grads: None) if scatter is None else scatter
    g_w_ffn_out = g_w_ffn_out.reshape(N_DEV, D_FF // N_DEV, d)
    d_y, d_proj, *recv_ffn_out = _d_mixed_merge_bwd(d_r1b, w_out, proj, y_conv, y_hgrn, hosted=send([g_w_ffn_out]))
    d_pre, sums_conv = _d_c_norm_bwd(d_y, w_conv_out, conv_pre, small["conv_ln_g"], small["conv_ln_b"])
    g_w_conv_out = _mm("g_w_conv_out", c_act, d_y, (CONV_DIM, d), BF16, (1, 1, nk),
                       pl.BlockSpec((tk, CONV_DIM), lambda i, j, k: (k, 0)),
                       pl.BlockSpec((None, tk, d), lambda i, j, k: (0, k, 0)),
                       pl.BlockSpec((CONV_DIM, d), lambda i, j, k: (0, 0)), TN, (CONV_DIM, d))
    g_w_conv_out = _relayout("g_w_conv_out_shards", g_w_conv_out, (CONV_DIM, 128), lambda j: (0, j),
                             (None, CONV_DIM, 128), lambda j: (j, 0, 0),
                             jax.ShapeDtypeStruct((N_DEV, CONV_DIM, 128), BF16))
    d_og = _mm("d_og", d_y, w_hgrn_out, (t, d), F32, (nm, 1, 1),
               pl.BlockSpec((None, tm, d), lambda i, j, k: (1, i, 0)), sq_w, row_tile, NT, (tm, d))
    g_w_hgrn_out = _mm("g_w_hgrn_out", og, d_y, (d, d), BF16, (1, 1, nk), k_tile,
                       pl.BlockSpec((None, tk, d), lambda i, j, k: (1, k, 0)), sq_w, TN, (d, d))
    d_proj, g_w_conv_dw, *recv_ffn_in = _conv_bwd_dw(d_pre, proj, small["w_conv_dw"], d_proj,
                                                     hosted=send([g_w_ffn_in]))
    early = [g_w_conv_out, g_w_hgrn_out.reshape(N_DEV, d // N_DEV, d), g_w_out.reshape(N_DEV, d // N_DEV, d)]
    d_proj, sums_hgrn, *early_recv = _hgrn_bwd(d_og, o, proj, states, small["hgrn_lb_logits"], small["hgrn_norm_g"],
                                               d_proj, hosted=send(early))
    early += [g_w_ffn_in, g_w_ffn_out]
    early_recv += recv_ffn_in + recv_ffn_out
    wide = IN_COLS // 4
    w_in_bwd = _to_backward_order(jnp.transpose(w_in, (1, 0, 2)).reshape(d, IN_COLS))
    g_w_in = _mm("g_w_in", xb, d_proj, (d, IN_COLS), BF16, (4, 1, nk), k_tile,
                 pl.BlockSpec((tk, wide), lambda i, j, k: (k, i)),
                 pl.BlockSpec((d, wide), lambda i, j, k: (0, i)), TN, (d, wide))
    g_w_in = jnp.transpose(_from_backward_order(g_w_in).reshape(d, N_DEV, IN_SHARD), (1, 0, 2))
    def add_residual(acc, ins, outs, scr):
        for rows in _row_blocks(ta):
            outs[0][rows, :] = ALPHA * ins[2][rows, :] + acc[rows, :]

    ta = _pick(t, 1024)
    acc_tile = pl.BlockSpec((ta, d), lambda i, j, k: (i, 0))
    grad_x, *late_recv = _mm_fused(
        "grad_x", (t // ta, 1, 4), (d_proj, w_in_bwd, d_r1),
        [pl.BlockSpec((ta, wide), lambda i, j, k: (i, k)), pl.BlockSpec((d, wide), lambda i, j, k: (0, k)), acc_tile],
        [jax.ShapeDtypeStruct((t, d), F32)], [acc_tile], NT, (ta, d), add_residual,
        hosted=None if scatter is None else scatter([g_w_in]))

    large_grads = [g_w_in] + early
    if scatter is not None:
        large_grads = list(zip(large_grads, late_recv + early_recv))
    return grad_x, large_grads, (sums_ln2, sums_conv, sums_hgrn, sums_ln1, sums_ffn, g_w_conv_dw)


def _small_views(sums):
    sums_ln2, sums_conv, sums_hgrn, sums_ln1, sums_ffn, g_w_conv_dw = sums
    d_l0 = sums_hgrn[1:2]
    return {
        "loss": sums_ln2[2:3, 0:128],
        "b_conv_dw": sums_conv[2:3], "conv_ln_g": sums_conv[0:1], "conv_ln_b": sums_conv[1:2],
        "hgrn_lb_logits": jnp.concatenate([d_l0, -d_l0], axis=1),
        "hgrn_norm_g": sums_hgrn[0:1],
        "ln1_g": sums_ln1[0:1], "ln1_b": sums_ln1[1:2],
        "b_ffn_dw": sums_ffn[:, FFN_K, :].reshape(1, D_FF),
        "ln2_g": sums_ln2[0:1], "ln2_b": sums_ln2[1:2],
        "w_conv_dw": g_w_conv_dw[0:CONV_K].reshape(1, CONV_K * CONV_DIM),
        "w_ffn_dw": jnp.transpose(sums_ffn[:, 0:FFN_K, :], (1, 0, 2)).reshape(1, FFN_K * D_FF),
    }


def _coords():
    return lax.axis_index("x"), lax.axis_index("y"), lax.axis_index("c")


def _gather(shards, staged=False):
    n = len(shards)
    later = range(1 if staged else 0, n)

    def parts(ins, outs, sems):
        send_sems, recv_sems, local_sems = sems
        x, y, c = _coords()
        me = 4 * x + 2 * y + c
        sibling = (x, y, 1 - c)
        chips = [(1 - x, y), (x, 1 - y), (1 - x, 1 - y)]

        def copy(a, k, block, to, src=None):
            return pltpu.make_async_remote_copy(
                src_ref=outs[a].at[block] if src is None else src, dst_ref=outs[a].at[block],
                send_sem=send_sems.at[a, k], recv_sem=recv_sems.at[a, k], device_id=to, device_id_type=MESH)

        local = [pltpu.make_async_copy(ins[a], outs[a].at[me], local_sems.at[a]) for a in range(n)]
        first = []
        for a in range(n):
            first.append(copy(a, 0, me, sibling, src=ins[a]))
            for j, chip in enumerate(chips):
                first.append(copy(a, 1 + j, me, (*chip, c), src=ins[a]))
        return x, y, c, sibling, chips, copy, local, first

    def start(ins, outs, sems):
        *_, local, first = parts(ins, outs, sems)
        for cp in local + first:
            cp.start()

    def arrive(ins, outs, sems, s):
        x, y, c, sibling, chips, copy, _, _ = parts(ins, outs, sems)
        if s == 1:
            block = 4 * x + 2 * y + 1 - c
            copy(0, 0, block, sibling).wait_recv()
        elif s <= 4:
            px, py = chips[s - 2]
            block = 4 * px + 2 * py + c
            copy(0, s - 1, block, sibling).wait_recv()
            copy(0, s + 2, block, sibling).start()
        else:
            px, py = chips[s - 5]
            block = 4 * px + 2 * py + 1 - c
            copy(0, s - 1, block, sibling).wait_recv()
        return block

    def finish(ins, outs, sems):
        x, y, c, sibling, chips, copy, local, first = parts(ins, outs, sems)
        passed = [copy(0, 4 + j, 4 * px + 2 * py + c, sibling) for j, (px, py) in enumerate(chips)] if staged else []
        for j, (px, py) in enumerate(chips):
            for a in later:
                copy(a, 1 + j, 4 * px + 2 * py + c, sibling).wait_recv()
                cp = copy(a, 4 + j, 4 * px + 2 * py + c, sibling)
                cp.start()
                passed.append(cp)
        for a in later:
            copy(a, 0, 4 * x + 2 * y + 1 - c, sibling).wait_recv()
            for j, (px, py) in enumerate(chips):
                copy(a, 4 + j, 4 * px + 2 * py + 1 - c, sibling).wait_recv()
        for cp in first + passed:
            cp.wait_send()
        for cp in local:
            cp.wait()

    hosted = _Hosted(shards, [jax.ShapeDtypeStruct((N_DEV,) + s.shape, s.dtype) for s in shards],
                     [pltpu.SemaphoreType.DMA((n, 7)), pltpu.SemaphoreType.DMA((n, 7)), pltpu.SemaphoreType.DMA((n,))],
                     start, finish)
    hosted.arrive = arrive
    return hosted


def _proj_gather(xb, gather, order):
    t, d = xb.shape
    tm = _pick(t, 2048)
    nm = t // tm
    n_in, n_out = len(gather.inputs), len(gather.out_shapes)

    def body(order_ref, x_ref, *refs):
        ins, refs = refs[:n_in], refs[n_in:]
        o_ref, outs, refs = refs[0], refs[1:1 + n_out], refs[1 + n_out:]
        w_buf, w_sem, sems = refs[0], refs[1], refs[2:]
        s, i = pl.program_id(0), pl.program_id(1)

        @pl.when((s == 0) & (i == 0))
        def _():
            gather.start(ins, outs, sems)

        def staging(step, src):
            return pltpu.make_async_copy(src, w_buf.at[step % 2], w_sem.at[step % 2])

        @pl.when((s == 0) & (i == 0))
        def _():
            staging(0, ins[0]).start()

        for step in range(1, N_DEV):
            @pl.when((s == step - 1) & (i == nm - 1))
            def _(step=step):
                staging(step, outs[0].at[gather.arrive(ins, outs, sems, step)]).start()

        for step in range(N_DEV):
            @pl.when((s == step) & (i == 0))
            def _(step=step):
                staging(step, ins[0]).wait()

        o_ref[...] = _dot(x_ref[...], w_buf[s % 2], NN)

        @pl.when((s == N_DEV - 1) & (i == nm - 1))
        def _():
            gather.finish(ins, outs, sems)

    outs = pl.pallas_call(
        body, name="proj_gather",
        grid_spec=pltpu.PrefetchScalarGridSpec(
            num_scalar_prefetch=1, grid=(N_DEV, nm),
            in_specs=[pl.BlockSpec((tm, d), lambda s, i, order_ref: (i, 0))] + [ANY] * n_in,
            out_specs=[pl.BlockSpec((tm, IN_SHARD), lambda s, i, order_ref: (i, order_ref[s]))] + [ANY] * n_out,
            scratch_shapes=[pltpu.VMEM((2, d, IN_SHARD), BF16), pltpu.SemaphoreType.DMA((2,))] + gather.sem_shapes),
        out_shape=[jax.ShapeDtypeStruct((t, IN_COLS), F32)] + gather.out_shapes,
        compiler_params=pltpu.CompilerParams(dimension_semantics=("arbitrary", "arbitrary"),
                                             vmem_limit_bytes=VMEM_LIMIT, has_side_effects=True),
    )(order, xb, *gather.inputs)
    return outs[0], list(outs[1:])


def _scatter(grads):
    n = len(grads)

    def copies(ins, outs, sems):
        send_sems, recv_sems = sems
        x, y, c = _coords()
        out = []
        for a in range(n):
            for k in range(1, N_DEV):
                px, py, pc = x ^ (k >> 2), y ^ ((k >> 1) & 1), c ^ (k & 1)
                out.append(pltpu.make_async_remote_copy(
                    src_ref=ins[a].at[4 * px + 2 * py + pc], dst_ref=outs[a].at[k - 1],
                    send_sem=send_sems.at[a, k - 1], recv_sem=recv_sems.at[a, k - 1],
                    device_id=(px, py, pc), device_id_type=MESH))
        return out

    def start(ins, outs, sems):
        for cp in copies(ins, outs, sems):
            cp.start()

    def finish(ins, outs, sems):
        for cp in copies(ins, outs, sems):
            cp.wait()

    return _Hosted(grads, [jax.ShapeDtypeStruct((N_DEV - 1,) + g.shape[1:], g.dtype) for g in grads],
                   [pltpu.SemaphoreType.DMA((n, N_DEV - 1)), pltpu.SemaphoreType.DMA((n, N_DEV - 1))], start, finish)


def _row_tile(rows):
    return 256 if rows % 256 == 0 else rows


def _adam_math(w, g, m, v):
    m_new = ADAM_B1 * m + (1.0 - ADAM_B1) * g
    v_new = ADAM_B2 * v + (1.0 - ADAM_B2) * (g * g)
    m_hat = m_new / (1.0 - ADAM_B1 ** ADAM_STEP)
    v_hat = v_new / (1.0 - ADAM_B2 ** ADAM_STEP)
    delta = -ADAM_LR * (m_hat / (jnp.sqrt(v_hat) + ADAM_EPS) + ADAM_WD * w)
    return delta, m_new, v_new


def _adam_large(name, own, recv, me, w, m, v):
    rows, cols = w.shape
    tr = _row_tile(rows)

    def body(me_ref, p_ref, r_ref, w_ref, m_ref, v_ref, g_out, d_out, m_out, v_out):
        g = p_ref[...].astype(F32)
        for k in range(N_DEV - 1):
            g = g + r_ref[k].astype(F32)
        delta, m_new, v_new = _adam_math(w_ref[...], g, m_ref[...], v_ref[...])
        g_out[...] = g
        d_out[...] = delta
        m_out[...] = m_new
        v_out[...] = v_new

    tile = pl.BlockSpec((tr, cols), lambda r, me_ref: (r, 0))
    sds = jax.ShapeDtypeStruct((rows, cols), F32)
    return pl.pallas_call(
        body, name=name,
        grid_spec=pltpu.PrefetchScalarGridSpec(
            num_scalar_prefetch=1, grid=(rows // tr,),
            in_specs=[pl.BlockSpec((None, tr, cols), lambda r, me_ref: (me_ref[0], r, 0)),
                      pl.BlockSpec((N_DEV - 1, tr, cols), lambda r, me_ref: (0, r, 0)), tile, tile, tile],
            out_specs=[tile, tile, tile, tile]),
        out_shape=[sds, sds, sds, sds],
        compiler_params=_params(("parallel",)),
    )(me, own, recv, w, m, v)


def _small_allreduce(arrays):
    n = len(arrays)

    def body(*refs):
        ins, outs, gats = refs[:n], refs[n:2 * n], refs[2 * n:3 * n]
        send_sems, recv_sems = refs[3 * n:]
        x, y, c = _coords()
        me = 4 * x + 2 * y + c
        peers = [(x ^ (k >> 2), y ^ ((k >> 1) & 1), c ^ (k & 1)) for k in range(1, N_DEV)]

        def copy(a, k, slot):
            return pltpu.make_async_remote_copy(
                src_ref=ins[a], dst_ref=gats[a].at[slot], send_sem=send_sems.at[a, k], recv_sem=recv_sems.at[a, k],
                device_id=peers[k], device_id_type=MESH)

        sends = [copy(a, k, me) for a in range(n) for k in range(N_DEV - 1)]
        for a in range(n):
            gats[a][me] = ins[a][...]
        for cp in sends:
            cp.start()
        for a in range(n):
            for k, (px, py, pc) in enumerate(peers):
                copy(a, k, 4 * px + 2 * py + pc).wait_recv()
        for cp in sends:
            cp.wait_send()
        for a in range(n):
            acc = gats[a][0]
            for dev in range(1, N_DEV):
                acc = acc + gats[a][dev]
            outs[a][...] = acc

    whole = pl.BlockSpec(memory_space=pltpu.VMEM)
    return pl.pallas_call(
        body, name="small_allreduce", in_specs=[whole] * n, out_specs=[whole] * n,
        out_shape=[jax.ShapeDtypeStruct(a.shape, F32) for a in arrays],
        scratch_shapes=[pltpu.VMEM((N_DEV,) + a.shape, F32) for a in arrays]
        + [pltpu.SemaphoreType.DMA((n, N_DEV - 1)), pltpu.SemaphoreType.DMA((n, N_DEV - 1))],
        compiler_params=pltpu.CompilerParams(has_side_effects=True, vmem_limit_bytes=VMEM_LIMIT),
    )(*arrays)


def _adam_replicated(sums, w, m, v):
    rows_of = {"conv_ln_g": (1, 0), "conv_ln_b": (1, 1), "b_conv_dw": (1, 2), "hgrn_norm_g": (2, 0),
               "ln1_g": (3, 0), "ln1_b": (3, 1), "ln2_g": (0, 0), "ln2_b": (0, 1)}
    names = list(rows_of) + ["hgrn_lb_logits"]
    n = len(names)

    def body(*refs):
        sum_refs, refs = refs[:4], refs[4:]
        w_refs, m_refs, v_refs, outs = refs[:n], refs[n:2 * n], refs[2 * n:3 * n], refs[3 * n:]
        for j, name in enumerate(names):
            if name == "hgrn_lb_logits":
                d_l0 = sum_refs[2][1:2, :]
                grads = [d_l0, -d_l0]
            else:
                a, row = rows_of[name]
                grads = [sum_refs[a][row:row + 1, :]]
            g_out, d_out, m_out, v_out = outs[4 * j:4 * j + 4]
            for r, g in enumerate(grads):
                rows = slice(r, r + 1)
                delta, m_new, v_new = _adam_math(w_refs[j][rows, :], g, m_refs[j][rows, :], v_refs[j][rows, :])
                g_out[rows, :] = g
                d_out[rows, :] = delta
                m_out[rows, :] = m_new
                v_out[rows, :] = v_new

    whole = pl.BlockSpec(memory_space=pltpu.VMEM)
    operands = list(sums) + [w[k] for k in names] + [m[k] for k in names] + [v[k] for k in names]
    outs = pl.pallas_call(
        body, name="adam_replicated", in_specs=[whole] * len(operands), out_specs=[whole] * (4 * n),
        out_shape=[jax.ShapeDtypeStruct(w[k].shape, F32) for k in names for _ in range(4)],
    )(*operands)
    return {name: tuple(outs[4 * j:4 * j + 4]) for j, name in enumerate(names)}


def _adam_small(w, g, m, v):
    def body(w_ref, g_ref, m_ref, v_ref, d_out, m_out, v_out):
        delta, m_new, v_new = _adam_math(w_ref[...], g_ref[...], m_ref[...], v_ref[...])
        d_out[...] = delta
        m_out[...] = m_new
        v_out[...] = v_new

    whole = pl.BlockSpec(memory_space=pltpu.VMEM)
    sds = jax.ShapeDtypeStruct(w.shape, F32)
    return pl.pallas_call(body, name="adam_small", in_specs=[whole] * 4, out_specs=[whole] * 3,
                          out_shape=[sds, sds, sds])(w, g, m, v)


_WEIGHTS = ["w_in", "w_conv_dw", "b_conv_dw", "conv_ln_g", "conv_ln_b", "w_conv_out", "hgrn_lb_logits", "hgrn_norm_g",
            "w_hgrn_out", "w_out", "ln1_g", "ln1_b", "w_ffn_in", "w_ffn_dw", "b_ffn_dw", "w_ffn_out", "ln2_g", "ln2_b"]
_LARGE = ["w_in", "w_conv_out", "w_hgrn_out", "w_out", "w_ffn_in", "w_ffn_out"]
_CONV_DW_SHARD = CONV_DIM // N_DEV
_FFN_DW_SHARD = D_FF // N_DEV


def kernel(x, w_in, w_conv_dw, b_conv_dw, conv_ln_g, conv_ln_b, w_conv_out, hgrn_lb_logits, hgrn_norm_g, w_hgrn_out, w_out, ln1_g, ln1_b, w_ffn_in, w_ffn_dw, b_ffn_dw, w_ffn_out, ln2_g, ln2_b, loss_target, m_w_in, m_w_conv_dw, m_b_conv_dw, m_conv_ln_g, m_conv_ln_b, m_w_conv_out, m_hgrn_lb_logits, m_hgrn_norm_g, m_w_hgrn_out, m_w_out, m_ln1_g, m_ln1_b, m_w_ffn_in, m_w_ffn_dw, m_b_ffn_dw, m_w_ffn_out, m_ln2_g, m_ln2_b, v_w_in, v_w_conv_dw, v_b_conv_dw, v_conv_ln_g, v_conv_ln_b, v_w_conv_out, v_hgrn_lb_logits, v_hgrn_norm_g, v_w_hgrn_out, v_w_out, v_ln1_g, v_ln1_b, v_w_ffn_in, v_w_ffn_dw, v_b_ffn_dw, v_w_ffn_out, v_ln2_g, v_ln2_b):
    w = dict(w_in=w_in, w_conv_dw=w_conv_dw, b_conv_dw=b_conv_dw, conv_ln_g=conv_ln_g, conv_ln_b=conv_ln_b,
             w_conv_out=w_conv_out, hgrn_lb_logits=hgrn_lb_logits, hgrn_norm_g=hgrn_norm_g, w_hgrn_out=w_hgrn_out,
             w_out=w_out, ln1_g=ln1_g, ln1_b=ln1_b, w_ffn_in=w_ffn_in, w_ffn_dw=w_ffn_dw, b_ffn_dw=b_ffn_dw,
             w_ffn_out=w_ffn_out, ln2_g=ln2_g, ln2_b=ln2_b)
    m = dict(w_in=m_w_in, w_conv_dw=m_w_conv_dw, b_conv_dw=m_b_conv_dw, conv_ln_g=m_conv_ln_g, conv_ln_b=m_conv_ln_b,
             w_conv_out=m_w_conv_out, hgrn_lb_logits=m_hgrn_lb_logits, hgrn_norm_g=m_hgrn_norm_g,
             w_hgrn_out=m_w_hgrn_out, w_out=m_w_out, ln1_g=m_ln1_g, ln1_b=m_ln1_b, w_ffn_in=m_w_ffn_in,
             w_ffn_dw=m_w_ffn_dw, b_ffn_dw=m_b_ffn_dw, w_ffn_out=m_w_ffn_out, ln2_g=m_ln2_g, ln2_b=m_ln2_b)
    v = dict(w_in=v_w_in, w_conv_dw=v_w_conv_dw, b_conv_dw=v_b_conv_dw, conv_ln_g=v_conv_ln_g, conv_ln_b=v_conv_ln_b,
             w_conv_out=v_w_conv_out, hgrn_lb_logits=v_hgrn_lb_logits, hgrn_norm_g=v_hgrn_norm_g,
             w_hgrn_out=v_w_hgrn_out, w_out=v_w_out, ln1_g=v_ln1_g, ln1_b=v_ln1_b, w_ffn_in=v_w_ffn_in,
             w_ffn_dw=v_w_ffn_dw, b_ffn_dw=v_b_ffn_dw, w_ffn_out=v_w_ffn_out, ln2_g=v_ln2_g, ln2_b=v_ln2_b)
    xi, yi, ci = lax.axis_index("x"), lax.axis_index("y"), lax.axis_index("c")
    me = 4 * xi + 2 * yi + ci
    me_op = jnp.reshape(me, (1,)).astype(jnp.int32)

    shards = [w[name][0].astype(BF16) for name in _LARGE]
    shards.append(jnp.pad(w_conv_dw[0], ((0, 1), (0, 128 - _CONV_DW_SHARD))))
    shards.append(jnp.pad(w_ffn_dw[0], ((0, 8 - FFN_K), (0, 384 - _FFN_DW_SHARD))))
    chips = [(1 - xi, yi), (xi, 1 - yi), (1 - xi, 1 - yi)]
    order = jnp.stack([me, me ^ 1] + [4 * px + 2 * py + ci for px, py in chips]
                      + [4 * px + 2 * py + 1 - ci for px, py in chips]).astype(jnp.int32)
    small = dict(b_conv_dw=b_conv_dw, conv_ln_g=conv_ln_g, conv_ln_b=conv_ln_b, hgrn_lb_logits=hgrn_lb_logits,
                 hgrn_norm_g=hgrn_norm_g, ln1_g=ln1_g, ln1_b=ln1_b, ln2_g=ln2_g, ln2_b=ln2_b, b_ffn_dw=b_ffn_dw)

    gathers = (_gather(shards[:4] + shards[6:], staged=True), _gather(shards[4:6]))
    grad_x, large_grads, small_sums = _local_step(x[0], loss_target[0], gathers, small, _scatter, order)

    out = {}
    for name, (own, recv) in zip(_LARGE, large_grads):
        out[name] = _adam_large("adam_" + name, own, recv, me_op, w[name][0], m[name][0], v[name][0])

    totals = _small_allreduce(list(small_sums))
    out.update(_adam_replicated(totals[:4], w, m, v))
    summed = _small_views(totals)
    loss = summed["loss"][0, 0]
    conv_dw_g = lax.dynamic_slice_in_dim(summed["w_conv_dw"].reshape(CONV_K, CONV_DIM), me * _CONV_DW_SHARD, _CONV_DW_SHARD, axis=1)
    ffn_dw_g = lax.dynamic_slice_in_dim(summed["w_ffn_dw"].reshape(FFN_K, D_FF), me * _FFN_DW_SHARD, _FFN_DW_SHARD, axis=1)
    small_g = dict(b_ffn_dw=summed["b_ffn_dw"], w_conv_dw=conv_dw_g.reshape(1, -1), w_ffn_dw=ffn_dw_g.reshape(1, -1))
    names = list(small_g)
    flat = lambda d, n: d[n].reshape(1, -1)
    n_small = sum(small_g[n].shape[1] for n in names)
    pad = (-n_small) % 1024
    pack = lambda pieces: jnp.pad(jnp.concatenate(pieces, axis=1), ((0, 0), (0, pad))).reshape(-1, 128)
    d_s, m_s, v_s = _adam_small(pack([flat(w, n) for n in names]), pack([small_g[n] for n in names]),
                                pack([flat(m, n) for n in names]), pack([flat(v, n) for n in names]))
    pos = 0
    for n in names:
        size = small_g[n].shape[1]
        cut = lambda a: a.reshape(1, -1)[:, pos:pos + size].reshape(w[n].shape)
        out[n] = (small_g[n].reshape(w[n].shape), cut(d_s), cut(m_s), cut(v_s))
        pos += size

    for name in _LARGE:
        out[name] = tuple(a.reshape(w[name].shape) for a in out[name])
    grads = [out[n][0] for n in _WEIGHTS]
    deltas = [out[n][1] for n in _WEIGHTS]
    new_m = [out[n][2] for n in _WEIGHTS]
    new_v = [out[n][3] for n in _WEIGHTS]
    return (loss, grad_x[None], *grads, *deltas, *new_m, *new_v)
```

```python
import functools
import math

import jax
import jax.numpy as jnp
from jax import lax
from jax.experimental import pallas as pl
from jax.experimental.pallas import tpu as pltpu

F32 = jnp.float32
BF16 = jnp.bfloat16

N_DEV = 8
D_MODEL = 1024
CONV_DIM = 512
CONV_K = 31
HGRN_DIM = 1024
HEADS = 8
HEAD_DIM = 128
D_FF = 2816
FFN_K = 3
FF_SHARD = 2 * D_FF // N_DEV
IN_COLS = 7168
IN_SHARD = IN_COLS // N_DEV
LN_EPS = 1e-5
RMS_EPS = 1e-6
ALPHA = 2.0 ** 0.25

ADAM_LR = 0.001
ADAM_B1 = 0.9
ADAM_B2 = 0.999
ADAM_EPS = 1e-08
ADAM_WD = 0.01
ADAM_STEP = 10

CHUNK = 64
CHUNKS_PER_BLOCK = 32
CONV_HALO = 32
FFN_HALO = 8
ROW_BLOCK = 64
SUBLANES = 8
VMEM_LIMIT = 48 * 1024 * 1024
MXU_DEPTH = 256

DP_MERGE_BLOCK = 0
DP_CONV_BLOCK = 2
DP_HEAD_BLOCK = 6

MESH = pl.DeviceIdType.MESH
ANY = pl.BlockSpec(memory_space=pl.ANY)

NN = (((1,), (0,)), ((), ()))
NT = (((1,), (1,)), ((), ()))
TN = (((0,), (0,)), ((), ()))


def _params(sem):
    return pltpu.CompilerParams(dimension_semantics=sem, vmem_limit_bytes=VMEM_LIMIT)


def _dot(a, b, dims):
    return lax.dot_general(a.astype(BF16), b.astype(BF16), dims, preferred_element_type=F32)


def _sigmoid(x):
    return jax.nn.sigmoid(x)


def _ln(r):
    mu = jnp.mean(r, axis=-1, keepdims=True)
    xc = r - mu
    var = jnp.mean(xc * xc, axis=-1, keepdims=True)
    rstd = lax.rsqrt(var + LN_EPS)
    return xc * rstd, rstd


def _ln_bwd(dy, xhat, rstd, g):
    dxh = dy * g
    m1 = jnp.mean(dxh, axis=-1, keepdims=True)
    m2 = jnp.mean(dxh * xhat, axis=-1, keepdims=True)
    return rstd * (dxh - m1 - xhat * m2)


def _colsum(x):
    return jnp.sum(x, axis=0, keepdims=True)


class _Hosted:
    def __init__(self, inputs, out_shapes, sem_shapes, start, finish, middle=None):
        self.inputs, self.out_shapes, self.sem_shapes = list(inputs), list(out_shapes), list(sem_shapes)
        self.start, self.finish, self.middle = start, finish, middle


def _call(body, *, name, grid, in_specs, out_specs, out_shape, scratch_shapes, semantics, operands, hosted=None,
          aliases=None):
    aliases = aliases or {}
    if hosted is None:
        return pl.pallas_call(
            body, name=name, grid=grid, in_specs=list(in_specs), out_specs=list(out_specs), out_shape=list(out_shape),
            scratch_shapes=list(scratch_shapes), input_output_aliases=aliases,
            compiler_params=_params(semantics))(*operands)
    n_in, n_out, n_scr = len(in_specs), len(out_specs), len(scratch_shapes)
    h_in, h_out = len(hosted.inputs), len(hosted.out_shapes)

    def full_body(*refs):
        ins, refs = refs[:n_in], refs[n_in:]
        h_ins, refs = refs[:h_in], refs[h_in:]
        outs, refs = refs[:n_out], refs[n_out:]
        h_outs, refs = refs[:h_out], refs[h_out:]
        scr, sems = refs[:n_scr], refs[n_scr:]
        first = functools.reduce(jnp.logical_and, [pl.program_id(d) == 0 for d in range(len(grid))])
        last = functools.reduce(jnp.logical_and, [pl.program_id(d) == grid[d] - 1 for d in range(len(grid))])

        @pl.when(first)
        def _():
            hosted.start(h_ins, h_outs, sems)

        body(*ins, *outs, *scr)

        if hosted.middle is not None:
            step, total = 0, 1
            for d in range(len(grid)):
                step, total = step * grid[d] + pl.program_id(d), total * grid[d]

            @pl.when(step == (2 * total) // 3)
            def _():
                hosted.middle(h_ins, h_outs, sems)

        @pl.when(last)
        def _():
            hosted.finish(h_ins, h_outs, sems)

    return pl.pallas_call(
        full_body, name=name, grid=grid, in_specs=list(in_specs) + [ANY] * h_in,
        out_specs=list(out_specs) + [ANY] * h_out, out_shape=list(out_shape) + hosted.out_shapes,
        scratch_shapes=list(scratch_shapes) + hosted.sem_shapes, input_output_aliases=aliases,
        compiler_params=pltpu.CompilerParams(dimension_semantics=("arbitrary",) * len(grid),
                                             vmem_limit_bytes=VMEM_LIMIT, has_side_effects=True),
    )(*operands, *hosted.inputs)


def _mm(name, a, b, out_shape, out_dtype, grid, a_spec, b_spec, o_spec, dims, acc_shape, hosted=None):
    nk = grid[2]
    if nk == 1:
        def body(a_ref, b_ref, o_ref):
            o_ref[...] = _dot(a_ref[...], b_ref[...], dims).astype(o_ref.dtype)
        scratch = []
    else:
        def body(a_ref, b_ref, o_ref, acc_ref):
            k = pl.program_id(2)

            @pl.when(k == 0)
            def _():
                acc_ref[...] = jnp.zeros_like(acc_ref)

            acc_ref[...] += _dot(a_ref[...], b_ref[...], dims)

            @pl.when(k == nk - 1)
            def _():
                o_ref[...] = acc_ref[...].astype(o_ref.dtype)
        scratch = [pltpu.VMEM(acc_shape, F32)]

    outs = _call(body, name=name, grid=grid, in_specs=[a_spec, b_spec], out_specs=[o_spec],
                 out_shape=[jax.ShapeDtypeStruct(out_shape, out_dtype)], scratch_shapes=scratch,
                 semantics=("parallel", "parallel", "arbitrary"), operands=(a, b), hosted=hosted)
    return outs[0] if hosted is None else (outs[0], list(outs[1:]))


def _mm_fused(name, grid, operands, in_specs, out_shape, out_specs, dims, acc_shape, epilogue, lhs=None, scratch=(),
              hosted=None):
    nk = grid[2]
    n_in, n_out = len(in_specs), len(out_specs)

    def body(*refs):
        ins, outs, scr = refs[:n_in], refs[n_in:n_in + n_out], refs[n_in + n_out:]
        acc_ref, k = scr[0], pl.program_id(2)
        a = ins[0][...] if lhs is None else lhs(ins, outs)
        part = _dot(a, ins[1][...], dims)
        if nk == 1:
            acc_ref[...] = part
            epilogue(acc_ref, ins, outs, scr[1:])
            return

        @pl.when(k == 0)
        def _():
            acc_ref[...] = jnp.zeros_like(acc_ref)

        acc_ref[...] += part

        @pl.when(k == nk - 1)
        def _():
            epilogue(acc_ref, ins, outs, scr[1:])

    return _call(body, name=name, grid=grid, in_specs=in_specs, out_specs=out_specs, out_shape=out_shape,
                 scratch_shapes=[pltpu.VMEM(acc_shape, F32)] + list(scratch), semantics=("arbitrary",) * 3,
                 operands=operands, hosted=hosted)


def _row_blocks(rows, block=256):
    block = block if rows % block == 0 else rows
    return [slice(r, r + block) for r in range(0, rows, block)]


def _pick(t, pref):
    return pref if t % pref == 0 else t


def _glu(p):
    return p[:, :CONV_DIM] * _sigmoid(p[:, CONV_DIM:])


def _by_phase(taps):
    phases = {}
    for off, payload in taps:
        phases.setdefault(off % SUBLANES, []).append((off - off % SUBLANES, payload))
    return sorted(phases.items())


def _tap_sum(src_ref, base, taps, rows, lanes):
    acc = None
    for phase, items in _by_phase(taps):
        n = rows if phase == 0 else rows + SUBLANES
        part = None
        for off, (w_ref, k) in items:
            term = w_ref[k:k + 1, lanes] * src_ref[base + off:base + off + n, lanes]
            part = term if part is None else part + term
        if phase:
            part = part[phase:phase + rows, :]
        acc = part if acc is None else acc + part
    return acc


def _tap_products(x, src_ref, base, taps, lanes):
    rows, cols = x.shape
    pad = jnp.zeros((SUBLANES, cols), x.dtype)
    padded = jnp.concatenate([pad, x, pad], axis=0)
    out = []
    for phase, items in _by_phase(taps):
        n = rows if phase == 0 else rows + SUBLANES
        shifted = x if phase == 0 else padded[SUBLANES - phase:SUBLANES - phase + n, :]
        for off, key in items:
            out.append((key, _colsum(shifted * src_ref[base + off:base + off + n, lanes])))
    return out


def _lane_blocks(cols, block=256):
    return [slice(c, min(c + block, cols)) for c in range(0, cols, block)]


def _conv_fwd(proj, w_dw, b_dw, g, b):
    t = proj.shape[0]
    tm = _pick(t, 512)
    nh = tm // CONV_HALO

    def body(p_ref, ph_ref, w_ref, bd_ref, g_ref, b_ref, act_ref, pre_ref, xs_ref):
        i = pl.program_id(0)
        halo = _glu(ph_ref[...])
        xs_ref[0:CONV_HALO, :] = jnp.where(i == 0, 0.0, halo)
        xs_ref[CONV_HALO:CONV_HALO + tm, :] = _glu(p_ref[...])
        taps = [(CONV_HALO - (CONV_K - 1) + k, (w_ref, k)) for k in range(CONV_K)]
        for r in range(tm // ROW_BLOCK):
            rows = slice(r * ROW_BLOCK, (r + 1) * ROW_BLOCK)
            for lanes in _lane_blocks(CONV_DIM):
                pre_ref[rows, lanes] = bd_ref[:, lanes] + _tap_sum(xs_ref, r * ROW_BLOCK, taps, ROW_BLOCK, lanes)
            acc = pre_ref[rows, :]
            xhat, _ = _ln(acc)
            yln = xhat * g_ref[...] + b_ref[...]
            act_ref[rows, :] = (yln * _sigmoid(yln)).astype(BF16)

    full = lambda s: pl.BlockSpec(s, lambda i: (0, 0))
    return pl.pallas_call(
        body, name="conv_fwd", grid=(t // tm,),
        in_specs=[pl.BlockSpec((tm, 2 * CONV_DIM), lambda i: (i, 0)),
                  pl.BlockSpec((CONV_HALO, 2 * CONV_DIM), lambda i: (jnp.maximum(i * nh - 1, 0), 0)),
                  full((CONV_K, CONV_DIM)), full((1, CONV_DIM)), full((1, CONV_DIM)), full((1, CONV_DIM))],
        out_specs=[pl.BlockSpec((tm, CONV_DIM), lambda i: (i, 0)), pl.BlockSpec((tm, CONV_DIM), lambda i: (i, 0))],
        out_shape=[jax.ShapeDtypeStruct((t, CONV_DIM), BF16), jax.ShapeDtypeStruct((t, CONV_DIM), F32)],
        scratch_shapes=[pltpu.VMEM((CONV_HALO + tm, CONV_DIM), F32)],
        compiler_params=_params(("arbitrary",)),
    )(proj, proj, w_dw, b_dw, g, b)


def _d_c_norm_bwd(d_y, w_conv_out, pre, g, b):
    t = pre.shape[0]
    tm = _pick(t, 512)
    d = D_MODEL

    def epilogue(d_c, ins, outs, scr):
        pre_ref, g_ref, b_ref = ins[2:]
        dpre_ref, sums_ref = outs
        i = pl.program_id(0)

        @pl.when(i == 0)
        def _():
            sums_ref[...] = jnp.zeros_like(sums_ref)

        for rows in _row_blocks(tm):
            xhat, rstd = _ln(pre_ref[rows, :])
            yln = xhat * g_ref[...] + b_ref[...]
            sg = _sigmoid(yln)
            dyln = d_c[rows, :] * (sg * (1.0 + yln * (1.0 - sg)))
            dpre = _ln_bwd(dyln, xhat, rstd, g_ref[...])
            dpre_ref[rows, :] = dpre
            sums_ref[0:1, :] += _colsum(dyln * xhat)
            sums_ref[1:2, :] += _colsum(dyln)
            sums_ref[2:3, :] += _colsum(dpre)

    full = lambda s: pl.BlockSpec(s, lambda i, j, k: (0, 0))
    tile = pl.BlockSpec((tm, CONV_DIM), lambda i, j, k: (i, 0))
    return _mm_fused(
        "d_c_norm_bwd", (t // tm, 1, 1), (d_y, w_conv_out, pre, g, b),
        [pl.BlockSpec((None, tm, d), lambda i, j, k: (0, i, 0)), full((CONV_DIM, d)), tile,
         full((1, CONV_DIM)), full((1, CONV_DIM))],
        [jax.ShapeDtypeStruct((t, CONV_DIM), F32), jax.ShapeDtypeStruct((8, CONV_DIM), F32)],
        [tile, full((8, CONV_DIM))], NT, (tm, CONV_DIM), epilogue)


def _conv_bwd_dw(d_pre, proj, w_dw, d_proj, hosted=None):
    t = d_pre.shape[0]
    tm = _pick(t, 512)
    nt = t // tm
    nh = tm // CONV_HALO
    last_h = t // CONV_HALO - 1

    def body(dp_ref, dph_ref, p_ref, ph_ref, w_ref, _, dproj_ref, dw_ref, xs_ref, ds_ref):
        i = pl.program_id(0)

        @pl.when(i == 0)
        def _():
            dw_ref[...] = jnp.zeros_like(dw_ref)

        halo = _glu(ph_ref[...])
        xs_ref[0:CONV_HALO, :] = jnp.where(i == 0, 0.0, halo)
        xs_ref[CONV_HALO:CONV_HALO + tm, :] = _glu(p_ref[...])
        ds_ref[0:tm, :] = dp_ref[...]
        ds_ref[tm:tm + CONV_HALO, :] = jnp.where(i == nt - 1, 0.0, dph_ref[...])
        back_taps = [(CONV_K - 1 - k, (w_ref, k)) for k in range(CONV_K)]
        grad_taps = [(CONV_HALO - (CONV_K - 1) + k, k) for k in range(CONV_K)]
        for r in range(tm // ROW_BLOCK):
            base = r * ROW_BLOCK
            rows = slice(base, base + ROW_BLOCK)
            for lanes in _lane_blocks(CONV_DIM):
                gate_lanes = slice(CONV_DIM + lanes.start, CONV_DIM + lanes.stop)
                acc = _tap_sum(ds_ref, base, back_taps, ROW_BLOCK, lanes)
                for k, total in _tap_products(ds_ref[rows, lanes], xs_ref, base, grad_taps, lanes):
                    dw_ref[k:k + 1, lanes] += total
                cval = p_ref[rows, lanes]
                sg = _sigmoid(p_ref[rows, gate_lanes])
                dproj_ref[rows, lanes] = (acc * sg).astype(BF16)
                dproj_ref[rows, gate_lanes] = (acc * cval * sg * (1.0 - sg)).astype(BF16)

    full = lambda s: pl.BlockSpec(s, lambda i: (0, 0))
    return _call(
        body, name="conv_bwd_dw", grid=(nt,),
        in_specs=[pl.BlockSpec((tm, CONV_DIM), lambda i: (i, 0)),
                  pl.BlockSpec((CONV_HALO, CONV_DIM), lambda i: (jnp.minimum((i + 1) * nh, last_h), 0)),
                  pl.BlockSpec((tm, 2 * CONV_DIM), lambda i: (i, 0)),
                  pl.BlockSpec((CONV_HALO, 2 * CONV_DIM), lambda i: (jnp.maximum(i * nh - 1, 0), 0)),
                  full((CONV_K, CONV_DIM)), ANY],
        out_specs=[pl.BlockSpec((tm, 2 * CONV_DIM), lambda i: (i, DP_CONV_BLOCK)), full((CONV_HALO, CONV_DIM))],
        out_shape=[jax.ShapeDtypeStruct(d_proj.shape, BF16), jax.ShapeDtypeStruct((CONV_HALO, CONV_DIM), F32)],
        scratch_shapes=[pltpu.VMEM((CONV_HALO + tm, CONV_DIM), F32), pltpu.VMEM((tm + CONV_HALO, CONV_DIM), F32)],
        semantics=("arbitrary",), operands=(d_pre, d_pre, proj, proj, w_dw, d_proj), hosted=hosted, aliases={5: 0})


def _lower_bound(logit_ref):
    l0 = logit_ref[0:1, :]
    l1 = logit_ref[1:2, :]
    m = jnp.maximum(l0, l1)
    e0 = jnp.exp(l0 - m)
    e1 = jnp.exp(l1 - m)
    return e0 / (e0 + e1)


def _tri(lower):
    r = lax.broadcasted_iota(jnp.int32, (CHUNK, CHUNK), 0)
    c = lax.broadcasted_iota(jnp.int32, (CHUNK, CHUNK), 1)
    return (c <= r) if lower else (c >= r)


def _hgrn_gates(fz, lb):
    s = _sigmoid(fz)
    sn = _sigmoid(-fz)
    f = lb + (1.0 - lb) * s
    return s, sn, f


def _block_tri(rows, lower=True):
    r = lax.broadcasted_iota(jnp.int32, (rows, rows), 0)
    c = lax.broadcasted_iota(jnp.int32, (rows, rows), 1)
    tri = (c <= r) if lower else (c >= r)
    return (tri & (r // CHUNK == c // CHUNK)).astype(BF16)


def _tri_rows(tm):
    return min(tm, MXU_DEPTH)


def _tri_matmul(tri_ref, x):
    hi = x.astype(BF16)
    lo = (x - hi.astype(F32)).astype(BF16)
    tri = tri_ref[...]
    return (lax.dot_general(tri, hi, NN, preferred_element_type=F32)
            + lax.dot_general(tri, lo, NN, preferred_element_type=F32))


def _groups(tm):
    g = _tri_rows(tm)
    return [slice(i * g, (i + 1) * g) for i in range(tm // g)]


def _hgrn_fwd(proj, logits, norm_g, hosted=None):
    t = proj.shape[0]
    tm = CHUNK * CHUNKS_PER_BLOCK if t % (CHUNK * CHUNKS_PER_BLOCK) == 0 else CHUNK
    cpb = tm // CHUNK
    nt = t // tm
    half = CHUNK // 2

    def body(qz_ref, fz_ref, iv_ref, gz_ref, lg_ref, ng_ref, tri_ref, o_ref, og_ref, st_ref,
             state_ref, qe_ref, ke_ref, qb_ref, kl_ref, v_ref, upd_ref, decay_ref, a_ref, q_ref, kk_ref, b_ref):
        j = pl.program_id(1)

        @pl.when(j == 0)
        def _():
            state_ref[...] = jnp.zeros_like(state_ref)

        lb = _lower_bound(lg_ref)
        chunks = [slice(c * CHUNK, (c + 1) * CHUNK) for c in range(cpb)]
        for rows in chunks:
            qz = qz_ref[rows, :]
            q_ref[rows, :] = qz * _sigmoid(qz)
            _, sn, f = _hgrn_gates(fz_ref[rows, :], lb)
            kk_ref[rows, :] = (1.0 - lb) * sn
            b_ref[rows, :] = jnp.log(f)
            v_ref[rows, :] = iv_ref[rows, :].astype(BF16)
        for rows in _groups(tm):
            b_ref[rows, :] = _tri_matmul(tri_ref, b_ref[rows, :])
        for c, rows in enumerate(chunks):
            b = b_ref[rows, :]
            bref = b[half - 1:half, :]
            blast = b[CHUNK - 1:CHUNK, :]
            q = q_ref[rows, :]
            kk = kk_ref[rows, :]
            qb_ref[rows, :] = (q * jnp.exp(b)).astype(BF16)
            qe_ref[rows, :] = (q * jnp.exp(b - bref)).astype(BF16)
            ke_ref[rows, :] = (kk * jnp.exp(bref - b)).astype(BF16)
            kl_ref[rows, :] = (kk * jnp.exp(blast - b)).astype(BF16)
            decay_ref[c:c + 1, :] = jnp.exp(blast)
        causal = _tri(True)
        for c, rows in enumerate(chunks):
            upd_ref[c] = _dot(v_ref[rows, :], kl_ref[rows, :], TN)
            a_ref[c] = jnp.where(causal, _dot(qe_ref[rows, :], ke_ref[rows, :], NT), 0.0).astype(BF16)
        state = state_ref[...]
        for c in range(cpb):
            st_ref[c] = state.astype(BF16)
            state = state * decay_ref[c:c + 1, :] + upd_ref[c]
        state_ref[...] = state
        for c, rows in enumerate(chunks):
            o_ref[rows, :] = _dot(a_ref[c], v_ref[rows, :], NN) + _dot(qb_ref[rows, :], st_ref[c], NT)
        for rows in chunks:
            o = o_ref[rows, :]
            r = lax.rsqrt(jnp.mean(o * o, axis=-1, keepdims=True) + RMS_EPS)
            gz = gz_ref[rows, :]
            og_ref[rows, :] = ((o * r * ng_ref[...]) * (gz * _sigmoid(gz))).astype(BF16)

    col = lambda base: pl.BlockSpec((tm, HEAD_DIM), lambda h, j: (j, base + h))
    tile_bf = pltpu.VMEM((tm, HEAD_DIM), BF16)
    tile_f32 = pltpu.VMEM((tm, HEAD_DIM), F32)
    return _call(
        body, name="hgrn_fwd", grid=(HEADS, nt),
        in_specs=[col(8), col(16), col(24), col(32),
                  pl.BlockSpec((2, HEAD_DIM), lambda h, j: (0, h)), pl.BlockSpec((1, HEAD_DIM), lambda h, j: (0, h)),
                  pl.BlockSpec((_tri_rows(tm), _tri_rows(tm)), lambda h, j: (0, 0))],
        out_specs=[col(0), col(0), pl.BlockSpec((None, cpb, HEAD_DIM, HEAD_DIM), lambda h, j: (h, j, 0, 0))],
        out_shape=[jax.ShapeDtypeStruct((t, HGRN_DIM), F32), jax.ShapeDtypeStruct((t, HGRN_DIM), BF16),
                   jax.ShapeDtypeStruct((HEADS, t // CHUNK, HEAD_DIM, HEAD_DIM), BF16)],
        scratch_shapes=[pltpu.VMEM((HEAD_DIM, HEAD_DIM), F32), tile_bf, tile_bf, tile_bf, tile_bf, tile_bf,
                        pltpu.VMEM((cpb, HEAD_DIM, HEAD_DIM), F32), pltpu.VMEM((max(cpb, 8), HEAD_DIM), F32),
                        pltpu.VMEM((cpb, CHUNK, CHUNK), BF16), tile_f32, tile_f32, tile_f32],
        semantics=("parallel", "arbitrary"),
        operands=(proj, proj, proj, proj, logits, norm_g, _block_tri(_tri_rows(tm))), hosted=hosted)


def _hgrn_bwd(d_og, o, proj, states, logits, norm_g, d_proj, hosted=None):
    t = proj.shape[0]
    tm = CHUNK * CHUNKS_PER_BLOCK if t % (CHUNK * CHUNKS_PER_BLOCK) == 0 else CHUNK
    cpb = tm // CHUNK
    nt = t // tm
    half = CHUNK // 2

    def body(dog_ref, o_ref, qz_ref, fz_ref, iv_ref, gz_ref, st_ref, lg_ref, ng_ref, tril_ref, triu_ref,
             _, dp_ref, sums_ref,
             dstate_ref, qe_ref, ke_ref, qb_ref, kl_ref, v_ref, do_ref, upd_ref, dst_ref, a_ref, da_ref,
             decay_ref, through_ref, q_ref, kk_ref, b_ref, dsilu_ref, gs_ref, gf_ref, sn_ref,
             eb_ref, ebr_ref, ekr_ref, ebl_ref, rev_ref, pre_ref, dk_ref):
        j = pl.program_id(1)

        @pl.when(j == 0)
        def _():
            dstate_ref[...] = jnp.zeros_like(dstate_ref)
            sums_ref[...] = jnp.zeros_like(sums_ref)

        lb = _lower_bound(lg_ref)
        ng = ng_ref[...]
        chunks = [slice(c * CHUNK, (c + 1) * CHUNK) for c in range(cpb)]
        for rows in chunks:
            qz = qz_ref[rows, :]
            sq = _sigmoid(qz)
            q_ref[rows, :] = qz * sq
            dsilu_ref[rows, :] = sq * (1.0 + qz * (1.0 - sq))
            s, sn, f = _hgrn_gates(fz_ref[rows, :], lb)
            kk_ref[rows, :] = (1.0 - lb) * sn
            b_ref[rows, :] = jnp.log(f)
            sn_ref[rows, :] = sn
            gf_ref[rows, :] = sn / f
            gs_ref[rows, :] = (1.0 - lb) * s
            v_ref[rows, :] = iv_ref[rows, :].astype(BF16)
            ov = o_ref[rows, :]
            r = lax.rsqrt(jnp.mean(ov * ov, axis=-1, keepdims=True) + RMS_EPS)
            on = ov * r
            gz = gz_ref[rows, :]
            sg = _sigmoid(gz)
            dog = dog_ref[rows, :]
            dp_ref[rows, 3 * HEAD_DIM:4 * HEAD_DIM] =(dog * (on * ng) * (sg * (1.0 + gz * (1.0 - sg)))).astype(BF16)
            d_ong = dog * (gz * sg)
            sums_ref[0:1, :] += _colsum(d_ong * on)
            d_on = d_ong * ng
            do_ref[rows, :] = (r * (d_on - on * jnp.mean(d_on * on, axis=-1, keepdims=True))).astype(BF16)
        for rows in _groups(tm):
            b_ref[rows, :] = _tri_matmul(tril_ref, b_ref[rows, :])
        for c, rows in enumerate(chunks):
            b = b_ref[rows, :]
            bref = b[half - 1:half, :]
            blast = b[CHUNK - 1:CHUNK, :]
            q = q_ref[rows, :]
            kk = kk_ref[rows, :]
            eb = jnp.exp(b)
            ebr = jnp.exp(b - bref)
            ekr = jnp.exp(bref - b)
            ebl = jnp.exp(blast - b)
            eb_ref[rows, :] = eb
            ebr_ref[rows, :] = ebr
            ekr_ref[rows, :] = ekr
            ebl_ref[rows, :] = ebl
            qb_ref[rows, :] = (q * eb).astype(BF16)
            qe_ref[rows, :] = (q * ebr).astype(BF16)
            ke_ref[rows, :] = (kk * ekr).astype(BF16)
            kl_ref[rows, :] = (kk * ebl).astype(BF16)
            decay_ref[c:c + 1, :] = jnp.exp(blast)
        causal = _tri(True)
        for c, rows in enumerate(chunks):
            upd_ref[c] = _dot(do_ref[rows, :], qb_ref[rows, :], TN)
            a_ref[c] = jnp.where(causal, _dot(qe_ref[rows, :], ke_ref[rows, :], NT), 0.0).astype(BF16)
            da_ref[c] = jnp.where(causal, _dot(do_ref[rows, :], v_ref[rows, :], NT), 0.0).astype(BF16)
        dstate = dstate_ref[...]
        for c in reversed(range(cpb)):
            dst_ref[c] = dstate.astype(BF16)
            decay = decay_ref[c:c + 1, :]
            through_ref[c:c + 1, :] = decay * _colsum(dstate * st_ref[c].astype(F32))
            dstate = dstate * decay + upd_ref[c]
        dstate_ref[...] = dstate
        for c, rows in enumerate(chunks):
            dp_ref[rows, 2 * HEAD_DIM:3 * HEAD_DIM] =(_dot(a_ref[c], do_ref[rows, :], TN)
                                + _dot(kl_ref[rows, :], dst_ref[c], NT)).astype(BF16)
        for c, rows in enumerate(chunks):
            dqe = _dot(da_ref[c], ke_ref[rows, :], NN)
            dq_inter = _dot(do_ref[rows, :], st_ref[c], NN) * eb_ref[rows, :]
            dp_ref[rows, 0:HEAD_DIM] =((dqe * ebr_ref[rows, :] + dq_inter) * dsilu_ref[rows, :]).astype(BF16)
            rev_ref[rows, :] = qe_ref[rows, :].astype(F32) * dqe + q_ref[rows, :] * dq_inter
        for c, rows in enumerate(chunks):
            dke = _dot(da_ref[c], qe_ref[rows, :], TN)
            dk_inter = _dot(v_ref[rows, :], dst_ref[c], NN) * ebl_ref[rows, :]
            dk_ref[rows, :] = dke * ekr_ref[rows, :] + dk_inter
            rev_ref[rows, :] -= ke_ref[rows, :].astype(F32) * dke
            pre_ref[rows, :] = kk_ref[rows, :] * dk_inter
        for rows in _groups(tm):
            pre = pre_ref[rows, :]
            rev_ref[rows, :] = _tri_matmul(triu_ref, rev_ref[rows, :]) + (_tri_matmul(tril_ref, pre) - pre)
        for c, rows in enumerate(chunks):
            dlf = rev_ref[rows, :] + through_ref[c:c + 1, :]
            common = gf_ref[rows, :] * dlf - sn_ref[rows, :] * dk_ref[rows, :]
            dp_ref[rows, HEAD_DIM:2 * HEAD_DIM] =(gs_ref[rows, :] * common).astype(BF16)
            sums_ref[1:2, :] += _colsum(common)

        @pl.when(j == nt - 1)
        def _():
            sums_ref[1:2, :] = sums_ref[1:2, :] * lb * (1.0 - lb)

    rev = lambda base: pl.BlockSpec((tm, HEAD_DIM), lambda h, j: (nt - 1 - j, base + h))
    vec = lambda n: pl.BlockSpec((n, HEAD_DIM), lambda h, j: (0, h))
    const = pl.BlockSpec((_tri_rows(tm), _tri_rows(tm)), lambda h, j: (0, 0))
    tile_bf = pltpu.VMEM((tm, HEAD_DIM), BF16)
    tile_f32 = pltpu.VMEM((tm, HEAD_DIM), F32)
    square = lambda dtype: pltpu.VMEM((cpb, HEAD_DIM, HEAD_DIM), dtype)
    rows8 = pltpu.VMEM((max(cpb, 8), HEAD_DIM), F32)
    operands = (d_og, o, proj, proj, proj, proj, states, logits, norm_g, _block_tri(_tri_rows(tm)),
                _block_tri(_tri_rows(tm), lower=False), d_proj)
    return _call(
        body, name="hgrn_bwd", grid=(HEADS, nt),
        in_specs=[rev(0), rev(0), rev(8), rev(16), rev(24), rev(32),
                  pl.BlockSpec((None, cpb, HEAD_DIM, HEAD_DIM), lambda h, j: (h, nt - 1 - j, 0, 0)),
                  vec(2), vec(1), const, const, ANY],
        out_specs=[pl.BlockSpec((tm, 4 * HEAD_DIM), lambda h, j: (nt - 1 - j, DP_HEAD_BLOCK + h)), vec(8)],
        out_shape=[jax.ShapeDtypeStruct(d_proj.shape, BF16), jax.ShapeDtypeStruct((8, HGRN_DIM), F32)],
        scratch_shapes=[pltpu.VMEM((HEAD_DIM, HEAD_DIM), F32)] + [tile_bf] * 6 + [square(F32), square(BF16)]
        + [pltpu.VMEM((cpb, CHUNK, CHUNK), BF16)] * 2 + [rows8, rows8] + [tile_f32] * 14,
        semantics=("parallel", "arbitrary"), operands=operands, hosted=hosted, aliases={len(operands) - 1: 0})


def _mix_ln1(proj, y_conv, y_hgrn, w_out, x, g, b, hosted=None):
    t = x.shape[0]
    tm = _pick(t, 512)
    d = D_MODEL

    def lhs(ins, outs):
        for rows in _row_blocks(tm):
            outs[0][rows, :] = (_sigmoid(ins[0][rows, :]) * ins[3][rows, :]
                                + _sigmoid(ins[2][rows, :]) * ins[4][rows, :]).astype(BF16)
        return outs[0][...]

    def epilogue(acc, ins, outs, scr):
        for rows in _row_blocks(tm):
            r = ALPHA * ins[5][rows, :] + acc[rows, :]
            outs[1][rows, :] = r
            xhat, _ = _ln(r)
            outs[2][rows, :] = (xhat * ins[6][...] + ins[7][...]).astype(BF16)

    tile = pl.BlockSpec((tm, d), lambda i, j, k: (i, 0))
    vec = pl.BlockSpec((1, d), lambda i, j, k: (0, 0))
    return _mm_fused(
        "mix_ln1", (t // tm, 1, 1), (proj, w_out, proj, y_conv, y_hgrn, x, g, b),
        [pl.BlockSpec((tm, d), lambda i, j, k: (i, 5)), pl.BlockSpec((d, d), lambda i, j, k: (0, 0)),
         pl.BlockSpec((tm, d), lambda i, j, k: (i, 6)), tile, tile, tile, vec, vec],
        [jax.ShapeDtypeStruct((t, d), BF16), jax.ShapeDtypeStruct((t, d), F32), jax.ShapeDtypeStruct((t, d), BF16)],
        [tile, tile, tile], NN, (tm, d), epilogue, lhs=lhs, hosted=hosted)


def _d_mixed_merge_bwd(d_r1b, w_out, proj, y_conv, y_hgrn, hosted=None):
    t = proj.shape[0]
    tm = _pick(t, 512)
    d = D_MODEL

    def epilogue(d_mixed, ins, outs, scr):
        dy_ref, dmz_ref = outs
        for rows in _row_blocks(tm):
            dm = d_mixed[rows, :]
            for br in range(2):
                sg = _sigmoid(ins[2 + br][rows, :])
                dy_ref[br, rows, :] = (sg * dm).astype(BF16)
                dmz_ref[rows, br * d:(br + 1) * d] = (dm * ins[4 + br][rows, :] * sg * (1.0 - sg)).astype(BF16)

    tile = pl.BlockSpec((tm, d), lambda i, j, k: (i, 0))
    return _mm_fused(
        "d_mixed_merge_bwd", (t // tm, 1, 1), (d_r1b, w_out, proj, proj, y_conv, y_hgrn),
        [tile, pl.BlockSpec((d, d), lambda i, j, k: (0, 0)), pl.BlockSpec((tm, d), lambda i, j, k: (i, 5)),
         pl.BlockSpec((tm, d), lambda i, j, k: (i, 6)), tile, tile],
        [jax.ShapeDtypeStruct((2, t, d), BF16), jax.ShapeDtypeStruct((t, IN_COLS), BF16)],
        [pl.BlockSpec((2, tm, d), lambda i, j, k: (0, i, 0)),
         pl.BlockSpec((tm, 2 * d), lambda i, j, k: (i, DP_MERGE_BLOCK))],
        NT, (tm, d), epilogue, hosted=hosted)


def _ffn_out_ln2(act, w_ffn_out, r1, target, g1, b1, g2, b2):
    t = r1.shape[0]
    tm = _pick(t, 1024)
    nt = t // tm
    d = D_MODEL

    def epilogue(y_ffn, ins, outs, scr):
        r1_ref, tg_ref, g1_ref, b1_ref, g2_ref, b2_ref = ins[2:]
        dr_ref, drb_ref, sums_ref = outs
        (sq_ref,) = scr
        i = pl.program_id(0)

        @pl.when(i == 0)
        def _():
            sums_ref[...] = jnp.zeros_like(sums_ref)
            sq_ref[...] = jnp.zeros_like(sq_ref)

        for rows in _row_blocks(tm):
            xh1, _ = _ln(r1_ref[rows, :])
            x1 = xh1 * g1_ref[...] + b1_ref[...]
            xh2, rstd2 = _ln(ALPHA * x1 + y_ffn[rows, :])
            diff = xh2 * g2_ref[...] + b2_ref[...] - tg_ref[rows, :]
            dy = diff * (1.0 / D_MODEL)
            dr = _ln_bwd(dy, xh2, rstd2, g2_ref[...])
            dr_ref[rows, :] = dr
            drb_ref[rows, :] = dr.astype(BF16)
            sums_ref[0:1, :] += _colsum(dy * xh2)
            sums_ref[1:2, :] += _colsum(dy)
            sq_ref[...] += _colsum(diff * diff)

        @pl.when(i == nt - 1)
        def _():
            total = jnp.sum(sq_ref[...], axis=-1, keepdims=True) * (0.5 / D_MODEL)
            sums_ref[2:3, :] = jnp.broadcast_to(total, (1, D_MODEL))

    tile = pl.BlockSpec((tm, d), lambda i, j, k: (i, 0))
    vec = pl.BlockSpec((1, d), lambda i, j, k: (0, 0))
    return _mm_fused(
        "ffn_out_ln2", (nt, 1, 4), (act, w_ffn_out, r1, target, g1, b1, g2, b2),
        [pl.BlockSpec((None, tm, FF_SHARD), lambda i, j, k: (k, i, 0)),
         pl.BlockSpec((None, FF_SHARD, d), lambda i, j, k: (k, 0, 0)), tile, tile, vec, vec, vec, vec],
        [jax.ShapeDtypeStruct((t, d), F32), jax.ShapeDtypeStruct((t, d), BF16), jax.ShapeDtypeStruct((8, d), F32)],
        [tile, tile, pl.BlockSpec((8, d), lambda i, j, k: (0, 0))], NN, (tm, d), epilogue,
        scratch=[pltpu.VMEM((1, d), F32)])


def _d_x1_ln1_bwd(d_z, w_ffn_in, d_r2, r1, g1):
    t = r1.shape[0]
    tm = _pick(t, 1024)
    d = D_MODEL

    def epilogue(dx_ffn, ins, outs, scr):
        dr2_ref, r1_ref, g_ref = ins[2:]
        dr1_ref, dr1b_ref, sums_ref = outs
        i = pl.program_id(0)

        @pl.when(i == 0)
        def _():
            sums_ref[...] = jnp.zeros_like(sums_ref)

        for rows in _row_blocks(tm):
            xhat, rstd = _ln(r1_ref[rows, :])
            dx1 = ALPHA * dr2_ref[rows, :] + dx_ffn[rows, :]
            dr1 = _ln_bwd(dx1, xhat, rstd, g_ref[...])
            dr1_ref[rows, :] = dr1
            dr1b_ref[rows, :] = dr1.astype(BF16)
            sums_ref[0:1, :] += _colsum(dx1 * xhat)
            sums_ref[1:2, :] += _colsum(dx1)

    tile = pl.BlockSpec((tm, d), lambda i, j, k: (i, 0))
    return _mm_fused(
        "d_x1_ln1_bwd", (t // tm, 1, N_DEV), (d_z, w_ffn_in, d_r2, r1, g1),
        [pl.BlockSpec((None, tm, FF_SHARD), lambda i, j, k: (k, i, 0)),
         pl.BlockSpec((None, d, FF_SHARD), lambda i, j, k: (k, 0, 0)), tile, tile,
         pl.BlockSpec((1, d), lambda i, j, k: (0, 0))],
        [jax.ShapeDtypeStruct((t, d), F32), jax.ShapeDtypeStruct((t, d), BF16), jax.ShapeDtypeStruct((8, d), F32)],
        [tile, tile, pl.BlockSpec((8, d), lambda i, j, k: (0, 0))], NT, (tm, d), epilogue)


def _cast_bf16(x):
    t = x.shape[0]
    tm = _pick(t, 512)

    def body(x_ref, o_ref):
        o_ref[...] = x_ref[...].astype(BF16)

    tile = pl.BlockSpec((tm, D_MODEL), lambda i: (i, 0))
    return pl.pallas_call(
        body, name="cast_x", grid=(t // tm,), in_specs=[tile], out_specs=tile,
        out_shape=jax.ShapeDtypeStruct((t, D_MODEL), BF16), compiler_params=_params(("parallel",)),
    )(x)


def _relayout(name, a, in_block, in_map, out_block, out_map, out_shape):
    def body(a_ref, o_ref):
        o_ref[...] = a_ref[...].astype(o_ref.dtype)

    return pl.pallas_call(
        body, name=name, grid=(N_DEV,), in_specs=[pl.BlockSpec(in_block, in_map)],
        out_specs=pl.BlockSpec(out_block, out_map), out_shape=out_shape, compiler_params=_params(("parallel",)),
    )(a)


_GELU_C = math.sqrt(2.0 / math.pi)


_GELU_CUBIC = 0.044715


def _gelu_parts(u):
    u2 = u * u
    th = jnp.tanh(u * (_GELU_C + (_GELU_C * _GELU_CUBIC) * u2))
    hu = 0.5 * u
    return th, hu + hu * th, u2, hu


BF16_ROWS = 16


def _ffn_act_fwd(z, w_dw, b_dw):
    t = z.shape[2]
    tm = _pick(t, 1024)
    nh = tm // FFN_HALO

    def body(z_ref, zh_ref, w_ref, b_ref, act_ref, gd_ref, us_ref):
        i = pl.program_id(1)
        us_ref[0:FFN_HALO, :] = jnp.where(i == 0, 0.0, zh_ref[...])
        us_ref[FFN_HALO:FFN_HALO + tm, :] = z_ref[0]
        for r in range(tm // ROW_BLOCK):
            base = r * ROW_BLOCK
            rows = slice(base, base + ROW_BLOCK)
            for lanes in _lane_blocks(FF_SHARD):
                uc = b_ref[:, lanes]
                for k in range(FFN_K):
                    off = base + FFN_HALO - (FFN_K - 1) + k
                    uc = uc + w_ref[k:k + 1, lanes] * us_ref[off:off + ROW_BLOCK, lanes]
                th, gelu, u2, hu = _gelu_parts(uc)
                dgelu = (0.5 + 0.5 * th) + (hu - hu * th * th) * (_GELU_C + (3.0 * _GELU_C * _GELU_CUBIC) * u2)
                act_ref[rows, lanes] = (gelu * z_ref[1, rows, lanes]).astype(BF16)
                gd_ref[0, rows, lanes] = gelu.astype(BF16)
                gd_ref[1, rows, lanes] = dgelu.astype(BF16)

    return pl.pallas_call(
        body, name="ffn_act_fwd", grid=(4, t // tm),
        in_specs=[pl.BlockSpec((2, None, tm, FF_SHARD), lambda j, i: (0, j, i, 0)),
                  pl.BlockSpec((None, None, FFN_HALO, FF_SHARD), lambda j, i: (0, j, jnp.maximum(i * nh - 1, 0), 0)),
                  pl.BlockSpec((None, FFN_K, FF_SHARD), lambda j, i: (j, 0, 0)),
                  pl.BlockSpec((None, 1, FF_SHARD), lambda j, i: (j, 0, 0))],
        out_specs=[pl.BlockSpec((None, tm, FF_SHARD), lambda j, i: (j, i, 0)),
                   pl.BlockSpec((2, None, tm, FF_SHARD), lambda j, i: (0, j, i, 0))],
        out_shape=[jax.ShapeDtypeStruct((4, t, FF_SHARD), BF16), jax.ShapeDtypeStruct((2, 4, t, FF_SHARD), BF16)],
        scratch_shapes=[pltpu.VMEM((FFN_HALO + tm, FF_SHARD), F32)],
        compiler_params=_params(("parallel", "arbitrary")),
    )(z, z, w_dw, b_dw)


def _ffn_act_bwd(d_act, z, gd, w_dw):
    t = z.shape[2]
    tm = _pick(t, 1024)
    nt = t // tm
    nh = tm // FFN_HALO
    last_h = t // FFN_HALO - 1
    pad = FFN_HALO - (FFN_K - 1)

    def fold(x):
        return functools.reduce(jnp.add, [x[r:r + SUBLANES, :] for r in range(0, x.shape[0], SUBLANES)])

    def body(da_ref, dah_ref, z_ref, zp_ref, gn_ref, gd_ref, gdn_ref, w_ref, dz_ref, sums_ref, us_ref, ds_ref,
             part_ref):
        i = pl.program_id(1)

        @pl.when(i == 0)
        def _():
            part_ref[...] = jnp.zeros_like(part_ref)

        us_ref[0:FFN_HALO, :] = jnp.where(i == 0, 0.0, zp_ref[...])
        us_ref[FFN_HALO:FFN_HALO + tm, :] = z_ref[0]
        for r in range(tm // ROW_BLOCK):
            base = r * ROW_BLOCK
            rows = slice(base, base + ROW_BLOCK)
            for lanes in _lane_blocks(FF_SHARD):
                da = da_ref[rows, lanes]
                dz_ref[1, rows, lanes] = (da * gd_ref[0, rows, lanes].astype(F32)).astype(BF16)
                duc = da * z_ref[1, rows, lanes] * gd_ref[1, rows, lanes].astype(F32)
                ds_ref[rows, lanes] = duc
                for k in range(FFN_K):
                    part_ref[k, :, lanes] += fold(duc * us_ref[base + pad + k:base + pad + k + ROW_BLOCK, lanes])
                part_ref[FFN_K, :, lanes] += fold(duc)
        duc_next = dah_ref[...] * gn_ref[...] * gdn_ref[0:FFN_HALO, :].astype(F32)
        ds_ref[tm:tm + FFN_HALO, :] = jnp.where(i == nt - 1, 0.0, duc_next)
        for r in range(tm // ROW_BLOCK):
            base = r * ROW_BLOCK
            for lanes in _lane_blocks(FF_SHARD):
                du = None
                for k in range(FFN_K):
                    off = base + FFN_K - 1 - k
                    term = w_ref[k:k + 1, lanes] * ds_ref[off:off + ROW_BLOCK, lanes]
                    du = term if du is None else du + term
                dz_ref[0, base:base + ROW_BLOCK, lanes] = du.astype(BF16)

        @pl.when(i == nt - 1)
        def _():
            sums_ref[...] = jnp.zeros_like(sums_ref)
            for k in range(FFN_K + 1):
                sums_ref[k:k + 1, :] = _colsum(part_ref[k])

    nxt = lambda i: jnp.minimum((i + 1) * nh, last_h)
    nxt_bf = lambda i: jnp.minimum((i + 1) * (tm // BF16_ROWS), t // BF16_ROWS - 1)
    return pl.pallas_call(
        body, name="ffn_act_bwd", grid=(4, nt),
        in_specs=[pl.BlockSpec((None, tm, FF_SHARD), lambda j, i: (j, i, 0)),
                  pl.BlockSpec((None, FFN_HALO, FF_SHARD), lambda j, i: (j, nxt(i), 0)),
                  pl.BlockSpec((2, None, tm, FF_SHARD), lambda j, i: (0, j, i, 0)),
                  pl.BlockSpec((None, None, FFN_HALO, FF_SHARD), lambda j, i: (0, j, jnp.maximum(i * nh - 1, 0), 0)),
                  pl.BlockSpec((None, None, FFN_HALO, FF_SHARD), lambda j, i: (1, j, nxt(i), 0)),
                  pl.BlockSpec((2, None, tm, FF_SHARD), lambda j, i: (0, j, i, 0)),
                  pl.BlockSpec((None, None, BF16_ROWS, FF_SHARD), lambda j, i: (1, j, nxt_bf(i), 0)),
                  pl.BlockSpec((None, FFN_K, FF_SHARD), lambda j, i: (j, 0, 0))],
        out_specs=[pl.BlockSpec((2, None, tm, FF_SHARD), lambda j, i: (0, j, i, 0)),
                   pl.BlockSpec((None, 8, FF_SHARD), lambda j, i: (j, 0, 0))],
        out_shape=[jax.ShapeDtypeStruct((2, 4, t, FF_SHARD), BF16), jax.ShapeDtypeStruct((4, 8, FF_SHARD), F32)],
        scratch_shapes=[pltpu.VMEM((FFN_HALO + tm, FF_SHARD), F32), pltpu.VMEM((tm + FFN_HALO, FF_SHARD), F32),
                        pltpu.VMEM((FFN_K + 1, SUBLANES, FF_SHARD), F32)],
        compiler_params=_params(("parallel", "arbitrary")),
    )(d_act, d_act, z, z, z, gd, gd, w_dw)


_HGRN_COLS = 4 * HGRN_DIM


def _to_backward_order(w):
    heads = w[:, 2 * CONV_DIM:2 * CONV_DIM + _HGRN_COLS].reshape(-1, 4, HEADS, HEAD_DIM)
    heads = jnp.swapaxes(heads, 1, 2).reshape(-1, _HGRN_COLS)
    return jnp.concatenate([w[:, 2 * CONV_DIM + _HGRN_COLS:], w[:, :2 * CONV_DIM], heads], axis=1)


def _from_backward_order(w):
    heads = w[:, 2 * D_MODEL + 2 * CONV_DIM:].reshape(-1, HEADS, 4, HEAD_DIM)
    heads = jnp.swapaxes(heads, 1, 2).reshape(-1, _HGRN_COLS)
    return jnp.concatenate([w[:, 2 * D_MODEL:2 * D_MODEL + 2 * CONV_DIM], heads, w[:, :2 * D_MODEL]], axis=1)
def _local_step(x, target, weights, small, scatter=None, order=None):
    t = x.shape[0]
    tm = _pick(t, 2048)
    tk = _pick(t, 2048)
    nm = t // tm
    nk = t // tk
    d = D_MODEL

    xb = _cast_bf16(x)
    ffn_in_gather = ffn_out_gather = None
    if isinstance(weights, tuple) and isinstance(weights[0], _Hosted):
        first_gather, ffn_in_gather, ffn_out_gather = weights
        proj, (w_in, w_conv_out8, w_hgrn_out8, w_out8, conv_dw8, ffn_dw8) = _proj_gather(xb, first_gather, order)
    else:
        w_in, w_conv_out8, w_hgrn_out8, w_out8, w_ffn_in, w_ffn_out8, conv_dw8, ffn_dw8 = weights
        proj = _mm("proj", xb, w_in, (t, IN_COLS), F32, (nm, N_DEV, 1),
                   pl.BlockSpec((tm, d), lambda i, j, k: (i, 0)),
                   pl.BlockSpec((None, d, IN_SHARD), lambda i, j, k: (j, 0, 0)),
                   pl.BlockSpec((tm, IN_SHARD), lambda i, j, k: (i, j)), NN, (tm, IN_SHARD))
    o, og, states, *late = _hgrn_fwd(proj, small["hgrn_lb_logits"], small["hgrn_norm_g"], hosted=ffn_in_gather)
    if ffn_in_gather is not None:
        (w_ffn_in,) = late
    w_conv_out = _relayout("w_conv_out_natural", w_conv_out8, (None, CONV_DIM, 128), lambda j: (j, 0, 0),
                           (CONV_DIM, 128), lambda j: (0, j), jax.ShapeDtypeStruct((CONV_DIM, d), BF16))
    w_hgrn_out = w_hgrn_out8.reshape(d, d)
    w_out = w_out8.reshape(d, d)
    conv_dw =jnp.transpose(conv_dw8[:, :CONV_K, :CONV_DIM // N_DEV], (1, 0, 2)).reshape(CONV_K, CONV_DIM)
    ffn_dw = jnp.transpose(ffn_dw8[:, :FFN_K, :D_FF // N_DEV], (1, 0, 2)).reshape(FFN_K, 4, FF_SHARD)
    small = dict(small, w_conv_dw=conv_dw, w_ffn_dw=jnp.transpose(ffn_dw, (1, 0, 2)),
                 b_ffn_dw=small["b_ffn_dw"].reshape(4, 1, FF_SHARD))

    c_act, conv_pre = _conv_fwd(proj, small["w_conv_dw"], small["b_conv_dw"], small["conv_ln_g"], small["conv_ln_b"])
    y_conv = _mm("y_conv", c_act, w_conv_out, (t, d), F32, (nm, 1, 1),
                 pl.BlockSpec((tm, CONV_DIM), lambda i, j, k: (i, 0)),
                 pl.BlockSpec((CONV_DIM, d), lambda i, j, k: (0, 0)),
                 pl.BlockSpec((tm, d), lambda i, j, k: (i, 0)), NN, (tm, d))
    sq_w = pl.BlockSpec((d, d), lambda i, j, k: (0, 0))
    row_tile = pl.BlockSpec((tm, d), lambda i, j, k: (i, 0))
    y_hgrn = _mm("y_hgrn", og, w_hgrn_out, (t, d), F32, (nm, 1, 1), row_tile, sq_w, row_tile, NN, (tm, d))
    mixed, r1, x1b, *late = _mix_ln1(proj, y_conv, y_hgrn, w_out, x, small["ln1_g"], small["ln1_b"],
                                     hosted=ffn_out_gather)
    if ffn_out_gather is not None:
        (w_ffn_out8,) = late
    w_ffn_out = w_ffn_out8.reshape(4, FF_SHARD, d)
    z = _mm("ffn_in", x1b, w_ffn_in, (N_DEV, t, FF_SHARD), F32, (nm, N_DEV, 1), row_tile,
            pl.BlockSpec((None, d, FF_SHARD), lambda i, j, k: (j, 0, 0)),
            pl.BlockSpec((None, tm, FF_SHARD), lambda i, j, k: (j, i, 0)), NN, (tm, FF_SHARD))
    z = z.reshape(2, 4, t, FF_SHARD)
    act, gelu_and_slope = _ffn_act_fwd(z, small["w_ffn_dw"], small["b_ffn_dw"])

    d_r2, d_r2b, sums_ln2 = _ffn_out_ln2(act, w_ffn_out, r1, target, small["ln1_g"], small["ln1_b"],
                                         small["ln2_g"], small["ln2_b"])
    d_act = _mm("d_act", d_r2b, w_ffn_out, (4, t, FF_SHARD), F32, (nm, 4, 1), row_tile,
                pl.BlockSpec((None, FF_SHARD, d), lambda i, j, k: (j, 0, 0)),
                pl.BlockSpec((None, tm, FF_SHARD), lambda i, j, k: (j, i, 0)), NT, (tm, FF_SHARD))
    g_w_ffn_out = _mm("g_w_ffn_out", act, d_r2b, (4, FF_SHARD, d), BF16, (4, 1, nk),
                      pl.BlockSpec((None, tk, FF_SHARD), lambda i, j, k: (i, k, 0)),
                      pl.BlockSpec((tk, d), lambda i, j, k: (k, 0)),
                      pl.BlockSpec((None, FF_SHARD, d), lambda i, j, k: (i, 0, 0)), TN, (FF_SHARD, d))
    d_z, sums_ffn = _ffn_act_bwd(d_act, z, gelu_and_slope, small["w_ffn_dw"])
    d_z8 = d_z.reshape(N_DEV, t, FF_SHARD)
    d_r1, d_r1b, sums_ln1 = _d_x1_ln1_bwd(d_z8, w_ffn_in, d_r2, r1, small["ln1_g"])
    g_w_ffn_in = _mm("g_w_ffn_in", x1b, d_z8, (N_DEV, d, FF_SHARD), BF16, (N_DEV, 1, nk),
                     pl.BlockSpec((tk, d), lambda i, j, k: (k, 0)),
                     pl.BlockSpec((None, tk, FF_SHARD), lambda i, j, k: (i, k, 0)),
                     pl.BlockSpec((None, d, FF_SHARD), lambda i, j, k: (i, 0, 0)), TN, (d, FF_SHARD))
    k_tile = pl.BlockSpec((tk, d), lambda i, j, k: (k, 0))
    g_w_out = _mm("g_w_out", mixed, d_r1b, (d, d), BF16, (1, 1, nk), k_tile, k_tile, sq_w, TN, (d, d))
    send = (lambda grads: None) if scatter is None else scatter
    g_w_ffn_out = g_w_ffn_out.reshape(N_DEV, D_FF // N_DEV, d)
    d_y, d_proj, *recv_ffn_out = _d_mixed_merge_bwd(d_r1b, w_out, proj, y_conv, y_hgrn, hosted=send([g_w_ffn_out]))
    d_pre, sums_conv = _d_c_norm_bwd(d_y, w_conv_out, conv_pre, small["conv_ln_g"], small["conv_ln_b"])
    g_w_conv_out = _mm("g_w_conv_out", c_act, d_y, (CONV_DIM, d), BF16, (1, 1, nk),
                       pl.BlockSpec((tk, CONV_DIM), lambda i, j, k: (k, 0)),
                       pl.BlockSpec((None, tk, d), lambda i, j, k: (0, k, 0)),
                       pl.BlockSpec((CONV_DIM, d), lambda i, j, k: (0, 0)), TN, (CONV_DIM, d))
    g_w_conv_out = _relayout("g_w_conv_out_shards", g_w_conv_out, (CONV_DIM, 128), lambda j: (0, j),
                             (None, CONV_DIM, 128), lambda j: (j, 0, 0),
                             jax.ShapeDtypeStruct((N_DEV, CONV_DIM, 128), BF16))
    d_og = _mm("d_og", d_y, w_hgrn_out, (t, d), F32, (nm, 1, 1),
               pl.BlockSpec((None, tm, d), lambda i, j, k: (1, i, 0)), sq_w, row_tile, NT, (tm, d))
    g_w_hgrn_out = _mm("g_w_hgrn_out", og, d_y, (d, d), BF16, (1, 1, nk), k_tile,
                       pl.BlockSpec((None, tk, d), lambda i, j, k: (1, k, 0)), sq_w, TN, (d, d))
    d_proj, g_w_conv_dw, *recv_ffn_in = _conv_bwd_dw(d_pre, proj, small["w_conv_dw"], d_proj,
                                                     hosted=send([g_w_ffn_in]))
    early = [g_w_conv_out, g_w_hgrn_out.reshape(N_DEV, d // N_DEV, d), g_w_out.reshape(N_DEV, d // N_DEV, d)]
    d_proj, sums_hgrn, *early_recv = _hgrn_bwd(d_og, o, proj, states, small["hgrn_lb_logits"], small["hgrn_norm_g"],
                                               d_proj, hosted=send(early))
    early += [g_w_ffn_in, g_w_ffn_out]
    early_recv += recv_ffn_in + recv_ffn_out
    wide = IN_COLS // 4
    w_in_bwd = _to_backward_order(jnp.transpose(w_in, (1, 0, 2)).reshape(d, IN_COLS))
    g_w_in = _mm("g_w_in", xb, d_proj, (d, IN_COLS), BF16, (4, 1, nk), k_tile,
                 pl.BlockSpec((tk, wide), lambda i, j, k: (k, i)),
                 pl.BlockSpec((d, wide), lambda i, j, k: (0, i)), TN, (d, wide))
    g_w_in = jnp.transpose(_from_backward_order(g_w_in).reshape(d, N_DEV, IN_SHARD), (1, 0, 2))
    def add_residual(acc, ins, outs, scr):
        for rows in _row_blocks(ta):
            outs[0][rows, :] = ALPHA * ins[2][rows, :] + acc[rows, :]

    ta = _pick(t, 1024)
    acc_tile = pl.BlockSpec((ta, d), lambda i, j, k: (i, 0))
    grad_x, *late_recv = _mm_fused(
        "grad_x", (t // ta, 1, 4), (d_proj, w_in_bwd, d_r1),
        [pl.BlockSpec((ta, wide), lambda i, j, k: (i, k)), pl.BlockSpec((d, wide), lambda i, j, k: (0, k)), acc_tile],
        [jax.ShapeDtypeStruct((t, d), F32)], [acc_tile], NT, (ta, d), add_residual,
        hosted=None if scatter is None else scatter([g_w_in]))

    large_grads = [g_w_in] + early
    if scatter is not None:
        large_grads = list(zip(large_grads, late_recv + early_recv))
    return grad_x, large_grads, (sums_ln2, sums_conv, sums_hgrn, sums_ln1, sums_ffn, g_w_conv_dw)


def _small_views(sums):
    sums_ln2, sums_conv, sums_hgrn, sums_ln1, sums_ffn, g_w_conv_dw = sums
    d_l0 = sums_hgrn[1:2]
    return {
        "loss": sums_ln2[2:3, 0:128],
        "b_conv_dw": sums_conv[2:3], "conv_ln_g": sums_conv[0:1], "conv_ln_b": sums_conv[1:2],
        "hgrn_lb_logits": jnp.concatenate([d_l0, -d_l0], axis=1),
        "hgrn_norm_g": sums_hgrn[0:1],
        "ln1_g": sums_ln1[0:1], "ln1_b": sums_ln1[1:2],
        "b_ffn_dw": sums_ffn[:, FFN_K, :].reshape(1, D_FF),
        "ln2_g": sums_ln2[0:1], "ln2_b": sums_ln2[1:2],
        "w_conv_dw": g_w_conv_dw[0:CONV_K].reshape(1, CONV_K * CONV_DIM),
        "w_ffn_dw": jnp.transpose(sums_ffn[:, 0:FFN_K, :], (1, 0, 2)).reshape(1, FFN_K * D_FF),
    }


def _coords():
    return lax.axis_index("x"), lax.axis_index("y"), lax.axis_index("c")


def _gather(shards, staged=False):
    n = len(shards)
    later = range(1 if staged else 0, n)

    def parts(ins, outs, sems):
        send_sems, recv_sems, local_sems = sems
        x, y, c = _coords()
        me = 4 * x + 2 * y + c
        sibling = (x, y, 1 - c)
        chips = [(1 - x, y), (x, 1 - y), (1 - x, 1 - y)]

        def copy(a, k, block, to, src=None):
            return pltpu.make_async_remote_copy(
                src_ref=outs[a].at[block] if src is None else src, dst_ref=outs[a].at[block],
                send_sem=send_sems.at[a, k], recv_sem=recv_sems.at[a, k], device_id=to, device_id_type=MESH)

        local = [pltpu.make_async_copy(ins[a], outs[a].at[me], local_sems.at[a]) for a in range(n)]
        first = []
        for a in range(n):
            first.append(copy(a, 0, me, sibling, src=ins[a]))
            for j, chip in enumerate(chips):
                first.append(copy(a, 1 + j, me, (*chip, c), src=ins[a]))
        return x, y, c, sibling, chips, copy, local, first

    def start(ins, outs, sems):
        *_, local, first = parts(ins, outs, sems)
        for cp in local + first:
            cp.start()

    def arrive(ins, outs, sems, s):
        x, y, c, sibling, chips, copy, _, _ = parts(ins, outs, sems)
        if s == 1:
            block = 4 * x + 2 * y + 1 - c
            copy(0, 0, block, sibling).wait_recv()
        elif s <= 4:
            px, py = chips[s - 2]
            block = 4 * px + 2 * py + c
            copy(0, s - 1, block, sibling).wait_recv()
            copy(0, s + 2, block, sibling).start()
        else:
            px, py = chips[s - 5]
            block = 4 * px + 2 * py + 1 - c
            copy(0, s - 1, block, sibling).wait_recv()
        return block

    def middle(ins, outs, sems):
        x, y, c, sibling, chips, copy, _, _ = parts(ins, outs, sems)
        for j, (px, py) in enumerate(chips):
            for a in later:
                copy(a, 1 + j, 4 * px + 2 * py + c, sibling).wait_recv()
                copy(a, 4 + j, 4 * px + 2 * py + c, sibling).start()

    def finish(ins, outs, sems):
        x, y, c, sibling, chips, copy, local, first = parts(ins, outs, sems)
        passed = [copy(a, 4 + j, 4 * px + 2 * py + c, sibling) for a in range(n) for j, (px, py) in enumerate(chips)]
        for a in later:
            copy(a, 0, 4 * x + 2 * y + 1 - c, sibling).wait_recv()
            for j, (px, py) in enumerate(chips):
                copy(a, 4 + j, 4 * px + 2 * py + 1 - c, sibling).wait_recv()
        for cp in first + passed:
            cp.wait_send()
        for cp in local:
            cp.wait()

    hosted = _Hosted(shards, [jax.ShapeDtypeStruct((N_DEV,) + s.shape, s.dtype) for s in shards],
                     [pltpu.SemaphoreType.DMA((n, 7)), pltpu.SemaphoreType.DMA((n, 7)), pltpu.SemaphoreType.DMA((n,))],
                     start, finish, middle)
    hosted.arrive = arrive
    return hosted


def _proj_gather(xb, gather, order):
    t, d = xb.shape
    tm = _pick(t, 2048)
    nm = t // tm
    n_in, n_out = len(gather.inputs), len(gather.out_shapes)

    def body(order_ref, x_ref, *refs):
        ins, refs = refs[:n_in], refs[n_in:]
        o_ref, outs, refs = refs[0], refs[1:1 + n_out], refs[1 + n_out:]
        w_buf, w_sem, sems = refs[0], refs[1], refs[2:]
        s, i = pl.program_id(0), pl.program_id(1)

        @pl.when((s == 0) & (i == 0))
        def _():
            gather.start(ins, outs, sems)

        def staging(step, src):
            return pltpu.make_async_copy(src, w_buf.at[step % 2], w_sem.at[step % 2])

        @pl.when((s == 0) & (i == 0))
        def _():
            staging(0, ins[0]).start()

        for step in range(1, N_DEV):
            @pl.when((s == step - 1) & (i == nm - 1))
            def _(step=step):
                staging(step, outs[0].at[gather.arrive(ins, outs, sems, step)]).start()

        for step in range(N_DEV):
            @pl.when((s == step) & (i == 0))
            def _(step=step):
                staging(step, ins[0]).wait()

        o_ref[...] = _dot(x_ref[...], w_buf[s % 2], NN)

        @pl.when((s == N_DEV - 2) & (i == 0))
        def _():
            gather.middle(ins, outs, sems)

        @pl.when((s == N_DEV - 1) & (i == nm - 1))
        def _():
            gather.finish(ins, outs, sems)

    outs = pl.pallas_call(
        body, name="proj_gather",
        grid_spec=pltpu.PrefetchScalarGridSpec(
            num_scalar_prefetch=1, grid=(N_DEV, nm),
            in_specs=[pl.BlockSpec((tm, d), lambda s, i, order_ref: (i, 0))] + [ANY] * n_in,
            out_specs=[pl.BlockSpec((tm, IN_SHARD), lambda s, i, order_ref: (i, order_ref[s]))] + [ANY] * n_out,
            scratch_shapes=[pltpu.VMEM((2, d, IN_SHARD), BF16), pltpu.SemaphoreType.DMA((2,))] + gather.sem_shapes),
        out_shape=[jax.ShapeDtypeStruct((t, IN_COLS), F32)] + gather.out_shapes,
        compiler_params=pltpu.CompilerParams(dimension_semantics=("arbitrary", "arbitrary"),
                                             vmem_limit_bytes=VMEM_LIMIT, has_side_effects=True),
    )(order, xb, *gather.inputs)
    return outs[0], list(outs[1:])


def _scatter(grads):
    n = len(grads)

    def copies(ins, outs, sems):
        send_sems, recv_sems = sems
        x, y, c = _coords()
        out = []
        for a in range(n):
            for k in range(1, N_DEV):
                px, py, pc = x ^ (k >> 2), y ^ ((k >> 1) & 1), c ^ (k & 1)
                out.append(pltpu.make_async_remote_copy(
                    src_ref=ins[a].at[4 * px + 2 * py + pc], dst_ref=outs[a].at[k - 1],
                    send_sem=send_sems.at[a, k - 1], recv_sem=recv_sems.at[a, k - 1],
                    device_id=(px, py, pc), device_id_type=MESH))
        return out

    def start(ins, outs, sems):
        for cp in copies(ins, outs, sems):
            cp.start()

    def finish(ins, outs, sems):
        for cp in copies(ins, outs, sems):
            cp.wait()

    return _Hosted(grads, [jax.ShapeDtypeStruct((N_DEV - 1,) + g.shape[1:], g.dtype) for g in grads],
                   [pltpu.SemaphoreType.DMA((n, N_DEV - 1)), pltpu.SemaphoreType.DMA((n, N_DEV - 1))], start, finish)


def _row_tile(rows):
    return 256 if rows % 256 == 0 else rows


def _adam_math(w, g, m, v):
    m_new = ADAM_B1 * m + (1.0 - ADAM_B1) * g
    v_new = ADAM_B2 * v + (1.0 - ADAM_B2) * (g * g)
    m_hat = m_new / (1.0 - ADAM_B1 ** ADAM_STEP)
    v_hat = v_new / (1.0 - ADAM_B2 ** ADAM_STEP)
    delta = -ADAM_LR * (m_hat / (jnp.sqrt(v_hat) + ADAM_EPS) + ADAM_WD * w)
    return delta, m_new, v_new


def _adam_large(name, own, recv, me, w, m, v):
    rows, cols = w.shape
    tr = _row_tile(rows)

    def body(me_ref, p_ref, r_ref, w_ref, m_ref, v_ref, g_out, d_out, m_out, v_out):
        g = p_ref[...].astype(F32)
        for k in range(N_DEV - 1):
            g = g + r_ref[k].astype(F32)
        delta, m_new, v_new = _adam_math(w_ref[...], g, m_ref[...], v_ref[...])
        g_out[...] = g
        d_out[...] = delta
        m_out[...] = m_new
        v_out[...] = v_new

    tile = pl.BlockSpec((tr, cols), lambda r, me_ref: (r, 0))
    sds = jax.ShapeDtypeStruct((rows, cols), F32)
    return pl.pallas_call(
        body, name=name,
        grid_spec=pltpu.PrefetchScalarGridSpec(
            num_scalar_prefetch=1, grid=(rows // tr,),
            in_specs=[pl.BlockSpec((None, tr, cols), lambda r, me_ref: (me_ref[0], r, 0)),
                      pl.BlockSpec((N_DEV - 1, tr, cols), lambda r, me_ref: (0, r, 0)), tile, tile, tile],
            out_specs=[tile, tile, tile, tile]),
        out_shape=[sds, sds, sds, sds],
        compiler_params=_params(("parallel",)),
    )(me, own, recv, w, m, v)


def _small_allreduce(arrays):
    n = len(arrays)

    def body(*refs):
        ins, outs, gats = refs[:n], refs[n:2 * n], refs[2 * n:3 * n]
        send_sems, recv_sems = refs[3 * n:]
        x, y, c = _coords()
        me = 4 * x + 2 * y + c
        peers = [(x ^ (k >> 2), y ^ ((k >> 1) & 1), c ^ (k & 1)) for k in range(1, N_DEV)]

        def copy(a, k, slot):
            return pltpu.make_async_remote_copy(
                src_ref=ins[a], dst_ref=gats[a].at[slot], send_sem=send_sems.at[a, k], recv_sem=recv_sems.at[a, k],
                device_id=peers[k], device_id_type=MESH)

        sends = [copy(a, k, me) for a in range(n) for k in range(N_DEV - 1)]
        for a in range(n):
            gats[a][me] = ins[a][...]
        for cp in sends:
            cp.start()
        for a in range(n):
            for k, (px, py, pc) in enumerate(peers):
                copy(a, k, 4 * px + 2 * py + pc).wait_recv()
        for cp in sends:
            cp.wait_send()
        for a in range(n):
            acc = gats[a][0]
            for dev in range(1, N_DEV):
                acc = acc + gats[a][dev]
            outs[a][...] = acc

    whole = pl.BlockSpec(memory_space=pltpu.VMEM)
    return pl.pallas_call(
        body, name="small_allreduce", in_specs=[whole] * n, out_specs=[whole] * n,
        out_shape=[jax.ShapeDtypeStruct(a.shape, F32) for a in arrays],
        scratch_shapes=[pltpu.VMEM((N_DEV,) + a.shape, F32) for a in arrays]
        + [pltpu.SemaphoreType.DMA((n, N_DEV - 1)), pltpu.SemaphoreType.DMA((n, N_DEV - 1))],
        compiler_params=pltpu.CompilerParams(has_side_effects=True, vmem_limit_bytes=VMEM_LIMIT),
    )(*arrays)


def _adam_replicated(sums, w, m, v):
    rows_of = {"conv_ln_g": (1, 0), "conv_ln_b": (1, 1), "b_conv_dw": (1, 2), "hgrn_norm_g": (2, 0),
               "ln1_g": (3, 0), "ln1_b": (3, 1), "ln2_g": (0, 0), "ln2_b": (0, 1)}
    names = list(rows_of) + ["hgrn_lb_logits"]
    n = len(names)

    def body(*refs):
        sum_refs, refs = refs[:4], refs[4:]
        w_refs, m_refs, v_refs, outs = refs[:n], refs[n:2 * n], refs[2 * n:3 * n], refs[3 * n:]
        for j, name in enumerate(names):
            if name == "hgrn_lb_logits":
                d_l0 = sum_refs[2][1:2, :]
                grads = [d_l0, -d_l0]
            else:
                a, row = rows_of[name]
                grads = [sum_refs[a][row:row + 1, :]]
            g_out, d_out, m_out, v_out = outs[4 * j:4 * j + 4]
            for r, g in enumerate(grads):
                rows = slice(r, r + 1)
                delta, m_new, v_new = _adam_math(w_refs[j][rows, :], g, m_refs[j][rows, :], v_refs[j][rows, :])
                g_out[rows, :] = g
                d_out[rows, :] = delta
                m_out[rows, :] = m_new
                v_out[rows, :] = v_new

    whole = pl.BlockSpec(memory_space=pltpu.VMEM)
    operands = list(sums) + [w[k] for k in names] + [m[k] for k in names] + [v[k] for k in names]
    outs = pl.pallas_call(
        body, name="adam_replicated", in_specs=[whole] * len(operands), out_specs=[whole] * (4 * n),
        out_shape=[jax.ShapeDtypeStruct(w[k].shape, F32) for k in names for _ in range(4)],
    )(*operands)
    return {name: tuple(outs[4 * j:4 * j + 4]) for j, name in enumerate(names)}


def _adam_small(w, g, m, v):
    def body(w_ref, g_ref, m_ref, v_ref, d_out, m_out, v_out):
        delta, m_new, v_new = _adam_math(w_ref[...], g_ref[...], m_ref[...], v_ref[...])
        d_out[...] = delta
        m_out[...] = m_new
        v_out[...] = v_new

    whole = pl.BlockSpec(memory_space=pltpu.VMEM)
    sds = jax.ShapeDtypeStruct(w.shape, F32)
    return pl.pallas_call(body, name="adam_small", in_specs=[whole] * 4, out_specs=[whole] * 3,
                          out_shape=[sds, sds, sds])(w, g, m, v)


_WEIGHTS = ["w_in", "w_conv_dw", "b_conv_dw", "conv_ln_g", "conv_ln_b", "w_conv_out", "hgrn_lb_logits", "hgrn_norm_g",
            "w_hgrn_out", "w_out", "ln1_g", "ln1_b", "w_ffn_in", "w_ffn_dw", "b_ffn_dw", "w_ffn_out", "ln2_g", "ln2_b"]
_LARGE = ["w_in", "w_conv_out", "w_hgrn_out", "w_out", "w_ffn_in", "w_ffn_out"]
_CONV_DW_SHARD = CONV_DIM // N_DEV
_FFN_DW_SHARD = D_FF // N_DEV


def kernel(x, w_in, w_conv_dw, b_conv_dw, conv_ln_g, conv_ln_b, w_conv_out, hgrn_lb_logits, hgrn_norm_g, w_hgrn_out, w_out, ln1_g, ln1_b, w_ffn_in, w_ffn_dw, b_ffn_dw, w_ffn_out, ln2_g, ln2_b, loss_target, m_w_in, m_w_conv_dw, m_b_conv_dw, m_conv_ln_g, m_conv_ln_b, m_w_conv_out, m_hgrn_lb_logits, m_hgrn_norm_g, m_w_hgrn_out, m_w_out, m_ln1_g, m_ln1_b, m_w_ffn_in, m_w_ffn_dw, m_b_ffn_dw, m_w_ffn_out, m_ln2_g, m_ln2_b, v_w_in, v_w_conv_dw, v_b_conv_dw, v_conv_ln_g, v_conv_ln_b, v_w_conv_out, v_hgrn_lb_logits, v_hgrn_norm_g, v_w_hgrn_out, v_w_out, v_ln1_g, v_ln1_b, v_w_ffn_in, v_w_ffn_dw, v_b_ffn_dw, v_w_ffn_out, v_ln2_g, v_ln2_b):
    w = dict(w_in=w_in, w_conv_dw=w_conv_dw, b_conv_dw=b_conv_dw, conv_ln_g=conv_ln_g, conv_ln_b=conv_ln_b,
             w_conv_out=w_conv_out, hgrn_lb_logits=hgrn_lb_logits, hgrn_norm_g=hgrn_norm_g, w_hgrn_out=w_hgrn_out,
             w_out=w_out, ln1_g=ln1_g, ln1_b=ln1_b, w_ffn_in=w_ffn_in, w_ffn_dw=w_ffn_dw, b_ffn_dw=b_ffn_dw,
             w_ffn_out=w_ffn_out, ln2_g=ln2_g, ln2_b=ln2_b)
    m = dict(w_in=m_w_in, w_conv_dw=m_w_conv_dw, b_conv_dw=m_b_conv_dw, conv_ln_g=m_conv_ln_g, conv_ln_b=m_conv_ln_b,
             w_conv_out=m_w_conv_out, hgrn_lb_logits=m_hgrn_lb_logits, hgrn_norm_g=m_hgrn_norm_g,
             w_hgrn_out=m_w_hgrn_out, w_out=m_w_out, ln1_g=m_ln1_g, ln1_b=m_ln1_b, w_ffn_in=m_w_ffn_in,
             w_ffn_dw=m_w_ffn_dw, b_ffn_dw=m_b_ffn_dw, w_ffn_out=m_w_ffn_out, ln2_g=m_ln2_g, ln2_b=m_ln2_b)
    v = dict(w_in=v_w_in, w_conv_dw=v_w_conv_dw, b_conv_dw=v_b_conv_dw, conv_ln_g=v_conv_ln_g, conv_ln_b=v_conv_ln_b,
             w_conv_out=v_w_conv_out, hgrn_lb_logits=v_hgrn_lb_logits, hgrn_norm_g=v_hgrn_norm_g,
             w_hgrn_out=v_w_hgrn_out, w_out=v_w_out, ln1_g=v_ln1_g, ln1_b=v_ln1_b, w_ffn_in=v_w_ffn_in,
             w_ffn_dw=v_w_ffn_dw, b_ffn_dw=v_b_ffn_dw, w_ffn_out=v_w_ffn_out, ln2_g=v_ln2_g, ln2_b=v_ln2_b)
    xi, yi, ci = lax.axis_index("x"), lax.axis_index("y"), lax.axis_index("c")
    me = 4 * xi + 2 * yi + ci
    me_op = jnp.reshape(me, (1,)).astype(jnp.int32)

    shards = [w[name][0].astype(BF16) for name in _LARGE]
    shards.append(jnp.pad(w_conv_dw[0], ((0, 1), (0, 128 - _CONV_DW_SHARD))))
    shards.append(jnp.pad(w_ffn_dw[0], ((0, 8 - FFN_K), (0, 384 - _FFN_DW_SHARD))))
    chips = [(1 - xi, yi), (xi, 1 - yi), (1 - xi, 1 - yi)]
    order = jnp.stack([me, me ^ 1] + [4 * px + 2 * py + ci for px, py in chips]
                      + [4 * px + 2 * py + 1 - ci for px, py in chips]).astype(jnp.int32)
    small = dict(b_conv_dw=b_conv_dw, conv_ln_g=conv_ln_g, conv_ln_b=conv_ln_b, hgrn_lb_logits=hgrn_lb_logits,
                 hgrn_norm_g=hgrn_norm_g, ln1_g=ln1_g, ln1_b=ln1_b, ln2_g=ln2_g, ln2_b=ln2_b, b_ffn_dw=b_ffn_dw)

    gathers = (_gather(shards[:4] + shards[6:], staged=True), _gather(shards[4:5]), _gather(shards[5:6]))
    grad_x, large_grads, small_sums = _local_step(x[0], loss_target[0], gathers, small, _scatter, order)

    out = {}
    for name, (own, recv) in zip(_LARGE, large_grads):
        out[name] = _adam_large("adam_" + name, own, recv, me_op, w[name][0], m[name][0], v[name][0])

    totals = _small_allreduce(list(small_sums))
    out.update(_adam_replicated(totals[:4], w, m, v))
    summed = _small_views(totals)
    loss = summed["loss"][0, 0]
    conv_dw_g = lax.dynamic_slice_in_dim(summed["w_conv_dw"].reshape(CONV_K, CONV_DIM), me * _CONV_DW_SHARD, _CONV_DW_SHARD, axis=1)
    ffn_dw_g = lax.dynamic_slice_in_dim(summed["w_ffn_dw"].reshape(FFN_K, D_FF), me * _FFN_DW_SHARD, _FFN_DW_SHARD, axis=1)
    small_g = dict(b_ffn_dw=summed["b_ffn_dw"], w_conv_dw=conv_dw_g.reshape(1, -1), w_ffn_dw=ffn_dw_g.reshape(1, -1))
    names = list(small_g)
    flat = lambda d, n: d[n].reshape(1, -1)
    n_small = sum(small_g[n].shape[1] for n in names)
    pad = (-n_small) % 1024
    pack = lambda pieces: jnp.pad(jnp.concatenate(pieces, axis=1), ((0, 0), (0, pad))).reshape(-1, 128)
    d_s, m_s, v_s = _adam_small(pack([flat(w, n) for n in names]), pack([small_g[n] for n in names]),
                                pack([flat(m, n) for n in names]), pack([flat(v, n) for n in names]))
    pos = 0
    for n in names:
        size = small_g[n].shape[1]
        cut = lambda a: a.reshape(1, -1)[:, pos:pos + size].reshape(w[n].shape)
        out[n] = (small_g[n].reshape(w[n].shape), cut(d_s), cut(m_s), cut(v_s))
        pos += size

    for name in _LARGE:
        out[name] = tuple(a.reshape(w[name].shape) for a in out[name])
    grads = [out[n][0] for n in _WEIGHTS]
    deltas = [out[n][1] for n in _WEIGHTS]
    new_m = [out[n][2] for n in _WEIGHTS]
    new_v = [out[n][3] for n in _WEIGHTS]
    return (loss, grad_x[None], *grads, *deltas, *new_m, *new_v)
```

```python
import functools
import math

import jax
import jax.numpy as jnp
from jax import lax
from jax.experimental import pallas as pl
from jax.experimental.pallas import tpu as pltpu

F32 = jnp.float32
BF16 = jnp.bfloat16

N_DEV = 8
D_MODEL = 1024
CONV_DIM = 512
CONV_K = 31
HGRN_DIM = 1024
HEADS = 8
HEAD_DIM = 128
D_FF = 2816
FFN_K = 3
FF_SHARD = 2 * D_FF // N_DEV
IN_COLS = 7168
IN_SHARD = IN_COLS // N_DEV
LN_EPS = 1e-5
RMS_EPS = 1e-6
ALPHA = 2.0 ** 0.25

ADAM_LR = 0.001
ADAM_B1 = 0.9
ADAM_B2 = 0.999
ADAM_EPS = 1e-08
ADAM_WD = 0.01
ADAM_STEP = 10

CHUNK = 64
CHUNKS_PER_BLOCK = 32
CONV_HALO = 32
FFN_HALO = 8
ROW_BLOCK = 64
SUBLANES = 8
VMEM_LIMIT = 48 * 1024 * 1024
MXU_DEPTH = 256

DP_MERGE_BLOCK = 0
DP_CONV_BLOCK = 2
DP_HEAD_BLOCK = 6

MESH = pl.DeviceIdType.MESH
ANY = pl.BlockSpec(memory_space=pl.ANY)

NN = (((1,), (0,)), ((), ()))
NT = (((1,), (1,)), ((), ()))
TN = (((0,), (0,)), ((), ()))


def _params(sem):
    return pltpu.CompilerParams(dimension_semantics=sem, vmem_limit_bytes=VMEM_LIMIT)


def _dot(a, b, dims):
    return lax.dot_general(a.astype(BF16), b.astype(BF16), dims, preferred_element_type=F32)


def _sigmoid(x):
    return jax.nn.sigmoid(x)


def _ln(r):
    mu = jnp.mean(r, axis=-1, keepdims=True)
    xc = r - mu
    var = jnp.mean(xc * xc, axis=-1, keepdims=True)
    rstd = lax.rsqrt(var + LN_EPS)
    return xc * rstd, rstd


def _ln_bwd(dy, xhat, rstd, g):
    dxh = dy * g
    m1 = jnp.mean(dxh, axis=-1, keepdims=True)
    m2 = jnp.mean(dxh * xhat, axis=-1, keepdims=True)
    return rstd * (dxh - m1 - xhat * m2)


def _colsum(x):
    return jnp.sum(x, axis=0, keepdims=True)


class _Hosted:
    def __init__(self, inputs, out_shapes, sem_shapes, start, finish, middle=None):
        self.inputs, self.out_shapes, self.sem_shapes = list(inputs), list(out_shapes), list(sem_shapes)
        self.start, self.finish, self.middle = start, finish, middle


def _call(body, *, name, grid, in_specs, out_specs, out_shape, scratch_shapes, semantics, operands, hosted=None,
          aliases=None):
    aliases = aliases or {}
    if hosted is None:
        return pl.pallas_call(
            body, name=name, grid=grid, in_specs=list(in_specs), out_specs=list(out_specs), out_shape=list(out_shape),
            scratch_shapes=list(scratch_shapes), input_output_aliases=aliases,
            compiler_params=_params(semantics))(*operands)
    n_in, n_out, n_scr = len(in_specs), len(out_specs), len(scratch_shapes)
    h_in, h_out = len(hosted.inputs), len(hosted.out_shapes)

    def full_body(*refs):
        ins, refs = refs[:n_in], refs[n_in:]
        h_ins, refs = refs[:h_in], refs[h_in:]
        outs, refs = refs[:n_out], refs[n_out:]
        h_outs, refs = refs[:h_out], refs[h_out:]
        scr, sems = refs[:n_scr], refs[n_scr:]
        first = functools.reduce(jnp.logical_and, [pl.program_id(d) == 0 for d in range(len(grid))])
        last = functools.reduce(jnp.logical_and, [pl.program_id(d) == grid[d] - 1 for d in range(len(grid))])

        @pl.when(first)
        def _():
            hosted.start(h_ins, h_outs, sems)

        body(*ins, *outs, *scr)

        if hosted.middle is not None:
            step, total = 0, 1
            for d in range(len(grid)):
                step, total = step * grid[d] + pl.program_id(d), total * grid[d]

            @pl.when(step == (5 * total) // 6)
            def _():
                hosted.middle(h_ins, h_outs, sems)

        @pl.when(last)
        def _():
            hosted.finish(h_ins, h_outs, sems)

    return pl.pallas_call(
        full_body, name=name, grid=grid, in_specs=list(in_specs) + [ANY] * h_in,
        out_specs=list(out_specs) + [ANY] * h_out, out_shape=list(out_shape) + hosted.out_shapes,
        scratch_shapes=list(scratch_shapes) + hosted.sem_shapes, input_output_aliases=aliases,
        compiler_params=pltpu.CompilerParams(dimension_semantics=("arbitrary",) * len(grid),
                                             vmem_limit_bytes=VMEM_LIMIT, has_side_effects=True),
    )(*operands, *hosted.inputs)


def _mm(name, a, b, out_shape, out_dtype, grid, a_spec, b_spec, o_spec, dims, acc_shape, hosted=None):
    nk = grid[2]
    if nk == 1:
        def body(a_ref, b_ref, o_ref):
            o_ref[...] = _dot(a_ref[...], b_ref[...], dims).astype(o_ref.dtype)
        scratch = []
    else:
        def body(a_ref, b_ref, o_ref, acc_ref):
            k = pl.program_id(2)

            @pl.when(k == 0)
            def _():
                acc_ref[...] = jnp.zeros_like(acc_ref)

            acc_ref[...] += _dot(a_ref[...], b_ref[...], dims)

            @pl.when(k == nk - 1)
            def _():
                o_ref[...] = acc_ref[...].astype(o_ref.dtype)
        scratch = [pltpu.VMEM(acc_shape, F32)]

    outs = _call(body, name=name, grid=grid, in_specs=[a_spec, b_spec], out_specs=[o_spec],
                 out_shape=[jax.ShapeDtypeStruct(out_shape, out_dtype)], scratch_shapes=scratch,
                 semantics=("parallel", "parallel", "arbitrary"), operands=(a, b), hosted=hosted)
    return outs[0] if hosted is None else (outs[0], list(outs[1:]))


def _mm_fused(name, grid, operands, in_specs, out_shape, out_specs, dims, acc_shape, epilogue, lhs=None, scratch=(),
              hosted=None):
    nk = grid[2]
    n_in, n_out = len(in_specs), len(out_specs)

    def body(*refs):
        ins, outs, scr = refs[:n_in], refs[n_in:n_in + n_out], refs[n_in + n_out:]
        acc_ref, k = scr[0], pl.program_id(2)
        a = ins[0][...] if lhs is None else lhs(ins, outs)
        part = _dot(a, ins[1][...], dims)
        if nk == 1:
            acc_ref[...] = part
            epilogue(acc_ref, ins, outs, scr[1:])
            return

        @pl.when(k == 0)
        def _():
            acc_ref[...] = jnp.zeros_like(acc_ref)

        acc_ref[...] += part

        @pl.when(k == nk - 1)
        def _():
            epilogue(acc_ref, ins, outs, scr[1:])

    return _call(body, name=name, grid=grid, in_specs=in_specs, out_specs=out_specs, out_shape=out_shape,
                 scratch_shapes=[pltpu.VMEM(acc_shape, F32)] + list(scratch), semantics=("arbitrary",) * 3,
                 operands=operands, hosted=hosted)


def _row_blocks(rows, block=256):
    block = block if rows % block == 0 else rows
    return [slice(r, r + block) for r in range(0, rows, block)]


def _pick(t, pref):
    return pref if t % pref == 0 else t


def _glu(p):
    return p[:, :CONV_DIM] * _sigmoid(p[:, CONV_DIM:])


def _by_phase(taps):
    phases = {}
    for off, payload in taps:
        phases.setdefault(off % SUBLANES, []).append((off - off % SUBLANES, payload))
    return sorted(phases.items())


def _tap_sum(src_ref, base, taps, rows, lanes):
    acc = None
    for phase, items in _by_phase(taps):
        n = rows if phase == 0 else rows + SUBLANES
        part = None
        for off, (w_ref, k) in items:
            term = w_ref[k:k + 1, lanes] * src_ref[base + off:base + off + n, lanes]
            part = term if part is None else part + term
        if phase:
            part = part[phase:phase + rows, :]
        acc = part if acc is None else acc + part
    return acc


def _tap_products(x, src_ref, base, taps, lanes):
    rows, cols = x.shape
    pad = jnp.zeros((SUBLANES, cols), x.dtype)
    padded = jnp.concatenate([pad, x, pad], axis=0)
    out = []
    for phase, items in _by_phase(taps):
        n = rows if phase == 0 else rows + SUBLANES
        shifted = x if phase == 0 else padded[SUBLANES - phase:SUBLANES - phase + n, :]
        for off, key in items:
            out.append((key, _colsum(shifted * src_ref[base + off:base + off + n, lanes])))
    return out


def _lane_blocks(cols, block=256):
    return [slice(c, min(c + block, cols)) for c in range(0, cols, block)]


def _conv_fwd(proj, w_dw, b_dw, g, b):
    t = proj.shape[0]
    tm = _pick(t, 1024)
    nh = tm // CONV_HALO

    def body(p_ref, ph_ref, w_ref, bd_ref, g_ref, b_ref, act_ref, pre_ref, xs_ref):
        i = pl.program_id(0)
        halo = _glu(ph_ref[...])
        xs_ref[0:CONV_HALO, :] = jnp.where(i == 0, 0.0, halo)
        xs_ref[CONV_HALO:CONV_HALO + tm, :] = _glu(p_ref[...])
        taps = [(CONV_HALO - (CONV_K - 1) + k, (w_ref, k)) for k in range(CONV_K)]
        for r in range(tm // ROW_BLOCK):
            rows = slice(r * ROW_BLOCK, (r + 1) * ROW_BLOCK)
            for lanes in _lane_blocks(CONV_DIM):
                pre_ref[rows, lanes] = bd_ref[:, lanes] + _tap_sum(xs_ref, r * ROW_BLOCK, taps, ROW_BLOCK, lanes)
            acc = pre_ref[rows, :]
            xhat, _ = _ln(acc)
            yln = xhat * g_ref[...] + b_ref[...]
            act_ref[rows, :] = (yln * _sigmoid(yln)).astype(BF16)

    full = lambda s: pl.BlockSpec(s, lambda i: (0, 0))
    return pl.pallas_call(
        body, name="conv_fwd", grid=(t // tm,),
        in_specs=[pl.BlockSpec((tm, 2 * CONV_DIM), lambda i: (i, 0)),
                  pl.BlockSpec((CONV_HALO, 2 * CONV_DIM), lambda i: (jnp.maximum(i * nh - 1, 0), 0)),
                  full((CONV_K, CONV_DIM)), full((1, CONV_DIM)), full((1, CONV_DIM)), full((1, CONV_DIM))],
        out_specs=[pl.BlockSpec((tm, CONV_DIM), lambda i: (i, 0)), pl.BlockSpec((tm, CONV_DIM), lambda i: (i, 0))],
        out_shape=[jax.ShapeDtypeStruct((t, CONV_DIM), BF16), jax.ShapeDtypeStruct((t, CONV_DIM), F32)],
        scratch_shapes=[pltpu.VMEM((CONV_HALO + tm, CONV_DIM), F32)],
        compiler_params=_params(("arbitrary",)),
    )(proj, proj, w_dw, b_dw, g, b)


def _d_c_norm_bwd(d_y, w_conv_out, pre, g, b):
    t = pre.shape[0]
    tm = _pick(t, 512)
    d = D_MODEL

    def epilogue(d_c, ins, outs, scr):
        pre_ref, g_ref, b_ref = ins[2:]
        dpre_ref, sums_ref = outs
        i = pl.program_id(0)

        @pl.when(i == 0)
        def _():
            sums_ref[...] = jnp.zeros_like(sums_ref)

        for rows in _row_blocks(tm):
            xhat, rstd = _ln(pre_ref[rows, :])
            yln = xhat * g_ref[...] + b_ref[...]
            sg = _sigmoid(yln)
            dyln = d_c[rows, :] * (sg * (1.0 + yln * (1.0 - sg)))
            dpre = _ln_bwd(dyln, xhat, rstd, g_ref[...])
            dpre_ref[rows, :] = dpre
            sums_ref[0:1, :] += _colsum(dyln * xhat)
            sums_ref[1:2, :] += _colsum(dyln)
            sums_ref[2:3, :] += _colsum(dpre)

    full = lambda s: pl.BlockSpec(s, lambda i, j, k: (0, 0))
    tile = pl.BlockSpec((tm, CONV_DIM), lambda i, j, k: (i, 0))
    return _mm_fused(
        "d_c_norm_bwd", (t // tm, 1, 1), (d_y, w_conv_out, pre, g, b),
        [pl.BlockSpec((None, tm, d), lambda i, j, k: (0, i, 0)), full((CONV_DIM, d)), tile,
         full((1, CONV_DIM)), full((1, CONV_DIM))],
        [jax.ShapeDtypeStruct((t, CONV_DIM), F32), jax.ShapeDtypeStruct((8, CONV_DIM), F32)],
        [tile, full((8, CONV_DIM))], NT, (tm, CONV_DIM), epilogue)


def _conv_bwd_dw(d_pre, proj, w_dw, d_proj, hosted=None):
    t = d_pre.shape[0]
    tm = _pick(t, 1024)
    nt = t // tm
    nh = tm // CONV_HALO
    last_h = t // CONV_HALO - 1

    def body(dp_ref, dph_ref, p_ref, ph_ref, w_ref, _, dproj_ref, dw_ref, xs_ref, ds_ref):
        i = pl.program_id(0)

        @pl.when(i == 0)
        def _():
            dw_ref[...] = jnp.zeros_like(dw_ref)

        halo = _glu(ph_ref[...])
        xs_ref[0:CONV_HALO, :] = jnp.where(i == 0, 0.0, halo)
        xs_ref[CONV_HALO:CONV_HALO + tm, :] = _glu(p_ref[...])
        ds_ref[0:tm, :] = dp_ref[...]
        ds_ref[tm:tm + CONV_HALO, :] = jnp.where(i == nt - 1, 0.0, dph_ref[...])
        back_taps = [(CONV_K - 1 - k, (w_ref, k)) for k in range(CONV_K)]
        grad_taps = [(CONV_HALO - (CONV_K - 1) + k, k) for k in range(CONV_K)]
        for r in range(tm // ROW_BLOCK):
            base = r * ROW_BLOCK
            rows = slice(base, base + ROW_BLOCK)
            for lanes in _lane_blocks(CONV_DIM):
                gate_lanes = slice(CONV_DIM + lanes.start, CONV_DIM + lanes.stop)
                acc = _tap_sum(ds_ref, base, back_taps, ROW_BLOCK, lanes)
                for k, total in _tap_products(ds_ref[rows, lanes], xs_ref, base, grad_taps, lanes):
                    dw_ref[k:k + 1, lanes] += total
                cval = p_ref[rows, lanes]
                sg = _sigmoid(p_ref[rows, gate_lanes])
                dproj_ref[rows, lanes] = (acc * sg).astype(BF16)
                dproj_ref[rows, gate_lanes] = (acc * cval * sg * (1.0 - sg)).astype(BF16)

    full = lambda s: pl.BlockSpec(s, lambda i: (0, 0))
    return _call(
        body, name="conv_bwd_dw", grid=(nt,),
        in_specs=[pl.BlockSpec((tm, CONV_DIM), lambda i: (i, 0)),
                  pl.BlockSpec((CONV_HALO, CONV_DIM), lambda i: (jnp.minimum((i + 1) * nh, last_h), 0)),
                  pl.BlockSpec((tm, 2 * CONV_DIM), lambda i: (i, 0)),
                  pl.BlockSpec((CONV_HALO, 2 * CONV_DIM), lambda i: (jnp.maximum(i * nh - 1, 0), 0)),
                  full((CONV_K, CONV_DIM)), ANY],
        out_specs=[pl.BlockSpec((tm, 2 * CONV_DIM), lambda i: (i, DP_CONV_BLOCK)), full((CONV_HALO, CONV_DIM))],
        out_shape=[jax.ShapeDtypeStruct(d_proj.shape, BF16), jax.ShapeDtypeStruct((CONV_HALO, CONV_DIM), F32)],
        scratch_shapes=[pltpu.VMEM((CONV_HALO + tm, CONV_DIM), F32), pltpu.VMEM((tm + CONV_HALO, CONV_DIM), F32)],
        semantics=("arbitrary",), operands=(d_pre, d_pre, proj, proj, w_dw, d_proj), hosted=hosted, aliases={5: 0})


def _lower_bound(logit_ref):
    l0 = logit_ref[0:1, :]
    l1 = logit_ref[1:2, :]
    m = jnp.maximum(l0, l1)
    e0 = jnp.exp(l0 - m)
    e1 = jnp.exp(l1 - m)
    return e0 / (e0 + e1)


def _tri(lower):
    r = lax.broadcasted_iota(jnp.int32, (CHUNK, CHUNK), 0)
    c = lax.broadcasted_iota(jnp.int32, (CHUNK, CHUNK), 1)
    return (c <= r) if lower else (c >= r)


def _hgrn_gates(fz, lb):
    s = _sigmoid(fz)
    sn = _sigmoid(-fz)
    f = lb + (1.0 - lb) * s
    return s, sn, f


def _block_tri(rows, lower=True):
    r = lax.broadcasted_iota(jnp.int32, (rows, rows), 0)
    c = lax.broadcasted_iota(jnp.int32, (rows, rows), 1)
    tri = (c <= r) if lower else (c >= r)
    return (tri & (r // CHUNK == c // CHUNK)).astype(BF16)


def _tri_rows(tm):
    return min(tm, MXU_DEPTH)


def _tri_matmul(tri_ref, x):
    hi = x.astype(BF16)
    lo = (x - hi.astype(F32)).astype(BF16)
    tri = tri_ref[...]
    return (lax.dot_general(tri, hi, NN, preferred_element_type=F32)
            + lax.dot_general(tri, lo, NN, preferred_element_type=F32))


def _groups(tm):
    g = _tri_rows(tm)
    return [slice(i * g, (i + 1) * g) for i in range(tm // g)]


def _hgrn_fwd(proj, logits, norm_g, hosted=None):
    t = proj.shape[0]
    tm = CHUNK * CHUNKS_PER_BLOCK if t % (CHUNK * CHUNKS_PER_BLOCK) == 0 else CHUNK
    cpb = tm // CHUNK
    nt = t // tm
    half = CHUNK // 2

    def body(qz_ref, fz_ref, iv_ref, gz_ref, lg_ref, ng_ref, tri_ref, o_ref, og_ref, st_ref,
             state_ref, qe_ref, ke_ref, qb_ref, kl_ref, v_ref, upd_ref, decay_ref, a_ref, q_ref, kk_ref, b_ref):
        j = pl.program_id(1)

        @pl.when(j == 0)
        def _():
            state_ref[...] = jnp.zeros_like(state_ref)

        lb = _lower_bound(lg_ref)
        chunks = [slice(c * CHUNK, (c + 1) * CHUNK) for c in range(cpb)]
        for rows in chunks:
            qz = qz_ref[rows, :]
            q_ref[rows, :] = qz * _sigmoid(qz)
            _, sn, f = _hgrn_gates(fz_ref[rows, :], lb)
            kk_ref[rows, :] = (1.0 - lb) * sn
            b_ref[rows, :] = jnp.log(f)
            v_ref[rows, :] = iv_ref[rows, :].astype(BF16)
        for rows in _groups(tm):
            b_ref[rows, :] = _tri_matmul(tri_ref, b_ref[rows, :])
        for c, rows in enumerate(chunks):
            b = b_ref[rows, :]
            bref = b[half - 1:half, :]
            blast = b[CHUNK - 1:CHUNK, :]
            q = q_ref[rows, :]
            kk = kk_ref[rows, :]
            qb_ref[rows, :] = (q * jnp.exp(b)).astype(BF16)
            qe_ref[rows, :] = (q * jnp.exp(b - bref)).astype(BF16)
            ke_ref[rows, :] = (kk * jnp.exp(bref - b)).astype(BF16)
            kl_ref[rows, :] = (kk * jnp.exp(blast - b)).astype(BF16)
            decay_ref[c:c + 1, :] = jnp.exp(blast)
        causal = _tri(True)
        for c, rows in enumerate(chunks):
            upd_ref[c] = _dot(v_ref[rows, :], kl_ref[rows, :], TN)
            a_ref[c] = jnp.where(causal, _dot(qe_ref[rows, :], ke_ref[rows, :], NT), 0.0).astype(BF16)
        state = state_ref[...]
        for c in range(cpb):
            st_ref[c] = state.astype(BF16)
            state = state * decay_ref[c:c + 1, :] + upd_ref[c]
        state_ref[...] = state
        for c, rows in enumerate(chunks):
            o_ref[rows, :] = _dot(a_ref[c], v_ref[rows, :], NN) + _dot(qb_ref[rows, :], st_ref[c], NT)
        for rows in chunks:
            o = o_ref[rows, :]
            r = lax.rsqrt(jnp.mean(o * o, axis=-1, keepdims=True) + RMS_EPS)
            gz = gz_ref[rows, :]
            og_ref[rows, :] = ((o * r * ng_ref[...]) * (gz * _sigmoid(gz))).astype(BF16)

    col = lambda base: pl.BlockSpec((tm, HEAD_DIM), lambda h, j: (j, base + h))
    tile_bf = pltpu.VMEM((tm, HEAD_DIM), BF16)
    tile_f32 = pltpu.VMEM((tm, HEAD_DIM), F32)
    return _call(
        body, name="hgrn_fwd", grid=(HEADS, nt),
        in_specs=[col(8), col(16), col(24), col(32),
                  pl.BlockSpec((2, HEAD_DIM), lambda h, j: (0, h)), pl.BlockSpec((1, HEAD_DIM), lambda h, j: (0, h)),
                  pl.BlockSpec((_tri_rows(tm), _tri_rows(tm)), lambda h, j: (0, 0))],
        out_specs=[col(0), col(0), pl.BlockSpec((None, cpb, HEAD_DIM, HEAD_DIM), lambda h, j: (h, j, 0, 0))],
        out_shape=[jax.ShapeDtypeStruct((t, HGRN_DIM), F32), jax.ShapeDtypeStruct((t, HGRN_DIM), BF16),
                   jax.ShapeDtypeStruct((HEADS, t // CHUNK, HEAD_DIM, HEAD_DIM), BF16)],
        scratch_shapes=[pltpu.VMEM((HEAD_DIM, HEAD_DIM), F32), tile_bf, tile_bf, tile_bf, tile_bf, tile_bf,
                        pltpu.VMEM((cpb, HEAD_DIM, HEAD_DIM), F32), pltpu.VMEM((max(cpb, 8), HEAD_DIM), F32),
                        pltpu.VMEM((cpb, CHUNK, CHUNK), BF16), tile_f32, tile_f32, tile_f32],
        semantics=("parallel", "arbitrary"),
        operands=(proj, proj, proj, proj, logits, norm_g, _block_tri(_tri_rows(tm))), hosted=hosted)


def _hgrn_bwd(d_og, o, proj, states, logits, norm_g, d_proj, hosted=None):
    t = proj.shape[0]
    tm = CHUNK * CHUNKS_PER_BLOCK if t % (CHUNK * CHUNKS_PER_BLOCK) == 0 else CHUNK
    cpb = tm // CHUNK
    nt = t // tm
    half = CHUNK // 2

    def body(dog_ref, o_ref, qz_ref, fz_ref, iv_ref, gz_ref, st_ref, lg_ref, ng_ref, tril_ref, triu_ref,
             _, dp_ref, sums_ref,
             dstate_ref, qe_ref, ke_ref, qb_ref, kl_ref, v_ref, do_ref, upd_ref, dst_ref, a_ref, da_ref,
             decay_ref, through_ref, q_ref, kk_ref, b_ref, dsilu_ref, gs_ref, gf_ref, sn_ref,
             eb_ref, ebr_ref, ekr_ref, ebl_ref, rev_ref, pre_ref, dk_ref):
        j = pl.program_id(1)

        @pl.when(j == 0)
        def _():
            dstate_ref[...] = jnp.zeros_like(dstate_ref)
            sums_ref[...] = jnp.zeros_like(sums_ref)

        lb = _lower_bound(lg_ref)
        ng = ng_ref[...]
        chunks = [slice(c * CHUNK, (c + 1) * CHUNK) for c in range(cpb)]
        for rows in chunks:
            qz = qz_ref[rows, :]
            sq = _sigmoid(qz)
            q_ref[rows, :] = qz * sq
            dsilu_ref[rows, :] = sq * (1.0 + qz * (1.0 - sq))
            s, sn, f = _hgrn_gates(fz_ref[rows, :], lb)
            kk_ref[rows, :] = (1.0 - lb) * sn
            b_ref[rows, :] = jnp.log(f)
            sn_ref[rows, :] = sn
            gf_ref[rows, :] = sn / f
            gs_ref[rows, :] = (1.0 - lb) * s
            v_ref[rows, :] = iv_ref[rows, :].astype(BF16)
            ov = o_ref[rows, :]
            r = lax.rsqrt(jnp.mean(ov * ov, axis=-1, keepdims=True) + RMS_EPS)
            on = ov * r
            gz = gz_ref[rows, :]
            sg = _sigmoid(gz)
            dog = dog_ref[rows, :]
            dp_ref[rows, 3 * HEAD_DIM:4 * HEAD_DIM] =(dog * (on * ng) * (sg * (1.0 + gz * (1.0 - sg)))).astype(BF16)
            d_ong = dog * (gz * sg)
            sums_ref[0:1, :] += _colsum(d_ong * on)
            d_on = d_ong * ng
            do_ref[rows, :] = (r * (d_on - on * jnp.mean(d_on * on, axis=-1, keepdims=True))).astype(BF16)
        for rows in _groups(tm):
            b_ref[rows, :] = _tri_matmul(tril_ref, b_ref[rows, :])
        for c, rows in enumerate(chunks):
            b = b_ref[rows, :]
            bref = b[half - 1:half, :]
            blast = b[CHUNK - 1:CHUNK, :]
            q = q_ref[rows, :]
            kk = kk_ref[rows, :]
            eb = jnp.exp(b)
            ebr = jnp.exp(b - bref)
            ekr = jnp.exp(bref - b)
            ebl = jnp.exp(blast - b)
            eb_ref[rows, :] = eb
            ebr_ref[rows, :] = ebr
            ekr_ref[rows, :] = ekr
            ebl_ref[rows, :] = ebl
            qb_ref[rows, :] = (q * eb).astype(BF16)
            qe_ref[rows, :] = (q * ebr).astype(BF16)
            ke_ref[rows, :] = (kk * ekr).astype(BF16)
            kl_ref[rows, :] = (kk * ebl).astype(BF16)
            decay_ref[c:c + 1, :] = jnp.exp(blast)
        causal = _tri(True)
        for c, rows in enumerate(chunks):
            upd_ref[c] = _dot(do_ref[rows, :], qb_ref[rows, :], TN)
            a_ref[c] = jnp.where(causal, _dot(qe_ref[rows, :], ke_ref[rows, :], NT), 0.0).astype(BF16)
            da_ref[c] = jnp.where(causal, _dot(do_ref[rows, :], v_ref[rows, :], NT), 0.0).astype(BF16)
        dstate = dstate_ref[...]
        for c in reversed(range(cpb)):
            dst_ref[c] = dstate.astype(BF16)
            decay = decay_ref[c:c + 1, :]
            through_ref[c:c + 1, :] = decay * _colsum(dstate * st_ref[c].astype(F32))
            dstate = dstate * decay + upd_ref[c]
        dstate_ref[...] = dstate
        for c, rows in enumerate(chunks):
            dp_ref[rows, 2 * HEAD_DIM:3 * HEAD_DIM] =(_dot(a_ref[c], do_ref[rows, :], TN)
                                + _dot(kl_ref[rows, :], dst_ref[c], NT)).astype(BF16)
        for c, rows in enumerate(chunks):
            dqe = _dot(da_ref[c], ke_ref[rows, :], NN)
            dq_inter = _dot(do_ref[rows, :], st_ref[c], NN) * eb_ref[rows, :]
            dp_ref[rows, 0:HEAD_DIM] =((dqe * ebr_ref[rows, :] + dq_inter) * dsilu_ref[rows, :]).astype(BF16)
            rev_ref[rows, :] = qe_ref[rows, :].astype(F32) * dqe + q_ref[rows, :] * dq_inter
        for c, rows in enumerate(chunks):
            dke = _dot(da_ref[c], qe_ref[rows, :], TN)
            dk_inter = _dot(v_ref[rows, :], dst_ref[c], NN) * ebl_ref[rows, :]
            dk_ref[rows, :] = dke * ekr_ref[rows, :] + dk_inter
            rev_ref[rows, :] -= ke_ref[rows, :].astype(F32) * dke
            pre_ref[rows, :] = kk_ref[rows, :] * dk_inter
        for rows in _groups(tm):
            pre = pre_ref[rows, :]
            rev_ref[rows, :] = _tri_matmul(triu_ref, rev_ref[rows, :]) + (_tri_matmul(tril_ref, pre) - pre)
        for c, rows in enumerate(chunks):
            dlf = rev_ref[rows, :] + through_ref[c:c + 1, :]
            common = gf_ref[rows, :] * dlf - sn_ref[rows, :] * dk_ref[rows, :]
            dp_ref[rows, HEAD_DIM:2 * HEAD_DIM] =(gs_ref[rows, :] * common).astype(BF16)
            sums_ref[1:2, :] += _colsum(common)

        @pl.when(j == nt - 1)
        def _():
            sums_ref[1:2, :] = sums_ref[1:2, :] * lb * (1.0 - lb)

    rev = lambda base: pl.BlockSpec((tm, HEAD_DIM), lambda h, j: (nt - 1 - j, base + h))
    vec = lambda n: pl.BlockSpec((n, HEAD_DIM), lambda h, j: (0, h))
    const = pl.BlockSpec((_tri_rows(tm), _tri_rows(tm)), lambda h, j: (0, 0))
    tile_bf = pltpu.VMEM((tm, HEAD_DIM), BF16)
    tile_f32 = pltpu.VMEM((tm, HEAD_DIM), F32)
    square = lambda dtype: pltpu.VMEM((cpb, HEAD_DIM, HEAD_DIM), dtype)
    rows8 = pltpu.VMEM((max(cpb, 8), HEAD_DIM), F32)
    operands = (d_og, o, proj, proj, proj, proj, states, logits, norm_g, _block_tri(_tri_rows(tm)),
                _block_tri(_tri_rows(tm), lower=False), d_proj)
    return _call(
        body, name="hgrn_bwd", grid=(HEADS, nt),
        in_specs=[rev(0), rev(0), rev(8), rev(16), rev(24), rev(32),
                  pl.BlockSpec((None, cpb, HEAD_DIM, HEAD_DIM), lambda h, j: (h, nt - 1 - j, 0, 0)),
                  vec(2), vec(1), const, const, ANY],
        out_specs=[pl.BlockSpec((tm, 4 * HEAD_DIM), lambda h, j: (nt - 1 - j, DP_HEAD_BLOCK + h)), vec(8)],
        out_shape=[jax.ShapeDtypeStruct(d_proj.shape, BF16), jax.ShapeDtypeStruct((8, HGRN_DIM), F32)],
        scratch_shapes=[pltpu.VMEM((HEAD_DIM, HEAD_DIM), F32)] + [tile_bf] * 6 + [square(F32), square(BF16)]
        + [pltpu.VMEM((cpb, CHUNK, CHUNK), BF16)] * 2 + [rows8, rows8] + [tile_f32] * 14,
        semantics=("parallel", "arbitrary"), operands=operands, hosted=hosted, aliases={len(operands) - 1: 0})


def _mix_ln1(proj, y_conv, y_hgrn, w_out, x, g, b, hosted=None):
    t = x.shape[0]
    tm = _pick(t, 512)
    d = D_MODEL

    def lhs(ins, outs):
        for rows in _row_blocks(tm):
            outs[0][rows, :] = (_sigmoid(ins[0][rows, :]) * ins[3][rows, :]
                                + _sigmoid(ins[2][rows, :]) * ins[4][rows, :]).astype(BF16)
        return outs[0][...]

    def epilogue(acc, ins, outs, scr):
        for rows in _row_blocks(tm):
            r = ALPHA * ins[5][rows, :] + acc[rows, :]
            outs[1][rows, :] = r
            xhat, _ = _ln(r)
            outs[2][rows, :] = (xhat * ins[6][...] + ins[7][...]).astype(BF16)

    tile = pl.BlockSpec((tm, d), lambda i, j, k: (i, 0))
    vec = pl.BlockSpec((1, d), lambda i, j, k: (0, 0))
    return _mm_fused(
        "mix_ln1", (t // tm, 1, 1), (proj, w_out, proj, y_conv, y_hgrn, x, g, b),
        [pl.BlockSpec((tm, d), lambda i, j, k: (i, 5)), pl.BlockSpec((d, d), lambda i, j, k: (0, 0)),
         pl.BlockSpec((tm, d), lambda i, j, k: (i, 6)), tile, tile, tile, vec, vec],
        [jax.ShapeDtypeStruct((t, d), BF16), jax.ShapeDtypeStruct((t, d), F32), jax.ShapeDtypeStruct((t, d), BF16)],
        [tile, tile, tile], NN, (tm, d), epilogue, lhs=lhs, hosted=hosted)


def _d_mixed_merge_bwd(d_r1b, w_out, proj, y_conv, y_hgrn, hosted=None):
    t = proj.shape[0]
    tm = _pick(t, 512)
    d = D_MODEL

    def epilogue(d_mixed, ins, outs, scr):
        dy_ref, dmz_ref = outs
        for rows in _row_blocks(tm):
            dm = d_mixed[rows, :]
            for br in range(2):
                sg = _sigmoid(ins[2 + br][rows, :])
                dy_ref[br, rows, :] = (sg * dm).astype(BF16)
                dmz_ref[rows, br * d:(br + 1) * d] = (dm * ins[4 + br][rows, :] * sg * (1.0 - sg)).astype(BF16)

    tile = pl.BlockSpec((tm, d), lambda i, j, k: (i, 0))
    return _mm_fused(
        "d_mixed_merge_bwd", (t // tm, 1, 1), (d_r1b, w_out, proj, proj, y_conv, y_hgrn),
        [tile, pl.BlockSpec((d, d), lambda i, j, k: (0, 0)), pl.BlockSpec((tm, d), lambda i, j, k: (i, 5)),
         pl.BlockSpec((tm, d), lambda i, j, k: (i, 6)), tile, tile],
        [jax.ShapeDtypeStruct((2, t, d), BF16), jax.ShapeDtypeStruct((t, IN_COLS), BF16)],
        [pl.BlockSpec((2, tm, d), lambda i, j, k: (0, i, 0)),
         pl.BlockSpec((tm, 2 * d), lambda i, j, k: (i, DP_MERGE_BLOCK))],
        NT, (tm, d), epilogue, hosted=hosted)


def _ffn_out_ln2(act, w_ffn_out, r1, target, g1, b1, g2, b2):
    t = r1.shape[0]
    tm = _pick(t, 1024)
    nt = t // tm
    d = D_MODEL

    def epilogue(y_ffn, ins, outs, scr):
        r1_ref, tg_ref, g1_ref, b1_ref, g2_ref, b2_ref = ins[2:]
        dr_ref, drb_ref, sums_ref = outs
        (sq_ref,) = scr
        i = pl.program_id(0)

        @pl.when(i == 0)
        def _():
            sums_ref[...] = jnp.zeros_like(sums_ref)
            sq_ref[...] = jnp.zeros_like(sq_ref)

        for rows in _row_blocks(tm):
            xh1, _ = _ln(r1_ref[rows, :])
            x1 = xh1 * g1_ref[...] + b1_ref[...]
            xh2, rstd2 = _ln(ALPHA * x1 + y_ffn[rows, :])
            diff = xh2 * g2_ref[...] + b2_ref[...] - tg_ref[rows, :]
            dy = diff * (1.0 / D_MODEL)
            dr = _ln_bwd(dy, xh2, rstd2, g2_ref[...])
            dr_ref[rows, :] = dr
            drb_ref[rows, :] = dr.astype(BF16)
            sums_ref[0:1, :] += _colsum(dy * xh2)
            sums_ref[1:2, :] += _colsum(dy)
            sq_ref[...] += _colsum(diff * diff)

        @pl.when(i == nt - 1)
        def _():
            total = jnp.sum(sq_ref[...], axis=-1, keepdims=True) * (0.5 / D_MODEL)
            sums_ref[2:3, :] = jnp.broadcast_to(total, (1, D_MODEL))

    tile = pl.BlockSpec((tm, d), lambda i, j, k: (i, 0))
    vec = pl.BlockSpec((1, d), lambda i, j, k: (0, 0))
    return _mm_fused(
        "ffn_out_ln2", (nt, 1, 4), (act, w_ffn_out, r1, target, g1, b1, g2, b2),
        [pl.BlockSpec((None, tm, FF_SHARD), lambda i, j, k: (k, i, 0)),
         pl.BlockSpec((None, FF_SHARD, d), lambda i, j, k: (k, 0, 0)), tile, tile, vec, vec, vec, vec],
        [jax.ShapeDtypeStruct((t, d), F32), jax.ShapeDtypeStruct((t, d), BF16), jax.ShapeDtypeStruct((8, d), F32)],
        [tile, tile, pl.BlockSpec((8, d), lambda i, j, k: (0, 0))], NN, (tm, d), epilogue,
        scratch=[pltpu.VMEM((1, d), F32)])


def _d_x1_ln1_bwd(d_z, w_ffn_in, d_r2, r1, g1):
    t = r1.shape[0]
    tm = _pick(t, 1024)
    d = D_MODEL

    def epilogue(dx_ffn, ins, outs, scr):
        dr2_ref, r1_ref, g_ref = ins[2:]
        dr1_ref, dr1b_ref, sums_ref = outs
        i = pl.program_id(0)

        @pl.when(i == 0)
        def _():
            sums_ref[...] = jnp.zeros_like(sums_ref)

        for rows in _row_blocks(tm):
            xhat, rstd = _ln(r1_ref[rows, :])
            dx1 = ALPHA * dr2_ref[rows, :] + dx_ffn[rows, :]
            dr1 = _ln_bwd(dx1, xhat, rstd, g_ref[...])
            dr1_ref[rows, :] = dr1
            dr1b_ref[rows, :] = dr1.astype(BF16)
            sums_ref[0:1, :] += _colsum(dx1 * xhat)
            sums_ref[1:2, :] += _colsum(dx1)

    tile = pl.BlockSpec((tm, d), lambda i, j, k: (i, 0))
    return _mm_fused(
        "d_x1_ln1_bwd", (t // tm, 1, N_DEV), (d_z, w_ffn_in, d_r2, r1, g1),
        [pl.BlockSpec((None, tm, FF_SHARD), lambda i, j, k: (k, i, 0)),
         pl.BlockSpec((None, d, FF_SHARD), lambda i, j, k: (k, 0, 0)), tile, tile,
         pl.BlockSpec((1, d), lambda i, j, k: (0, 0))],
        [jax.ShapeDtypeStruct((t, d), F32), jax.ShapeDtypeStruct((t, d), BF16), jax.ShapeDtypeStruct((8, d), F32)],
        [tile, tile, pl.BlockSpec((8, d), lambda i, j, k: (0, 0))], NT, (tm, d), epilogue)


def _cast_bf16(x):
    t = x.shape[0]
    tm = _pick(t, 512)

    def body(x_ref, o_ref):
        o_ref[...] = x_ref[...].astype(BF16)

    tile = pl.BlockSpec((tm, D_MODEL), lambda i: (i, 0))
    return pl.pallas_call(
        body, name="cast_x", grid=(t // tm,), in_specs=[tile], out_specs=tile,
        out_shape=jax.ShapeDtypeStruct((t, D_MODEL), BF16), compiler_params=_params(("parallel",)),
    )(x)


def _relayout(name, a, in_block, in_map, out_block, out_map, out_shape):
    def body(a_ref, o_ref):
        o_ref[...] = a_ref[...].astype(o_ref.dtype)

    return pl.pallas_call(
        body, name=name, grid=(N_DEV,), in_specs=[pl.BlockSpec(in_block, in_map)],
        out_specs=pl.BlockSpec(out_block, out_map), out_shape=out_shape, compiler_params=_params(("parallel",)),
    )(a)


_GELU_C = math.sqrt(2.0 / math.pi)


_GELU_CUBIC = 0.044715


def _gelu_parts(u):
    u2 = u * u
    th = jnp.tanh(u * (_GELU_C + (_GELU_C * _GELU_CUBIC) * u2))
    hu = 0.5 * u
    return th, hu + hu * th, u2, hu


BF16_ROWS = 16


def _ffn_act_fwd(z, w_dw, b_dw):
    t = z.shape[2]
    tm = _pick(t, 1024)
    nh = tm // FFN_HALO

    def body(z_ref, zh_ref, w_ref, b_ref, act_ref, gd_ref, us_ref):
        i = pl.program_id(1)
        us_ref[0:FFN_HALO, :] = jnp.where(i == 0, 0.0, zh_ref[...])
        us_ref[FFN_HALO:FFN_HALO + tm, :] = z_ref[0]
        for r in range(tm // ROW_BLOCK):
            base = r * ROW_BLOCK
            rows = slice(base, base + ROW_BLOCK)
            for lanes in _lane_blocks(FF_SHARD):
                uc = b_ref[:, lanes]
                for k in range(FFN_K):
                    off = base + FFN_HALO - (FFN_K - 1) + k
                    uc = uc + w_ref[k:k + 1, lanes] * us_ref[off:off + ROW_BLOCK, lanes]
                th, gelu, u2, hu = _gelu_parts(uc)
                dgelu = (0.5 + 0.5 * th) + (hu - hu * th * th) * (_GELU_C + (3.0 * _GELU_C * _GELU_CUBIC) * u2)
                act_ref[rows, lanes] = (gelu * z_ref[1, rows, lanes]).astype(BF16)
                gd_ref[0, rows, lanes] = gelu.astype(BF16)
                gd_ref[1, rows, lanes] = dgelu.astype(BF16)

    return pl.pallas_call(
        body, name="ffn_act_fwd", grid=(4, t // tm),
        in_specs=[pl.BlockSpec((2, None, tm, FF_SHARD), lambda j, i: (0, j, i, 0)),
                  pl.BlockSpec((None, None, FFN_HALO, FF_SHARD), lambda j, i: (0, j, jnp.maximum(i * nh - 1, 0), 0)),
                  pl.BlockSpec((None, FFN_K, FF_SHARD), lambda j, i: (j, 0, 0)),
                  pl.BlockSpec((None, 1, FF_SHARD), lambda j, i: (j, 0, 0))],
        out_specs=[pl.BlockSpec((None, tm, FF_SHARD), lambda j, i: (j, i, 0)),
                   pl.BlockSpec((2, None, tm, FF_SHARD), lambda j, i: (0, j, i, 0))],
        out_shape=[jax.ShapeDtypeStruct((4, t, FF_SHARD), BF16), jax.ShapeDtypeStruct((2, 4, t, FF_SHARD), BF16)],
        scratch_shapes=[pltpu.VMEM((FFN_HALO + tm, FF_SHARD), F32)],
        compiler_params=_params(("parallel", "arbitrary")),
    )(z, z, w_dw, b_dw)


def _ffn_act_bwd(d_act, z, gd, w_dw):
    t = z.shape[2]
    tm = _pick(t, 1024)
    nt = t // tm
    nh = tm // FFN_HALO
    last_h = t // FFN_HALO - 1
    pad = FFN_HALO - (FFN_K - 1)

    def fold(x):
        return functools.reduce(jnp.add, [x[r:r + SUBLANES, :] for r in range(0, x.shape[0], SUBLANES)])

    def body(da_ref, dah_ref, z_ref, zp_ref, gn_ref, gd_ref, gdn_ref, w_ref, dz_ref, sums_ref, us_ref, ds_ref,
             part_ref):
        i = pl.program_id(1)

        @pl.when(i == 0)
        def _():
            part_ref[...] = jnp.zeros_like(part_ref)

        us_ref[0:FFN_HALO, :] = jnp.where(i == 0, 0.0, zp_ref[...])
        us_ref[FFN_HALO:FFN_HALO + tm, :] = z_ref[0]
        for r in range(tm // ROW_BLOCK):
            base = r * ROW_BLOCK
            rows = slice(base, base + ROW_BLOCK)
            for lanes in _lane_blocks(FF_SHARD):
                da = da_ref[rows, lanes]
                dz_ref[1, rows, lanes] = (da * gd_ref[0, rows, lanes].astype(F32)).astype(BF16)
                duc = da * z_ref[1, rows, lanes] * gd_ref[1, rows, lanes].astype(F32)
                ds_ref[rows, lanes] = duc
                for k in range(FFN_K):
                    part_ref[k, :, lanes] += fold(duc * us_ref[base + pad + k:base + pad + k + ROW_BLOCK, lanes])
                part_ref[FFN_K, :, lanes] += fold(duc)
        duc_next = dah_ref[...] * gn_ref[...] * gdn_ref[0:FFN_HALO, :].astype(F32)
        ds_ref[tm:tm + FFN_HALO, :] = jnp.where(i == nt - 1, 0.0, duc_next)
        for r in range(tm // ROW_BLOCK):
            base = r * ROW_BLOCK
            for lanes in _lane_blocks(FF_SHARD):
                du = None
                for k in range(FFN_K):
                    off = base + FFN_K - 1 - k
                    term = w_ref[k:k + 1, lanes] * ds_ref[off:off + ROW_BLOCK, lanes]
                    du = term if du is None else du + term
                dz_ref[0, base:base + ROW_BLOCK, lanes] = du.astype(BF16)

        @pl.when(i == nt - 1)
        def _():
            sums_ref[...] = jnp.zeros_like(sums_ref)
            for k in range(FFN_K + 1):
                sums_ref[k:k + 1, :] = _colsum(part_ref[k])

    nxt = lambda i: jnp.minimum((i + 1) * nh, last_h)
    nxt_bf = lambda i: jnp.minimum((i + 1) * (tm // BF16_ROWS), t // BF16_ROWS - 1)
    return pl.pallas_call(
        body, name="ffn_act_bwd", grid=(4, nt),
        in_specs=[pl.BlockSpec((None, tm, FF_SHARD), lambda j, i: (j, i, 0)),
                  pl.BlockSpec((None, FFN_HALO, FF_SHARD), lambda j, i: (j, nxt(i), 0)),
                  pl.BlockSpec((2, None, tm, FF_SHARD), lambda j, i: (0, j, i, 0)),
                  pl.BlockSpec((None, None, FFN_HALO, FF_SHARD), lambda j, i: (0, j, jnp.maximum(i * nh - 1, 0), 0)),
                  pl.BlockSpec((None, None, FFN_HALO, FF_SHARD), lambda j, i: (1, j, nxt(i), 0)),
                  pl.BlockSpec((2, None, tm, FF_SHARD), lambda j, i: (0, j, i, 0)),
                  pl.BlockSpec((None, None, BF16_ROWS, FF_SHARD), lambda j, i: (1, j, nxt_bf(i), 0)),
                  pl.BlockSpec((None, FFN_K, FF_SHARD), lambda j, i: (j, 0, 0))],
        out_specs=[pl.BlockSpec((2, None, tm, FF_SHARD), lambda j, i: (0, j, i, 0)),
                   pl.BlockSpec((None, 8, FF_SHARD), lambda j, i: (j, 0, 0))],
        out_shape=[jax.ShapeDtypeStruct((2, 4, t, FF_SHARD), BF16), jax.ShapeDtypeStruct((4, 8, FF_SHARD), F32)],
        scratch_shapes=[pltpu.VMEM((FFN_HALO + tm, FF_SHARD), F32), pltpu.VMEM((tm + FFN_HALO, FF_SHARD), F32),
                        pltpu.VMEM((FFN_K + 1, SUBLANES, FF_SHARD), F32)],
        compiler_params=_params(("parallel", "arbitrary")),
    )(d_act, d_act, z, z, z, gd, gd, w_dw)


_HGRN_COLS = 4 * HGRN_DIM


def _to_backward_order(w):
    heads = w[:, 2 * CONV_DIM:2 * CONV_DIM + _HGRN_COLS].reshape(-1, 4, HEADS, HEAD_DIM)
    heads = jnp.swapaxes(heads, 1, 2).reshape(-1, _HGRN_COLS)
    return jnp.concatenate([w[:, 2 * CONV_DIM + _HGRN_COLS:], w[:, :2 * CONV_DIM], heads], axis=1)


def _from_backward_order(w):
    heads = w[:, 2 * D_MODEL + 2 * CONV_DIM:].reshape(-1, HEADS, 4, HEAD_DIM)
    heads = jnp.swapaxes(heads, 1, 2).reshape(-1, _HGRN_COLS)
    return jnp.concatenate([w[:, 2 * D_MODEL:2 * D_MODEL + 2 * CONV_DIM], heads, w[:, :2 * D_MODEL]], axis=1)
def _local_step(x, target, weights, small, scatter=None, order=None):
    t = x.shape[0]
    tm = _pick(t, 2048)
    tk = _pick(t, 2048)
    nm = t // tm
    nk = t // tk
    d = D_MODEL

    xb = _cast_bf16(x)
    ffn_in_gather = ffn_out_gather = None
    if isinstance(weights, tuple) and isinstance(weights[0], _Hosted):
        first_gather, ffn_in_gather, ffn_out_gather = weights
        proj, (w_in, w_conv_out8, w_hgrn_out8, w_out8, conv_dw8, ffn_dw8) = _proj_gather(xb, first_gather, order)
    else:
        w_in, w_conv_out8, w_hgrn_out8, w_out8, w_ffn_in, w_ffn_out8, conv_dw8, ffn_dw8 = weights
        proj = _mm("proj", xb, w_in, (t, IN_COLS), F32, (nm, N_DEV, 1),
                   pl.BlockSpec((tm, d), lambda i, j, k: (i, 0)),
                   pl.BlockSpec((None, d, IN_SHARD), lambda i, j, k: (j, 0, 0)),
                   pl.BlockSpec((tm, IN_SHARD), lambda i, j, k: (i, j)), NN, (tm, IN_SHARD))
    o, og, states, *late = _hgrn_fwd(proj, small["hgrn_lb_logits"], small["hgrn_norm_g"], hosted=ffn_in_gather)
    if ffn_in_gather is not None:
        (w_ffn_in,) = late
    w_conv_out = _relayout("w_conv_out_natural", w_conv_out8, (None, CONV_DIM, 128), lambda j: (j, 0, 0),
                           (CONV_DIM, 128), lambda j: (0, j), jax.ShapeDtypeStruct((CONV_DIM, d), BF16))
    w_hgrn_out = w_hgrn_out8.reshape(d, d)
    w_out = w_out8.reshape(d, d)
    conv_dw =jnp.transpose(conv_dw8[:, :CONV_K, :CONV_DIM // N_DEV], (1, 0, 2)).reshape(CONV_K, CONV_DIM)
    ffn_dw = jnp.transpose(ffn_dw8[:, :FFN_K, :D_FF // N_DEV], (1, 0, 2)).reshape(FFN_K, 4, FF_SHARD)
    small = dict(small, w_conv_dw=conv_dw, w_ffn_dw=jnp.transpose(ffn_dw, (1, 0, 2)),
                 b_ffn_dw=small["b_ffn_dw"].reshape(4, 1, FF_SHARD))

    c_act, conv_pre = _conv_fwd(proj, small["w_conv_dw"], small["b_conv_dw"], small["conv_ln_g"], small["conv_ln_b"])
    y_conv = _mm("y_conv", c_act, w_conv_out, (t, d), F32, (nm, 1, 1),
                 pl.BlockSpec((tm, CONV_DIM), lambda i, j, k: (i, 0)),
                 pl.BlockSpec((CONV_DIM, d), lambda i, j, k: (0, 0)),
                 pl.BlockSpec((tm, d), lambda i, j, k: (i, 0)), NN, (tm, d))
    sq_w = pl.BlockSpec((d, d), lambda i, j, k: (0, 0))
    row_tile = pl.BlockSpec((tm, d), lambda i, j, k: (i, 0))
    y_hgrn = _mm("y_hgrn", og, w_hgrn_out, (t, d), F32, (nm, 1, 1), row_tile, sq_w, row_tile, NN, (tm, d))
    mixed, r1, x1b, *late = _mix_ln1(proj, y_conv, y_hgrn, w_out, x, small["ln1_g"], small["ln1_b"],
                                     hosted=ffn_out_gather)
    if ffn_out_gather is not None:
        (w_ffn_out8,) = late
    w_ffn_out = w_ffn_out8.reshape(4, FF_SHARD, d)
    z = _mm("ffn_in", x1b, w_ffn_in, (N_DEV, t, FF_SHARD), F32, (nm, N_DEV, 1), row_tile,
            pl.BlockSpec((None, d, FF_SHARD), lambda i, j, k: (j, 0, 0)),
            pl.BlockSpec((None, tm, FF_SHARD), lambda i, j, k: (j, i, 0)), NN, (tm, FF_SHARD))
    z = z.reshape(2, 4, t, FF_SHARD)
    act, gelu_and_slope = _ffn_act_fwd(z, small["w_ffn_dw"], small["b_ffn_dw"])

    d_r2, d_r2b, sums_ln2 = _ffn_out_ln2(act, w_ffn_out, r1, target, small["ln1_g"], small["ln1_b"],
                                         small["ln2_g"], small["ln2_b"])
    d_act = _mm("d_act", d_r2b, w_ffn_out, (4, t, FF_SHARD), F32, (nm, 4, 1), row_tile,
                pl.BlockSpec((None, FF_SHARD, d), lambda i, j, k: (j, 0, 0)),
                pl.BlockSpec((None, tm, FF_SHARD), lambda i, j, k: (j, i, 0)), NT, (tm, FF_SHARD))
    g_w_ffn_out = _mm("g_w_ffn_out", act, d_r2b, (4, FF_SHARD, d), BF16, (4, 1, nk),
                      pl.BlockSpec((None, tk, FF_SHARD), lambda i, j, k: (i, k, 0)),
                      pl.BlockSpec((tk, d), lambda i, j, k: (k, 0)),
                      pl.BlockSpec((None, FF_SHARD, d), lambda i, j, k: (i, 0, 0)), TN, (FF_SHARD, d))
    d_z, sums_ffn = _ffn_act_bwd(d_act, z, gelu_and_slope, small["w_ffn_dw"])
    d_z8 = d_z.reshape(N_DEV, t, FF_SHARD)
    d_r1, d_r1b, sums_ln1 = _d_x1_ln1_bwd(d_z8, w_ffn_in, d_r2, r1, small["ln1_g"])
    g_w_ffn_in = _mm("g_w_ffn_in", x1b, d_z8, (N_DEV, d, FF_SHARD), BF16, (N_DEV, 1, nk),
                     pl.BlockSpec((tk, d), lambda i, j, k: (k, 0)),
                     pl.BlockSpec((None, tk, FF_SHARD), lambda i, j, k: (i, k, 0)),
                     pl.BlockSpec((None, d, FF_SHARD), lambda i, j, k: (i, 0, 0)), TN, (d, FF_SHARD))
    k_tile = pl.BlockSpec((tk, d), lambda i, j, k: (k, 0))
    g_w_out = _mm("g_w_out", mixed, d_r1b, (d, d), BF16, (1, 1, nk), k_tile, k_tile, sq_w, TN, (d, d))
    send = (lambda grads: None) if scatter is None else scatter
    g_w_ffn_out = g_w_ffn_out.reshape(N_DEV, D_FF // N_DEV, d)
    d_y, d_proj, *recv_ffn_out = _d_mixed_merge_bwd(d_r1b, w_out, proj, y_conv, y_hgrn, hosted=send([g_w_ffn_out]))
    d_pre, sums_conv = _d_c_norm_bwd(d_y, w_conv_out, conv_pre, small["conv_ln_g"], small["conv_ln_b"])
    g_w_conv_out = _mm("g_w_conv_out", c_act, d_y, (CONV_DIM, d), BF16, (1, 1, nk),
                       pl.BlockSpec((tk, CONV_DIM), lambda i, j, k: (k, 0)),
                       pl.BlockSpec((None, tk, d), lambda i, j, k: (0, k, 0)),
                       pl.BlockSpec((CONV_DIM, d), lambda i, j, k: (0, 0)), TN, (CONV_DIM, d))
    g_w_conv_out = _relayout("g_w_conv_out_shards", g_w_conv_out, (CONV_DIM, 128), lambda j: (0, j),
                             (None, CONV_DIM, 128), lambda j: (j, 0, 0),
                             jax.ShapeDtypeStruct((N_DEV, CONV_DIM, 128), BF16))
    d_og = _mm("d_og", d_y, w_hgrn_out, (t, d), F32, (nm, 1, 1),
               pl.BlockSpec((None, tm, d), lambda i, j, k: (1, i, 0)), sq_w, row_tile, NT, (tm, d))
    g_w_hgrn_out = _mm("g_w_hgrn_out", og, d_y, (d, d), BF16, (1, 1, nk), k_tile,
                       pl.BlockSpec((None, tk, d), lambda i, j, k: (1, k, 0)), sq_w, TN, (d, d))
    d_proj, g_w_conv_dw, *recv_ffn_in = _conv_bwd_dw(d_pre, proj, small["w_conv_dw"], d_proj,
                                                     hosted=send([g_w_ffn_in]))
    early = [g_w_conv_out, g_w_hgrn_out.reshape(N_DEV, d // N_DEV, d), g_w_out.reshape(N_DEV, d // N_DEV, d)]
    d_proj, sums_hgrn, *early_recv = _hgrn_bwd(d_og, o, proj, states, small["hgrn_lb_logits"], small["hgrn_norm_g"],
                                               d_proj, hosted=send(early))
    early += [g_w_ffn_in, g_w_ffn_out]
    early_recv += recv_ffn_in + recv_ffn_out
    wide = IN_COLS // 4
    w_in_bwd = _to_backward_order(jnp.transpose(w_in, (1, 0, 2)).reshape(d, IN_COLS))
    g_w_in = _mm("g_w_in", xb, d_proj, (d, IN_COLS), BF16, (4, 1, nk), k_tile,
                 pl.BlockSpec((tk, wide), lambda i, j, k: (k, i)),
                 pl.BlockSpec((d, wide), lambda i, j, k: (0, i)), TN, (d, wide))
    g_w_in = jnp.transpose(_from_backward_order(g_w_in).reshape(d, N_DEV, IN_SHARD), (1, 0, 2))
    def add_residual(acc, ins, outs, scr):
        for rows in _row_blocks(ta):
            outs[0][rows, :] = ALPHA * ins[2][rows, :] + acc[rows, :]

    ta = _pick(t, 1024)
    acc_tile = pl.BlockSpec((ta, d), lambda i, j, k: (i, 0))
    grad_x, *late_recv = _mm_fused(
        "grad_x", (t // ta, 1, 4), (d_proj, w_in_bwd, d_r1),
        [pl.BlockSpec((ta, wide), lambda i, j, k: (i, k)), pl.BlockSpec((d, wide), lambda i, j, k: (0, k)), acc_tile],
        [jax.ShapeDtypeStruct((t, d), F32)], [acc_tile], NT, (ta, d), add_residual,
        hosted=None if scatter is None else scatter([g_w_in]))

    large_grads = [g_w_in] + early
    if scatter is not None:
        large_grads = list(zip(large_grads, late_recv + early_recv))
    return grad_x, large_grads, (sums_ln2, sums_conv, sums_hgrn, sums_ln1, sums_ffn, g_w_conv_dw)


def _small_views(sums):
    sums_ln2, sums_conv, sums_hgrn, sums_ln1, sums_ffn, g_w_conv_dw = sums
    d_l0 = sums_hgrn[1:2]
    return {
        "loss": sums_ln2[2:3, 0:128],
        "b_conv_dw": sums_conv[2:3], "conv_ln_g": sums_conv[0:1], "conv_ln_b": sums_conv[1:2],
        "hgrn_lb_logits": jnp.concatenate([d_l0, -d_l0], axis=1),
        "hgrn_norm_g": sums_hgrn[0:1],
        "ln1_g": sums_ln1[0:1], "ln1_b": sums_ln1[1:2],
        "b_ffn_dw": sums_ffn[:, FFN_K, :].reshape(1, D_FF),
        "ln2_g": sums_ln2[0:1], "ln2_b": sums_ln2[1:2],
        "w_conv_dw": g_w_conv_dw[0:CONV_K].reshape(1, CONV_K * CONV_DIM),
        "w_ffn_dw": jnp.transpose(sums_ffn[:, 0:FFN_K, :], (1, 0, 2)).reshape(1, FFN_K * D_FF),
    }


def _coords():
    return lax.axis_index("x"), lax.axis_index("y"), lax.axis_index("c")


def _gather(shards, staged=False):
    n = len(shards)
    later = range(1 if staged else 0, n)

    def parts(ins, outs, sems):
        send_sems, recv_sems, local_sems = sems
        x, y, c = _coords()
        me = 4 * x + 2 * y + c
        sibling = (x, y, 1 - c)
        chips = [(1 - x, y), (x, 1 - y), (1 - x, 1 - y)]

        def copy(a, k, block, to, src=None):
            return pltpu.make_async_remote_copy(
                src_ref=outs[a].at[block] if src is None else src, dst_ref=outs[a].at[block],
                send_sem=send_sems.at[a, k], recv_sem=recv_sems.at[a, k], device_id=to, device_id_type=MESH)

        local = [pltpu.make_async_copy(ins[a], outs[a].at[me], local_sems.at[a]) for a in range(n)]
        first = []
        for a in range(n):
            first.append(copy(a, 0, me, sibling, src=ins[a]))
            for j, chip in enumerate(chips):
                first.append(copy(a, 1 + j, me, (*chip, c), src=ins[a]))
        return x, y, c, sibling, chips, copy, local, first

    def start(ins, outs, sems):
        *_, local, first = parts(ins, outs, sems)
        for cp in local + first:
            cp.start()

    def arrive(ins, outs, sems, s):
        x, y, c, sibling, chips, copy, _, _ = parts(ins, outs, sems)
        if s == 1:
            block = 4 * x + 2 * y + 1 - c
            copy(0, 0, block, sibling).wait_recv()
        elif s <= 4:
            px, py = chips[s - 2]
            block = 4 * px + 2 * py + c
            copy(0, s - 1, block, sibling).wait_recv()
            copy(0, s + 2, block, sibling).start()
        else:
            px, py = chips[s - 5]
            block = 4 * px + 2 * py + 1 - c
            copy(0, s - 1, block, sibling).wait_recv()
        return block

    def middle(ins, outs, sems):
        x, y, c, sibling, chips, copy, _, _ = parts(ins, outs, sems)
        for j, (px, py) in enumerate(chips):
            for a in later:
                copy(a, 1 + j, 4 * px + 2 * py + c, sibling).wait_recv()
                copy(a, 4 + j, 4 * px + 2 * py + c, sibling).start()

    def finish(ins, outs, sems):
        x, y, c, sibling, chips, copy, local, first = parts(ins, outs, sems)
        passed = [copy(a, 4 + j, 4 * px + 2 * py + c, sibling) for a in range(n) for j, (px, py) in enumerate(chips)]
        for a in later:
            copy(a, 0, 4 * x + 2 * y + 1 - c, sibling).wait_recv()
            for j, (px, py) in enumerate(chips):
                copy(a, 4 + j, 4 * px + 2 * py + 1 - c, sibling).wait_recv()
        for cp in first + passed:
            cp.wait_send()
        for cp in local:
            cp.wait()

    hosted = _Hosted(shards, [jax.ShapeDtypeStruct((N_DEV,) + s.shape, s.dtype) for s in shards],
                     [pltpu.SemaphoreType.DMA((n, 7)), pltpu.SemaphoreType.DMA((n, 7)), pltpu.SemaphoreType.DMA((n,))],
                     start, finish, middle)
    hosted.arrive = arrive
    return hosted


def _proj_gather(xb, gather, order):
    t, d = xb.shape
    tm = _pick(t, 2048)
    nm = t // tm
    n_in, n_out = len(gather.inputs), len(gather.out_shapes)

    def body(order_ref, x_ref, *refs):
        ins, refs = refs[:n_in], refs[n_in:]
        o_ref, outs, refs = refs[0], refs[1:1 + n_out], refs[1 + n_out:]
        w_buf, w_sem, sems = refs[0], refs[1], refs[2:]
        s, i = pl.program_id(0), pl.program_id(1)

        @pl.when((s == 0) & (i == 0))
        def _():
            gather.start(ins, outs, sems)

        def staging(step, src):
            return pltpu.make_async_copy(src, w_buf.at[step % 2], w_sem.at[step % 2])

        @pl.when((s == 0) & (i == 0))
        def _():
            staging(0, ins[0]).start()

        for step in range(1, N_DEV):
            @pl.when((s == step - 1) & (i == nm - 1))
            def _(step=step):
                staging(step, outs[0].at[gather.arrive(ins, outs, sems, step)]).start()

        for step in range(N_DEV):
            @pl.when((s == step) & (i == 0))
            def _(step=step):
                staging(step, ins[0]).wait()

        o_ref[...] = _dot(x_ref[...], w_buf[s % 2], NN)

        @pl.when((s == N_DEV - 2) & (i == 0))
        def _():
            gather.middle(ins, outs, sems)

        @pl.when((s == N_DEV - 1) & (i == nm - 1))
        def _():
            gather.finish(ins, outs, sems)

    outs = pl.pallas_call(
        body, name="proj_gather",
        grid_spec=pltpu.PrefetchScalarGridSpec(
            num_scalar_prefetch=1, grid=(N_DEV, nm),
            in_specs=[pl.BlockSpec((tm, d), lambda s, i, order_ref: (i, 0))] + [ANY] * n_in,
            out_specs=[pl.BlockSpec((tm, IN_SHARD), lambda s, i, order_ref: (i, order_ref[s]))] + [ANY] * n_out,
            scratch_shapes=[pltpu.VMEM((2, d, IN_SHARD), BF16), pltpu.SemaphoreType.DMA((2,))] + gather.sem_shapes),
        out_shape=[jax.ShapeDtypeStruct((t, IN_COLS), F32)] + gather.out_shapes,
        compiler_params=pltpu.CompilerParams(dimension_semantics=("arbitrary", "arbitrary"),
                                             vmem_limit_bytes=VMEM_LIMIT, has_side_effects=True),
    )(order, xb, *gather.inputs)
    return outs[0], list(outs[1:])


def _scatter(grads):
    n = len(grads)

    def copies(ins, outs, sems):
        send_sems, recv_sems = sems
        x, y, c = _coords()
        out = []
        for a in range(n):
            for k in range(1, N_DEV):
                px, py, pc = x ^ (k >> 2), y ^ ((k >> 1) & 1), c ^ (k & 1)
                out.append(pltpu.make_async_remote_copy(
                    src_ref=ins[a].at[4 * px + 2 * py + pc], dst_ref=outs[a].at[k - 1],
                    send_sem=send_sems.at[a, k - 1], recv_sem=recv_sems.at[a, k - 1],
                    device_id=(px, py, pc), device_id_type=MESH))
        return out

    def start(ins, outs, sems):
        for cp in copies(ins, outs, sems):
            cp.start()

    def finish(ins, outs, sems):
        for cp in copies(ins, outs, sems):
            cp.wait()

    return _Hosted(grads, [jax.ShapeDtypeStruct((N_DEV - 1,) + g.shape[1:], g.dtype) for g in grads],
                   [pltpu.SemaphoreType.DMA((n, N_DEV - 1)), pltpu.SemaphoreType.DMA((n, N_DEV - 1))], start, finish)


def _row_tile(rows):
    return 256 if rows % 256 == 0 else rows


def _adam_math(w, g, m, v):
    m_new = ADAM_B1 * m + (1.0 - ADAM_B1) * g
    v_new = ADAM_B2 * v + (1.0 - ADAM_B2) * (g * g)
    m_hat = m_new / (1.0 - ADAM_B1 ** ADAM_STEP)
    v_hat = v_new / (1.0 - ADAM_B2 ** ADAM_STEP)
    delta = -ADAM_LR * (m_hat / (jnp.sqrt(v_hat) + ADAM_EPS) + ADAM_WD * w)
    return delta, m_new, v_new


def _adam_large(name, own, recv, me, w, m, v):
    rows, cols = w.shape
    tr = _row_tile(rows)

    def body(me_ref, p_ref, r_ref, w_ref, m_ref, v_ref, g_out, d_out, m_out, v_out):
        g = p_ref[...].astype(F32)
        for k in range(N_DEV - 1):
            g = g + r_ref[k].astype(F32)
        delta, m_new, v_new = _adam_math(w_ref[...], g, m_ref[...], v_ref[...])
        g_out[...] = g
        d_out[...] = delta
        m_out[...] = m_new
        v_out[...] = v_new

    tile = pl.BlockSpec((tr, cols), lambda r, me_ref: (r, 0))
    sds = jax.ShapeDtypeStruct((rows, cols), F32)
    return pl.pallas_call(
        body, name=name,
        grid_spec=pltpu.PrefetchScalarGridSpec(
            num_scalar_prefetch=1, grid=(rows // tr,),
            in_specs=[pl.BlockSpec((None, tr, cols), lambda r, me_ref: (me_ref[0], r, 0)),
                      pl.BlockSpec((N_DEV - 1, tr, cols), lambda r, me_ref: (0, r, 0)), tile, tile, tile],
            out_specs=[tile, tile, tile, tile]),
        out_shape=[sds, sds, sds, sds],
        compiler_params=_params(("parallel",)),
    )(me, own, recv, w, m, v)


def _small_allreduce(arrays):
    n = len(arrays)

    def body(*refs):
        ins, outs, gats = refs[:n], refs[n:2 * n], refs[2 * n:3 * n]
        send_sems, recv_sems = refs[3 * n:]
        x, y, c = _coords()
        me = 4 * x + 2 * y + c
        peers = [(x ^ (k >> 2), y ^ ((k >> 1) & 1), c ^ (k & 1)) for k in range(1, N_DEV)]

        def copy(a, k, slot):
            return pltpu.make_async_remote_copy(
                src_ref=ins[a], dst_ref=gats[a].at[slot], send_sem=send_sems.at[a, k], recv_sem=recv_sems.at[a, k],
                device_id=peers[k], device_id_type=MESH)

        sends = [copy(a, k, me) for a in range(n) for k in range(N_DEV - 1)]
        for a in range(n):
            gats[a][me] = ins[a][...]
        for cp in sends:
            cp.start()
        for a in range(n):
            for k, (px, py, pc) in enumerate(peers):
                copy(a, k, 4 * px + 2 * py + pc).wait_recv()
        for cp in sends:
            cp.wait_send()
        for a in range(n):
            acc = gats[a][0]
            for dev in range(1, N_DEV):
                acc = acc + gats[a][dev]
            outs[a][...] = acc

    whole = pl.BlockSpec(memory_space=pltpu.VMEM)
    return pl.pallas_call(
        body, name="small_allreduce", in_specs=[whole] * n, out_specs=[whole] * n,
        out_shape=[jax.ShapeDtypeStruct(a.shape, F32) for a in arrays],
        scratch_shapes=[pltpu.VMEM((N_DEV,) + a.shape, F32) for a in arrays]
        + [pltpu.SemaphoreType.DMA((n, N_DEV - 1)), pltpu.SemaphoreType.DMA((n, N_DEV - 1))],
        compiler_params=pltpu.CompilerParams(has_side_effects=True, vmem_limit_bytes=VMEM_LIMIT),
    )(*arrays)


def _adam_replicated(sums, w, m, v):
    rows_of = {"conv_ln_g": (1, 0), "conv_ln_b": (1, 1), "b_conv_dw": (1, 2), "hgrn_norm_g": (2, 0),
               "ln1_g": (3, 0), "ln1_b": (3, 1), "ln2_g": (0, 0), "ln2_b": (0, 1)}
    names = list(rows_of) + ["hgrn_lb_logits"]
    n = len(names)

    def body(*refs):
        sum_refs, refs = refs[:4], refs[4:]
        w_refs, m_refs, v_refs, outs = refs[:n], refs[n:2 * n], refs[2 * n:3 * n], refs[3 * n:]
        for j, name in enumerate(names):
            if name == "hgrn_lb_logits":
                d_l0 = sum_refs[2][1:2, :]
                grads = [d_l0, -d_l0]
            else:
                a, row = rows_of[name]
                grads = [sum_refs[a][row:row + 1, :]]
            g_out, d_out, m_out, v_out = outs[4 * j:4 * j + 4]
            for r, g in enumerate(grads):
                rows = slice(r, r + 1)
                delta, m_new, v_new = _adam_math(w_refs[j][rows, :], g, m_refs[j][rows, :], v_refs[j][rows, :])
                g_out[rows, :] = g
                d_out[rows, :] = delta
                m_out[rows, :] = m_new
                v_out[rows, :] = v_new

    whole = pl.BlockSpec(memory_space=pltpu.VMEM)
    operands = list(sums) + [w[k] for k in names] + [m[k] for k in names] + [v[k] for k in names]
    outs = pl.pallas_call(
        body, name="adam_replicated", in_specs=[whole] * len(operands), out_specs=[whole] * (4 * n),
        out_shape=[jax.ShapeDtypeStruct(w[k].shape, F32) for k in names for _ in range(4)],
    )(*operands)
    return {name: tuple(outs[4 * j:4 * j + 4]) for j, name in enumerate(names)}


def _adam_small(w, g, m, v):
    def body(w_ref, g_ref, m_ref, v_ref, d_out, m_out, v_out):
        delta, m_new, v_new = _adam_math(w_ref[...], g_ref[...], m_ref[...], v_ref[...])
        d_out[...] = delta
        m_out[...] = m_new
        v_out[...] = v_new

    whole = pl.BlockSpec(memory_space=pltpu.VMEM)
    sds = jax.ShapeDtypeStruct(w.shape, F32)
    return pl.pallas_call(body, name="adam_small", in_specs=[whole] * 4, out_specs=[whole] * 3,
                          out_shape=[sds, sds, sds])(w, g, m, v)


_WEIGHTS = ["w_in", "w_conv_dw", "b_conv_dw", "conv_ln_g", "conv_ln_b", "w_conv_out", "hgrn_lb_logits", "hgrn_norm_g",
            "w_hgrn_out", "w_out", "ln1_g", "ln1_b", "w_ffn_in", "w_ffn_dw", "b_ffn_dw", "w_ffn_out", "ln2_g", "ln2_b"]
_LARGE = ["w_in", "w_conv_out", "w_hgrn_out", "w_out", "w_ffn_in", "w_ffn_out"]
_CONV_DW_SHARD = CONV_DIM // N_DEV
_FFN_DW_SHARD = D_FF // N_DEV


def kernel(x, w_in, w_conv_dw, b_conv_dw, conv_ln_g, conv_ln_b, w_conv_out, hgrn_lb_logits, hgrn_norm_g, w_hgrn_out, w_out, ln1_g, ln1_b, w_ffn_in, w_ffn_dw, b_ffn_dw, w_ffn_out, ln2_g, ln2_b, loss_target, m_w_in, m_w_conv_dw, m_b_conv_dw, m_conv_ln_g, m_conv_ln_b, m_w_conv_out, m_hgrn_lb_logits, m_hgrn_norm_g, m_w_hgrn_out, m_w_out, m_ln1_g, m_ln1_b, m_w_ffn_in, m_w_ffn_dw, m_b_ffn_dw, m_w_ffn_out, m_ln2_g, m_ln2_b, v_w_in, v_w_conv_dw, v_b_conv_dw, v_conv_ln_g, v_conv_ln_b, v_w_conv_out, v_hgrn_lb_logits, v_hgrn_norm_g, v_w_hgrn_out, v_w_out, v_ln1_g, v_ln1_b, v_w_ffn_in, v_w_ffn_dw, v_b_ffn_dw, v_w_ffn_out, v_ln2_g, v_ln2_b):
    w = dict(w_in=w_in, w_conv_dw=w_conv_dw, b_conv_dw=b_conv_dw, conv_ln_g=conv_ln_g, conv_ln_b=conv_ln_b,
             w_conv_out=w_conv_out, hgrn_lb_logits=hgrn_lb_logits, hgrn_norm_g=hgrn_norm_g, w_hgrn_out=w_hgrn_out,
             w_out=w_out, ln1_g=ln1_g, ln1_b=ln1_b, w_ffn_in=w_ffn_in, w_ffn_dw=w_ffn_dw, b_ffn_dw=b_ffn_dw,
             w_ffn_out=w_ffn_out, ln2_g=ln2_g, ln2_b=ln2_b)
    m = dict(w_in=m_w_in, w_conv_dw=m_w_conv_dw, b_conv_dw=m_b_conv_dw, conv_ln_g=m_conv_ln_g, conv_ln_b=m_conv_ln_b,
             w_conv_out=m_w_conv_out, hgrn_lb_logits=m_hgrn_lb_logits, hgrn_norm_g=m_hgrn_norm_g,
             w_hgrn_out=m_w_hgrn_out, w_out=m_w_out, ln1_g=m_ln1_g, ln1_b=m_ln1_b, w_ffn_in=m_w_ffn_in,
             w_ffn_dw=m_w_ffn_dw, b_ffn_dw=m_b_ffn_dw, w_ffn_out=m_w_ffn_out, ln2_g=m_ln2_g, ln2_b=m_ln2_b)
    v = dict(w_in=v_w_in, w_conv_dw=v_w_conv_dw, b_conv_dw=v_b_conv_dw, conv_ln_g=v_conv_ln_g, conv_ln_b=v_conv_ln_b,
             w_conv_out=v_w_conv_out, hgrn_lb_logits=v_hgrn_lb_logits, hgrn_norm_g=v_hgrn_norm_g,
             w_hgrn_out=v_w_hgrn_out, w_out=v_w_out, ln1_g=v_ln1_g, ln1_b=v_ln1_b, w_ffn_in=v_w_ffn_in,
             w_ffn_dw=v_w_ffn_dw, b_ffn_dw=v_b_ffn_dw, w_ffn_out=v_w_ffn_out, ln2_g=v_ln2_g, ln2_b=v_ln2_b)
    xi, yi, ci = lax.axis_index("x"), lax.axis_index("y"), lax.axis_index("c")
    me = 4 * xi + 2 * yi + ci
    me_op = jnp.reshape(me, (1,)).astype(jnp.int32)

    shards = [w[name][0].astype(BF16) for name in _LARGE]
    shards.append(jnp.pad(w_conv_dw[0], ((0, 1), (0, 128 - _CONV_DW_SHARD))))
    shards.append(jnp.pad(w_ffn_dw[0], ((0, 8 - FFN_K), (0, 384 - _FFN_DW_SHARD))))
    chips = [(1 - xi, yi), (xi, 1 - yi), (1 - xi, 1 - yi)]
    order = jnp.stack([me, me ^ 1] + [4 * px + 2 * py + ci for px, py in chips]
                      + [4 * px + 2 * py + 1 - ci for px, py in chips]).astype(jnp.int32)
    small = dict(b_conv_dw=b_conv_dw, conv_ln_g=conv_ln_g, conv_ln_b=conv_ln_b, hgrn_lb_logits=hgrn_lb_logits,
                 hgrn_norm_g=hgrn_norm_g, ln1_g=ln1_g, ln1_b=ln1_b, ln2_g=ln2_g, ln2_b=ln2_b, b_ffn_dw=b_ffn_dw)

    gathers = (_gather(shards[:4] + shards[6:], staged=True), _gather(shards[4:5]), _gather(shards[5:6]))
    grad_x, large_grads, small_sums = _local_step(x[0], loss_target[0], gathers, small, _scatter, order)

    out = {}
    for name, (own, recv) in zip(_LARGE, large_grads):
        out[name] = _adam_large("adam_" + name, own, recv, me_op, w[name][0], m[name][0], v[name][0])

    totals = _small_allreduce(list(small_sums))
    out.update(_adam_replicated(totals[:4], w, m, v))
    summed = _small_views(totals)
    loss = summed["loss"][0, 0]
    conv_dw_g = lax.dynamic_slice_in_dim(summed["w_conv_dw"].reshape(CONV_K, CONV_DIM), me * _CONV_DW_SHARD, _CONV_DW_SHARD, axis=1)
    ffn_dw_g = lax.dynamic_slice_in_dim(summed["w_ffn_dw"].reshape(FFN_K, D_FF), me * _FFN_DW_SHARD, _FFN_DW_SHARD, axis=1)
    small_g = dict(b_ffn_dw=summed["b_ffn_dw"], w_conv_dw=conv_dw_g.reshape(1, -1), w_ffn_dw=ffn_dw_g.reshape(1, -1))
    names = list(small_g)
    flat = lambda d, n: d[n].reshape(1, -1)
    n_small = sum(small_g[n].shape[1] for n in names)
    pad = (-n_small) % 1024
    pack = lambda pieces: jnp.pad(jnp.concatenate(pieces, axis=1), ((0, 0), (0, pad))).reshape(-1, 128)
    d_s, m_s, v_s = _adam_small(pack([flat(w, n) for n in names]), pack([small_g[n] for n in names]),
                                pack([flat(m, n) for n in names]), pack([flat(v, n) for n in names]))
    pos = 0
    for n in names:
        size = small_g[n].shape[1]
        cut = lambda a: a.reshape(1, -1)[:, pos:pos + size].reshape(w[n].shape)
        out[n] = (small_g[n].reshape(w[n].shape), cut(d_s), cut(m_s), cut(v_s))
        pos += size

    for name in _LARGE:
        out[name] = tuple(a.reshape(w[name].shape) for a in out[name])
    grads = [out[n][0] for n in _WEIGHTS]
    deltas = [out[n][1] for n in _WEIGHTS]
    new_m = [out[n][2] for n in _WEIGHTS]
    new_v = [out[n][3] for n in _WEIGHTS]
    return (loss, grad_x[None], *grads, *deltas, *new_m, *new_v)
```

```python
import functools
import math

import jax
import jax.numpy as jnp
from jax import lax
from jax.experimental import pallas as pl
from jax.experimental.pallas import tpu as pltpu

F32 = jnp.float32
BF16 = jnp.bfloat16

N_DEV = 8
D_MODEL = 1024
CONV_DIM = 512
CONV_K = 31
HGRN_DIM = 1024
HEADS = 8
HEAD_DIM = 128
D_FF = 2816
FFN_K = 3
FF_SHARD = 2 * D_FF // N_DEV
IN_COLS = 7168
IN_SHARD = IN_COLS // N_DEV
LN_EPS = 1e-5
RMS_EPS = 1e-6
ALPHA = 2.0 ** 0.25

ADAM_LR = 0.001
ADAM_B1 = 0.9
ADAM_B2 = 0.999
ADAM_EPS = 1e-08
ADAM_WD = 0.01
ADAM_STEP = 10

CHUNK = 64
CHUNKS_PER_BLOCK = 32
CONV_HALO = 32
FFN_HALO = 8
ROW_BLOCK = 64
SUBLANES = 8
VMEM_LIMIT = 48 * 1024 * 1024
MXU_DEPTH = 256

DP_MERGE_BLOCK = 0
DP_CONV_BLOCK = 2
DP_HEAD_BLOCK = 6

MESH = pl.DeviceIdType.MESH
ANY = pl.BlockSpec(memory_space=pl.ANY)

NN = (((1,), (0,)), ((), ()))
NT = (((1,), (1,)), ((), ()))
TN = (((0,), (0,)), ((), ()))


def _params(sem):
    return pltpu.CompilerParams(dimension_semantics=sem, vmem_limit_bytes=VMEM_LIMIT)


def _dot(a, b, dims):
    return lax.dot_general(a.astype(BF16), b.astype(BF16), dims, preferred_element_type=F32)


def _sigmoid(x):
    return jax.nn.sigmoid(x)


def _ln(r):
    mu = jnp.mean(r, axis=-1, keepdims=True)
    xc = r - mu
    var = jnp.mean(xc * xc, axis=-1, keepdims=True)
    rstd = lax.rsqrt(var + LN_EPS)
    return xc * rstd, rstd


def _ln_bwd(dy, xhat, rstd, g):
    dxh = dy * g
    m1 = jnp.mean(dxh, axis=-1, keepdims=True)
    m2 = jnp.mean(dxh * xhat, axis=-1, keepdims=True)
    return rstd * (dxh - m1 - xhat * m2)


def _colsum(x):
    return jnp.sum(x, axis=0, keepdims=True)


class _Hosted:
    def __init__(self, inputs, out_shapes, sem_shapes, start, finish, middle=None):
        self.inputs, self.out_shapes, self.sem_shapes = list(inputs), list(out_shapes), list(sem_shapes)
        self.start, self.finish, self.middle = start, finish, middle


def _call(body, *, name, grid, in_specs, out_specs, out_shape, scratch_shapes, semantics, operands, hosted=None,
          aliases=None):
    aliases = aliases or {}
    if hosted is None:
        return pl.pallas_call(
            body, name=name, grid=grid, in_specs=list(in_specs), out_specs=list(out_specs), out_shape=list(out_shape),
            scratch_shapes=list(scratch_shapes), input_output_aliases=aliases,
            compiler_params=_params(semantics))(*operands)
    n_in, n_out, n_scr = len(in_specs), len(out_specs), len(scratch_shapes)
    h_in, h_out = len(hosted.inputs), len(hosted.out_shapes)

    def full_body(*refs):
        ins, refs = refs[:n_in], refs[n_in:]
        h_ins, refs = refs[:h_in], refs[h_in:]
        outs, refs = refs[:n_out], refs[n_out:]
        h_outs, refs = refs[:h_out], refs[h_out:]
        scr, sems = refs[:n_scr], refs[n_scr:]
        first = functools.reduce(jnp.logical_and, [pl.program_id(d) == 0 for d in range(len(grid))])
        last = functools.reduce(jnp.logical_and, [pl.program_id(d) == grid[d] - 1 for d in range(len(grid))])

        @pl.when(first)
        def _():
            hosted.start(h_ins, h_outs, sems)

        body(*ins, *outs, *scr)

        if hosted.middle is not None:
            step, total = 0, 1
            for d in range(len(grid)):
                step, total = step * grid[d] + pl.program_id(d), total * grid[d]

            @pl.when(step == (2 * total) // 3)
            def _():
                hosted.middle(h_ins, h_outs, sems)

        @pl.when(last)
        def _():
            hosted.finish(h_ins, h_outs, sems)

    return pl.pallas_call(
        full_body, name=name, grid=grid, in_specs=list(in_specs) + [ANY] * h_in,
        out_specs=list(out_specs) + [ANY] * h_out, out_shape=list(out_shape) + hosted.out_shapes,
        scratch_shapes=list(scratch_shapes) + hosted.sem_shapes, input_output_aliases=aliases,
        compiler_params=pltpu.CompilerParams(dimension_semantics=("arbitrary",) * len(grid),
                                             vmem_limit_bytes=VMEM_LIMIT, has_side_effects=True),
    )(*operands, *hosted.inputs)


def _mm(name, a, b, out_shape, out_dtype, grid, a_spec, b_spec, o_spec, dims, acc_shape, hosted=None):
    nk = grid[2]
    if nk == 1:
        def body(a_ref, b_ref, o_ref):
            o_ref[...] = _dot(a_ref[...], b_ref[...], dims).astype(o_ref.dtype)
        scratch = []
    else:
        def body(a_ref, b_ref, o_ref, acc_ref):
            k = pl.program_id(2)

            @pl.when(k == 0)
            def _():
                acc_ref[...] = jnp.zeros_like(acc_ref)

            acc_ref[...] += _dot(a_ref[...], b_ref[...], dims)

            @pl.when(k == nk - 1)
            def _():
                o_ref[...] = acc_ref[...].astype(o_ref.dtype)
        scratch = [pltpu.VMEM(acc_shape, F32)]

    outs = _call(body, name=name, grid=grid, in_specs=[a_spec, b_spec], out_specs=[o_spec],
                 out_shape=[jax.ShapeDtypeStruct(out_shape, out_dtype)], scratch_shapes=scratch,
                 semantics=("parallel", "parallel", "arbitrary"), operands=(a, b), hosted=hosted)
    return outs[0] if hosted is None else (outs[0], list(outs[1:]))


def _mm_fused(name, grid, operands, in_specs, out_shape, out_specs, dims, acc_shape, epilogue, lhs=None, scratch=(),
              hosted=None):
    nk = grid[2]
    n_in, n_out = len(in_specs), len(out_specs)

    def body(*refs):
        ins, outs, scr = refs[:n_in], refs[n_in:n_in + n_out], refs[n_in + n_out:]
        acc_ref, k = scr[0], pl.program_id(2)
        a = ins[0][...] if lhs is None else lhs(ins, outs)
        part = _dot(a, ins[1][...], dims)
        if nk == 1:
            acc_ref[...] = part
            epilogue(acc_ref, ins, outs, scr[1:])
            return

        @pl.when(k == 0)
        def _():
            acc_ref[...] = jnp.zeros_like(acc_ref)

        acc_ref[...] += part

        @pl.when(k == nk - 1)
        def _():
            epilogue(acc_ref, ins, outs, scr[1:])

    return _call(body, name=name, grid=grid, in_specs=in_specs, out_specs=out_specs, out_shape=out_shape,
                 scratch_shapes=[pltpu.VMEM(acc_shape, F32)] + list(scratch), semantics=("arbitrary",) * 3,
                 operands=operands, hosted=hosted)


def _row_blocks(rows, block=256):
    block = block if rows % block == 0 else rows
    return [slice(r, r + block) for r in range(0, rows, block)]


def _pick(t, pref):
    return pref if t % pref == 0 else t


def _glu(p):
    return p[:, :CONV_DIM] * _sigmoid(p[:, CONV_DIM:])


def _by_phase(taps):
    phases = {}
    for off, payload in taps:
        phases.setdefault(off % SUBLANES, []).append((off - off % SUBLANES, payload))
    return sorted(phases.items())


def _tap_sum(src_ref, base, taps, rows, lanes):
    acc = None
    for phase, items in _by_phase(taps):
        n = rows if phase == 0 else rows + SUBLANES
        part = None
        for off, (w_ref, k) in items:
            term = w_ref[k:k + 1, lanes] * src_ref[base + off:base + off + n, lanes]
            part = term if part is None else part + term
        if phase:
            part = part[phase:phase + rows, :]
        acc = part if acc is None else acc + part
    return acc


def _tap_products(x, src_ref, base, taps, lanes):
    rows, cols = x.shape
    pad = jnp.zeros((SUBLANES, cols), x.dtype)
    padded = jnp.concatenate([pad, x, pad], axis=0)
    out = []
    for phase, items in _by_phase(taps):
        n = rows if phase == 0 else rows + SUBLANES
        shifted = x if phase == 0 else padded[SUBLANES - phase:SUBLANES - phase + n, :]
        for off, key in items:
            out.append((key, _colsum(shifted * src_ref[base + off:base + off + n, lanes])))
    return out


def _lane_blocks(cols, block=256):
    return [slice(c, min(c + block, cols)) for c in range(0, cols, block)]


def _conv_fwd(proj, w_dw, b_dw, g, b):
    t = proj.shape[0]
    tm = _pick(t, 512)
    nh = tm // CONV_HALO

    def body(p_ref, ph_ref, w_ref, bd_ref, g_ref, b_ref, act_ref, pre_ref, xs_ref):
        i = pl.program_id(0)
        halo = _glu(ph_ref[...])
        xs_ref[0:CONV_HALO, :] = jnp.where(i == 0, 0.0, halo)
        xs_ref[CONV_HALO:CONV_HALO + tm, :] = _glu(p_ref[...])
        taps = [(CONV_HALO - (CONV_K - 1) + k, (w_ref, k)) for k in range(CONV_K)]
        for r in range(tm // ROW_BLOCK):
            rows = slice(r * ROW_BLOCK, (r + 1) * ROW_BLOCK)
            for lanes in _lane_blocks(CONV_DIM):
                pre_ref[rows, lanes] = bd_ref[:, lanes] + _tap_sum(xs_ref, r * ROW_BLOCK, taps, ROW_BLOCK, lanes)
            acc = pre_ref[rows, :]
            xhat, _ = _ln(acc)
            yln = xhat * g_ref[...] + b_ref[...]
            act_ref[rows, :] = (yln * _sigmoid(yln)).astype(BF16)

    full = lambda s: pl.BlockSpec(s, lambda i: (0, 0))
    return pl.pallas_call(
        body, name="conv_fwd", grid=(t // tm,),
        in_specs=[pl.BlockSpec((tm, 2 * CONV_DIM), lambda i: (i, 0)),
                  pl.BlockSpec((CONV_HALO, 2 * CONV_DIM), lambda i: (jnp.maximum(i * nh - 1, 0), 0)),
                  full((CONV_K, CONV_DIM)), full((1, CONV_DIM)), full((1, CONV_DIM)), full((1, CONV_DIM))],
        out_specs=[pl.BlockSpec((tm, CONV_DIM), lambda i: (i, 0)), pl.BlockSpec((tm, CONV_DIM), lambda i: (i, 0))],
        out_shape=[jax.ShapeDtypeStruct((t, CONV_DIM), BF16), jax.ShapeDtypeStruct((t, CONV_DIM), F32)],
        scratch_shapes=[pltpu.VMEM((CONV_HALO + tm, CONV_DIM), F32)],
        compiler_params=_params(("arbitrary",)),
    )(proj, proj, w_dw, b_dw, g, b)


def _d_c_norm_bwd(d_y, w_conv_out, pre, g, b):
    t = pre.shape[0]
    tm = _pick(t, 512)
    d = D_MODEL

    def epilogue(d_c, ins, outs, scr):
        pre_ref, g_ref, b_ref = ins[2:]
        dpre_ref, sums_ref = outs
        i = pl.program_id(0)

        @pl.when(i == 0)
        def _():
            sums_ref[...] = jnp.zeros_like(sums_ref)

        for rows in _row_blocks(tm):
            xhat, rstd = _ln(pre_ref[rows, :])
            yln = xhat * g_ref[...] + b_ref[...]
            sg = _sigmoid(yln)
            dyln = d_c[rows, :] * (sg * (1.0 + yln * (1.0 - sg)))
            dpre = _ln_bwd(dyln, xhat, rstd, g_ref[...])
            dpre_ref[rows, :] = dpre
            sums_ref[0:1, :] += _colsum(dyln * xhat)
            sums_ref[1:2, :] += _colsum(dyln)
            sums_ref[2:3, :] += _colsum(dpre)

    full = lambda s: pl.BlockSpec(s, lambda i, j, k: (0, 0))
    tile = pl.BlockSpec((tm, CONV_DIM), lambda i, j, k: (i, 0))
    return _mm_fused(
        "d_c_norm_bwd", (t // tm, 1, 1), (d_y, w_conv_out, pre, g, b),
        [pl.BlockSpec((None, tm, d), lambda i, j, k: (0, i, 0)), full((CONV_DIM, d)), tile,
         full((1, CONV_DIM)), full((1, CONV_DIM))],
        [jax.ShapeDtypeStruct((t, CONV_DIM), F32), jax.ShapeDtypeStruct((8, CONV_DIM), F32)],
        [tile, full((8, CONV_DIM))], NT, (tm, CONV_DIM), epilogue)


def _conv_bwd_dw(d_pre, proj, w_dw, d_proj, hosted=None):
    t = d_pre.shape[0]
    tm = _pick(t, 512)
    nt = t // tm
    nh = tm // CONV_HALO
    last_h = t // CONV_HALO - 1

    def body(dp_ref, dph_ref, p_ref, ph_ref, w_ref, _, dproj_ref, dw_ref, xs_ref, ds_ref):
        i = pl.program_id(0)

        @pl.when(i == 0)
        def _():
            dw_ref[...] = jnp.zeros_like(dw_ref)

        halo = _glu(ph_ref[...])
        xs_ref[0:CONV_HALO, :] = jnp.where(i == 0, 0.0, halo)
        xs_ref[CONV_HALO:CONV_HALO + tm, :] = _glu(p_ref[...])
        ds_ref[0:tm, :] = dp_ref[...]
        ds_ref[tm:tm + CONV_HALO, :] = jnp.where(i == nt - 1, 0.0, dph_ref[...])
        back_taps = [(CONV_K - 1 - k, (w_ref, k)) for k in range(CONV_K)]
        grad_taps = [(CONV_HALO - (CONV_K - 1) + k, k) for k in range(CONV_K)]
        for r in range(tm // ROW_BLOCK):
            base = r * ROW_BLOCK
            rows = slice(base, base + ROW_BLOCK)
            for lanes in _lane_blocks(CONV_DIM):
                gate_lanes = slice(CONV_DIM + lanes.start, CONV_DIM + lanes.stop)
                acc = _tap_sum(ds_ref, base, back_taps, ROW_BLOCK, lanes)
                for k, total in _tap_products(ds_ref[rows, lanes], xs_ref, base, grad_taps, lanes):
                    dw_ref[k:k + 1, lanes] += total
                cval = p_ref[rows, lanes]
                sg = _sigmoid(p_ref[rows, gate_lanes])
                dproj_ref[rows, lanes] = (acc * sg).astype(BF16)
                dproj_ref[rows, gate_lanes] = (acc * cval * sg * (1.0 - sg)).astype(BF16)

    full = lambda s: pl.BlockSpec(s, lambda i: (0, 0))
    return _call(
        body, name="conv_bwd_dw", grid=(nt,),
        in_specs=[pl.BlockSpec((tm, CONV_DIM), lambda i: (i, 0)),
                  pl.BlockSpec((CONV_HALO, CONV_DIM), lambda i: (jnp.minimum((i + 1) * nh, last_h), 0)),
                  pl.BlockSpec((tm, 2 * CONV_DIM), lambda i: (i, 0)),
                  pl.BlockSpec((CONV_HALO, 2 * CONV_DIM), lambda i: (jnp.maximum(i * nh - 1, 0), 0)),
                  full((CONV_K, CONV_DIM)), ANY],
        out_specs=[pl.BlockSpec((tm, 2 * CONV_DIM), lambda i: (i, DP_CONV_BLOCK)), full((CONV_HALO, CONV_DIM))],
        out_shape=[jax.ShapeDtypeStruct(d_proj.shape, BF16), jax.ShapeDtypeStruct((CONV_HALO, CONV_DIM), F32)],
        scratch_shapes=[pltpu.VMEM((CONV_HALO + tm, CONV_DIM), F32), pltpu.VMEM((tm + CONV_HALO, CONV_DIM), F32)],
        semantics=("arbitrary",), operands=(d_pre, d_pre, proj, proj, w_dw, d_proj), hosted=hosted, aliases={5: 0})


def _lower_bound(logit_ref):
    l0 = logit_ref[0:1, :]
    l1 = logit_ref[1:2, :]
    m = jnp.maximum(l0, l1)
    e0 = jnp.exp(l0 - m)
    e1 = jnp.exp(l1 - m)
    return e0 / (e0 + e1)


def _tri(lower):
    r = lax.broadcasted_iota(jnp.int32, (CHUNK, CHUNK), 0)
    c = lax.broadcasted_iota(jnp.int32, (CHUNK, CHUNK), 1)
    return (c <= r) if lower else (c >= r)


def _hgrn_gates(fz, lb):
    s = _sigmoid(fz)
    sn = _sigmoid(-fz)
    f = lb + (1.0 - lb) * s
    return s, sn, f


def _block_tri(rows, lower=True):
    r = lax.broadcasted_iota(jnp.int32, (rows, rows), 0)
    c = lax.broadcasted_iota(jnp.int32, (rows, rows), 1)
    tri = (c <= r) if lower else (c >= r)
    return (tri & (r // CHUNK == c // CHUNK)).astype(BF16)


def _tri_rows(tm):
    return min(tm, MXU_DEPTH)


def _tri_matmul(tri_ref, x):
    hi = x.astype(BF16)
    lo = (x - hi.astype(F32)).astype(BF16)
    tri = tri_ref[...]
    return (lax.dot_general(tri, hi, NN, preferred_element_type=F32)
            + lax.dot_general(tri, lo, NN, preferred_element_type=F32))


def _groups(tm):
    g = _tri_rows(tm)
    return [slice(i * g, (i + 1) * g) for i in range(tm // g)]


def _hgrn_fwd(proj, logits, norm_g, hosted=None):
    t = proj.shape[0]
    tm = CHUNK * CHUNKS_PER_BLOCK if t % (CHUNK * CHUNKS_PER_BLOCK) == 0 else CHUNK
    cpb = tm // CHUNK
    nt = t // tm
    half = CHUNK // 2

    def body(qz_ref, fz_ref, iv_ref, gz_ref, lg_ref, ng_ref, tri_ref, o_ref, og_ref, st_ref,
             state_ref, qe_ref, ke_ref, qb_ref, kl_ref, v_ref, upd_ref, decay_ref, a_ref, q_ref, kk_ref, b_ref):
        j = pl.program_id(1)

        @pl.when(j == 0)
        def _():
            state_ref[...] = jnp.zeros_like(state_ref)

        lb = _lower_bound(lg_ref)
        chunks = [slice(c * CHUNK, (c + 1) * CHUNK) for c in range(cpb)]
        for rows in chunks:
            qz = qz_ref[rows, :]
            q_ref[rows, :] = qz * _sigmoid(qz)
            _, sn, f = _hgrn_gates(fz_ref[rows, :], lb)
            kk_ref[rows, :] = (1.0 - lb) * sn
            b_ref[rows, :] = jnp.log(f)
            v_ref[rows, :] = iv_ref[rows, :].astype(BF16)
        for rows in _groups(tm):
            b_ref[rows, :] = _tri_matmul(tri_ref, b_ref[rows, :])
        for c, rows in enumerate(chunks):
            b = b_ref[rows, :]
            bref = b[half - 1:half, :]
            blast = b[CHUNK - 1:CHUNK, :]
            q = q_ref[rows, :]
            kk = kk_ref[rows, :]
            qb_ref[rows, :] = (q * jnp.exp(b)).astype(BF16)
            qe_ref[rows, :] = (q * jnp.exp(b - bref)).astype(BF16)
            ke_ref[rows, :] = (kk * jnp.exp(bref - b)).astype(BF16)
            kl_ref[rows, :] = (kk * jnp.exp(blast - b)).astype(BF16)
            decay_ref[c:c + 1, :] = jnp.exp(blast)
        causal = _tri(True)
        for c, rows in enumerate(chunks):
            upd_ref[c] = _dot(v_ref[rows, :], kl_ref[rows, :], TN)
            a_ref[c] = jnp.where(causal, _dot(qe_ref[rows, :], ke_ref[rows, :], NT), 0.0).astype(BF16)
        state = state_ref[...]
        for c in range(cpb):
            st_ref[c] = state.astype(BF16)
            state = state * decay_ref[c:c + 1, :] + upd_ref[c]
        state_ref[...] = state
        for c, rows in enumerate(chunks):
            o_ref[rows, :] = _dot(a_ref[c], v_ref[rows, :], NN) + _dot(qb_ref[rows, :], st_ref[c], NT)
        for rows in chunks:
            o = o_ref[rows, :]
            r = lax.rsqrt(jnp.mean(o * o, axis=-1, keepdims=True) + RMS_EPS)
            gz = gz_ref[rows, :]
            og_ref[rows, :] = ((o * r * ng_ref[...]) * (gz * _sigmoid(gz))).astype(BF16)

    col = lambda base: pl.BlockSpec((tm, HEAD_DIM), lambda h, j: (j, base + h))
    tile_bf = pltpu.VMEM((tm, HEAD_DIM), BF16)
    tile_f32 = pltpu.VMEM((tm, HEAD_DIM), F32)
    return _call(
        body, name="hgrn_fwd", grid=(HEADS, nt),
        in_specs=[col(8), col(16), col(24), col(32),
                  pl.BlockSpec((2, HEAD_DIM), lambda h, j: (0, h)), pl.BlockSpec((1, HEAD_DIM), lambda h, j: (0, h)),
                  pl.BlockSpec((_tri_rows(tm), _tri_rows(tm)), lambda h, j: (0, 0))],
        out_specs=[col(0), col(0), pl.BlockSpec((None, cpb, HEAD_DIM, HEAD_DIM), lambda h, j: (h, j, 0, 0))],
        out_shape=[jax.ShapeDtypeStruct((t, HGRN_DIM), F32), jax.ShapeDtypeStruct((t, HGRN_DIM), BF16),
                   jax.ShapeDtypeStruct((HEADS, t // CHUNK, HEAD_DIM, HEAD_DIM), BF16)],
        scratch_shapes=[pltpu.VMEM((HEAD_DIM, HEAD_DIM), F32), tile_bf, tile_bf, tile_bf, tile_bf, tile_bf,
                        pltpu.VMEM((cpb, HEAD_DIM, HEAD_DIM), F32), pltpu.VMEM((max(cpb, 8), HEAD_DIM), F32),
                        pltpu.VMEM((cpb, CHUNK, CHUNK), BF16), tile_f32, tile_f32, tile_f32],
        semantics=("parallel", "arbitrary"),
        operands=(proj, proj, proj, proj, logits, norm_g, _block_tri(_tri_rows(tm))), hosted=hosted)


def _hgrn_bwd(d_og, o, proj, states, logits, norm_g, d_proj, hosted=None):
    t = proj.shape[0]
    tm = CHUNK * CHUNKS_PER_BLOCK if t % (CHUNK * CHUNKS_PER_BLOCK) == 0 else CHUNK
    cpb = tm // CHUNK
    nt = t // tm
    half = CHUNK // 2

    def body(dog_ref, o_ref, qz_ref, fz_ref, iv_ref, gz_ref, st_ref, lg_ref, ng_ref, tril_ref, triu_ref,
             _, dp_ref, sums_ref,
             dstate_ref, qe_ref, ke_ref, qb_ref, kl_ref, v_ref, do_ref, upd_ref, dst_ref, a_ref, da_ref,
             decay_ref, through_ref, q_ref, kk_ref, b_ref, dsilu_ref, gs_ref, gf_ref, sn_ref,
             eb_ref, ebr_ref, ekr_ref, ebl_ref, rev_ref, pre_ref, dk_ref):
        j = pl.program_id(1)

        @pl.when(j == 0)
        def _():
            dstate_ref[...] = jnp.zeros_like(dstate_ref)
            sums_ref[...] = jnp.zeros_like(sums_ref)

        lb = _lower_bound(lg_ref)
        ng = ng_ref[...]
        chunks = [slice(c * CHUNK, (c + 1) * CHUNK) for c in range(cpb)]
        for rows in chunks:
            qz = qz_ref[rows, :]
            sq = _sigmoid(qz)
            q_ref[rows, :] = qz * sq
            dsilu_ref[rows, :] = sq * (1.0 + qz * (1.0 - sq))
            s, sn, f = _hgrn_gates(fz_ref[rows, :], lb)
            kk_ref[rows, :] = (1.0 - lb) * sn
            b_ref[rows, :] = jnp.log(f)
            sn_ref[rows, :] = sn
            gf_ref[rows, :] = sn / f
            gs_ref[rows, :] = (1.0 - lb) * s
            v_ref[rows, :] = iv_ref[rows, :].astype(BF16)
            ov = o_ref[rows, :]
            r = lax.rsqrt(jnp.mean(ov * ov, axis=-1, keepdims=True) + RMS_EPS)
            on = ov * r
            gz = gz_ref[rows, :]
            sg = _sigmoid(gz)
            dog = dog_ref[rows, :]
            dp_ref[rows, 3 * HEAD_DIM:4 * HEAD_DIM] =(dog * (on * ng) * (sg * (1.0 + gz * (1.0 - sg)))).astype(BF16)
            d_ong = dog * (gz * sg)
            sums_ref[0:1, :] += _colsum(d_ong * on)
            d_on = d_ong * ng
            do_ref[rows, :] = (r * (d_on - on * jnp.mean(d_on * on, axis=-1, keepdims=True))).astype(BF16)
        for rows in _groups(tm):
            b_ref[rows, :] = _tri_matmul(tril_ref, b_ref[rows, :])
        for c, rows in enumerate(chunks):
            b = b_ref[rows, :]
            bref = b[half - 1:half, :]
            blast = b[CHUNK - 1:CHUNK, :]
            q = q_ref[rows, :]
            kk = kk_ref[rows, :]
            eb = jnp.exp(b)
            ebr = jnp.exp(b - bref)
            ekr = jnp.exp(bref - b)
            ebl = jnp.exp(blast - b)
            eb_ref[rows, :] = eb
            ebr_ref[rows, :] = ebr
            ekr_ref[rows, :] = ekr
            ebl_ref[rows, :] = ebl
            qb_ref[rows, :] = (q * eb).astype(BF16)
            qe_ref[rows, :] = (q * ebr).astype(BF16)
            ke_ref[rows, :] = (kk * ekr).astype(BF16)
            kl_ref[rows, :] = (kk * ebl).astype(BF16)
            decay_ref[c:c + 1, :] = jnp.exp(blast)
        causal = _tri(True)
        for c, rows in enumerate(chunks):
            upd_ref[c] = _dot(do_ref[rows, :], qb_ref[rows, :], TN)
            a_ref[c] = jnp.where(causal, _dot(qe_ref[rows, :], ke_ref[rows, :], NT), 0.0).astype(BF16)
            da_ref[c] = jnp.where(causal, _dot(do_ref[rows, :], v_ref[rows, :], NT), 0.0).astype(BF16)
        dstate = dstate_ref[...]
        for c in reversed(range(cpb)):
            dst_ref[c] = dstate.astype(BF16)
            decay = decay_ref[c:c + 1, :]
            through_ref[c:c + 1, :] = decay * _colsum(dstate * st_ref[c].astype(F32))
            dstate = dstate * decay + upd_ref[c]
        dstate_ref[...] = dstate
        for c, rows in enumerate(chunks):
            dp_ref[rows, 2 * HEAD_DIM:3 * HEAD_DIM] =(_dot(a_ref[c], do_ref[rows, :], TN)
                                + _dot(kl_ref[rows, :], dst_ref[c], NT)).astype(BF16)
        for c, rows in enumerate(chunks):
            dqe = _dot(da_ref[c], ke_ref[rows, :], NN)
            dq_inter = _dot(do_ref[rows, :], st_ref[c], NN) * eb_ref[rows, :]
            dp_ref[rows, 0:HEAD_DIM] =((dqe * ebr_ref[rows, :] + dq_inter) * dsilu_ref[rows, :]).astype(BF16)
            rev_ref[rows, :] = qe_ref[rows, :].astype(F32) * dqe + q_ref[rows, :] * dq_inter
        for c, rows in enumerate(chunks):
            dke = _dot(da_ref[c], qe_ref[rows, :], TN)
            dk_inter = _dot(v_ref[rows, :], dst_ref[c], NN) * ebl_ref[rows, :]
            dk_ref[rows, :] = dke * ekr_ref[rows, :] + dk_inter
            rev_ref[rows, :] -= ke_ref[rows, :].astype(F32) * dke
            pre_ref[rows, :] = kk_ref[rows, :] * dk_inter
        for rows in _groups(tm):
            pre = pre_ref[rows, :]
            rev_ref[rows, :] = _tri_matmul(triu_ref, rev_ref[rows, :]) + (_tri_matmul(tril_ref, pre) - pre)
        for c, rows in enumerate(chunks):
            dlf = rev_ref[rows, :] + through_ref[c:c + 1, :]
            common = gf_ref[rows, :] * dlf - sn_ref[rows, :] * dk_ref[rows, :]
            dp_ref[rows, HEAD_DIM:2 * HEAD_DIM] =(gs_ref[rows, :] * common).astype(BF16)
            sums_ref[1:2, :] += _colsum(common)

        @pl.when(j == nt - 1)
        def _():
            sums_ref[1:2, :] = sums_ref[1:2, :] * lb * (1.0 - lb)

    rev = lambda base: pl.BlockSpec((tm, HEAD_DIM), lambda h, j: (nt - 1 - j, base + h))
    vec = lambda n: pl.BlockSpec((n, HEAD_DIM), lambda h, j: (0, h))
    const = pl.BlockSpec((_tri_rows(tm), _tri_rows(tm)), lambda h, j: (0, 0))
    tile_bf = pltpu.VMEM((tm, HEAD_DIM), BF16)
    tile_f32 = pltpu.VMEM((tm, HEAD_DIM), F32)
    square = lambda dtype: pltpu.VMEM((cpb, HEAD_DIM, HEAD_DIM), dtype)
    rows8 = pltpu.VMEM((max(cpb, 8), HEAD_DIM), F32)
    operands = (d_og, o, proj, proj, proj, proj, states, logits, norm_g, _block_tri(_tri_rows(tm)),
                _block_tri(_tri_rows(tm), lower=False), d_proj)
    return _call(
        body, name="hgrn_bwd", grid=(HEADS, nt),
        in_specs=[rev(0), rev(0), rev(8), rev(16), rev(24), rev(32),
                  pl.BlockSpec((None, cpb, HEAD_DIM, HEAD_DIM), lambda h, j: (h, nt - 1 - j, 0, 0)),
                  vec(2), vec(1), const, const, ANY],
        out_specs=[pl.BlockSpec((tm, 4 * HEAD_DIM), lambda h, j: (nt - 1 - j, DP_HEAD_BLOCK + h)), vec(8)],
        out_shape=[jax.ShapeDtypeStruct(d_proj.shape, BF16), jax.ShapeDtypeStruct((8, HGRN_DIM), F32)],
        scratch_shapes=[pltpu.VMEM((HEAD_DIM, HEAD_DIM), F32)] + [tile_bf] * 6 + [square(F32), square(BF16)]
        + [pltpu.VMEM((cpb, CHUNK, CHUNK), BF16)] * 2 + [rows8, rows8] + [tile_f32] * 14,
        semantics=("parallel", "arbitrary"), operands=operands, hosted=hosted, aliases={len(operands) - 1: 0})


def _mix_ln1(proj, y_conv, y_hgrn, w_out, x, g, b, hosted=None):
    t = x.shape[0]
    tm = _pick(t, 512)
    d = D_MODEL

    def lhs(ins, outs):
        for rows in _row_blocks(tm):
            outs[0][rows, :] = (_sigmoid(ins[0][rows, :]) * ins[3][rows, :]
                                + _sigmoid(ins[2][rows, :]) * ins[4][rows, :]).astype(BF16)
        return outs[0][...]

    def epilogue(acc, ins, outs, scr):
        for rows in _row_blocks(tm):
            r = ALPHA * ins[5][rows, :] + acc[rows, :]
            outs[1][rows, :] = r
            xhat, _ = _ln(r)
            outs[2][rows, :] = (xhat * ins[6][...] + ins[7][...]).astype(BF16)

    tile = pl.BlockSpec((tm, d), lambda i, j, k: (i, 0))
    vec = pl.BlockSpec((1, d), lambda i, j, k: (0, 0))
    return _mm_fused(
        "mix_ln1", (t // tm, 1, 1), (proj, w_out, proj, y_conv, y_hgrn, x, g, b),
        [pl.BlockSpec((tm, d), lambda i, j, k: (i, 5)), pl.BlockSpec((d, d), lambda i, j, k: (0, 0)),
         pl.BlockSpec((tm, d), lambda i, j, k: (i, 6)), tile, tile, tile, vec, vec],
        [jax.ShapeDtypeStruct((t, d), BF16), jax.ShapeDtypeStruct((t, d), F32), jax.ShapeDtypeStruct((t, d), BF16)],
        [tile, tile, tile], NN, (tm, d), epilogue, lhs=lhs, hosted=hosted)


def _d_mixed_merge_bwd(d_r1b, w_out, proj, y_conv, y_hgrn, hosted=None):
    t = proj.shape[0]
    tm = _pick(t, 512)
    d = D_MODEL

    def epilogue(d_mixed, ins, outs, scr):
        dy_ref, dmz_ref = outs
        for rows in _row_blocks(tm):
            dm = d_mixed[rows, :]
            for br in range(2):
                sg = _sigmoid(ins[2 + br][rows, :])
                dy_ref[br, rows, :] = (sg * dm).astype(BF16)
                dmz_ref[rows, br * d:(br + 1) * d] = (dm * ins[4 + br][rows, :] * sg * (1.0 - sg)).astype(BF16)

    tile = pl.BlockSpec((tm, d), lambda i, j, k: (i, 0))
    return _mm_fused(
        "d_mixed_merge_bwd", (t // tm, 1, 1), (d_r1b, w_out, proj, proj, y_conv, y_hgrn),
        [tile, pl.BlockSpec((d, d), lambda i, j, k: (0, 0)), pl.BlockSpec((tm, d), lambda i, j, k: (i, 5)),
         pl.BlockSpec((tm, d), lambda i, j, k: (i, 6)), tile, tile],
        [jax.ShapeDtypeStruct((2, t, d), BF16), jax.ShapeDtypeStruct((t, IN_COLS), BF16)],
        [pl.BlockSpec((2, tm, d), lambda i, j, k: (0, i, 0)),
         pl.BlockSpec((tm, 2 * d), lambda i, j, k: (i, DP_MERGE_BLOCK))],
        NT, (tm, d), epilogue, hosted=hosted)


def _ffn_out_ln2(act, w_ffn_out, r1, target, g1, b1, g2, b2):
    t = r1.shape[0]
    tm = _pick(t, 1024)
    nt = t // tm
    d = D_MODEL

    def epilogue(y_ffn, ins, outs, scr):
        r1_ref, tg_ref, g1_ref, b1_ref, g2_ref, b2_ref = ins[2:]
        dr_ref, drb_ref, sums_ref = outs
        (sq_ref,) = scr
        i = pl.program_id(0)

        @pl.when(i == 0)
        def _():
            sums_ref[...] = jnp.zeros_like(sums_ref)
            sq_ref[...] = jnp.zeros_like(sq_ref)

        for rows in _row_blocks(tm):
            xh1, _ = _ln(r1_ref[rows, :])
            x1 = xh1 * g1_ref[...] + b1_ref[...]
            xh2, rstd2 = _ln(ALPHA * x1 + y_ffn[rows, :])
            diff = xh2 * g2_ref[...] + b2_ref[...] - tg_ref[rows, :]
            dy = diff * (1.0 / D_MODEL)
            dr = _ln_bwd(dy, xh2, rstd2, g2_ref[...])
            dr_ref[rows, :] = dr
            drb_ref[rows, :] = dr.astype(BF16)
            sums_ref[0:1, :] += _colsum(dy * xh2)
            sums_ref[1:2, :] += _colsum(dy)
            sq_ref[...] += _colsum(diff * diff)

        @pl.when(i == nt - 1)
        def _():
            total = jnp.sum(sq_ref[...], axis=-1, keepdims=True) * (0.5 / D_MODEL)
            sums_ref[2:3, :] = jnp.broadcast_to(total, (1, D_MODEL))

    tile = pl.BlockSpec((tm, d), lambda i, j, k: (i, 0))
    vec = pl.BlockSpec((1, d), lambda i, j, k: (0, 0))
    return _mm_fused(
        "ffn_out_ln2", (nt, 1, 4), (act, w_ffn_out, r1, target, g1, b1, g2, b2),
        [pl.BlockSpec((None, tm, FF_SHARD), lambda i, j, k: (k, i, 0)),
         pl.BlockSpec((None, FF_SHARD, d), lambda i, j, k: (k, 0, 0)), tile, tile, vec, vec, vec, vec],
        [jax.ShapeDtypeStruct((t, d), F32), jax.ShapeDtypeStruct((t, d), BF16), jax.ShapeDtypeStruct((8, d), F32)],
        [tile, tile, pl.BlockSpec((8, d), lambda i, j, k: (0, 0))], NN, (tm, d), epilogue,
        scratch=[pltpu.VMEM((1, d), F32)])


def _d_x1_ln1_bwd(d_z, w_ffn_in, d_r2, r1, g1):
    t = r1.shape[0]
    tm = _pick(t, 1024)
    d = D_MODEL

    def epilogue(dx_ffn, ins, outs, scr):
        dr2_ref, r1_ref, g_ref = ins[2:]
        dr1_ref, dr1b_ref, sums_ref = outs
        i = pl.program_id(0)

        @pl.when(i == 0)
        def _():
            sums_ref[...] = jnp.zeros_like(sums_ref)

        for rows in _row_blocks(tm):
            xhat, rstd = _ln(r1_ref[rows, :])
            dx1 = ALPHA * dr2_ref[rows, :] + dx_ffn[rows, :]
            dr1 = _ln_bwd(dx1, xhat, rstd, g_ref[...])
            dr1_ref[rows, :] = dr1
            dr1b_ref[rows, :] = dr1.astype(BF16)
            sums_ref[0:1, :] += _colsum(dx1 * xhat)
            sums_ref[1:2, :] += _colsum(dx1)

    tile = pl.BlockSpec((tm, d), lambda i, j, k: (i, 0))
    return _mm_fused(
        "d_x1_ln1_bwd", (t // tm, 1, N_DEV), (d_z, w_ffn_in, d_r2, r1, g1),
        [pl.BlockSpec((None, tm, FF_SHARD), lambda i, j, k: (k, i, 0)),
         pl.BlockSpec((None, d, FF_SHARD), lambda i, j, k: (k, 0, 0)), tile, tile,
         pl.BlockSpec((1, d), lambda i, j, k: (0, 0))],
        [jax.ShapeDtypeStruct((t, d), F32), jax.ShapeDtypeStruct((t, d), BF16), jax.ShapeDtypeStruct((8, d), F32)],
        [tile, tile, pl.BlockSpec((8, d), lambda i, j, k: (0, 0))], NT, (tm, d), epilogue)


def _cast_bf16(x):
    t = x.shape[0]
    tm = _pick(t, 512)

    def body(x_ref, o_ref):
        o_ref[...] = x_ref[...].astype(BF16)

    tile = pl.BlockSpec((tm, D_MODEL), lambda i: (i, 0))
    return pl.pallas_call(
        body, name="cast_x", grid=(t // tm,), in_specs=[tile], out_specs=tile,
        out_shape=jax.ShapeDtypeStruct((t, D_MODEL), BF16), compiler_params=_params(("parallel",)),
    )(x)


def _relayout(name, a, in_block, in_map, out_block, out_map, out_shape):
    def body(a_ref, o_ref):
        o_ref[...] = a_ref[...].astype(o_ref.dtype)

    return pl.pallas_call(
        body, name=name, grid=(N_DEV,), in_specs=[pl.BlockSpec(in_block, in_map)],
        out_specs=pl.BlockSpec(out_block, out_map), out_shape=out_shape, compiler_params=_params(("parallel",)),
    )(a)


_GELU_C = math.sqrt(2.0 / math.pi)


_GELU_CUBIC = 0.044715


def _gelu_parts(u):
    u2 = u * u
    th = jnp.tanh(u * (_GELU_C + (_GELU_C * _GELU_CUBIC) * u2))
    hu = 0.5 * u
    return th, hu + hu * th, u2, hu


BF16_ROWS = 16


def _ffn_act_fwd(z, w_dw, b_dw):
    t = z.shape[2]
    tm = _pick(t, 1024)
    nh = tm // FFN_HALO

    def body(z_ref, zh_ref, w_ref, b_ref, act_ref, gd_ref, us_ref):
        i = pl.program_id(1)
        us_ref[0:FFN_HALO, :] = jnp.where(i == 0, 0.0, zh_ref[...])
        us_ref[FFN_HALO:FFN_HALO + tm, :] = z_ref[0]
        for r in range(tm // ROW_BLOCK):
            base = r * ROW_BLOCK
            rows = slice(base, base + ROW_BLOCK)
            for lanes in _lane_blocks(FF_SHARD):
                uc = b_ref[:, lanes]
                for k in range(FFN_K):
                    off = base + FFN_HALO - (FFN_K - 1) + k
                    uc = uc + w_ref[k:k + 1, lanes] * us_ref[off:off + ROW_BLOCK, lanes]
                th, gelu, u2, hu = _gelu_parts(uc)
                dgelu = (0.5 + 0.5 * th) + (hu - hu * th * th) * (_GELU_C + (3.0 * _GELU_C * _GELU_CUBIC) * u2)
                act_ref[rows, lanes] = (gelu * z_ref[1, rows, lanes]).astype(BF16)
                gd_ref[0, rows, lanes] = gelu.astype(BF16)
                gd_ref[1, rows, lanes] = dgelu.astype(BF16)

    return pl.pallas_call(
        body, name="ffn_act_fwd", grid=(4, t // tm),
        in_specs=[pl.BlockSpec((2, None, tm, FF_SHARD), lambda j, i: (0, j, i, 0)),
                  pl.BlockSpec((None, None, FFN_HALO, FF_SHARD), lambda j, i: (0, j, jnp.maximum(i * nh - 1, 0), 0)),
                  pl.BlockSpec((None, FFN_K, FF_SHARD), lambda j, i: (j, 0, 0)),
                  pl.BlockSpec((None, 1, FF_SHARD), lambda j, i: (j, 0, 0))],
        out_specs=[pl.BlockSpec((None, tm, FF_SHARD), lambda j, i: (j, i, 0)),
                   pl.BlockSpec((2, None, tm, FF_SHARD), lambda j, i: (0, j, i, 0))],
        out_shape=[jax.ShapeDtypeStruct((4, t, FF_SHARD), BF16), jax.ShapeDtypeStruct((2, 4, t, FF_SHARD), BF16)],
        scratch_shapes=[pltpu.VMEM((FFN_HALO + tm, FF_SHARD), F32)],
        compiler_params=_params(("parallel", "arbitrary")),
    )(z, z, w_dw, b_dw)


def _ffn_act_bwd(d_act, z, gd, w_dw):
    t = z.shape[2]
    tm = _pick(t, 1024)
    nt = t // tm
    nh = tm // FFN_HALO
    last_h = t // FFN_HALO - 1
    pad = FFN_HALO - (FFN_K - 1)

    def fold(x):
        return functools.reduce(jnp.add, [x[r:r + SUBLANES, :] for r in range(0, x.shape[0], SUBLANES)])

    def body(da_ref, dah_ref, z_ref, zp_ref, gn_ref, gd_ref, gdn_ref, w_ref, dz_ref, sums_ref, us_ref, ds_ref,
             part_ref):
        i = pl.program_id(1)

        @pl.when(i == 0)
        def _():
            part_ref[...] = jnp.zeros_like(part_ref)

        us_ref[0:FFN_HALO, :] = jnp.where(i == 0, 0.0, zp_ref[...])
        us_ref[FFN_HALO:FFN_HALO + tm, :] = z_ref[0]
        for r in range(tm // ROW_BLOCK):
            base = r * ROW_BLOCK
            rows = slice(base, base + ROW_BLOCK)
            for lanes in _lane_blocks(FF_SHARD):
                da = da_ref[rows, lanes]
                dz_ref[1, rows, lanes] = (da * gd_ref[0, rows, lanes].astype(F32)).astype(BF16)
                duc = da * z_ref[1, rows, lanes] * gd_ref[1, rows, lanes].astype(F32)
                ds_ref[rows, lanes] = duc
                for k in range(FFN_K):
                    part_ref[k, :, lanes] += fold(duc * us_ref[base + pad + k:base + pad + k + ROW_BLOCK, lanes])
                part_ref[FFN_K, :, lanes] += fold(duc)
        duc_next = dah_ref[...] * gn_ref[...] * gdn_ref[0:FFN_HALO, :].astype(F32)
        ds_ref[tm:tm + FFN_HALO, :] = jnp.where(i == nt - 1, 0.0, duc_next)
        for r in range(tm // ROW_BLOCK):
            base = r * ROW_BLOCK
            for lanes in _lane_blocks(FF_SHARD):
                du = None
                for k in range(FFN_K):
                    off = base + FFN_K - 1 - k
                    term = w_ref[k:k + 1, lanes] * ds_ref[off:off + ROW_BLOCK, lanes]
                    du = term if du is None else du + term
                dz_ref[0, base:base + ROW_BLOCK, lanes] = du.astype(BF16)

        @pl.when(i == nt - 1)
        def _():
            sums_ref[...] = jnp.zeros_like(sums_ref)
            for k in range(FFN_K + 1):
                sums_ref[k:k + 1, :] = _colsum(part_ref[k])

    nxt = lambda i: jnp.minimum((i + 1) * nh, last_h)
    nxt_bf = lambda i: jnp.minimum((i + 1) * (tm // BF16_ROWS), t // BF16_ROWS - 1)
    return pl.pallas_call(
        body, name="ffn_act_bwd", grid=(4, nt),
        in_specs=[pl.BlockSpec((None, tm, FF_SHARD), lambda j, i: (j, i, 0)),
                  pl.BlockSpec((None, FFN_HALO, FF_SHARD), lambda j, i: (j, nxt(i), 0)),
                  pl.BlockSpec((2, None, tm, FF_SHARD), lambda j, i: (0, j, i, 0)),
                  pl.BlockSpec((None, None, FFN_HALO, FF_SHARD), lambda j, i: (0, j, jnp.maximum(i * nh - 1, 0), 0)),
                  pl.BlockSpec((None, None, FFN_HALO, FF_SHARD), lambda j, i: (1, j, nxt(i), 0)),
                  pl.BlockSpec((2, None, tm, FF_SHARD), lambda j, i: (0, j, i, 0)),
                  pl.BlockSpec((None, None, BF16_ROWS, FF_SHARD), lambda j, i: (1, j, nxt_bf(i), 0)),
                  pl.BlockSpec((None, FFN_K, FF_SHARD), lambda j, i: (j, 0, 0))],
        out_specs=[pl.BlockSpec((2, None, tm, FF_SHARD), lambda j, i: (0, j, i, 0)),
                   pl.BlockSpec((None, 8, FF_SHARD), lambda j, i: (j, 0, 0))],
        out_shape=[jax.ShapeDtypeStruct((2, 4, t, FF_SHARD), BF16), jax.ShapeDtypeStruct((4, 8, FF_SHARD), F32)],
        scratch_shapes=[pltpu.VMEM((FFN_HALO + tm, FF_SHARD), F32), pltpu.VMEM((tm + FFN_HALO, FF_SHARD), F32),
                        pltpu.VMEM((FFN_K + 1, SUBLANES, FF_SHARD), F32)],
        compiler_params=_params(("parallel", "arbitrary")),
    )(d_act, d_act, z, z, z, gd, gd, w_dw)


_HGRN_COLS = 4 * HGRN_DIM


def _to_backward_order(w):
    heads = w[:, 2 * CONV_DIM:2 * CONV_DIM + _HGRN_COLS].reshape(-1, 4, HEADS, HEAD_DIM)
    heads = jnp.swapaxes(heads, 1, 2).reshape(-1, _HGRN_COLS)
    return jnp.concatenate([w[:, 2 * CONV_DIM + _HGRN_COLS:], w[:, :2 * CONV_DIM], heads], axis=1)


def _from_backward_order(w):
    heads = w[:, 2 * D_MODEL + 2 * CONV_DIM:].reshape(-1, HEADS, 4, HEAD_DIM)
    heads = jnp.swapaxes(heads, 1, 2).reshape(-1, _HGRN_COLS)
    return jnp.concatenate([w[:, 2 * D_MODEL:2 * D_MODEL + 2 * CONV_DIM], heads, w[:, :2 * D_MODEL]], axis=1)
def _local_step(x, target, weights, small, scatter=None, order=None):
    t = x.shape[0]
    tm = _pick(t, 2048)
    tk = _pick(t, 2048)
    nm = t // tm
    nk = t // tk
    d = D_MODEL

    xb = _cast_bf16(x)
    ffn_in_gather = ffn_out_gather = None
    if isinstance(weights, tuple) and isinstance(weights[0], _Hosted):
        first_gather, ffn_in_gather, ffn_out_gather = weights
        proj, w_in_bwd, gathered = _proj_gather(xb, first_gather, order)
        w_in, w_conv_out8, w_hgrn_out8, w_out8, conv_dw8, ffn_dw8 = gathered
    else:
        w_in, w_conv_out8, w_hgrn_out8, w_out8, w_ffn_in, w_ffn_out8, conv_dw8, ffn_dw8 = weights
        proj = _mm("proj", xb, w_in, (t, IN_COLS), F32, (nm, N_DEV, 1),
                   pl.BlockSpec((tm, d), lambda i, j, k: (i, 0)),
                   pl.BlockSpec((None, d, IN_SHARD), lambda i, j, k: (j, 0, 0)),
                   pl.BlockSpec((tm, IN_SHARD), lambda i, j, k: (i, j)), NN, (tm, IN_SHARD))
        w_in_bwd = _to_backward_order(jnp.transpose(w_in, (1, 0, 2)).reshape(d, IN_COLS))
    o, og, states, *late = _hgrn_fwd(proj, small["hgrn_lb_logits"], small["hgrn_norm_g"], hosted=ffn_in_gather)
    if ffn_in_gather is not None:
        (w_ffn_in,) = late
    w_conv_out = _relayout("w_conv_out_natural", w_conv_out8, (None, CONV_DIM, 128), lambda j: (j, 0, 0),
                           (CONV_DIM, 128), lambda j: (0, j), jax.ShapeDtypeStruct((CONV_DIM, d), BF16))
    w_hgrn_out = w_hgrn_out8.reshape(d, d)
    w_out = w_out8.reshape(d, d)
    conv_dw =jnp.transpose(conv_dw8[:, :CONV_K, :CONV_DIM // N_DEV], (1, 0, 2)).reshape(CONV_K, CONV_DIM)
    ffn_dw = jnp.transpose(ffn_dw8[:, :FFN_K, :D_FF // N_DEV], (1, 0, 2)).reshape(FFN_K, 4, FF_SHARD)
    small = dict(small, w_conv_dw=conv_dw, w_ffn_dw=jnp.transpose(ffn_dw, (1, 0, 2)),
                 b_ffn_dw=small["b_ffn_dw"].reshape(4, 1, FF_SHARD))

    c_act, conv_pre = _conv_fwd(proj, small["w_conv_dw"], small["b_conv_dw"], small["conv_ln_g"], small["conv_ln_b"])
    y_conv = _mm("y_conv", c_act, w_conv_out, (t, d), F32, (nm, 1, 1),
                 pl.BlockSpec((tm, CONV_DIM), lambda i, j, k: (i, 0)),
                 pl.BlockSpec((CONV_DIM, d), lambda i, j, k: (0, 0)),
                 pl.BlockSpec((tm, d), lambda i, j, k: (i, 0)), NN, (tm, d))
    sq_w = pl.BlockSpec((d, d), lambda i, j, k: (0, 0))
    row_tile = pl.BlockSpec((tm, d), lambda i, j, k: (i, 0))
    y_hgrn = _mm("y_hgrn", og, w_hgrn_out, (t, d), F32, (nm, 1, 1), row_tile, sq_w, row_tile, NN, (tm, d))
    mixed, r1, x1b, *late = _mix_ln1(proj, y_conv, y_hgrn, w_out, x, small["ln1_g"], small["ln1_b"],
                                     hosted=ffn_out_gather)
    if ffn_out_gather is not None:
        (w_ffn_out8,) = late
    w_ffn_out = w_ffn_out8.reshape(4, FF_SHARD, d)
    z = _mm("ffn_in", x1b, w_ffn_in, (N_DEV, t, FF_SHARD), F32, (nm, N_DEV, 1), row_tile,
            pl.BlockSpec((None, d, FF_SHARD), lambda i, j, k: (j, 0, 0)),
            pl.BlockSpec((None, tm, FF_SHARD), lambda i, j, k: (j, i, 0)), NN, (tm, FF_SHARD))
    z = z.reshape(2, 4, t, FF_SHARD)
    act, gelu_and_slope = _ffn_act_fwd(z, small["w_ffn_dw"], small["b_ffn_dw"])

    d_r2, d_r2b, sums_ln2 = _ffn_out_ln2(act, w_ffn_out, r1, target, small["ln1_g"], small["ln1_b"],
                                         small["ln2_g"], small["ln2_b"])
    d_act = _mm("d_act", d_r2b, w_ffn_out, (4, t, FF_SHARD), F32, (nm, 4, 1), row_tile,
                pl.BlockSpec((None, FF_SHARD, d), lambda i, j, k: (j, 0, 0)),
                pl.BlockSpec((None, tm, FF_SHARD), lambda i, j, k: (j, i, 0)), NT, (tm, FF_SHARD))
    g_w_ffn_out = _mm("g_w_ffn_out", act, d_r2b, (4, FF_SHARD, d), BF16, (4, 1, nk),
                      pl.BlockSpec((None, tk, FF_SHARD), lambda i, j, k: (i, k, 0)),
                      pl.BlockSpec((tk, d), lambda i, j, k: (k, 0)),
                      pl.BlockSpec((None, FF_SHARD, d), lambda i, j, k: (i, 0, 0)), TN, (FF_SHARD, d))
    d_z, sums_ffn = _ffn_act_bwd(d_act, z, gelu_and_slope, small["w_ffn_dw"])
    d_z8 = d_z.reshape(N_DEV, t, FF_SHARD)
    d_r1, d_r1b, sums_ln1 = _d_x1_ln1_bwd(d_z8, w_ffn_in, d_r2, r1, small["ln1_g"])
    g_w_ffn_in = _mm("g_w_ffn_in", x1b, d_z8, (N_DEV, d, FF_SHARD), BF16, (N_DEV, 1, nk),
                     pl.BlockSpec((tk, d), lambda i, j, k: (k, 0)),
                     pl.BlockSpec((None, tk, FF_SHARD), lambda i, j, k: (i, k, 0)),
                     pl.BlockSpec((None, d, FF_SHARD), lambda i, j, k: (i, 0, 0)), TN, (d, FF_SHARD))
    k_tile = pl.BlockSpec((tk, d), lambda i, j, k: (k, 0))
    g_w_out = _mm("g_w_out", mixed, d_r1b, (d, d), BF16, (1, 1, nk), k_tile, k_tile, sq_w, TN, (d, d))
    send = (lambda grads: None) if scatter is None else scatter
    g_w_ffn_out = g_w_ffn_out.reshape(N_DEV, D_FF // N_DEV, d)
    d_y, d_proj, *recv_ffn_out = _d_mixed_merge_bwd(d_r1b, w_out, proj, y_conv, y_hgrn, hosted=send([g_w_ffn_out]))
    d_pre, sums_conv = _d_c_norm_bwd(d_y, w_conv_out, conv_pre, small["conv_ln_g"], small["conv_ln_b"])
    g_w_conv_out = _mm("g_w_conv_out", c_act, d_y, (CONV_DIM, d), BF16, (1, 1, nk),
                       pl.BlockSpec((tk, CONV_DIM), lambda i, j, k: (k, 0)),
                       pl.BlockSpec((None, tk, d), lambda i, j, k: (0, k, 0)),
                       pl.BlockSpec((CONV_DIM, d), lambda i, j, k: (0, 0)), TN, (CONV_DIM, d))
    g_w_conv_out = _relayout("g_w_conv_out_shards", g_w_conv_out, (CONV_DIM, 128), lambda j: (0, j),
                             (None, CONV_DIM, 128), lambda j: (j, 0, 0),
                             jax.ShapeDtypeStruct((N_DEV, CONV_DIM, 128), BF16))
    d_og = _mm("d_og", d_y, w_hgrn_out, (t, d), F32, (nm, 1, 1),
               pl.BlockSpec((None, tm, d), lambda i, j, k: (1, i, 0)), sq_w, row_tile, NT, (tm, d))
    g_w_hgrn_out = _mm("g_w_hgrn_out", og, d_y, (d, d), BF16, (1, 1, nk), k_tile,
                       pl.BlockSpec((None, tk, d), lambda i, j, k: (1, k, 0)), sq_w, TN, (d, d))
    d_proj, g_w_conv_dw, *recv_ffn_in = _conv_bwd_dw(d_pre, proj, small["w_conv_dw"], d_proj,
                                                     hosted=send([g_w_ffn_in]))
    early = [g_w_conv_out, g_w_hgrn_out.reshape(N_DEV, d // N_DEV, d), g_w_out.reshape(N_DEV, d // N_DEV, d)]
    d_proj, sums_hgrn, *early_recv = _hgrn_bwd(d_og, o, proj, states, small["hgrn_lb_logits"], small["hgrn_norm_g"],
                                               d_proj, hosted=send(early))
    early += [g_w_ffn_in, g_w_ffn_out]
    early_recv += recv_ffn_in + recv_ffn_out
    wide = IN_COLS // 4
    g_w_in = _mm("g_w_in", xb, d_proj, (d, IN_COLS), BF16, (4, 1, nk), k_tile,
                 pl.BlockSpec((tk, wide), lambda i, j, k: (k, i)),
                 pl.BlockSpec((d, wide), lambda i, j, k: (0, i)), TN, (d, wide))
    g_w_in = jnp.transpose(_from_backward_order(g_w_in).reshape(d, N_DEV, IN_SHARD), (1, 0, 2))
    def add_residual(acc, ins, outs, scr):
        for rows in _row_blocks(ta):
            outs[0][rows, :] = ALPHA * ins[2][rows, :] + acc[rows, :]

    ta = _pick(t, 1024)
    acc_tile = pl.BlockSpec((ta, d), lambda i, j, k: (i, 0))
    grad_x, *late_recv = _mm_fused(
        "grad_x", (t // ta, 1, 4), (d_proj, w_in_bwd, d_r1),
        [pl.BlockSpec((ta, wide), lambda i, j, k: (i, k)), pl.BlockSpec((d, wide), lambda i, j, k: (0, k)), acc_tile],
        [jax.ShapeDtypeStruct((t, d), F32)], [acc_tile], NT, (ta, d), add_residual,
        hosted=None if scatter is None else scatter([g_w_in]))

    large_grads = [g_w_in] + early
    if scatter is not None:
        large_grads = list(zip(large_grads, late_recv + early_recv))
    return grad_x, large_grads, (sums_ln2, sums_conv, sums_hgrn, sums_ln1, sums_ffn, g_w_conv_dw)


def _small_views(sums):
    sums_ln2, sums_conv, sums_hgrn, sums_ln1, sums_ffn, g_w_conv_dw = sums
    d_l0 = sums_hgrn[1:2]
    return {
        "loss": sums_ln2[2:3, 0:128],
        "b_conv_dw": sums_conv[2:3], "conv_ln_g": sums_conv[0:1], "conv_ln_b": sums_conv[1:2],
        "hgrn_lb_logits": jnp.concatenate([d_l0, -d_l0], axis=1),
        "hgrn_norm_g": sums_hgrn[0:1],
        "ln1_g": sums_ln1[0:1], "ln1_b": sums_ln1[1:2],
        "b_ffn_dw": sums_ffn[:, FFN_K, :].reshape(1, D_FF),
        "ln2_g": sums_ln2[0:1], "ln2_b": sums_ln2[1:2],
        "w_conv_dw": g_w_conv_dw[0:CONV_K].reshape(1, CONV_K * CONV_DIM),
        "w_ffn_dw": jnp.transpose(sums_ffn[:, 0:FFN_K, :], (1, 0, 2)).reshape(1, FFN_K * D_FF),
    }


def _coords():
    return lax.axis_index("x"), lax.axis_index("y"), lax.axis_index("c")


def _gather(shards, staged=False):
    n = len(shards)
    later = range(1 if staged else 0, n)

    def parts(ins, outs, sems):
        send_sems, recv_sems, local_sems = sems
        x, y, c = _coords()
        me = 4 * x + 2 * y + c
        sibling = (x, y, 1 - c)
        chips = [(1 - x, y), (x, 1 - y), (1 - x, 1 - y)]

        def copy(a, k, block, to, src=None):
            return pltpu.make_async_remote_copy(
                src_ref=outs[a].at[block] if src is None else src, dst_ref=outs[a].at[block],
                send_sem=send_sems.at[a, k], recv_sem=recv_sems.at[a, k], device_id=to, device_id_type=MESH)

        local = [pltpu.make_async_copy(ins[a], outs[a].at[me], local_sems.at[a]) for a in range(n)]
        first = []
        for a in range(n):
            first.append(copy(a, 0, me, sibling, src=ins[a]))
            for j, chip in enumerate(chips):
                first.append(copy(a, 1 + j, me, (*chip, c), src=ins[a]))
        return x, y, c, sibling, chips, copy, local, first

    def start(ins, outs, sems):
        *_, local, first = parts(ins, outs, sems)
        for cp in local + first:
            cp.start()

    def arrive(ins, outs, sems, s):
        x, y, c, sibling, chips, copy, _, _ = parts(ins, outs, sems)
        if s == 1:
            block = 4 * x + 2 * y + 1 - c
            copy(0, 0, block, sibling).wait_recv()
        elif s <= 4:
            px, py = chips[s - 2]
            block = 4 * px + 2 * py + c
            copy(0, s - 1, block, sibling).wait_recv()
            copy(0, s + 2, block, sibling).start()
        else:
            px, py = chips[s - 5]
            block = 4 * px + 2 * py + 1 - c
            copy(0, s - 1, block, sibling).wait_recv()
        return block

    def middle(ins, outs, sems):
        x, y, c, sibling, chips, copy, _, _ = parts(ins, outs, sems)
        for j, (px, py) in enumerate(chips):
            for a in later:
                copy(a, 1 + j, 4 * px + 2 * py + c, sibling).wait_recv()
                copy(a, 4 + j, 4 * px + 2 * py + c, sibling).start()

    def finish(ins, outs, sems):
        x, y, c, sibling, chips, copy, local, first = parts(ins, outs, sems)
        passed = [copy(a, 4 + j, 4 * px + 2 * py + c, sibling) for a in range(n) for j, (px, py) in enumerate(chips)]
        for a in later:
            copy(a, 0, 4 * x + 2 * y + 1 - c, sibling).wait_recv()
            for j, (px, py) in enumerate(chips):
                copy(a, 4 + j, 4 * px + 2 * py + 1 - c, sibling).wait_recv()
        for cp in first + passed:
            cp.wait_send()
        for cp in local:
            cp.wait()

    hosted = _Hosted(shards, [jax.ShapeDtypeStruct((N_DEV,) + s.shape, s.dtype) for s in shards],
                     [pltpu.SemaphoreType.DMA((n, 7)), pltpu.SemaphoreType.DMA((n, 7)), pltpu.SemaphoreType.DMA((n,))],
                     start, finish, middle)
    hosted.arrive = arrive
    return hosted


def _proj_gather(xb, gather, order):
    t, d = xb.shape
    tm = _pick(t, 2048)
    nm = t // tm
    n_in, n_out = len(gather.inputs), len(gather.out_shapes)

    blocks_per_shard = IN_SHARD // HEAD_DIM

    def body(order_ref, x_ref, *refs):
        ins, refs = refs[:n_in], refs[n_in:]
        o_ref, bwd_ref, outs, refs = refs[0], refs[1], refs[2:2 + n_out], refs[2 + n_out:]
        w_buf, w_sem, bwd_sem, sems = refs[0], refs[1], refs[2], refs[3:]
        s, i = pl.program_id(0), pl.program_id(1)

        def reorder_copies(step):
            copies = []
            for j in range(blocks_per_shard):
                n = blocks_per_shard * order_ref[step] + j
                head_part = n - 2 * CONV_DIM // HEAD_DIM
                p = jnp.where(n >= (2 * CONV_DIM + _HGRN_COLS) // HEAD_DIM, n - (2 * CONV_DIM + _HGRN_COLS) // HEAD_DIM,
                              jnp.where(n < 2 * CONV_DIM // HEAD_DIM, 2 * D_MODEL // HEAD_DIM + n,
                                        (2 * D_MODEL + 2 * CONV_DIM) // HEAD_DIM + 4 * (head_part % HEADS)
                                        + head_part // HEADS))
                copies.append(pltpu.make_async_copy(
                    w_buf.at[step % 2, :, j * HEAD_DIM:(j + 1) * HEAD_DIM],
                    bwd_ref.at[:, pl.ds(pl.multiple_of(p * HEAD_DIM, HEAD_DIM), HEAD_DIM)], bwd_sem.at[step % 2]))
            return copies

        def reorder_done(step):
            pltpu.make_async_copy(w_buf.at[step % 2], w_buf.at[step % 2], bwd_sem.at[step % 2]).wait()

        @pl.when((s == 0) & (i == 0))
        def _():
            gather.start(ins, outs, sems)

        def staging(step, src):
            return pltpu.make_async_copy(src, w_buf.at[step % 2], w_sem.at[step % 2])

        @pl.when((s == 0) & (i == 0))
        def _():
            staging(0, ins[0]).start()

        for step in range(1, N_DEV):
            @pl.when((s == step - 1) & (i == nm - 1))
            def _(step=step):
                if step >= 2:
                    reorder_done(step - 2)
                staging(step, outs[0].at[gather.arrive(ins, outs, sems, step)]).start()

        for step in range(N_DEV):
            @pl.when((s == step) & (i == 0))
            def _(step=step):
                staging(step, ins[0]).wait()
                for cp in reorder_copies(step):
                    cp.start()

        o_ref[...] = _dot(x_ref[...], w_buf[s % 2], NN)

        @pl.when((s == N_DEV - 2) & (i == 0))
        def _():
            gather.middle(ins, outs, sems)

        @pl.when((s == N_DEV - 1) & (i == nm - 1))
        def _():
            reorder_done(N_DEV - 2)
            reorder_done(N_DEV - 1)
            gather.finish(ins, outs, sems)

    outs = pl.pallas_call(
        body, name="proj_gather",
        grid_spec=pltpu.PrefetchScalarGridSpec(
            num_scalar_prefetch=1, grid=(N_DEV, nm),
            in_specs=[pl.BlockSpec((tm, d), lambda s, i, order_ref: (i, 0))] + [ANY] * n_in,
            out_specs=[pl.BlockSpec((tm, IN_SHARD), lambda s, i, order_ref: (i, order_ref[s])), ANY] + [ANY] * n_out,
            scratch_shapes=[pltpu.VMEM((2, d, IN_SHARD), BF16), pltpu.SemaphoreType.DMA((2,)),
                            pltpu.SemaphoreType.DMA((2,))] + gather.sem_shapes),
        out_shape=[jax.ShapeDtypeStruct((t, IN_COLS), F32), jax.ShapeDtypeStruct((d, IN_COLS), BF16)]
        + gather.out_shapes,
        compiler_params=pltpu.CompilerParams(dimension_semantics=("arbitrary", "arbitrary"),
                                             vmem_limit_bytes=VMEM_LIMIT, has_side_effects=True),
    )(order, xb, *gather.inputs)
    return outs[0], outs[1], list(outs[2:])


def _scatter(grads):
    n = len(grads)

    def copies(ins, outs, sems):
        send_sems, recv_sems = sems
        x, y, c = _coords()
        out = []
        for a in range(n):
            for k in range(1, N_DEV):
                px, py, pc = x ^ (k >> 2), y ^ ((k >> 1) & 1), c ^ (k & 1)
                out.append(pltpu.make_async_remote_copy(
                    src_ref=ins[a].at[4 * px + 2 * py + pc], dst_ref=outs[a].at[k - 1],
                    send_sem=send_sems.at[a, k - 1], recv_sem=recv_sems.at[a, k - 1],
                    device_id=(px, py, pc), device_id_type=MESH))
        return out

    def start(ins, outs, sems):
        for cp in copies(ins, outs, sems):
            cp.start()

    def finish(ins, outs, sems):
        for cp in copies(ins, outs, sems):
            cp.wait()

    return _Hosted(grads, [jax.ShapeDtypeStruct((N_DEV - 1,) + g.shape[1:], g.dtype) for g in grads],
                   [pltpu.SemaphoreType.DMA((n, N_DEV - 1)), pltpu.SemaphoreType.DMA((n, N_DEV - 1))], start, finish)


def _row_tile(rows):
    return 256 if rows % 256 == 0 else rows


def _adam_math(w, g, m, v):
    m_new = ADAM_B1 * m + (1.0 - ADAM_B1) * g
    v_new = ADAM_B2 * v + (1.0 - ADAM_B2) * (g * g)
    m_hat = m_new / (1.0 - ADAM_B1 ** ADAM_STEP)
    v_hat = v_new / (1.0 - ADAM_B2 ** ADAM_STEP)
    delta = -ADAM_LR * (m_hat / (jnp.sqrt(v_hat) + ADAM_EPS) + ADAM_WD * w)
    return delta, m_new, v_new


def _adam_large(name, own, recv, me, w, m, v):
    rows, cols = w.shape
    tr = _row_tile(rows)

    def body(me_ref, p_ref, r_ref, w_ref, m_ref, v_ref, g_out, d_out, m_out, v_out):
        g = p_ref[...].astype(F32)
        for k in range(N_DEV - 1):
            g = g + r_ref[k].astype(F32)
        delta, m_new, v_new = _adam_math(w_ref[...], g, m_ref[...], v_ref[...])
        g_out[...] = g
        d_out[...] = delta
        m_out[...] = m_new
        v_out[...] = v_new

    tile = pl.BlockSpec((tr, cols), lambda r, me_ref: (r, 0))
    sds = jax.ShapeDtypeStruct((rows, cols), F32)
    return pl.pallas_call(
        body, name=name,
        grid_spec=pltpu.PrefetchScalarGridSpec(
            num_scalar_prefetch=1, grid=(rows // tr,),
            in_specs=[pl.BlockSpec((None, tr, cols), lambda r, me_ref: (me_ref[0], r, 0)),
                      pl.BlockSpec((N_DEV - 1, tr, cols), lambda r, me_ref: (0, r, 0)), tile, tile, tile],
            out_specs=[tile, tile, tile, tile]),
        out_shape=[sds, sds, sds, sds],
        compiler_params=_params(("parallel",)),
    )(me, own, recv, w, m, v)


def _small_allreduce(arrays):
    n = len(arrays)

    def body(*refs):
        ins, outs, gats = refs[:n], refs[n:2 * n], refs[2 * n:3 * n]
        send_sems, recv_sems = refs[3 * n:]
        x, y, c = _coords()
        me = 4 * x + 2 * y + c
        peers = [(x ^ (k >> 2), y ^ ((k >> 1) & 1), c ^ (k & 1)) for k in range(1, N_DEV)]

        def copy(a, k, slot):
            return pltpu.make_async_remote_copy(
                src_ref=ins[a], dst_ref=gats[a].at[slot], send_sem=send_sems.at[a, k], recv_sem=recv_sems.at[a, k],
                device_id=peers[k], device_id_type=MESH)

        sends = [copy(a, k, me) for a in range(n) for k in range(N_DEV - 1)]
        for a in range(n):
            gats[a][me] = ins[a][...]
        for cp in sends:
            cp.start()
        for a in range(n):
            for k, (px, py, pc) in enumerate(peers):
                copy(a, k, 4 * px + 2 * py + pc).wait_recv()
        for cp in sends:
            cp.wait_send()
        for a in range(n):
            acc = gats[a][0]
            for dev in range(1, N_DEV):
                acc = acc + gats[a][dev]
            outs[a][...] = acc

    whole = pl.BlockSpec(memory_space=pltpu.VMEM)
    return pl.pallas_call(
        body, name="small_allreduce", in_specs=[whole] * n, out_specs=[whole] * n,
        out_shape=[jax.ShapeDtypeStruct(a.shape, F32) for a in arrays],
        scratch_shapes=[pltpu.VMEM((N_DEV,) + a.shape, F32) for a in arrays]
        + [pltpu.SemaphoreType.DMA((n, N_DEV - 1)), pltpu.SemaphoreType.DMA((n, N_DEV - 1))],
        compiler_params=pltpu.CompilerParams(has_side_effects=True, vmem_limit_bytes=VMEM_LIMIT),
    )(*arrays)


def _adam_replicated(sums, w, m, v):
    rows_of = {"conv_ln_g": (1, 0), "conv_ln_b": (1, 1), "b_conv_dw": (1, 2), "hgrn_norm_g": (2, 0),
               "ln1_g": (3, 0), "ln1_b": (3, 1), "ln2_g": (0, 0), "ln2_b": (0, 1)}
    names = list(rows_of) + ["hgrn_lb_logits"]
    n = len(names)

    def body(*refs):
        sum_refs, refs = refs[:4], refs[4:]
        w_refs, m_refs, v_refs, outs = refs[:n], refs[n:2 * n], refs[2 * n:3 * n], refs[3 * n:]
        for j, name in enumerate(names):
            if name == "hgrn_lb_logits":
                d_l0 = sum_refs[2][1:2, :]
                grads = [d_l0, -d_l0]
            else:
                a, row = rows_of[name]
                grads = [sum_refs[a][row:row + 1, :]]
            g_out, d_out, m_out, v_out = outs[4 * j:4 * j + 4]
            for r, g in enumerate(grads):
                rows = slice(r, r + 1)
                delta, m_new, v_new = _adam_math(w_refs[j][rows, :], g, m_refs[j][rows, :], v_refs[j][rows, :])
                g_out[rows, :] = g
                d_out[rows, :] = delta
                m_out[rows, :] = m_new
                v_out[rows, :] = v_new

    whole = pl.BlockSpec(memory_space=pltpu.VMEM)
    operands = list(sums) + [w[k] for k in names] + [m[k] for k in names] + [v[k] for k in names]
    outs = pl.pallas_call(
        body, name="adam_replicated", in_specs=[whole] * len(operands), out_specs=[whole] * (4 * n),
        out_shape=[jax.ShapeDtypeStruct(w[k].shape, F32) for k in names for _ in range(4)],
    )(*operands)
    return {name: tuple(outs[4 * j:4 * j + 4]) for j, name in enumerate(names)}


def _adam_small(w, g, m, v):
    def body(w_ref, g_ref, m_ref, v_ref, d_out, m_out, v_out):
        delta, m_new, v_new = _adam_math(w_ref[...], g_ref[...], m_ref[...], v_ref[...])
        d_out[...] = delta
        m_out[...] = m_new
        v_out[...] = v_new

    whole = pl.BlockSpec(memory_space=pltpu.VMEM)
    sds = jax.ShapeDtypeStruct(w.shape, F32)
    return pl.pallas_call(body, name="adam_small", in_specs=[whole] * 4, out_specs=[whole] * 3,
                          out_shape=[sds, sds, sds])(w, g, m, v)


_WEIGHTS = ["w_in", "w_conv_dw", "b_conv_dw", "conv_ln_g", "conv_ln_b", "w_conv_out", "hgrn_lb_logits", "hgrn_norm_g",
            "w_hgrn_out", "w_out", "ln1_g", "ln1_b", "w_ffn_in", "w_ffn_dw", "b_ffn_dw", "w_ffn_out", "ln2_g", "ln2_b"]
_LARGE = ["w_in", "w_conv_out", "w_hgrn_out", "w_out", "w_ffn_in", "w_ffn_out"]
_CONV_DW_SHARD = CONV_DIM // N_DEV
_FFN_DW_SHARD = D_FF // N_DEV


def kernel(x, w_in, w_conv_dw, b_conv_dw, conv_ln_g, conv_ln_b, w_conv_out, hgrn_lb_logits, hgrn_norm_g, w_hgrn_out, w_out, ln1_g, ln1_b, w_ffn_in, w_ffn_dw, b_ffn_dw, w_ffn_out, ln2_g, ln2_b, loss_target, m_w_in, m_w_conv_dw, m_b_conv_dw, m_conv_ln_g, m_conv_ln_b, m_w_conv_out, m_hgrn_lb_logits, m_hgrn_norm_g, m_w_hgrn_out, m_w_out, m_ln1_g, m_ln1_b, m_w_ffn_in, m_w_ffn_dw, m_b_ffn_dw, m_w_ffn_out, m_ln2_g, m_ln2_b, v_w_in, v_w_conv_dw, v_b_conv_dw, v_conv_ln_g, v_conv_ln_b, v_w_conv_out, v_hgrn_lb_logits, v_hgrn_norm_g, v_w_hgrn_out, v_w_out, v_ln1_g, v_ln1_b, v_w_ffn_in, v_w_ffn_dw, v_b_ffn_dw, v_w_ffn_out, v_ln2_g, v_ln2_b):
    w = dict(w_in=w_in, w_conv_dw=w_conv_dw, b_conv_dw=b_conv_dw, conv_ln_g=conv_ln_g, conv_ln_b=conv_ln_b,
             w_conv_out=w_conv_out, hgrn_lb_logits=hgrn_lb_logits, hgrn_norm_g=hgrn_norm_g, w_hgrn_out=w_hgrn_out,
             w_out=w_out, ln1_g=ln1_g, ln1_b=ln1_b, w_ffn_in=w_ffn_in, w_ffn_dw=w_ffn_dw, b_ffn_dw=b_ffn_dw,
             w_ffn_out=w_ffn_out, ln2_g=ln2_g, ln2_b=ln2_b)
    m = dict(w_in=m_w_in, w_conv_dw=m_w_conv_dw, b_conv_dw=m_b_conv_dw, conv_ln_g=m_conv_ln_g, conv_ln_b=m_conv_ln_b,
             w_conv_out=m_w_conv_out, hgrn_lb_logits=m_hgrn_lb_logits, hgrn_norm_g=m_hgrn_norm_g,
             w_hgrn_out=m_w_hgrn_out, w_out=m_w_out, ln1_g=m_ln1_g, ln1_b=m_ln1_b, w_ffn_in=m_w_ffn_in,
             w_ffn_dw=m_w_ffn_dw, b_ffn_dw=m_b_ffn_dw, w_ffn_out=m_w_ffn_out, ln2_g=m_ln2_g, ln2_b=m_ln2_b)
    v = dict(w_in=v_w_in, w_conv_dw=v_w_conv_dw, b_conv_dw=v_b_conv_dw, conv_ln_g=v_conv_ln_g, conv_ln_b=v_conv_ln_b,
             w_conv_out=v_w_conv_out, hgrn_lb_logits=v_hgrn_lb_logits, hgrn_norm_g=v_hgrn_norm_g,
             w_hgrn_out=v_w_hgrn_out, w_out=v_w_out, ln1_g=v_ln1_g, ln1_b=v_ln1_b, w_ffn_in=v_w_ffn_in,
             w_ffn_dw=v_w_ffn_dw, b_ffn_dw=v_b_ffn_dw, w_ffn_out=v_w_ffn_out, ln2_g=v_ln2_g, ln2_b=v_ln2_b)
    xi, yi, ci = lax.axis_index("x"), lax.axis_index("y"), lax.axis_index("c")
    me = 4 * xi + 2 * yi + ci
    me_op = jnp.reshape(me, (1,)).astype(jnp.int32)

    shards = [w[name][0].astype(BF16) for name in _LARGE]
    shards.append(jnp.pad(w_conv_dw[0], ((0, 1), (0, 128 - _CONV_DW_SHARD))))
    shards.append(jnp.pad(w_ffn_dw[0], ((0, 8 - FFN_K), (0, 384 - _FFN_DW_SHARD))))
    chips = [(1 - xi, yi), (xi, 1 - yi), (1 - xi, 1 - yi)]
    order = jnp.stack([me, me ^ 1] + [4 * px + 2 * py + ci for px, py in chips]
                      + [4 * px + 2 * py + 1 - ci for px, py in chips]).astype(jnp.int32)
    small = dict(b_conv_dw=b_conv_dw, conv_ln_g=conv_ln_g, conv_ln_b=conv_ln_b, hgrn_lb_logits=hgrn_lb_logits,
                 hgrn_norm_g=hgrn_norm_g, ln1_g=ln1_g, ln1_b=ln1_b, ln2_g=ln2_g, ln2_b=ln2_b, b_ffn_dw=b_ffn_dw)

    gathers = (_gather(shards[:4] + shards[6:], staged=True), _gather(shards[4:5]), _gather(shards[5:6]))
    grad_x, large_grads, small_sums = _local_step(x[0], loss_target[0], gathers, small, _scatter, order)

    out = {}
    for name, (own, recv) in zip(_LARGE, large_grads):
        out[name] = _adam_large("adam_" + name, own, recv, me_op, w[name][0], m[name][0], v[name][0])

    totals = _small_allreduce(list(small_sums))
    out.update(_adam_replicated(totals[:4], w, m, v))
    summed = _small_views(totals)
    loss = summed["loss"][0, 0]
    conv_dw_g = lax.dynamic_slice_in_dim(summed["w_conv_dw"].reshape(CONV_K, CONV_DIM), me * _CONV_DW_SHARD, _CONV_DW_SHARD, axis=1)
    ffn_dw_g = lax.dynamic_slice_in_dim(summed["w_ffn_dw"].reshape(FFN_K, D_FF), me * _FFN_DW_SHARD, _FFN_DW_SHARD, axis=1)
    small_g = dict(b_ffn_dw=summed["b_ffn_dw"], w_conv_dw=conv_dw_g.reshape(1, -1), w_ffn_dw=ffn_dw_g.reshape(1, -1))
    names = list(small_g)
    flat = lambda d, n: d[n].reshape(1, -1)
    n_small = sum(small_g[n].shape[1] for n in names)
    pad = (-n_small) % 1024
    pack = lambda pieces: jnp.pad(jnp.concatenate(pieces, axis=1), ((0, 0), (0, pad))).reshape(-1, 128)
    d_s, m_s, v_s = _adam_small(pack([flat(w, n) for n in names]), pack([small_g[n] for n in names]),
                                pack([flat(m, n) for n in names]), pack([flat(v, n) for n in names]))
    pos = 0
    for n in names:
        size = small_g[n].shape[1]
        cut = lambda a: a.reshape(1, -1)[:, pos:pos + size].reshape(w[n].shape)
        out[n] = (small_g[n].reshape(w[n].shape), cut(d_s), cut(m_s), cut(v_s))
        pos += size

    for name in _LARGE:
        out[name] = tuple(a.reshape(w[name].shape) for a in out[name])
    grads = [out[n][0] for n in _WEIGHTS]
    deltas = [out[n][1] for n in _WEIGHTS]
    new_m = [out[n][2] for n in _WEIGHTS]
    new_v = [out[n][3] for n in _WEIGHTS]
    return (loss, grad_x[None], *grads, *deltas, *new_m, *new_v)
```

```python
import functools
import math

import jax
import jax.numpy as jnp
from jax import lax
from jax.experimental import pallas as pl
from jax.experimental.pallas import tpu as pltpu

F32 = jnp.float32
BF16 = jnp.bfloat16

N_DEV = 8
D_MODEL = 1024
CONV_DIM = 512
CONV_K = 31
HGRN_DIM = 1024
HEADS = 8
HEAD_DIM = 128
D_FF = 2816
FFN_K = 3
FF_SHARD = 2 * D_FF // N_DEV
IN_COLS = 7168
IN_SHARD = IN_COLS // N_DEV
LN_EPS = 1e-5
RMS_EPS = 1e-6
ALPHA = 2.0 ** 0.25

ADAM_LR = 0.001
ADAM_B1 = 0.9
ADAM_B2 = 0.999
ADAM_EPS = 1e-08
ADAM_WD = 0.01
ADAM_STEP = 10

CHUNK = 64
CHUNKS_PER_BLOCK = 32
CONV_HALO = 32
FFN_HALO = 8
ROW_BLOCK = 64
SUBLANES = 8
VMEM_LIMIT = 48 * 1024 * 1024
MXU_DEPTH = 256

DP_MERGE_BLOCK = 0
DP_CONV_BLOCK = 2
DP_HEAD_BLOCK = 6

MESH = pl.DeviceIdType.MESH
ANY = pl.BlockSpec(memory_space=pl.ANY)

NN = (((1,), (0,)), ((), ()))
NT = (((1,), (1,)), ((), ()))
TN = (((0,), (0,)), ((), ()))


def _params(sem):
    return pltpu.CompilerParams(dimension_semantics=sem, vmem_limit_bytes=VMEM_LIMIT)


def _dot(a, b, dims):
    return lax.dot_general(a.astype(BF16), b.astype(BF16), dims, preferred_element_type=F32)


def _sigmoid(x):
    return jax.nn.sigmoid(x)


def _ln(r):
    mu = jnp.mean(r, axis=-1, keepdims=True)
    xc = r - mu
    var = jnp.mean(xc * xc, axis=-1, keepdims=True)
    rstd = lax.rsqrt(var + LN_EPS)
    return xc * rstd, rstd


def _ln_bwd(dy, xhat, rstd, g):
    dxh = dy * g
    m1 = jnp.mean(dxh, axis=-1, keepdims=True)
    m2 = jnp.mean(dxh * xhat, axis=-1, keepdims=True)
    return rstd * (dxh - m1 - xhat * m2)


def _colsum(x):
    return jnp.sum(x, axis=0, keepdims=True)


class _Hosted:
    def __init__(self, inputs, out_shapes, sem_shapes, start, finish, middle=None):
        self.inputs, self.out_shapes, self.sem_shapes = list(inputs), list(out_shapes), list(sem_shapes)
        self.start, self.finish, self.middle = start, finish, middle


def _call(body, *, name, grid, in_specs, out_specs, out_shape, scratch_shapes, semantics, operands, hosted=None,
          aliases=None):
    aliases = aliases or {}
    if hosted is None:
        return pl.pallas_call(
            body, name=name, grid=grid, in_specs=list(in_specs), out_specs=list(out_specs), out_shape=list(out_shape),
            scratch_shapes=list(scratch_shapes), input_output_aliases=aliases,
            compiler_params=_params(semantics))(*operands)
    n_in, n_out, n_scr = len(in_specs), len(out_specs), len(scratch_shapes)
    h_in, h_out = len(hosted.inputs), len(hosted.out_shapes)

    def full_body(*refs):
        ins, refs = refs[:n_in], refs[n_in:]
        h_ins, refs = refs[:h_in], refs[h_in:]
        outs, refs = refs[:n_out], refs[n_out:]
        h_outs, refs = refs[:h_out], refs[h_out:]
        scr, sems = refs[:n_scr], refs[n_scr:]
        first = functools.reduce(jnp.logical_and, [pl.program_id(d) == 0 for d in range(len(grid))])
        last = functools.reduce(jnp.logical_and, [pl.program_id(d) == grid[d] - 1 for d in range(len(grid))])

        @pl.when(first)
        def _():
            hosted.start(h_ins, h_outs, sems)

        body(*ins, *outs, *scr)

        if hosted.middle is not None:
            step, total = 0, 1
            for d in range(len(grid)):
                step, total = step * grid[d] + pl.program_id(d), total * grid[d]

            @pl.when(step == (2 * total) // 3)
            def _():
                hosted.middle(h_ins, h_outs, sems)

        @pl.when(last)
        def _():
            hosted.finish(h_ins, h_outs, sems)

    return pl.pallas_call(
        full_body, name=name, grid=grid, in_specs=list(in_specs) + [ANY] * h_in,
        out_specs=list(out_specs) + [ANY] * h_out, out_shape=list(out_shape) + hosted.out_shapes,
        scratch_shapes=list(scratch_shapes) + hosted.sem_shapes, input_output_aliases=aliases,
        compiler_params=pltpu.CompilerParams(dimension_semantics=("arbitrary",) * len(grid),
                                             vmem_limit_bytes=VMEM_LIMIT, has_side_effects=True),
    )(*operands, *hosted.inputs)


def _mm(name, a, b, out_shape, out_dtype, grid, a_spec, b_spec, o_spec, dims, acc_shape, hosted=None):
    nk = grid[2]
    if nk == 1:
        def body(a_ref, b_ref, o_ref):
            o_ref[...] = _dot(a_ref[...], b_ref[...], dims).astype(o_ref.dtype)
        scratch = []
    else:
        def body(a_ref, b_ref, o_ref, acc_ref):
            k = pl.program_id(2)

            @pl.when(k == 0)
            def _():
                acc_ref[...] = jnp.zeros_like(acc_ref)

            acc_ref[...] += _dot(a_ref[...], b_ref[...], dims)

            @pl.when(k == nk - 1)
            def _():
                o_ref[...] = acc_ref[...].astype(o_ref.dtype)
        scratch = [pltpu.VMEM(acc_shape, F32)]

    outs = _call(body, name=name, grid=grid, in_specs=[a_spec, b_spec], out_specs=[o_spec],
                 out_shape=[jax.ShapeDtypeStruct(out_shape, out_dtype)], scratch_shapes=scratch,
                 semantics=("parallel", "parallel", "arbitrary"), operands=(a, b), hosted=hosted)
    return outs[0] if hosted is None else (outs[0], list(outs[1:]))


def _mm_fused(name, grid, operands, in_specs, out_shape, out_specs, dims, acc_shape, epilogue, lhs=None, scratch=(),
              hosted=None):
    nk = grid[2]
    n_in, n_out = len(in_specs), len(out_specs)

    def body(*refs):
        ins, outs, scr = refs[:n_in], refs[n_in:n_in + n_out], refs[n_in + n_out:]
        acc_ref, k = scr[0], pl.program_id(2)
        a = ins[0][...] if lhs is None else lhs(ins, outs)
        part = _dot(a, ins[1][...], dims)
        if nk == 1:
            acc_ref[...] = part
            epilogue(acc_ref, ins, outs, scr[1:])
            return

        @pl.when(k == 0)
        def _():
            acc_ref[...] = jnp.zeros_like(acc_ref)

        acc_ref[...] += part

        @pl.when(k == nk - 1)
        def _():
            epilogue(acc_ref, ins, outs, scr[1:])

    return _call(body, name=name, grid=grid, in_specs=in_specs, out_specs=out_specs, out_shape=out_shape,
                 scratch_shapes=[pltpu.VMEM(acc_shape, F32)] + list(scratch), semantics=("arbitrary",) * 3,
                 operands=operands, hosted=hosted)


def _row_blocks(rows, block=256):
    block = block if rows % block == 0 else rows
    return [slice(r, r + block) for r in range(0, rows, block)]


def _pick(t, pref):
    return pref if t % pref == 0 else t


def _glu(p):
    return p[:, :CONV_DIM] * _sigmoid(p[:, CONV_DIM:])


def _by_phase(taps):
    phases = {}
    for off, payload in taps:
        phases.setdefault(off % SUBLANES, []).append((off - off % SUBLANES, payload))
    return sorted(phases.items())


def _tap_sum(src_ref, base, taps, rows, lanes):
    acc = None
    for phase, items in _by_phase(taps):
        n = rows if phase == 0 else rows + SUBLANES
        part = None
        for off, (w_ref, k) in items:
            term = w_ref[k:k + 1, lanes] * src_ref[base + off:base + off + n, lanes]
            part = term if part is None else part + term
        if phase:
            part = part[phase:phase + rows, :]
        acc = part if acc is None else acc + part
    return acc


def _tap_products(x, src_ref, base, taps, lanes):
    rows, cols = x.shape
    pad = jnp.zeros((SUBLANES, cols), x.dtype)
    padded = jnp.concatenate([pad, x, pad], axis=0)
    out = []
    for phase, items in _by_phase(taps):
        n = rows if phase == 0 else rows + SUBLANES
        shifted = x if phase == 0 else padded[SUBLANES - phase:SUBLANES - phase + n, :]
        for off, key in items:
            out.append((key, _colsum(shifted * src_ref[base + off:base + off + n, lanes])))
    return out


def _lane_blocks(cols, block=256):
    return [slice(c, min(c + block, cols)) for c in range(0, cols, block)]


def _conv_fwd(proj, w_dw, b_dw, g, b):
    t = proj.shape[0]
    tm = _pick(t, 512)
    nh = tm // CONV_HALO

    def body(p_ref, ph_ref, w_ref, bd_ref, g_ref, b_ref, act_ref, pre_ref, xs_ref):
        i = pl.program_id(0)
        halo = _glu(ph_ref[...])
        xs_ref[0:CONV_HALO, :] = jnp.where(i == 0, 0.0, halo)
        xs_ref[CONV_HALO:CONV_HALO + tm, :] = _glu(p_ref[...])
        taps = [(CONV_HALO - (CONV_K - 1) + k, (w_ref, k)) for k in range(CONV_K)]
        for r in range(tm // ROW_BLOCK):
            rows = slice(r * ROW_BLOCK, (r + 1) * ROW_BLOCK)
            for lanes in _lane_blocks(CONV_DIM):
                pre_ref[rows, lanes] = bd_ref[:, lanes] + _tap_sum(xs_ref, r * ROW_BLOCK, taps, ROW_BLOCK, lanes)
            acc = pre_ref[rows, :]
            xhat, _ = _ln(acc)
            yln = xhat * g_ref[...] + b_ref[...]
            act_ref[rows, :] = (yln * _sigmoid(yln)).astype(BF16)

    full = lambda s: pl.BlockSpec(s, lambda i: (0, 0))
    return pl.pallas_call(
        body, name="conv_fwd", grid=(t // tm,),
        in_specs=[pl.BlockSpec((tm, 2 * CONV_DIM), lambda i: (i, 0)),
                  pl.BlockSpec((CONV_HALO, 2 * CONV_DIM), lambda i: (jnp.maximum(i * nh - 1, 0), 0)),
                  full((CONV_K, CONV_DIM)), full((1, CONV_DIM)), full((1, CONV_DIM)), full((1, CONV_DIM))],
        out_specs=[pl.BlockSpec((tm, CONV_DIM), lambda i: (i, 0)), pl.BlockSpec((tm, CONV_DIM), lambda i: (i, 0))],
        out_shape=[jax.ShapeDtypeStruct((t, CONV_DIM), BF16), jax.ShapeDtypeStruct((t, CONV_DIM), F32)],
        scratch_shapes=[pltpu.VMEM((CONV_HALO + tm, CONV_DIM), F32)],
        compiler_params=_params(("arbitrary",)),
    )(proj, proj, w_dw, b_dw, g, b)


def _d_c_norm_bwd(d_y, w_conv_out, pre, g, b):
    t = pre.shape[0]
    tm = _pick(t, 512)
    d = D_MODEL

    def epilogue(d_c, ins, outs, scr):
        pre_ref, g_ref, b_ref = ins[2:]
        dpre_ref, sums_ref = outs
        i = pl.program_id(0)

        @pl.when(i == 0)
        def _():
            sums_ref[...] = jnp.zeros_like(sums_ref)

        for rows in _row_blocks(tm):
            xhat, rstd = _ln(pre_ref[rows, :])
            yln = xhat * g_ref[...] + b_ref[...]
            sg = _sigmoid(yln)
            dyln = d_c[rows, :] * (sg * (1.0 + yln * (1.0 - sg)))
            dpre = _ln_bwd(dyln, xhat, rstd, g_ref[...])
            dpre_ref[rows, :] = dpre
            sums_ref[0:1, :] += _colsum(dyln * xhat)
            sums_ref[1:2, :] += _colsum(dyln)
            sums_ref[2:3, :] += _colsum(dpre)

    full = lambda s: pl.BlockSpec(s, lambda i, j, k: (0, 0))
    tile = pl.BlockSpec((tm, CONV_DIM), lambda i, j, k: (i, 0))
    return _mm_fused(
        "d_c_norm_bwd", (t // tm, 1, 1), (d_y, w_conv_out, pre, g, b),
        [pl.BlockSpec((None, tm, d), lambda i, j, k: (0, i, 0)), full((CONV_DIM, d)), tile,
         full((1, CONV_DIM)), full((1, CONV_DIM))],
        [jax.ShapeDtypeStruct((t, CONV_DIM), F32), jax.ShapeDtypeStruct((8, CONV_DIM), F32)],
        [tile, full((8, CONV_DIM))], NT, (tm, CONV_DIM), epilogue)


def _conv_bwd_dw(d_pre, proj, w_dw, d_proj, hosted=None):
    t = d_pre.shape[0]
    tm = _pick(t, 512)
    nt = t // tm
    nh = tm // CONV_HALO
    last_h = t // CONV_HALO - 1

    def body(dp_ref, dph_ref, p_ref, ph_ref, w_ref, _, dproj_ref, dw_ref, xs_ref, ds_ref):
        i = pl.program_id(0)

        @pl.when(i == 0)
        def _():
            dw_ref[...] = jnp.zeros_like(dw_ref)

        halo = _glu(ph_ref[...])
        xs_ref[0:CONV_HALO, :] = jnp.where(i == 0, 0.0, halo)
        xs_ref[CONV_HALO:CONV_HALO + tm, :] = _glu(p_ref[...])
        ds_ref[0:tm, :] = dp_ref[...]
        ds_ref[tm:tm + CONV_HALO, :] = jnp.where(i == nt - 1, 0.0, dph_ref[...])
        back_taps = [(CONV_K - 1 - k, (w_ref, k)) for k in range(CONV_K)]
        grad_taps = [(CONV_HALO - (CONV_K - 1) + k, k) for k in range(CONV_K)]
        for r in range(tm // ROW_BLOCK):
            base = r * ROW_BLOCK
            rows = slice(base, base + ROW_BLOCK)
            for lanes in _lane_blocks(CONV_DIM):
                gate_lanes = slice(CONV_DIM + lanes.start, CONV_DIM + lanes.stop)
                acc = _tap_sum(ds_ref, base, back_taps, ROW_BLOCK, lanes)
                for k, total in _tap_products(ds_ref[rows, lanes], xs_ref, base, grad_taps, lanes):
                    dw_ref[k:k + 1, lanes] += total
                cval = p_ref[rows, lanes]
                sg = _sigmoid(p_ref[rows, gate_lanes])
                dproj_ref[rows, lanes] = (acc * sg).astype(BF16)
                dproj_ref[rows, gate_lanes] = (acc * cval * sg * (1.0 - sg)).astype(BF16)

    full = lambda s: pl.BlockSpec(s, lambda i: (0, 0))
    return _call(
        body, name="conv_bwd_dw", grid=(nt,),
        in_specs=[pl.BlockSpec((tm, CONV_DIM), lambda i: (i, 0)),
                  pl.BlockSpec((CONV_HALO, CONV_DIM), lambda i: (jnp.minimum((i + 1) * nh, last_h), 0)),
                  pl.BlockSpec((tm, 2 * CONV_DIM), lambda i: (i, 0)),
                  pl.BlockSpec((CONV_HALO, 2 * CONV_DIM), lambda i: (jnp.maximum(i * nh - 1, 0), 0)),
                  full((CONV_K, CONV_DIM)), ANY],
        out_specs=[pl.BlockSpec((tm, 2 * CONV_DIM), lambda i: (i, DP_CONV_BLOCK)), full((CONV_HALO, CONV_DIM))],
        out_shape=[jax.ShapeDtypeStruct(d_proj.shape, BF16), jax.ShapeDtypeStruct((CONV_HALO, CONV_DIM), F32)],
        scratch_shapes=[pltpu.VMEM((CONV_HALO + tm, CONV_DIM), F32), pltpu.VMEM((tm + CONV_HALO, CONV_DIM), F32)],
        semantics=("arbitrary",), operands=(d_pre, d_pre, proj, proj, w_dw, d_proj), hosted=hosted, aliases={5: 0})


def _lower_bound(logit_ref):
    l0 = logit_ref[0:1, :]
    l1 = logit_ref[1:2, :]
    m = jnp.maximum(l0, l1)
    e0 = jnp.exp(l0 - m)
    e1 = jnp.exp(l1 - m)
    return e0 / (e0 + e1)


def _tri(lower):
    r = lax.broadcasted_iota(jnp.int32, (CHUNK, CHUNK), 0)
    c = lax.broadcasted_iota(jnp.int32, (CHUNK, CHUNK), 1)
    return (c <= r) if lower else (c >= r)


def _hgrn_gates(fz, lb):
    s = _sigmoid(fz)
    sn = _sigmoid(-fz)
    f = lb + (1.0 - lb) * s
    return s, sn, f


def _block_tri(rows, lower=True):
    r = lax.broadcasted_iota(jnp.int32, (rows, rows), 0)
    c = lax.broadcasted_iota(jnp.int32, (rows, rows), 1)
    tri = (c <= r) if lower else (c >= r)
    return (tri & (r // CHUNK == c // CHUNK)).astype(BF16)


def _tri_rows(tm):
    return min(tm, MXU_DEPTH)


def _tri_matmul(tri_ref, x):
    hi = x.astype(BF16)
    lo = (x - hi.astype(F32)).astype(BF16)
    tri = tri_ref[...]
    return (lax.dot_general(tri, hi, NN, preferred_element_type=F32)
            + lax.dot_general(tri, lo, NN, preferred_element_type=F32))


def _groups(tm):
    g = _tri_rows(tm)
    return [slice(i * g, (i + 1) * g) for i in range(tm // g)]


def _hgrn_fwd(proj, logits, norm_g, hosted=None):
    t = proj.shape[0]
    tm = CHUNK * CHUNKS_PER_BLOCK if t % (CHUNK * CHUNKS_PER_BLOCK) == 0 else CHUNK
    cpb = tm // CHUNK
    nt = t // tm
    half = CHUNK // 2

    def body(qz_ref, fz_ref, iv_ref, gz_ref, lg_ref, ng_ref, tri_ref, o_ref, og_ref, st_ref,
             state_ref, qe_ref, ke_ref, qb_ref, kl_ref, v_ref, upd_ref, decay_ref, a_ref, q_ref, kk_ref, b_ref):
        j = pl.program_id(1)

        @pl.when(j == 0)
        def _():
            state_ref[...] = jnp.zeros_like(state_ref)

        lb = _lower_bound(lg_ref)
        chunks = [slice(c * CHUNK, (c + 1) * CHUNK) for c in range(cpb)]
        for rows in chunks:
            qz = qz_ref[rows, :]
            q_ref[rows, :] = qz * _sigmoid(qz)
            _, sn, f = _hgrn_gates(fz_ref[rows, :], lb)
            kk_ref[rows, :] = (1.0 - lb) * sn
            b_ref[rows, :] = jnp.log(f)
            v_ref[rows, :] = iv_ref[rows, :].astype(BF16)
        for rows in _groups(tm):
            b_ref[rows, :] = _tri_matmul(tri_ref, b_ref[rows, :])
        for c, rows in enumerate(chunks):
            b = b_ref[rows, :]
            bref = b[half - 1:half, :]
            blast = b[CHUNK - 1:CHUNK, :]
            q = q_ref[rows, :]
            kk = kk_ref[rows, :]
            qb_ref[rows, :] = (q * jnp.exp(b)).astype(BF16)
            qe_ref[rows, :] = (q * jnp.exp(b - bref)).astype(BF16)
            ke_ref[rows, :] = (kk * jnp.exp(bref - b)).astype(BF16)
            kl_ref[rows, :] = (kk * jnp.exp(blast - b)).astype(BF16)
            decay_ref[c:c + 1, :] = jnp.exp(blast)
        causal = _tri(True)
        for c, rows in enumerate(chunks):
            upd_ref[c] = _dot(v_ref[rows, :], kl_ref[rows, :], TN)
            a_ref[c] = jnp.where(causal, _dot(qe_ref[rows, :], ke_ref[rows, :], NT), 0.0).astype(BF16)
        state = state_ref[...]
        for c in range(cpb):
            st_ref[c] = state.astype(BF16)
            state = state * decay_ref[c:c + 1, :] + upd_ref[c]
        state_ref[...] = state
        for c, rows in enumerate(chunks):
            o_ref[rows, :] = _dot(a_ref[c], v_ref[rows, :], NN) + _dot(qb_ref[rows, :], st_ref[c], NT)
        for rows in chunks:
            o = o_ref[rows, :]
            r = lax.rsqrt(jnp.mean(o * o, axis=-1, keepdims=True) + RMS_EPS)
            gz = gz_ref[rows, :]
            og_ref[rows, :] = ((o * r * ng_ref[...]) * (gz * _sigmoid(gz))).astype(BF16)

    col = lambda base: pl.BlockSpec((tm, HEAD_DIM), lambda h, j: (j, base + h))
    tile_bf = pltpu.VMEM((tm, HEAD_DIM), BF16)
    tile_f32 = pltpu.VMEM((tm, HEAD_DIM), F32)
    return _call(
        body, name="hgrn_fwd", grid=(HEADS, nt),
        in_specs=[col(8), col(16), col(24), col(32),
                  pl.BlockSpec((2, HEAD_DIM), lambda h, j: (0, h)), pl.BlockSpec((1, HEAD_DIM), lambda h, j: (0, h)),
                  pl.BlockSpec((_tri_rows(tm), _tri_rows(tm)), lambda h, j: (0, 0))],
        out_specs=[col(0), col(0), pl.BlockSpec((None, cpb, HEAD_DIM, HEAD_DIM), lambda h, j: (h, j, 0, 0))],
        out_shape=[jax.ShapeDtypeStruct((t, HGRN_DIM), F32), jax.ShapeDtypeStruct((t, HGRN_DIM), BF16),
                   jax.ShapeDtypeStruct((HEADS, t // CHUNK, HEAD_DIM, HEAD_DIM), BF16)],
        scratch_shapes=[pltpu.VMEM((HEAD_DIM, HEAD_DIM), F32), tile_bf, tile_bf, tile_bf, tile_bf, tile_bf,
                        pltpu.VMEM((cpb, HEAD_DIM, HEAD_DIM), F32), pltpu.VMEM((max(cpb, 8), HEAD_DIM), F32),
                        pltpu.VMEM((cpb, CHUNK, CHUNK), BF16), tile_f32, tile_f32, tile_f32],
        semantics=("parallel", "arbitrary"),
        operands=(proj, proj, proj, proj, logits, norm_g, _block_tri(_tri_rows(tm))), hosted=hosted)


def _hgrn_bwd(d_og, o, proj, states, logits, norm_g, d_proj, hosted=None):
    t = proj.shape[0]
    tm = CHUNK * CHUNKS_PER_BLOCK if t % (CHUNK * CHUNKS_PER_BLOCK) == 0 else CHUNK
    cpb = tm // CHUNK
    nt = t // tm
    half = CHUNK // 2

    def body(dog_ref, o_ref, qz_ref, fz_ref, iv_ref, gz_ref, st_ref, lg_ref, ng_ref, tril_ref, triu_ref,
             _, dp_ref, sums_ref,
             dstate_ref, qe_ref, ke_ref, qb_ref, kl_ref, v_ref, do_ref, upd_ref, dst_ref, a_ref, da_ref,
             decay_ref, through_ref, q_ref, kk_ref, b_ref, dsilu_ref, gs_ref, gf_ref, sn_ref,
             eb_ref, ebr_ref, ekr_ref, ebl_ref, rev_ref, pre_ref, dk_ref):
        j = pl.program_id(1)

        @pl.when(j == 0)
        def _():
            dstate_ref[...] = jnp.zeros_like(dstate_ref)
            sums_ref[...] = jnp.zeros_like(sums_ref)

        lb = _lower_bound(lg_ref)
        ng = ng_ref[...]
        chunks = [slice(c * CHUNK, (c + 1) * CHUNK) for c in range(cpb)]
        for rows in chunks:
            qz = qz_ref[rows, :]
            sq = _sigmoid(qz)
            q_ref[rows, :] = qz * sq
            dsilu_ref[rows, :] = sq * (1.0 + qz * (1.0 - sq))
            s, sn, f = _hgrn_gates(fz_ref[rows, :], lb)
            kk_ref[rows, :] = (1.0 - lb) * sn
            b_ref[rows, :] = jnp.log(f)
            sn_ref[rows, :] = sn
            gf_ref[rows, :] = sn / f
            gs_ref[rows, :] = (1.0 - lb) * s
            v_ref[rows, :] = iv_ref[rows, :].astype(BF16)
            ov = o_ref[rows, :]
            r = lax.rsqrt(jnp.mean(ov * ov, axis=-1, keepdims=True) + RMS_EPS)
            on = ov * r
            gz = gz_ref[rows, :]
            sg = _sigmoid(gz)
            dog = dog_ref[rows, :]
            dp_ref[rows, 3 * HEAD_DIM:4 * HEAD_DIM] =(dog * (on * ng) * (sg * (1.0 + gz * (1.0 - sg)))).astype(BF16)
            d_ong = dog * (gz * sg)
            sums_ref[0:1, :] += _colsum(d_ong * on)
            d_on = d_ong * ng
            do_ref[rows, :] = (r * (d_on - on * jnp.mean(d_on * on, axis=-1, keepdims=True))).astype(BF16)
        for rows in _groups(tm):
            b_ref[rows, :] = _tri_matmul(tril_ref, b_ref[rows, :])
        for c, rows in enumerate(chunks):
            b = b_ref[rows, :]
            bref = b[half - 1:half, :]
            blast = b[CHUNK - 1:CHUNK, :]
            q = q_ref[rows, :]
            kk = kk_ref[rows, :]
            eb = jnp.exp(b)
            ebr = jnp.exp(b - bref)
            ekr = jnp.exp(bref - b)
            ebl = jnp.exp(blast - b)
            eb_ref[rows, :] = eb
            ebr_ref[rows, :] = ebr
            ekr_ref[rows, :] = ekr
            ebl_ref[rows, :] = ebl
            qb_ref[rows, :] = (q * eb).astype(BF16)
            qe_ref[rows, :] = (q * ebr).astype(BF16)
            ke_ref[rows, :] = (kk * ekr).astype(BF16)
            kl_ref[rows, :] = (kk * ebl).astype(BF16)
            decay_ref[c:c + 1, :] = jnp.exp(blast)
        causal = _tri(True)
        for c, rows in enumerate(chunks):
            upd_ref[c] = _dot(do_ref[rows, :], qb_ref[rows, :], TN)
            a_ref[c] = jnp.where(causal, _dot(qe_ref[rows, :], ke_ref[rows, :], NT), 0.0).astype(BF16)
            da_ref[c] = jnp.where(causal, _dot(do_ref[rows, :], v_ref[rows, :], NT), 0.0).astype(BF16)
        dstate = dstate_ref[...]
        for c in reversed(range(cpb)):
            dst_ref[c] = dstate.astype(BF16)
            decay = decay_ref[c:c + 1, :]
            through_ref[c:c + 1, :] = decay * _colsum(dstate * st_ref[c].astype(F32))
            dstate = dstate * decay + upd_ref[c]
        dstate_ref[...] = dstate
        for c, rows in enumerate(chunks):
            dp_ref[rows, 2 * HEAD_DIM:3 * HEAD_DIM] =(_dot(a_ref[c], do_ref[rows, :], TN)
                                + _dot(kl_ref[rows, :], dst_ref[c], NT)).astype(BF16)
        for c, rows in enumerate(chunks):
            dqe = _dot(da_ref[c], ke_ref[rows, :], NN)
            dq_inter = _dot(do_ref[rows, :], st_ref[c], NN) * eb_ref[rows, :]
            dp_ref[rows, 0:HEAD_DIM] =((dqe * ebr_ref[rows, :] + dq_inter) * dsilu_ref[rows, :]).astype(BF16)
            rev_ref[rows, :] = qe_ref[rows, :].astype(F32) * dqe + q_ref[rows, :] * dq_inter
        for c, rows in enumerate(chunks):
            dke = _dot(da_ref[c], qe_ref[rows, :], TN)
            dk_inter = _dot(v_ref[rows, :], dst_ref[c], NN) * ebl_ref[rows, :]
            dk_ref[rows, :] = dke * ekr_ref[rows, :] + dk_inter
            rev_ref[rows, :] -= ke_ref[rows, :].astype(F32) * dke
            pre_ref[rows, :] = kk_ref[rows, :] * dk_inter
        for rows in _groups(tm):
            pre = pre_ref[rows, :]
            rev_ref[rows, :] = _tri_matmul(triu_ref, rev_ref[rows, :]) + (_tri_matmul(tril_ref, pre) - pre)
        for c, rows in enumerate(chunks):
            dlf = rev_ref[rows, :] + through_ref[c:c + 1, :]
            common = gf_ref[rows, :] * dlf - sn_ref[rows, :] * dk_ref[rows, :]
            dp_ref[rows, HEAD_DIM:2 * HEAD_DIM] =(gs_ref[rows, :] * common).astype(BF16)
            sums_ref[1:2, :] += _colsum(common)

        @pl.when(j == nt - 1)
        def _():
            sums_ref[1:2, :] = sums_ref[1:2, :] * lb * (1.0 - lb)

    rev = lambda base: pl.BlockSpec((tm, HEAD_DIM), lambda h, j: (nt - 1 - j, base + h))
    vec = lambda n: pl.BlockSpec((n, HEAD_DIM), lambda h, j: (0, h))
    const = pl.BlockSpec((_tri_rows(tm), _tri_rows(tm)), lambda h, j: (0, 0))
    tile_bf = pltpu.VMEM((tm, HEAD_DIM), BF16)
    tile_f32 = pltpu.VMEM((tm, HEAD_DIM), F32)
    square = lambda dtype: pltpu.VMEM((cpb, HEAD_DIM, HEAD_DIM), dtype)
    rows8 = pltpu.VMEM((max(cpb, 8), HEAD_DIM), F32)
    operands = (d_og, o, proj, proj, proj, proj, states, logits, norm_g, _block_tri(_tri_rows(tm)),
                _block_tri(_tri_rows(tm), lower=False), d_proj)
    return _call(
        body, name="hgrn_bwd", grid=(HEADS, nt),
        in_specs=[rev(0), rev(0), rev(8), rev(16), rev(24), rev(32),
                  pl.BlockSpec((None, cpb, HEAD_DIM, HEAD_DIM), lambda h, j: (h, nt - 1 - j, 0, 0)),
                  vec(2), vec(1), const, const, ANY],
        out_specs=[pl.BlockSpec((tm, 4 * HEAD_DIM), lambda h, j: (nt - 1 - j, DP_HEAD_BLOCK + h)), vec(8)],
        out_shape=[jax.ShapeDtypeStruct(d_proj.shape, BF16), jax.ShapeDtypeStruct((8, HGRN_DIM), F32)],
        scratch_shapes=[pltpu.VMEM((HEAD_DIM, HEAD_DIM), F32)] + [tile_bf] * 6 + [square(F32), square(BF16)]
        + [pltpu.VMEM((cpb, CHUNK, CHUNK), BF16)] * 2 + [rows8, rows8] + [tile_f32] * 14,
        semantics=("parallel", "arbitrary"), operands=operands, hosted=hosted, aliases={len(operands) - 1: 0})


def _mix_ln1(proj, y_conv, y_hgrn, w_out, x, g, b, hosted=None):
    t = x.shape[0]
    tm = _pick(t, 512)
    d = D_MODEL

    def lhs(ins, outs):
        for rows in _row_blocks(tm):
            outs[0][rows, :] = (_sigmoid(ins[0][rows, :]) * ins[3][rows, :]
                                + _sigmoid(ins[2][rows, :]) * ins[4][rows, :]).astype(BF16)
        return outs[0][...]

    def epilogue(acc, ins, outs, scr):
        for rows in _row_blocks(tm):
            r = ALPHA * ins[5][rows, :] + acc[rows, :]
            outs[1][rows, :] = r
            xhat, _ = _ln(r)
            outs[2][rows, :] = (xhat * ins[6][...] + ins[7][...]).astype(BF16)

    tile = pl.BlockSpec((tm, d), lambda i, j, k: (i, 0))
    vec = pl.BlockSpec((1, d), lambda i, j, k: (0, 0))
    return _mm_fused(
        "mix_ln1", (t // tm, 1, 1), (proj, w_out, proj, y_conv, y_hgrn, x, g, b),
        [pl.BlockSpec((tm, d), lambda i, j, k: (i, 5)), pl.BlockSpec((d, d), lambda i, j, k: (0, 0)),
         pl.BlockSpec((tm, d), lambda i, j, k: (i, 6)), tile, tile, tile, vec, vec],
        [jax.ShapeDtypeStruct((t, d), BF16), jax.ShapeDtypeStruct((t, d), F32), jax.ShapeDtypeStruct((t, d), BF16)],
        [tile, tile, tile], NN, (tm, d), epilogue, lhs=lhs, hosted=hosted)


def _d_mixed_merge_bwd(d_r1b, w_out, proj, y_conv, y_hgrn, hosted=None):
    t = proj.shape[0]
    tm = _pick(t, 512)
    d = D_MODEL

    def epilogue(d_mixed, ins, outs, scr):
        dy_ref, dmz_ref = outs
        for rows in _row_blocks(tm):
            dm = d_mixed[rows, :]
            for br in range(2):
                sg = _sigmoid(ins[2 + br][rows, :])
                dy_ref[br, rows, :] = (sg * dm).astype(BF16)
                dmz_ref[rows, br * d:(br + 1) * d] = (dm * ins[4 + br][rows, :] * sg * (1.0 - sg)).astype(BF16)

    tile = pl.BlockSpec((tm, d), lambda i, j, k: (i, 0))
    return _mm_fused(
        "d_mixed_merge_bwd", (t // tm, 1, 1), (d_r1b, w_out, proj, proj, y_conv, y_hgrn),
        [tile, pl.BlockSpec((d, d), lambda i, j, k: (0, 0)), pl.BlockSpec((tm, d), lambda i, j, k: (i, 5)),
         pl.BlockSpec((tm, d), lambda i, j, k: (i, 6)), tile, tile],
        [jax.ShapeDtypeStruct((2, t, d), BF16), jax.ShapeDtypeStruct((t, IN_COLS), BF16)],
        [pl.BlockSpec((2, tm, d), lambda i, j, k: (0, i, 0)),
         pl.BlockSpec((tm, 2 * d), lambda i, j, k: (i, DP_MERGE_BLOCK))],
        NT, (tm, d), epilogue, hosted=hosted)


def _ffn_out_ln2(act, w_ffn_out, r1, target, g1, b1, g2, b2):
    t = r1.shape[0]
    tm = _pick(t, 1024)
    nt = t // tm
    d = D_MODEL

    def epilogue(y_ffn, ins, outs, scr):
        r1_ref, tg_ref, g1_ref, b1_ref, g2_ref, b2_ref = ins[2:]
        dr_ref, drb_ref, sums_ref = outs
        (sq_ref,) = scr
        i = pl.program_id(0)

        @pl.when(i == 0)
        def _():
            sums_ref[...] = jnp.zeros_like(sums_ref)
            sq_ref[...] = jnp.zeros_like(sq_ref)

        for rows in _row_blocks(tm):
            xh1, _ = _ln(r1_ref[rows, :])
            x1 = xh1 * g1_ref[...] + b1_ref[...]
            xh2, rstd2 = _ln(ALPHA * x1 + y_ffn[rows, :])
            diff = xh2 * g2_ref[...] + b2_ref[...] - tg_ref[rows, :]
            dy = diff * (1.0 / D_MODEL)
            dr = _ln_bwd(dy, xh2, rstd2, g2_ref[...])
            dr_ref[rows, :] = dr
            drb_ref[rows, :] = dr.astype(BF16)
            sums_ref[0:1, :] += _colsum(dy * xh2)
            sums_ref[1:2, :] += _colsum(dy)
            sq_ref[...] += _colsum(diff * diff)

        @pl.when(i == nt - 1)
        def _():
            total = jnp.sum(sq_ref[...], axis=-1, keepdims=True) * (0.5 / D_MODEL)
            sums_ref[2:3, :] = jnp.broadcast_to(total, (1, D_MODEL))

    tile = pl.BlockSpec((tm, d), lambda i, j, k: (i, 0))
    vec = pl.BlockSpec((1, d), lambda i, j, k: (0, 0))
    return _mm_fused(
        "ffn_out_ln2", (nt, 1, 4), (act, w_ffn_out, r1, target, g1, b1, g2, b2),
        [pl.BlockSpec((None, tm, FF_SHARD), lambda i, j, k: (k, i, 0)),
         pl.BlockSpec((None, FF_SHARD, d), lambda i, j, k: (k, 0, 0)), tile, tile, vec, vec, vec, vec],
        [jax.ShapeDtypeStruct((t, d), F32), jax.ShapeDtypeStruct((t, d), BF16), jax.ShapeDtypeStruct((8, d), F32)],
        [tile, tile, pl.BlockSpec((8, d), lambda i, j, k: (0, 0))], NN, (tm, d), epilogue,
        scratch=[pltpu.VMEM((1, d), F32)])


def _d_x1_ln1_bwd(d_z, w_ffn_in, d_r2, r1, g1):
    t = r1.shape[0]
    tm = _pick(t, 1024)
    d = D_MODEL

    def epilogue(dx_ffn, ins, outs, scr):
        dr2_ref, r1_ref, g_ref = ins[2:]
        dr1_ref, dr1b_ref, sums_ref = outs
        i = pl.program_id(0)

        @pl.when(i == 0)
        def _():
            sums_ref[...] = jnp.zeros_like(sums_ref)

        for rows in _row_blocks(tm):
            xhat, rstd = _ln(r1_ref[rows, :])
            dx1 = ALPHA * dr2_ref[rows, :] + dx_ffn[rows, :]
            dr1 = _ln_bwd(dx1, xhat, rstd, g_ref[...])
            dr1_ref[rows, :] = dr1
            dr1b_ref[rows, :] = dr1.astype(BF16)
            sums_ref[0:1, :] += _colsum(dx1 * xhat)
            sums_ref[1:2, :] += _colsum(dx1)

    tile = pl.BlockSpec((tm, d), lambda i, j, k: (i, 0))
    return _mm_fused(
        "d_x1_ln1_bwd", (t // tm, 1, N_DEV), (d_z, w_ffn_in, d_r2, r1, g1),
        [pl.BlockSpec((None, tm, FF_SHARD), lambda i, j, k: (k, i, 0)),
         pl.BlockSpec((None, d, FF_SHARD), lambda i, j, k: (k, 0, 0)), tile, tile,
         pl.BlockSpec((1, d), lambda i, j, k: (0, 0))],
        [jax.ShapeDtypeStruct((t, d), F32), jax.ShapeDtypeStruct((t, d), BF16), jax.ShapeDtypeStruct((8, d), F32)],
        [tile, tile, pl.BlockSpec((8, d), lambda i, j, k: (0, 0))], NT, (tm, d), epilogue)


def _cast_bf16(x):
    t = x.shape[0]
    tm = _pick(t, 512)

    def body(x_ref, o_ref):
        o_ref[...] = x_ref[...].astype(BF16)

    tile = pl.BlockSpec((tm, D_MODEL), lambda i: (i, 0))
    return pl.pallas_call(
        body, name="cast_x", grid=(t // tm,), in_specs=[tile], out_specs=tile,
        out_shape=jax.ShapeDtypeStruct((t, D_MODEL), BF16), compiler_params=_params(("parallel",)),
    )(x)


def _relayout(name, a, in_block, in_map, out_block, out_map, out_shape):
    def body(a_ref, o_ref):
        o_ref[...] = a_ref[...].astype(o_ref.dtype)

    return pl.pallas_call(
        body, name=name, grid=(N_DEV,), in_specs=[pl.BlockSpec(in_block, in_map)],
        out_specs=pl.BlockSpec(out_block, out_map), out_shape=out_shape, compiler_params=_params(("parallel",)),
    )(a)


_GELU_C = math.sqrt(2.0 / math.pi)


_GELU_CUBIC = 0.044715


def _gelu_parts(u):
    u2 = u * u
    th = jnp.tanh(u * (_GELU_C + (_GELU_C * _GELU_CUBIC) * u2))
    hu = 0.5 * u
    return th, hu + hu * th, u2, hu


BF16_ROWS = 16


def _ffn_act_fwd(z, w_dw, b_dw):
    t = z.shape[2]
    tm = _pick(t, 1024)
    nh = tm // FFN_HALO

    def body(z_ref, zh_ref, w_ref, b_ref, act_ref, gd_ref, us_ref):
        i = pl.program_id(1)
        us_ref[0:FFN_HALO, :] = jnp.where(i == 0, 0.0, zh_ref[...])
        us_ref[FFN_HALO:FFN_HALO + tm, :] = z_ref[0]
        for r in range(tm // ROW_BLOCK):
            base = r * ROW_BLOCK
            rows = slice(base, base + ROW_BLOCK)
            for lanes in _lane_blocks(FF_SHARD):
                uc = b_ref[:, lanes]
                for k in range(FFN_K):
                    off = base + FFN_HALO - (FFN_K - 1) + k
                    uc = uc + w_ref[k:k + 1, lanes] * us_ref[off:off + ROW_BLOCK, lanes]
                th, gelu, u2, hu = _gelu_parts(uc)
                dgelu = (0.5 + 0.5 * th) + (hu - hu * th * th) * (_GELU_C + (3.0 * _GELU_C * _GELU_CUBIC) * u2)
                act_ref[rows, lanes] = (gelu * z_ref[1, rows, lanes]).astype(BF16)
                gd_ref[0, rows, lanes] = gelu.astype(BF16)
                gd_ref[1, rows, lanes] = dgelu.astype(BF16)

    return pl.pallas_call(
        body, name="ffn_act_fwd", grid=(4, t // tm),
        in_specs=[pl.BlockSpec((2, None, tm, FF_SHARD), lambda j, i: (0, j, i, 0)),
                  pl.BlockSpec((None, None, FFN_HALO, FF_SHARD), lambda j, i: (0, j, jnp.maximum(i * nh - 1, 0), 0)),
                  pl.BlockSpec((None, FFN_K, FF_SHARD), lambda j, i: (j, 0, 0)),
                  pl.BlockSpec((None, 1, FF_SHARD), lambda j, i: (j, 0, 0))],
        out_specs=[pl.BlockSpec((None, tm, FF_SHARD), lambda j, i: (j, i, 0)),
                   pl.BlockSpec((2, None, tm, FF_SHARD), lambda j, i: (0, j, i, 0))],
        out_shape=[jax.ShapeDtypeStruct((4, t, FF_SHARD), BF16), jax.ShapeDtypeStruct((2, 4, t, FF_SHARD), BF16)],
        scratch_shapes=[pltpu.VMEM((FFN_HALO + tm, FF_SHARD), F32)],
        compiler_params=_params(("parallel", "arbitrary")),
    )(z, z, w_dw, b_dw)


def _ffn_act_bwd(d_act, z, gd, w_dw):
    t = z.shape[2]
    tm = _pick(t, 1024)
    nt = t // tm
    nh = tm // FFN_HALO
    last_h = t // FFN_HALO - 1
    pad = FFN_HALO - (FFN_K - 1)

    def fold(x):
        return functools.reduce(jnp.add, [x[r:r + SUBLANES, :] for r in range(0, x.shape[0], SUBLANES)])

    def body(da_ref, dah_ref, z_ref, zp_ref, gn_ref, gd_ref, gdn_ref, w_ref, dz_ref, sums_ref, us_ref, ds_ref,
             part_ref):
        i = pl.program_id(1)

        @pl.when(i == 0)
        def _():
            part_ref[...] = jnp.zeros_like(part_ref)

        us_ref[0:FFN_HALO, :] = jnp.where(i == 0, 0.0, zp_ref[...])
        us_ref[FFN_HALO:FFN_HALO + tm, :] = z_ref[0]
        for r in range(tm // ROW_BLOCK):
            base = r * ROW_BLOCK
            rows = slice(base, base + ROW_BLOCK)
            for lanes in _lane_blocks(FF_SHARD):
                da = da_ref[rows, lanes]
                dz_ref[1, rows, lanes] = (da * gd_ref[0, rows, lanes].astype(F32)).astype(BF16)
                duc = da * z_ref[1, rows, lanes] * gd_ref[1, rows, lanes].astype(F32)
                ds_ref[rows, lanes] = duc
                for k in range(FFN_K):
                    part_ref[k, :, lanes] += fold(duc * us_ref[base + pad + k:base + pad + k + ROW_BLOCK, lanes])
                part_ref[FFN_K, :, lanes] += fold(duc)
        duc_next = dah_ref[...] * gn_ref[...] * gdn_ref[0:FFN_HALO, :].astype(F32)
        ds_ref[tm:tm + FFN_HALO, :] = jnp.where(i == nt - 1, 0.0, duc_next)
        for r in range(tm // ROW_BLOCK):
            base = r * ROW_BLOCK
            for lanes in _lane_blocks(FF_SHARD):
                du = None
                for k in range(FFN_K):
                    off = base + FFN_K - 1 - k
                    term = w_ref[k:k + 1, lanes] * ds_ref[off:off + ROW_BLOCK, lanes]
                    du = term if du is None else du + term
                dz_ref[0, base:base + ROW_BLOCK, lanes] = du.astype(BF16)

        @pl.when(i == nt - 1)
        def _():
            sums_ref[...] = jnp.zeros_like(sums_ref)
            for k in range(FFN_K + 1):
                sums_ref[k:k + 1, :] = _colsum(part_ref[k])

    nxt = lambda i: jnp.minimum((i + 1) * nh, last_h)
    nxt_bf = lambda i: jnp.minimum((i + 1) * (tm // BF16_ROWS), t // BF16_ROWS - 1)
    return pl.pallas_call(
        body, name="ffn_act_bwd", grid=(4, nt),
        in_specs=[pl.BlockSpec((None, tm, FF_SHARD), lambda j, i: (j, i, 0)),
                  pl.BlockSpec((None, FFN_HALO, FF_SHARD), lambda j, i: (j, nxt(i), 0)),
                  pl.BlockSpec((2, None, tm, FF_SHARD), lambda j, i: (0, j, i, 0)),
                  pl.BlockSpec((None, None, FFN_HALO, FF_SHARD), lambda j, i: (0, j, jnp.maximum(i * nh - 1, 0), 0)),
                  pl.BlockSpec((None, None, FFN_HALO, FF_SHARD), lambda j, i: (1, j, nxt(i), 0)),
                  pl.BlockSpec((2, None, tm, FF_SHARD), lambda j, i: (0, j, i, 0)),
                  pl.BlockSpec((None, None, BF16_ROWS, FF_SHARD), lambda j, i: (1, j, nxt_bf(i), 0)),
                  pl.BlockSpec((None, FFN_K, FF_SHARD), lambda j, i: (j, 0, 0))],
        out_specs=[pl.BlockSpec((2, None, tm, FF_SHARD), lambda j, i: (0, j, i, 0)),
                   pl.BlockSpec((None, 8, FF_SHARD), lambda j, i: (j, 0, 0))],
        out_shape=[jax.ShapeDtypeStruct((2, 4, t, FF_SHARD), BF16), jax.ShapeDtypeStruct((4, 8, FF_SHARD), F32)],
        scratch_shapes=[pltpu.VMEM((FFN_HALO + tm, FF_SHARD), F32), pltpu.VMEM((tm + FFN_HALO, FF_SHARD), F32),
                        pltpu.VMEM((FFN_K + 1, SUBLANES, FF_SHARD), F32)],
        compiler_params=_params(("parallel", "arbitrary")),
    )(d_act, d_act, z, z, z, gd, gd, w_dw)


_HGRN_COLS = 4 * HGRN_DIM


def _to_backward_order(w):
    heads = w[:, 2 * CONV_DIM:2 * CONV_DIM + _HGRN_COLS].reshape(-1, 4, HEADS, HEAD_DIM)
    heads = jnp.swapaxes(heads, 1, 2).reshape(-1, _HGRN_COLS)
    return jnp.concatenate([w[:, 2 * CONV_DIM + _HGRN_COLS:], w[:, :2 * CONV_DIM], heads], axis=1)


def _natural_block(p):
    merge, conv = 2 * D_MODEL // HEAD_DIM, 2 * CONV_DIM // HEAD_DIM
    if p < merge:
        return conv + _HGRN_COLS // HEAD_DIM + p
    if p < merge + conv:
        return p - merge
    head, part = divmod(p - merge - conv, 4)
    return conv + part * HEADS + head


def _g_w_in(xb, d_proj):
    t, d = xb.shape
    tk = _pick(t, 2048)
    nk = t // tk
    wide = IN_COLS // 4
    per_block = wide // HEAD_DIM
    per_shard = IN_SHARD // HEAD_DIM

    def body(a_ref, b_ref, out_ref, acc_ref, stage_ref, sem):
        i, k = pl.program_id(0), pl.program_id(2)

        def drain():
            pltpu.make_async_copy(stage_ref, stage_ref, sem).wait()

        @pl.when(k == 0)
        def _():
            acc_ref[...] = jnp.zeros_like(acc_ref)

        acc_ref[...] += _dot(a_ref[...], b_ref[...], TN)

        @pl.when(k == nk - 1)
        def _():
            @pl.when(i > 0)
            def _():
                drain()

            stage_ref[...] = acc_ref[...].astype(BF16)
            for block in range(IN_COLS // wide):
                @pl.when(i == block)
                def _(block=block):
                    for j in range(per_block):
                        shard, off = divmod(_natural_block(per_block * block + j), per_shard)
                        pltpu.make_async_copy(
                            stage_ref.at[:, j * HEAD_DIM:(j + 1) * HEAD_DIM],
                            out_ref.at[shard, :, off * HEAD_DIM:(off + 1) * HEAD_DIM], sem).start()

            @pl.when(i == IN_COLS // wide - 1)
            def _():
                drain()

    return pl.pallas_call(
        body, name="g_w_in", grid=(IN_COLS // wide, 1, nk),
        in_specs=[pl.BlockSpec((tk, d), lambda i, j, k: (k, 0)), pl.BlockSpec((tk, wide), lambda i, j, k: (k, i))],
        out_specs=ANY, out_shape=jax.ShapeDtypeStruct((N_DEV, d, IN_SHARD), BF16),
        scratch_shapes=[pltpu.VMEM((d, wide), F32), pltpu.VMEM((d, wide), BF16), pltpu.SemaphoreType.DMA],
        compiler_params=_params(("arbitrary", "arbitrary", "arbitrary")),
    )(xb, d_proj)


def _local_step(x, target, weights, small, scatter=None, order=None):
    t = x.shape[0]
    tm = _pick(t, 2048)
    tk = _pick(t, 2048)
    nm = t // tm
    nk = t // tk
    d = D_MODEL

    xb = _cast_bf16(x)
    ffn_in_gather = ffn_out_gather = None
    if isinstance(weights, tuple) and isinstance(weights[0], _Hosted):
        first_gather, ffn_in_gather, ffn_out_gather = weights
        proj, w_in_bwd, gathered = _proj_gather(xb, first_gather, order)
        w_in, w_conv_out8, w_hgrn_out8, w_out8, conv_dw8, ffn_dw8 = gathered
    else:
        w_in, w_conv_out8, w_hgrn_out8, w_out8, w_ffn_in, w_ffn_out8, conv_dw8, ffn_dw8 = weights
        proj = _mm("proj", xb, w_in, (t, IN_COLS), F32, (nm, N_DEV, 1),
                   pl.BlockSpec((tm, d), lambda i, j, k: (i, 0)),
                   pl.BlockSpec((None, d, IN_SHARD), lambda i, j, k: (j, 0, 0)),
                   pl.BlockSpec((tm, IN_SHARD), lambda i, j, k: (i, j)), NN, (tm, IN_SHARD))
        w_in_bwd = _to_backward_order(jnp.transpose(w_in, (1, 0, 2)).reshape(d, IN_COLS))
    o, og, states, *late = _hgrn_fwd(proj, small["hgrn_lb_logits"], small["hgrn_norm_g"], hosted=ffn_in_gather)
    if ffn_in_gather is not None:
        (w_ffn_in,) = late
    w_conv_out = _relayout("w_conv_out_natural", w_conv_out8, (None, CONV_DIM, 128), lambda j: (j, 0, 0),
                           (CONV_DIM, 128), lambda j: (0, j), jax.ShapeDtypeStruct((CONV_DIM, d), BF16))
    w_hgrn_out = w_hgrn_out8.reshape(d, d)
    w_out = w_out8.reshape(d, d)
    conv_dw =jnp.transpose(conv_dw8[:, :CONV_K, :CONV_DIM // N_DEV], (1, 0, 2)).reshape(CONV_K, CONV_DIM)
    ffn_dw = jnp.transpose(ffn_dw8[:, :FFN_K, :D_FF // N_DEV], (1, 0, 2)).reshape(FFN_K, 4, FF_SHARD)
    small = dict(small, w_conv_dw=conv_dw, w_ffn_dw=jnp.transpose(ffn_dw, (1, 0, 2)),
                 b_ffn_dw=small["b_ffn_dw"].reshape(4, 1, FF_SHARD))

    c_act, conv_pre = _conv_fwd(proj, small["w_conv_dw"], small["b_conv_dw"], small["conv_ln_g"], small["conv_ln_b"])
    y_conv = _mm("y_conv", c_act, w_conv_out, (t, d), F32, (nm, 1, 1),
                 pl.BlockSpec((tm, CONV_DIM), lambda i, j, k: (i, 0)),
                 pl.BlockSpec((CONV_DIM, d), lambda i, j, k: (0, 0)),
                 pl.BlockSpec((tm, d), lambda i, j, k: (i, 0)), NN, (tm, d))
    sq_w = pl.BlockSpec((d, d), lambda i, j, k: (0, 0))
    row_tile = pl.BlockSpec((tm, d), lambda i, j, k: (i, 0))
    y_hgrn = _mm("y_hgrn", og, w_hgrn_out, (t, d), F32, (nm, 1, 1), row_tile, sq_w, row_tile, NN, (tm, d))
    mixed, r1, x1b, *late = _mix_ln1(proj, y_conv, y_hgrn, w_out, x, small["ln1_g"], small["ln1_b"],
                                     hosted=ffn_out_gather)
    if ffn_out_gather is not None:
        (w_ffn_out8,) = late
    w_ffn_out = w_ffn_out8.reshape(4, FF_SHARD, d)
    z = _mm("ffn_in", x1b, w_ffn_in, (N_DEV, t, FF_SHARD), F32, (nm, N_DEV, 1), row_tile,
            pl.BlockSpec((None, d, FF_SHARD), lambda i, j, k: (j, 0, 0)),
            pl.BlockSpec((None, tm, FF_SHARD), lambda i, j, k: (j, i, 0)), NN, (tm, FF_SHARD))
    z = z.reshape(2, 4, t, FF_SHARD)
    act, gelu_and_slope = _ffn_act_fwd(z, small["w_ffn_dw"], small["b_ffn_dw"])

    d_r2, d_r2b, sums_ln2 = _ffn_out_ln2(act, w_ffn_out, r1, target, small["ln1_g"], small["ln1_b"],
                                         small["ln2_g"], small["ln2_b"])
    d_act = _mm("d_act", d_r2b, w_ffn_out, (4, t, FF_SHARD), F32, (nm, 4, 1), row_tile,
                pl.BlockSpec((None, FF_SHARD, d), lambda i, j, k: (j, 0, 0)),
                pl.BlockSpec((None, tm, FF_SHARD), lambda i, j, k: (j, i, 0)), NT, (tm, FF_SHARD))
    g_w_ffn_out = _mm("g_w_ffn_out", act, d_r2b, (4, FF_SHARD, d), BF16, (4, 1, nk),
                      pl.BlockSpec((None, tk, FF_SHARD), lambda i, j, k: (i, k, 0)),
                      pl.BlockSpec((tk, d), lambda i, j, k: (k, 0)),
                      pl.BlockSpec((None, FF_SHARD, d), lambda i, j, k: (i, 0, 0)), TN, (FF_SHARD, d))
    d_z, sums_ffn = _ffn_act_bwd(d_act, z, gelu_and_slope, small["w_ffn_dw"])
    d_z8 = d_z.reshape(N_DEV, t, FF_SHARD)
    d_r1, d_r1b, sums_ln1 = _d_x1_ln1_bwd(d_z8, w_ffn_in, d_r2, r1, small["ln1_g"])
    g_w_ffn_in = _mm("g_w_ffn_in", x1b, d_z8, (N_DEV, d, FF_SHARD), BF16, (N_DEV, 1, nk),
                     pl.BlockSpec((tk, d), lambda i, j, k: (k, 0)),
                     pl.BlockSpec((None, tk, FF_SHARD), lambda i, j, k: (i, k, 0)),
                     pl.BlockSpec((None, d, FF_SHARD), lambda i, j, k: (i, 0, 0)), TN, (d, FF_SHARD))
    k_tile = pl.BlockSpec((tk, d), lambda i, j, k: (k, 0))
    g_w_out = _mm("g_w_out", mixed, d_r1b, (d, d), BF16, (1, 1, nk), k_tile, k_tile, sq_w, TN, (d, d))
    send = (lambda grads: None) if scatter is None else scatter
    g_w_ffn_out = g_w_ffn_out.reshape(N_DEV, D_FF // N_DEV, d)
    d_y, d_proj, *recv_ffn_out = _d_mixed_merge_bwd(d_r1b, w_out, proj, y_conv, y_hgrn, hosted=send([g_w_ffn_out]))
    d_pre, sums_conv = _d_c_norm_bwd(d_y, w_conv_out, conv_pre, small["conv_ln_g"], small["conv_ln_b"])
    g_w_conv_out = _mm("g_w_conv_out", c_act, d_y, (CONV_DIM, d), BF16, (1, 1, nk),
                       pl.BlockSpec((tk, CONV_DIM), lambda i, j, k: (k, 0)),
                       pl.BlockSpec((None, tk, d), lambda i, j, k: (0, k, 0)),
                       pl.BlockSpec((CONV_DIM, d), lambda i, j, k: (0, 0)), TN, (CONV_DIM, d))
    g_w_conv_out = _relayout("g_w_conv_out_shards", g_w_conv_out, (CONV_DIM, 128), lambda j: (0, j),
                             (None, CONV_DIM, 128), lambda j: (j, 0, 0),
                             jax.ShapeDtypeStruct((N_DEV, CONV_DIM, 128), BF16))
    d_og = _mm("d_og", d_y, w_hgrn_out, (t, d), F32, (nm, 1, 1),
               pl.BlockSpec((None, tm, d), lambda i, j, k: (1, i, 0)), sq_w, row_tile, NT, (tm, d))
    g_w_hgrn_out = _mm("g_w_hgrn_out", og, d_y, (d, d), BF16, (1, 1, nk), k_tile,
                       pl.BlockSpec((None, tk, d), lambda i, j, k: (1, k, 0)), sq_w, TN, (d, d))
    d_proj, g_w_conv_dw, *recv_ffn_in = _conv_bwd_dw(d_pre, proj, small["w_conv_dw"], d_proj,
                                                     hosted=send([g_w_ffn_in]))
    early = [g_w_conv_out, g_w_hgrn_out.reshape(N_DEV, d // N_DEV, d), g_w_out.reshape(N_DEV, d // N_DEV, d)]
    d_proj, sums_hgrn, *early_recv = _hgrn_bwd(d_og, o, proj, states, small["hgrn_lb_logits"], small["hgrn_norm_g"],
                                               d_proj, hosted=send(early))
    early += [g_w_ffn_in, g_w_ffn_out]
    early_recv += recv_ffn_in + recv_ffn_out
    wide = IN_COLS // 4
    g_w_in = _g_w_in(xb, d_proj)
    def add_residual(acc, ins, outs, scr):
        for rows in _row_blocks(ta):
            outs[0][rows, :] = ALPHA * ins[2][rows, :] + acc[rows, :]

    ta = _pick(t, 1024)
    acc_tile = pl.BlockSpec((ta, d), lambda i, j, k: (i, 0))
    grad_x, *late_recv = _mm_fused(
        "grad_x", (t // ta, 1, 4), (d_proj, w_in_bwd, d_r1),
        [pl.BlockSpec((ta, wide), lambda i, j, k: (i, k)), pl.BlockSpec((d, wide), lambda i, j, k: (0, k)), acc_tile],
        [jax.ShapeDtypeStruct((t, d), F32)], [acc_tile], NT, (ta, d), add_residual,
        hosted=None if scatter is None else scatter([g_w_in]))

    large_grads = [g_w_in] + early
    if scatter is not None:
        large_grads = list(zip(large_grads, late_recv + early_recv))
    return grad_x, large_grads, (sums_ln2, sums_conv, sums_hgrn, sums_ln1, sums_ffn, g_w_conv_dw)


def _small_views(sums):
    sums_ln2, sums_conv, sums_hgrn, sums_ln1, sums_ffn, g_w_conv_dw = sums
    d_l0 = sums_hgrn[1:2]
    return {
        "loss": sums_ln2[2:3, 0:128],
        "b_conv_dw": sums_conv[2:3], "conv_ln_g": sums_conv[0:1], "conv_ln_b": sums_conv[1:2],
        "hgrn_lb_logits": jnp.concatenate([d_l0, -d_l0], axis=1),
        "hgrn_norm_g": sums_hgrn[0:1],
        "ln1_g": sums_ln1[0:1], "ln1_b": sums_ln1[1:2],
        "b_ffn_dw": sums_ffn[:, FFN_K, :].reshape(1, D_FF),
        "ln2_g": sums_ln2[0:1], "ln2_b": sums_ln2[1:2],
        "w_conv_dw": g_w_conv_dw[0:CONV_K].reshape(1, CONV_K * CONV_DIM),
        "w_ffn_dw": jnp.transpose(sums_ffn[:, 0:FFN_K, :], (1, 0, 2)).reshape(1, FFN_K * D_FF),
    }


def _coords():
    return lax.axis_index("x"), lax.axis_index("y"), lax.axis_index("c")


def _gather(shards, staged=False):
    n = len(shards)
    later = range(1 if staged else 0, n)

    def parts(ins, outs, sems):
        send_sems, recv_sems, local_sems = sems
        x, y, c = _coords()
        me = 4 * x + 2 * y + c
        sibling = (x, y, 1 - c)
        chips = [(1 - x, y), (x, 1 - y), (1 - x, 1 - y)]

        def copy(a, k, block, to, src=None):
            return pltpu.make_async_remote_copy(
                src_ref=outs[a].at[block] if src is None else src, dst_ref=outs[a].at[block],
                send_sem=send_sems.at[a, k], recv_sem=recv_sems.at[a, k], device_id=to, device_id_type=MESH)

        local = [pltpu.make_async_copy(ins[a], outs[a].at[me], local_sems.at[a]) for a in range(n)]
        first = []
        for a in range(n):
            first.append(copy(a, 0, me, sibling, src=ins[a]))
            for j, chip in enumerate(chips):
                first.append(copy(a, 1 + j, me, (*chip, c), src=ins[a]))
        return x, y, c, sibling, chips, copy, local, first

    def start(ins, outs, sems):
        *_, local, first = parts(ins, outs, sems)
        for cp in local + first:
            cp.start()

    def arrive(ins, outs, sems, s):
        x, y, c, sibling, chips, copy, _, _ = parts(ins, outs, sems)
        if s == 1:
            block = 4 * x + 2 * y + 1 - c
            copy(0, 0, block, sibling).wait_recv()
        elif s <= 4:
            px, py = chips[s - 2]
            block = 4 * px + 2 * py + c
            copy(0, s - 1, block, sibling).wait_recv()
            copy(0, s + 2, block, sibling).start()
        else:
            px, py = chips[s - 5]
            block = 4 * px + 2 * py + 1 - c
            copy(0, s - 1, block, sibling).wait_recv()
        return block

    def middle(ins, outs, sems):
        x, y, c, sibling, chips, copy, _, _ = parts(ins, outs, sems)
        for j, (px, py) in enumerate(chips):
            for a in later:
                copy(a, 1 + j, 4 * px + 2 * py + c, sibling).wait_recv()
                copy(a, 4 + j, 4 * px + 2 * py + c, sibling).start()

    def finish(ins, outs, sems):
        x, y, c, sibling, chips, copy, local, first = parts(ins, outs, sems)
        passed = [copy(a, 4 + j, 4 * px + 2 * py + c, sibling) for a in range(n) for j, (px, py) in enumerate(chips)]
        for a in later:
            copy(a, 0, 4 * x + 2 * y + 1 - c, sibling).wait_recv()
            for j, (px, py) in enumerate(chips):
                copy(a, 4 + j, 4 * px + 2 * py + 1 - c, sibling).wait_recv()
        for cp in first + passed:
            cp.wait_send()
        for cp in local:
            cp.wait()

    hosted = _Hosted(shards, [jax.ShapeDtypeStruct((N_DEV,) + s.shape, s.dtype) for s in shards],
                     [pltpu.SemaphoreType.DMA((n, 7)), pltpu.SemaphoreType.DMA((n, 7)), pltpu.SemaphoreType.DMA((n,))],
                     start, finish, middle)
    hosted.arrive = arrive
    return hosted


def _proj_gather(xb, gather, order):
    t, d = xb.shape
    tm = _pick(t, 2048)
    nm = t // tm
    n_in, n_out = len(gather.inputs), len(gather.out_shapes)

    blocks_per_shard = IN_SHARD // HEAD_DIM

    def body(order_ref, x_ref, *refs):
        ins, refs = refs[:n_in], refs[n_in:]
        o_ref, bwd_ref, outs, refs = refs[0], refs[1], refs[2:2 + n_out], refs[2 + n_out:]
        w_buf, w_sem, bwd_sem, sems = refs[0], refs[1], refs[2], refs[3:]
        s, i = pl.program_id(0), pl.program_id(1)

        def reorder_copies(step):
            copies = []
            for j in range(blocks_per_shard):
                n = blocks_per_shard * order_ref[step] + j
                head_part = n - 2 * CONV_DIM // HEAD_DIM
                p = jnp.where(n >= (2 * CONV_DIM + _HGRN_COLS) // HEAD_DIM, n - (2 * CONV_DIM + _HGRN_COLS) // HEAD_DIM,
                              jnp.where(n < 2 * CONV_DIM // HEAD_DIM, 2 * D_MODEL // HEAD_DIM + n,
                                        (2 * D_MODEL + 2 * CONV_DIM) // HEAD_DIM + 4 * (head_part % HEADS)
                                        + head_part // HEADS))
                copies.append(pltpu.make_async_copy(
                    w_buf.at[step % 2, :, j * HEAD_DIM:(j + 1) * HEAD_DIM],
                    bwd_ref.at[:, pl.ds(pl.multiple_of(p * HEAD_DIM, HEAD_DIM), HEAD_DIM)], bwd_sem.at[step % 2]))
            return copies

        def reorder_done(step):
            pltpu.make_async_copy(w_buf.at[step % 2], w_buf.at[step % 2], bwd_sem.at[step % 2]).wait()

        @pl.when((s == 0) & (i == 0))
        def _():
            gather.start(ins, outs, sems)

        def staging(step, src):
            return pltpu.make_async_copy(src, w_buf.at[step % 2], w_sem.at[step % 2])

        @pl.when((s == 0) & (i == 0))
        def _():
            staging(0, ins[0]).start()

        for step in range(1, N_DEV):
            @pl.when((s == step - 1) & (i == nm - 1))
            def _(step=step):
                if step >= 2:
                    reorder_done(step - 2)
                staging(step, outs[0].at[gather.arrive(ins, outs, sems, step)]).start()

        for step in range(N_DEV):
            @pl.when((s == step) & (i == 0))
            def _(step=step):
                staging(step, ins[0]).wait()
                for cp in reorder_copies(step):
                    cp.start()

        o_ref[...] = _dot(x_ref[...], w_buf[s % 2], NN)

        @pl.when((s == N_DEV - 2) & (i == 0))
        def _():
            gather.middle(ins, outs, sems)

        @pl.when((s == N_DEV - 1) & (i == nm - 1))
        def _():
            reorder_done(N_DEV - 2)
            reorder_done(N_DEV - 1)
            gather.finish(ins, outs, sems)

    outs = pl.pallas_call(
        body, name="proj_gather",
        grid_spec=pltpu.PrefetchScalarGridSpec(
            num_scalar_prefetch=1, grid=(N_DEV, nm),
            in_specs=[pl.BlockSpec((tm, d), lambda s, i, order_ref: (i, 0))] + [ANY] * n_in,
            out_specs=[pl.BlockSpec((tm, IN_SHARD), lambda s, i, order_ref: (i, order_ref[s])), ANY] + [ANY] * n_out,
            scratch_shapes=[pltpu.VMEM((2, d, IN_SHARD), BF16), pltpu.SemaphoreType.DMA((2,)),
                            pltpu.SemaphoreType.DMA((2,))] + gather.sem_shapes),
        out_shape=[jax.ShapeDtypeStruct((t, IN_COLS), F32), jax.ShapeDtypeStruct((d, IN_COLS), BF16)]
        + gather.out_shapes,
        compiler_params=pltpu.CompilerParams(dimension_semantics=("arbitrary", "arbitrary"),
                                             vmem_limit_bytes=VMEM_LIMIT, has_side_effects=True),
    )(order, xb, *gather.inputs)
    return outs[0], outs[1], list(outs[2:])


def _scatter(grads):
    n = len(grads)

    def copies(ins, outs, sems):
        send_sems, recv_sems = sems
        x, y, c = _coords()
        out = []
        for a in range(n):
            for k in range(1, N_DEV):
                px, py, pc = x ^ (k >> 2), y ^ ((k >> 1) & 1), c ^ (k & 1)
                out.append(pltpu.make_async_remote_copy(
                    src_ref=ins[a].at[4 * px + 2 * py + pc], dst_ref=outs[a].at[k - 1],
                    send_sem=send_sems.at[a, k - 1], recv_sem=recv_sems.at[a, k - 1],
                    device_id=(px, py, pc), device_id_type=MESH))
        return out

    def start(ins, outs, sems):
        for cp in copies(ins, outs, sems):
            cp.start()

    def finish(ins, outs, sems):
        for cp in copies(ins, outs, sems):
            cp.wait()

    return _Hosted(grads, [jax.ShapeDtypeStruct((N_DEV - 1,) + g.shape[1:], g.dtype) for g in grads],
                   [pltpu.SemaphoreType.DMA((n, N_DEV - 1)), pltpu.SemaphoreType.DMA((n, N_DEV - 1))], start, finish)


def _row_tile(rows):
    return 256 if rows % 256 == 0 else rows


def _adam_math(w, g, m, v):
    m_new = ADAM_B1 * m + (1.0 - ADAM_B1) * g
    v_new = ADAM_B2 * v + (1.0 - ADAM_B2) * (g * g)
    m_hat = m_new / (1.0 - ADAM_B1 ** ADAM_STEP)
    v_hat = v_new / (1.0 - ADAM_B2 ** ADAM_STEP)
    delta = -ADAM_LR * (m_hat / (jnp.sqrt(v_hat) + ADAM_EPS) + ADAM_WD * w)
    return delta, m_new, v_new


def _adam_large(name, own, recv, me, w, m, v):
    rows, cols = w.shape
    tr = _row_tile(rows)

    def body(me_ref, p_ref, r_ref, w_ref, m_ref, v_ref, g_out, d_out, m_out, v_out):
        g = p_ref[...].astype(F32)
        for k in range(N_DEV - 1):
            g = g + r_ref[k].astype(F32)
        delta, m_new, v_new = _adam_math(w_ref[...], g, m_ref[...], v_ref[...])
        g_out[...] = g
        d_out[...] = delta
        m_out[...] = m_new
        v_out[...] = v_new

    tile = pl.BlockSpec((tr, cols), lambda r, me_ref: (r, 0))
    sds = jax.ShapeDtypeStruct((rows, cols), F32)
    return pl.pallas_call(
        body, name=name,
        grid_spec=pltpu.PrefetchScalarGridSpec(
            num_scalar_prefetch=1, grid=(rows // tr,),
            in_specs=[pl.BlockSpec((None, tr, cols), lambda r, me_ref: (me_ref[0], r, 0)),
                      pl.BlockSpec((N_DEV - 1, tr, cols), lambda r, me_ref: (0, r, 0)), tile, tile, tile],
            out_specs=[tile, tile, tile, tile]),
        out_shape=[sds, sds, sds, sds],
        compiler_params=_params(("parallel",)),
    )(me, own, recv, w, m, v)


def _small_allreduce(arrays):
    n = len(arrays)

    def body(*refs):
        ins, outs, gats = refs[:n], refs[n:2 * n], refs[2 * n:3 * n]
        send_sems, recv_sems = refs[3 * n:]
        x, y, c = _coords()
        me = 4 * x + 2 * y + c
        peers = [(x ^ (k >> 2), y ^ ((k >> 1) & 1), c ^ (k & 1)) for k in range(1, N_DEV)]

        def copy(a, k, slot):
            return pltpu.make_async_remote_copy(
                src_ref=ins[a], dst_ref=gats[a].at[slot], send_sem=send_sems.at[a, k], recv_sem=recv_sems.at[a, k],
                device_id=peers[k], device_id_type=MESH)

        sends = [copy(a, k, me) for a in range(n) for k in range(N_DEV - 1)]
        for a in range(n):
            gats[a][me] = ins[a][...]
        for cp in sends:
            cp.start()
        for a in range(n):
            for k, (px, py, pc) in enumerate(peers):
                copy(a, k, 4 * px + 2 * py + pc).wait_recv()
        for cp in sends:
            cp.wait_send()
        for a in range(n):
            acc = gats[a][0]
            for dev in range(1, N_DEV):
                acc = acc + gats[a][dev]
            outs[a][...] = acc

    whole = pl.BlockSpec(memory_space=pltpu.VMEM)
    return pl.pallas_call(
        body, name="small_allreduce", in_specs=[whole] * n, out_specs=[whole] * n,
        out_shape=[jax.ShapeDtypeStruct(a.shape, F32) for a in arrays],
        scratch_shapes=[pltpu.VMEM((N_DEV,) + a.shape, F32) for a in arrays]
        + [pltpu.SemaphoreType.DMA((n, N_DEV - 1)), pltpu.SemaphoreType.DMA((n, N_DEV - 1))],
        compiler_params=pltpu.CompilerParams(has_side_effects=True, vmem_limit_bytes=VMEM_LIMIT),
    )(*arrays)


def _adam_replicated(sums, w, m, v):
    rows_of = {"conv_ln_g": (1, 0), "conv_ln_b": (1, 1), "b_conv_dw": (1, 2), "hgrn_norm_g": (2, 0),
               "ln1_g": (3, 0), "ln1_b": (3, 1), "ln2_g": (0, 0), "ln2_b": (0, 1)}
    names = list(rows_of) + ["hgrn_lb_logits"]
    n = len(names)

    def body(*refs):
        sum_refs, refs = refs[:4], refs[4:]
        w_refs, m_refs, v_refs, outs = refs[:n], refs[n:2 * n], refs[2 * n:3 * n], refs[3 * n:]
        for j, name in enumerate(names):
            if name == "hgrn_lb_logits":
                d_l0 = sum_refs[2][1:2, :]
                grads = [d_l0, -d_l0]
            else:
                a, row = rows_of[name]
                grads = [sum_refs[a][row:row + 1, :]]
            g_out, d_out, m_out, v_out = outs[4 * j:4 * j + 4]
            for r, g in enumerate(grads):
                rows = slice(r, r + 1)
                delta, m_new, v_new = _adam_math(w_refs[j][rows, :], g, m_refs[j][rows, :], v_refs[j][rows, :])
                g_out[rows, :] = g
                d_out[rows, :] = delta
                m_out[rows, :] = m_new
                v_out[rows, :] = v_new

    whole = pl.BlockSpec(memory_space=pltpu.VMEM)
    operands = list(sums) + [w[k] for k in names] + [m[k] for k in names] + [v[k] for k in names]
    outs = pl.pallas_call(
        body, name="adam_replicated", in_specs=[whole] * len(operands), out_specs=[whole] * (4 * n),
        out_shape=[jax.ShapeDtypeStruct(w[k].shape, F32) for k in names for _ in range(4)],
    )(*operands)
    return {name: tuple(outs[4 * j:4 * j + 4]) for j, name in enumerate(names)}


def _adam_small(w, g, m, v):
    def body(w_ref, g_ref, m_ref, v_ref, d_out, m_out, v_out):
        delta, m_new, v_new = _adam_math(w_ref[...], g_ref[...], m_ref[...], v_ref[...])
        d_out[...] = delta
        m_out[...] = m_new
        v_out[...] = v_new

    whole = pl.BlockSpec(memory_space=pltpu.VMEM)
    sds = jax.ShapeDtypeStruct(w.shape, F32)
    return pl.pallas_call(body, name="adam_small", in_specs=[whole] * 4, out_specs=[whole] * 3,
                          out_shape=[sds, sds, sds])(w, g, m, v)


_WEIGHTS = ["w_in", "w_conv_dw", "b_conv_dw", "conv_ln_g", "conv_ln_b", "w_conv_out", "hgrn_lb_logits", "hgrn_norm_g",
            "w_hgrn_out", "w_out", "ln1_g", "ln1_b", "w_ffn_in", "w_ffn_dw", "b_ffn_dw", "w_ffn_out", "ln2_g", "ln2_b"]
_LARGE = ["w_in", "w_conv_out", "w_hgrn_out", "w_out", "w_ffn_in", "w_ffn_out"]
_CONV_DW_SHARD = CONV_DIM // N_DEV
_FFN_DW_SHARD = D_FF // N_DEV


def kernel(x, w_in, w_conv_dw, b_conv_dw, conv_ln_g, conv_ln_b, w_conv_out, hgrn_lb_logits, hgrn_norm_g, w_hgrn_out, w_out, ln1_g, ln1_b, w_ffn_in, w_ffn_dw, b_ffn_dw, w_ffn_out, ln2_g, ln2_b, loss_target, m_w_in, m_w_conv_dw, m_b_conv_dw, m_conv_ln_g, m_conv_ln_b, m_w_conv_out, m_hgrn_lb_logits, m_hgrn_norm_g, m_w_hgrn_out, m_w_out, m_ln1_g, m_ln1_b, m_w_ffn_in, m_w_ffn_dw, m_b_ffn_dw, m_w_ffn_out, m_ln2_g, m_ln2_b, v_w_in, v_w_conv_dw, v_b_conv_dw, v_conv_ln_g, v_conv_ln_b, v_w_conv_out, v_hgrn_lb_logits, v_hgrn_norm_g, v_w_hgrn_out, v_w_out, v_ln1_g, v_ln1_b, v_w_ffn_in, v_w_ffn_dw, v_b_ffn_dw, v_w_ffn_out, v_ln2_g, v_ln2_b):
    w = dict(w_in=w_in, w_conv_dw=w_conv_dw, b_conv_dw=b_conv_dw, conv_ln_g=conv_ln_g, conv_ln_b=conv_ln_b,
             w_conv_out=w_conv_out, hgrn_lb_logits=hgrn_lb_logits, hgrn_norm_g=hgrn_norm_g, w_hgrn_out=w_hgrn_out,
             w_out=w_out, ln1_g=ln1_g, ln1_b=ln1_b, w_ffn_in=w_ffn_in, w_ffn_dw=w_ffn_dw, b_ffn_dw=b_ffn_dw,
             w_ffn_out=w_ffn_out, ln2_g=ln2_g, ln2_b=ln2_b)
    m = dict(w_in=m_w_in, w_conv_dw=m_w_conv_dw, b_conv_dw=m_b_conv_dw, conv_ln_g=m_conv_ln_g, conv_ln_b=m_conv_ln_b,
             w_conv_out=m_w_conv_out, hgrn_lb_logits=m_hgrn_lb_logits, hgrn_norm_g=m_hgrn_norm_g,
             w_hgrn_out=m_w_hgrn_out, w_out=m_w_out, ln1_g=m_ln1_g, ln1_b=m_ln1_b, w_ffn_in=m_w_ffn_in,
             w_ffn_dw=m_w_ffn_dw, b_ffn_dw=m_b_ffn_dw, w_ffn_out=m_w_ffn_out, ln2_g=m_ln2_g, ln2_b=m_ln2_b)
    v = dict(w_in=v_w_in, w_conv_dw=v_w_conv_dw, b_conv_dw=v_b_conv_dw, conv_ln_g=v_conv_ln_g, conv_ln_b=v_conv_ln_b,
             w_conv_out=v_w_conv_out, hgrn_lb_logits=v_hgrn_lb_logits, hgrn_norm_g=v_hgrn_norm_g,
             w_hgrn_out=v_w_hgrn_out, w_out=v_w_out, ln1_g=v_ln1_g, ln1_b=v_ln1_b, w_ffn_in=v_w_ffn_in,
             w_ffn_dw=v_w_ffn_dw, b_ffn_dw=v_b_ffn_dw, w_ffn_out=v_w_ffn_out, ln2_g=v_ln2_g, ln2_b=v_ln2_b)
    xi, yi, ci = lax.axis_index("x"), lax.axis_index("y"), lax.axis_index("c")
    me = 4 * xi + 2 * yi + ci
    me_op = jnp.reshape(me, (1,)).astype(jnp.int32)

    shards = [w[name][0].astype(BF16) for name in _LARGE]
    shards.append(jnp.pad(w_conv_dw[0], ((0, 1), (0, 128 - _CONV_DW_SHARD))))
    shards.append(jnp.pad(w_ffn_dw[0], ((0, 8 - FFN_K), (0, 384 - _FFN_DW_SHARD))))
    chips = [(1 - xi, yi), (xi, 1 - yi), (1 - xi, 1 - yi)]
    order = jnp.stack([me, me ^ 1] + [4 * px + 2 * py + ci for px, py in chips]
                      + [4 * px + 2 * py + 1 - ci for px, py in chips]).astype(jnp.int32)
    small = dict(b_conv_dw=b_conv_dw, conv_ln_g=conv_ln_g, conv_ln_b=conv_ln_b, hgrn_lb_logits=hgrn_lb_logits,
                 hgrn_norm_g=hgrn_norm_g, ln1_g=ln1_g, ln1_b=ln1_b, ln2_g=ln2_g, ln2_b=ln2_b, b_ffn_dw=b_ffn_dw)

    gathers = (_gather(shards[:4] + shards[6:], staged=True), _gather(shards[4:5]), _gather(shards[5:6]))
    grad_x, large_grads, small_sums = _local_step(x[0], loss_target[0], gathers, small, _scatter, order)

    out = {}
    for name, (own, recv) in zip(_LARGE, large_grads):
        out[name] = _adam_large("adam_" + name, own, recv, me_op, w[name][0], m[name][0], v[name][0])

    totals = _small_allreduce(list(small_sums))
    out.update(_adam_replicated(totals[:4], w, m, v))
    summed = _small_views(totals)
    loss = summed["loss"][0, 0]
    conv_dw_g = lax.dynamic_slice_in_dim(summed["w_conv_dw"].reshape(CONV_K, CONV_DIM), me * _CONV_DW_SHARD, _CONV_DW_SHARD, axis=1)
    ffn_dw_g = lax.dynamic_slice_in_dim(summed["w_ffn_dw"].reshape(FFN_K, D_FF), me * _FFN_DW_SHARD, _FFN_DW_SHARD, axis=1)
    small_g = dict(b_ffn_dw=summed["b_ffn_dw"], w_conv_dw=conv_dw_g.reshape(1, -1), w_ffn_dw=ffn_dw_g.reshape(1, -1))
    names = list(small_g)
    flat = lambda d, n: d[n].reshape(1, -1)
    n_small = sum(small_g[n].shape[1] for n in names)
    pad = (-n_small) % 1024
    pack = lambda pieces: jnp.pad(jnp.concatenate(pieces, axis=1), ((0, 0), (0, pad))).reshape(-1, 128)
    d_s, m_s, v_s = _adam_small(pack([flat(w, n) for n in names]), pack([small_g[n] for n in names]),
                                pack([flat(m, n) for n in names]), pack([flat(v, n) for n in names]))
    pos = 0
    for n in names:
        size = small_g[n].shape[1]
        cut = lambda a: a.reshape(1, -1)[:, pos:pos + size].reshape(w[n].shape)
        out[n] = (small_g[n].reshape(w[n].shape), cut(d_s), cut(m_s), cut(v_s))
        pos += size

    for name in _LARGE:
        out[name] = tuple(a.reshape(w[name].shape) for a in out[name])
    grads = [out[n][0] for n in _WEIGHTS]
    deltas = [out[n][1] for n in _WEIGHTS]
    new_m = [out[n][2] for n in _WEIGHTS]
    new_v = [out[n][3] for n in _WEIGHTS]
    return (loss, grad_x[None], *grads, *deltas, *new_m, *new_v)
```

```python
import functools
import math

import jax
import jax.numpy as jnp
from jax import lax
from jax.experimental import pallas as pl
from jax.experimental.pallas import tpu as pltpu

F32 = jnp.float32
BF16 = jnp.bfloat16

N_DEV = 8
D_MODEL = 1024
CONV_DIM = 512
CONV_K = 31
HGRN_DIM = 1024
HEADS = 8
HEAD_DIM = 128
D_FF = 2816
FFN_K = 3
FF_SHARD = 2 * D_FF // N_DEV
IN_COLS = 7168
IN_SHARD = IN_COLS // N_DEV
LN_EPS = 1e-5
RMS_EPS = 1e-6
ALPHA = 2.0 ** 0.25

ADAM_LR = 0.001
ADAM_B1 = 0.9
ADAM_B2 = 0.999
ADAM_EPS = 1e-08
ADAM_WD = 0.01
ADAM_STEP = 10

CHUNK = 64
CHUNKS_PER_BLOCK = 32
CONV_HALO = 32
FFN_HALO = 8
ROW_BLOCK = 64
SUBLANES = 8
VMEM_LIMIT = 48 * 1024 * 1024
MXU_DEPTH = 256

DP_MERGE_BLOCK = 0
DP_CONV_BLOCK = 2
DP_HEAD_BLOCK = 6

MESH = pl.DeviceIdType.MESH
ANY = pl.BlockSpec(memory_space=pl.ANY)

NN = (((1,), (0,)), ((), ()))
NT = (((1,), (1,)), ((), ()))
TN = (((0,), (0,)), ((), ()))


def _params(sem):
    return pltpu.CompilerParams(dimension_semantics=sem, vmem_limit_bytes=VMEM_LIMIT)


def _dot(a, b, dims):
    return lax.dot_general(a.astype(BF16), b.astype(BF16), dims, preferred_element_type=F32)


def _sigmoid(x):
    return jax.nn.sigmoid(x)


def _ln(r):
    mu = jnp.mean(r, axis=-1, keepdims=True)
    xc = r - mu
    var = jnp.mean(xc * xc, axis=-1, keepdims=True)
    rstd = lax.rsqrt(var + LN_EPS)
    return xc * rstd, rstd


def _ln_bwd(dy, xhat, rstd, g):
    dxh = dy * g
    m1 = jnp.mean(dxh, axis=-1, keepdims=True)
    m2 = jnp.mean(dxh * xhat, axis=-1, keepdims=True)
    return rstd * (dxh - m1 - xhat * m2)


def _colsum(x):
    return jnp.sum(x, axis=0, keepdims=True)


class _Hosted:
    def __init__(self, inputs, out_shapes, sem_shapes, start, finish, middle=None):
        self.inputs, self.out_shapes, self.sem_shapes = list(inputs), list(out_shapes), list(sem_shapes)
        self.start, self.finish, self.middle = start, finish, middle


def _call(body, *, name, grid, in_specs, out_specs, out_shape, scratch_shapes, semantics, operands, hosted=None,
          aliases=None):
    aliases = aliases or {}
    if hosted is None:
        return pl.pallas_call(
            body, name=name, grid=grid, in_specs=list(in_specs), out_specs=list(out_specs), out_shape=list(out_shape),
            scratch_shapes=list(scratch_shapes), input_output_aliases=aliases,
            compiler_params=_params(semantics))(*operands)
    n_in, n_out, n_scr = len(in_specs), len(out_specs), len(scratch_shapes)
    h_in, h_out = len(hosted.inputs), len(hosted.out_shapes)

    def full_body(*refs):
        ins, refs = refs[:n_in], refs[n_in:]
        h_ins, refs = refs[:h_in], refs[h_in:]
        outs, refs = refs[:n_out], refs[n_out:]
        h_outs, refs = refs[:h_out], refs[h_out:]
        scr, sems = refs[:n_scr], refs[n_scr:]
        first = functools.reduce(jnp.logical_and, [pl.program_id(d) == 0 for d in range(len(grid))])
        last = functools.reduce(jnp.logical_and, [pl.program_id(d) == grid[d] - 1 for d in range(len(grid))])

        @pl.when(first)
        def _():
            hosted.start(h_ins, h_outs, sems)

        body(*ins, *outs, *scr)

        if hosted.middle is not None:
            step, total = 0, 1
            for d in range(len(grid)):
                step, total = step * grid[d] + pl.program_id(d), total * grid[d]

            @pl.when(step == (2 * total) // 3)
            def _():
                hosted.middle(h_ins, h_outs, sems)

        @pl.when(last)
        def _():
            hosted.finish(h_ins, h_outs, sems)

    return pl.pallas_call(
        full_body, name=name, grid=grid, in_specs=list(in_specs) + [ANY] * h_in,
        out_specs=list(out_specs) + [ANY] * h_out, out_shape=list(out_shape) + hosted.out_shapes,
        scratch_shapes=list(scratch_shapes) + hosted.sem_shapes, input_output_aliases=aliases,
        compiler_params=pltpu.CompilerParams(dimension_semantics=("arbitrary",) * len(grid),
                                             vmem_limit_bytes=VMEM_LIMIT, has_side_effects=True),
    )(*operands, *hosted.inputs)


def _mm(name, a, b, out_shape, out_dtype, grid, a_spec, b_spec, o_spec, dims, acc_shape, hosted=None):
    nk = grid[2]
    if nk == 1:
        def body(a_ref, b_ref, o_ref):
            o_ref[...] = _dot(a_ref[...], b_ref[...], dims).astype(o_ref.dtype)
        scratch = []
    else:
        def body(a_ref, b_ref, o_ref, acc_ref):
            k = pl.program_id(2)

            @pl.when(k == 0)
            def _():
                acc_ref[...] = jnp.zeros_like(acc_ref)

            acc_ref[...] += _dot(a_ref[...], b_ref[...], dims)

            @pl.when(k == nk - 1)
            def _():
                o_ref[...] = acc_ref[...].astype(o_ref.dtype)
        scratch = [pltpu.VMEM(acc_shape, F32)]

    outs = _call(body, name=name, grid=grid, in_specs=[a_spec, b_spec], out_specs=[o_spec],
                 out_shape=[jax.ShapeDtypeStruct(out_shape, out_dtype)], scratch_shapes=scratch,
                 semantics=("parallel", "parallel", "arbitrary"), operands=(a, b), hosted=hosted)
    return outs[0] if hosted is None else (outs[0], list(outs[1:]))


def _mm_fused(name, grid, operands, in_specs, out_shape, out_specs, dims, acc_shape, epilogue, lhs=None, scratch=(),
              hosted=None, aliases=None):
    nk = grid[2]
    n_in, n_out = len(in_specs), len(out_specs)

    def body(*refs):
        ins, outs, scr = refs[:n_in], refs[n_in:n_in + n_out], refs[n_in + n_out:]
        acc_ref, k = scr[0], pl.program_id(2)
        a = ins[0][...] if lhs is None else lhs(ins, outs)
        part = _dot(a, ins[1][...], dims)
        if nk == 1:
            acc_ref[...] = part
            epilogue(acc_ref, ins, outs, scr[1:])
            return

        @pl.when(k == 0)
        def _():
            acc_ref[...] = jnp.zeros_like(acc_ref)

        acc_ref[...] += part

        @pl.when(k == nk - 1)
        def _():
            epilogue(acc_ref, ins, outs, scr[1:])

    return _call(body, name=name, grid=grid, in_specs=in_specs, out_specs=out_specs, out_shape=out_shape,
                 scratch_shapes=[pltpu.VMEM(acc_shape, F32)] + list(scratch), semantics=("arbitrary",) * 3,
                 operands=operands, hosted=hosted, aliases=aliases)


def _row_blocks(rows, block=256):
    block = block if rows % block == 0 else rows
    return [slice(r, r + block) for r in range(0, rows, block)]


def _pick(t, pref):
    return pref if t % pref == 0 else t


def _glu(p):
    return p[:, :CONV_DIM] * _sigmoid(p[:, CONV_DIM:])


def _by_phase(taps):
    phases = {}
    for off, payload in taps:
        phases.setdefault(off % SUBLANES, []).append((off - off % SUBLANES, payload))
    return sorted(phases.items())


def _tap_sum(src_ref, base, taps, rows, lanes):
    acc = None
    for phase, items in _by_phase(taps):
        n = rows if phase == 0 else rows + SUBLANES
        part = None
        for off, (w_ref, k) in items:
            term = w_ref[k:k + 1, lanes] * src_ref[base + off:base + off + n, lanes]
            part = term if part is None else part + term
        if phase:
            part = part[phase:phase + rows, :]
        acc = part if acc is None else acc + part
    return acc


def _tap_products(x, src_ref, base, taps, lanes):
    rows, cols = x.shape
    pad = jnp.zeros((SUBLANES, cols), x.dtype)
    padded = jnp.concatenate([pad, x, pad], axis=0)
    out = []
    for phase, items in _by_phase(taps):
        n = rows if phase == 0 else rows + SUBLANES
        shifted = x if phase == 0 else padded[SUBLANES - phase:SUBLANES - phase + n, :]
        for off, key in items:
            out.append((key, _colsum(shifted * src_ref[base + off:base + off + n, lanes])))
    return out


def _lane_blocks(cols, block=256):
    return [slice(c, min(c + block, cols)) for c in range(0, cols, block)]


def _conv_fwd(proj, w_dw, b_dw, g, b):
    t = proj.shape[0]
    tm = _pick(t, 512)
    nh = tm // CONV_HALO

    def body(p_ref, ph_ref, w_ref, bd_ref, g_ref, b_ref, act_ref, pre_ref, xs_ref):
        i = pl.program_id(0)
        halo = _glu(ph_ref[...])
        xs_ref[0:CONV_HALO, :] = jnp.where(i == 0, 0.0, halo)
        xs_ref[CONV_HALO:CONV_HALO + tm, :] = _glu(p_ref[...])
        taps = [(CONV_HALO - (CONV_K - 1) + k, (w_ref, k)) for k in range(CONV_K)]
        for r in range(tm // ROW_BLOCK):
            rows = slice(r * ROW_BLOCK, (r + 1) * ROW_BLOCK)
            for lanes in _lane_blocks(CONV_DIM):
                pre_ref[rows, lanes] = bd_ref[:, lanes] + _tap_sum(xs_ref, r * ROW_BLOCK, taps, ROW_BLOCK, lanes)
            acc = pre_ref[rows, :]
            xhat, _ = _ln(acc)
            yln = xhat * g_ref[...] + b_ref[...]
            act_ref[rows, :] = (yln * _sigmoid(yln)).astype(BF16)

    full = lambda s: pl.BlockSpec(s, lambda i: (0, 0))
    return pl.pallas_call(
        body, name="conv_fwd", grid=(t // tm,),
        in_specs=[pl.BlockSpec((tm, 2 * CONV_DIM), lambda i: (i, 0)),
                  pl.BlockSpec((CONV_HALO, 2 * CONV_DIM), lambda i: (jnp.maximum(i * nh - 1, 0), 0)),
                  full((CONV_K, CONV_DIM)), full((1, CONV_DIM)), full((1, CONV_DIM)), full((1, CONV_DIM))],
        out_specs=[pl.BlockSpec((tm, CONV_DIM), lambda i: (i, 0)), pl.BlockSpec((tm, CONV_DIM), lambda i: (i, 0))],
        out_shape=[jax.ShapeDtypeStruct((t, CONV_DIM), BF16), jax.ShapeDtypeStruct((t, CONV_DIM), F32)],
        scratch_shapes=[pltpu.VMEM((CONV_HALO + tm, CONV_DIM), F32)],
        compiler_params=_params(("arbitrary",)),
    )(proj, proj, w_dw, b_dw, g, b)


def _d_c_norm_bwd(d_y, w_conv_out, pre, g, b):
    t = pre.shape[0]
    tm = _pick(t, 512)
    d = D_MODEL

    def epilogue(d_c, ins, outs, scr):
        pre_ref, g_ref, b_ref = ins[2:]
        dpre_ref, sums_ref = outs
        i = pl.program_id(0)

        @pl.when(i == 0)
        def _():
            sums_ref[...] = jnp.zeros_like(sums_ref)

        for rows in _row_blocks(tm):
            xhat, rstd = _ln(pre_ref[rows, :])
            yln = xhat * g_ref[...] + b_ref[...]
            sg = _sigmoid(yln)
            dyln = d_c[rows, :] * (sg * (1.0 + yln * (1.0 - sg)))
            dpre = _ln_bwd(dyln, xhat, rstd, g_ref[...])
            dpre_ref[rows, :] = dpre
            sums_ref[0:1, :] += _colsum(dyln * xhat)
            sums_ref[1:2, :] += _colsum(dyln)
            sums_ref[2:3, :] += _colsum(dpre)

    full = lambda s: pl.BlockSpec(s, lambda i, j, k: (0, 0))
    tile = pl.BlockSpec((tm, CONV_DIM), lambda i, j, k: (i, 0))
    return _mm_fused(
        "d_c_norm_bwd", (t // tm, 1, 1), (d_y, w_conv_out, pre, g, b),
        [pl.BlockSpec((None, tm, d), lambda i, j, k: (0, i, 0)), full((CONV_DIM, d)), tile,
         full((1, CONV_DIM)), full((1, CONV_DIM))],
        [jax.ShapeDtypeStruct((t, CONV_DIM), F32), jax.ShapeDtypeStruct((8, CONV_DIM), F32)],
        [tile, full((8, CONV_DIM))], NT, (tm, CONV_DIM), epilogue)


def _conv_bwd_dw(d_pre, proj, w_dw, d_proj, hosted=None):
    t = d_pre.shape[0]
    tm = _pick(t, 512)
    nt = t // tm
    nh = tm // CONV_HALO
    last_h = t // CONV_HALO - 1

    def body(dp_ref, dph_ref, p_ref, ph_ref, w_ref, _, dproj_ref, dw_ref, xs_ref, ds_ref):
        i = pl.program_id(0)

        @pl.when(i == 0)
        def _():
            dw_ref[...] = jnp.zeros_like(dw_ref)

        halo = _glu(ph_ref[...])
        xs_ref[0:CONV_HALO, :] = jnp.where(i == 0, 0.0, halo)
        xs_ref[CONV_HALO:CONV_HALO + tm, :] = _glu(p_ref[...])
        ds_ref[0:tm, :] = dp_ref[...]
        ds_ref[tm:tm + CONV_HALO, :] = jnp.where(i == nt - 1, 0.0, dph_ref[...])
        back_taps = [(CONV_K - 1 - k, (w_ref, k)) for k in range(CONV_K)]
        grad_taps = [(CONV_HALO - (CONV_K - 1) + k, k) for k in range(CONV_K)]
        for r in range(tm // ROW_BLOCK):
            base = r * ROW_BLOCK
            rows = slice(base, base + ROW_BLOCK)
            for lanes in _lane_blocks(CONV_DIM):
                gate_lanes = slice(CONV_DIM + lanes.start, CONV_DIM + lanes.stop)
                acc = _tap_sum(ds_ref, base, back_taps, ROW_BLOCK, lanes)
                for k, total in _tap_products(ds_ref[rows, lanes], xs_ref, base, grad_taps, lanes):
                    dw_ref[k:k + 1, lanes] += total
                cval = p_ref[rows, lanes]
                sg = _sigmoid(p_ref[rows, gate_lanes])
                dproj_ref[rows, lanes] = (acc * sg).astype(BF16)
                dproj_ref[rows, gate_lanes] = (acc * cval * sg * (1.0 - sg)).astype(BF16)

    full = lambda s: pl.BlockSpec(s, lambda i: (0, 0))
    return _call(
        body, name="conv_bwd_dw", grid=(nt,),
        in_specs=[pl.BlockSpec((tm, CONV_DIM), lambda i: (i, 0)),
                  pl.BlockSpec((CONV_HALO, CONV_DIM), lambda i: (jnp.minimum((i + 1) * nh, last_h), 0)),
                  pl.BlockSpec((tm, 2 * CONV_DIM), lambda i: (i, 0)),
                  pl.BlockSpec((CONV_HALO, 2 * CONV_DIM), lambda i: (jnp.maximum(i * nh - 1, 0), 0)),
                  full((CONV_K, CONV_DIM)), ANY],
        out_specs=[pl.BlockSpec((tm, 2 * CONV_DIM), lambda i: (i, DP_CONV_BLOCK)), full((CONV_HALO, CONV_DIM))],
        out_shape=[jax.ShapeDtypeStruct(d_proj.shape, BF16), jax.ShapeDtypeStruct((CONV_HALO, CONV_DIM), F32)],
        scratch_shapes=[pltpu.VMEM((CONV_HALO + tm, CONV_DIM), F32), pltpu.VMEM((tm + CONV_HALO, CONV_DIM), F32)],
        semantics=("arbitrary",), operands=(d_pre, d_pre, proj, proj, w_dw, d_proj), hosted=hosted, aliases={5: 0})


def _lower_bound(logit_ref):
    l0 = logit_ref[0:1, :]
    l1 = logit_ref[1:2, :]
    m = jnp.maximum(l0, l1)
    e0 = jnp.exp(l0 - m)
    e1 = jnp.exp(l1 - m)
    return e0 / (e0 + e1)


def _tri(lower):
    r = lax.broadcasted_iota(jnp.int32, (CHUNK, CHUNK), 0)
    c = lax.broadcasted_iota(jnp.int32, (CHUNK, CHUNK), 1)
    return (c <= r) if lower else (c >= r)


def _hgrn_gates(fz, lb):
    s = _sigmoid(fz)
    sn = _sigmoid(-fz)
    f = lb + (1.0 - lb) * s
    return s, sn, f


def _block_tri(rows, lower=True):
    r = lax.broadcasted_iota(jnp.int32, (rows, rows), 0)
    c = lax.broadcasted_iota(jnp.int32, (rows, rows), 1)
    tri = (c <= r) if lower else (c >= r)
    return (tri & (r // CHUNK == c // CHUNK)).astype(BF16)


def _tri_rows(tm):
    return min(tm, MXU_DEPTH)


def _tri_matmul(tri_ref, x):
    hi = x.astype(BF16)
    lo = (x - hi.astype(F32)).astype(BF16)
    tri = tri_ref[...]
    return (lax.dot_general(tri, hi, NN, preferred_element_type=F32)
            + lax.dot_general(tri, lo, NN, preferred_element_type=F32))


def _groups(tm):
    g = _tri_rows(tm)
    return [slice(i * g, (i + 1) * g) for i in range(tm // g)]


def _hgrn_fwd(proj, logits, norm_g, hosted=None):
    t = proj.shape[0]
    tm = CHUNK * CHUNKS_PER_BLOCK if t % (CHUNK * CHUNKS_PER_BLOCK) == 0 else CHUNK
    cpb = tm // CHUNK
    nt = t // tm
    half = CHUNK // 2

    def body(qz_ref, fz_ref, iv_ref, gz_ref, lg_ref, ng_ref, tri_ref, o_ref, og_ref, st_ref,
             state_ref, qe_ref, ke_ref, qb_ref, kl_ref, v_ref, upd_ref, decay_ref, a_ref, q_ref, kk_ref, b_ref):
        j = pl.program_id(1)

        @pl.when(j == 0)
        def _():
            state_ref[...] = jnp.zeros_like(state_ref)

        lb = _lower_bound(lg_ref)
        chunks = [slice(c * CHUNK, (c + 1) * CHUNK) for c in range(cpb)]
        for rows in chunks:
            qz = qz_ref[rows, :]
            q_ref[rows, :] = qz * _sigmoid(qz)
            _, sn, f = _hgrn_gates(fz_ref[rows, :], lb)
            kk_ref[rows, :] = (1.0 - lb) * sn
            b_ref[rows, :] = jnp.log(f)
            v_ref[rows, :] = iv_ref[rows, :].astype(BF16)
        for rows in _groups(tm):
            b_ref[rows, :] = _tri_matmul(tri_ref, b_ref[rows, :])
        for c, rows in enumerate(chunks):
            b = b_ref[rows, :]
            bref = b[half - 1:half, :]
            blast = b[CHUNK - 1:CHUNK, :]
            q = q_ref[rows, :]
            kk = kk_ref[rows, :]
            qb_ref[rows, :] = (q * jnp.exp(b)).astype(BF16)
            qe_ref[rows, :] = (q * jnp.exp(b - bref)).astype(BF16)
            ke_ref[rows, :] = (kk * jnp.exp(bref - b)).astype(BF16)
            kl_ref[rows, :] = (kk * jnp.exp(blast - b)).astype(BF16)
            decay_ref[c:c + 1, :] = jnp.exp(blast)
        causal = _tri(True)
        for c, rows in enumerate(chunks):
            upd_ref[c] = _dot(v_ref[rows, :], kl_ref[rows, :], TN)
            a_ref[c] = jnp.where(causal, _dot(qe_ref[rows, :], ke_ref[rows, :], NT), 0.0).astype(BF16)
        state = state_ref[...]
        for c in range(cpb):
            st_ref[c] = state.astype(BF16)
            state = state * decay_ref[c:c + 1, :] + upd_ref[c]
        state_ref[...] = state
        for c, rows in enumerate(chunks):
            o_ref[rows, :] = _dot(a_ref[c], v_ref[rows, :], NN) + _dot(qb_ref[rows, :], st_ref[c], NT)
        for rows in chunks:
            o = o_ref[rows, :]
            r = lax.rsqrt(jnp.mean(o * o, axis=-1, keepdims=True) + RMS_EPS)
            gz = gz_ref[rows, :]
            og_ref[rows, :] = ((o * r * ng_ref[...]) * (gz * _sigmoid(gz))).astype(BF16)

    col = lambda base: pl.BlockSpec((tm, HEAD_DIM), lambda h, j: (j, base + h))
    tile_bf = pltpu.VMEM((tm, HEAD_DIM), BF16)
    tile_f32 = pltpu.VMEM((tm, HEAD_DIM), F32)
    return _call(
        body, name="hgrn_fwd", grid=(HEADS, nt),
        in_specs=[col(8), col(16), col(24), col(32),
                  pl.BlockSpec((2, HEAD_DIM), lambda h, j: (0, h)), pl.BlockSpec((1, HEAD_DIM), lambda h, j: (0, h)),
                  pl.BlockSpec((_tri_rows(tm), _tri_rows(tm)), lambda h, j: (0, 0))],
        out_specs=[col(0), col(0), pl.BlockSpec((None, cpb, HEAD_DIM, HEAD_DIM), lambda h, j: (h, j, 0, 0))],
        out_shape=[jax.ShapeDtypeStruct((t, HGRN_DIM), F32), jax.ShapeDtypeStruct((t, HGRN_DIM), BF16),
                   jax.ShapeDtypeStruct((HEADS, t // CHUNK, HEAD_DIM, HEAD_DIM), BF16)],
        scratch_shapes=[pltpu.VMEM((HEAD_DIM, HEAD_DIM), F32), tile_bf, tile_bf, tile_bf, tile_bf, tile_bf,
                        pltpu.VMEM((cpb, HEAD_DIM, HEAD_DIM), F32), pltpu.VMEM((max(cpb, 8), HEAD_DIM), F32),
                        pltpu.VMEM((cpb, CHUNK, CHUNK), BF16), tile_f32, tile_f32, tile_f32],
        semantics=("parallel", "arbitrary"),
        operands=(proj, proj, proj, proj, logits, norm_g, _block_tri(_tri_rows(tm))), hosted=hosted)


def _hgrn_bwd(d_og, o, proj, states, logits, norm_g, d_proj, hosted=None):
    t = proj.shape[0]
    tm = CHUNK * CHUNKS_PER_BLOCK if t % (CHUNK * CHUNKS_PER_BLOCK) == 0 else CHUNK
    cpb = tm // CHUNK
    nt = t // tm
    half = CHUNK // 2

    def body(dog_ref, o_ref, qz_ref, fz_ref, iv_ref, gz_ref, st_ref, lg_ref, ng_ref, tril_ref, triu_ref,
             _, dp_ref, sums_ref,
             dstate_ref, qe_ref, ke_ref, qb_ref, kl_ref, v_ref, do_ref, upd_ref, dst_ref, a_ref, da_ref,
             decay_ref, through_ref, q_ref, kk_ref, b_ref, dsilu_ref, gs_ref, gf_ref, sn_ref,
             eb_ref, ebr_ref, ekr_ref, ebl_ref, rev_ref, pre_ref, dk_ref):
        j = pl.program_id(1)

        @pl.when(j == 0)
        def _():
            dstate_ref[...] = jnp.zeros_like(dstate_ref)
            sums_ref[...] = jnp.zeros_like(sums_ref)

        lb = _lower_bound(lg_ref)
        ng = ng_ref[...]
        chunks = [slice(c * CHUNK, (c + 1) * CHUNK) for c in range(cpb)]
        for rows in chunks:
            qz = qz_ref[rows, :]
            sq = _sigmoid(qz)
            q_ref[rows, :] = qz * sq
            dsilu_ref[rows, :] = sq * (1.0 + qz * (1.0 - sq))
            s, sn, f = _hgrn_gates(fz_ref[rows, :], lb)
            kk_ref[rows, :] = (1.0 - lb) * sn
            b_ref[rows, :] = jnp.log(f)
            sn_ref[rows, :] = sn
            gf_ref[rows, :] = sn / f
            gs_ref[rows, :] = (1.0 - lb) * s
            v_ref[rows, :] = iv_ref[rows, :].astype(BF16)
            ov = o_ref[rows, :]
            r = lax.rsqrt(jnp.mean(ov * ov, axis=-1, keepdims=True) + RMS_EPS)
            on = ov * r
            gz = gz_ref[rows, :]
            sg = _sigmoid(gz)
            dog = dog_ref[rows, :]
            dp_ref[rows, 3 * HEAD_DIM:4 * HEAD_DIM] =(dog * (on * ng) * (sg * (1.0 + gz * (1.0 - sg)))).astype(BF16)
            d_ong = dog * (gz * sg)
            sums_ref[0:1, :] += _colsum(d_ong * on)
            d_on = d_ong * ng
            do_ref[rows, :] = (r * (d_on - on * jnp.mean(d_on * on, axis=-1, keepdims=True))).astype(BF16)
        for rows in _groups(tm):
            b_ref[rows, :] = _tri_matmul(tril_ref, b_ref[rows, :])
        for c, rows in enumerate(chunks):
            b = b_ref[rows, :]
            bref = b[half - 1:half, :]
            blast = b[CHUNK - 1:CHUNK, :]
            q = q_ref[rows, :]
            kk = kk_ref[rows, :]
            eb = jnp.exp(b)
            ebr = jnp.exp(b - bref)
            ekr = jnp.exp(bref - b)
            ebl = jnp.exp(blast - b)
            eb_ref[rows, :] = eb
            ebr_ref[rows, :] = ebr
            ekr_ref[rows, :] = ekr
            ebl_ref[rows, :] = ebl
            qb_ref[rows, :] = (q * eb).astype(BF16)
            qe_ref[rows, :] = (q * ebr).astype(BF16)
            ke_ref[rows, :] = (kk * ekr).astype(BF16)
            kl_ref[rows, :] = (kk * ebl).astype(BF16)
            decay_ref[c:c + 1, :] = jnp.exp(blast)
        causal = _tri(True)
        for c, rows in enumerate(chunks):
            upd_ref[c] = _dot(do_ref[rows, :], qb_ref[rows, :], TN)
            a_ref[c] = jnp.where(causal, _dot(qe_ref[rows, :], ke_ref[rows, :], NT), 0.0).astype(BF16)
            da_ref[c] = jnp.where(causal, _dot(do_ref[rows, :], v_ref[rows, :], NT), 0.0).astype(BF16)
        dstate = dstate_ref[...]
        for c in reversed(range(cpb)):
            dst_ref[c] = dstate.astype(BF16)
            decay = decay_ref[c:c + 1, :]
            through_ref[c:c + 1, :] = decay * _colsum(dstate * st_ref[c].astype(F32))
            dstate = dstate * decay + upd_ref[c]
        dstate_ref[...] = dstate
        for c, rows in enumerate(chunks):
            dp_ref[rows, 2 * HEAD_DIM:3 * HEAD_DIM] =(_dot(a_ref[c], do_ref[rows, :], TN)
                                + _dot(kl_ref[rows, :], dst_ref[c], NT)).astype(BF16)
        for c, rows in enumerate(chunks):
            dqe = _dot(da_ref[c], ke_ref[rows, :], NN)
            dq_inter = _dot(do_ref[rows, :], st_ref[c], NN) * eb_ref[rows, :]
            dp_ref[rows, 0:HEAD_DIM] =((dqe * ebr_ref[rows, :] + dq_inter) * dsilu_ref[rows, :]).astype(BF16)
            rev_ref[rows, :] = qe_ref[rows, :].astype(F32) * dqe + q_ref[rows, :] * dq_inter
        for c, rows in enumerate(chunks):
            dke = _dot(da_ref[c], qe_ref[rows, :], TN)
            dk_inter = _dot(v_ref[rows, :], dst_ref[c], NN) * ebl_ref[rows, :]
            dk_ref[rows, :] = dke * ekr_ref[rows, :] + dk_inter
            rev_ref[rows, :] -= ke_ref[rows, :].astype(F32) * dke
            pre_ref[rows, :] = kk_ref[rows, :] * dk_inter
        for rows in _groups(tm):
            pre = pre_ref[rows, :]
            rev_ref[rows, :] = _tri_matmul(triu_ref, rev_ref[rows, :]) + (_tri_matmul(tril_ref, pre) - pre)
        for c, rows in enumerate(chunks):
            dlf = rev_ref[rows, :] + through_ref[c:c + 1, :]
            common = gf_ref[rows, :] * dlf - sn_ref[rows, :] * dk_ref[rows, :]
            dp_ref[rows, HEAD_DIM:2 * HEAD_DIM] =(gs_ref[rows, :] * common).astype(BF16)
            sums_ref[1:2, :] += _colsum(common)

        @pl.when(j == nt - 1)
        def _():
            sums_ref[1:2, :] = sums_ref[1:2, :] * lb * (1.0 - lb)

    rev = lambda base: pl.BlockSpec((tm, HEAD_DIM), lambda h, j: (nt - 1 - j, base + h))
    vec = lambda n: pl.BlockSpec((n, HEAD_DIM), lambda h, j: (0, h))
    const = pl.BlockSpec((_tri_rows(tm), _tri_rows(tm)), lambda h, j: (0, 0))
    tile_bf = pltpu.VMEM((tm, HEAD_DIM), BF16)
    tile_f32 = pltpu.VMEM((tm, HEAD_DIM), F32)
    square = lambda dtype: pltpu.VMEM((cpb, HEAD_DIM, HEAD_DIM), dtype)
    rows8 = pltpu.VMEM((max(cpb, 8), HEAD_DIM), F32)
    operands = (d_og, o, proj, proj, proj, proj, states, logits, norm_g, _block_tri(_tri_rows(tm)),
                _block_tri(_tri_rows(tm), lower=False), d_proj)
    return _call(
        body, name="hgrn_bwd", grid=(HEADS, nt),
        in_specs=[rev(0), rev(0), rev(8), rev(16), rev(24), rev(32),
                  pl.BlockSpec((None, cpb, HEAD_DIM, HEAD_DIM), lambda h, j: (h, nt - 1 - j, 0, 0)),
                  vec(2), vec(1), const, const, ANY],
        out_specs=[pl.BlockSpec((tm, 4 * HEAD_DIM), lambda h, j: (nt - 1 - j, DP_HEAD_BLOCK + h)), vec(8)],
        out_shape=[jax.ShapeDtypeStruct(d_proj.shape, BF16), jax.ShapeDtypeStruct((8, HGRN_DIM), F32)],
        scratch_shapes=[pltpu.VMEM((HEAD_DIM, HEAD_DIM), F32)] + [tile_bf] * 6 + [square(F32), square(BF16)]
        + [pltpu.VMEM((cpb, CHUNK, CHUNK), BF16)] * 2 + [rows8, rows8] + [tile_f32] * 14,
        semantics=("parallel", "arbitrary"), operands=operands, hosted=hosted, aliases={len(operands) - 1: 0})


def _mix_ln1(proj, y_conv, y_hgrn, w_out, x, g, b, hosted=None):
    t = x.shape[0]
    tm = _pick(t, 512)
    d = D_MODEL

    def lhs(ins, outs):
        for rows in _row_blocks(tm):
            outs[0][rows, :] = (_sigmoid(ins[0][rows, :]) * ins[3][rows, :]
                                + _sigmoid(ins[2][rows, :]) * ins[4][rows, :]).astype(BF16)
        return outs[0][...]

    def epilogue(acc, ins, outs, scr):
        for rows in _row_blocks(tm):
            r = ALPHA * ins[5][rows, :] + acc[rows, :]
            outs[1][rows, :] = r
            xhat, _ = _ln(r)
            outs[2][rows, :] = (xhat * ins[6][...] + ins[7][...]).astype(BF16)

    tile = pl.BlockSpec((tm, d), lambda i, j, k: (i, 0))
    vec = pl.BlockSpec((1, d), lambda i, j, k: (0, 0))
    return _mm_fused(
        "mix_ln1", (t // tm, 1, 1), (proj, w_out, proj, y_conv, y_hgrn, x, g, b),
        [pl.BlockSpec((tm, d), lambda i, j, k: (i, 5)), pl.BlockSpec((d, d), lambda i, j, k: (0, 0)),
         pl.BlockSpec((tm, d), lambda i, j, k: (i, 6)), tile, tile, tile, vec, vec],
        [jax.ShapeDtypeStruct((t, d), BF16), jax.ShapeDtypeStruct((t, d), F32), jax.ShapeDtypeStruct((t, d), BF16)],
        [tile, tile, tile], NN, (tm, d), epilogue, lhs=lhs, hosted=hosted)


def _d_mixed_merge_bwd(d_r1b, w_out, proj, y_conv, y_hgrn, hosted=None):
    t = proj.shape[0]
    tm = _pick(t, 512)
    d = D_MODEL

    def epilogue(d_mixed, ins, outs, scr):
        dy_ref, dmz_ref = outs
        for rows in _row_blocks(tm):
            dm = d_mixed[rows, :]
            for br in range(2):
                sg = _sigmoid(ins[2 + br][rows, :])
                dy_ref[br, rows, :] = (sg * dm).astype(BF16)
                dmz_ref[rows, br * d:(br + 1) * d] = (dm * ins[4 + br][rows, :] * sg * (1.0 - sg)).astype(BF16)

    tile = pl.BlockSpec((tm, d), lambda i, j, k: (i, 0))
    return _mm_fused(
        "d_mixed_merge_bwd", (t // tm, 1, 1), (d_r1b, w_out, proj, proj, y_conv, y_hgrn),
        [tile, pl.BlockSpec((d, d), lambda i, j, k: (0, 0)), pl.BlockSpec((tm, d), lambda i, j, k: (i, 5)),
         pl.BlockSpec((tm, d), lambda i, j, k: (i, 6)), tile, tile],
        [jax.ShapeDtypeStruct((2, t, d), BF16), jax.ShapeDtypeStruct((t, IN_COLS), BF16)],
        [pl.BlockSpec((2, tm, d), lambda i, j, k: (0, i, 0)),
         pl.BlockSpec((tm, 2 * d), lambda i, j, k: (i, DP_MERGE_BLOCK))],
        NT, (tm, d), epilogue, hosted=hosted)


def _ffn_out_ln2(act, w_ffn_out, r1, target, g1, b1, g2, b2):
    t = r1.shape[0]
    tm = _pick(t, 1024)
    nt = t // tm
    d = D_MODEL

    def epilogue(y_ffn, ins, outs, scr):
        r1_ref, tg_ref, g1_ref, b1_ref, g2_ref, b2_ref = ins[2:]
        dr_ref, drb_ref, sums_ref = outs
        (sq_ref,) = scr
        i = pl.program_id(0)

        @pl.when(i == 0)
        def _():
            sums_ref[...] = jnp.zeros_like(sums_ref)
            sq_ref[...] = jnp.zeros_like(sq_ref)

        for rows in _row_blocks(tm):
            xh1, _ = _ln(r1_ref[rows, :])
            x1 = xh1 * g1_ref[...] + b1_ref[...]
            xh2, rstd2 = _ln(ALPHA * x1 + y_ffn[rows, :])
            diff = xh2 * g2_ref[...] + b2_ref[...] - tg_ref[rows, :]
            dy = diff * (1.0 / D_MODEL)
            dr = _ln_bwd(dy, xh2, rstd2, g2_ref[...])
            dr_ref[rows, :] = dr
            drb_ref[rows, :] = dr.astype(BF16)
            sums_ref[0:1, :] += _colsum(dy * xh2)
            sums_ref[1:2, :] += _colsum(dy)
            sq_ref[...] += _colsum(diff * diff)

        @pl.when(i == nt - 1)
        def _():
            total = jnp.sum(sq_ref[...], axis=-1, keepdims=True) * (0.5 / D_MODEL)
            sums_ref[2:3, :] = jnp.broadcast_to(total, (1, D_MODEL))

    tile = pl.BlockSpec((tm, d), lambda i, j, k: (i, 0))
    vec = pl.BlockSpec((1, d), lambda i, j, k: (0, 0))
    return _mm_fused(
        "ffn_out_ln2", (nt, 1, 4), (act, w_ffn_out, r1, target, g1, b1, g2, b2),
        [pl.BlockSpec((None, tm, FF_SHARD), lambda i, j, k: (k, i, 0)),
         pl.BlockSpec((None, FF_SHARD, d), lambda i, j, k: (k, 0, 0)), tile, tile, vec, vec, vec, vec],
        [jax.ShapeDtypeStruct((t, d), F32), jax.ShapeDtypeStruct((t, d), BF16), jax.ShapeDtypeStruct((8, d), F32)],
        [tile, tile, pl.BlockSpec((8, d), lambda i, j, k: (0, 0))], NN, (tm, d), epilogue,
        scratch=[pltpu.VMEM((1, d), F32)])


def _d_x1_ln1_bwd(d_z, w_ffn_in, d_r2, r1, g1):
    t = r1.shape[0]
    tm = _pick(t, 1024)
    d = D_MODEL

    def epilogue(dx_ffn, ins, outs, scr):
        dr2_ref, r1_ref, g_ref = ins[2:]
        dr1_ref, dr1b_ref, sums_ref = outs
        i = pl.program_id(0)

        @pl.when(i == 0)
        def _():
            sums_ref[...] = jnp.zeros_like(sums_ref)

        for rows in _row_blocks(tm):
            xhat, rstd = _ln(r1_ref[rows, :])
            dx1 = ALPHA * dr2_ref[rows, :] + dx_ffn[rows, :]
            dr1 = _ln_bwd(dx1, xhat, rstd, g_ref[...])
            dr1_ref[rows, :] = dr1
            dr1b_ref[rows, :] = dr1.astype(BF16)
            sums_ref[0:1, :] += _colsum(dx1 * xhat)
            sums_ref[1:2, :] += _colsum(dx1)

    tile = pl.BlockSpec((tm, d), lambda i, j, k: (i, 0))
    return _mm_fused(
        "d_x1_ln1_bwd", (t // tm, 1, N_DEV), (d_z, w_ffn_in, d_r2, r1, g1),
        [pl.BlockSpec((None, tm, FF_SHARD), lambda i, j, k: (k, i, 0)),
         pl.BlockSpec((None, d, FF_SHARD), lambda i, j, k: (k, 0, 0)), tile, tile,
         pl.BlockSpec((1, d), lambda i, j, k: (0, 0))],
        [jax.ShapeDtypeStruct((t, d), F32), jax.ShapeDtypeStruct((t, d), BF16), jax.ShapeDtypeStruct((8, d), F32)],
        [tile, tile, pl.BlockSpec((8, d), lambda i, j, k: (0, 0))], NT, (tm, d), epilogue)


def _cast_bf16(x):
    t = x.shape[0]
    tm = _pick(t, 512)

    def body(x_ref, o_ref):
        o_ref[...] = x_ref[...].astype(BF16)

    tile = pl.BlockSpec((tm, D_MODEL), lambda i: (i, 0))
    return pl.pallas_call(
        body, name="cast_x", grid=(t // tm,), in_specs=[tile], out_specs=tile,
        out_shape=jax.ShapeDtypeStruct((t, D_MODEL), BF16), compiler_params=_params(("parallel",)),
    )(x)


def _relayout(name, a, in_block, in_map, out_block, out_map, out_shape):
    def body(a_ref, o_ref):
        o_ref[...] = a_ref[...].astype(o_ref.dtype)

    return pl.pallas_call(
        body, name=name, grid=(N_DEV,), in_specs=[pl.BlockSpec(in_block, in_map)],
        out_specs=pl.BlockSpec(out_block, out_map), out_shape=out_shape, compiler_params=_params(("parallel",)),
    )(a)


_GELU_C = math.sqrt(2.0 / math.pi)


_GELU_CUBIC = 0.044715


def _gelu_parts(u):
    u2 = u * u
    th = jnp.tanh(u * (_GELU_C + (_GELU_C * _GELU_CUBIC) * u2))
    hu = 0.5 * u
    return th, hu + hu * th, u2, hu


BF16_ROWS = 16


def _ffn_act_fwd(z, w_dw, b_dw):
    t = z.shape[2]
    tm = _pick(t, 1024)
    nh = tm // FFN_HALO

    def body(z_ref, zh_ref, w_ref, b_ref, act_ref, gd_ref, us_ref):
        i = pl.program_id(1)
        us_ref[0:FFN_HALO, :] = jnp.where(i == 0, 0.0, zh_ref[...])
        us_ref[FFN_HALO:FFN_HALO + tm, :] = z_ref[0]
        for r in range(tm // ROW_BLOCK):
            base = r * ROW_BLOCK
            rows = slice(base, base + ROW_BLOCK)
            for lanes in _lane_blocks(FF_SHARD):
                uc = b_ref[:, lanes]
                for k in range(FFN_K):
                    off = base + FFN_HALO - (FFN_K - 1) + k
                    uc = uc + w_ref[k:k + 1, lanes] * us_ref[off:off + ROW_BLOCK, lanes]
                th, gelu, u2, hu = _gelu_parts(uc)
                dgelu = (0.5 + 0.5 * th) + (hu - hu * th * th) * (_GELU_C + (3.0 * _GELU_C * _GELU_CUBIC) * u2)
                act_ref[rows, lanes] = (gelu * z_ref[1, rows, lanes]).astype(BF16)
                gd_ref[0, rows, lanes] = gelu.astype(BF16)
                gd_ref[1, rows, lanes] = dgelu.astype(BF16)

    return pl.pallas_call(
        body, name="ffn_act_fwd", grid=(4, t // tm),
        in_specs=[pl.BlockSpec((2, None, tm, FF_SHARD), lambda j, i: (0, j, i, 0)),
                  pl.BlockSpec((None, None, FFN_HALO, FF_SHARD), lambda j, i: (0, j, jnp.maximum(i * nh - 1, 0), 0)),
                  pl.BlockSpec((None, FFN_K, FF_SHARD), lambda j, i: (j, 0, 0)),
                  pl.BlockSpec((None, 1, FF_SHARD), lambda j, i: (j, 0, 0))],
        out_specs=[pl.BlockSpec((None, tm, FF_SHARD), lambda j, i: (j, i, 0)),
                   pl.BlockSpec((2, None, tm, FF_SHARD), lambda j, i: (0, j, i, 0))],
        out_shape=[jax.ShapeDtypeStruct((4, t, FF_SHARD), BF16), jax.ShapeDtypeStruct((2, 4, t, FF_SHARD), BF16)],
        scratch_shapes=[pltpu.VMEM((FFN_HALO + tm, FF_SHARD), F32)],
        compiler_params=_params(("parallel", "arbitrary")),
    )(z, z, w_dw, b_dw)


def _ffn_act_bwd(d_act, z, gd, w_dw):
    t = z.shape[2]
    tm = _pick(t, 1024)
    nt = t // tm
    nh = tm // FFN_HALO
    last_h = t // FFN_HALO - 1
    pad = FFN_HALO - (FFN_K - 1)

    def fold(x):
        return functools.reduce(jnp.add, [x[r:r + SUBLANES, :] for r in range(0, x.shape[0], SUBLANES)])

    def body(da_ref, dah_ref, z_ref, zp_ref, gn_ref, gd_ref, gdn_ref, w_ref, dz_ref, sums_ref, us_ref, ds_ref,
             part_ref):
        i = pl.program_id(1)

        @pl.when(i == 0)
        def _():
            part_ref[...] = jnp.zeros_like(part_ref)

        us_ref[0:FFN_HALO, :] = jnp.where(i == 0, 0.0, zp_ref[...])
        us_ref[FFN_HALO:FFN_HALO + tm, :] = z_ref[0]
        for r in range(tm // ROW_BLOCK):
            base = r * ROW_BLOCK
            rows = slice(base, base + ROW_BLOCK)
            for lanes in _lane_blocks(FF_SHARD):
                da = da_ref[rows, lanes]
                dz_ref[1, rows, lanes] = (da * gd_ref[0, rows, lanes].astype(F32)).astype(BF16)
                duc = da * z_ref[1, rows, lanes] * gd_ref[1, rows, lanes].astype(F32)
                ds_ref[rows, lanes] = duc
                for k in range(FFN_K):
                    part_ref[k, :, lanes] += fold(duc * us_ref[base + pad + k:base + pad + k + ROW_BLOCK, lanes])
                part_ref[FFN_K, :, lanes] += fold(duc)
        duc_next = dah_ref[...] * gn_ref[...] * gdn_ref[0:FFN_HALO, :].astype(F32)
        ds_ref[tm:tm + FFN_HALO, :] = jnp.where(i == nt - 1, 0.0, duc_next)
        for r in range(tm // ROW_BLOCK):
            base = r * ROW_BLOCK
            for lanes in _lane_blocks(FF_SHARD):
                du = None
                for k in range(FFN_K):
                    off = base + FFN_K - 1 - k
                    term = w_ref[k:k + 1, lanes] * ds_ref[off:off + ROW_BLOCK, lanes]
                    du = term if du is None else du + term
                dz_ref[0, base:base + ROW_BLOCK, lanes] = du.astype(BF16)

        @pl.when(i == nt - 1)
        def _():
            sums_ref[...] = jnp.zeros_like(sums_ref)
            for k in range(FFN_K + 1):
                sums_ref[k:k + 1, :] = _colsum(part_ref[k])

    nxt = lambda i: jnp.minimum((i + 1) * nh, last_h)
    nxt_bf = lambda i: jnp.minimum((i + 1) * (tm // BF16_ROWS), t // BF16_ROWS - 1)
    return pl.pallas_call(
        body, name="ffn_act_bwd", grid=(4, nt),
        in_specs=[pl.BlockSpec((None, tm, FF_SHARD), lambda j, i: (j, i, 0)),
                  pl.BlockSpec((None, FFN_HALO, FF_SHARD), lambda j, i: (j, nxt(i), 0)),
                  pl.BlockSpec((2, None, tm, FF_SHARD), lambda j, i: (0, j, i, 0)),
                  pl.BlockSpec((None, None, FFN_HALO, FF_SHARD), lambda j, i: (0, j, jnp.maximum(i * nh - 1, 0), 0)),
                  pl.BlockSpec((None, None, FFN_HALO, FF_SHARD), lambda j, i: (1, j, nxt(i), 0)),
                  pl.BlockSpec((2, None, tm, FF_SHARD), lambda j, i: (0, j, i, 0)),
                  pl.BlockSpec((None, None, BF16_ROWS, FF_SHARD), lambda j, i: (1, j, nxt_bf(i), 0)),
                  pl.BlockSpec((None, FFN_K, FF_SHARD), lambda j, i: (j, 0, 0))],
        out_specs=[pl.BlockSpec((2, None, tm, FF_SHARD), lambda j, i: (0, j, i, 0)),
                   pl.BlockSpec((None, 8, FF_SHARD), lambda j, i: (j, 0, 0))],
        out_shape=[jax.ShapeDtypeStruct((2, 4, t, FF_SHARD), BF16), jax.ShapeDtypeStruct((4, 8, FF_SHARD), F32)],
        scratch_shapes=[pltpu.VMEM((FFN_HALO + tm, FF_SHARD), F32), pltpu.VMEM((tm + FFN_HALO, FF_SHARD), F32),
                        pltpu.VMEM((FFN_K + 1, SUBLANES, FF_SHARD), F32)],
        compiler_params=_params(("parallel", "arbitrary")),
    )(d_act, d_act, z, z, z, gd, gd, w_dw)


_HGRN_COLS = 4 * HGRN_DIM


def _to_backward_order(w):
    heads = w[:, 2 * CONV_DIM:2 * CONV_DIM + _HGRN_COLS].reshape(-1, 4, HEADS, HEAD_DIM)
    heads = jnp.swapaxes(heads, 1, 2).reshape(-1, _HGRN_COLS)
    return jnp.concatenate([w[:, 2 * CONV_DIM + _HGRN_COLS:], w[:, :2 * CONV_DIM], heads], axis=1)


def _natural_block(p):
    merge, conv = 2 * D_MODEL // HEAD_DIM, 2 * CONV_DIM // HEAD_DIM
    if p < merge:
        return conv + _HGRN_COLS // HEAD_DIM + p
    if p < merge + conv:
        return p - merge
    head, part = divmod(p - merge - conv, 4)
    return conv + part * HEADS + head


def _g_w_in(xb, d_proj):
    t, d = xb.shape
    tk = _pick(t, 2048)
    nk = t // tk
    wide = IN_COLS // 4
    per_block = wide // HEAD_DIM
    per_shard = IN_SHARD // HEAD_DIM

    def body(a_ref, b_ref, out_ref, acc_ref, stage_ref, sem):
        i, k = pl.program_id(0), pl.program_id(2)

        def drain():
            pltpu.make_async_copy(stage_ref, stage_ref, sem).wait()

        @pl.when(k == 0)
        def _():
            acc_ref[...] = jnp.zeros_like(acc_ref)

        acc_ref[...] += _dot(a_ref[...], b_ref[...], TN)

        @pl.when(k == nk - 1)
        def _():
            @pl.when(i > 0)
            def _():
                drain()

            stage_ref[...] = acc_ref[...].astype(BF16)
            for block in range(IN_COLS // wide):
                @pl.when(i == block)
                def _(block=block):
                    for j in range(per_block):
                        shard, off = divmod(_natural_block(per_block * block + j), per_shard)
                        pltpu.make_async_copy(
                            stage_ref.at[:, j * HEAD_DIM:(j + 1) * HEAD_DIM],
                            out_ref.at[shard, :, off * HEAD_DIM:(off + 1) * HEAD_DIM], sem).start()

            @pl.when(i == IN_COLS // wide - 1)
            def _():
                drain()

    return pl.pallas_call(
        body, name="g_w_in", grid=(IN_COLS // wide, 1, nk),
        in_specs=[pl.BlockSpec((tk, d), lambda i, j, k: (k, 0)), pl.BlockSpec((tk, wide), lambda i, j, k: (k, i))],
        out_specs=ANY, out_shape=jax.ShapeDtypeStruct((N_DEV, d, IN_SHARD), BF16),
        scratch_shapes=[pltpu.VMEM((d, wide), F32), pltpu.VMEM((d, wide), BF16), pltpu.SemaphoreType.DMA],
        compiler_params=_params(("arbitrary", "arbitrary", "arbitrary")),
    )(xb, d_proj)


def _local_step(x, target, weights, small, scatter=None, order=None):
    t = x.shape[0]
    tm = _pick(t, 2048)
    tk = _pick(t, 2048)
    nm = t // tm
    nk = t // tk
    d = D_MODEL

    xb = _cast_bf16(x)
    ffn_in_gather = ffn_out_gather = None
    if isinstance(weights, tuple) and isinstance(weights[0], _Hosted):
        first_gather, ffn_in_gather, ffn_out_gather = weights
        proj, w_in_bwd, gathered = _proj_gather(xb, first_gather, order)
        w_in, w_conv_out8, w_hgrn_out8, w_out8, conv_dw8, ffn_dw8 = gathered
    else:
        w_in, w_conv_out8, w_hgrn_out8, w_out8, w_ffn_in, w_ffn_out8, conv_dw8, ffn_dw8 = weights
        proj = _mm("proj", xb, w_in, (t, IN_COLS), F32, (nm, N_DEV, 1),
                   pl.BlockSpec((tm, d), lambda i, j, k: (i, 0)),
                   pl.BlockSpec((None, d, IN_SHARD), lambda i, j, k: (j, 0, 0)),
                   pl.BlockSpec((tm, IN_SHARD), lambda i, j, k: (i, j)), NN, (tm, IN_SHARD))
        w_in_bwd = _to_backward_order(jnp.transpose(w_in, (1, 0, 2)).reshape(d, IN_COLS))
    o, og, states, *late = _hgrn_fwd(proj, small["hgrn_lb_logits"], small["hgrn_norm_g"], hosted=ffn_in_gather)
    if ffn_in_gather is not None:
        (w_ffn_in,) = late
    w_conv_out = _relayout("w_conv_out_natural", w_conv_out8, (None, CONV_DIM, 128), lambda j: (j, 0, 0),
                           (CONV_DIM, 128), lambda j: (0, j), jax.ShapeDtypeStruct((CONV_DIM, d), BF16))
    w_hgrn_out = w_hgrn_out8.reshape(d, d)
    w_out = w_out8.reshape(d, d)
    conv_dw =jnp.transpose(conv_dw8[:, :CONV_K, :CONV_DIM // N_DEV], (1, 0, 2)).reshape(CONV_K, CONV_DIM)
    ffn_dw = jnp.transpose(ffn_dw8[:, :FFN_K, :D_FF // N_DEV], (1, 0, 2)).reshape(FFN_K, 4, FF_SHARD)
    small = dict(small, w_conv_dw=conv_dw, w_ffn_dw=jnp.transpose(ffn_dw, (1, 0, 2)),
                 b_ffn_dw=small["b_ffn_dw"].reshape(4, 1, FF_SHARD))

    c_act, conv_pre = _conv_fwd(proj, small["w_conv_dw"], small["b_conv_dw"], small["conv_ln_g"], small["conv_ln_b"])
    y_conv = _mm("y_conv", c_act, w_conv_out, (t, d), F32, (nm, 1, 1),
                 pl.BlockSpec((tm, CONV_DIM), lambda i, j, k: (i, 0)),
                 pl.BlockSpec((CONV_DIM, d), lambda i, j, k: (0, 0)),
                 pl.BlockSpec((tm, d), lambda i, j, k: (i, 0)), NN, (tm, d))
    sq_w = pl.BlockSpec((d, d), lambda i, j, k: (0, 0))
    row_tile = pl.BlockSpec((tm, d), lambda i, j, k: (i, 0))
    y_hgrn = _mm("y_hgrn", og, w_hgrn_out, (t, d), F32, (nm, 1, 1), row_tile, sq_w, row_tile, NN, (tm, d))
    mixed, r1, x1b, *late = _mix_ln1(proj, y_conv, y_hgrn, w_out, x, small["ln1_g"], small["ln1_b"],
                                     hosted=ffn_out_gather)
    if ffn_out_gather is not None:
        (w_ffn_out8,) = late
    w_ffn_out = w_ffn_out8.reshape(4, FF_SHARD, d)
    z = _mm("ffn_in", x1b, w_ffn_in, (N_DEV, t, FF_SHARD), F32, (nm, N_DEV, 1), row_tile,
            pl.BlockSpec((None, d, FF_SHARD), lambda i, j, k: (j, 0, 0)),
            pl.BlockSpec((None, tm, FF_SHARD), lambda i, j, k: (j, i, 0)), NN, (tm, FF_SHARD))
    z = z.reshape(2, 4, t, FF_SHARD)
    act, gelu_and_slope = _ffn_act_fwd(z, small["w_ffn_dw"], small["b_ffn_dw"])

    d_r2, d_r2b, sums_ln2 = _ffn_out_ln2(act, w_ffn_out, r1, target, small["ln1_g"], small["ln1_b"],
                                         small["ln2_g"], small["ln2_b"])
    d_act = _mm("d_act", d_r2b, w_ffn_out, (4, t, FF_SHARD), F32, (nm, 4, 1), row_tile,
                pl.BlockSpec((None, FF_SHARD, d), lambda i, j, k: (j, 0, 0)),
                pl.BlockSpec((None, tm, FF_SHARD), lambda i, j, k: (j, i, 0)), NT, (tm, FF_SHARD))
    g_w_ffn_out = _mm("g_w_ffn_out", act, d_r2b, (4, FF_SHARD, d), BF16, (4, 1, nk),
                      pl.BlockSpec((None, tk, FF_SHARD), lambda i, j, k: (i, k, 0)),
                      pl.BlockSpec((tk, d), lambda i, j, k: (k, 0)),
                      pl.BlockSpec((None, FF_SHARD, d), lambda i, j, k: (i, 0, 0)), TN, (FF_SHARD, d))
    d_z, sums_ffn = _ffn_act_bwd(d_act, z, gelu_and_slope, small["w_ffn_dw"])
    d_z8 = d_z.reshape(N_DEV, t, FF_SHARD)
    d_r1, d_r1b, sums_ln1 = _d_x1_ln1_bwd(d_z8, w_ffn_in, d_r2, r1, small["ln1_g"])
    g_w_ffn_in = _mm("g_w_ffn_in", x1b, d_z8, (N_DEV, d, FF_SHARD), BF16, (N_DEV, 1, nk),
                     pl.BlockSpec((tk, d), lambda i, j, k: (k, 0)),
                     pl.BlockSpec((None, tk, FF_SHARD), lambda i, j, k: (i, k, 0)),
                     pl.BlockSpec((None, d, FF_SHARD), lambda i, j, k: (i, 0, 0)), TN, (d, FF_SHARD))
    k_tile = pl.BlockSpec((tk, d), lambda i, j, k: (k, 0))
    g_w_out = _mm("g_w_out", mixed, d_r1b, (d, d), BF16, (1, 1, nk), k_tile, k_tile, sq_w, TN, (d, d))
    send = (lambda grads: None) if scatter is None else scatter
    g_w_ffn_out = g_w_ffn_out.reshape(N_DEV, D_FF // N_DEV, d)
    d_y, d_proj, *recv_ffn_out = _d_mixed_merge_bwd(d_r1b, w_out, proj, y_conv, y_hgrn, hosted=send([g_w_ffn_out]))
    d_pre, sums_conv = _d_c_norm_bwd(d_y, w_conv_out, conv_pre, small["conv_ln_g"], small["conv_ln_b"])
    g_w_conv_out = _mm("g_w_conv_out", c_act, d_y, (CONV_DIM, d), BF16, (1, 1, nk),
                       pl.BlockSpec((tk, CONV_DIM), lambda i, j, k: (k, 0)),
                       pl.BlockSpec((None, tk, d), lambda i, j, k: (0, k, 0)),
                       pl.BlockSpec((CONV_DIM, d), lambda i, j, k: (0, 0)), TN, (CONV_DIM, d))
    g_w_conv_out = _relayout("g_w_conv_out_shards", g_w_conv_out, (CONV_DIM, 128), lambda j: (0, j),
                             (None, CONV_DIM, 128), lambda j: (j, 0, 0),
                             jax.ShapeDtypeStruct((N_DEV, CONV_DIM, 128), BF16))
    d_og = _mm("d_og", d_y, w_hgrn_out, (t, d), F32, (nm, 1, 1),
               pl.BlockSpec((None, tm, d), lambda i, j, k: (1, i, 0)), sq_w, row_tile, NT, (tm, d))
    g_w_hgrn_out = _mm("g_w_hgrn_out", og, d_y, (d, d), BF16, (1, 1, nk), k_tile,
                       pl.BlockSpec((None, tk, d), lambda i, j, k: (1, k, 0)), sq_w, TN, (d, d))
    d_proj, g_w_conv_dw, *recv_ffn_in = _conv_bwd_dw(d_pre, proj, small["w_conv_dw"], d_proj,
                                                     hosted=send([g_w_ffn_in]))
    early = [g_w_conv_out, g_w_hgrn_out.reshape(N_DEV, d // N_DEV, d), g_w_out.reshape(N_DEV, d // N_DEV, d)]
    d_proj, sums_hgrn, *early_recv = _hgrn_bwd(d_og, o, proj, states, small["hgrn_lb_logits"], small["hgrn_norm_g"],
                                               d_proj, hosted=send(early))
    early += [g_w_ffn_in, g_w_ffn_out]
    early_recv += recv_ffn_in + recv_ffn_out
    wide = IN_COLS // 4
    g_w_in = _g_w_in(xb, d_proj)
    def add_residual(acc, ins, outs, scr):
        for rows in _row_blocks(ta):
            outs[0][rows, :] = ALPHA * ins[2][rows, :] + acc[rows, :]

    ta = _pick(t, 1024)
    n_head = max(t // ta - 1, 1)

    def grad_x_rows(name, first, count, extra, **kwargs):
        tile = pl.BlockSpec((ta, d), lambda i, j, k: (first + i, 0))
        return _mm_fused(
            name, (count, 1, 4), (d_proj, w_in_bwd, d_r1) + extra,
            [pl.BlockSpec((ta, wide), lambda i, j, k: (first + i, k)), pl.BlockSpec((d, wide), lambda i, j, k: (0, k)),
             tile] + [ANY] * len(extra),
            [jax.ShapeDtypeStruct((t, d), F32)], [tile], NT, (ta, d), add_residual, **kwargs)

    grad_x, *late_recv = grad_x_rows("grad_x", 0, n_head, (),
                                     hosted=None if scatter is None else scatter([g_w_in]))
    if n_head < t // ta:
        (grad_x,) = grad_x_rows("grad_x_tail", n_head, t // ta - n_head, (grad_x,), aliases={3: 0})

    large_grads = [g_w_in] + early
    if scatter is not None:
        large_grads = list(zip(large_grads, late_recv + early_recv))
    return grad_x, large_grads, (sums_ln2, sums_conv, sums_hgrn, sums_ln1, sums_ffn, g_w_conv_dw)


def _small_views(sums):
    sums_ln2, sums_conv, sums_hgrn, sums_ln1, sums_ffn, g_w_conv_dw = sums
    d_l0 = sums_hgrn[1:2]
    return {
        "loss": sums_ln2[2:3, 0:128],
        "b_conv_dw": sums_conv[2:3], "conv_ln_g": sums_conv[0:1], "conv_ln_b": sums_conv[1:2],
        "hgrn_lb_logits": jnp.concatenate([d_l0, -d_l0], axis=1),
        "hgrn_norm_g": sums_hgrn[0:1],
        "ln1_g": sums_ln1[0:1], "ln1_b": sums_ln1[1:2],
        "b_ffn_dw": sums_ffn[:, FFN_K, :].reshape(1, D_FF),
        "ln2_g": sums_ln2[0:1], "ln2_b": sums_ln2[1:2],
        "w_conv_dw": g_w_conv_dw[0:CONV_K].reshape(1, CONV_K * CONV_DIM),
        "w_ffn_dw": jnp.transpose(sums_ffn[:, 0:FFN_K, :], (1, 0, 2)).reshape(1, FFN_K * D_FF),
    }


def _coords():
    return lax.axis_index("x"), lax.axis_index("y"), lax.axis_index("c")


def _gather(shards, staged=False):
    n = len(shards)
    later = range(1 if staged else 0, n)

    def parts(ins, outs, sems):
        send_sems, recv_sems, local_sems = sems
        x, y, c = _coords()
        me = 4 * x + 2 * y + c
        sibling = (x, y, 1 - c)
        chips = [(1 - x, y), (x, 1 - y), (1 - x, 1 - y)]

        def copy(a, k, block, to, src=None):
            return pltpu.make_async_remote_copy(
                src_ref=outs[a].at[block] if src is None else src, dst_ref=outs[a].at[block],
                send_sem=send_sems.at[a, k], recv_sem=recv_sems.at[a, k], device_id=to, device_id_type=MESH)

        local = [pltpu.make_async_copy(ins[a], outs[a].at[me], local_sems.at[a]) for a in range(n)]
        first = []
        for a in range(n):
            first.append(copy(a, 0, me, sibling, src=ins[a]))
            for j, chip in enumerate(chips):
                first.append(copy(a, 1 + j, me, (*chip, c), src=ins[a]))
        return x, y, c, sibling, chips, copy, local, first

    def start(ins, outs, sems):
        *_, local, first = parts(ins, outs, sems)
        for cp in local + first:
            cp.start()

    def arrive(ins, outs, sems, s):
        x, y, c, sibling, chips, copy, _, _ = parts(ins, outs, sems)
        if s == 1:
            block = 4 * x + 2 * y + 1 - c
            copy(0, 0, block, sibling).wait_recv()
        elif s <= 4:
            px, py = chips[s - 2]
            block = 4 * px + 2 * py + c
            copy(0, s - 1, block, sibling).wait_recv()
            copy(0, s + 2, block, sibling).start()
        else:
            px, py = chips[s - 5]
            block = 4 * px + 2 * py + 1 - c
            copy(0, s - 1, block, sibling).wait_recv()
        return block

    def middle(ins, outs, sems):
        x, y, c, sibling, chips, copy, _, _ = parts(ins, outs, sems)
        for j, (px, py) in enumerate(chips):
            for a in later:
                copy(a, 1 + j, 4 * px + 2 * py + c, sibling).wait_recv()
                copy(a, 4 + j, 4 * px + 2 * py + c, sibling).start()

    def finish(ins, outs, sems):
        x, y, c, sibling, chips, copy, local, first = parts(ins, outs, sems)
        passed = [copy(a, 4 + j, 4 * px + 2 * py + c, sibling) for a in range(n) for j, (px, py) in enumerate(chips)]
        for a in later:
            copy(a, 0, 4 * x + 2 * y + 1 - c, sibling).wait_recv()
            for j, (px, py) in enumerate(chips):
                copy(a, 4 + j, 4 * px + 2 * py + 1 - c, sibling).wait_recv()
        for cp in first + passed:
            cp.wait_send()
        for cp in local:
            cp.wait()

    hosted = _Hosted(shards, [jax.ShapeDtypeStruct((N_DEV,) + s.shape, s.dtype) for s in shards],
                     [pltpu.SemaphoreType.DMA((n, 7)), pltpu.SemaphoreType.DMA((n, 7)), pltpu.SemaphoreType.DMA((n,))],
                     start, finish, middle)
    hosted.arrive = arrive
    return hosted


def _proj_gather(xb, gather, order):
    t, d = xb.shape
    tm = _pick(t, 2048)
    nm = t // tm
    n_in, n_out = len(gather.inputs), len(gather.out_shapes)

    blocks_per_shard = IN_SHARD // HEAD_DIM

    def body(order_ref, x_ref, *refs):
        ins, refs = refs[:n_in], refs[n_in:]
        o_ref, bwd_ref, outs, refs = refs[0], refs[1], refs[2:2 + n_out], refs[2 + n_out:]
        w_buf, w_sem, bwd_sem, sems = refs[0], refs[1], refs[2], refs[3:]
        s, i = pl.program_id(0), pl.program_id(1)

        def reorder_copies(step):
            copies = []
            for j in range(blocks_per_shard):
                n = blocks_per_shard * order_ref[step] + j
                head_part = n - 2 * CONV_DIM // HEAD_DIM
                p = jnp.where(n >= (2 * CONV_DIM + _HGRN_COLS) // HEAD_DIM, n - (2 * CONV_DIM + _HGRN_COLS) // HEAD_DIM,
                              jnp.where(n < 2 * CONV_DIM // HEAD_DIM, 2 * D_MODEL // HEAD_DIM + n,
                                        (2 * D_MODEL + 2 * CONV_DIM) // HEAD_DIM + 4 * (head_part % HEADS)
                                        + head_part // HEADS))
                copies.append(pltpu.make_async_copy(
                    w_buf.at[step % 2, :, j * HEAD_DIM:(j + 1) * HEAD_DIM],
                    bwd_ref.at[:, pl.ds(pl.multiple_of(p * HEAD_DIM, HEAD_DIM), HEAD_DIM)], bwd_sem.at[step % 2]))
            return copies

        def reorder_done(step):
            pltpu.make_async_copy(w_buf.at[step % 2], w_buf.at[step % 2], bwd_sem.at[step % 2]).wait()

        @pl.when((s == 0) & (i == 0))
        def _():
            gather.start(ins, outs, sems)

        def staging(step, src):
            return pltpu.make_async_copy(src, w_buf.at[step % 2], w_sem.at[step % 2])

        @pl.when((s == 0) & (i == 0))
        def _():
            staging(0, ins[0]).start()

        for step in range(1, N_DEV):
            @pl.when((s == step - 1) & (i == nm - 1))
            def _(step=step):
                if step >= 2:
                    reorder_done(step - 2)
                staging(step, outs[0].at[gather.arrive(ins, outs, sems, step)]).start()

        for step in range(N_DEV):
            @pl.when((s == step) & (i == 0))
            def _(step=step):
                staging(step, ins[0]).wait()
                for cp in reorder_copies(step):
                    cp.start()

        o_ref[...] = _dot(x_ref[...], w_buf[s % 2], NN)

        @pl.when((s == N_DEV - 2) & (i == 0))
        def _():
            gather.middle(ins, outs, sems)

        @pl.when((s == N_DEV - 1) & (i == nm - 1))
        def _():
            reorder_done(N_DEV - 2)
            reorder_done(N_DEV - 1)
            gather.finish(ins, outs, sems)

    outs = pl.pallas_call(
        body, name="proj_gather",
        grid_spec=pltpu.PrefetchScalarGridSpec(
            num_scalar_prefetch=1, grid=(N_DEV, nm),
            in_specs=[pl.BlockSpec((tm, d), lambda s, i, order_ref: (i, 0))] + [ANY] * n_in,
            out_specs=[pl.BlockSpec((tm, IN_SHARD), lambda s, i, order_ref: (i, order_ref[s])), ANY] + [ANY] * n_out,
            scratch_shapes=[pltpu.VMEM((2, d, IN_SHARD), BF16), pltpu.SemaphoreType.DMA((2,)),
                            pltpu.SemaphoreType.DMA((2,))] + gather.sem_shapes),
        out_shape=[jax.ShapeDtypeStruct((t, IN_COLS), F32), jax.ShapeDtypeStruct((d, IN_COLS), BF16)]
        + gather.out_shapes,
        compiler_params=pltpu.CompilerParams(dimension_semantics=("arbitrary", "arbitrary"),
                                             vmem_limit_bytes=VMEM_LIMIT, has_side_effects=True),
    )(order, xb, *gather.inputs)
    return outs[0], outs[1], list(outs[2:])


def _scatter(grads):
    n = len(grads)

    def copies(ins, outs, sems):
        send_sems, recv_sems = sems
        x, y, c = _coords()
        out = []
        for a in range(n):
            for k in range(1, N_DEV):
                px, py, pc = x ^ (k >> 2), y ^ ((k >> 1) & 1), c ^ (k & 1)
                out.append(pltpu.make_async_remote_copy(
                    src_ref=ins[a].at[4 * px + 2 * py + pc], dst_ref=outs[a].at[k - 1],
                    send_sem=send_sems.at[a, k - 1], recv_sem=recv_sems.at[a, k - 1],
                    device_id=(px, py, pc), device_id_type=MESH))
        return out

    def start(ins, outs, sems):
        for cp in copies(ins, outs, sems):
            cp.start()

    def finish(ins, outs, sems):
        for cp in copies(ins, outs, sems):
            cp.wait()

    return _Hosted(grads, [jax.ShapeDtypeStruct((N_DEV - 1,) + g.shape[1:], g.dtype) for g in grads],
                   [pltpu.SemaphoreType.DMA((n, N_DEV - 1)), pltpu.SemaphoreType.DMA((n, N_DEV - 1))], start, finish)


def _row_tile(rows):
    return 256 if rows % 256 == 0 else rows


def _adam_math(w, g, m, v):
    m_new = ADAM_B1 * m + (1.0 - ADAM_B1) * g
    v_new = ADAM_B2 * v + (1.0 - ADAM_B2) * (g * g)
    m_hat = m_new / (1.0 - ADAM_B1 ** ADAM_STEP)
    v_hat = v_new / (1.0 - ADAM_B2 ** ADAM_STEP)
    delta = -ADAM_LR * (m_hat / (jnp.sqrt(v_hat) + ADAM_EPS) + ADAM_WD * w)
    return delta, m_new, v_new


def _adam_large(name, own, recv, me, w, m, v):
    rows, cols = w.shape
    tr = _row_tile(rows)

    def body(me_ref, p_ref, r_ref, w_ref, m_ref, v_ref, g_out, d_out, m_out, v_out):
        g = p_ref[...].astype(F32)
        for k in range(N_DEV - 1):
            g = g + r_ref[k].astype(F32)
        delta, m_new, v_new = _adam_math(w_ref[...], g, m_ref[...], v_ref[...])
        g_out[...] = g
        d_out[...] = delta
        m_out[...] = m_new
        v_out[...] = v_new

    tile = pl.BlockSpec((tr, cols), lambda r, me_ref: (r, 0))
    sds = jax.ShapeDtypeStruct((rows, cols), F32)
    return pl.pallas_call(
        body, name=name,
        grid_spec=pltpu.PrefetchScalarGridSpec(
            num_scalar_prefetch=1, grid=(rows // tr,),
            in_specs=[pl.BlockSpec((None, tr, cols), lambda r, me_ref: (me_ref[0], r, 0)),
                      pl.BlockSpec((N_DEV - 1, tr, cols), lambda r, me_ref: (0, r, 0)), tile, tile, tile],
            out_specs=[tile, tile, tile, tile]),
        out_shape=[sds, sds, sds, sds],
        compiler_params=_params(("parallel",)),
    )(me, own, recv, w, m, v)


def _small_allreduce(arrays):
    n = len(arrays)

    def body(*refs):
        ins, outs, gats = refs[:n], refs[n:2 * n], refs[2 * n:3 * n]
        send_sems, recv_sems = refs[3 * n:]
        x, y, c = _coords()
        me = 4 * x + 2 * y + c
        peers = [(x ^ (k >> 2), y ^ ((k >> 1) & 1), c ^ (k & 1)) for k in range(1, N_DEV)]

        def copy(a, k, slot):
            return pltpu.make_async_remote_copy(
                src_ref=ins[a], dst_ref=gats[a].at[slot], send_sem=send_sems.at[a, k], recv_sem=recv_sems.at[a, k],
                device_id=peers[k], device_id_type=MESH)

        sends = [copy(a, k, me) for a in range(n) for k in range(N_DEV - 1)]
        for a in range(n):
            gats[a][me] = ins[a][...]
        for cp in sends:
            cp.start()
        for a in range(n):
            for k, (px, py, pc) in enumerate(peers):
                copy(a, k, 4 * px + 2 * py + pc).wait_recv()
        for cp in sends:
            cp.wait_send()
        for a in range(n):
            acc = gats[a][0]
            for dev in range(1, N_DEV):
                acc = acc + gats[a][dev]
            outs[a][...] = acc

    whole = pl.BlockSpec(memory_space=pltpu.VMEM)
    return pl.pallas_call(
        body, name="small_allreduce", in_specs=[whole] * n, out_specs=[whole] * n,
        out_shape=[jax.ShapeDtypeStruct(a.shape, F32) for a in arrays],
        scratch_shapes=[pltpu.VMEM((N_DEV,) + a.shape, F32) for a in arrays]
        + [pltpu.SemaphoreType.DMA((n, N_DEV - 1)), pltpu.SemaphoreType.DMA((n, N_DEV - 1))],
        compiler_params=pltpu.CompilerParams(has_side_effects=True, vmem_limit_bytes=VMEM_LIMIT),
    )(*arrays)


def _adam_replicated(sums, w, m, v):
    rows_of = {"conv_ln_g": (1, 0), "conv_ln_b": (1, 1), "b_conv_dw": (1, 2), "hgrn_norm_g": (2, 0),
               "ln1_g": (3, 0), "ln1_b": (3, 1), "ln2_g": (0, 0), "ln2_b": (0, 1)}
    names = list(rows_of) + ["hgrn_lb_logits"]
    n = len(names)

    def body(*refs):
        sum_refs, refs = refs[:4], refs[4:]
        w_refs, m_refs, v_refs, outs = refs[:n], refs[n:2 * n], refs[2 * n:3 * n], refs[3 * n:]
        for j, name in enumerate(names):
            if name == "hgrn_lb_logits":
                d_l0 = sum_refs[2][1:2, :]
                grads = [d_l0, -d_l0]
            else:
                a, row = rows_of[name]
                grads = [sum_refs[a][row:row + 1, :]]
            g_out, d_out, m_out, v_out = outs[4 * j:4 * j + 4]
            for r, g in enumerate(grads):
                rows = slice(r, r + 1)
                delta, m_new, v_new = _adam_math(w_refs[j][rows, :], g, m_refs[j][rows, :], v_refs[j][rows, :])
                g_out[rows, :] = g
                d_out[rows, :] = delta
                m_out[rows, :] = m_new
                v_out[rows, :] = v_new

    whole = pl.BlockSpec(memory_space=pltpu.VMEM)
    operands = list(sums) + [w[k] for k in names] + [m[k] for k in names] + [v[k] for k in names]
    outs = pl.pallas_call(
        body, name="adam_replicated", in_specs=[whole] * len(operands), out_specs=[whole] * (4 * n),
        out_shape=[jax.ShapeDtypeStruct(w[k].shape, F32) for k in names for _ in range(4)],
    )(*operands)
    return {name: tuple(outs[4 * j:4 * j + 4]) for j, name in enumerate(names)}


def _adam_small(w, g, m, v):
    def body(w_ref, g_ref, m_ref, v_ref, d_out, m_out, v_out):
        delta, m_new, v_new = _adam_math(w_ref[...], g_ref[...], m_ref[...], v_ref[...])
        d_out[...] = delta
        m_out[...] = m_new
        v_out[...] = v_new

    whole = pl.BlockSpec(memory_space=pltpu.VMEM)
    sds = jax.ShapeDtypeStruct(w.shape, F32)
    return pl.pallas_call(body, name="adam_small", in_specs=[whole] * 4, out_specs=[whole] * 3,
                          out_shape=[sds, sds, sds])(w, g, m, v)


_WEIGHTS = ["w_in", "w_conv_dw", "b_conv_dw", "conv_ln_g", "conv_ln_b", "w_conv_out", "hgrn_lb_logits", "hgrn_norm_g",
            "w_hgrn_out", "w_out", "ln1_g", "ln1_b", "w_ffn_in", "w_ffn_dw", "b_ffn_dw", "w_ffn_out", "ln2_g", "ln2_b"]
_LARGE = ["w_in", "w_conv_out", "w_hgrn_out", "w_out", "w_ffn_in", "w_ffn_out"]
_CONV_DW_SHARD = CONV_DIM // N_DEV
_FFN_DW_SHARD = D_FF // N_DEV


def kernel(x, w_in, w_conv_dw, b_conv_dw, conv_ln_g, conv_ln_b, w_conv_out, hgrn_lb_logits, hgrn_norm_g, w_hgrn_out, w_out, ln1_g, ln1_b, w_ffn_in, w_ffn_dw, b_ffn_dw, w_ffn_out, ln2_g, ln2_b, loss_target, m_w_in, m_w_conv_dw, m_b_conv_dw, m_conv_ln_g, m_conv_ln_b, m_w_conv_out, m_hgrn_lb_logits, m_hgrn_norm_g, m_w_hgrn_out, m_w_out, m_ln1_g, m_ln1_b, m_w_ffn_in, m_w_ffn_dw, m_b_ffn_dw, m_w_ffn_out, m_ln2_g, m_ln2_b, v_w_in, v_w_conv_dw, v_b_conv_dw, v_conv_ln_g, v_conv_ln_b, v_w_conv_out, v_hgrn_lb_logits, v_hgrn_norm_g, v_w_hgrn_out, v_w_out, v_ln1_g, v_ln1_b, v_w_ffn_in, v_w_ffn_dw, v_b_ffn_dw, v_w_ffn_out, v_ln2_g, v_ln2_b):
    w = dict(w_in=w_in, w_conv_dw=w_conv_dw, b_conv_dw=b_conv_dw, conv_ln_g=conv_ln_g, conv_ln_b=conv_ln_b,
             w_conv_out=w_conv_out, hgrn_lb_logits=hgrn_lb_logits, hgrn_norm_g=hgrn_norm_g, w_hgrn_out=w_hgrn_out,
             w_out=w_out, ln1_g=ln1_g, ln1_b=ln1_b, w_ffn_in=w_ffn_in, w_ffn_dw=w_ffn_dw, b_ffn_dw=b_ffn_dw,
             w_ffn_out=w_ffn_out, ln2_g=ln2_g, ln2_b=ln2_b)
    m = dict(w_in=m_w_in, w_conv_dw=m_w_conv_dw, b_conv_dw=m_b_conv_dw, conv_ln_g=m_conv_ln_g, conv_ln_b=m_conv_ln_b,
             w_conv_out=m_w_conv_out, hgrn_lb_logits=m_hgrn_lb_logits, hgrn_norm_g=m_hgrn_norm_g,
             w_hgrn_out=m_w_hgrn_out, w_out=m_w_out, ln1_g=m_ln1_g, ln1_b=m_ln1_b, w_ffn_in=m_w_ffn_in,
             w_ffn_dw=m_w_ffn_dw, b_ffn_dw=m_b_ffn_dw, w_ffn_out=m_w_ffn_out, ln2_g=m_ln2_g, ln2_b=m_ln2_b)
    v = dict(w_in=v_w_in, w_conv_dw=v_w_conv_dw, b_conv_dw=v_b_conv_dw, conv_ln_g=v_conv_ln_g, conv_ln_b=v_conv_ln_b,
             w_conv_out=v_w_conv_out, hgrn_lb_logits=v_hgrn_lb_logits, hgrn_norm_g=v_hgrn_norm_g,
             w_hgrn_out=v_w_hgrn_out, w_out=v_w_out, ln1_g=v_ln1_g, ln1_b=v_ln1_b, w_ffn_in=v_w_ffn_in,
             w_ffn_dw=v_w_ffn_dw, b_ffn_dw=v_b_ffn_dw, w_ffn_out=v_w_ffn_out, ln2_g=v_ln2_g, ln2_b=v_ln2_b)
    xi, yi, ci = lax.axis_index("x"), lax.axis_index("y"), lax.axis_index("c")
    me = 4 * xi + 2 * yi + ci
    me_op = jnp.reshape(me, (1,)).astype(jnp.int32)

    shards = [w[name][0].astype(BF16) for name in _LARGE]
    shards.append(jnp.pad(w_conv_dw[0], ((0, 1), (0, 128 - _CONV_DW_SHARD))))
    shards.append(jnp.pad(w_ffn_dw[0], ((0, 8 - FFN_K), (0, 384 - _FFN_DW_SHARD))))
    chips = [(1 - xi, yi), (xi, 1 - yi), (1 - xi, 1 - yi)]
    order = jnp.stack([me, me ^ 1] + [4 * px + 2 * py + ci for px, py in chips]
                      + [4 * px + 2 * py + 1 - ci for px, py in chips]).astype(jnp.int32)
    small = dict(b_conv_dw=b_conv_dw, conv_ln_g=conv_ln_g, conv_ln_b=conv_ln_b, hgrn_lb_logits=hgrn_lb_logits,
                 hgrn_norm_g=hgrn_norm_g, ln1_g=ln1_g, ln1_b=ln1_b, ln2_g=ln2_g, ln2_b=ln2_b, b_ffn_dw=b_ffn_dw)

    gathers = (_gather(shards[:4] + shards[6:], staged=True), _gather(shards[4:5]), _gather(shards[5:6]))
    grad_x, large_grads, small_sums = _local_step(x[0], loss_target[0], gathers, small, _scatter, order)

    out = {}
    for name, (own, recv) in zip(_LARGE, large_grads):
        out[name] = _adam_large("adam_" + name, own, recv, me_op, w[name][0], m[name][0], v[name][0])

    totals = _small_allreduce(list(small_sums))
    out.update(_adam_replicated(totals[:4], w, m, v))
    summed = _small_views(totals)
    loss = summed["loss"][0, 0]
    conv_dw_g = lax.dynamic_slice_in_dim(summed["w_conv_dw"].reshape(CONV_K, CONV_DIM), me * _CONV_DW_SHARD, _CONV_DW_SHARD, axis=1)
    ffn_dw_g = lax.dynamic_slice_in_dim(summed["w_ffn_dw"].reshape(FFN_K, D_FF), me * _FFN_DW_SHARD, _FFN_DW_SHARD, axis=1)
    small_g = dict(b_ffn_dw=summed["b_ffn_dw"], w_conv_dw=conv_dw_g.reshape(1, -1), w_ffn_dw=ffn_dw_g.reshape(1, -1))
    names = list(small_g)
    flat = lambda d, n: d[n].reshape(1, -1)
    n_small = sum(small_g[n].shape[1] for n in names)
    pad = (-n_small) % 1024
    pack = lambda pieces: jnp.pad(jnp.concatenate(pieces, axis=1), ((0, 0), (0, pad))).reshape(-1, 128)
    d_s, m_s, v_s = _adam_small(pack([flat(w, n) for n in names]), pack([small_g[n] for n in names]),
                                pack([flat(m, n) for n in names]), pack([flat(v, n) for n in names]))
    pos = 0
    for n in names:
        size = small_g[n].shape[1]
        cut = lambda a: a.reshape(1, -1)[:, pos:pos + size].reshape(w[n].shape)
        out[n] = (small_g[n].reshape(w[n].shape), cut(d_s), cut(m_s), cut(v_s))
        pos += size

    for name in _LARGE:
        out[name] = tuple(a.reshape(w[name].shape) for a in out[name])
    grads = [out[n][0] for n in _WEIGHTS]
    deltas = [out[n][1] for n in _WEIGHTS]
    new_m = [out[n][2] for n in _WEIGHTS]
    new_v = [out[n][3] for n in _WEIGHTS]
    return (loss, grad_x[None], *grads, *deltas, *new_m, *new_v)
```

```python
import functools
import math

import jax
import jax.numpy as jnp
from jax import lax
from jax.experimental import pallas as pl
from jax.experimental.pallas import tpu as pltpu

F32 = jnp.float32
BF16 = jnp.bfloat16

N_DEV = 8
D_MODEL = 1024
CONV_DIM = 512
CONV_K = 31
HGRN_DIM = 1024
HEADS = 8
HEAD_DIM = 128
D_FF = 2816
FFN_K = 3
FF_SHARD = 2 * D_FF // N_DEV
IN_COLS = 7168
IN_SHARD = IN_COLS // N_DEV
LN_EPS = 1e-5
RMS_EPS = 1e-6
ALPHA = 2.0 ** 0.25

ADAM_LR = 0.001
ADAM_B1 = 0.9
ADAM_B2 = 0.999
ADAM_EPS = 1e-08
ADAM_WD = 0.01
ADAM_STEP = 10

CHUNK = 64
CHUNKS_PER_BLOCK = 32
CONV_HALO = 32
FFN_HALO = 8
ROW_BLOCK = 64
SUBLANES = 8
VMEM_LIMIT = 48 * 1024 * 1024
MXU_DEPTH = 256

DP_MERGE_BLOCK = 0
DP_CONV_BLOCK = 2
DP_HEAD_BLOCK = 6

MESH = pl.DeviceIdType.MESH
ANY = pl.BlockSpec(memory_space=pl.ANY)

NN = (((1,), (0,)), ((), ()))
NT = (((1,), (1,)), ((), ()))
TN = (((0,), (0,)), ((), ()))


def _params(sem):
    return pltpu.CompilerParams(dimension_semantics=sem, vmem_limit_bytes=VMEM_LIMIT)


def _dot(a, b, dims):
    return lax.dot_general(a.astype(BF16), b.astype(BF16), dims, preferred_element_type=F32)


def _sigmoid(x):
    return jax.nn.sigmoid(x)


def _ln(r):
    mu = jnp.mean(r, axis=-1, keepdims=True)
    xc = r - mu
    var = jnp.mean(xc * xc, axis=-1, keepdims=True)
    rstd = lax.rsqrt(var + LN_EPS)
    return xc * rstd, rstd


def _ln_bwd(dy, xhat, rstd, g):
    dxh = dy * g
    m1 = jnp.mean(dxh, axis=-1, keepdims=True)
    m2 = jnp.mean(dxh * xhat, axis=-1, keepdims=True)
    return rstd * (dxh - m1 - xhat * m2)


def _colsum(x):
    return jnp.sum(x, axis=0, keepdims=True)


class _Hosted:
    def __init__(self, inputs, out_shapes, sem_shapes, start, finish, middle=None):
        self.inputs, self.out_shapes, self.sem_shapes = list(inputs), list(out_shapes), list(sem_shapes)
        self.start, self.finish, self.middle = start, finish, middle


def _call(body, *, name, grid, in_specs, out_specs, out_shape, scratch_shapes, semantics, operands, hosted=None,
          aliases=None):
    aliases = aliases or {}
    if hosted is None:
        return pl.pallas_call(
            body, name=name, grid=grid, in_specs=list(in_specs), out_specs=list(out_specs), out_shape=list(out_shape),
            scratch_shapes=list(scratch_shapes), input_output_aliases=aliases,
            compiler_params=_params(semantics))(*operands)
    n_in, n_out, n_scr = len(in_specs), len(out_specs), len(scratch_shapes)
    h_in, h_out = len(hosted.inputs), len(hosted.out_shapes)

    def full_body(*refs):
        ins, refs = refs[:n_in], refs[n_in:]
        h_ins, refs = refs[:h_in], refs[h_in:]
        outs, refs = refs[:n_out], refs[n_out:]
        h_outs, refs = refs[:h_out], refs[h_out:]
        scr, sems = refs[:n_scr], refs[n_scr:]
        first = functools.reduce(jnp.logical_and, [pl.program_id(d) == 0 for d in range(len(grid))])
        last = functools.reduce(jnp.logical_and, [pl.program_id(d) == grid[d] - 1 for d in range(len(grid))])

        @pl.when(first)
        def _():
            hosted.start(h_ins, h_outs, sems)

        body(*ins, *outs, *scr)

        if hosted.middle is not None:
            step, total = 0, 1
            for d in range(len(grid)):
                step, total = step * grid[d] + pl.program_id(d), total * grid[d]

            @pl.when(step == (2 * total) // 3)
            def _():
                hosted.middle(h_ins, h_outs, sems)

        @pl.when(last)
        def _():
            hosted.finish(h_ins, h_outs, sems)

    return pl.pallas_call(
        full_body, name=name, grid=grid, in_specs=list(in_specs) + [ANY] * h_in,
        out_specs=list(out_specs) + [ANY] * h_out, out_shape=list(out_shape) + hosted.out_shapes,
        scratch_shapes=list(scratch_shapes) + hosted.sem_shapes, input_output_aliases=aliases,
        compiler_params=pltpu.CompilerParams(dimension_semantics=("arbitrary",) * len(grid),
                                             vmem_limit_bytes=VMEM_LIMIT, has_side_effects=True),
    )(*operands, *hosted.inputs)


def _mm(name, a, b, out_shape, out_dtype, grid, a_spec, b_spec, o_spec, dims, acc_shape, hosted=None):
    nk = grid[2]
    if nk == 1:
        def body(a_ref, b_ref, o_ref):
            o_ref[...] = _dot(a_ref[...], b_ref[...], dims).astype(o_ref.dtype)
        scratch = []
    else:
        def body(a_ref, b_ref, o_ref, acc_ref):
            k = pl.program_id(2)

            @pl.when(k == 0)
            def _():
                acc_ref[...] = jnp.zeros_like(acc_ref)

            acc_ref[...] += _dot(a_ref[...], b_ref[...], dims)

            @pl.when(k == nk - 1)
            def _():
                o_ref[...] = acc_ref[...].astype(o_ref.dtype)
        scratch = [pltpu.VMEM(acc_shape, F32)]

    outs = _call(body, name=name, grid=grid, in_specs=[a_spec, b_spec], out_specs=[o_spec],
                 out_shape=[jax.ShapeDtypeStruct(out_shape, out_dtype)], scratch_shapes=scratch,
                 semantics=("parallel", "parallel", "arbitrary"), operands=(a, b), hosted=hosted)
    return outs[0] if hosted is None else (outs[0], list(outs[1:]))


def _mm_fused(name, grid, operands, in_specs, out_shape, out_specs, dims, acc_shape, epilogue, lhs=None, scratch=(),
              hosted=None):
    nk = grid[2]
    n_in, n_out = len(in_specs), len(out_specs)

    def body(*refs):
        ins, outs, scr = refs[:n_in], refs[n_in:n_in + n_out], refs[n_in + n_out:]
        acc_ref, k = scr[0], pl.program_id(2)
        a = ins[0][...] if lhs is None else lhs(ins, outs)
        part = _dot(a, ins[1][...], dims)
        if nk == 1:
            acc_ref[...] = part
            epilogue(acc_ref, ins, outs, scr[1:])
            return

        @pl.when(k == 0)
        def _():
            acc_ref[...] = jnp.zeros_like(acc_ref)

        acc_ref[...] += part

        @pl.when(k == nk - 1)
        def _():
            epilogue(acc_ref, ins, outs, scr[1:])

    return _call(body, name=name, grid=grid, in_specs=in_specs, out_specs=out_specs, out_shape=out_shape,
                 scratch_shapes=[pltpu.VMEM(acc_shape, F32)] + list(scratch), semantics=("arbitrary",) * 3,
                 operands=operands, hosted=hosted)


def _row_blocks(rows, block=256):
    block = block if rows % block == 0 else rows
    return [slice(r, r + block) for r in range(0, rows, block)]


def _pick(t, pref):
    return pref if t % pref == 0 else t


def _glu(p):
    return p[:, :CONV_DIM] * _sigmoid(p[:, CONV_DIM:])


def _by_phase(taps):
    phases = {}
    for off, payload in taps:
        phases.setdefault(off % SUBLANES, []).append((off - off % SUBLANES, payload))
    return sorted(phases.items())


def _tap_sum(src_ref, base, taps, rows, lanes):
    acc = None
    for phase, items in _by_phase(taps):
        n = rows if phase == 0 else rows + SUBLANES
        part = None
        for off, (w_ref, k) in items:
            term = w_ref[k:k + 1, lanes] * src_ref[base + off:base + off + n, lanes]
            part = term if part is None else part + term
        if phase:
            part = part[phase:phase + rows, :]
        acc = part if acc is None else acc + part
    return acc


def _tap_products(x, src_ref, base, taps, lanes):
    rows, cols = x.shape
    pad = jnp.zeros((SUBLANES, cols), x.dtype)
    padded = jnp.concatenate([pad, x, pad], axis=0)
    out = []
    for phase, items in _by_phase(taps):
        n = rows if phase == 0 else rows + SUBLANES
        shifted = x if phase == 0 else padded[SUBLANES - phase:SUBLANES - phase + n, :]
        for off, key in items:
            out.append((key, _colsum(shifted * src_ref[base + off:base + off + n, lanes])))
    return out


def _lane_blocks(cols, block=256):
    return [slice(c, min(c + block, cols)) for c in range(0, cols, block)]


def _conv_fwd(proj, w_dw, b_dw, g, b):
    t = proj.shape[0]
    tm = _pick(t, 512)
    nh = tm // CONV_HALO

    def body(p_ref, ph_ref, w_ref, bd_ref, g_ref, b_ref, act_ref, pre_ref, xs_ref):
        i = pl.program_id(0)
        halo = _glu(ph_ref[...])
        xs_ref[0:CONV_HALO, :] = jnp.where(i == 0, 0.0, halo)
        xs_ref[CONV_HALO:CONV_HALO + tm, :] = _glu(p_ref[...])
        taps = [(CONV_HALO - (CONV_K - 1) + k, (w_ref, k)) for k in range(CONV_K)]
        for r in range(tm // ROW_BLOCK):
            rows = slice(r * ROW_BLOCK, (r + 1) * ROW_BLOCK)
            for lanes in _lane_blocks(CONV_DIM):
                pre_ref[rows, lanes] = bd_ref[:, lanes] + _tap_sum(xs_ref, r * ROW_BLOCK, taps, ROW_BLOCK, lanes)
            acc = pre_ref[rows, :]
            xhat, _ = _ln(acc)
            yln = xhat * g_ref[...] + b_ref[...]
            act_ref[rows, :] = (yln * _sigmoid(yln)).astype(BF16)

    full = lambda s: pl.BlockSpec(s, lambda i: (0, 0))
    return pl.pallas_call(
        body, name="conv_fwd", grid=(t // tm,),
        in_specs=[pl.BlockSpec((tm, 2 * CONV_DIM), lambda i: (i, 0)),
                  pl.BlockSpec((CONV_HALO, 2 * CONV_DIM), lambda i: (jnp.maximum(i * nh - 1, 0), 0)),
                  full((CONV_K, CONV_DIM)), full((1, CONV_DIM)), full((1, CONV_DIM)), full((1, CONV_DIM))],
        out_specs=[pl.BlockSpec((tm, CONV_DIM), lambda i: (i, 0)), pl.BlockSpec((tm, CONV_DIM), lambda i: (i, 0))],
        out_shape=[jax.ShapeDtypeStruct((t, CONV_DIM), BF16), jax.ShapeDtypeStruct((t, CONV_DIM), F32)],
        scratch_shapes=[pltpu.VMEM((CONV_HALO + tm, CONV_DIM), F32)],
        compiler_params=_params(("arbitrary",)),
    )(proj, proj, w_dw, b_dw, g, b)


def _d_c_norm_bwd(d_y, w_conv_out, pre, g, b):
    t = pre.shape[0]
    tm = _pick(t, 512)
    d = D_MODEL

    def epilogue(d_c, ins, outs, scr):
        pre_ref, g_ref, b_ref = ins[2:]
        dpre_ref, sums_ref = outs
        i = pl.program_id(0)

        @pl.when(i == 0)
        def _():
            sums_ref[...] = jnp.zeros_like(sums_ref)

        for rows in _row_blocks(tm):
            xhat, rstd = _ln(pre_ref[rows, :])
            yln = xhat * g_ref[...] + b_ref[...]
            sg = _sigmoid(yln)
            dyln = d_c[rows, :] * (sg * (1.0 + yln * (1.0 - sg)))
            dpre = _ln_bwd(dyln, xhat, rstd, g_ref[...])
            dpre_ref[rows, :] = dpre
            sums_ref[0:1, :] += _colsum(dyln * xhat)
            sums_ref[1:2, :] += _colsum(dyln)
            sums_ref[2:3, :] += _colsum(dpre)

    full = lambda s: pl.BlockSpec(s, lambda i, j, k: (0, 0))
    tile = pl.BlockSpec((tm, CONV_DIM), lambda i, j, k: (i, 0))
    return _mm_fused(
        "d_c_norm_bwd", (t // tm, 1, 1), (d_y, w_conv_out, pre, g, b),
        [pl.BlockSpec((None, tm, d), lambda i, j, k: (0, i, 0)), full((CONV_DIM, d)), tile,
         full((1, CONV_DIM)), full((1, CONV_DIM))],
        [jax.ShapeDtypeStruct((t, CONV_DIM), F32), jax.ShapeDtypeStruct((8, CONV_DIM), F32)],
        [tile, full((8, CONV_DIM))], NT, (tm, CONV_DIM), epilogue)


def _conv_bwd_dw(d_pre, proj, w_dw, d_proj, hosted=None):
    t = d_pre.shape[0]
    tm = _pick(t, 512)
    nt = t // tm
    nh = tm // CONV_HALO
    last_h = t // CONV_HALO - 1

    def body(dp_ref, dph_ref, p_ref, ph_ref, w_ref, _, dproj_ref, dw_ref, xs_ref, ds_ref):
        i = pl.program_id(0)

        @pl.when(i == 0)
        def _():
            dw_ref[...] = jnp.zeros_like(dw_ref)

        halo = _glu(ph_ref[...])
        xs_ref[0:CONV_HALO, :] = jnp.where(i == 0, 0.0, halo)
        xs_ref[CONV_HALO:CONV_HALO + tm, :] = _glu(p_ref[...])
        ds_ref[0:tm, :] = dp_ref[...]
        ds_ref[tm:tm + CONV_HALO, :] = jnp.where(i == nt - 1, 0.0, dph_ref[...])
        back_taps = [(CONV_K - 1 - k, (w_ref, k)) for k in range(CONV_K)]
        grad_taps = [(CONV_HALO - (CONV_K - 1) + k, k) for k in range(CONV_K)]
        for r in range(tm // ROW_BLOCK):
            base = r * ROW_BLOCK
            rows = slice(base, base + ROW_BLOCK)
            for lanes in _lane_blocks(CONV_DIM):
                gate_lanes = slice(CONV_DIM + lanes.start, CONV_DIM + lanes.stop)
                acc = _tap_sum(ds_ref, base, back_taps, ROW_BLOCK, lanes)
                for k, total in _tap_products(ds_ref[rows, lanes], xs_ref, base, grad_taps, lanes):
                    dw_ref[k:k + 1, lanes] += total
                cval = p_ref[rows, lanes]
                sg = _sigmoid(p_ref[rows, gate_lanes])
                dproj_ref[rows, lanes] = (acc * sg).astype(BF16)
                dproj_ref[rows, gate_lanes] = (acc * cval * sg * (1.0 - sg)).astype(BF16)

    full = lambda s: pl.BlockSpec(s, lambda i: (0, 0))
    return _call(
        body, name="conv_bwd_dw", grid=(nt,),
        in_specs=[pl.BlockSpec((tm, CONV_DIM), lambda i: (i, 0)),
                  pl.BlockSpec((CONV_HALO, CONV_DIM), lambda i: (jnp.minimum((i + 1) * nh, last_h), 0)),
                  pl.BlockSpec((tm, 2 * CONV_DIM), lambda i: (i, 0)),
                  pl.BlockSpec((CONV_HALO, 2 * CONV_DIM), lambda i: (jnp.maximum(i * nh - 1, 0), 0)),
                  full((CONV_K, CONV_DIM)), ANY],
        out_specs=[pl.BlockSpec((tm, 2 * CONV_DIM), lambda i: (i, DP_CONV_BLOCK)), full((CONV_HALO, CONV_DIM))],
        out_shape=[jax.ShapeDtypeStruct(d_proj.shape, BF16), jax.ShapeDtypeStruct((CONV_HALO, CONV_DIM), F32)],
        scratch_shapes=[pltpu.VMEM((CONV_HALO + tm, CONV_DIM), F32), pltpu.VMEM((tm + CONV_HALO, CONV_DIM), F32)],
        semantics=("arbitrary",), operands=(d_pre, d_pre, proj, proj, w_dw, d_proj), hosted=hosted, aliases={5: 0})


def _lower_bound(logit_ref):
    l0 = logit_ref[0:1, :]
    l1 = logit_ref[1:2, :]
    m = jnp.maximum(l0, l1)
    e0 = jnp.exp(l0 - m)
    e1 = jnp.exp(l1 - m)
    return e0 / (e0 + e1)


def _tri(lower):
    r = lax.broadcasted_iota(jnp.int32, (CHUNK, CHUNK), 0)
    c = lax.broadcasted_iota(jnp.int32, (CHUNK, CHUNK), 1)
    return (c <= r) if lower else (c >= r)


def _hgrn_gates(fz, lb):
    s = _sigmoid(fz)
    sn = _sigmoid(-fz)
    f = lb + (1.0 - lb) * s
    return s, sn, f


def _block_tri(rows, lower=True):
    r = lax.broadcasted_iota(jnp.int32, (rows, rows), 0)
    c = lax.broadcasted_iota(jnp.int32, (rows, rows), 1)
    tri = (c <= r) if lower else (c >= r)
    return (tri & (r // CHUNK == c // CHUNK)).astype(BF16)


def _tri_rows(tm):
    return min(tm, MXU_DEPTH)


def _tri_matmul(tri_ref, x):
    hi = x.astype(BF16)
    lo = (x - hi.astype(F32)).astype(BF16)
    tri = tri_ref[...]
    return (lax.dot_general(tri, hi, NN, preferred_element_type=F32)
            + lax.dot_general(tri, lo, NN, preferred_element_type=F32))


def _groups(tm):
    g = _tri_rows(tm)
    return [slice(i * g, (i + 1) * g) for i in range(tm // g)]


def _hgrn_fwd(proj, logits, norm_g, hosted=None):
    t = proj.shape[0]
    tm = CHUNK * CHUNKS_PER_BLOCK if t % (CHUNK * CHUNKS_PER_BLOCK) == 0 else CHUNK
    cpb = tm // CHUNK
    nt = t // tm
    half = CHUNK // 2

    def body(qz_ref, fz_ref, iv_ref, gz_ref, lg_ref, ng_ref, tri_ref, o_ref, og_ref, st_ref,
             state_ref, qe_ref, ke_ref, qb_ref, kl_ref, v_ref, upd_ref, decay_ref, a_ref, q_ref, kk_ref, b_ref):
        j = pl.program_id(1)

        @pl.when(j == 0)
        def _():
            state_ref[...] = jnp.zeros_like(state_ref)

        lb = _lower_bound(lg_ref)
        chunks = [slice(c * CHUNK, (c + 1) * CHUNK) for c in range(cpb)]
        for rows in chunks:
            qz = qz_ref[rows, :]
            q_ref[rows, :] = qz * _sigmoid(qz)
            _, sn, f = _hgrn_gates(fz_ref[rows, :], lb)
            kk_ref[rows, :] = (1.0 - lb) * sn
            b_ref[rows, :] = jnp.log(f)
            v_ref[rows, :] = iv_ref[rows, :].astype(BF16)
        for rows in _groups(tm):
            b_ref[rows, :] = _tri_matmul(tri_ref, b_ref[rows, :])
        for c, rows in enumerate(chunks):
            b = b_ref[rows, :]
            bref = b[half - 1:half, :]
            blast = b[CHUNK - 1:CHUNK, :]
            q = q_ref[rows, :]
            kk = kk_ref[rows, :]
            qb_ref[rows, :] = (q * jnp.exp(b)).astype(BF16)
            qe_ref[rows, :] = (q * jnp.exp(b - bref)).astype(BF16)
            ke_ref[rows, :] = (kk * jnp.exp(bref - b)).astype(BF16)
            kl_ref[rows, :] = (kk * jnp.exp(blast - b)).astype(BF16)
            decay_ref[c:c + 1, :] = jnp.exp(blast)
        causal = _tri(True)
        for c, rows in enumerate(chunks):
            upd_ref[c] = _dot(v_ref[rows, :], kl_ref[rows, :], TN)
            a_ref[c] = jnp.where(causal, _dot(qe_ref[rows, :], ke_ref[rows, :], NT), 0.0).astype(BF16)
        state = state_ref[...]
        for c in range(cpb):
            st_ref[c] = state.astype(BF16)
            state = state * decay_ref[c:c + 1, :] + upd_ref[c]
        state_ref[...] = state
        for c, rows in enumerate(chunks):
            o_ref[rows, :] = _dot(a_ref[c], v_ref[rows, :], NN) + _dot(qb_ref[rows, :], st_ref[c], NT)
        for rows in chunks:
            o = o_ref[rows, :]
            r = lax.rsqrt(jnp.mean(o * o, axis=-1, keepdims=True) + RMS_EPS)
            gz = gz_ref[rows, :]
            og_ref[rows, :] = ((o * r * ng_ref[...]) * (gz * _sigmoid(gz))).astype(BF16)

    col = lambda base: pl.BlockSpec((tm, HEAD_DIM), lambda h, j: (j, base + h))
    tile_bf = pltpu.VMEM((tm, HEAD_DIM), BF16)
    tile_f32 = pltpu.VMEM((tm, HEAD_DIM), F32)
    return _call(
        body, name="hgrn_fwd", grid=(HEADS, nt),
        in_specs=[col(8), col(16), col(24), col(32),
                  pl.BlockSpec((2, HEAD_DIM), lambda h, j: (0, h)), pl.BlockSpec((1, HEAD_DIM), lambda h, j: (0, h)),
                  pl.BlockSpec((_tri_rows(tm), _tri_rows(tm)), lambda h, j: (0, 0))],
        out_specs=[col(0), col(0), pl.BlockSpec((None, cpb, HEAD_DIM, HEAD_DIM), lambda h, j: (h, j, 0, 0))],
        out_shape=[jax.ShapeDtypeStruct((t, HGRN_DIM), F32), jax.ShapeDtypeStruct((t, HGRN_DIM), BF16),
                   jax.ShapeDtypeStruct((HEADS, t // CHUNK, HEAD_DIM, HEAD_DIM), BF16)],
        scratch_shapes=[pltpu.VMEM((HEAD_DIM, HEAD_DIM), F32), tile_bf, tile_bf, tile_bf, tile_bf, tile_bf,
                        pltpu.VMEM((cpb, HEAD_DIM, HEAD_DIM), F32), pltpu.VMEM((max(cpb, 8), HEAD_DIM), F32),
                        pltpu.VMEM((cpb, CHUNK, CHUNK), BF16), tile_f32, tile_f32, tile_f32],
        semantics=("parallel", "arbitrary"),
        operands=(proj, proj, proj, proj, logits, norm_g, _block_tri(_tri_rows(tm))), hosted=hosted)


def _hgrn_bwd(d_og, o, proj, states, logits, norm_g, d_proj, hosted=None):
    t = proj.shape[0]
    tm = CHUNK * CHUNKS_PER_BLOCK if t % (CHUNK * CHUNKS_PER_BLOCK) == 0 else CHUNK
    cpb = tm // CHUNK
    nt = t // tm
    half = CHUNK // 2

    def body(dog_ref, o_ref, qz_ref, fz_ref, iv_ref, gz_ref, st_ref, lg_ref, ng_ref, tril_ref, triu_ref,
             _, dp_ref, sums_ref,
             dstate_ref, qe_ref, ke_ref, qb_ref, kl_ref, v_ref, do_ref, upd_ref, dst_ref, a_ref, da_ref,
             decay_ref, through_ref, q_ref, kk_ref, b_ref, dsilu_ref, gs_ref, gf_ref, sn_ref,
             eb_ref, ebr_ref, ekr_ref, ebl_ref, rev_ref, pre_ref, dk_ref):
        j = pl.program_id(1)

        @pl.when(j == 0)
        def _():
            dstate_ref[...] = jnp.zeros_like(dstate_ref)
            sums_ref[...] = jnp.zeros_like(sums_ref)

        lb = _lower_bound(lg_ref)
        ng = ng_ref[...]
        chunks = [slice(c * CHUNK, (c + 1) * CHUNK) for c in range(cpb)]
        for rows in chunks:
            qz = qz_ref[rows, :]
            sq = _sigmoid(qz)
            q_ref[rows, :] = qz * sq
            dsilu_ref[rows, :] = sq * (1.0 + qz * (1.0 - sq))
            s, sn, f = _hgrn_gates(fz_ref[rows, :], lb)
            kk_ref[rows, :] = (1.0 - lb) * sn
            b_ref[rows, :] = jnp.log(f)
            sn_ref[rows, :] = sn
            gf_ref[rows, :] = sn / f
            gs_ref[rows, :] = (1.0 - lb) * s
            v_ref[rows, :] = iv_ref[rows, :].astype(BF16)
            ov = o_ref[rows, :]
            r = lax.rsqrt(jnp.mean(ov * ov, axis=-1, keepdims=True) + RMS_EPS)
            on = ov * r
            gz = gz_ref[rows, :]
            sg = _sigmoid(gz)
            dog = dog_ref[rows, :]
            dp_ref[rows, 3 * HEAD_DIM:4 * HEAD_DIM] =(dog * (on * ng) * (sg * (1.0 + gz * (1.0 - sg)))).astype(BF16)
            d_ong = dog * (gz * sg)
            sums_ref[0:1, :] += _colsum(d_ong * on)
            d_on = d_ong * ng
            do_ref[rows, :] = (r * (d_on - on * jnp.mean(d_on * on, axis=-1, keepdims=True))).astype(BF16)
        for rows in _groups(tm):
            b_ref[rows, :] = _tri_matmul(tril_ref, b_ref[rows, :])
        for c, rows in enumerate(chunks):
            b = b_ref[rows, :]
            bref = b[half - 1:half, :]
            blast = b[CHUNK - 1:CHUNK, :]
            q = q_ref[rows, :]
            kk = kk_ref[rows, :]
            eb = jnp.exp(b)
            ebr = jnp.exp(b - bref)
            ekr = jnp.exp(bref - b)
            ebl = jnp.exp(blast - b)
            eb_ref[rows, :] = eb
            ebr_ref[rows, :] = ebr
            ekr_ref[rows, :] = ekr
            ebl_ref[rows, :] = ebl
            qb_ref[rows, :] = (q * eb).astype(BF16)
            qe_ref[rows, :] = (q * ebr).astype(BF16)
            ke_ref[rows, :] = (kk * ekr).astype(BF16)
            kl_ref[rows, :] = (kk * ebl).astype(BF16)
            decay_ref[c:c + 1, :] = jnp.exp(blast)
        causal = _tri(True)
        for c, rows in enumerate(chunks):
            upd_ref[c] = _dot(do_ref[rows, :], qb_ref[rows, :], TN)
            a_ref[c] = jnp.where(causal, _dot(qe_ref[rows, :], ke_ref[rows, :], NT), 0.0).astype(BF16)
            da_ref[c] = jnp.where(causal, _dot(do_ref[rows, :], v_ref[rows, :], NT), 0.0).astype(BF16)
        dstate = dstate_ref[...]
        for c in reversed(range(cpb)):
            dst_ref[c] = dstate.astype(BF16)
            decay = decay_ref[c:c + 1, :]
            through_ref[c:c + 1, :] = decay * _colsum(dstate * st_ref[c].astype(F32))
            dstate = dstate * decay + upd_ref[c]
        dstate_ref[...] = dstate
        for c, rows in enumerate(chunks):
            dp_ref[rows, 2 * HEAD_DIM:3 * HEAD_DIM] =(_dot(a_ref[c], do_ref[rows, :], TN)
                                + _dot(kl_ref[rows, :], dst_ref[c], NT)).astype(BF16)
        for c, rows in enumerate(chunks):
            dqe = _dot(da_ref[c], ke_ref[rows, :], NN)
            dq_inter = _dot(do_ref[rows, :], st_ref[c], NN) * eb_ref[rows, :]
            dp_ref[rows, 0:HEAD_DIM] =((dqe * ebr_ref[rows, :] + dq_inter) * dsilu_ref[rows, :]).astype(BF16)
            rev_ref[rows, :] = qe_ref[rows, :].astype(F32) * dqe + q_ref[rows, :] * dq_inter
        for c, rows in enumerate(chunks):
            dke = _dot(da_ref[c], qe_ref[rows, :], TN)
            dk_inter = _dot(v_ref[rows, :], dst_ref[c], NN) * ebl_ref[rows, :]
            dk_ref[rows, :] = dke * ekr_ref[rows, :] + dk_inter
            rev_ref[rows, :] -= ke_ref[rows, :].astype(F32) * dke
            pre_ref[rows, :] = kk_ref[rows, :] * dk_inter
        for rows in _groups(tm):
            pre = pre_ref[rows, :]
            rev_ref[rows, :] = _tri_matmul(triu_ref, rev_ref[rows, :]) + (_tri_matmul(tril_ref, pre) - pre)
        for c, rows in enumerate(chunks):
            dlf = rev_ref[rows, :] + through_ref[c:c + 1, :]
            common = gf_ref[rows, :] * dlf - sn_ref[rows, :] * dk_ref[rows, :]
            dp_ref[rows, HEAD_DIM:2 * HEAD_DIM] =(gs_ref[rows, :] * common).astype(BF16)
            sums_ref[1:2, :] += _colsum(common)

        @pl.when(j == nt - 1)
        def _():
            sums_ref[1:2, :] = sums_ref[1:2, :] * lb * (1.0 - lb)

    rev = lambda base: pl.BlockSpec((tm, HEAD_DIM), lambda h, j: (nt - 1 - j, base + h))
    vec = lambda n: pl.BlockSpec((n, HEAD_DIM), lambda h, j: (0, h))
    const = pl.BlockSpec((_tri_rows(tm), _tri_rows(tm)), lambda h, j: (0, 0))
    tile_bf = pltpu.VMEM((tm, HEAD_DIM), BF16)
    tile_f32 = pltpu.VMEM((tm, HEAD_DIM), F32)
    square = lambda dtype: pltpu.VMEM((cpb, HEAD_DIM, HEAD_DIM), dtype)
    rows8 = pltpu.VMEM((max(cpb, 8), HEAD_DIM), F32)
    operands = (d_og, o, proj, proj, proj, proj, states, logits, norm_g, _block_tri(_tri_rows(tm)),
                _block_tri(_tri_rows(tm), lower=False), d_proj)
    return _call(
        body, name="hgrn_bwd", grid=(HEADS, nt),
        in_specs=[rev(0), rev(0), rev(8), rev(16), rev(24), rev(32),
                  pl.BlockSpec((None, cpb, HEAD_DIM, HEAD_DIM), lambda h, j: (h, nt - 1 - j, 0, 0)),
                  vec(2), vec(1), const, const, ANY],
        out_specs=[pl.BlockSpec((tm, 4 * HEAD_DIM), lambda h, j: (nt - 1 - j, DP_HEAD_BLOCK + h)), vec(8)],
        out_shape=[jax.ShapeDtypeStruct(d_proj.shape, BF16), jax.ShapeDtypeStruct((8, HGRN_DIM), F32)],
        scratch_shapes=[pltpu.VMEM((HEAD_DIM, HEAD_DIM), F32)] + [tile_bf] * 6 + [square(F32), square(BF16)]
        + [pltpu.VMEM((cpb, CHUNK, CHUNK), BF16)] * 2 + [rows8, rows8] + [tile_f32] * 14,
        semantics=("parallel", "arbitrary"), operands=operands, hosted=hosted, aliases={len(operands) - 1: 0})


def _mix_ln1(proj, y_conv, y_hgrn, w_out, x, g, b, hosted=None):
    t = x.shape[0]
    tm = _pick(t, 512)
    d = D_MODEL

    def lhs(ins, outs):
        for rows in _row_blocks(tm):
            outs[0][rows, :] = (_sigmoid(ins[0][rows, :]) * ins[3][rows, :]
                                + _sigmoid(ins[2][rows, :]) * ins[4][rows, :]).astype(BF16)
        return outs[0][...]

    def epilogue(acc, ins, outs, scr):
        for rows in _row_blocks(tm):
            r = ALPHA * ins[5][rows, :] + acc[rows, :]
            outs[1][rows, :] = r
            xhat, _ = _ln(r)
            outs[2][rows, :] = (xhat * ins[6][...] + ins[7][...]).astype(BF16)

    tile = pl.BlockSpec((tm, d), lambda i, j, k: (i, 0))
    vec = pl.BlockSpec((1, d), lambda i, j, k: (0, 0))
    return _mm_fused(
        "mix_ln1", (t // tm, 1, 1), (proj, w_out, proj, y_conv, y_hgrn, x, g, b),
        [pl.BlockSpec((tm, d), lambda i, j, k: (i, 5)), pl.BlockSpec((d, d), lambda i, j, k: (0, 0)),
         pl.BlockSpec((tm, d), lambda i, j, k: (i, 6)), tile, tile, tile, vec, vec],
        [jax.ShapeDtypeStruct((t, d), BF16), jax.ShapeDtypeStruct((t, d), F32), jax.ShapeDtypeStruct((t, d), BF16)],
        [tile, tile, tile], NN, (tm, d), epilogue, lhs=lhs, hosted=hosted)


def _d_mixed_merge_bwd(d_r1b, w_out, proj, y_conv, y_hgrn, hosted=None):
    t = proj.shape[0]
    tm = _pick(t, 512)
    d = D_MODEL

    def epilogue(d_mixed, ins, outs, scr):
        dy_ref, dmz_ref = outs
        for rows in _row_blocks(tm):
            dm = d_mixed[rows, :]
            for br in range(2):
                sg = _sigmoid(ins[2 + br][rows, :])
                dy_ref[br, rows, :] = (sg * dm).astype(BF16)
                dmz_ref[rows, br * d:(br + 1) * d] = (dm * ins[4 + br][rows, :] * sg * (1.0 - sg)).astype(BF16)

    tile = pl.BlockSpec((tm, d), lambda i, j, k: (i, 0))
    return _mm_fused(
        "d_mixed_merge_bwd", (t // tm, 1, 1), (d_r1b, w_out, proj, proj, y_conv, y_hgrn),
        [tile, pl.BlockSpec((d, d), lambda i, j, k: (0, 0)), pl.BlockSpec((tm, d), lambda i, j, k: (i, 5)),
         pl.BlockSpec((tm, d), lambda i, j, k: (i, 6)), tile, tile],
        [jax.ShapeDtypeStruct((2, t, d), BF16), jax.ShapeDtypeStruct((t, IN_COLS), BF16)],
        [pl.BlockSpec((2, tm, d), lambda i, j, k: (0, i, 0)),
         pl.BlockSpec((tm, 2 * d), lambda i, j, k: (i, DP_MERGE_BLOCK))],
        NT, (tm, d), epilogue, hosted=hosted)


def _ffn_out_ln2(act, w_ffn_out, r1, target, g1, b1, g2, b2):
    t = r1.shape[0]
    tm = _pick(t, 1024)
    nt = t // tm
    d = D_MODEL

    def epilogue(y_ffn, ins, outs, scr):
        r1_ref, tg_ref, g1_ref, b1_ref, g2_ref, b2_ref = ins[2:]
        dr_ref, drb_ref, sums_ref = outs
        (sq_ref,) = scr
        i = pl.program_id(0)

        @pl.when(i == 0)
        def _():
            sums_ref[...] = jnp.zeros_like(sums_ref)
            sq_ref[...] = jnp.zeros_like(sq_ref)

        for rows in _row_blocks(tm):
            xh1, _ = _ln(r1_ref[rows, :])
            x1 = xh1 * g1_ref[...] + b1_ref[...]
            xh2, rstd2 = _ln(ALPHA * x1 + y_ffn[rows, :])
            diff = xh2 * g2_ref[...] + b2_ref[...] - tg_ref[rows, :]
            dy = diff * (1.0 / D_MODEL)
            dr = _ln_bwd(dy, xh2, rstd2, g2_ref[...])
            dr_ref[rows, :] = dr
            drb_ref[rows, :] = dr.astype(BF16)
            sums_ref[0:1, :] += _colsum(dy * xh2)
            sums_ref[1:2, :] += _colsum(dy)
            sq_ref[...] += _colsum(diff * diff)

        @pl.when(i == nt - 1)
        def _():
            total = jnp.sum(sq_ref[...], axis=-1, keepdims=True) * (0.5 / D_MODEL)
            sums_ref[2:3, :] = jnp.broadcast_to(total, (1, D_MODEL))

    tile = pl.BlockSpec((tm, d), lambda i, j, k: (i, 0))
    vec = pl.BlockSpec((1, d), lambda i, j, k: (0, 0))
    return _mm_fused(
        "ffn_out_ln2", (nt, 1, 4), (act, w_ffn_out, r1, target, g1, b1, g2, b2),
        [pl.BlockSpec((None, tm, FF_SHARD), lambda i, j, k: (k, i, 0)),
         pl.BlockSpec((None, FF_SHARD, d), lambda i, j, k: (k, 0, 0)), tile, tile, vec, vec, vec, vec],
        [jax.ShapeDtypeStruct((t, d), F32), jax.ShapeDtypeStruct((t, d), BF16), jax.ShapeDtypeStruct((8, d), F32)],
        [tile, tile, pl.BlockSpec((8, d), lambda i, j, k: (0, 0))], NN, (tm, d), epilogue,
        scratch=[pltpu.VMEM((1, d), F32)])


def _d_x1_ln1_bwd(d_z, w_ffn_in, d_r2, r1, g1):
    t = r1.shape[0]
    tm = _pick(t, 1024)
    d = D_MODEL

    def epilogue(dx_ffn, ins, outs, scr):
        dr2_ref, r1_ref, g_ref = ins[2:]
        dr1_ref, dr1b_ref, sums_ref = outs
        i = pl.program_id(0)

        @pl.when(i == 0)
        def _():
            sums_ref[...] = jnp.zeros_like(sums_ref)

        for rows in _row_blocks(tm):
            xhat, rstd = _ln(r1_ref[rows, :])
            dx1 = ALPHA * dr2_ref[rows, :] + dx_ffn[rows, :]
            dr1 = _ln_bwd(dx1, xhat, rstd, g_ref[...])
            dr1_ref[rows, :] = dr1
            dr1b_ref[rows, :] = dr1.astype(BF16)
            sums_ref[0:1, :] += _colsum(dx1 * xhat)
            sums_ref[1:2, :] += _colsum(dx1)

    tile = pl.BlockSpec((tm, d), lambda i, j, k: (i, 0))
    return _mm_fused(
        "d_x1_ln1_bwd", (t // tm, 1, N_DEV), (d_z, w_ffn_in, d_r2, r1, g1),
        [pl.BlockSpec((None, tm, FF_SHARD), lambda i, j, k: (k, i, 0)),
         pl.BlockSpec((None, d, FF_SHARD), lambda i, j, k: (k, 0, 0)), tile, tile,
         pl.BlockSpec((1, d), lambda i, j, k: (0, 0))],
        [jax.ShapeDtypeStruct((t, d), F32), jax.ShapeDtypeStruct((t, d), BF16), jax.ShapeDtypeStruct((8, d), F32)],
        [tile, tile, pl.BlockSpec((8, d), lambda i, j, k: (0, 0))], NT, (tm, d), epilogue)


def _cast_bf16(x):
    t = x.shape[0]
    tm = _pick(t, 512)

    def body(x_ref, o_ref):
        o_ref[...] = x_ref[...].astype(BF16)

    tile = pl.BlockSpec((tm, D_MODEL), lambda i: (i, 0))
    return pl.pallas_call(
        body, name="cast_x", grid=(t // tm,), in_specs=[tile], out_specs=tile,
        out_shape=jax.ShapeDtypeStruct((t, D_MODEL), BF16), compiler_params=_params(("parallel",)),
    )(x)


def _relayout(name, a, in_block, in_map, out_block, out_map, out_shape):
    def body(a_ref, o_ref):
        o_ref[...] = a_ref[...].astype(o_ref.dtype)

    return pl.pallas_call(
        body, name=name, grid=(N_DEV,), in_specs=[pl.BlockSpec(in_block, in_map)],
        out_specs=pl.BlockSpec(out_block, out_map), out_shape=out_shape, compiler_params=_params(("parallel",)),
    )(a)


_GELU_C = math.sqrt(2.0 / math.pi)


_GELU_CUBIC = 0.044715


def _gelu_parts(u):
    u2 = u * u
    th = jnp.tanh(u * (_GELU_C + (_GELU_C * _GELU_CUBIC) * u2))
    hu = 0.5 * u
    return th, hu + hu * th, u2, hu


BF16_ROWS = 16


def _ffn_act_fwd(z, w_dw, b_dw):
    t = z.shape[2]
    tm = _pick(t, 1024)
    nh = tm // FFN_HALO

    def body(z_ref, zh_ref, w_ref, b_ref, act_ref, gd_ref, us_ref):
        i = pl.program_id(1)
        us_ref[0:FFN_HALO, :] = jnp.where(i == 0, 0.0, zh_ref[...])
        us_ref[FFN_HALO:FFN_HALO + tm, :] = z_ref[0]
        for r in range(tm // ROW_BLOCK):
            base = r * ROW_BLOCK
            rows = slice(base, base + ROW_BLOCK)
            for lanes in _lane_blocks(FF_SHARD):
                uc = b_ref[:, lanes]
                for k in range(FFN_K):
                    off = base + FFN_HALO - (FFN_K - 1) + k
                    uc = uc + w_ref[k:k + 1, lanes] * us_ref[off:off + ROW_BLOCK, lanes]
                th, gelu, u2, hu = _gelu_parts(uc)
                dgelu = (0.5 + 0.5 * th) + (hu - hu * th * th) * (_GELU_C + (3.0 * _GELU_C * _GELU_CUBIC) * u2)
                act_ref[rows, lanes] = (gelu * z_ref[1, rows, lanes]).astype(BF16)
                gd_ref[0, rows, lanes] = gelu.astype(BF16)
                gd_ref[1, rows, lanes] = dgelu.astype(BF16)

    return pl.pallas_call(
        body, name="ffn_act_fwd", grid=(4, t // tm),
        in_specs=[pl.BlockSpec((2, None, tm, FF_SHARD), lambda j, i: (0, j, i, 0)),
                  pl.BlockSpec((None, None, FFN_HALO, FF_SHARD), lambda j, i: (0, j, jnp.maximum(i * nh - 1, 0), 0)),
                  pl.BlockSpec((None, FFN_K, FF_SHARD), lambda j, i: (j, 0, 0)),
                  pl.BlockSpec((None, 1, FF_SHARD), lambda j, i: (j, 0, 0))],
        out_specs=[pl.BlockSpec((None, tm, FF_SHARD), lambda j, i: (j, i, 0)),
                   pl.BlockSpec((2, None, tm, FF_SHARD), lambda j, i: (0, j, i, 0))],
        out_shape=[jax.ShapeDtypeStruct((4, t, FF_SHARD), BF16), jax.ShapeDtypeStruct((2, 4, t, FF_SHARD), BF16)],
        scratch_shapes=[pltpu.VMEM((FFN_HALO + tm, FF_SHARD), F32)],
        compiler_params=_params(("parallel", "arbitrary")),
    )(z, z, w_dw, b_dw)


def _ffn_act_bwd(d_act, z, gd, w_dw):
    t = z.shape[2]
    tm = _pick(t, 1024)
    nt = t // tm
    nh = tm // FFN_HALO
    last_h = t // FFN_HALO - 1
    pad = FFN_HALO - (FFN_K - 1)

    def fold(x):
        return functools.reduce(jnp.add, [x[r:r + SUBLANES, :] for r in range(0, x.shape[0], SUBLANES)])

    def body(da_ref, dah_ref, z_ref, zp_ref, gn_ref, gd_ref, gdn_ref, w_ref, dz_ref, sums_ref, us_ref, ds_ref,
             part_ref):
        i = pl.program_id(1)

        @pl.when(i == 0)
        def _():
            part_ref[...] = jnp.zeros_like(part_ref)

        us_ref[0:FFN_HALO, :] = jnp.where(i == 0, 0.0, zp_ref[...])
        us_ref[FFN_HALO:FFN_HALO + tm, :] = z_ref[0]
        for r in range(tm // ROW_BLOCK):
            base = r * ROW_BLOCK
            rows = slice(base, base + ROW_BLOCK)
            for lanes in _lane_blocks(FF_SHARD):
                da = da_ref[rows, lanes].astype(F32)
                dz_ref[1, rows, lanes] = (da * gd_ref[0, rows, lanes].astype(F32)).astype(BF16)
                duc = da * z_ref[1, rows, lanes] * gd_ref[1, rows, lanes].astype(F32)
                ds_ref[rows, lanes] = duc
                for k in range(FFN_K):
                    part_ref[k, :, lanes] += fold(duc * us_ref[base + pad + k:base + pad + k + ROW_BLOCK, lanes])
                part_ref[FFN_K, :, lanes] += fold(duc)
        duc_next = dah_ref[0:FFN_HALO, :].astype(F32) * gn_ref[...] * gdn_ref[0:FFN_HALO, :].astype(F32)
        ds_ref[tm:tm + FFN_HALO, :] = jnp.where(i == nt - 1, 0.0, duc_next)
        for r in range(tm // ROW_BLOCK):
            base = r * ROW_BLOCK
            for lanes in _lane_blocks(FF_SHARD):
                du = None
                for k in range(FFN_K):
                    off = base + FFN_K - 1 - k
                    term = w_ref[k:k + 1, lanes] * ds_ref[off:off + ROW_BLOCK, lanes]
                    du = term if du is None else du + term
                dz_ref[0, base:base + ROW_BLOCK, lanes] = du.astype(BF16)

        @pl.when(i == nt - 1)
        def _():
            sums_ref[...] = jnp.zeros_like(sums_ref)
            for k in range(FFN_K + 1):
                sums_ref[k:k + 1, :] = _colsum(part_ref[k])

    nxt = lambda i: jnp.minimum((i + 1) * nh, last_h)
    nxt_bf = lambda i: jnp.minimum((i + 1) * (tm // BF16_ROWS), t // BF16_ROWS - 1)
    return pl.pallas_call(
        body, name="ffn_act_bwd", grid=(4, nt),
        in_specs=[pl.BlockSpec((None, tm, FF_SHARD), lambda j, i: (j, i, 0)),
                  pl.BlockSpec((None, BF16_ROWS, FF_SHARD), lambda j, i: (j, nxt_bf(i), 0)),
                  pl.BlockSpec((2, None, tm, FF_SHARD), lambda j, i: (0, j, i, 0)),
                  pl.BlockSpec((None, None, FFN_HALO, FF_SHARD), lambda j, i: (0, j, jnp.maximum(i * nh - 1, 0), 0)),
                  pl.BlockSpec((None, None, FFN_HALO, FF_SHARD), lambda j, i: (1, j, nxt(i), 0)),
                  pl.BlockSpec((2, None, tm, FF_SHARD), lambda j, i: (0, j, i, 0)),
                  pl.BlockSpec((None, None, BF16_ROWS, FF_SHARD), lambda j, i: (1, j, nxt_bf(i), 0)),
                  pl.BlockSpec((None, FFN_K, FF_SHARD), lambda j, i: (j, 0, 0))],
        out_specs=[pl.BlockSpec((2, None, tm, FF_SHARD), lambda j, i: (0, j, i, 0)),
                   pl.BlockSpec((None, 8, FF_SHARD), lambda j, i: (j, 0, 0))],
        out_shape=[jax.ShapeDtypeStruct((2, 4, t, FF_SHARD), BF16), jax.ShapeDtypeStruct((4, 8, FF_SHARD), F32)],
        scratch_shapes=[pltpu.VMEM((FFN_HALO + tm, FF_SHARD), F32), pltpu.VMEM((tm + FFN_HALO, FF_SHARD), F32),
                        pltpu.VMEM((FFN_K + 1, SUBLANES, FF_SHARD), F32)],
        compiler_params=_params(("parallel", "arbitrary")),
    )(d_act, d_act, z, z, z, gd, gd, w_dw)


_HGRN_COLS = 4 * HGRN_DIM


def _to_backward_order(w):
    heads = w[:, 2 * CONV_DIM:2 * CONV_DIM + _HGRN_COLS].reshape(-1, 4, HEADS, HEAD_DIM)
    heads = jnp.swapaxes(heads, 1, 2).reshape(-1, _HGRN_COLS)
    return jnp.concatenate([w[:, 2 * CONV_DIM + _HGRN_COLS:], w[:, :2 * CONV_DIM], heads], axis=1)


def _natural_block(p):
    merge, conv = 2 * D_MODEL // HEAD_DIM, 2 * CONV_DIM // HEAD_DIM
    if p < merge:
        return conv + _HGRN_COLS // HEAD_DIM + p
    if p < merge + conv:
        return p - merge
    head, part = divmod(p - merge - conv, 4)
    return conv + part * HEADS + head


def _g_w_in(xb, d_proj):
    t, d = xb.shape
    tk = _pick(t, 2048)
    nk = t // tk
    wide = IN_COLS // 4
    per_block = wide // HEAD_DIM
    per_shard = IN_SHARD // HEAD_DIM

    def body(a_ref, b_ref, out_ref, acc_ref, stage_ref, sem):
        i, k = pl.program_id(0), pl.program_id(2)

        def drain():
            pltpu.make_async_copy(stage_ref, stage_ref, sem).wait()

        @pl.when(k == 0)
        def _():
            acc_ref[...] = jnp.zeros_like(acc_ref)

        acc_ref[...] += _dot(a_ref[...], b_ref[...], TN)

        @pl.when(k == nk - 1)
        def _():
            @pl.when(i > 0)
            def _():
                drain()

            stage_ref[...] = acc_ref[...].astype(BF16)
            for block in range(IN_COLS // wide):
                @pl.when(i == block)
                def _(block=block):
                    for j in range(per_block):
                        shard, off = divmod(_natural_block(per_block * block + j), per_shard)
                        pltpu.make_async_copy(
                            stage_ref.at[:, j * HEAD_DIM:(j + 1) * HEAD_DIM],
                            out_ref.at[shard, :, off * HEAD_DIM:(off + 1) * HEAD_DIM], sem).start()

            @pl.when(i == IN_COLS // wide - 1)
            def _():
                drain()

    return pl.pallas_call(
        body, name="g_w_in", grid=(IN_COLS // wide, 1, nk),
        in_specs=[pl.BlockSpec((tk, d), lambda i, j, k: (k, 0)), pl.BlockSpec((tk, wide), lambda i, j, k: (k, i))],
        out_specs=ANY, out_shape=jax.ShapeDtypeStruct((N_DEV, d, IN_SHARD), BF16),
        scratch_shapes=[pltpu.VMEM((d, wide), F32), pltpu.VMEM((d, wide), BF16), pltpu.SemaphoreType.DMA],
        compiler_params=_params(("arbitrary", "arbitrary", "arbitrary")),
    )(xb, d_proj)


def _local_step(x, target, weights, small, scatter=None, order=None):
    t = x.shape[0]
    tm = _pick(t, 2048)
    tk = _pick(t, 2048)
    nm = t // tm
    nk = t // tk
    d = D_MODEL

    xb = _cast_bf16(x)
    ffn_in_gather = ffn_out_gather = None
    if isinstance(weights, tuple) and isinstance(weights[0], _Hosted):
        first_gather, ffn_in_gather, ffn_out_gather = weights
        proj, w_in_bwd, gathered = _proj_gather(xb, first_gather, order)
        w_in, w_conv_out8, w_hgrn_out8, w_out8, conv_dw8, ffn_dw8 = gathered
    else:
        w_in, w_conv_out8, w_hgrn_out8, w_out8, w_ffn_in, w_ffn_out8, conv_dw8, ffn_dw8 = weights
        proj = _mm("proj", xb, w_in, (t, IN_COLS), F32, (nm, N_DEV, 1),
                   pl.BlockSpec((tm, d), lambda i, j, k: (i, 0)),
                   pl.BlockSpec((None, d, IN_SHARD), lambda i, j, k: (j, 0, 0)),
                   pl.BlockSpec((tm, IN_SHARD), lambda i, j, k: (i, j)), NN, (tm, IN_SHARD))
        w_in_bwd = _to_backward_order(jnp.transpose(w_in, (1, 0, 2)).reshape(d, IN_COLS))
    o, og, states, *late = _hgrn_fwd(proj, small["hgrn_lb_logits"], small["hgrn_norm_g"], hosted=ffn_in_gather)
    if ffn_in_gather is not None:
        (w_ffn_in,) = late
    w_conv_out = _relayout("w_conv_out_natural", w_conv_out8, (None, CONV_DIM, 128), lambda j: (j, 0, 0),
                           (CONV_DIM, 128), lambda j: (0, j), jax.ShapeDtypeStruct((CONV_DIM, d), BF16))
    w_hgrn_out = w_hgrn_out8.reshape(d, d)
    w_out = w_out8.reshape(d, d)
    conv_dw =jnp.transpose(conv_dw8[:, :CONV_K, :CONV_DIM // N_DEV], (1, 0, 2)).reshape(CONV_K, CONV_DIM)
    ffn_dw = jnp.transpose(ffn_dw8[:, :FFN_K, :D_FF // N_DEV], (1, 0, 2)).reshape(FFN_K, 4, FF_SHARD)
    small = dict(small, w_conv_dw=conv_dw, w_ffn_dw=jnp.transpose(ffn_dw, (1, 0, 2)),
                 b_ffn_dw=small["b_ffn_dw"].reshape(4, 1, FF_SHARD))

    c_act, conv_pre = _conv_fwd(proj, small["w_conv_dw"], small["b_conv_dw"], small["conv_ln_g"], small["conv_ln_b"])
    y_conv = _mm("y_conv", c_act, w_conv_out, (t, d), F32, (nm, 1, 1),
                 pl.BlockSpec((tm, CONV_DIM), lambda i, j, k: (i, 0)),
                 pl.BlockSpec((CONV_DIM, d), lambda i, j, k: (0, 0)),
                 pl.BlockSpec((tm, d), lambda i, j, k: (i, 0)), NN, (tm, d))
    sq_w = pl.BlockSpec((d, d), lambda i, j, k: (0, 0))
    row_tile = pl.BlockSpec((tm, d), lambda i, j, k: (i, 0))
    y_hgrn = _mm("y_hgrn", og, w_hgrn_out, (t, d), F32, (nm, 1, 1), row_tile, sq_w, row_tile, NN, (tm, d))
    mixed, r1, x1b, *late = _mix_ln1(proj, y_conv, y_hgrn, w_out, x, small["ln1_g"], small["ln1_b"],
                                     hosted=ffn_out_gather)
    if ffn_out_gather is not None:
        (w_ffn_out8,) = late
    w_ffn_out = w_ffn_out8.reshape(4, FF_SHARD, d)
    z = _mm("ffn_in", x1b, w_ffn_in, (N_DEV, t, FF_SHARD), F32, (nm, N_DEV, 1), row_tile,
            pl.BlockSpec((None, d, FF_SHARD), lambda i, j, k: (j, 0, 0)),
            pl.BlockSpec((None, tm, FF_SHARD), lambda i, j, k: (j, i, 0)), NN, (tm, FF_SHARD))
    z = z.reshape(2, 4, t, FF_SHARD)
    act, gelu_and_slope = _ffn_act_fwd(z, small["w_ffn_dw"], small["b_ffn_dw"])

    d_r2, d_r2b, sums_ln2 = _ffn_out_ln2(act, w_ffn_out, r1, target, small["ln1_g"], small["ln1_b"],
                                         small["ln2_g"], small["ln2_b"])
    d_act = _mm("d_act", d_r2b, w_ffn_out, (4, t, FF_SHARD), BF16, (nm, 4, 1), row_tile,
                pl.BlockSpec((None, FF_SHARD, d), lambda i, j, k: (j, 0, 0)),
                pl.BlockSpec((None, tm, FF_SHARD), lambda i, j, k: (j, i, 0)), NT, (tm, FF_SHARD))
    g_w_ffn_out = _mm("g_w_ffn_out", act, d_r2b, (4, FF_SHARD, d), BF16, (4, 1, nk),
                      pl.BlockSpec((None, tk, FF_SHARD), lambda i, j, k: (i, k, 0)),
                      pl.BlockSpec((tk, d), lambda i, j, k: (k, 0)),
                      pl.BlockSpec((None, FF_SHARD, d), lambda i, j, k: (i, 0, 0)), TN, (FF_SHARD, d))
    d_z, sums_ffn = _ffn_act_bwd(d_act, z, gelu_and_slope, small["w_ffn_dw"])
    d_z8 = d_z.reshape(N_DEV, t, FF_SHARD)
    d_r1, d_r1b, sums_ln1 = _d_x1_ln1_bwd(d_z8, w_ffn_in, d_r2, r1, small["ln1_g"])
    g_w_ffn_in = _mm("g_w_ffn_in", x1b, d_z8, (N_DEV, d, FF_SHARD), BF16, (N_DEV, 1, nk),
                     pl.BlockSpec((tk, d), lambda i, j, k: (k, 0)),
                     pl.BlockSpec((None, tk, FF_SHARD), lambda i, j, k: (i, k, 0)),
                     pl.BlockSpec((None, d, FF_SHARD), lambda i, j, k: (i, 0, 0)), TN, (d, FF_SHARD))
    k_tile = pl.BlockSpec((tk, d), lambda i, j, k: (k, 0))
    g_w_out = _mm("g_w_out", mixed, d_r1b, (d, d), BF16, (1, 1, nk), k_tile, k_tile, sq_w, TN, (d, d))
    send = (lambda grads: None) if scatter is None else scatter
    g_w_ffn_out = g_w_ffn_out.reshape(N_DEV, D_FF // N_DEV, d)
    d_y, d_proj, *recv_ffn_out = _d_mixed_merge_bwd(d_r1b, w_out, proj, y_conv, y_hgrn, hosted=send([g_w_ffn_out]))
    d_pre, sums_conv = _d_c_norm_bwd(d_y, w_conv_out, conv_pre, small["conv_ln_g"], small["conv_ln_b"])
    g_w_conv_out = _mm("g_w_conv_out", c_act, d_y, (CONV_DIM, d), BF16, (1, 1, nk),
                       pl.BlockSpec((tk, CONV_DIM), lambda i, j, k: (k, 0)),
                       pl.BlockSpec((None, tk, d), lambda i, j, k: (0, k, 0)),
                       pl.BlockSpec((CONV_DIM, d), lambda i, j, k: (0, 0)), TN, (CONV_DIM, d))
    g_w_conv_out = _relayout("g_w_conv_out_shards", g_w_conv_out, (CONV_DIM, 128), lambda j: (0, j),
                             (None, CONV_DIM, 128), lambda j: (j, 0, 0),
                             jax.ShapeDtypeStruct((N_DEV, CONV_DIM, 128), BF16))
    d_og = _mm("d_og", d_y, w_hgrn_out, (t, d), F32, (nm, 1, 1),
               pl.BlockSpec((None, tm, d), lambda i, j, k: (1, i, 0)), sq_w, row_tile, NT, (tm, d))
    g_w_hgrn_out = _mm("g_w_hgrn_out", og, d_y, (d, d), BF16, (1, 1, nk), k_tile,
                       pl.BlockSpec((None, tk, d), lambda i, j, k: (1, k, 0)), sq_w, TN, (d, d))
    d_proj, g_w_conv_dw, *recv_ffn_in = _conv_bwd_dw(d_pre, proj, small["w_conv_dw"], d_proj,
                                                     hosted=send([g_w_ffn_in]))
    early = [g_w_conv_out, g_w_hgrn_out.reshape(N_DEV, d // N_DEV, d), g_w_out.reshape(N_DEV, d // N_DEV, d)]
    d_proj, sums_hgrn, *early_recv = _hgrn_bwd(d_og, o, proj, states, small["hgrn_lb_logits"], small["hgrn_norm_g"],
                                               d_proj, hosted=send(early))
    early += [g_w_ffn_in, g_w_ffn_out]
    early_recv += recv_ffn_in + recv_ffn_out
    wide = IN_COLS // 4
    g_w_in = _g_w_in(xb, d_proj)
    def add_residual(acc, ins, outs, scr):
        for rows in _row_blocks(ta):
            outs[0][rows, :] = ALPHA * ins[2][rows, :] + acc[rows, :]

    ta = _pick(t, 1024)
    acc_tile = pl.BlockSpec((ta, d), lambda i, j, k: (i, 0))
    grad_x, *late_recv = _mm_fused(
        "grad_x", (t // ta, 1, 4), (d_proj, w_in_bwd, d_r1),
        [pl.BlockSpec((ta, wide), lambda i, j, k: (i, k)), pl.BlockSpec((d, wide), lambda i, j, k: (0, k)), acc_tile],
        [jax.ShapeDtypeStruct((t, d), F32)], [acc_tile], NT, (ta, d), add_residual,
        hosted=None if scatter is None else scatter([g_w_in]))

    large_grads = [g_w_in] + early
    if scatter is not None:
        large_grads = list(zip(large_grads, late_recv + early_recv))
    return grad_x, large_grads, (sums_ln2, sums_conv, sums_hgrn, sums_ln1, sums_ffn, g_w_conv_dw)


def _small_views(sums):
    sums_ln2, sums_conv, sums_hgrn, sums_ln1, sums_ffn, g_w_conv_dw = sums
    d_l0 = sums_hgrn[1:2]
    return {
        "loss": sums_ln2[2:3, 0:128],
        "b_conv_dw": sums_conv[2:3], "conv_ln_g": sums_conv[0:1], "conv_ln_b": sums_conv[1:2],
        "hgrn_lb_logits": jnp.concatenate([d_l0, -d_l0], axis=1),
        "hgrn_norm_g": sums_hgrn[0:1],
        "ln1_g": sums_ln1[0:1], "ln1_b": sums_ln1[1:2],
        "b_ffn_dw": sums_ffn[:, FFN_K, :].reshape(1, D_FF),
        "ln2_g": sums_ln2[0:1], "ln2_b": sums_ln2[1:2],
        "w_conv_dw": g_w_conv_dw[0:CONV_K].reshape(1, CONV_K * CONV_DIM),
        "w_ffn_dw": jnp.transpose(sums_ffn[:, 0:FFN_K, :], (1, 0, 2)).reshape(1, FFN_K * D_FF),
    }


def _coords():
    return lax.axis_index("x"), lax.axis_index("y"), lax.axis_index("c")


def _gather(shards, staged=False):
    n = len(shards)
    later = range(1 if staged else 0, n)

    def parts(ins, outs, sems):
        send_sems, recv_sems, local_sems = sems
        x, y, c = _coords()
        me = 4 * x + 2 * y + c
        sibling = (x, y, 1 - c)
        chips = [(1 - x, y), (x, 1 - y), (1 - x, 1 - y)]

        def copy(a, k, block, to, src=None):
            return pltpu.make_async_remote_copy(
                src_ref=outs[a].at[block] if src is None else src, dst_ref=outs[a].at[block],
                send_sem=send_sems.at[a, k], recv_sem=recv_sems.at[a, k], device_id=to, device_id_type=MESH)

        local = [pltpu.make_async_copy(ins[a], outs[a].at[me], local_sems.at[a]) for a in range(n)]
        first = []
        for a in range(n):
            first.append(copy(a, 0, me, sibling, src=ins[a]))
            for j, chip in enumerate(chips):
                first.append(copy(a, 1 + j, me, (*chip, c), src=ins[a]))
        return x, y, c, sibling, chips, copy, local, first

    def start(ins, outs, sems):
        *_, local, first = parts(ins, outs, sems)
        for cp in local + first:
            cp.start()

    def arrive(ins, outs, sems, s):
        x, y, c, sibling, chips, copy, _, _ = parts(ins, outs, sems)
        if s == 1:
            block = 4 * x + 2 * y + 1 - c
            copy(0, 0, block, sibling).wait_recv()
        elif s <= 4:
            px, py = chips[s - 2]
            block = 4 * px + 2 * py + c
            copy(0, s - 1, block, sibling).wait_recv()
            copy(0, s + 2, block, sibling).start()
        else:
            px, py = chips[s - 5]
            block = 4 * px + 2 * py + 1 - c
            copy(0, s - 1, block, sibling).wait_recv()
        return block

    def middle(ins, outs, sems):
        x, y, c, sibling, chips, copy, _, _ = parts(ins, outs, sems)
        for j, (px, py) in enumerate(chips):
            for a in later:
                copy(a, 1 + j, 4 * px + 2 * py + c, sibling).wait_recv()
                copy(a, 4 + j, 4 * px + 2 * py + c, sibling).start()

    def finish(ins, outs, sems):
        x, y, c, sibling, chips, copy, local, first = parts(ins, outs, sems)
        passed = [copy(a, 4 + j, 4 * px + 2 * py + c, sibling) for a in range(n) for j, (px, py) in enumerate(chips)]
        for a in later:
            copy(a, 0, 4 * x + 2 * y + 1 - c, sibling).wait_recv()
            for j, (px, py) in enumerate(chips):
                copy(a, 4 + j, 4 * px + 2 * py + 1 - c, sibling).wait_recv()
        for cp in first + passed:
            cp.wait_send()
        for cp in local:
            cp.wait()

    hosted = _Hosted(shards, [jax.ShapeDtypeStruct((N_DEV,) + s.shape, s.dtype) for s in shards],
                     [pltpu.SemaphoreType.DMA((n, 7)), pltpu.SemaphoreType.DMA((n, 7)), pltpu.SemaphoreType.DMA((n,))],
                     start, finish, middle)
    hosted.arrive = arrive
    return hosted


def _proj_gather(xb, gather, order):
    t, d = xb.shape
    tm = _pick(t, 2048)
    nm = t // tm
    n_in, n_out = len(gather.inputs), len(gather.out_shapes)

    blocks_per_shard = IN_SHARD // HEAD_DIM

    def body(order_ref, x_ref, *refs):
        ins, refs = refs[:n_in], refs[n_in:]
        o_ref, bwd_ref, outs, refs = refs[0], refs[1], refs[2:2 + n_out], refs[2 + n_out:]
        w_buf, w_sem, bwd_sem, sems = refs[0], refs[1], refs[2], refs[3:]
        s, i = pl.program_id(0), pl.program_id(1)

        def reorder_copies(step):
            copies = []
            for j in range(blocks_per_shard):
                n = blocks_per_shard * order_ref[step] + j
                head_part = n - 2 * CONV_DIM // HEAD_DIM
                p = jnp.where(n >= (2 * CONV_DIM + _HGRN_COLS) // HEAD_DIM, n - (2 * CONV_DIM + _HGRN_COLS) // HEAD_DIM,
                              jnp.where(n < 2 * CONV_DIM // HEAD_DIM, 2 * D_MODEL // HEAD_DIM + n,
                                        (2 * D_MODEL + 2 * CONV_DIM) // HEAD_DIM + 4 * (head_part % HEADS)
                                        + head_part // HEADS))
                copies.append(pltpu.make_async_copy(
                    w_buf.at[step % 2, :, j * HEAD_DIM:(j + 1) * HEAD_DIM],
                    bwd_ref.at[:, pl.ds(pl.multiple_of(p * HEAD_DIM, HEAD_DIM), HEAD_DIM)], bwd_sem.at[step % 2]))
            return copies

        def reorder_done(step):
            pltpu.make_async_copy(w_buf.at[step % 2], w_buf.at[step % 2], bwd_sem.at[step % 2]).wait()

        @pl.when((s == 0) & (i == 0))
        def _():
            gather.start(ins, outs, sems)

        def staging(step, src):
            return pltpu.make_async_copy(src, w_buf.at[step % 2], w_sem.at[step % 2])

        @pl.when((s == 0) & (i == 0))
        def _():
            staging(0, ins[0]).start()

        for step in range(1, N_DEV):
            @pl.when((s == step - 1) & (i == nm - 1))
            def _(step=step):
                if step >= 2:
                    reorder_done(step - 2)
                staging(step, outs[0].at[gather.arrive(ins, outs, sems, step)]).start()

        for step in range(N_DEV):
            @pl.when((s == step) & (i == 0))
            def _(step=step):
                staging(step, ins[0]).wait()
                for cp in reorder_copies(step):
                    cp.start()

        o_ref[...] = _dot(x_ref[...], w_buf[s % 2], NN)

        @pl.when((s == N_DEV - 2) & (i == 0))
        def _():
            gather.middle(ins, outs, sems)

        @pl.when((s == N_DEV - 1) & (i == nm - 1))
        def _():
            reorder_done(N_DEV - 2)
            reorder_done(N_DEV - 1)
            gather.finish(ins, outs, sems)

    outs = pl.pallas_call(
        body, name="proj_gather",
        grid_spec=pltpu.PrefetchScalarGridSpec(
            num_scalar_prefetch=1, grid=(N_DEV, nm),
            in_specs=[pl.BlockSpec((tm, d), lambda s, i, order_ref: (i, 0))] + [ANY] * n_in,
            out_specs=[pl.BlockSpec((tm, IN_SHARD), lambda s, i, order_ref: (i, order_ref[s])), ANY] + [ANY] * n_out,
            scratch_shapes=[pltpu.VMEM((2, d, IN_SHARD), BF16), pltpu.SemaphoreType.DMA((2,)),
                            pltpu.SemaphoreType.DMA((2,))] + gather.sem_shapes),
        out_shape=[jax.ShapeDtypeStruct((t, IN_COLS), F32), jax.ShapeDtypeStruct((d, IN_COLS), BF16)]
        + gather.out_shapes,
        compiler_params=pltpu.CompilerParams(dimension_semantics=("arbitrary", "arbitrary"),
                                             vmem_limit_bytes=VMEM_LIMIT, has_side_effects=True),
    )(order, xb, *gather.inputs)
    return outs[0], outs[1], list(outs[2:])


def _scatter(grads):
    n = len(grads)

    def copies(ins, outs, sems):
        send_sems, recv_sems = sems
        x, y, c = _coords()
        out = []
        for a in range(n):
            for k in range(1, N_DEV):
                px, py, pc = x ^ (k >> 2), y ^ ((k >> 1) & 1), c ^ (k & 1)
                out.append(pltpu.make_async_remote_copy(
                    src_ref=ins[a].at[4 * px + 2 * py + pc], dst_ref=outs[a].at[k - 1],
                    send_sem=send_sems.at[a, k - 1], recv_sem=recv_sems.at[a, k - 1],
                    device_id=(px, py, pc), device_id_type=MESH))
        return out

    def start(ins, outs, sems):
        for cp in copies(ins, outs, sems):
            cp.start()

    def finish(ins, outs, sems):
        for cp in copies(ins, outs, sems):
            cp.wait()

    return _Hosted(grads, [jax.ShapeDtypeStruct((N_DEV - 1,) + g.shape[1:], g.dtype) for g in grads],
                   [pltpu.SemaphoreType.DMA((n, N_DEV - 1)), pltpu.SemaphoreType.DMA((n, N_DEV - 1))], start, finish)


def _row_tile(rows):
    return 256 if rows % 256 == 0 else rows


def _adam_math(w, g, m, v):
    m_new = ADAM_B1 * m + (1.0 - ADAM_B1) * g
    v_new = ADAM_B2 * v + (1.0 - ADAM_B2) * (g * g)
    m_hat = m_new / (1.0 - ADAM_B1 ** ADAM_STEP)
    v_hat = v_new / (1.0 - ADAM_B2 ** ADAM_STEP)
    delta = -ADAM_LR * (m_hat / (jnp.sqrt(v_hat) + ADAM_EPS) + ADAM_WD * w)
    return delta, m_new, v_new


def _adam_large(name, own, recv, me, w, m, v):
    rows, cols = w.shape
    tr = _row_tile(rows)

    def body(me_ref, p_ref, r_ref, w_ref, m_ref, v_ref, g_out, d_out, m_out, v_out):
        g = p_ref[...].astype(F32)
        for k in range(N_DEV - 1):
            g = g + r_ref[k].astype(F32)
        delta, m_new, v_new = _adam_math(w_ref[...], g, m_ref[...], v_ref[...])
        g_out[...] = g
        d_out[...] = delta
        m_out[...] = m_new
        v_out[...] = v_new

    tile = pl.BlockSpec((tr, cols), lambda r, me_ref: (r, 0))
    sds = jax.ShapeDtypeStruct((rows, cols), F32)
    return pl.pallas_call(
        body, name=name,
        grid_spec=pltpu.PrefetchScalarGridSpec(
            num_scalar_prefetch=1, grid=(rows // tr,),
            in_specs=[pl.BlockSpec((None, tr, cols), lambda r, me_ref: (me_ref[0], r, 0)),
                      pl.BlockSpec((N_DEV - 1, tr, cols), lambda r, me_ref: (0, r, 0)), tile, tile, tile],
            out_specs=[tile, tile, tile, tile]),
        out_shape=[sds, sds, sds, sds],
        compiler_params=_params(("parallel",)),
    )(me, own, recv, w, m, v)


def _small_allreduce(arrays):
    n = len(arrays)

    def body(*refs):
        ins, outs, gats = refs[:n], refs[n:2 * n], refs[2 * n:3 * n]
        send_sems, recv_sems = refs[3 * n:]
        x, y, c = _coords()
        me = 4 * x + 2 * y + c
        peers = [(x ^ (k >> 2), y ^ ((k >> 1) & 1), c ^ (k & 1)) for k in range(1, N_DEV)]

        def copy(a, k, slot):
            return pltpu.make_async_remote_copy(
                src_ref=ins[a], dst_ref=gats[a].at[slot], send_sem=send_sems.at[a, k], recv_sem=recv_sems.at[a, k],
                device_id=peers[k], device_id_type=MESH)

        sends = [copy(a, k, me) for a in range(n) for k in range(N_DEV - 1)]
        for a in range(n):
            gats[a][me] = ins[a][...]
        for cp in sends:
            cp.start()
        for a in range(n):
            for k, (px, py, pc) in enumerate(peers):
                copy(a, k, 4 * px + 2 * py + pc).wait_recv()
        for cp in sends:
            cp.wait_send()
        for a in range(n):
            acc = gats[a][0]
            for dev in range(1, N_DEV):
                acc = acc + gats[a][dev]
            outs[a][...] = acc

    whole = pl.BlockSpec(memory_space=pltpu.VMEM)
    return pl.pallas_call(
        body, name="small_allreduce", in_specs=[whole] * n, out_specs=[whole] * n,
        out_shape=[jax.ShapeDtypeStruct(a.shape, F32) for a in arrays],
        scratch_shapes=[pltpu.VMEM((N_DEV,) + a.shape, F32) for a in arrays]
        + [pltpu.SemaphoreType.DMA((n, N_DEV - 1)), pltpu.SemaphoreType.DMA((n, N_DEV - 1))],
        compiler_params=pltpu.CompilerParams(has_side_effects=True, vmem_limit_bytes=VMEM_LIMIT),
    )(*arrays)


def _adam_replicated(sums, w, m, v):
    rows_of = {"conv_ln_g": (1, 0), "conv_ln_b": (1, 1), "b_conv_dw": (1, 2), "hgrn_norm_g": (2, 0),
               "ln1_g": (3, 0), "ln1_b": (3, 1), "ln2_g": (0, 0), "ln2_b": (0, 1)}
    names = list(rows_of) + ["hgrn_lb_logits"]
    n = len(names)

    def body(*refs):
        sum_refs, refs = refs[:4], refs[4:]
        w_refs, m_refs, v_refs, outs = refs[:n], refs[n:2 * n], refs[2 * n:3 * n], refs[3 * n:]
        for j, name in enumerate(names):
            if name == "hgrn_lb_logits":
                d_l0 = sum_refs[2][1:2, :]
                grads = [d_l0, -d_l0]
            else:
                a, row = rows_of[name]
                grads = [sum_refs[a][row:row + 1, :]]
            g_out, d_out, m_out, v_out = outs[4 * j:4 * j + 4]
            for r, g in enumerate(grads):
                rows = slice(r, r + 1)
                delta, m_new, v_new = _adam_math(w_refs[j][rows, :], g, m_refs[j][rows, :], v_refs[j][rows, :])
                g_out[rows, :] = g
                d_out[rows, :] = delta
                m_out[rows, :] = m_new
                v_out[rows, :] = v_new

    whole = pl.BlockSpec(memory_space=pltpu.VMEM)
    operands = list(sums) + [w[k] for k in names] + [m[k] for k in names] + [v[k] for k in names]
    outs = pl.pallas_call(
        body, name="adam_replicated", in_specs=[whole] * len(operands), out_specs=[whole] * (4 * n),
        out_shape=[jax.ShapeDtypeStruct(w[k].shape, F32) for k in names for _ in range(4)],
    )(*operands)
    return {name: tuple(outs[4 * j:4 * j + 4]) for j, name in enumerate(names)}


def _adam_small(w, g, m, v):
    def body(w_ref, g_ref, m_ref, v_ref, d_out, m_out, v_out):
        delta, m_new, v_new = _adam_math(w_ref[...], g_ref[...], m_ref[...], v_ref[...])
        d_out[...] = delta
        m_out[...] = m_new
        v_out[...] = v_new

    whole = pl.BlockSpec(memory_space=pltpu.VMEM)
    sds = jax.ShapeDtypeStruct(w.shape, F32)
    return pl.pallas_call(body, name="adam_small", in_specs=[whole] * 4, out_specs=[whole] * 3,
                          out_shape=[sds, sds, sds])(w, g, m, v)


_WEIGHTS = ["w_in", "w_conv_dw", "b_conv_dw", "conv_ln_g", "conv_ln_b", "w_conv_out", "hgrn_lb_logits", "hgrn_norm_g",
            "w_hgrn_out", "w_out", "ln1_g", "ln1_b", "w_ffn_in", "w_ffn_dw", "b_ffn_dw", "w_ffn_out", "ln2_g", "ln2_b"]
_LARGE = ["w_in", "w_conv_out", "w_hgrn_out", "w_out", "w_ffn_in", "w_ffn_out"]
_CONV_DW_SHARD = CONV_DIM // N_DEV
_FFN_DW_SHARD = D_FF // N_DEV


def kernel(x, w_in, w_conv_dw, b_conv_dw, conv_ln_g, conv_ln_b, w_conv_out, hgrn_lb_logits, hgrn_norm_g, w_hgrn_out, w_out, ln1_g, ln1_b, w_ffn_in, w_ffn_dw, b_ffn_dw, w_ffn_out, ln2_g, ln2_b, loss_target, m_w_in, m_w_conv_dw, m_b_conv_dw, m_conv_ln_g, m_conv_ln_b, m_w_conv_out, m_hgrn_lb_logits, m_hgrn_norm_g, m_w_hgrn_out, m_w_out, m_ln1_g, m_ln1_b, m_w_ffn_in, m_w_ffn_dw, m_b_ffn_dw, m_w_ffn_out, m_ln2_g, m_ln2_b, v_w_in, v_w_conv_dw, v_b_conv_dw, v_conv_ln_g, v_conv_ln_b, v_w_conv_out, v_hgrn_lb_logits, v_hgrn_norm_g, v_w_hgrn_out, v_w_out, v_ln1_g, v_ln1_b, v_w_ffn_in, v_w_ffn_dw, v_b_ffn_dw, v_w_ffn_out, v_ln2_g, v_ln2_b):
    w = dict(w_in=w_in, w_conv_dw=w_conv_dw, b_conv_dw=b_conv_dw, conv_ln_g=conv_ln_g, conv_ln_b=conv_ln_b,
             w_conv_out=w_conv_out, hgrn_lb_logits=hgrn_lb_logits, hgrn_norm_g=hgrn_norm_g, w_hgrn_out=w_hgrn_out,
             w_out=w_out, ln1_g=ln1_g, ln1_b=ln1_b, w_ffn_in=w_ffn_in, w_ffn_dw=w_ffn_dw, b_ffn_dw=b_ffn_dw,
             w_ffn_out=w_ffn_out, ln2_g=ln2_g, ln2_b=ln2_b)
    m = dict(w_in=m_w_in, w_conv_dw=m_w_conv_dw, b_conv_dw=m_b_conv_dw, conv_ln_g=m_conv_ln_g, conv_ln_b=m_conv_ln_b,
             w_conv_out=m_w_conv_out, hgrn_lb_logits=m_hgrn_lb_logits, hgrn_norm_g=m_hgrn_norm_g,
             w_hgrn_out=m_w_hgrn_out, w_out=m_w_out, ln1_g=m_ln1_g, ln1_b=m_ln1_b, w_ffn_in=m_w_ffn_in,
             w_ffn_dw=m_w_ffn_dw, b_ffn_dw=m_b_ffn_dw, w_ffn_out=m_w_ffn_out, ln2_g=m_ln2_g, ln2_b=m_ln2_b)
    v = dict(w_in=v_w_in, w_conv_dw=v_w_conv_dw, b_conv_dw=v_b_conv_dw, conv_ln_g=v_conv_ln_g, conv_ln_b=v_conv_ln_b,
             w_conv_out=v_w_conv_out, hgrn_lb_logits=v_hgrn_lb_logits, hgrn_norm_g=v_hgrn_norm_g,
             w_hgrn_out=v_w_hgrn_out, w_out=v_w_out, ln1_g=v_ln1_g, ln1_b=v_ln1_b, w_ffn_in=v_w_ffn_in,
             w_ffn_dw=v_w_ffn_dw, b_ffn_dw=v_b_ffn_dw, w_ffn_out=v_w_ffn_out, ln2_g=v_ln2_g, ln2_b=v_ln2_b)
    xi, yi, ci = lax.axis_index("x"), lax.axis_index("y"), lax.axis_index("c")
    me = 4 * xi + 2 * yi + ci
    me_op = jnp.reshape(me, (1,)).astype(jnp.int32)

    shards = [w[name][0].astype(BF16) for name in _LARGE]
    shards.append(jnp.pad(w_conv_dw[0], ((0, 1), (0, 128 - _CONV_DW_SHARD))))
    shards.append(jnp.pad(w_ffn_dw[0], ((0, 8 - FFN_K), (0, 384 - _FFN_DW_SHARD))))
    chips = [(1 - xi, yi), (xi, 1 - yi), (1 - xi, 1 - yi)]
    order = jnp.stack([me, me ^ 1] + [4 * px + 2 * py + ci for px, py in chips]
                      + [4 * px + 2 * py + 1 - ci for px, py in chips]).astype(jnp.int32)
    small = dict(b_conv_dw=b_conv_dw, conv_ln_g=conv_ln_g, conv_ln_b=conv_ln_b, hgrn_lb_logits=hgrn_lb_logits,
                 hgrn_norm_g=hgrn_norm_g, ln1_g=ln1_g, ln1_b=ln1_b, ln2_g=ln2_g, ln2_b=ln2_b, b_ffn_dw=b_ffn_dw)

    gathers = (_gather(shards[:4] + shards[6:], staged=True), _gather(shards[4:5]), _gather(shards[5:6]))
    grad_x, large_grads, small_sums = _local_step(x[0], loss_target[0], gathers, small, _scatter, order)

    out = {}
    for name, (own, recv) in zip(_LARGE, large_grads):
        out[name] = _adam_large("adam_" + name, own, recv, me_op, w[name][0], m[name][0], v[name][0])

    totals = _small_allreduce(list(small_sums))
    out.update(_adam_replicated(totals[:4], w, m, v))
    summed = _small_views(totals)
    loss = summed["loss"][0, 0]
    conv_dw_g = lax.dynamic_slice_in_dim(summed["w_conv_dw"].reshape(CONV_K, CONV_DIM), me * _CONV_DW_SHARD, _CONV_DW_SHARD, axis=1)
    ffn_dw_g = lax.dynamic_slice_in_dim(summed["w_ffn_dw"].reshape(FFN_K, D_FF), me * _FFN_DW_SHARD, _FFN_DW_SHARD, axis=1)
    small_g = dict(b_ffn_dw=summed["b_ffn_dw"], w_conv_dw=conv_dw_g.reshape(1, -1), w_ffn_dw=ffn_dw_g.reshape(1, -1))
    names = list(small_g)
    flat = lambda d, n: d[n].reshape(1, -1)
    n_small = sum(small_g[n].shape[1] for n in names)
    pad = (-n_small) % 1024
    pack = lambda pieces: jnp.pad(jnp.concatenate(pieces, axis=1), ((0, 0), (0, pad))).reshape(-1, 128)
    d_s, m_s, v_s = _adam_small(pack([flat(w, n) for n in names]), pack([small_g[n] for n in names]),
                                pack([flat(m, n) for n in names]), pack([flat(v, n) for n in names]))
    pos = 0
    for n in names:
        size = small_g[n].shape[1]
        cut = lambda a: a.reshape(1, -1)[:, pos:pos + size].reshape(w[n].shape)
        out[n] = (small_g[n].reshape(w[n].shape), cut(d_s), cut(m_s), cut(v_s))
        pos += size

    for name in _LARGE:
        out[name] = tuple(a.reshape(w[name].shape) for a in out[name])
    grads = [out[n][0] for n in _WEIGHTS]
    deltas = [out[n][1] for n in _WEIGHTS]
    new_m = [out[n][2] for n in _WEIGHTS]
    new_v = [out[n][3] for n in _WEIGHTS]
    return (loss, grad_x[None], *grads, *deltas, *new_m, *new_v)
```

```python
import functools
import math

import jax
import jax.numpy as jnp
from jax import lax
from jax.experimental import pallas as pl
from jax.experimental.pallas import tpu as pltpu

F32 = jnp.float32
BF16 = jnp.bfloat16

N_DEV = 8
D_MODEL = 1024
CONV_DIM = 512
CONV_K = 31
HGRN_DIM = 1024
HEADS = 8
HEAD_DIM = 128
D_FF = 2816
FFN_K = 3
FF_SHARD = 2 * D_FF // N_DEV
IN_COLS = 7168
IN_SHARD = IN_COLS // N_DEV
LN_EPS = 1e-5
RMS_EPS = 1e-6
ALPHA = 2.0 ** 0.25

ADAM_LR = 0.001
ADAM_B1 = 0.9
ADAM_B2 = 0.999
ADAM_EPS = 1e-08
ADAM_WD = 0.01
ADAM_STEP = 10

CHUNK = 64
CHUNKS_PER_BLOCK = 32
CONV_HALO = 32
FFN_HALO = 8
ROW_BLOCK = 64
SUBLANES = 8
VMEM_LIMIT = 48 * 1024 * 1024
MXU_DEPTH = 256

DP_MERGE_BLOCK = 0
DP_CONV_BLOCK = 2
DP_HEAD_BLOCK = 6

MESH = pl.DeviceIdType.MESH
ANY = pl.BlockSpec(memory_space=pl.ANY)

NN = (((1,), (0,)), ((), ()))
NT = (((1,), (1,)), ((), ()))
TN = (((0,), (0,)), ((), ()))


def _params(sem):
    return pltpu.CompilerParams(dimension_semantics=sem, vmem_limit_bytes=VMEM_LIMIT)


def _dot(a, b, dims):
    return lax.dot_general(a.astype(BF16), b.astype(BF16), dims, preferred_element_type=F32)


def _sigmoid(x):
    return jax.nn.sigmoid(x)


def _ln(r):
    mu = jnp.mean(r, axis=-1, keepdims=True)
    xc = r - mu
    var = jnp.mean(xc * xc, axis=-1, keepdims=True)
    rstd = lax.rsqrt(var + LN_EPS)
    return xc * rstd, rstd


def _ln_bwd(dy, xhat, rstd, g):
    dxh = dy * g
    m1 = jnp.mean(dxh, axis=-1, keepdims=True)
    m2 = jnp.mean(dxh * xhat, axis=-1, keepdims=True)
    return rstd * (dxh - m1 - xhat * m2)


def _colsum(x):
    return jnp.sum(x, axis=0, keepdims=True)


class _Hosted:
    def __init__(self, inputs, out_shapes, sem_shapes, start, finish, middle=None):
        self.inputs, self.out_shapes, self.sem_shapes = list(inputs), list(out_shapes), list(sem_shapes)
        self.start, self.finish, self.middle = start, finish, middle


def _call(body, *, name, grid, in_specs, out_specs, out_shape, scratch_shapes, semantics, operands, hosted=None,
          aliases=None):
    aliases = aliases or {}
    if hosted is None:
        return pl.pallas_call(
            body, name=name, grid=grid, in_specs=list(in_specs), out_specs=list(out_specs), out_shape=list(out_shape),
            scratch_shapes=list(scratch_shapes), input_output_aliases=aliases,
            compiler_params=_params(semantics))(*operands)
    n_in, n_out, n_scr = len(in_specs), len(out_specs), len(scratch_shapes)
    h_in, h_out = len(hosted.inputs), len(hosted.out_shapes)

    def full_body(*refs):
        ins, refs = refs[:n_in], refs[n_in:]
        h_ins, refs = refs[:h_in], refs[h_in:]
        outs, refs = refs[:n_out], refs[n_out:]
        h_outs, refs = refs[:h_out], refs[h_out:]
        scr, sems = refs[:n_scr], refs[n_scr:]
        first = functools.reduce(jnp.logical_and, [pl.program_id(d) == 0 for d in range(len(grid))])
        last = functools.reduce(jnp.logical_and, [pl.program_id(d) == grid[d] - 1 for d in range(len(grid))])

        @pl.when(first)
        def _():
            hosted.start(h_ins, h_outs, sems)

        body(*ins, *outs, *scr)

        if hosted.middle is not None:
            step, total = 0, 1
            for d in range(len(grid)):
                step, total = step * grid[d] + pl.program_id(d), total * grid[d]

            @pl.when(step == (2 * total) // 3)
            def _():
                hosted.middle(h_ins, h_outs, sems)

        @pl.when(last)
        def _():
            hosted.finish(h_ins, h_outs, sems)

    return pl.pallas_call(
        full_body, name=name, grid=grid, in_specs=list(in_specs) + [ANY] * h_in,
        out_specs=list(out_specs) + [ANY] * h_out, out_shape=list(out_shape) + hosted.out_shapes,
        scratch_shapes=list(scratch_shapes) + hosted.sem_shapes, input_output_aliases=aliases,
        compiler_params=pltpu.CompilerParams(dimension_semantics=("arbitrary",) * len(grid),
                                             vmem_limit_bytes=VMEM_LIMIT, has_side_effects=True),
    )(*operands, *hosted.inputs)


def _mm(name, a, b, out_shape, out_dtype, grid, a_spec, b_spec, o_spec, dims, acc_shape, hosted=None):
    nk = grid[2]
    if nk == 1:
        def body(a_ref, b_ref, o_ref):
            o_ref[...] = _dot(a_ref[...], b_ref[...], dims).astype(o_ref.dtype)
        scratch = []
    else:
        def body(a_ref, b_ref, o_ref, acc_ref):
            k = pl.program_id(2)

            @pl.when(k == 0)
            def _():
                acc_ref[...] = jnp.zeros_like(acc_ref)

            acc_ref[...] += _dot(a_ref[...], b_ref[...], dims)

            @pl.when(k == nk - 1)
            def _():
                o_ref[...] = acc_ref[...].astype(o_ref.dtype)
        scratch = [pltpu.VMEM(acc_shape, F32)]

    outs = _call(body, name=name, grid=grid, in_specs=[a_spec, b_spec], out_specs=[o_spec],
                 out_shape=[jax.ShapeDtypeStruct(out_shape, out_dtype)], scratch_shapes=scratch,
                 semantics=("parallel", "parallel", "arbitrary"), operands=(a, b), hosted=hosted)
    return outs[0] if hosted is None else (outs[0], list(outs[1:]))


def _mm_fused(name, grid, operands, in_specs, out_shape, out_specs, dims, acc_shape, epilogue, lhs=None, scratch=(),
              hosted=None):
    nk = grid[2]
    n_in, n_out = len(in_specs), len(out_specs)

    def body(*refs):
        ins, outs, scr = refs[:n_in], refs[n_in:n_in + n_out], refs[n_in + n_out:]
        acc_ref, k = scr[0], pl.program_id(2)
        a = ins[0][...] if lhs is None else lhs(ins, outs)
        part = _dot(a, ins[1][...], dims)
        if nk == 1:
            acc_ref[...] = part
            epilogue(acc_ref, ins, outs, scr[1:])
            return

        @pl.when(k == 0)
        def _():
            acc_ref[...] = jnp.zeros_like(acc_ref)

        acc_ref[...] += part

        @pl.when(k == nk - 1)
        def _():
            epilogue(acc_ref, ins, outs, scr[1:])

    return _call(body, name=name, grid=grid, in_specs=in_specs, out_specs=out_specs, out_shape=out_shape,
                 scratch_shapes=[pltpu.VMEM(acc_shape, F32)] + list(scratch), semantics=("arbitrary",) * 3,
                 operands=operands, hosted=hosted)


def _row_blocks(rows, block=256):
    block = block if rows % block == 0 else rows
    return [slice(r, r + block) for r in range(0, rows, block)]


def _pick(t, pref):
    return pref if t % pref == 0 else t


def _glu(p):
    return p[:, :CONV_DIM] * _sigmoid(p[:, CONV_DIM:])


def _by_phase(taps):
    phases = {}
    for off, payload in taps:
        phases.setdefault(off % SUBLANES, []).append((off - off % SUBLANES, payload))
    return sorted(phases.items())


def _tap_sum(src_ref, base, taps, rows, lanes):
    acc = None
    for phase, items in _by_phase(taps):
        n = rows if phase == 0 else rows + SUBLANES
        part = None
        for off, (w_ref, k) in items:
            term = w_ref[k:k + 1, lanes] * src_ref[base + off:base + off + n, lanes]
            part = term if part is None else part + term
        if phase:
            part = part[phase:phase + rows, :]
        acc = part if acc is None else acc + part
    return acc


def _tap_products(x, src_ref, base, taps, lanes):
    rows, cols = x.shape
    pad = jnp.zeros((SUBLANES, cols), x.dtype)
    padded = jnp.concatenate([pad, x, pad], axis=0)
    out = []
    for phase, items in _by_phase(taps):
        n = rows if phase == 0 else rows + SUBLANES
        shifted = x if phase == 0 else padded[SUBLANES - phase:SUBLANES - phase + n, :]
        for off, key in items:
            out.append((key, _colsum(shifted * src_ref[base + off:base + off + n, lanes])))
    return out


def _lane_blocks(cols, block=256):
    return [slice(c, min(c + block, cols)) for c in range(0, cols, block)]


def _conv_fwd(proj, w_dw, b_dw, g, b):
    t = proj.shape[0]
    tm = _pick(t, 512)
    nh = tm // CONV_HALO

    def body(p_ref, ph_ref, w_ref, bd_ref, g_ref, b_ref, act_ref, pre_ref, xs_ref):
        i = pl.program_id(0)
        halo = _glu(ph_ref[...])
        xs_ref[0:CONV_HALO, :] = jnp.where(i == 0, 0.0, halo)
        xs_ref[CONV_HALO:CONV_HALO + tm, :] = _glu(p_ref[...])
        taps = [(CONV_HALO - (CONV_K - 1) + k, (w_ref, k)) for k in range(CONV_K)]
        for r in range(tm // ROW_BLOCK):
            rows = slice(r * ROW_BLOCK, (r + 1) * ROW_BLOCK)
            for lanes in _lane_blocks(CONV_DIM):
                pre_ref[rows, lanes] = bd_ref[:, lanes] + _tap_sum(xs_ref, r * ROW_BLOCK, taps, ROW_BLOCK, lanes)
            acc = pre_ref[rows, :]
            xhat, _ = _ln(acc)
            yln = xhat * g_ref[...] + b_ref[...]
            act_ref[rows, :] = (yln * _sigmoid(yln)).astype(BF16)

    full = lambda s: pl.BlockSpec(s, lambda i: (0, 0))
    return pl.pallas_call(
        body, name="conv_fwd", grid=(t // tm,),
        in_specs=[pl.BlockSpec((tm, 2 * CONV_DIM), lambda i: (i, 0)),
                  pl.BlockSpec((CONV_HALO, 2 * CONV_DIM), lambda i: (jnp.maximum(i * nh - 1, 0), 0)),
                  full((CONV_K, CONV_DIM)), full((1, CONV_DIM)), full((1, CONV_DIM)), full((1, CONV_DIM))],
        out_specs=[pl.BlockSpec((tm, CONV_DIM), lambda i: (i, 0)), pl.BlockSpec((tm, CONV_DIM), lambda i: (i, 0))],
        out_shape=[jax.ShapeDtypeStruct((t, CONV_DIM), BF16), jax.ShapeDtypeStruct((t, CONV_DIM), F32)],
        scratch_shapes=[pltpu.VMEM((CONV_HALO + tm, CONV_DIM), F32)],
        compiler_params=_params(("arbitrary",)),
    )(proj, proj, w_dw, b_dw, g, b)


def _d_c_norm_bwd(d_y, w_conv_out, pre, g, b):
    t = pre.shape[0]
    tm = _pick(t, 512)
    d = D_MODEL

    def epilogue(d_c, ins, outs, scr):
        pre_ref, g_ref, b_ref = ins[2:]
        dpre_ref, sums_ref = outs
        i = pl.program_id(0)

        @pl.when(i == 0)
        def _():
            sums_ref[...] = jnp.zeros_like(sums_ref)

        for rows in _row_blocks(tm):
            xhat, rstd = _ln(pre_ref[rows, :])
            yln = xhat * g_ref[...] + b_ref[...]
            sg = _sigmoid(yln)
            dyln = d_c[rows, :] * (sg * (1.0 + yln * (1.0 - sg)))
            dpre = _ln_bwd(dyln, xhat, rstd, g_ref[...])
            dpre_ref[rows, :] = dpre
            sums_ref[0:1, :] += _colsum(dyln * xhat)
            sums_ref[1:2, :] += _colsum(dyln)
            sums_ref[2:3, :] += _colsum(dpre)

    full = lambda s: pl.BlockSpec(s, lambda i, j, k: (0, 0))
    tile = pl.BlockSpec((tm, CONV_DIM), lambda i, j, k: (i, 0))
    return _mm_fused(
        "d_c_norm_bwd", (t // tm, 1, 1), (d_y, w_conv_out, pre, g, b),
        [pl.BlockSpec((None, tm, d), lambda i, j, k: (0, i, 0)), full((CONV_DIM, d)), tile,
         full((1, CONV_DIM)), full((1, CONV_DIM))],
        [jax.ShapeDtypeStruct((t, CONV_DIM), F32), jax.ShapeDtypeStruct((8, CONV_DIM), F32)],
        [tile, full((8, CONV_DIM))], NT, (tm, CONV_DIM), epilogue)


def _conv_bwd_dw(d_pre, proj, w_dw, d_proj, hosted=None):
    t = d_pre.shape[0]
    tm = _pick(t, 512)
    nt = t // tm
    nh = tm // CONV_HALO
    last_h = t // CONV_HALO - 1

    def body(dp_ref, dph_ref, p_ref, ph_ref, w_ref, _, dproj_ref, dw_ref, xs_ref, ds_ref):
        i = pl.program_id(0)

        @pl.when(i == 0)
        def _():
            dw_ref[...] = jnp.zeros_like(dw_ref)

        halo = _glu(ph_ref[...])
        xs_ref[0:CONV_HALO, :] = jnp.where(i == 0, 0.0, halo)
        xs_ref[CONV_HALO:CONV_HALO + tm, :] = _glu(p_ref[...])
        ds_ref[0:tm, :] = dp_ref[...]
        ds_ref[tm:tm + CONV_HALO, :] = jnp.where(i == nt - 1, 0.0, dph_ref[...])
        back_taps = [(CONV_K - 1 - k, (w_ref, k)) for k in range(CONV_K)]
        grad_taps = [(CONV_HALO - (CONV_K - 1) + k, k) for k in range(CONV_K)]
        for r in range(tm // ROW_BLOCK):
            base = r * ROW_BLOCK
            rows = slice(base, base + ROW_BLOCK)
            for lanes in _lane_blocks(CONV_DIM):
                gate_lanes = slice(CONV_DIM + lanes.start, CONV_DIM + lanes.stop)
                acc = _tap_sum(ds_ref, base, back_taps, ROW_BLOCK, lanes)
                for k, total in _tap_products(ds_ref[rows, lanes], xs_ref, base, grad_taps, lanes):
                    dw_ref[k:k + 1, lanes] += total
                cval = p_ref[rows, lanes]
                sg = _sigmoid(p_ref[rows, gate_lanes])
                dproj_ref[rows, lanes] = (acc * sg).astype(BF16)
                dproj_ref[rows, gate_lanes] = (acc * cval * sg * (1.0 - sg)).astype(BF16)

    full = lambda s: pl.BlockSpec(s, lambda i: (0, 0))
    return _call(
        body, name="conv_bwd_dw", grid=(nt,),
        in_specs=[pl.BlockSpec((tm, CONV_DIM), lambda i: (i, 0)),
                  pl.BlockSpec((CONV_HALO, CONV_DIM), lambda i: (jnp.minimum((i + 1) * nh, last_h), 0)),
                  pl.BlockSpec((tm, 2 * CONV_DIM), lambda i: (i, 0)),
                  pl.BlockSpec((CONV_HALO, 2 * CONV_DIM), lambda i: (jnp.maximum(i * nh - 1, 0), 0)),
                  full((CONV_K, CONV_DIM)), ANY],
        out_specs=[pl.BlockSpec((tm, 2 * CONV_DIM), lambda i: (i, DP_CONV_BLOCK)), full((CONV_HALO, CONV_DIM))],
        out_shape=[jax.ShapeDtypeStruct(d_proj.shape, BF16), jax.ShapeDtypeStruct((CONV_HALO, CONV_DIM), F32)],
        scratch_shapes=[pltpu.VMEM((CONV_HALO + tm, CONV_DIM), F32), pltpu.VMEM((tm + CONV_HALO, CONV_DIM), F32)],
        semantics=("arbitrary",), operands=(d_pre, d_pre, proj, proj, w_dw, d_proj), hosted=hosted, aliases={5: 0})


def _lower_bound(logit_ref):
    l0 = logit_ref[0:1, :]
    l1 = logit_ref[1:2, :]
    m = jnp.maximum(l0, l1)
    e0 = jnp.exp(l0 - m)
    e1 = jnp.exp(l1 - m)
    return e0 / (e0 + e1)


def _tri(lower):
    r = lax.broadcasted_iota(jnp.int32, (CHUNK, CHUNK), 0)
    c = lax.broadcasted_iota(jnp.int32, (CHUNK, CHUNK), 1)
    return (c <= r) if lower else (c >= r)


def _hgrn_gates(fz, lb):
    s = _sigmoid(fz)
    sn = _sigmoid(-fz)
    f = lb + (1.0 - lb) * s
    return s, sn, f


def _block_tri(rows, lower=True):
    r = lax.broadcasted_iota(jnp.int32, (rows, rows), 0)
    c = lax.broadcasted_iota(jnp.int32, (rows, rows), 1)
    tri = (c <= r) if lower else (c >= r)
    return (tri & (r // CHUNK == c // CHUNK)).astype(BF16)


def _tri_rows(tm):
    return min(tm, MXU_DEPTH)


def _tri_matmul(tri_ref, x):
    hi = x.astype(BF16)
    lo = (x - hi.astype(F32)).astype(BF16)
    tri = tri_ref[...]
    return (lax.dot_general(tri, hi, NN, preferred_element_type=F32)
            + lax.dot_general(tri, lo, NN, preferred_element_type=F32))


def _groups(tm):
    g = _tri_rows(tm)
    return [slice(i * g, (i + 1) * g) for i in range(tm // g)]


def _hgrn_fwd(proj, logits, norm_g, hosted=None):
    t = proj.shape[0]
    tm = CHUNK * CHUNKS_PER_BLOCK if t % (CHUNK * CHUNKS_PER_BLOCK) == 0 else CHUNK
    cpb = tm // CHUNK
    nt = t // tm
    half = CHUNK // 2

    def body(qz_ref, fz_ref, iv_ref, gz_ref, lg_ref, ng_ref, tri_ref, o_ref, og_ref, st_ref,
             state_ref, qe_ref, ke_ref, qb_ref, kl_ref, v_ref, upd_ref, decay_ref, a_ref, q_ref, kk_ref, b_ref):
        j = pl.program_id(1)

        @pl.when(j == 0)
        def _():
            state_ref[...] = jnp.zeros_like(state_ref)

        lb = _lower_bound(lg_ref)
        chunks = [slice(c * CHUNK, (c + 1) * CHUNK) for c in range(cpb)]
        for rows in chunks:
            qz = qz_ref[rows, :]
            q_ref[rows, :] = qz * _sigmoid(qz)
            _, sn, f = _hgrn_gates(fz_ref[rows, :], lb)
            kk_ref[rows, :] = (1.0 - lb) * sn
            b_ref[rows, :] = jnp.log(f)
            v_ref[rows, :] = iv_ref[rows, :].astype(BF16)
        for rows in _groups(tm):
            b_ref[rows, :] = _tri_matmul(tri_ref, b_ref[rows, :])
        for c, rows in enumerate(chunks):
            b = b_ref[rows, :]
            bref = b[half - 1:half, :]
            blast = b[CHUNK - 1:CHUNK, :]
            q = q_ref[rows, :]
            kk = kk_ref[rows, :]
            qb_ref[rows, :] = (q * jnp.exp(b)).astype(BF16)
            qe_ref[rows, :] = (q * jnp.exp(b - bref)).astype(BF16)
            ke_ref[rows, :] = (kk * jnp.exp(bref - b)).astype(BF16)
            kl_ref[rows, :] = (kk * jnp.exp(blast - b)).astype(BF16)
            decay_ref[c:c + 1, :] = jnp.exp(blast)
        causal = _tri(True)
        for c, rows in enumerate(chunks):
            upd_ref[c] = _dot(v_ref[rows, :], kl_ref[rows, :], TN)
            a_ref[c] = jnp.where(causal, _dot(qe_ref[rows, :], ke_ref[rows, :], NT), 0.0).astype(BF16)
        state = state_ref[...]
        for c in range(cpb):
            st_ref[c] = state.astype(BF16)
            state = state * decay_ref[c:c + 1, :] + upd_ref[c]
        state_ref[...] = state
        for c, rows in enumerate(chunks):
            o_ref[rows, :] = _dot(a_ref[c], v_ref[rows, :], NN) + _dot(qb_ref[rows, :], st_ref[c], NT)
        for rows in chunks:
            o = o_ref[rows, :]
            r = lax.rsqrt(jnp.mean(o * o, axis=-1, keepdims=True) + RMS_EPS)
            gz = gz_ref[rows, :]
            og_ref[rows, :] = ((o * r * ng_ref[...]) * (gz * _sigmoid(gz))).astype(BF16)

    col = lambda base: pl.BlockSpec((tm, HEAD_DIM), lambda h, j: (j, base + h))
    tile_bf = pltpu.VMEM((tm, HEAD_DIM), BF16)
    tile_f32 = pltpu.VMEM((tm, HEAD_DIM), F32)
    return _call(
        body, name="hgrn_fwd", grid=(HEADS, nt),
        in_specs=[col(8), col(16), col(24), col(32),
                  pl.BlockSpec((2, HEAD_DIM), lambda h, j: (0, h)), pl.BlockSpec((1, HEAD_DIM), lambda h, j: (0, h)),
                  pl.BlockSpec((_tri_rows(tm), _tri_rows(tm)), lambda h, j: (0, 0))],
        out_specs=[col(0), col(0), pl.BlockSpec((None, cpb, HEAD_DIM, HEAD_DIM), lambda h, j: (h, j, 0, 0))],
        out_shape=[jax.ShapeDtypeStruct((t, HGRN_DIM), F32), jax.ShapeDtypeStruct((t, HGRN_DIM), BF16),
                   jax.ShapeDtypeStruct((HEADS, t // CHUNK, HEAD_DIM, HEAD_DIM), BF16)],
        scratch_shapes=[pltpu.VMEM((HEAD_DIM, HEAD_DIM), F32), tile_bf, tile_bf, tile_bf, tile_bf, tile_bf,
                        pltpu.VMEM((cpb, HEAD_DIM, HEAD_DIM), F32), pltpu.VMEM((max(cpb, 8), HEAD_DIM), F32),
                        pltpu.VMEM((cpb, CHUNK, CHUNK), BF16), tile_f32, tile_f32, tile_f32],
        semantics=("parallel", "arbitrary"),
        operands=(proj, proj, proj, proj, logits, norm_g, _block_tri(_tri_rows(tm))), hosted=hosted)


def _hgrn_bwd(d_og, o, proj, states, logits, norm_g, d_proj, hosted=None):
    t = proj.shape[0]
    tm = CHUNK * CHUNKS_PER_BLOCK if t % (CHUNK * CHUNKS_PER_BLOCK) == 0 else CHUNK
    cpb = tm // CHUNK
    nt = t // tm
    half = CHUNK // 2

    def body(dog_ref, o_ref, qz_ref, fz_ref, iv_ref, gz_ref, st_ref, lg_ref, ng_ref, tril_ref, triu_ref,
             _, dp_ref, sums_ref,
             dstate_ref, qe_ref, ke_ref, qb_ref, kl_ref, v_ref, do_ref, upd_ref, dst_ref, a_ref, da_ref,
             decay_ref, through_ref, q_ref, kk_ref, b_ref, dsilu_ref, gs_ref, gf_ref, sn_ref,
             eb_ref, ebr_ref, ekr_ref, ebl_ref, rev_ref, pre_ref, dk_ref):
        j = pl.program_id(1)

        @pl.when(j == 0)
        def _():
            dstate_ref[...] = jnp.zeros_like(dstate_ref)
            sums_ref[...] = jnp.zeros_like(sums_ref)

        lb = _lower_bound(lg_ref)
        ng = ng_ref[...]
        chunks = [slice(c * CHUNK, (c + 1) * CHUNK) for c in range(cpb)]
        for rows in chunks:
            qz = qz_ref[rows, :]
            sq = _sigmoid(qz)
            q_ref[rows, :] = qz * sq
            dsilu_ref[rows, :] = sq * (1.0 + qz * (1.0 - sq))
            s, sn, f = _hgrn_gates(fz_ref[rows, :], lb)
            kk_ref[rows, :] = (1.0 - lb) * sn
            b_ref[rows, :] = jnp.log(f)
            sn_ref[rows, :] = sn
            gf_ref[rows, :] = sn / f
            gs_ref[rows, :] = (1.0 - lb) * s
            v_ref[rows, :] = iv_ref[rows, :].astype(BF16)
            ov = o_ref[rows, :]
            r = lax.rsqrt(jnp.mean(ov * ov, axis=-1, keepdims=True) + RMS_EPS)
            on = ov * r
            gz = gz_ref[rows, :]
            sg = _sigmoid(gz)
            dog = dog_ref[rows, :]
            dp_ref[rows, 3 * HEAD_DIM:4 * HEAD_DIM] =(dog * (on * ng) * (sg * (1.0 + gz * (1.0 - sg)))).astype(BF16)
            d_ong = dog * (gz * sg)
            sums_ref[0:1, :] += _colsum(d_ong * on)
            d_on = d_ong * ng
            do_ref[rows, :] = (r * (d_on - on * jnp.mean(d_on * on, axis=-1, keepdims=True))).astype(BF16)
        for rows in _groups(tm):
            b_ref[rows, :] = _tri_matmul(tril_ref, b_ref[rows, :])
        for c, rows in enumerate(chunks):
            b = b_ref[rows, :]
            bref = b[half - 1:half, :]
            blast = b[CHUNK - 1:CHUNK, :]
            q = q_ref[rows, :]
            kk = kk_ref[rows, :]
            eb = jnp.exp(b)
            ebr = jnp.exp(b - bref)
            ekr = jnp.exp(bref - b)
            ebl = jnp.exp(blast - b)
            eb_ref[rows, :] = eb
            ebr_ref[rows, :] = ebr
            ekr_ref[rows, :] = ekr
            ebl_ref[rows, :] = ebl
            qb_ref[rows, :] = (q * eb).astype(BF16)
            qe_ref[rows, :] = (q * ebr).astype(BF16)
            ke_ref[rows, :] = (kk * ekr).astype(BF16)
            kl_ref[rows, :] = (kk * ebl).astype(BF16)
            decay_ref[c:c + 1, :] = jnp.exp(blast)
        causal = _tri(True)
        for c, rows in enumerate(chunks):
            upd_ref[c] = _dot(do_ref[rows, :], qb_ref[rows, :], TN)
            a_ref[c] = jnp.where(causal, _dot(qe_ref[rows, :], ke_ref[rows, :], NT), 0.0).astype(BF16)
            da_ref[c] = jnp.where(causal, _dot(do_ref[rows, :], v_ref[rows, :], NT), 0.0).astype(BF16)
        dstate = dstate_ref[...]
        for c in reversed(range(cpb)):
            dst_ref[c] = dstate.astype(BF16)
            decay = decay_ref[c:c + 1, :]
            through_ref[c:c + 1, :] = decay * _colsum(dstate * st_ref[c].astype(F32))
            dstate = dstate * decay + upd_ref[c]
        dstate_ref[...] = dstate
        for c, rows in enumerate(chunks):
            dp_ref[rows, 2 * HEAD_DIM:3 * HEAD_DIM] =(_dot(a_ref[c], do_ref[rows, :], TN)
                                + _dot(kl_ref[rows, :], dst_ref[c], NT)).astype(BF16)
        for c, rows in enumerate(chunks):
            dqe = _dot(da_ref[c], ke_ref[rows, :], NN)
            dq_inter = _dot(do_ref[rows, :], st_ref[c], NN) * eb_ref[rows, :]
            dp_ref[rows, 0:HEAD_DIM] =((dqe * ebr_ref[rows, :] + dq_inter) * dsilu_ref[rows, :]).astype(BF16)
            rev_ref[rows, :] = qe_ref[rows, :].astype(F32) * dqe + q_ref[rows, :] * dq_inter
        for c, rows in enumerate(chunks):
            dke = _dot(da_ref[c], qe_ref[rows, :], TN)
            dk_inter = _dot(v_ref[rows, :], dst_ref[c], NN) * ebl_ref[rows, :]
            dk_ref[rows, :] = dke * ekr_ref[rows, :] + dk_inter
            rev_ref[rows, :] -= ke_ref[rows, :].astype(F32) * dke
            pre_ref[rows, :] = kk_ref[rows, :] * dk_inter
        for rows in _groups(tm):
            pre = pre_ref[rows, :]
            rev_ref[rows, :] = _tri_matmul(triu_ref, rev_ref[rows, :]) + (_tri_matmul(tril_ref, pre) - pre)
        for c, rows in enumerate(chunks):
            dlf = rev_ref[rows, :] + through_ref[c:c + 1, :]
            common = gf_ref[rows, :] * dlf - sn_ref[rows, :] * dk_ref[rows, :]
            dp_ref[rows, HEAD_DIM:2 * HEAD_DIM] =(gs_ref[rows, :] * common).astype(BF16)
            sums_ref[1:2, :] += _colsum(common)

        @pl.when(j == nt - 1)
        def _():
            sums_ref[1:2, :] = sums_ref[1:2, :] * lb * (1.0 - lb)

    rev = lambda base: pl.BlockSpec((tm, HEAD_DIM), lambda h, j: (nt - 1 - j, base + h))
    vec = lambda n: pl.BlockSpec((n, HEAD_DIM), lambda h, j: (0, h))
    const = pl.BlockSpec((_tri_rows(tm), _tri_rows(tm)), lambda h, j: (0, 0))
    tile_bf = pltpu.VMEM((tm, HEAD_DIM), BF16)
    tile_f32 = pltpu.VMEM((tm, HEAD_DIM), F32)
    square = lambda dtype: pltpu.VMEM((cpb, HEAD_DIM, HEAD_DIM), dtype)
    rows8 = pltpu.VMEM((max(cpb, 8), HEAD_DIM), F32)
    operands = (d_og, o, proj, proj, proj, proj, states, logits, norm_g, _block_tri(_tri_rows(tm)),
                _block_tri(_tri_rows(tm), lower=False), d_proj)
    return _call(
        body, name="hgrn_bwd", grid=(HEADS, nt),
        in_specs=[rev(0), rev(0), rev(8), rev(16), rev(24), rev(32),
                  pl.BlockSpec((None, cpb, HEAD_DIM, HEAD_DIM), lambda h, j: (h, nt - 1 - j, 0, 0)),
                  vec(2), vec(1), const, const, ANY],
        out_specs=[pl.BlockSpec((tm, 4 * HEAD_DIM), lambda h, j: (nt - 1 - j, DP_HEAD_BLOCK + h)), vec(8)],
        out_shape=[jax.ShapeDtypeStruct(d_proj.shape, BF16), jax.ShapeDtypeStruct((8, HGRN_DIM), F32)],
        scratch_shapes=[pltpu.VMEM((HEAD_DIM, HEAD_DIM), F32)] + [tile_bf] * 6 + [square(F32), square(BF16)]
        + [pltpu.VMEM((cpb, CHUNK, CHUNK), BF16)] * 2 + [rows8, rows8] + [tile_f32] * 14,
        semantics=("parallel", "arbitrary"), operands=operands, hosted=hosted, aliases={len(operands) - 1: 0})


def _mix_ln1(proj, y_conv, y_hgrn, w_out, x, g, b, hosted=None):
    t = x.shape[0]
    tm = _pick(t, 512)
    d = D_MODEL

    def lhs(ins, outs):
        for rows in _row_blocks(tm):
            outs[0][rows, :] = (_sigmoid(ins[0][rows, :]) * ins[3][rows, :]
                                + _sigmoid(ins[2][rows, :]) * ins[4][rows, :]).astype(BF16)
        return outs[0][...]

    def epilogue(acc, ins, outs, scr):
        for rows in _row_blocks(tm):
            r = ALPHA * ins[5][rows, :] + acc[rows, :]
            outs[1][rows, :] = r
            xhat, _ = _ln(r)
            outs[2][rows, :] = (xhat * ins[6][...] + ins[7][...]).astype(BF16)

    tile = pl.BlockSpec((tm, d), lambda i, j, k: (i, 0))
    vec = pl.BlockSpec((1, d), lambda i, j, k: (0, 0))
    return _mm_fused(
        "mix_ln1", (t // tm, 1, 1), (proj, w_out, proj, y_conv, y_hgrn, x, g, b),
        [pl.BlockSpec((tm, d), lambda i, j, k: (i, 5)), pl.BlockSpec((d, d), lambda i, j, k: (0, 0)),
         pl.BlockSpec((tm, d), lambda i, j, k: (i, 6)), tile, tile, tile, vec, vec],
        [jax.ShapeDtypeStruct((t, d), BF16), jax.ShapeDtypeStruct((t, d), F32), jax.ShapeDtypeStruct((t, d), BF16)],
        [tile, tile, tile], NN, (tm, d), epilogue, lhs=lhs, hosted=hosted)


def _d_mixed_merge_bwd(d_r1b, w_out, proj, y_conv, y_hgrn, hosted=None):
    t = proj.shape[0]
    tm = _pick(t, 512)
    d = D_MODEL

    def epilogue(d_mixed, ins, outs, scr):
        dy_ref, dmz_ref = outs
        for rows in _row_blocks(tm):
            dm = d_mixed[rows, :]
            for br in range(2):
                sg = _sigmoid(ins[2 + br][rows, :])
                dy_ref[br, rows, :] = (sg * dm).astype(BF16)
                dmz_ref[rows, br * d:(br + 1) * d] = (dm * ins[4 + br][rows, :] * sg * (1.0 - sg)).astype(BF16)

    tile = pl.BlockSpec((tm, d), lambda i, j, k: (i, 0))
    return _mm_fused(
        "d_mixed_merge_bwd", (t // tm, 1, 1), (d_r1b, w_out, proj, proj, y_conv, y_hgrn),
        [tile, pl.BlockSpec((d, d), lambda i, j, k: (0, 0)), pl.BlockSpec((tm, d), lambda i, j, k: (i, 5)),
         pl.BlockSpec((tm, d), lambda i, j, k: (i, 6)), tile, tile],
        [jax.ShapeDtypeStruct((2, t, d), BF16), jax.ShapeDtypeStruct((t, IN_COLS), BF16)],
        [pl.BlockSpec((2, tm, d), lambda i, j, k: (0, i, 0)),
         pl.BlockSpec((tm, 2 * d), lambda i, j, k: (i, DP_MERGE_BLOCK))],
        NT, (tm, d), epilogue, hosted=hosted)


def _ffn_out_ln2(act, w_ffn_out, r1, target, g1, b1, g2, b2):
    t = r1.shape[0]
    tm = _pick(t, 1024)
    nt = t // tm
    d = D_MODEL

    def epilogue(y_ffn, ins, outs, scr):
        r1_ref, tg_ref, g1_ref, b1_ref, g2_ref, b2_ref = ins[2:]
        dr_ref, drb_ref, sums_ref = outs
        (sq_ref,) = scr
        i = pl.program_id(0)

        @pl.when(i == 0)
        def _():
            sums_ref[...] = jnp.zeros_like(sums_ref)
            sq_ref[...] = jnp.zeros_like(sq_ref)

        for rows in _row_blocks(tm):
            xh1, _ = _ln(r1_ref[rows, :])
            x1 = xh1 * g1_ref[...] + b1_ref[...]
            xh2, rstd2 = _ln(ALPHA * x1 + y_ffn[rows, :])
            diff = xh2 * g2_ref[...] + b2_ref[...] - tg_ref[rows, :]
            dy = diff * (1.0 / D_MODEL)
            dr = _ln_bwd(dy, xh2, rstd2, g2_ref[...])
            dr_ref[rows, :] = dr
            drb_ref[rows, :] = dr.astype(BF16)
            sums_ref[0:1, :] += _colsum(dy * xh2)
            sums_ref[1:2, :] += _colsum(dy)
            sq_ref[...] += _colsum(diff * diff)

        @pl.when(i == nt - 1)
        def _():
            total = jnp.sum(sq_ref[...], axis=-1, keepdims=True) * (0.5 / D_MODEL)
            sums_ref[2:3, :] = jnp.broadcast_to(total, (1, D_MODEL))

    tile = pl.BlockSpec((tm, d), lambda i, j, k: (i, 0))
    vec = pl.BlockSpec((1, d), lambda i, j, k: (0, 0))
    return _mm_fused(
        "ffn_out_ln2", (nt, 1, 4), (act, w_ffn_out, r1, target, g1, b1, g2, b2),
        [pl.BlockSpec((None, tm, FF_SHARD), lambda i, j, k: (k, i, 0)),
         pl.BlockSpec((None, FF_SHARD, d), lambda i, j, k: (k, 0, 0)), tile, tile, vec, vec, vec, vec],
        [jax.ShapeDtypeStruct((t, d), F32), jax.ShapeDtypeStruct((t, d), BF16), jax.ShapeDtypeStruct((8, d), F32)],
        [tile, tile, pl.BlockSpec((8, d), lambda i, j, k: (0, 0))], NN, (tm, d), epilogue,
        scratch=[pltpu.VMEM((1, d), F32)])


def _d_x1_ln1_bwd(d_z, w_ffn_in, d_r2, r1, g1):
    t = r1.shape[0]
    tm = _pick(t, 1024)
    d = D_MODEL

    def epilogue(dx_ffn, ins, outs, scr):
        dr2_ref, r1_ref, g_ref = ins[2:]
        dr1_ref, dr1b_ref, sums_ref = outs
        i = pl.program_id(0)

        @pl.when(i == 0)
        def _():
            sums_ref[...] = jnp.zeros_like(sums_ref)

        for rows in _row_blocks(tm):
            xhat, rstd = _ln(r1_ref[rows, :])
            dx1 = ALPHA * dr2_ref[rows, :] + dx_ffn[rows, :]
            dr1 = _ln_bwd(dx1, xhat, rstd, g_ref[...])
            dr1_ref[rows, :] = dr1
            dr1b_ref[rows, :] = dr1.astype(BF16)
            sums_ref[0:1, :] += _colsum(dx1 * xhat)
            sums_ref[1:2, :] += _colsum(dx1)

    tile = pl.BlockSpec((tm, d), lambda i, j, k: (i, 0))
    return _mm_fused(
        "d_x1_ln1_bwd", (t // tm, 1, N_DEV), (d_z, w_ffn_in, d_r2, r1, g1),
        [pl.BlockSpec((None, tm, FF_SHARD), lambda i, j, k: (k, i, 0)),
         pl.BlockSpec((None, d, FF_SHARD), lambda i, j, k: (k, 0, 0)), tile, tile,
         pl.BlockSpec((1, d), lambda i, j, k: (0, 0))],
        [jax.ShapeDtypeStruct((t, d), F32), jax.ShapeDtypeStruct((t, d), BF16), jax.ShapeDtypeStruct((8, d), F32)],
        [tile, tile, pl.BlockSpec((8, d), lambda i, j, k: (0, 0))], NT, (tm, d), epilogue)


def _cast_bf16(x):
    t = x.shape[0]
    tm = _pick(t, 512)

    def body(x_ref, o_ref):
        o_ref[...] = x_ref[...].astype(BF16)

    tile = pl.BlockSpec((tm, D_MODEL), lambda i: (i, 0))
    return pl.pallas_call(
        body, name="cast_x", grid=(t // tm,), in_specs=[tile], out_specs=tile,
        out_shape=jax.ShapeDtypeStruct((t, D_MODEL), BF16), compiler_params=_params(("parallel",)),
    )(x)


def _relayout(name, a, in_block, in_map, out_block, out_map, out_shape):
    def body(a_ref, o_ref):
        o_ref[...] = a_ref[...].astype(o_ref.dtype)

    return pl.pallas_call(
        body, name=name, grid=(N_DEV,), in_specs=[pl.BlockSpec(in_block, in_map)],
        out_specs=pl.BlockSpec(out_block, out_map), out_shape=out_shape, compiler_params=_params(("parallel",)),
    )(a)


_GELU_C = math.sqrt(2.0 / math.pi)


_GELU_CUBIC = 0.044715


def _gelu_parts(u):
    u2 = u * u
    th = jnp.tanh(u * (_GELU_C + (_GELU_C * _GELU_CUBIC) * u2))
    hu = 0.5 * u
    return th, hu + hu * th, u2, hu


BF16_ROWS = 16


def _ffn_act_fwd(z, w_dw, b_dw):
    t = z.shape[2]
    tm = _pick(t, 1024)
    nh = tm // FFN_HALO

    def body(z_ref, zh_ref, w_ref, b_ref, act_ref, gd_ref, us_ref):
        i = pl.program_id(1)
        us_ref[0:FFN_HALO, :] = jnp.where(i == 0, 0.0, zh_ref[BF16_ROWS - FFN_HALO:BF16_ROWS, :].astype(F32))
        us_ref[FFN_HALO:FFN_HALO + tm, :] = z_ref[0].astype(F32)
        for r in range(tm // ROW_BLOCK):
            base = r * ROW_BLOCK
            rows = slice(base, base + ROW_BLOCK)
            for lanes in _lane_blocks(FF_SHARD):
                uc = b_ref[:, lanes]
                for k in range(FFN_K):
                    off = base + FFN_HALO - (FFN_K - 1) + k
                    uc = uc + w_ref[k:k + 1, lanes] * us_ref[off:off + ROW_BLOCK, lanes]
                th, gelu, u2, hu = _gelu_parts(uc)
                dgelu = (0.5 + 0.5 * th) + (hu - hu * th * th) * (_GELU_C + (3.0 * _GELU_C * _GELU_CUBIC) * u2)
                act_ref[rows, lanes] = (gelu * z_ref[1, rows, lanes].astype(F32)).astype(BF16)
                gd_ref[0, rows, lanes] = gelu.astype(BF16)
                gd_ref[1, rows, lanes] = dgelu.astype(BF16)

    return pl.pallas_call(
        body, name="ffn_act_fwd", grid=(4, t // tm),
        in_specs=[pl.BlockSpec((2, None, tm, FF_SHARD), lambda j, i: (0, j, i, 0)),
                  pl.BlockSpec((None, None, BF16_ROWS, FF_SHARD),
                               lambda j, i: (0, j, jnp.maximum(i * (tm // BF16_ROWS) - 1, 0), 0)),
                  pl.BlockSpec((None, FFN_K, FF_SHARD), lambda j, i: (j, 0, 0)),
                  pl.BlockSpec((None, 1, FF_SHARD), lambda j, i: (j, 0, 0))],
        out_specs=[pl.BlockSpec((None, tm, FF_SHARD), lambda j, i: (j, i, 0)),
                   pl.BlockSpec((2, None, tm, FF_SHARD), lambda j, i: (0, j, i, 0))],
        out_shape=[jax.ShapeDtypeStruct((4, t, FF_SHARD), BF16), jax.ShapeDtypeStruct((2, 4, t, FF_SHARD), BF16)],
        scratch_shapes=[pltpu.VMEM((FFN_HALO + tm, FF_SHARD), F32)],
        compiler_params=_params(("parallel", "arbitrary")),
    )(z, z, w_dw, b_dw)


def _ffn_act_bwd(d_act, z, gd, w_dw):
    t = z.shape[2]
    tm = _pick(t, 1024)
    nt = t // tm
    nh = tm // FFN_HALO
    last_h = t // FFN_HALO - 1
    pad = FFN_HALO - (FFN_K - 1)

    def fold(x):
        return functools.reduce(jnp.add, [x[r:r + SUBLANES, :] for r in range(0, x.shape[0], SUBLANES)])

    def body(da_ref, dah_ref, z_ref, zp_ref, gn_ref, gd_ref, gdn_ref, w_ref, dz_ref, sums_ref, us_ref, ds_ref,
             part_ref):
        i = pl.program_id(1)

        @pl.when(i == 0)
        def _():
            part_ref[...] = jnp.zeros_like(part_ref)

        us_ref[0:FFN_HALO, :] = jnp.where(i == 0, 0.0, zp_ref[BF16_ROWS - FFN_HALO:BF16_ROWS, :].astype(F32))
        us_ref[FFN_HALO:FFN_HALO + tm, :] = z_ref[0].astype(F32)
        for r in range(tm // ROW_BLOCK):
            base = r * ROW_BLOCK
            rows = slice(base, base + ROW_BLOCK)
            for lanes in _lane_blocks(FF_SHARD):
                da = da_ref[rows, lanes].astype(F32)
                dz_ref[1, rows, lanes] = (da * gd_ref[0, rows, lanes].astype(F32)).astype(BF16)
                duc = da * z_ref[1, rows, lanes].astype(F32) * gd_ref[1, rows, lanes].astype(F32)
                ds_ref[rows, lanes] = duc
                for k in range(FFN_K):
                    part_ref[k, :, lanes] += fold(duc * us_ref[base + pad + k:base + pad + k + ROW_BLOCK, lanes])
                part_ref[FFN_K, :, lanes] += fold(duc)
        duc_next = (dah_ref[0:FFN_HALO, :].astype(F32) * gn_ref[0:FFN_HALO, :].astype(F32)
                    * gdn_ref[0:FFN_HALO, :].astype(F32))
        ds_ref[tm:tm + FFN_HALO, :] = jnp.where(i == nt - 1, 0.0, duc_next)
        for r in range(tm // ROW_BLOCK):
            base = r * ROW_BLOCK
            for lanes in _lane_blocks(FF_SHARD):
                du = None
                for k in range(FFN_K):
                    off = base + FFN_K - 1 - k
                    term = w_ref[k:k + 1, lanes] * ds_ref[off:off + ROW_BLOCK, lanes]
                    du = term if du is None else du + term
                dz_ref[0, base:base + ROW_BLOCK, lanes] = du.astype(BF16)

        @pl.when(i == nt - 1)
        def _():
            sums_ref[...] = jnp.zeros_like(sums_ref)
            for k in range(FFN_K + 1):
                sums_ref[k:k + 1, :] = _colsum(part_ref[k])

    nxt = lambda i: jnp.minimum((i + 1) * nh, last_h)
    nxt_bf = lambda i: jnp.minimum((i + 1) * (tm // BF16_ROWS), t // BF16_ROWS - 1)
    return pl.pallas_call(
        body, name="ffn_act_bwd", grid=(4, nt),
        in_specs=[pl.BlockSpec((None, tm, FF_SHARD), lambda j, i: (j, i, 0)),
                  pl.BlockSpec((None, BF16_ROWS, FF_SHARD), lambda j, i: (j, nxt_bf(i), 0)),
                  pl.BlockSpec((2, None, tm, FF_SHARD), lambda j, i: (0, j, i, 0)),
                  pl.BlockSpec((None, None, BF16_ROWS, FF_SHARD),
                               lambda j, i: (0, j, jnp.maximum(i * (tm // BF16_ROWS) - 1, 0), 0)),
                  pl.BlockSpec((None, None, BF16_ROWS, FF_SHARD), lambda j, i: (1, j, nxt_bf(i), 0)),
                  pl.BlockSpec((2, None, tm, FF_SHARD), lambda j, i: (0, j, i, 0)),
                  pl.BlockSpec((None, None, BF16_ROWS, FF_SHARD), lambda j, i: (1, j, nxt_bf(i), 0)),
                  pl.BlockSpec((None, FFN_K, FF_SHARD), lambda j, i: (j, 0, 0))],
        out_specs=[pl.BlockSpec((2, None, tm, FF_SHARD), lambda j, i: (0, j, i, 0)),
                   pl.BlockSpec((None, 8, FF_SHARD), lambda j, i: (j, 0, 0))],
        out_shape=[jax.ShapeDtypeStruct((2, 4, t, FF_SHARD), BF16), jax.ShapeDtypeStruct((4, 8, FF_SHARD), F32)],
        scratch_shapes=[pltpu.VMEM((FFN_HALO + tm, FF_SHARD), F32), pltpu.VMEM((tm + FFN_HALO, FF_SHARD), F32),
                        pltpu.VMEM((FFN_K + 1, SUBLANES, FF_SHARD), F32)],
        compiler_params=_params(("parallel", "arbitrary")),
    )(d_act, d_act, z, z, z, gd, gd, w_dw)


_HGRN_COLS = 4 * HGRN_DIM


def _to_backward_order(w):
    heads = w[:, 2 * CONV_DIM:2 * CONV_DIM + _HGRN_COLS].reshape(-1, 4, HEADS, HEAD_DIM)
    heads = jnp.swapaxes(heads, 1, 2).reshape(-1, _HGRN_COLS)
    return jnp.concatenate([w[:, 2 * CONV_DIM + _HGRN_COLS:], w[:, :2 * CONV_DIM], heads], axis=1)


def _natural_block(p):
    merge, conv = 2 * D_MODEL // HEAD_DIM, 2 * CONV_DIM // HEAD_DIM
    if p < merge:
        return conv + _HGRN_COLS // HEAD_DIM + p
    if p < merge + conv:
        return p - merge
    head, part = divmod(p - merge - conv, 4)
    return conv + part * HEADS + head


def _g_w_in(xb, d_proj):
    t, d = xb.shape
    tk = _pick(t, 2048)
    nk = t // tk
    wide = IN_COLS // 4
    per_block = wide // HEAD_DIM
    per_shard = IN_SHARD // HEAD_DIM

    def body(a_ref, b_ref, out_ref, acc_ref, stage_ref, sem):
        i, k = pl.program_id(0), pl.program_id(2)

        def drain():
            pltpu.make_async_copy(stage_ref, stage_ref, sem).wait()

        @pl.when(k == 0)
        def _():
            acc_ref[...] = jnp.zeros_like(acc_ref)

        acc_ref[...] += _dot(a_ref[...], b_ref[...], TN)

        @pl.when(k == nk - 1)
        def _():
            @pl.when(i > 0)
            def _():
                drain()

            stage_ref[...] = acc_ref[...].astype(BF16)
            for block in range(IN_COLS // wide):
                @pl.when(i == block)
                def _(block=block):
                    for j in range(per_block):
                        shard, off = divmod(_natural_block(per_block * block + j), per_shard)
                        pltpu.make_async_copy(
                            stage_ref.at[:, j * HEAD_DIM:(j + 1) * HEAD_DIM],
                            out_ref.at[shard, :, off * HEAD_DIM:(off + 1) * HEAD_DIM], sem).start()

            @pl.when(i == IN_COLS // wide - 1)
            def _():
                drain()

    return pl.pallas_call(
        body, name="g_w_in", grid=(IN_COLS // wide, 1, nk),
        in_specs=[pl.BlockSpec((tk, d), lambda i, j, k: (k, 0)), pl.BlockSpec((tk, wide), lambda i, j, k: (k, i))],
        out_specs=ANY, out_shape=jax.ShapeDtypeStruct((N_DEV, d, IN_SHARD), BF16),
        scratch_shapes=[pltpu.VMEM((d, wide), F32), pltpu.VMEM((d, wide), BF16), pltpu.SemaphoreType.DMA],
        compiler_params=_params(("arbitrary", "arbitrary", "arbitrary")),
    )(xb, d_proj)


def _local_step(x, target, weights, small, scatter=None, order=None):
    t = x.shape[0]
    tm = _pick(t, 2048)
    tk = _pick(t, 2048)
    nm = t // tm
    nk = t // tk
    d = D_MODEL

    xb = _cast_bf16(x)
    ffn_in_gather = ffn_out_gather = None
    if isinstance(weights, tuple) and isinstance(weights[0], _Hosted):
        first_gather, ffn_in_gather, ffn_out_gather = weights
        proj, w_in_bwd, gathered = _proj_gather(xb, first_gather, order)
        w_in, w_conv_out8, w_hgrn_out8, w_out8, conv_dw8, ffn_dw8 = gathered
    else:
        w_in, w_conv_out8, w_hgrn_out8, w_out8, w_ffn_in, w_ffn_out8, conv_dw8, ffn_dw8 = weights
        proj = _mm("proj", xb, w_in, (t, IN_COLS), F32, (nm, N_DEV, 1),
                   pl.BlockSpec((tm, d), lambda i, j, k: (i, 0)),
                   pl.BlockSpec((None, d, IN_SHARD), lambda i, j, k: (j, 0, 0)),
                   pl.BlockSpec((tm, IN_SHARD), lambda i, j, k: (i, j)), NN, (tm, IN_SHARD))
        w_in_bwd = _to_backward_order(jnp.transpose(w_in, (1, 0, 2)).reshape(d, IN_COLS))
    o, og, states, *late = _hgrn_fwd(proj, small["hgrn_lb_logits"], small["hgrn_norm_g"], hosted=ffn_in_gather)
    if ffn_in_gather is not None:
        (w_ffn_in,) = late
    w_conv_out = _relayout("w_conv_out_natural", w_conv_out8, (None, CONV_DIM, 128), lambda j: (j, 0, 0),
                           (CONV_DIM, 128), lambda j: (0, j), jax.ShapeDtypeStruct((CONV_DIM, d), BF16))
    w_hgrn_out = w_hgrn_out8.reshape(d, d)
    w_out = w_out8.reshape(d, d)
    conv_dw =jnp.transpose(conv_dw8[:, :CONV_K, :CONV_DIM // N_DEV], (1, 0, 2)).reshape(CONV_K, CONV_DIM)
    ffn_dw = jnp.transpose(ffn_dw8[:, :FFN_K, :D_FF // N_DEV], (1, 0, 2)).reshape(FFN_K, 4, FF_SHARD)
    small = dict(small, w_conv_dw=conv_dw, w_ffn_dw=jnp.transpose(ffn_dw, (1, 0, 2)),
                 b_ffn_dw=small["b_ffn_dw"].reshape(4, 1, FF_SHARD))

    c_act, conv_pre = _conv_fwd(proj, small["w_conv_dw"], small["b_conv_dw"], small["conv_ln_g"], small["conv_ln_b"])
    y_conv = _mm("y_conv", c_act, w_conv_out, (t, d), F32, (nm, 1, 1),
                 pl.BlockSpec((tm, CONV_DIM), lambda i, j, k: (i, 0)),
                 pl.BlockSpec((CONV_DIM, d), lambda i, j, k: (0, 0)),
                 pl.BlockSpec((tm, d), lambda i, j, k: (i, 0)), NN, (tm, d))
    sq_w = pl.BlockSpec((d, d), lambda i, j, k: (0, 0))
    row_tile = pl.BlockSpec((tm, d), lambda i, j, k: (i, 0))
    y_hgrn = _mm("y_hgrn", og, w_hgrn_out, (t, d), F32, (nm, 1, 1), row_tile, sq_w, row_tile, NN, (tm, d))
    mixed, r1, x1b, *late = _mix_ln1(proj, y_conv, y_hgrn, w_out, x, small["ln1_g"], small["ln1_b"],
                                     hosted=ffn_out_gather)
    if ffn_out_gather is not None:
        (w_ffn_out8,) = late
    w_ffn_out = w_ffn_out8.reshape(4, FF_SHARD, d)
    z = _mm("ffn_in", x1b, w_ffn_in, (N_DEV, t, FF_SHARD), BF16, (nm, N_DEV, 1), row_tile,
            pl.BlockSpec((None, d, FF_SHARD), lambda i, j, k: (j, 0, 0)),
            pl.BlockSpec((None, tm, FF_SHARD), lambda i, j, k: (j, i, 0)), NN, (tm, FF_SHARD))
    z = z.reshape(2, 4, t, FF_SHARD)
    act, gelu_and_slope = _ffn_act_fwd(z, small["w_ffn_dw"], small["b_ffn_dw"])

    d_r2, d_r2b, sums_ln2 = _ffn_out_ln2(act, w_ffn_out, r1, target, small["ln1_g"], small["ln1_b"],
                                         small["ln2_g"], small["ln2_b"])
    d_act = _mm("d_act", d_r2b, w_ffn_out, (4, t, FF_SHARD), BF16, (nm, 4, 1), row_tile,
                pl.BlockSpec((None, FF_SHARD, d), lambda i, j, k: (j, 0, 0)),
                pl.BlockSpec((None, tm, FF_SHARD), lambda i, j, k: (j, i, 0)), NT, (tm, FF_SHARD))
    g_w_ffn_out = _mm("g_w_ffn_out", act, d_r2b, (4, FF_SHARD, d), BF16, (4, 1, nk),
                      pl.BlockSpec((None, tk, FF_SHARD), lambda i, j, k: (i, k, 0)),
                      pl.BlockSpec((tk, d), lambda i, j, k: (k, 0)),
                      pl.BlockSpec((None, FF_SHARD, d), lambda i, j, k: (i, 0, 0)), TN, (FF_SHARD, d))
    d_z, sums_ffn = _ffn_act_bwd(d_act, z, gelu_and_slope, small["w_ffn_dw"])
    d_z8 = d_z.reshape(N_DEV, t, FF_SHARD)
    d_r1, d_r1b, sums_ln1 = _d_x1_ln1_bwd(d_z8, w_ffn_in, d_r2, r1, small["ln1_g"])
    g_w_ffn_in = _mm("g_w_ffn_in", x1b, d_z8, (N_DEV, d, FF_SHARD), BF16, (N_DEV, 1, nk),
                     pl.BlockSpec((tk, d), lambda i, j, k: (k, 0)),
                     pl.BlockSpec((None, tk, FF_SHARD), lambda i, j, k: (i, k, 0)),
                     pl.BlockSpec((None, d, FF_SHARD), lambda i, j, k: (i, 0, 0)), TN, (d, FF_SHARD))
    k_tile = pl.BlockSpec((tk, d), lambda i, j, k: (k, 0))
    g_w_out = _mm("g_w_out", mixed, d_r1b, (d, d), BF16, (1, 1, nk), k_tile, k_tile, sq_w, TN, (d, d))
    send = (lambda grads: None) if scatter is None else scatter
    g_w_ffn_out = g_w_ffn_out.reshape(N_DEV, D_FF // N_DEV, d)
    d_y, d_proj, *recv_ffn_out = _d_mixed_merge_bwd(d_r1b, w_out, proj, y_conv, y_hgrn, hosted=send([g_w_ffn_out]))
    d_pre, sums_conv = _d_c_norm_bwd(d_y, w_conv_out, conv_pre, small["conv_ln_g"], small["conv_ln_b"])
    g_w_conv_out = _mm("g_w_conv_out", c_act, d_y, (CONV_DIM, d), BF16, (1, 1, nk),
                       pl.BlockSpec((tk, CONV_DIM), lambda i, j, k: (k, 0)),
                       pl.BlockSpec((None, tk, d), lambda i, j, k: (0, k, 0)),
                       pl.BlockSpec((CONV_DIM, d), lambda i, j, k: (0, 0)), TN, (CONV_DIM, d))
    g_w_conv_out = _relayout("g_w_conv_out_shards", g_w_conv_out, (CONV_DIM, 128), lambda j: (0, j),
                             (None, CONV_DIM, 128), lambda j: (j, 0, 0),
                             jax.ShapeDtypeStruct((N_DEV, CONV_DIM, 128), BF16))
    d_og = _mm("d_og", d_y, w_hgrn_out, (t, d), F32, (nm, 1, 1),
               pl.BlockSpec((None, tm, d), lambda i, j, k: (1, i, 0)), sq_w, row_tile, NT, (tm, d))
    g_w_hgrn_out = _mm("g_w_hgrn_out", og, d_y, (d, d), BF16, (1, 1, nk), k_tile,
                       pl.BlockSpec((None, tk, d), lambda i, j, k: (1, k, 0)), sq_w, TN, (d, d))
    d_proj, g_w_conv_dw, *recv_ffn_in = _conv_bwd_dw(d_pre, proj, small["w_conv_dw"], d_proj,
                                                     hosted=send([g_w_ffn_in]))
    early = [g_w_conv_out, g_w_hgrn_out.reshape(N_DEV, d // N_DEV, d), g_w_out.reshape(N_DEV, d // N_DEV, d)]
    d_proj, sums_hgrn, *early_recv = _hgrn_bwd(d_og, o, proj, states, small["hgrn_lb_logits"], small["hgrn_norm_g"],
                                               d_proj, hosted=send(early))
    early += [g_w_ffn_in, g_w_ffn_out]
    early_recv += recv_ffn_in + recv_ffn_out
    wide = IN_COLS // 4
    g_w_in = _g_w_in(xb, d_proj)
    def add_residual(acc, ins, outs, scr):
        for rows in _row_blocks(ta):
            outs[0][rows, :] = ALPHA * ins[2][rows, :] + acc[rows, :]

    ta = _pick(t, 1024)
    acc_tile = pl.BlockSpec((ta, d), lambda i, j, k: (i, 0))
    grad_x, *late_recv = _mm_fused(
        "grad_x", (t // ta, 1, 4), (d_proj, w_in_bwd, d_r1),
        [pl.BlockSpec((ta, wide), lambda i, j, k: (i, k)), pl.BlockSpec((d, wide), lambda i, j, k: (0, k)), acc_tile],
        [jax.ShapeDtypeStruct((t, d), F32)], [acc_tile], NT, (ta, d), add_residual,
        hosted=None if scatter is None else scatter([g_w_in]))

    large_grads = [g_w_in] + early
    if scatter is not None:
        large_grads = list(zip(large_grads, late_recv + early_recv))
    return grad_x, large_grads, (sums_ln2, sums_conv, sums_hgrn, sums_ln1, sums_ffn, g_w_conv_dw)


def _small_views(sums):
    sums_ln2, sums_conv, sums_hgrn, sums_ln1, sums_ffn, g_w_conv_dw = sums
    d_l0 = sums_hgrn[1:2]
    return {
        "loss": sums_ln2[2:3, 0:128],
        "b_conv_dw": sums_conv[2:3], "conv_ln_g": sums_conv[0:1], "conv_ln_b": sums_conv[1:2],
        "hgrn_lb_logits": jnp.concatenate([d_l0, -d_l0], axis=1),
        "hgrn_norm_g": sums_hgrn[0:1],
        "ln1_g": sums_ln1[0:1], "ln1_b": sums_ln1[1:2],
        "b_ffn_dw": sums_ffn[:, FFN_K, :].reshape(1, D_FF),
        "ln2_g": sums_ln2[0:1], "ln2_b": sums_ln2[1:2],
        "w_conv_dw": g_w_conv_dw[0:CONV_K].reshape(1, CONV_K * CONV_DIM),
        "w_ffn_dw": jnp.transpose(sums_ffn[:, 0:FFN_K, :], (1, 0, 2)).reshape(1, FFN_K * D_FF),
    }


def _coords():
    return lax.axis_index("x"), lax.axis_index("y"), lax.axis_index("c")


def _gather(shards, staged=False):
    n = len(shards)
    later = range(1 if staged else 0, n)

    def parts(ins, outs, sems):
        send_sems, recv_sems, local_sems = sems
        x, y, c = _coords()
        me = 4 * x + 2 * y + c
        sibling = (x, y, 1 - c)
        chips = [(1 - x, y), (x, 1 - y), (1 - x, 1 - y)]

        def copy(a, k, block, to, src=None):
            return pltpu.make_async_remote_copy(
                src_ref=outs[a].at[block] if src is None else src, dst_ref=outs[a].at[block],
                send_sem=send_sems.at[a, k], recv_sem=recv_sems.at[a, k], device_id=to, device_id_type=MESH)

        local = [pltpu.make_async_copy(ins[a], outs[a].at[me], local_sems.at[a]) for a in range(n)]
        first = []
        for a in range(n):
            first.append(copy(a, 0, me, sibling, src=ins[a]))
            for j, chip in enumerate(chips):
                first.append(copy(a, 1 + j, me, (*chip, c), src=ins[a]))
        return x, y, c, sibling, chips, copy, local, first

    def start(ins, outs, sems):
        *_, local, first = parts(ins, outs, sems)
        for cp in local + first:
            cp.start()

    def arrive(ins, outs, sems, s):
        x, y, c, sibling, chips, copy, _, _ = parts(ins, outs, sems)
        if s == 1:
            block = 4 * x + 2 * y + 1 - c
            copy(0, 0, block, sibling).wait_recv()
        elif s <= 4:
            px, py = chips[s - 2]
            block = 4 * px + 2 * py + c
            copy(0, s - 1, block, sibling).wait_recv()
            copy(0, s + 2, block, sibling).start()
        else:
            px, py = chips[s - 5]
            block = 4 * px + 2 * py + 1 - c
            copy(0, s - 1, block, sibling).wait_recv()
        return block

    def middle(ins, outs, sems):
        x, y, c, sibling, chips, copy, _, _ = parts(ins, outs, sems)
        for j, (px, py) in enumerate(chips):
            for a in later:
                copy(a, 1 + j, 4 * px + 2 * py + c, sibling).wait_recv()
                copy(a, 4 + j, 4 * px + 2 * py + c, sibling).start()

    def finish(ins, outs, sems):
        x, y, c, sibling, chips, copy, local, first = parts(ins, outs, sems)
        passed = [copy(a, 4 + j, 4 * px + 2 * py + c, sibling) for a in range(n) for j, (px, py) in enumerate(chips)]
        for a in later:
            copy(a, 0, 4 * x + 2 * y + 1 - c, sibling).wait_recv()
            for j, (px, py) in enumerate(chips):
                copy(a, 4 + j, 4 * px + 2 * py + 1 - c, sibling).wait_recv()
        for cp in first + passed:
            cp.wait_send()
        for cp in local:
            cp.wait()

    hosted = _Hosted(shards, [jax.ShapeDtypeStruct((N_DEV,) + s.shape, s.dtype) for s in shards],
                     [pltpu.SemaphoreType.DMA((n, 7)), pltpu.SemaphoreType.DMA((n, 7)), pltpu.SemaphoreType.DMA((n,))],
                     start, finish, middle)
    hosted.arrive = arrive
    return hosted


def _proj_gather(xb, gather, order):
    t, d = xb.shape
    tm = _pick(t, 2048)
    nm = t // tm
    n_in, n_out = len(gather.inputs), len(gather.out_shapes)

    blocks_per_shard = IN_SHARD // HEAD_DIM

    def body(order_ref, x_ref, *refs):
        ins, refs = refs[:n_in], refs[n_in:]
        o_ref, bwd_ref, outs, refs = refs[0], refs[1], refs[2:2 + n_out], refs[2 + n_out:]
        w_buf, w_sem, bwd_sem, sems = refs[0], refs[1], refs[2], refs[3:]
        s, i = pl.program_id(0), pl.program_id(1)

        def reorder_copies(step):
            copies = []
            for j in range(blocks_per_shard):
                n = blocks_per_shard * order_ref[step] + j
                head_part = n - 2 * CONV_DIM // HEAD_DIM
                p = jnp.where(n >= (2 * CONV_DIM + _HGRN_COLS) // HEAD_DIM, n - (2 * CONV_DIM + _HGRN_COLS) // HEAD_DIM,
                              jnp.where(n < 2 * CONV_DIM // HEAD_DIM, 2 * D_MODEL // HEAD_DIM + n,
                                        (2 * D_MODEL + 2 * CONV_DIM) // HEAD_DIM + 4 * (head_part % HEADS)
                                        + head_part // HEADS))
                copies.append(pltpu.make_async_copy(
                    w_buf.at[step % 2, :, j * HEAD_DIM:(j + 1) * HEAD_DIM],
                    bwd_ref.at[:, pl.ds(pl.multiple_of(p * HEAD_DIM, HEAD_DIM), HEAD_DIM)], bwd_sem.at[step % 2]))
            return copies

        def reorder_done(step):
            pltpu.make_async_copy(w_buf.at[step % 2], w_buf.at[step % 2], bwd_sem.at[step % 2]).wait()

        @pl.when((s == 0) & (i == 0))
        def _():
            gather.start(ins, outs, sems)

        def staging(step, src):
            return pltpu.make_async_copy(src, w_buf.at[step % 2], w_sem.at[step % 2])

        @pl.when((s == 0) & (i == 0))
        def _():
            staging(0, ins[0]).start()

        for step in range(1, N_DEV):
            @pl.when((s == step - 1) & (i == nm - 1))
            def _(step=step):
                if step >= 2:
                    reorder_done(step - 2)
                staging(step, outs[0].at[gather.arrive(ins, outs, sems, step)]).start()

        for step in range(N_DEV):
            @pl.when((s == step) & (i == 0))
            def _(step=step):
                staging(step, ins[0]).wait()
                for cp in reorder_copies(step):
                    cp.start()

        o_ref[...] = _dot(x_ref[...], w_buf[s % 2], NN)

        @pl.when((s == N_DEV - 2) & (i == 0))
        def _():
            gather.middle(ins, outs, sems)

        @pl.when((s == N_DEV - 1) & (i == nm - 1))
        def _():
            reorder_done(N_DEV - 2)
            reorder_done(N_DEV - 1)
            gather.finish(ins, outs, sems)

    outs = pl.pallas_call(
        body, name="proj_gather",
        grid_spec=pltpu.PrefetchScalarGridSpec(
            num_scalar_prefetch=1, grid=(N_DEV, nm),
            in_specs=[pl.BlockSpec((tm, d), lambda s, i, order_ref: (i, 0))] + [ANY] * n_in,
            out_specs=[pl.BlockSpec((tm, IN_SHARD), lambda s, i, order_ref: (i, order_ref[s])), ANY] + [ANY] * n_out,
            scratch_shapes=[pltpu.VMEM((2, d, IN_SHARD), BF16), pltpu.SemaphoreType.DMA((2,)),
                            pltpu.SemaphoreType.DMA((2,))] + gather.sem_shapes),
        out_shape=[jax.ShapeDtypeStruct((t, IN_COLS), F32), jax.ShapeDtypeStruct((d, IN_COLS), BF16)]
        + gather.out_shapes,
        compiler_params=pltpu.CompilerParams(dimension_semantics=("arbitrary", "arbitrary"),
                                             vmem_limit_bytes=VMEM_LIMIT, has_side_effects=True),
    )(order, xb, *gather.inputs)
    return outs[0], outs[1], list(outs[2:])


def _scatter(grads):
    n = len(grads)

    def copies(ins, outs, sems):
        send_sems, recv_sems = sems
        x, y, c = _coords()
        out = []
        for a in range(n):
            for k in range(1, N_DEV):
                px, py, pc = x ^ (k >> 2), y ^ ((k >> 1) & 1), c ^ (k & 1)
                out.append(pltpu.make_async_remote_copy(
                    src_ref=ins[a].at[4 * px + 2 * py + pc], dst_ref=outs[a].at[k - 1],
                    send_sem=send_sems.at[a, k - 1], recv_sem=recv_sems.at[a, k - 1],
                    device_id=(px, py, pc), device_id_type=MESH))
        return out

    def start(ins, outs, sems):
        for cp in copies(ins, outs, sems):
            cp.start()

    def finish(ins, outs, sems):
        for cp in copies(ins, outs, sems):
            cp.wait()

    return _Hosted(grads, [jax.ShapeDtypeStruct((N_DEV - 1,) + g.shape[1:], g.dtype) for g in grads],
                   [pltpu.SemaphoreType.DMA((n, N_DEV - 1)), pltpu.SemaphoreType.DMA((n, N_DEV - 1))], start, finish)


def _row_tile(rows):
    return 256 if rows % 256 == 0 else rows


def _adam_math(w, g, m, v):
    m_new = ADAM_B1 * m + (1.0 - ADAM_B1) * g
    v_new = ADAM_B2 * v + (1.0 - ADAM_B2) * (g * g)
    m_hat = m_new / (1.0 - ADAM_B1 ** ADAM_STEP)
    v_hat = v_new / (1.0 - ADAM_B2 ** ADAM_STEP)
    delta = -ADAM_LR * (m_hat / (jnp.sqrt(v_hat) + ADAM_EPS) + ADAM_WD * w)
    return delta, m_new, v_new


def _adam_large(name, own, recv, me, w, m, v):
    rows, cols = w.shape
    tr = _row_tile(rows)

    def body(me_ref, p_ref, r_ref, w_ref, m_ref, v_ref, g_out, d_out, m_out, v_out):
        g = p_ref[...].astype(F32)
        for k in range(N_DEV - 1):
            g = g + r_ref[k].astype(F32)
        delta, m_new, v_new = _adam_math(w_ref[...], g, m_ref[...], v_ref[...])
        g_out[...] = g
        d_out[...] = delta
        m_out[...] = m_new
        v_out[...] = v_new

    tile = pl.BlockSpec((tr, cols), lambda r, me_ref: (r, 0))
    sds = jax.ShapeDtypeStruct((rows, cols), F32)
    return pl.pallas_call(
        body, name=name,
        grid_spec=pltpu.PrefetchScalarGridSpec(
            num_scalar_prefetch=1, grid=(rows // tr,),
            in_specs=[pl.BlockSpec((None, tr, cols), lambda r, me_ref: (me_ref[0], r, 0)),
                      pl.BlockSpec((N_DEV - 1, tr, cols), lambda r, me_ref: (0, r, 0)), tile, tile, tile],
            out_specs=[tile, tile, tile, tile]),
        out_shape=[sds, sds, sds, sds],
        compiler_params=_params(("parallel",)),
    )(me, own, recv, w, m, v)


def _small_allreduce(arrays):
    n = len(arrays)

    def body(*refs):
        ins, outs, gats = refs[:n], refs[n:2 * n], refs[2 * n:3 * n]
        send_sems, recv_sems = refs[3 * n:]
        x, y, c = _coords()
        me = 4 * x + 2 * y + c
        peers = [(x ^ (k >> 2), y ^ ((k >> 1) & 1), c ^ (k & 1)) for k in range(1, N_DEV)]

        def copy(a, k, slot):
            return pltpu.make_async_remote_copy(
                src_ref=ins[a], dst_ref=gats[a].at[slot], send_sem=send_sems.at[a, k], recv_sem=recv_sems.at[a, k],
                device_id=peers[k], device_id_type=MESH)

        sends = [copy(a, k, me) for a in range(n) for k in range(N_DEV - 1)]
        for a in range(n):
            gats[a][me] = ins[a][...]
        for cp in sends:
            cp.start()
        for a in range(n):
            for k, (px, py, pc) in enumerate(peers):
                copy(a, k, 4 * px + 2 * py + pc).wait_recv()
        for cp in sends:
            cp.wait_send()
        for a in range(n):
            acc = gats[a][0]
            for dev in range(1, N_DEV):
                acc = acc + gats[a][dev]
            outs[a][...] = acc

    whole = pl.BlockSpec(memory_space=pltpu.VMEM)
    return pl.pallas_call(
        body, name="small_allreduce", in_specs=[whole] * n, out_specs=[whole] * n,
        out_shape=[jax.ShapeDtypeStruct(a.shape, F32) for a in arrays],
        scratch_shapes=[pltpu.VMEM((N_DEV,) + a.shape, F32) for a in arrays]
        + [pltpu.SemaphoreType.DMA((n, N_DEV - 1)), pltpu.SemaphoreType.DMA((n, N_DEV - 1))],
        compiler_params=pltpu.CompilerParams(has_side_effects=True, vmem_limit_bytes=VMEM_LIMIT),
    )(*arrays)


def _adam_replicated(sums, w, m, v):
    rows_of = {"conv_ln_g": (1, 0), "conv_ln_b": (1, 1), "b_conv_dw": (1, 2), "hgrn_norm_g": (2, 0),
               "ln1_g": (3, 0), "ln1_b": (3, 1), "ln2_g": (0, 0), "ln2_b": (0, 1)}
    names = list(rows_of) + ["hgrn_lb_logits"]
    n = len(names)

    def body(*refs):
        sum_refs, refs = refs[:4], refs[4:]
        w_refs, m_refs, v_refs, outs = refs[:n], refs[n:2 * n], refs[2 * n:3 * n], refs[3 * n:]
        for j, name in enumerate(names):
            if name == "hgrn_lb_logits":
                d_l0 = sum_refs[2][1:2, :]
                grads = [d_l0, -d_l0]
            else:
                a, row = rows_of[name]
                grads = [sum_refs[a][row:row + 1, :]]
            g_out, d_out, m_out, v_out = outs[4 * j:4 * j + 4]
            for r, g in enumerate(grads):
                rows = slice(r, r + 1)
                delta, m_new, v_new = _adam_math(w_refs[j][rows, :], g, m_refs[j][rows, :], v_refs[j][rows, :])
                g_out[rows, :] = g
                d_out[rows, :] = delta
                m_out[rows, :] = m_new
                v_out[rows, :] = v_new

    whole = pl.BlockSpec(memory_space=pltpu.VMEM)
    operands = list(sums) + [w[k] for k in names] + [m[k] for k in names] + [v[k] for k in names]
    outs = pl.pallas_call(
        body, name="adam_replicated", in_specs=[whole] * len(operands), out_specs=[whole] * (4 * n),
        out_shape=[jax.ShapeDtypeStruct(w[k].shape, F32) for k in names for _ in range(4)],
    )(*operands)
    return {name: tuple(outs[4 * j:4 * j + 4]) for j, name in enumerate(names)}


def _adam_small(w, g, m, v):
    def body(w_ref, g_ref, m_ref, v_ref, d_out, m_out, v_out):
        delta, m_new, v_new = _adam_math(w_ref[...], g_ref[...], m_ref[...], v_ref[...])
        d_out[...] = delta
        m_out[...] = m_new
        v_out[...] = v_new

    whole = pl.BlockSpec(memory_space=pltpu.VMEM)
    sds = jax.ShapeDtypeStruct(w.shape, F32)
    return pl.pallas_call(body, name="adam_small", in_specs=[whole] * 4, out_specs=[whole] * 3,
                          out_shape=[sds, sds, sds])(w, g, m, v)


_WEIGHTS = ["w_in", "w_conv_dw", "b_conv_dw", "conv_ln_g", "conv_ln_b", "w_conv_out", "hgrn_lb_logits", "hgrn_norm_g",
            "w_hgrn_out", "w_out", "ln1_g", "ln1_b", "w_ffn_in", "w_ffn_dw", "b_ffn_dw", "w_ffn_out", "ln2_g", "ln2_b"]
_LARGE = ["w_in", "w_conv_out", "w_hgrn_out", "w_out", "w_ffn_in", "w_ffn_out"]
_CONV_DW_SHARD = CONV_DIM // N_DEV
_FFN_DW_SHARD = D_FF // N_DEV


def kernel(x, w_in, w_conv_dw, b_conv_dw, conv_ln_g, conv_ln_b, w_conv_out, hgrn_lb_logits, hgrn_norm_g, w_hgrn_out, w_out, ln1_g, ln1_b, w_ffn_in, w_ffn_dw, b_ffn_dw, w_ffn_out, ln2_g, ln2_b, loss_target, m_w_in, m_w_conv_dw, m_b_conv_dw, m_conv_ln_g, m_conv_ln_b, m_w_conv_out, m_hgrn_lb_logits, m_hgrn_norm_g, m_w_hgrn_out, m_w_out, m_ln1_g, m_ln1_b, m_w_ffn_in, m_w_ffn_dw, m_b_ffn_dw, m_w_ffn_out, m_ln2_g, m_ln2_b, v_w_in, v_w_conv_dw, v_b_conv_dw, v_conv_ln_g, v_conv_ln_b, v_w_conv_out, v_hgrn_lb_logits, v_hgrn_norm_g, v_w_hgrn_out, v_w_out, v_ln1_g, v_ln1_b, v_w_ffn_in, v_w_ffn_dw, v_b_ffn_dw, v_w_ffn_out, v_ln2_g, v_ln2_b):
    w = dict(w_in=w_in, w_conv_dw=w_conv_dw, b_conv_dw=b_conv_dw, conv_ln_g=conv_ln_g, conv_ln_b=conv_ln_b,
             w_conv_out=w_conv_out, hgrn_lb_logits=hgrn_lb_logits, hgrn_norm_g=hgrn_norm_g, w_hgrn_out=w_hgrn_out,
             w_out=w_out, ln1_g=ln1_g, ln1_b=ln1_b, w_ffn_in=w_ffn_in, w_ffn_dw=w_ffn_dw, b_ffn_dw=b_ffn_dw,
             w_ffn_out=w_ffn_out, ln2_g=ln2_g, ln2_b=ln2_b)
    m = dict(w_in=m_w_in, w_conv_dw=m_w_conv_dw, b_conv_dw=m_b_conv_dw, conv_ln_g=m_conv_ln_g, conv_ln_b=m_conv_ln_b,
             w_conv_out=m_w_conv_out, hgrn_lb_logits=m_hgrn_lb_logits, hgrn_norm_g=m_hgrn_norm_g,
             w_hgrn_out=m_w_hgrn_out, w_out=m_w_out, ln1_g=m_ln1_g, ln1_b=m_ln1_b, w_ffn_in=m_w_ffn_in,
             w_ffn_dw=m_w_ffn_dw, b_ffn_dw=m_b_ffn_dw, w_ffn_out=m_w_ffn_out, ln2_g=m_ln2_g, ln2_b=m_ln2_b)
    v = dict(w_in=v_w_in, w_conv_dw=v_w_conv_dw, b_conv_dw=v_b_conv_dw, conv_ln_g=v_conv_ln_g, conv_ln_b=v_conv_ln_b,
             w_conv_out=v_w_conv_out, hgrn_lb_logits=v_hgrn_lb_logits, hgrn_norm_g=v_hgrn_norm_g,
             w_hgrn_out=v_w_hgrn_out, w_out=v_w_out, ln1_g=v_ln1_g, ln1_b=v_ln1_b, w_ffn_in=v_w_ffn_in,
             w_ffn_dw=v_w_ffn_dw, b_ffn_dw=v_b_ffn_dw, w_ffn_out=v_w_ffn_out, ln2_g=v_ln2_g, ln2_b=v_ln2_b)
    xi, yi, ci = lax.axis_index("x"), lax.axis_index("y"), lax.axis_index("c")
    me = 4 * xi + 2 * yi + ci
    me_op = jnp.reshape(me, (1,)).astype(jnp.int32)

    shards = [w[name][0].astype(BF16) for name in _LARGE]
    shards.append(jnp.pad(w_conv_dw[0], ((0, 1), (0, 128 - _CONV_DW_SHARD))))
    shards.append(jnp.pad(w_ffn_dw[0], ((0, 8 - FFN_K), (0, 384 - _FFN_DW_SHARD))))
    chips = [(1 - xi, yi), (xi, 1 - yi), (1 - xi, 1 - yi)]
    order = jnp.stack([me, me ^ 1] + [4 * px + 2 * py + ci for px, py in chips]
                      + [4 * px + 2 * py + 1 - ci for px, py in chips]).astype(jnp.int32)
    small = dict(b_conv_dw=b_conv_dw, conv_ln_g=conv_ln_g, conv_ln_b=conv_ln_b, hgrn_lb_logits=hgrn_lb_logits,
                 hgrn_norm_g=hgrn_norm_g, ln1_g=ln1_g, ln1_b=ln1_b, ln2_g=ln2_g, ln2_b=ln2_b, b_ffn_dw=b_ffn_dw)

    gathers = (_gather(shards[:4] + shards[6:], staged=True), _gather(shards[4:5]), _gather(shards[5:6]))
    grad_x, large_grads, small_sums = _local_step(x[0], loss_target[0], gathers, small, _scatter, order)

    out = {}
    for name, (own, recv) in zip(_LARGE, large_grads):
        out[name] = _adam_large("adam_" + name, own, recv, me_op, w[name][0], m[name][0], v[name][0])

    totals = _small_allreduce(list(small_sums))
    out.update(_adam_replicated(totals[:4], w, m, v))
    summed = _small_views(totals)
    loss = summed["loss"][0, 0]
    conv_dw_g = lax.dynamic_slice_in_dim(summed["w_conv_dw"].reshape(CONV_K, CONV_DIM), me * _CONV_DW_SHARD, _CONV_DW_SHARD, axis=1)
    ffn_dw_g = lax.dynamic_slice_in_dim(summed["w_ffn_dw"].reshape(FFN_K, D_FF), me * _FFN_DW_SHARD, _FFN_DW_SHARD, axis=1)
    small_g = dict(b_ffn_dw=summed["b_ffn_dw"], w_conv_dw=conv_dw_g.reshape(1, -1), w_ffn_dw=ffn_dw_g.reshape(1, -1))
    names = list(small_g)
    flat = lambda d, n: d[n].reshape(1, -1)
    n_small = sum(small_g[n].shape[1] for n in names)
    pad = (-n_small) % 1024
    pack = lambda pieces: jnp.pad(jnp.concatenate(pieces, axis=1), ((0, 0), (0, pad))).reshape(-1, 128)
    d_s, m_s, v_s = _adam_small(pack([flat(w, n) for n in names]), pack([small_g[n] for n in names]),
                                pack([flat(m, n) for n in names]), pack([flat(v, n) for n in names]))
    pos = 0
    for n in names:
        size = small_g[n].shape[1]
        cut = lambda a: a.reshape(1, -1)[:, pos:pos + size].reshape(w[n].shape)
        out[n] = (small_g[n].reshape(w[n].shape), cut(d_s), cut(m_s), cut(v_s))
        pos += size

    for name in _LARGE:
        out[name] = tuple(a.reshape(w[name].shape) for a in out[name])
    grads = [out[n][0] for n in _WEIGHTS]
    deltas = [out[n][1] for n in _WEIGHTS]
    new_m = [out[n][2] for n in _WEIGHTS]
    new_v = [out[n][3] for n in _WEIGHTS]
    return (loss, grad_x[None], *grads, *deltas, *new_m, *new_v)
```
